```python
import math
import jax, jax.numpy as jnp
from jax import lax
import numpy as np

D_MODEL = 2048
BATCH = 16
SEQ = 2048
DEPTH = 4

N_MIXERS = 4
NORM_EPS = 1e-6
ROPE_THETA = 10000.0
NEG_INF = -1e30

CC_CH = D_MODEL
CC_WIDTH = 31

DSA_HEAD_DIM = 128
DSA_HEADS = 8
DSA_CONFIGS = ((128, 1), (512, 4), (2048, 16))
DSA_GROUPS = len(DSA_CONFIGS)
DSA_BLOCK = 128

MLA_HEADS = 16
MLA_Q_RANK = 512
MLA_KV_RANK = 512
MLA_NOPE = 128
MLA_ROPE = 64
MLA_V = 128
ATTN_BLOCK = 128

SG_CHUNK = 128
SG_HALF = 3 * D_MODEL
SG_GROUPS = 8

FFN_DIM = 5632
FFN_WIDTH = 3

kernel_name = 'hybrid_interleaved_conv_dilated_mla_sgu_convffn'


def rmsnorm(x, g):
    x32 = x.astype(jnp.float32)
    y = x32 * lax.rsqrt(jnp.mean(x32 * x32, axis=-1, keepdims=True) + NORM_EPS)
    return (y * g.astype(jnp.float32)).astype(x.dtype)


def layernorm(x, g, b):
    x32 = x.astype(jnp.float32)
    mu = jnp.mean(x32, axis=-1, keepdims=True)
    xc = x32 - mu
    y = xc * lax.rsqrt(jnp.mean(xc * xc, axis=-1, keepdims=True) + NORM_EPS)
    return (y * g.astype(jnp.float32) + b.astype(jnp.float32)).astype(x.dtype)


def rope_tables(seq, dim):
    pos = jnp.arange(seq, dtype=jnp.float32)
    inv = ROPE_THETA ** (-(jnp.arange(0, dim, 2, dtype=jnp.float32) / dim))
    ang = pos[:, None] * inv[None, :]
    return jnp.cos(ang)[:, None, :], jnp.sin(ang)[:, None, :]


def apply_rope(x, cos, sin):
    x1, x2 = jnp.split(x, 2, axis=-1)
    c = cos.astype(x.dtype)
    s = sin.astype(x.dtype)
    return jnp.concatenate([x1 * c - x2 * s, x2 * c + x1 * s], axis=-1)


def causal_dwconv(x, w, b):
    k = w.shape[0]
    y = lax.conv_general_dilated(
        x, w[:, None, :].astype(x.dtype), window_strides=(1,), padding=((k - 1, 0),),
        dimension_numbers=('NWC', 'WIO', 'NWC'), feature_group_count=x.shape[-1])
    return y + b


def conformer_conv_module(h, w_in, b_in, dw_w, dw_b, ln_g, ln_b, w_out, b_out):
    z = h @ w_in + b_in
    a, g = jnp.split(z, 2, axis=-1)
    z = a * jax.nn.sigmoid(g)
    z = causal_dwconv(z, dw_w, dw_b)
    z = jax.nn.silu(layernorm(z, ln_g, ln_b))
    return z @ w_out + b_out


def dilated_group_attention(q, k, v, window, dil):
    B, S, H, Dh = q.shape
    span = window // dil
    L = S // dil
    nb = -(-L // DSA_BLOCK)
    Lp = nb * DSA_BLOCK

    def to_sub(t):
        t = t.reshape(B, L, dil, H, Dh).transpose(0, 2, 3, 1, 4)
        t = jnp.pad(t, ((0, 0), (0, 0), (0, 0), (0, Lp - L), (0, 0)))
        return t.reshape(B, dil, H, nb, DSA_BLOCK, Dh)

    def with_prev(t):
        prev = jnp.pad(t[:, :, :, :-1], ((0, 0), (0, 0), (0, 0), (1, 0), (0, 0), (0, 0)))
        return jnp.concatenate([prev, t], axis=4)

    qb = to_sub(q)
    kw = with_prev(to_sub(k))
    vw = with_prev(to_sub(v))
    s = jnp.einsum('bdhnqc,bdhnkc->bdhnqk', qb, kw).astype(jnp.float32) * (Dh ** -0.5)
    qi = jnp.arange(DSA_BLOCK)[:, None]
    kj = jnp.arange(2 * DSA_BLOCK)[None, :]
    dist = DSA_BLOCK + qi - kj
    band = (dist >= 0) & (dist <= span)
    blk = jnp.arange(nb)[:, None, None]
    mask = band[None] & ((blk > 0) | (kj[None] >= DSA_BLOCK))
    s = jnp.where(mask, s, NEG_INF)
    m = jnp.max(s, axis=-1, keepdims=True)
    p = jnp.exp(s - m)
    l = jnp.sum(p, axis=-1, keepdims=True)
    o = jnp.einsum('bdhnqk,bdhnkc->bdhnqc', p.astype(v.dtype), vw).astype(jnp.float32) / l
    lse = (m + jnp.log(l))[..., 0]
    o = o.reshape(B, dil, H, Lp, Dh)[:, :, :, :L].transpose(0, 3, 1, 2, 4).reshape(B, S, H, Dh)
    lse = lse.reshape(B, dil, H, Lp)[:, :, :, :L].transpose(0, 3, 1, 2).reshape(B, S, H)
    return o, lse


def dilated_attention_mixer(h, w_qkv, w_o):
    B, S, _ = h.shape
    qkv = (h @ w_qkv).reshape(B, S, DSA_GROUPS, 3, DSA_HEADS, DSA_HEAD_DIM)
    cos, sin = rope_tables(S, DSA_HEAD_DIM)
    outs, lses = [], []
    for g, (window, dil) in enumerate(DSA_CONFIGS):
        q = apply_rope(qkv[:, :, g, 0], cos, sin)
        k = apply_rope(qkv[:, :, g, 1], cos, sin)
        o, lse = dilated_group_attention(q, k, qkv[:, :, g, 2], window, dil)
        outs.append(o)
        lses.append(lse)
    alpha = jax.nn.softmax(jnp.stack(lses, axis=0), axis=0)
    o = jnp.einsum('gbsh,gbshc->bshc', alpha, jnp.stack(outs, axis=0))
    return o.reshape(B, S, DSA_HEADS * DSA_HEAD_DIM).astype(h.dtype) @ w_o


def blocked_causal_attention(q, k, v, scale):
    B, S, H, Dq = q.shape
    Dv = v.shape[-1]
    nb = S // ATTN_BLOCK
    qb = q.reshape(B, nb, ATTN_BLOCK, H, Dq).transpose(1, 0, 2, 3, 4)
    kpos = jnp.arange(S)

    def one_block(args):
        qi, i = args
        s = jnp.einsum('bqhc,bkhc->bhqk', qi, k).astype(jnp.float32) * scale
        qpos = i * ATTN_BLOCK + jnp.arange(ATTN_BLOCK)
        s = jnp.where((kpos[None, :] <= qpos[:, None])[None, None], s, NEG_INF)
        p = jax.nn.softmax(s, axis=-1)
        return jnp.einsum('bhqk,bkhc->bqhc', p.astype(v.dtype), v)

    out = lax.map(one_block, (qb, jnp.arange(nb)))
    return out.transpose(1, 0, 2, 3, 4).reshape(B, S, H, Dv)


def mla_mixer(h, w_in, q_norm, w_qb, kv_norm, w_kvb, w_o):
    B, S, _ = h.shape
    c = h @ w_in
    cq = c[..., :MLA_Q_RANK]
    ckv = c[..., MLA_Q_RANK:MLA_Q_RANK + MLA_KV_RANK]
    k_pe = c[..., MLA_Q_RANK + MLA_KV_RANK:]
    cos, sin = rope_tables(S, MLA_ROPE)
    q = (rmsnorm(cq, q_norm) @ w_qb).reshape(B, S, MLA_HEADS, MLA_NOPE + MLA_ROPE)
    q = jnp.concatenate([q[..., :MLA_NOPE], apply_rope(q[..., MLA_NOPE:], cos, sin)], axis=-1)
    kv = (rmsnorm(ckv, kv_norm) @ w_kvb).reshape(B, S, MLA_HEADS, MLA_NOPE + MLA_V)
    k_pe = apply_rope(k_pe[:, :, None, :], cos, sin)
    k = jnp.concatenate([kv[..., :MLA_NOPE], jnp.broadcast_to(k_pe, (B, S, MLA_HEADS, MLA_ROPE))], axis=-1)
    v = kv[..., MLA_NOPE:]
    o = blocked_causal_attention(q, k, v, (MLA_NOPE + MLA_ROPE) ** -0.5)
    return o.reshape(B, S, MLA_HEADS * MLA_V) @ w_o


def chunked_sgu_mixer(h, w_in, b_in, ln_g, ln_b, w_s, b_s, w_out, b_out):
    B, S, _ = h.shape
    nc = S // SG_CHUNK
    z = jax.nn.gelu(h @ w_in + b_in, approximate=False)
    u, v = jnp.split(z, 2, axis=-1)
    v = layernorm(v, ln_g, ln_b).reshape(B, nc, SG_CHUNK, SG_GROUPS, SG_HALF // SG_GROUPS)
    ws = w_s * jnp.tril(jnp.ones((SG_CHUNK, SG_CHUNK), w_s.dtype))[None]
    v = jnp.einsum('gts,bnsgc->bntgc', ws, v) + b_s.T[None, None, :, :, None]
    return (u * v.reshape(B, S, SG_HALF)) @ w_out + b_out


def conv_ffn(h, w_up, dw_w, dw_b, w_down):
    z = causal_dwconv(h @ w_up, dw_w, dw_b)
    g, a = jnp.split(z, 2, axis=-1)
    return (jax.nn.silu(g) * a) @ w_down


def _fwd_setup_inputs(seed: int = 0) -> dict:
    key = jax.random.key(seed)
    counter = [0]

    def nk():
        counter[0] += 1
        return jax.random.fold_in(key, counter[0])

    def w(shape, fan_in):
        return jax.random.normal(nk(), shape, jnp.float32) * (fan_in ** -0.5)

    def gain(n):
        return 1.0 + 0.05 * jax.random.normal(nk(), (n,), jnp.float32)

    def bias(shape):
        return 0.02 * jax.random.normal(nk(), shape, jnp.float32)

    def ffn(p, pre):
        p[pre + '_norm_ffn'] = gain(D_MODEL)
        p[pre + '_ffn_w_up'] = w((D_MODEL, 2 * FFN_DIM), D_MODEL)
        p[pre + '_ffn_dw_w'] = w((FFN_WIDTH, 2 * FFN_DIM), FFN_WIDTH)
        p[pre + '_ffn_dw_b'] = bias((2 * FFN_DIM,))
        p[pre + '_ffn_w_down'] = w((FFN_DIM, D_MODEL), FFN_DIM)

    p = {}
    p['x'] = jax.random.normal(nk(), (BATCH, SEQ, D_MODEL), jnp.float32)
    p['l0_norm_mix'] = gain(D_MODEL)
    p['l0_cc_w_in'] = w((D_MODEL, 2 * CC_CH), D_MODEL)
    p['l0_cc_b_in'] = bias((2 * CC_CH,))
    p['l0_cc_dw_w'] = w((CC_WIDTH, CC_CH), CC_WIDTH)
    p['l0_cc_dw_b'] = bias((CC_CH,))
    p['l0_cc_ln_g'] = gain(CC_CH)
    p['l0_cc_ln_b'] = bias((CC_CH,))
    p['l0_cc_w_out'] = w((CC_CH, D_MODEL), CC_CH)
    p['l0_cc_b_out'] = bias((D_MODEL,))
    ffn(p, 'l0')
    p['l1_norm_mix'] = gain(D_MODEL)
    p['l1_dsa_w_qkv'] = w((D_MODEL, DSA_GROUPS * 3 * DSA_HEADS * DSA_HEAD_DIM), D_MODEL)
    p['l1_dsa_w_o'] = w((DSA_HEADS * DSA_HEAD_DIM, D_MODEL), DSA_HEADS * DSA_HEAD_DIM)
    ffn(p, 'l1')
    p['l2_norm_mix'] = gain(D_MODEL)
    p['l2_mla_w_in'] = w((D_MODEL, MLA_Q_RANK + MLA_KV_RANK + MLA_ROPE), D_MODEL)
    p['l2_mla_q_norm'] = gain(MLA_Q_RANK)
    p['l2_mla_w_qb'] = w((MLA_Q_RANK, MLA_HEADS * (MLA_NOPE + MLA_ROPE)), MLA_Q_RANK)
    p['l2_mla_kv_norm'] = gain(MLA_KV_RANK)
    p['l2_mla_w_kvb'] = w((MLA_KV_RANK, MLA_HEADS * (MLA_NOPE + MLA_V)), MLA_KV_RANK)
    p['l2_mla_w_o'] = w((MLA_HEADS * MLA_V, D_MODEL), MLA_HEADS * MLA_V)
    ffn(p, 'l2')
    p['l3_norm_mix'] = gain(D_MODEL)
    p['l3_sg_w_in'] = w((D_MODEL, 2 * SG_HALF), D_MODEL)
    p['l3_sg_b_in'] = bias((2 * SG_HALF,))
    p['l3_sg_ln_g'] = gain(SG_HALF)
    p['l3_sg_ln_b'] = bias((SG_HALF,))
    p['l3_sg_w_s'] = w((SG_GROUPS, SG_CHUNK, SG_CHUNK), SG_CHUNK)
    p['l3_sg_b_s'] = 1.0 + 0.05 * jax.random.normal(nk(), (SG_GROUPS, SG_CHUNK), jnp.float32)
    p['l3_sg_w_out'] = w((SG_HALF, D_MODEL), SG_HALF)
    p['l3_sg_b_out'] = bias((D_MODEL,))
    ffn(p, 'l3')
    p['final_norm'] = gain(D_MODEL)
    return p


def _fwd_reference(x,
              l0_norm_mix, l0_cc_w_in, l0_cc_b_in, l0_cc_dw_w, l0_cc_dw_b, l0_cc_ln_g, l0_cc_ln_b,
              l0_cc_w_out, l0_cc_b_out,
              l0_norm_ffn, l0_ffn_w_up, l0_ffn_dw_w, l0_ffn_dw_b, l0_ffn_w_down,
              l1_norm_mix, l1_dsa_w_qkv, l1_dsa_w_o,
              l1_norm_ffn, l1_ffn_w_up, l1_ffn_dw_w, l1_ffn_dw_b, l1_ffn_w_down,
              l2_norm_mix, l2_mla_w_in, l2_mla_q_norm, l2_mla_w_qb, l2_mla_kv_norm, l2_mla_w_kvb,
              l2_mla_w_o,
              l2_norm_ffn, l2_ffn_w_up, l2_ffn_dw_w, l2_ffn_dw_b, l2_ffn_w_down,
              l3_norm_mix, l3_sg_w_in, l3_sg_b_in, l3_sg_ln_g, l3_sg_ln_b, l3_sg_w_s, l3_sg_b_s,
              l3_sg_w_out, l3_sg_b_out,
              l3_norm_ffn, l3_ffn_w_up, l3_ffn_dw_w, l3_ffn_dw_b, l3_ffn_w_down,
              final_norm):
    mixers = (
        lambda h: conformer_conv_module(h, l0_cc_w_in, l0_cc_b_in, l0_cc_dw_w, l0_cc_dw_b,
                                        l0_cc_ln_g, l0_cc_ln_b, l0_cc_w_out, l0_cc_b_out),
        lambda h: dilated_attention_mixer(h, l1_dsa_w_qkv, l1_dsa_w_o),
        lambda h: mla_mixer(h, l2_mla_w_in, l2_mla_q_norm, l2_mla_w_qb, l2_mla_kv_norm,
                            l2_mla_w_kvb, l2_mla_w_o),
        lambda h: chunked_sgu_mixer(h, l3_sg_w_in, l3_sg_b_in, l3_sg_ln_g, l3_sg_ln_b,
                                    l3_sg_w_s, l3_sg_b_s, l3_sg_w_out, l3_sg_b_out),
    )
    norms_mix = (l0_norm_mix, l1_norm_mix, l2_norm_mix, l3_norm_mix)
    norms_ffn = (l0_norm_ffn, l1_norm_ffn, l2_norm_ffn, l3_norm_ffn)
    ffns = (
        (l0_ffn_w_up, l0_ffn_dw_w, l0_ffn_dw_b, l0_ffn_w_down),
        (l1_ffn_w_up, l1_ffn_dw_w, l1_ffn_dw_b, l1_ffn_w_down),
        (l2_ffn_w_up, l2_ffn_dw_w, l2_ffn_dw_b, l2_ffn_w_down),
        (l3_ffn_w_up, l3_ffn_dw_w, l3_ffn_dw_b, l3_ffn_w_down),
    )
    h = x
    for i in range(DEPTH):
        h = h + mixers[i % N_MIXERS](rmsnorm(h, norms_mix[i]))
        h = h + conv_ffn(rmsnorm(h, norms_ffn[i]), *ffns[i])
    return rmsnorm(h, final_norm)


import jax as _jax
import jax.numpy as _jnp

TWIN_FORMAT = 'train_step'
FWD_PARAMS = ['x', 'l0_norm_mix', 'l0_cc_w_in', 'l0_cc_b_in', 'l0_cc_dw_w', 'l0_cc_dw_b', 'l0_cc_ln_g', 'l0_cc_ln_b', 'l0_cc_w_out', 'l0_cc_b_out', 'l0_norm_ffn', 'l0_ffn_w_up', 'l0_ffn_dw_w', 'l0_ffn_dw_b', 'l0_ffn_w_down', 'l1_norm_mix', 'l1_dsa_w_qkv', 'l1_dsa_w_o', 'l1_norm_ffn', 'l1_ffn_w_up', 'l1_ffn_dw_w', 'l1_ffn_dw_b', 'l1_ffn_w_down', 'l2_norm_mix', 'l2_mla_w_in', 'l2_mla_q_norm', 'l2_mla_w_qb', 'l2_mla_kv_norm', 'l2_mla_w_kvb', 'l2_mla_w_o', 'l2_norm_ffn', 'l2_ffn_w_up', 'l2_ffn_dw_w', 'l2_ffn_dw_b', 'l2_ffn_w_down', 'l3_norm_mix', 'l3_sg_w_in', 'l3_sg_b_in', 'l3_sg_ln_g', 'l3_sg_ln_b', 'l3_sg_w_s', 'l3_sg_b_s', 'l3_sg_w_out', 'l3_sg_b_out', 'l3_norm_ffn', 'l3_ffn_w_up', 'l3_ffn_dw_w', 'l3_ffn_dw_b', 'l3_ffn_w_down', 'final_norm']
TWIN_WEIGHTS = ['l0_norm_mix', 'l0_cc_w_in', 'l0_cc_b_in', 'l0_cc_dw_w', 'l0_cc_dw_b', 'l0_cc_ln_g', 'l0_cc_ln_b', 'l0_cc_w_out', 'l0_cc_b_out', 'l0_norm_ffn', 'l0_ffn_w_up', 'l0_ffn_dw_w', 'l0_ffn_dw_b', 'l0_ffn_w_down', 'l1_norm_mix', 'l1_dsa_w_qkv', 'l1_dsa_w_o', 'l1_norm_ffn', 'l1_ffn_w_up', 'l1_ffn_dw_w', 'l1_ffn_dw_b', 'l1_ffn_w_down', 'l2_norm_mix', 'l2_mla_w_in', 'l2_mla_q_norm', 'l2_mla_w_qb', 'l2_mla_kv_norm', 'l2_mla_w_kvb', 'l2_mla_w_o', 'l2_norm_ffn', 'l2_ffn_w_up', 'l2_ffn_dw_w', 'l2_ffn_dw_b', 'l2_ffn_w_down', 'l3_norm_mix', 'l3_sg_w_in', 'l3_sg_b_in', 'l3_sg_ln_g', 'l3_sg_ln_b', 'l3_sg_w_s', 'l3_sg_b_s', 'l3_sg_w_out', 'l3_sg_b_out', 'l3_norm_ffn', 'l3_ffn_w_up', 'l3_ffn_dw_w', 'l3_ffn_dw_b', 'l3_ffn_w_down', 'final_norm']
TWIN_DIFF_INPUT = 'x'
TWIN_INPUTS = ['x', 'l0_norm_mix', 'l0_cc_w_in', 'l0_cc_b_in', 'l0_cc_dw_w', 'l0_cc_dw_b', 'l0_cc_ln_g', 'l0_cc_ln_b', 'l0_cc_w_out', 'l0_cc_b_out', 'l0_norm_ffn', 'l0_ffn_w_up', 'l0_ffn_dw_w', 'l0_ffn_dw_b', 'l0_ffn_w_down', 'l1_norm_mix', 'l1_dsa_w_qkv', 'l1_dsa_w_o', 'l1_norm_ffn', 'l1_ffn_w_up', 'l1_ffn_dw_w', 'l1_ffn_dw_b', 'l1_ffn_w_down', 'l2_norm_mix', 'l2_mla_w_in', 'l2_mla_q_norm', 'l2_mla_w_qb', 'l2_mla_kv_norm', 'l2_mla_w_kvb', 'l2_mla_w_o', 'l2_norm_ffn', 'l2_ffn_w_up', 'l2_ffn_dw_w', 'l2_ffn_dw_b', 'l2_ffn_w_down', 'l3_norm_mix', 'l3_sg_w_in', 'l3_sg_b_in', 'l3_sg_ln_g', 'l3_sg_ln_b', 'l3_sg_w_s', 'l3_sg_b_s', 'l3_sg_w_out', 'l3_sg_b_out', 'l3_norm_ffn', 'l3_ffn_w_up', 'l3_ffn_dw_w', 'l3_ffn_dw_b', 'l3_ffn_w_down', 'final_norm', 'loss_target', 'm_l0_norm_mix', 'm_l0_cc_w_in', 'm_l0_cc_b_in', 'm_l0_cc_dw_w', 'm_l0_cc_dw_b', 'm_l0_cc_ln_g', 'm_l0_cc_ln_b', 'm_l0_cc_w_out', 'm_l0_cc_b_out', 'm_l0_norm_ffn', 'm_l0_ffn_w_up', 'm_l0_ffn_dw_w', 'm_l0_ffn_dw_b', 'm_l0_ffn_w_down', 'm_l1_norm_mix', 'm_l1_dsa_w_qkv', 'm_l1_dsa_w_o', 'm_l1_norm_ffn', 'm_l1_ffn_w_up', 'm_l1_ffn_dw_w', 'm_l1_ffn_dw_b', 'm_l1_ffn_w_down', 'm_l2_norm_mix', 'm_l2_mla_w_in', 'm_l2_mla_q_norm', 'm_l2_mla_w_qb', 'm_l2_mla_kv_norm', 'm_l2_mla_w_kvb', 'm_l2_mla_w_o', 'm_l2_norm_ffn', 'm_l2_ffn_w_up', 'm_l2_ffn_dw_w', 'm_l2_ffn_dw_b', 'm_l2_ffn_w_down', 'm_l3_norm_mix', 'm_l3_sg_w_in', 'm_l3_sg_b_in', 'm_l3_sg_ln_g', 'm_l3_sg_ln_b', 'm_l3_sg_w_s', 'm_l3_sg_b_s', 'm_l3_sg_w_out', 'm_l3_sg_b_out', 'm_l3_norm_ffn', 'm_l3_ffn_w_up', 'm_l3_ffn_dw_w', 'm_l3_ffn_dw_b', 'm_l3_ffn_w_down', 'm_final_norm', 'v_l0_norm_mix', 'v_l0_cc_w_in', 'v_l0_cc_b_in', 'v_l0_cc_dw_w', 'v_l0_cc_dw_b', 'v_l0_cc_ln_g', 'v_l0_cc_ln_b', 'v_l0_cc_w_out', 'v_l0_cc_b_out', 'v_l0_norm_ffn', 'v_l0_ffn_w_up', 'v_l0_ffn_dw_w', 'v_l0_ffn_dw_b', 'v_l0_ffn_w_down', 'v_l1_norm_mix', 'v_l1_dsa_w_qkv', 'v_l1_dsa_w_o', 'v_l1_norm_ffn', 'v_l1_ffn_w_up', 'v_l1_ffn_dw_w', 'v_l1_ffn_dw_b', 'v_l1_ffn_w_down', 'v_l2_norm_mix', 'v_l2_mla_w_in', 'v_l2_mla_q_norm', 'v_l2_mla_w_qb', 'v_l2_mla_kv_norm', 'v_l2_mla_w_kvb', 'v_l2_mla_w_o', 'v_l2_norm_ffn', 'v_l2_ffn_w_up', 'v_l2_ffn_dw_w', 'v_l2_ffn_dw_b', 'v_l2_ffn_w_down', 'v_l3_norm_mix', 'v_l3_sg_w_in', 'v_l3_sg_b_in', 'v_l3_sg_ln_g', 'v_l3_sg_ln_b', 'v_l3_sg_w_s', 'v_l3_sg_b_s', 'v_l3_sg_w_out', 'v_l3_sg_b_out', 'v_l3_norm_ffn', 'v_l3_ffn_w_up', 'v_l3_ffn_dw_w', 'v_l3_ffn_dw_b', 'v_l3_ffn_w_down', 'v_final_norm']
TWIN_OUTPUTS = ['loss', 'grad_x', 'grad_l0_norm_mix', 'grad_l0_cc_w_in', 'grad_l0_cc_b_in', 'grad_l0_cc_dw_w', 'grad_l0_cc_dw_b', 'grad_l0_cc_ln_g', 'grad_l0_cc_ln_b', 'grad_l0_cc_w_out', 'grad_l0_cc_b_out', 'grad_l0_norm_ffn', 'grad_l0_ffn_w_up', 'grad_l0_ffn_dw_w', 'grad_l0_ffn_dw_b', 'grad_l0_ffn_w_down', 'grad_l1_norm_mix', 'grad_l1_dsa_w_qkv', 'grad_l1_dsa_w_o', 'grad_l1_norm_ffn', 'grad_l1_ffn_w_up', 'grad_l1_ffn_dw_w', 'grad_l1_ffn_dw_b', 'grad_l1_ffn_w_down', 'grad_l2_norm_mix', 'grad_l2_mla_w_in', 'grad_l2_mla_q_norm', 'grad_l2_mla_w_qb', 'grad_l2_mla_kv_norm', 'grad_l2_mla_w_kvb', 'grad_l2_mla_w_o', 'grad_l2_norm_ffn', 'grad_l2_ffn_w_up', 'grad_l2_ffn_dw_w', 'grad_l2_ffn_dw_b', 'grad_l2_ffn_w_down', 'grad_l3_norm_mix', 'grad_l3_sg_w_in', 'grad_l3_sg_b_in', 'grad_l3_sg_ln_g', 'grad_l3_sg_ln_b', 'grad_l3_sg_w_s', 'grad_l3_sg_b_s', 'grad_l3_sg_w_out', 'grad_l3_sg_b_out', 'grad_l3_norm_ffn', 'grad_l3_ffn_w_up', 'grad_l3_ffn_dw_w', 'grad_l3_ffn_dw_b', 'grad_l3_ffn_w_down', 'grad_final_norm', 'delta_l0_norm_mix', 'delta_l0_cc_w_in', 'delta_l0_cc_b_in', 'delta_l0_cc_dw_w', 'delta_l0_cc_dw_b', 'delta_l0_cc_ln_g', 'delta_l0_cc_ln_b', 'delta_l0_cc_w_out', 'delta_l0_cc_b_out', 'delta_l0_norm_ffn', 'delta_l0_ffn_w_up', 'delta_l0_ffn_dw_w', 'delta_l0_ffn_dw_b', 'delta_l0_ffn_w_down', 'delta_l1_norm_mix', 'delta_l1_dsa_w_qkv', 'delta_l1_dsa_w_o', 'delta_l1_norm_ffn', 'delta_l1_ffn_w_up', 'delta_l1_ffn_dw_w', 'delta_l1_ffn_dw_b', 'delta_l1_ffn_w_down', 'delta_l2_norm_mix', 'delta_l2_mla_w_in', 'delta_l2_mla_q_norm', 'delta_l2_mla_w_qb', 'delta_l2_mla_kv_norm', 'delta_l2_mla_w_kvb', 'delta_l2_mla_w_o', 'delta_l2_norm_ffn', 'delta_l2_ffn_w_up', 'delta_l2_ffn_dw_w', 'delta_l2_ffn_dw_b', 'delta_l2_ffn_w_down', 'delta_l3_norm_mix', 'delta_l3_sg_w_in', 'delta_l3_sg_b_in', 'delta_l3_sg_ln_g', 'delta_l3_sg_ln_b', 'delta_l3_sg_w_s', 'delta_l3_sg_b_s', 'delta_l3_sg_w_out', 'delta_l3_sg_b_out', 'delta_l3_norm_ffn', 'delta_l3_ffn_w_up', 'delta_l3_ffn_dw_w', 'delta_l3_ffn_dw_b', 'delta_l3_ffn_w_down', 'delta_final_norm', 'new_m_l0_norm_mix', 'new_m_l0_cc_w_in', 'new_m_l0_cc_b_in', 'new_m_l0_cc_dw_w', 'new_m_l0_cc_dw_b', 'new_m_l0_cc_ln_g', 'new_m_l0_cc_ln_b', 'new_m_l0_cc_w_out', 'new_m_l0_cc_b_out', 'new_m_l0_norm_ffn', 'new_m_l0_ffn_w_up', 'new_m_l0_ffn_dw_w', 'new_m_l0_ffn_dw_b', 'new_m_l0_ffn_w_down', 'new_m_l1_norm_mix', 'new_m_l1_dsa_w_qkv', 'new_m_l1_dsa_w_o', 'new_m_l1_norm_ffn', 'new_m_l1_ffn_w_up', 'new_m_l1_ffn_dw_w', 'new_m_l1_ffn_dw_b', 'new_m_l1_ffn_w_down', 'new_m_l2_norm_mix', 'new_m_l2_mla_w_in', 'new_m_l2_mla_q_norm', 'new_m_l2_mla_w_qb', 'new_m_l2_mla_kv_norm', 'new_m_l2_mla_w_kvb', 'new_m_l2_mla_w_o', 'new_m_l2_norm_ffn', 'new_m_l2_ffn_w_up', 'new_m_l2_ffn_dw_w', 'new_m_l2_ffn_dw_b', 'new_m_l2_ffn_w_down', 'new_m_l3_norm_mix', 'new_m_l3_sg_w_in', 'new_m_l3_sg_b_in', 'new_m_l3_sg_ln_g', 'new_m_l3_sg_ln_b', 'new_m_l3_sg_w_s', 'new_m_l3_sg_b_s', 'new_m_l3_sg_w_out', 'new_m_l3_sg_b_out', 'new_m_l3_norm_ffn', 'new_m_l3_ffn_w_up', 'new_m_l3_ffn_dw_w', 'new_m_l3_ffn_dw_b', 'new_m_l3_ffn_w_down', 'new_m_final_norm', 'new_v_l0_norm_mix', 'new_v_l0_cc_w_in', 'new_v_l0_cc_b_in', 'new_v_l0_cc_dw_w', 'new_v_l0_cc_dw_b', 'new_v_l0_cc_ln_g', 'new_v_l0_cc_ln_b', 'new_v_l0_cc_w_out', 'new_v_l0_cc_b_out', 'new_v_l0_norm_ffn', 'new_v_l0_ffn_w_up', 'new_v_l0_ffn_dw_w', 'new_v_l0_ffn_dw_b', 'new_v_l0_ffn_w_down', 'new_v_l1_norm_mix', 'new_v_l1_dsa_w_qkv', 'new_v_l1_dsa_w_o', 'new_v_l1_norm_ffn', 'new_v_l1_ffn_w_up', 'new_v_l1_ffn_dw_w', 'new_v_l1_ffn_dw_b', 'new_v_l1_ffn_w_down', 'new_v_l2_norm_mix', 'new_v_l2_mla_w_in', 'new_v_l2_mla_q_norm', 'new_v_l2_mla_w_qb', 'new_v_l2_mla_kv_norm', 'new_v_l2_mla_w_kvb', 'new_v_l2_mla_w_o', 'new_v_l2_norm_ffn', 'new_v_l2_ffn_w_up', 'new_v_l2_ffn_dw_w', 'new_v_l2_ffn_dw_b', 'new_v_l2_ffn_w_down', 'new_v_l3_norm_mix', 'new_v_l3_sg_w_in', 'new_v_l3_sg_b_in', 'new_v_l3_sg_ln_g', 'new_v_l3_sg_ln_b', 'new_v_l3_sg_w_s', 'new_v_l3_sg_b_s', 'new_v_l3_sg_w_out', 'new_v_l3_sg_b_out', 'new_v_l3_norm_ffn', 'new_v_l3_ffn_w_up', 'new_v_l3_ffn_dw_w', 'new_v_l3_ffn_dw_b', 'new_v_l3_ffn_w_down', 'new_v_final_norm']
TWIN_LEAF_KINDS = {'loss': 'loss', 'grad_x': 'grad_x', 'grad_l0_norm_mix': 'grad_w', 'grad_l0_cc_w_in': 'grad_w', 'grad_l0_cc_b_in': 'grad_w', 'grad_l0_cc_dw_w': 'grad_w', 'grad_l0_cc_dw_b': 'grad_w', 'grad_l0_cc_ln_g': 'grad_w', 'grad_l0_cc_ln_b': 'grad_w', 'grad_l0_cc_w_out': 'grad_w', 'grad_l0_cc_b_out': 'grad_w', 'grad_l0_norm_ffn': 'grad_w', 'grad_l0_ffn_w_up': 'grad_w', 'grad_l0_ffn_dw_w': 'grad_w', 'grad_l0_ffn_dw_b': 'grad_w', 'grad_l0_ffn_w_down': 'grad_w', 'grad_l1_norm_mix': 'grad_w', 'grad_l1_dsa_w_qkv': 'grad_w', 'grad_l1_dsa_w_o': 'grad_w', 'grad_l1_norm_ffn': 'grad_w', 'grad_l1_ffn_w_up': 'grad_w', 'grad_l1_ffn_dw_w': 'grad_w', 'grad_l1_ffn_dw_b': 'grad_w', 'grad_l1_ffn_w_down': 'grad_w', 'grad_l2_norm_mix': 'grad_w', 'grad_l2_mla_w_in': 'grad_w', 'grad_l2_mla_q_norm': 'grad_w', 'grad_l2_mla_w_qb': 'grad_w', 'grad_l2_mla_kv_norm': 'grad_w', 'grad_l2_mla_w_kvb': 'grad_w', 'grad_l2_mla_w_o': 'grad_w', 'grad_l2_norm_ffn': 'grad_w', 'grad_l2_ffn_w_up': 'grad_w', 'grad_l2_ffn_dw_w': 'grad_w', 'grad_l2_ffn_dw_b': 'grad_w', 'grad_l2_ffn_w_down': 'grad_w', 'grad_l3_norm_mix': 'grad_w', 'grad_l3_sg_w_in': 'grad_w', 'grad_l3_sg_b_in': 'grad_w', 'grad_l3_sg_ln_g': 'grad_w', 'grad_l3_sg_ln_b': 'grad_w', 'grad_l3_sg_w_s': 'grad_w', 'grad_l3_sg_b_s': 'grad_w', 'grad_l3_sg_w_out': 'grad_w', 'grad_l3_sg_b_out': 'grad_w', 'grad_l3_norm_ffn': 'grad_w', 'grad_l3_ffn_w_up': 'grad_w', 'grad_l3_ffn_dw_w': 'grad_w', 'grad_l3_ffn_dw_b': 'grad_w', 'grad_l3_ffn_w_down': 'grad_w', 'grad_final_norm': 'grad_w', 'delta_l0_norm_mix': 'delta_w', 'delta_l0_cc_w_in': 'delta_w', 'delta_l0_cc_b_in': 'delta_w', 'delta_l0_cc_dw_w': 'delta_w', 'delta_l0_cc_dw_b': 'delta_w', 'delta_l0_cc_ln_g': 'delta_w', 'delta_l0_cc_ln_b': 'delta_w', 'delta_l0_cc_w_out': 'delta_w', 'delta_l0_cc_b_out': 'delta_w', 'delta_l0_norm_ffn': 'delta_w', 'delta_l0_ffn_w_up': 'delta_w', 'delta_l0_ffn_dw_w': 'delta_w', 'delta_l0_ffn_dw_b': 'delta_w', 'delta_l0_ffn_w_down': 'delta_w', 'delta_l1_norm_mix': 'delta_w', 'delta_l1_dsa_w_qkv': 'delta_w', 'delta_l1_dsa_w_o': 'delta_w', 'delta_l1_norm_ffn': 'delta_w', 'delta_l1_ffn_w_up': 'delta_w', 'delta_l1_ffn_dw_w': 'delta_w', 'delta_l1_ffn_dw_b': 'delta_w', 'delta_l1_ffn_w_down': 'delta_w', 'delta_l2_norm_mix': 'delta_w', 'delta_l2_mla_w_in': 'delta_w', 'delta_l2_mla_q_norm': 'delta_w', 'delta_l2_mla_w_qb': 'delta_w', 'delta_l2_mla_kv_norm': 'delta_w', 'delta_l2_mla_w_kvb': 'delta_w', 'delta_l2_mla_w_o': 'delta_w', 'delta_l2_norm_ffn': 'delta_w', 'delta_l2_ffn_w_up': 'delta_w', 'delta_l2_ffn_dw_w': 'delta_w', 'delta_l2_ffn_dw_b': 'delta_w', 'delta_l2_ffn_w_down': 'delta_w', 'delta_l3_norm_mix': 'delta_w', 'delta_l3_sg_w_in': 'delta_w', 'delta_l3_sg_b_in': 'delta_w', 'delta_l3_sg_ln_g': 'delta_w', 'delta_l3_sg_ln_b': 'delta_w', 'delta_l3_sg_w_s': 'delta_w', 'delta_l3_sg_b_s': 'delta_w', 'delta_l3_sg_w_out': 'delta_w', 'delta_l3_sg_b_out': 'delta_w', 'delta_l3_norm_ffn': 'delta_w', 'delta_l3_ffn_w_up': 'delta_w', 'delta_l3_ffn_dw_w': 'delta_w', 'delta_l3_ffn_dw_b': 'delta_w', 'delta_l3_ffn_w_down': 'delta_w', 'delta_final_norm': 'delta_w', 'new_m_l0_norm_mix': 'new_m', 'new_m_l0_cc_w_in': 'new_m', 'new_m_l0_cc_b_in': 'new_m', 'new_m_l0_cc_dw_w': 'new_m', 'new_m_l0_cc_dw_b': 'new_m', 'new_m_l0_cc_ln_g': 'new_m', 'new_m_l0_cc_ln_b': 'new_m', 'new_m_l0_cc_w_out': 'new_m', 'new_m_l0_cc_b_out': 'new_m', 'new_m_l0_norm_ffn': 'new_m', 'new_m_l0_ffn_w_up': 'new_m', 'new_m_l0_ffn_dw_w': 'new_m', 'new_m_l0_ffn_dw_b': 'new_m', 'new_m_l0_ffn_w_down': 'new_m', 'new_m_l1_norm_mix': 'new_m', 'new_m_l1_dsa_w_qkv': 'new_m', 'new_m_l1_dsa_w_o': 'new_m', 'new_m_l1_norm_ffn': 'new_m', 'new_m_l1_ffn_w_up': 'new_m', 'new_m_l1_ffn_dw_w': 'new_m', 'new_m_l1_ffn_dw_b': 'new_m', 'new_m_l1_ffn_w_down': 'new_m', 'new_m_l2_norm_mix': 'new_m', 'new_m_l2_mla_w_in': 'new_m', 'new_m_l2_mla_q_norm': 'new_m', 'new_m_l2_mla_w_qb': 'new_m', 'new_m_l2_mla_kv_norm': 'new_m', 'new_m_l2_mla_w_kvb': 'new_m', 'new_m_l2_mla_w_o': 'new_m', 'new_m_l2_norm_ffn': 'new_m', 'new_m_l2_ffn_w_up': 'new_m', 'new_m_l2_ffn_dw_w': 'new_m', 'new_m_l2_ffn_dw_b': 'new_m', 'new_m_l2_ffn_w_down': 'new_m', 'new_m_l3_norm_mix': 'new_m', 'new_m_l3_sg_w_in': 'new_m', 'new_m_l3_sg_b_in': 'new_m', 'new_m_l3_sg_ln_g': 'new_m', 'new_m_l3_sg_ln_b': 'new_m', 'new_m_l3_sg_w_s': 'new_m', 'new_m_l3_sg_b_s': 'new_m', 'new_m_l3_sg_w_out': 'new_m', 'new_m_l3_sg_b_out': 'new_m', 'new_m_l3_norm_ffn': 'new_m', 'new_m_l3_ffn_w_up': 'new_m', 'new_m_l3_ffn_dw_w': 'new_m', 'new_m_l3_ffn_dw_b': 'new_m', 'new_m_l3_ffn_w_down': 'new_m', 'new_m_final_norm': 'new_m', 'new_v_l0_norm_mix': 'new_v', 'new_v_l0_cc_w_in': 'new_v', 'new_v_l0_cc_b_in': 'new_v', 'new_v_l0_cc_dw_w': 'new_v', 'new_v_l0_cc_dw_b': 'new_v', 'new_v_l0_cc_ln_g': 'new_v', 'new_v_l0_cc_ln_b': 'new_v', 'new_v_l0_cc_w_out': 'new_v', 'new_v_l0_cc_b_out': 'new_v', 'new_v_l0_norm_ffn': 'new_v', 'new_v_l0_ffn_w_up': 'new_v', 'new_v_l0_ffn_dw_w': 'new_v', 'new_v_l0_ffn_dw_b': 'new_v', 'new_v_l0_ffn_w_down': 'new_v', 'new_v_l1_norm_mix': 'new_v', 'new_v_l1_dsa_w_qkv': 'new_v', 'new_v_l1_dsa_w_o': 'new_v', 'new_v_l1_norm_ffn': 'new_v', 'new_v_l1_ffn_w_up': 'new_v', 'new_v_l1_ffn_dw_w': 'new_v', 'new_v_l1_ffn_dw_b': 'new_v', 'new_v_l1_ffn_w_down': 'new_v', 'new_v_l2_norm_mix': 'new_v', 'new_v_l2_mla_w_in': 'new_v', 'new_v_l2_mla_q_norm': 'new_v', 'new_v_l2_mla_w_qb': 'new_v', 'new_v_l2_mla_kv_norm': 'new_v', 'new_v_l2_mla_w_kvb': 'new_v', 'new_v_l2_mla_w_o': 'new_v', 'new_v_l2_norm_ffn': 'new_v', 'new_v_l2_ffn_w_up': 'new_v', 'new_v_l2_ffn_dw_w': 'new_v', 'new_v_l2_ffn_dw_b': 'new_v', 'new_v_l2_ffn_w_down': 'new_v', 'new_v_l3_norm_mix': 'new_v', 'new_v_l3_sg_w_in': 'new_v', 'new_v_l3_sg_b_in': 'new_v', 'new_v_l3_sg_ln_g': 'new_v', 'new_v_l3_sg_ln_b': 'new_v', 'new_v_l3_sg_w_s': 'new_v', 'new_v_l3_sg_b_s': 'new_v', 'new_v_l3_sg_w_out': 'new_v', 'new_v_l3_sg_b_out': 'new_v', 'new_v_l3_norm_ffn': 'new_v', 'new_v_l3_ffn_w_up': 'new_v', 'new_v_l3_ffn_dw_w': 'new_v', 'new_v_l3_ffn_dw_b': 'new_v', 'new_v_l3_ffn_w_down': 'new_v', 'new_v_final_norm': 'new_v'}


def _forward(args):
    return _fwd_reference(*[args[k] for k in FWD_PARAMS])


def _output_shape():
    out = _jax.eval_shape(lambda: _forward(_fwd_setup_inputs(0)))
    return out.shape, out.dtype

N_MICROBATCH = 1
ADAM_LR = 0.001
ADAM_B1 = 0.9
ADAM_B2 = 0.999
ADAM_EPS = 1e-08
ADAM_WD = 0.01
ADAM_STEP = 10
PER_EXAMPLE_BATCH_AXIS = {'x': 0, 'loss_target': 0}
SHARED_INPUTS = []
_WEIGHT_DTYPES = {'l0_norm_mix': _jnp.float32, 'l0_cc_w_in': _jnp.float32, 'l0_cc_b_in': _jnp.float32, 'l0_cc_dw_w': _jnp.float32, 'l0_cc_dw_b': _jnp.float32, 'l0_cc_ln_g': _jnp.float32, 'l0_cc_ln_b': _jnp.float32, 'l0_cc_w_out': _jnp.float32, 'l0_cc_b_out': _jnp.float32, 'l0_norm_ffn': _jnp.float32, 'l0_ffn_w_up': _jnp.float32, 'l0_ffn_dw_w': _jnp.float32, 'l0_ffn_dw_b': _jnp.float32, 'l0_ffn_w_down': _jnp.float32, 'l1_norm_mix': _jnp.float32, 'l1_dsa_w_qkv': _jnp.float32, 'l1_dsa_w_o': _jnp.float32, 'l1_norm_ffn': _jnp.float32, 'l1_ffn_w_up': _jnp.float32, 'l1_ffn_dw_w': _jnp.float32, 'l1_ffn_dw_b': _jnp.float32, 'l1_ffn_w_down': _jnp.float32, 'l2_norm_mix': _jnp.float32, 'l2_mla_w_in': _jnp.float32, 'l2_mla_q_norm': _jnp.float32, 'l2_mla_w_qb': _jnp.float32, 'l2_mla_kv_norm': _jnp.float32, 'l2_mla_w_kvb': _jnp.float32, 'l2_mla_w_o': _jnp.float32, 'l2_norm_ffn': _jnp.float32, 'l2_ffn_w_up': _jnp.float32, 'l2_ffn_dw_w': _jnp.float32, 'l2_ffn_dw_b': _jnp.float32, 'l2_ffn_w_down': _jnp.float32, 'l3_norm_mix': _jnp.float32, 'l3_sg_w_in': _jnp.float32, 'l3_sg_b_in': _jnp.float32, 'l3_sg_ln_g': _jnp.float32, 'l3_sg_ln_b': _jnp.float32, 'l3_sg_w_s': _jnp.float32, 'l3_sg_b_s': _jnp.float32, 'l3_sg_w_out': _jnp.float32, 'l3_sg_b_out': _jnp.float32, 'l3_norm_ffn': _jnp.float32, 'l3_ffn_w_up': _jnp.float32, 'l3_ffn_dw_w': _jnp.float32, 'l3_ffn_dw_b': _jnp.float32, 'l3_ffn_w_down': _jnp.float32, 'final_norm': _jnp.float32}
MOMENT_SCALE = {'l0_norm_mix': 7.429305e-02, 'l0_cc_w_in': 5.280627e-02, 'l0_cc_b_in': 8.706687e-02, 'l0_cc_dw_w': 6.915936e-02, 'l0_cc_dw_b': 1.946132e-01, 'l0_cc_ln_g': 1.027117e-01, 'l0_cc_ln_b': 1.182085e-01, 'l0_cc_w_out': 7.480240e-02, 'l0_cc_b_out': 2.153485e-01, 'l0_norm_ffn': 8.614472e-02, 'l0_ffn_w_up': 3.377186e-02, 'l0_ffn_dw_w': 3.370189e-02, 'l0_ffn_dw_b': 3.914300e-02, 'l0_ffn_w_down': 5.540143e-02, 'l1_norm_mix': 2.986112e-02, 'l1_dsa_w_qkv': 1.422794e-02, 'l1_dsa_w_o': 2.225219e-02, 'l1_norm_ffn': 6.432286e-02, 'l1_ffn_w_up': 2.745069e-02, 'l1_ffn_dw_w': 2.727657e-02, 'l1_ffn_dw_b': 3.067879e-02, 'l1_ffn_w_down': 4.512930e-02, 'l2_norm_mix': 2.630754e-02, 'l2_mla_w_in': 3.627079e-02, 'l2_mla_q_norm': 2.455217e-02, 'l2_mla_w_qb': 9.815712e-03, 'l2_mla_kv_norm': 4.591809e-02, 'l2_mla_w_kvb': 1.581903e-02, 'l2_mla_w_o': 2.016594e-02, 'l2_norm_ffn': 5.513902e-02, 'l2_ffn_w_up': 2.366107e-02, 'l2_ffn_dw_w': 2.412988e-02, 'l2_ffn_dw_b': 2.729265e-02, 'l2_ffn_w_down': 3.899322e-02, 'l3_norm_mix': 5.172019e-02, 'l3_sg_w_in': 2.173831e-02, 'l3_sg_b_in': 3.330782e-02, 'l3_sg_ln_g': 1.362240e-02, 'l3_sg_ln_b': 1.446616e-02, 'l3_sg_w_s': 3.405952e-02, 'l3_sg_b_s': 4.704474e-02, 'l3_sg_w_out': 5.514699e-02, 'l3_sg_b_out': 1.601302e-01, 'l3_norm_ffn': 4.148442e-02, 'l3_ffn_w_up': 1.850067e-02, 'l3_ffn_dw_w': 1.843112e-02, 'l3_ffn_dw_b': 2.198565e-02, 'l3_ffn_w_down': 3.076615e-02, 'final_norm': 1.608318e+01}


def _to_microbatches(a, axis):
    t = _jnp.moveaxis(a, axis, 0)
    t = t.reshape((N_MICROBATCH, t.shape[0] // N_MICROBATCH) + t.shape[1:])
    return _jnp.moveaxis(t, 1, axis + 1)


def setup_inputs(seed: int = 0) -> dict:
    inp = _fwd_setup_inputs(seed)
    key = _jax.random.fold_in(_jax.random.key(seed), 7919)
    shape, _ = _output_shape()
    out = dict(inp)
    out["loss_target"] = _jax.random.normal(_jax.random.fold_in(key, 0), shape, _jnp.float32)
    for i, name in enumerate(TWIN_WEIGHTS):
        w = inp[name].astype(_jnp.float32)
        if MOMENT_SCALE is None:
            s = _jnp.sqrt(_jnp.mean(_jnp.square(w)) + 1e-30)
        else:
            s = MOMENT_SCALE[name]
        km, kv = _jax.random.split(_jax.random.fold_in(key, i + 1))
        out[name] = w
        out["m_" + name] = s * _jax.random.normal(km, w.shape, _jnp.float32)
        out["v_" + name] = (s * s) * _jax.random.uniform(kv, w.shape, _jnp.float32, 0.5, 1.5)
    if N_MICROBATCH > 1:
        for name, axis in PER_EXAMPLE_BATCH_AXIS.items():
            out[name] = _to_microbatches(out[name], axis)
    return {'x': out['x'], 'l0_norm_mix': out['l0_norm_mix'], 'l0_cc_w_in': out['l0_cc_w_in'], 'l0_cc_b_in': out['l0_cc_b_in'], 'l0_cc_dw_w': out['l0_cc_dw_w'], 'l0_cc_dw_b': out['l0_cc_dw_b'], 'l0_cc_ln_g': out['l0_cc_ln_g'], 'l0_cc_ln_b': out['l0_cc_ln_b'], 'l0_cc_w_out': out['l0_cc_w_out'], 'l0_cc_b_out': out['l0_cc_b_out'], 'l0_norm_ffn': out['l0_norm_ffn'], 'l0_ffn_w_up': out['l0_ffn_w_up'], 'l0_ffn_dw_w': out['l0_ffn_dw_w'], 'l0_ffn_dw_b': out['l0_ffn_dw_b'], 'l0_ffn_w_down': out['l0_ffn_w_down'], 'l1_norm_mix': out['l1_norm_mix'], 'l1_dsa_w_qkv': out['l1_dsa_w_qkv'], 'l1_dsa_w_o': out['l1_dsa_w_o'], 'l1_norm_ffn': out['l1_norm_ffn'], 'l1_ffn_w_up': out['l1_ffn_w_up'], 'l1_ffn_dw_w': out['l1_ffn_dw_w'], 'l1_ffn_dw_b': out['l1_ffn_dw_b'], 'l1_ffn_w_down': out['l1_ffn_w_down'], 'l2_norm_mix': out['l2_norm_mix'], 'l2_mla_w_in': out['l2_mla_w_in'], 'l2_mla_q_norm': out['l2_mla_q_norm'], 'l2_mla_w_qb': out['l2_mla_w_qb'], 'l2_mla_kv_norm': out['l2_mla_kv_norm'], 'l2_mla_w_kvb': out['l2_mla_w_kvb'], 'l2_mla_w_o': out['l2_mla_w_o'], 'l2_norm_ffn': out['l2_norm_ffn'], 'l2_ffn_w_up': out['l2_ffn_w_up'], 'l2_ffn_dw_w': out['l2_ffn_dw_w'], 'l2_ffn_dw_b': out['l2_ffn_dw_b'], 'l2_ffn_w_down': out['l2_ffn_w_down'], 'l3_norm_mix': out['l3_norm_mix'], 'l3_sg_w_in': out['l3_sg_w_in'], 'l3_sg_b_in': out['l3_sg_b_in'], 'l3_sg_ln_g': out['l3_sg_ln_g'], 'l3_sg_ln_b': out['l3_sg_ln_b'], 'l3_sg_w_s': out['l3_sg_w_s'], 'l3_sg_b_s': out['l3_sg_b_s'], 'l3_sg_w_out': out['l3_sg_w_out'], 'l3_sg_b_out': out['l3_sg_b_out'], 'l3_norm_ffn': out['l3_norm_ffn'], 'l3_ffn_w_up': out['l3_ffn_w_up'], 'l3_ffn_dw_w': out['l3_ffn_dw_w'], 'l3_ffn_dw_b': out['l3_ffn_dw_b'], 'l3_ffn_w_down': out['l3_ffn_w_down'], 'final_norm': out['final_norm'], 'loss_target': out['loss_target'], 'm_l0_norm_mix': out['m_l0_norm_mix'], 'm_l0_cc_w_in': out['m_l0_cc_w_in'], 'm_l0_cc_b_in': out['m_l0_cc_b_in'], 'm_l0_cc_dw_w': out['m_l0_cc_dw_w'], 'm_l0_cc_dw_b': out['m_l0_cc_dw_b'], 'm_l0_cc_ln_g': out['m_l0_cc_ln_g'], 'm_l0_cc_ln_b': out['m_l0_cc_ln_b'], 'm_l0_cc_w_out': out['m_l0_cc_w_out'], 'm_l0_cc_b_out': out['m_l0_cc_b_out'], 'm_l0_norm_ffn': out['m_l0_norm_ffn'], 'm_l0_ffn_w_up': out['m_l0_ffn_w_up'], 'm_l0_ffn_dw_w': out['m_l0_ffn_dw_w'], 'm_l0_ffn_dw_b': out['m_l0_ffn_dw_b'], 'm_l0_ffn_w_down': out['m_l0_ffn_w_down'], 'm_l1_norm_mix': out['m_l1_norm_mix'], 'm_l1_dsa_w_qkv': out['m_l1_dsa_w_qkv'], 'm_l1_dsa_w_o': out['m_l1_dsa_w_o'], 'm_l1_norm_ffn': out['m_l1_norm_ffn'], 'm_l1_ffn_w_up': out['m_l1_ffn_w_up'], 'm_l1_ffn_dw_w': out['m_l1_ffn_dw_w'], 'm_l1_ffn_dw_b': out['m_l1_ffn_dw_b'], 'm_l1_ffn_w_down': out['m_l1_ffn_w_down'], 'm_l2_norm_mix': out['m_l2_norm_mix'], 'm_l2_mla_w_in': out['m_l2_mla_w_in'], 'm_l2_mla_q_norm': out['m_l2_mla_q_norm'], 'm_l2_mla_w_qb': out['m_l2_mla_w_qb'], 'm_l2_mla_kv_norm': out['m_l2_mla_kv_norm'], 'm_l2_mla_w_kvb': out['m_l2_mla_w_kvb'], 'm_l2_mla_w_o': out['m_l2_mla_w_o'], 'm_l2_norm_ffn': out['m_l2_norm_ffn'], 'm_l2_ffn_w_up': out['m_l2_ffn_w_up'], 'm_l2_ffn_dw_w': out['m_l2_ffn_dw_w'], 'm_l2_ffn_dw_b': out['m_l2_ffn_dw_b'], 'm_l2_ffn_w_down': out['m_l2_ffn_w_down'], 'm_l3_norm_mix': out['m_l3_norm_mix'], 'm_l3_sg_w_in': out['m_l3_sg_w_in'], 'm_l3_sg_b_in': out['m_l3_sg_b_in'], 'm_l3_sg_ln_g': out['m_l3_sg_ln_g'], 'm_l3_sg_ln_b': out['m_l3_sg_ln_b'], 'm_l3_sg_w_s': out['m_l3_sg_w_s'], 'm_l3_sg_b_s': out['m_l3_sg_b_s'], 'm_l3_sg_w_out': out['m_l3_sg_w_out'], 'm_l3_sg_b_out': out['m_l3_sg_b_out'], 'm_l3_norm_ffn': out['m_l3_norm_ffn'], 'm_l3_ffn_w_up': out['m_l3_ffn_w_up'], 'm_l3_ffn_dw_w': out['m_l3_ffn_dw_w'], 'm_l3_ffn_dw_b': out['m_l3_ffn_dw_b'], 'm_l3_ffn_w_down': out['m_l3_ffn_w_down'], 'm_final_norm': out['m_final_norm'], 'v_l0_norm_mix': out['v_l0_norm_mix'], 'v_l0_cc_w_in': out['v_l0_cc_w_in'], 'v_l0_cc_b_in': out['v_l0_cc_b_in'], 'v_l0_cc_dw_w': out['v_l0_cc_dw_w'], 'v_l0_cc_dw_b': out['v_l0_cc_dw_b'], 'v_l0_cc_ln_g': out['v_l0_cc_ln_g'], 'v_l0_cc_ln_b': out['v_l0_cc_ln_b'], 'v_l0_cc_w_out': out['v_l0_cc_w_out'], 'v_l0_cc_b_out': out['v_l0_cc_b_out'], 'v_l0_norm_ffn': out['v_l0_norm_ffn'], 'v_l0_ffn_w_up': out['v_l0_ffn_w_up'], 'v_l0_ffn_dw_w': out['v_l0_ffn_dw_w'], 'v_l0_ffn_dw_b': out['v_l0_ffn_dw_b'], 'v_l0_ffn_w_down': out['v_l0_ffn_w_down'], 'v_l1_norm_mix': out['v_l1_norm_mix'], 'v_l1_dsa_w_qkv': out['v_l1_dsa_w_qkv'], 'v_l1_dsa_w_o': out['v_l1_dsa_w_o'], 'v_l1_norm_ffn': out['v_l1_norm_ffn'], 'v_l1_ffn_w_up': out['v_l1_ffn_w_up'], 'v_l1_ffn_dw_w': out['v_l1_ffn_dw_w'], 'v_l1_ffn_dw_b': out['v_l1_ffn_dw_b'], 'v_l1_ffn_w_down': out['v_l1_ffn_w_down'], 'v_l2_norm_mix': out['v_l2_norm_mix'], 'v_l2_mla_w_in': out['v_l2_mla_w_in'], 'v_l2_mla_q_norm': out['v_l2_mla_q_norm'], 'v_l2_mla_w_qb': out['v_l2_mla_w_qb'], 'v_l2_mla_kv_norm': out['v_l2_mla_kv_norm'], 'v_l2_mla_w_kvb': out['v_l2_mla_w_kvb'], 'v_l2_mla_w_o': out['v_l2_mla_w_o'], 'v_l2_norm_ffn': out['v_l2_norm_ffn'], 'v_l2_ffn_w_up': out['v_l2_ffn_w_up'], 'v_l2_ffn_dw_w': out['v_l2_ffn_dw_w'], 'v_l2_ffn_dw_b': out['v_l2_ffn_dw_b'], 'v_l2_ffn_w_down': out['v_l2_ffn_w_down'], 'v_l3_norm_mix': out['v_l3_norm_mix'], 'v_l3_sg_w_in': out['v_l3_sg_w_in'], 'v_l3_sg_b_in': out['v_l3_sg_b_in'], 'v_l3_sg_ln_g': out['v_l3_sg_ln_g'], 'v_l3_sg_ln_b': out['v_l3_sg_ln_b'], 'v_l3_sg_w_s': out['v_l3_sg_w_s'], 'v_l3_sg_b_s': out['v_l3_sg_b_s'], 'v_l3_sg_w_out': out['v_l3_sg_w_out'], 'v_l3_sg_b_out': out['v_l3_sg_b_out'], 'v_l3_norm_ffn': out['v_l3_norm_ffn'], 'v_l3_ffn_w_up': out['v_l3_ffn_w_up'], 'v_l3_ffn_dw_w': out['v_l3_ffn_dw_w'], 'v_l3_ffn_dw_b': out['v_l3_ffn_dw_b'], 'v_l3_ffn_w_down': out['v_l3_ffn_w_down'], 'v_final_norm': out['v_final_norm']}


def _loss(weights, diff, rest, loss_target):
    with _jax.named_scope("forward"):
        args = {**rest, TWIN_DIFF_INPUT: diff, **{k: w.astype(_WEIGHT_DTYPES[k]) for k, w in weights.items()}}
        y = _forward(args)
    with _jax.named_scope("loss_head"):
        err = _jnp.square(y.astype(_jnp.float32) - loss_target)
        return 0.5 * _jnp.sum(_jnp.mean(err, axis=-1)) if err.ndim else 0.5 * err


def _adamw(w, g, m, v):
    m = ADAM_B1 * m + (1.0 - ADAM_B1) * g
    v = ADAM_B2 * v + (1.0 - ADAM_B2) * _jnp.square(g)
    m_hat = m / (1.0 - ADAM_B1 ** ADAM_STEP)
    v_hat = v / (1.0 - ADAM_B2 ** ADAM_STEP)
    delta = -ADAM_LR * (m_hat / (_jnp.sqrt(v_hat) + ADAM_EPS) + ADAM_WD * w)
    return delta, m, v


def reference(x, l0_norm_mix, l0_cc_w_in, l0_cc_b_in, l0_cc_dw_w, l0_cc_dw_b, l0_cc_ln_g, l0_cc_ln_b, l0_cc_w_out, l0_cc_b_out, l0_norm_ffn, l0_ffn_w_up, l0_ffn_dw_w, l0_ffn_dw_b, l0_ffn_w_down, l1_norm_mix, l1_dsa_w_qkv, l1_dsa_w_o, l1_norm_ffn, l1_ffn_w_up, l1_ffn_dw_w, l1_ffn_dw_b, l1_ffn_w_down, l2_norm_mix, l2_mla_w_in, l2_mla_q_norm, l2_mla_w_qb, l2_mla_kv_norm, l2_mla_w_kvb, l2_mla_w_o, l2_norm_ffn, l2_ffn_w_up, l2_ffn_dw_w, l2_ffn_dw_b, l2_ffn_w_down, l3_norm_mix, l3_sg_w_in, l3_sg_b_in, l3_sg_ln_g, l3_sg_ln_b, l3_sg_w_s, l3_sg_b_s, l3_sg_w_out, l3_sg_b_out, l3_norm_ffn, l3_ffn_w_up, l3_ffn_dw_w, l3_ffn_dw_b, l3_ffn_w_down, final_norm, loss_target, m_l0_norm_mix, m_l0_cc_w_in, m_l0_cc_b_in, m_l0_cc_dw_w, m_l0_cc_dw_b, m_l0_cc_ln_g, m_l0_cc_ln_b, m_l0_cc_w_out, m_l0_cc_b_out, m_l0_norm_ffn, m_l0_ffn_w_up, m_l0_ffn_dw_w, m_l0_ffn_dw_b, m_l0_ffn_w_down, m_l1_norm_mix, m_l1_dsa_w_qkv, m_l1_dsa_w_o, m_l1_norm_ffn, m_l1_ffn_w_up, m_l1_ffn_dw_w, m_l1_ffn_dw_b, m_l1_ffn_w_down, m_l2_norm_mix, m_l2_mla_w_in, m_l2_mla_q_norm, m_l2_mla_w_qb, m_l2_mla_kv_norm, m_l2_mla_w_kvb, m_l2_mla_w_o, m_l2_norm_ffn, m_l2_ffn_w_up, m_l2_ffn_dw_w, m_l2_ffn_dw_b, m_l2_ffn_w_down, m_l3_norm_mix, m_l3_sg_w_in, m_l3_sg_b_in, m_l3_sg_ln_g, m_l3_sg_ln_b, m_l3_sg_w_s, m_l3_sg_b_s, m_l3_sg_w_out, m_l3_sg_b_out, m_l3_norm_ffn, m_l3_ffn_w_up, m_l3_ffn_dw_w, m_l3_ffn_dw_b, m_l3_ffn_w_down, m_final_norm, v_l0_norm_mix, v_l0_cc_w_in, v_l0_cc_b_in, v_l0_cc_dw_w, v_l0_cc_dw_b, v_l0_cc_ln_g, v_l0_cc_ln_b, v_l0_cc_w_out, v_l0_cc_b_out, v_l0_norm_ffn, v_l0_ffn_w_up, v_l0_ffn_dw_w, v_l0_ffn_dw_b, v_l0_ffn_w_down, v_l1_norm_mix, v_l1_dsa_w_qkv, v_l1_dsa_w_o, v_l1_norm_ffn, v_l1_ffn_w_up, v_l1_ffn_dw_w, v_l1_ffn_dw_b, v_l1_ffn_w_down, v_l2_norm_mix, v_l2_mla_w_in, v_l2_mla_q_norm, v_l2_mla_w_qb, v_l2_mla_kv_norm, v_l2_mla_w_kvb, v_l2_mla_w_o, v_l2_norm_ffn, v_l2_ffn_w_up, v_l2_ffn_dw_w, v_l2_ffn_dw_b, v_l2_ffn_w_down, v_l3_norm_mix, v_l3_sg_w_in, v_l3_sg_b_in, v_l3_sg_ln_g, v_l3_sg_ln_b, v_l3_sg_w_s, v_l3_sg_b_s, v_l3_sg_w_out, v_l3_sg_b_out, v_l3_norm_ffn, v_l3_ffn_w_up, v_l3_ffn_dw_w, v_l3_ffn_dw_b, v_l3_ffn_w_down, v_final_norm):
    given = dict(x=x, l0_norm_mix=l0_norm_mix, l0_cc_w_in=l0_cc_w_in, l0_cc_b_in=l0_cc_b_in, l0_cc_dw_w=l0_cc_dw_w, l0_cc_dw_b=l0_cc_dw_b, l0_cc_ln_g=l0_cc_ln_g, l0_cc_ln_b=l0_cc_ln_b, l0_cc_w_out=l0_cc_w_out, l0_cc_b_out=l0_cc_b_out, l0_norm_ffn=l0_norm_ffn, l0_ffn_w_up=l0_ffn_w_up, l0_ffn_dw_w=l0_ffn_dw_w, l0_ffn_dw_b=l0_ffn_dw_b, l0_ffn_w_down=l0_ffn_w_down, l1_norm_mix=l1_norm_mix, l1_dsa_w_qkv=l1_dsa_w_qkv, l1_dsa_w_o=l1_dsa_w_o, l1_norm_ffn=l1_norm_ffn, l1_ffn_w_up=l1_ffn_w_up, l1_ffn_dw_w=l1_ffn_dw_w, l1_ffn_dw_b=l1_ffn_dw_b, l1_ffn_w_down=l1_ffn_w_down, l2_norm_mix=l2_norm_mix, l2_mla_w_in=l2_mla_w_in, l2_mla_q_norm=l2_mla_q_norm, l2_mla_w_qb=l2_mla_w_qb, l2_mla_kv_norm=l2_mla_kv_norm, l2_mla_w_kvb=l2_mla_w_kvb, l2_mla_w_o=l2_mla_w_o, l2_norm_ffn=l2_norm_ffn, l2_ffn_w_up=l2_ffn_w_up, l2_ffn_dw_w=l2_ffn_dw_w, l2_ffn_dw_b=l2_ffn_dw_b, l2_ffn_w_down=l2_ffn_w_down, l3_norm_mix=l3_norm_mix, l3_sg_w_in=l3_sg_w_in, l3_sg_b_in=l3_sg_b_in, l3_sg_ln_g=l3_sg_ln_g, l3_sg_ln_b=l3_sg_ln_b, l3_sg_w_s=l3_sg_w_s, l3_sg_b_s=l3_sg_b_s, l3_sg_w_out=l3_sg_w_out, l3_sg_b_out=l3_sg_b_out, l3_norm_ffn=l3_norm_ffn, l3_ffn_w_up=l3_ffn_w_up, l3_ffn_dw_w=l3_ffn_dw_w, l3_ffn_dw_b=l3_ffn_dw_b, l3_ffn_w_down=l3_ffn_w_down, final_norm=final_norm, loss_target=loss_target, m_l0_norm_mix=m_l0_norm_mix, m_l0_cc_w_in=m_l0_cc_w_in, m_l0_cc_b_in=m_l0_cc_b_in, m_l0_cc_dw_w=m_l0_cc_dw_w, m_l0_cc_dw_b=m_l0_cc_dw_b, m_l0_cc_ln_g=m_l0_cc_ln_g, m_l0_cc_ln_b=m_l0_cc_ln_b, m_l0_cc_w_out=m_l0_cc_w_out, m_l0_cc_b_out=m_l0_cc_b_out, m_l0_norm_ffn=m_l0_norm_ffn, m_l0_ffn_w_up=m_l0_ffn_w_up, m_l0_ffn_dw_w=m_l0_ffn_dw_w, m_l0_ffn_dw_b=m_l0_ffn_dw_b, m_l0_ffn_w_down=m_l0_ffn_w_down, m_l1_norm_mix=m_l1_norm_mix, m_l1_dsa_w_qkv=m_l1_dsa_w_qkv, m_l1_dsa_w_o=m_l1_dsa_w_o, m_l1_norm_ffn=m_l1_norm_ffn, m_l1_ffn_w_up=m_l1_ffn_w_up, m_l1_ffn_dw_w=m_l1_ffn_dw_w, m_l1_ffn_dw_b=m_l1_ffn_dw_b, m_l1_ffn_w_down=m_l1_ffn_w_down, m_l2_norm_mix=m_l2_norm_mix, m_l2_mla_w_in=m_l2_mla_w_in, m_l2_mla_q_norm=m_l2_mla_q_norm, m_l2_mla_w_qb=m_l2_mla_w_qb, m_l2_mla_kv_norm=m_l2_mla_kv_norm, m_l2_mla_w_kvb=m_l2_mla_w_kvb, m_l2_mla_w_o=m_l2_mla_w_o, m_l2_norm_ffn=m_l2_norm_ffn, m_l2_ffn_w_up=m_l2_ffn_w_up, m_l2_ffn_dw_w=m_l2_ffn_dw_w, m_l2_ffn_dw_b=m_l2_ffn_dw_b, m_l2_ffn_w_down=m_l2_ffn_w_down, m_l3_norm_mix=m_l3_norm_mix, m_l3_sg_w_in=m_l3_sg_w_in, m_l3_sg_b_in=m_l3_sg_b_in, m_l3_sg_ln_g=m_l3_sg_ln_g, m_l3_sg_ln_b=m_l3_sg_ln_b, m_l3_sg_w_s=m_l3_sg_w_s, m_l3_sg_b_s=m_l3_sg_b_s, m_l3_sg_w_out=m_l3_sg_w_out, m_l3_sg_b_out=m_l3_sg_b_out, m_l3_norm_ffn=m_l3_norm_ffn, m_l3_ffn_w_up=m_l3_ffn_w_up, m_l3_ffn_dw_w=m_l3_ffn_dw_w, m_l3_ffn_dw_b=m_l3_ffn_dw_b, m_l3_ffn_w_down=m_l3_ffn_w_down, m_final_norm=m_final_norm, v_l0_norm_mix=v_l0_norm_mix, v_l0_cc_w_in=v_l0_cc_w_in, v_l0_cc_b_in=v_l0_cc_b_in, v_l0_cc_dw_w=v_l0_cc_dw_w, v_l0_cc_dw_b=v_l0_cc_dw_b, v_l0_cc_ln_g=v_l0_cc_ln_g, v_l0_cc_ln_b=v_l0_cc_ln_b, v_l0_cc_w_out=v_l0_cc_w_out, v_l0_cc_b_out=v_l0_cc_b_out, v_l0_norm_ffn=v_l0_norm_ffn, v_l0_ffn_w_up=v_l0_ffn_w_up, v_l0_ffn_dw_w=v_l0_ffn_dw_w, v_l0_ffn_dw_b=v_l0_ffn_dw_b, v_l0_ffn_w_down=v_l0_ffn_w_down, v_l1_norm_mix=v_l1_norm_mix, v_l1_dsa_w_qkv=v_l1_dsa_w_qkv, v_l1_dsa_w_o=v_l1_dsa_w_o, v_l1_norm_ffn=v_l1_norm_ffn, v_l1_ffn_w_up=v_l1_ffn_w_up, v_l1_ffn_dw_w=v_l1_ffn_dw_w, v_l1_ffn_dw_b=v_l1_ffn_dw_b, v_l1_ffn_w_down=v_l1_ffn_w_down, v_l2_norm_mix=v_l2_norm_mix, v_l2_mla_w_in=v_l2_mla_w_in, v_l2_mla_q_norm=v_l2_mla_q_norm, v_l2_mla_w_qb=v_l2_mla_w_qb, v_l2_mla_kv_norm=v_l2_mla_kv_norm, v_l2_mla_w_kvb=v_l2_mla_w_kvb, v_l2_mla_w_o=v_l2_mla_w_o, v_l2_norm_ffn=v_l2_norm_ffn, v_l2_ffn_w_up=v_l2_ffn_w_up, v_l2_ffn_dw_w=v_l2_ffn_dw_w, v_l2_ffn_dw_b=v_l2_ffn_dw_b, v_l2_ffn_w_down=v_l2_ffn_w_down, v_l3_norm_mix=v_l3_norm_mix, v_l3_sg_w_in=v_l3_sg_w_in, v_l3_sg_b_in=v_l3_sg_b_in, v_l3_sg_ln_g=v_l3_sg_ln_g, v_l3_sg_ln_b=v_l3_sg_ln_b, v_l3_sg_w_s=v_l3_sg_w_s, v_l3_sg_b_s=v_l3_sg_b_s, v_l3_sg_w_out=v_l3_sg_w_out, v_l3_sg_b_out=v_l3_sg_b_out, v_l3_norm_ffn=v_l3_norm_ffn, v_l3_ffn_w_up=v_l3_ffn_w_up, v_l3_ffn_dw_w=v_l3_ffn_dw_w, v_l3_ffn_dw_b=v_l3_ffn_dw_b, v_l3_ffn_w_down=v_l3_ffn_w_down, v_final_norm=v_final_norm)
    weights = {n: given[n] for n in TWIN_WEIGHTS}
    shared = {n: given[n] for n in SHARED_INPUTS}
    per_example = {n: given[n] for n in ['x']}
    grad_fn = _jax.value_and_grad(_loss, argnums=(0, 1))

    def one_microbatch(ex, loss_target):
        ex = dict(ex)
        diff = ex.pop(TWIN_DIFF_INPUT)
        return grad_fn(weights, diff, {**shared, **ex}, loss_target)

    if N_MICROBATCH == 1:
        loss, (grad_w, grad_x) = one_microbatch(per_example, given["loss_target"])
    else:
        def body(carry, xs):
            loss_sum, grad_sum = carry
            l_k, (gw_k, gx_k) = one_microbatch(xs[0], xs[1])
            with _jax.named_scope("update"):
                return (loss_sum + l_k, _jax.tree.map(_jnp.add, grad_sum, gw_k)), gx_k

        init = (_jnp.zeros((), _jnp.float32), _jax.tree.map(_jnp.zeros_like, weights))
        (loss, grad_w), grad_x = _jax.lax.scan(body, init, (per_example, given["loss_target"]))
    with _jax.named_scope("update"):
        delta_w, new_m, new_v = {}, {}, {}
        for n in TWIN_WEIGHTS:
            delta_w[n], new_m[n], new_v[n] = _adamw(weights[n], grad_w[n], given["m_" + n], given["v_" + n])
    return (loss, grad_x, *[grad_w[n] for n in TWIN_WEIGHTS], *[delta_w[n] for n in TWIN_WEIGHTS],
            *[new_m[n] for n in TWIN_WEIGHTS], *[new_v[n] for n in TWIN_WEIGHTS])
```

```python
import functools
import inspect
import math

import jax
import jax.numpy as jnp
from jax import lax
from jax.experimental import pallas as pl
from jax.experimental.pallas import tpu as pltpu

F32 = jnp.float32
BF16 = jnp.bfloat16
NORM_EPS = 1e-6
NEG_INF = -1e30
ROPE_THETA = 10000.0
LANES = 128
SUBLANES = 8
VMEM_LIMIT = 56 * 1024 * 1024
N_DEV = 8

DSA_CONFIGS = ((128, 1), (512, 4), (2048, 16))
DSA_HEADS = 8
DSA_BLOCK = 128
MLA_HEADS = 16
MLA_RANK = 512
MLA_NOPE = 128
MLA_ROPE = 64
SG_CHUNK = 128
SG_GROUPS = 8
CC_WIDTH = 31
FFN_WIDTH = 3
ADAM_LR, ADAM_B1, ADAM_B2, ADAM_EPS, ADAM_WD, ADAM_STEP = 0.001, 0.9, 0.999, 1e-08, 0.01, 10


def _tile(n, cap, unit=LANES):
    if n <= cap:
        return n
    best = 0
    for t in range(unit, cap + 1, unit):
        if n % t == 0:
            best = t
    assert best, (n, cap, unit)
    return best


def _call(body, name, grid, in_specs, out_specs, out_shape, scratch=(), prefetch=0):
    params = pltpu.CompilerParams(vmem_limit_bytes=VMEM_LIMIT)
    if prefetch:
        spec = pltpu.PrefetchScalarGridSpec(num_scalar_prefetch=prefetch, grid=grid, in_specs=in_specs,
                                            out_specs=out_specs, scratch_shapes=list(scratch))
        return pl.pallas_call(body, out_shape=out_shape, grid_spec=spec, name=name, compiler_params=params)
    return pl.pallas_call(body, out_shape=out_shape, grid=grid, in_specs=in_specs, out_specs=out_specs,
                          scratch_shapes=list(scratch), name=name, compiler_params=params)


def _sds(shape, dtype):
    return jax.ShapeDtypeStruct(tuple(shape), dtype)


def _bf(v):
    return v if v.dtype == BF16 else v.astype(BF16)


def _dot(a, b):
    return jnp.dot(a, b, preferred_element_type=F32)


def _dot_nt(a, b):
    return lax.dot_general(a, b, (((1,), (1,)), ((), ())), preferred_element_type=F32)


def _dot_tn(a, b):
    return lax.dot_general(a, b, (((0,), (0,)), ((), ())), preferred_element_type=F32)


def _sigmoid(v):
    return 1.0 / (1.0 + jnp.exp(-v))


_ERF_A = (-2.72614225801306e-10, 2.77068142495902e-08, -2.10102402082508e-06, -5.69250639462346e-05,
          -7.34990630326855e-04, -2.95459980854025e-03, -1.60960333262415e-02)
_ERF_B = (-1.45660718464996e-05, -2.13374055278905e-04, -1.68282697438203e-03, -7.37332916720468e-03,
          -1.42647390514189e-02)


def _erf(v):
    v = jnp.clip(v, -4.0, 4.0)
    v2 = v * v
    p = jnp.full_like(v, _ERF_A[0])
    for c in _ERF_A[1:]:
        p = p * v2 + c
    q = jnp.full_like(v, _ERF_B[0])
    for c in _ERF_B[1:]:
        q = q * v2 + c
    return v * p / q


def _gelu(v):
    return 0.5 * v * (1.0 + _erf(v * (2.0 ** -0.5)))


def _gelu_grad(v):
    return 0.5 * (1.0 + _erf(v * (2.0 ** -0.5))) + v * jnp.exp(-0.5 * v * v) * ((2.0 * math.pi) ** -0.5)


def _shift_down(v, k):
    if k == 0:
        return v
    rows = lax.broadcasted_iota(jnp.int32, v.shape, 0)
    return jnp.where(rows >= k, pltpu.roll(v, k, 0), 0.0)


def _shift_up(v, k):
    if k == 0:
        return v
    n = v.shape[0]
    rows = lax.broadcasted_iota(jnp.int32, v.shape, 0)
    return jnp.where(rows < n - k, pltpu.roll(v, n - k, 0), 0.0)


def _rope(v, cos, sin):
    return v * cos + pltpu.roll(v, 64, 1) * sin


def _rope_t(dv, cos, sin):
    return dv * cos + pltpu.roll(dv * sin, 64, 1)


def mm_nn(a, w, name, bias=None, out_parts=1, out_dtype=F32, tm_cap=512, tn_cap=1536, tk_cap=2816):
    pa, m, kp = a.shape
    j, k, n = w.shape
    assert pa * kp == k
    big_n = j * n
    npo = big_n // out_parts
    tm = _tile(m, tm_cap, SUBLANES)
    tn = _tile(math.gcd(n, npo), tn_cap)
    tk = _tile(kp, tk_cap)
    npj, nbo, kbp, nk = n // tn, npo // tn, kp // tk, k // tk
    has_bias = bias is not None

    def body(*refs):
        a_ref, w_ref = refs[0], refs[1]
        b_ref = refs[2] if has_bias else None
        o_ref = refs[2 + has_bias]

        def finish(acc):
            if has_bias:
                acc = acc + b_ref[...]
            o_ref[...] = acc.astype(o_ref.dtype)

        part = _dot(_bf(a_ref[...]), _bf(w_ref[...]))
        if nk == 1:
            finish(part)
        else:
            acc_ref = refs[3 + has_bias]
            ki = pl.program_id(2)

            @pl.when(ki == 0)
            def _():
                acc_ref[...] = part

            @pl.when(ki > 0)
            def _():
                acc_ref[...] += part

            @pl.when(ki == nk - 1)
            def _():
                finish(acc_ref[...])

    in_specs = [pl.BlockSpec((None, tm, tk), lambda ni, mi, ki: (ki // kbp, mi, ki % kbp)),
                pl.BlockSpec((None, tk, tn), lambda ni, mi, ki: (ni // npj, ki, ni % npj))]
    args = [a, w]
    if has_bias:
        in_specs.append(pl.BlockSpec((1, tn), lambda ni, mi, ki: (0, ni)))
        args.append(bias.reshape(1, big_n))
    out_spec = pl.BlockSpec((None, tm, tn), lambda ni, mi, ki: (ni // nbo, mi, ni % nbo))
    scratch = [pltpu.VMEM((tm, tn), F32)] if nk > 1 else []
    return _call(body, name, (big_n // tn, m // tm, nk), in_specs, out_spec,
                 _sds((out_parts, m, npo), out_dtype), scratch)(*args)


def mm_nt(a, w, name, out_dtype=F32, tm_cap=1024, tko_cap=2048, tc_cap=1536):
    pa, m, npa = a.shape
    j, k, n = w.shape
    assert pa * npa == j * n
    tm = _tile(m, tm_cap, SUBLANES)
    tko = _tile(k, tko_cap)
    tc = _tile(math.gcd(n, npa), tc_cap)
    npj, nba, nc = n // tc, npa // tc, (j * n) // tc

    def body(a_ref, w_ref, o_ref, *scr):
        part = _dot_nt(_bf(a_ref[...]), _bf(w_ref[...]))
        if nc == 1:
            o_ref[...] = part.astype(o_ref.dtype)
        else:
            acc_ref = scr[0]
            ci = pl.program_id(2)

            @pl.when(ci == 0)
            def _():
                acc_ref[...] = part

            @pl.when(ci > 0)
            def _():
                acc_ref[...] += part

            @pl.when(ci == nc - 1)
            def _():
                o_ref[...] = acc_ref[...].astype(o_ref.dtype)

    in_specs = [pl.BlockSpec((None, tm, tc), lambda mi, ko, ci: (ci // nba, mi, ci % nba)),
                pl.BlockSpec((None, tko, tc), lambda mi, ko, ci: (ci // npj, ko, ci % npj))]
    out_spec = pl.BlockSpec((tm, tko), lambda mi, ko, ci: (mi, ko))
    scratch = [pltpu.VMEM((tm, tko), F32)] if nc > 1 else []
    return _call(body, name, (m // tm, k // tko, nc), in_specs, out_spec, _sds((m, k), out_dtype), scratch)(a, w)


def mm_tn(a, b, n_shards, name, out_dtype=BF16, tt_cap=2048, tkk_cap=512, tn_cap=1536):
    pa, m, kp = a.shape
    pb, m2, npb = b.shape
    assert m == m2
    k, big_n = pa * kp, pb * npb
    n = big_n // n_shards
    tt = _tile(m, tt_cap, SUBLANES)
    tkk = _tile(kp, tkk_cap)
    tn = _tile(math.gcd(n, npb), tn_cap)
    kbp, nbb, npj, nt = kp // tkk, npb // tn, n // tn, m // tt

    def body(a_ref, b_ref, o_ref, *scr):
        part = _dot_tn(_bf(a_ref[...]), _bf(b_ref[...]))
        if nt == 1:
            o_ref[...] = part.astype(o_ref.dtype)
        else:
            acc_ref = scr[0]
            ti = pl.program_id(2)

            @pl.when(ti == 0)
            def _():
                acc_ref[...] = part

            @pl.when(ti > 0)
            def _():
                acc_ref[...] += part

            @pl.when(ti == nt - 1)
            def _():
                o_ref[...] = acc_ref[...].astype(o_ref.dtype)

    in_specs = [pl.BlockSpec((None, tt, tkk), lambda ni, ki, ti: (ki // kbp, ti, ki % kbp)),
                pl.BlockSpec((None, tt, tn), lambda ni, ki, ti: (ni // nbb, ti, ni % nbb))]
    out_spec = pl.BlockSpec((None, tkk, tn), lambda ni, ki, ti: (ni // npj, ki, ni % npj))
    scratch = [pltpu.VMEM((tkk, tn), F32)] if nt > 1 else []
    return _call(body, name, (big_n // tn, k // tkk, nt), in_specs, out_spec,
                 _sds((n_shards, k, n), out_dtype), scratch)(a, b)


_ANY = pl.BlockSpec(memory_space=pl.ANY)
_MESH = pl.DeviceIdType.MESH


def all_gather(shards, name):
    n = len(shards)

    def body(*refs):
        ins, outs = refs[:n], refs[n:2 * n]
        send_sems, recv_sems, local_sems = refs[2 * n:]
        x, y, c = lax.axis_index("x"), lax.axis_index("y"), lax.axis_index("c")
        me, sibling = (x, y, c), (x, y, 1 - c)
        chips = [(1 - x, y), (x, 1 - y), (1 - x, 1 - y)]

        def copy(i, k, block, to, src=None):
            px, py, pc = block
            dst = outs[i].at[4 * px + 2 * py + pc]
            return pltpu.make_async_remote_copy(src_ref=dst if src is None else src, dst_ref=dst,
                                                send_sem=send_sems.at[i, k], recv_sem=recv_sems.at[i, k],
                                                device_id=to, device_id_type=_MESH)

        mine = [pltpu.make_async_copy(ins[i], outs[i].at[4 * x + 2 * y + c], local_sems.at[i]) for i in range(n)]
        for cp in mine:
            cp.start()
        first = []
        for i in range(n):
            first.append(copy(i, 0, me, sibling, src=ins[i]))
            first += [copy(i, 1 + q, me, (*chip, c), src=ins[i]) for q, chip in enumerate(chips)]
        for cp in first:
            cp.start()
        passed = []
        for q, chip in enumerate(chips):
            for i in range(n):
                copy(i, 1 + q, (*chip, c), me).wait_recv()
                cp = copy(i, 4 + q, (*chip, c), sibling)
                cp.start()
                passed.append(cp)
        for i in range(n):
            copy(i, 0, sibling, me).wait_recv()
            for q, chip in enumerate(chips):
                copy(i, 4 + q, (*chip, 1 - c), me).wait_recv()
        for cp in first + passed:
            cp.wait_send()
        for cp in mine:
            cp.wait()

    outs = pl.pallas_call(
        body, name=name, out_shape=[_sds((N_DEV,) + s.shape, s.dtype) for s in shards],
        in_specs=[_ANY] * n, out_specs=[_ANY] * n,
        scratch_shapes=[pltpu.SemaphoreType.DMA((n, 7)), pltpu.SemaphoreType.DMA((n, 7)),
                        pltpu.SemaphoreType.DMA((n,))],
    )(*shards)
    return list(outs)


def rs_core_exchange(parts, name):
    n = len(parts)

    def body(*refs):
        ins, outs = refs[:n], refs[n:2 * n]
        send_sems, recv_sems = refs[2 * n:]
        x, y, c = lax.axis_index("x"), lax.axis_index("y"), lax.axis_index("c")
        sibling = (x, y, 1 - c)
        started = []
        for i in range(n):
            for chip in range(4):
                cp = pltpu.make_async_remote_copy(src_ref=ins[i].at[2 * chip + (1 - c)], dst_ref=outs[i].at[chip],
                                                  send_sem=send_sems.at[i, chip], recv_sem=recv_sems.at[i, chip],
                                                  device_id=sibling, device_id_type=_MESH)
                cp.start()
                started.append(cp)
        for cp in started:
            cp.wait()

    outs = pl.pallas_call(
        body, name=name, out_shape=[_sds((4,) + p.shape[1:], p.dtype) for p in parts],
        in_specs=[_ANY] * n, out_specs=[_ANY] * n,
        scratch_shapes=[pltpu.SemaphoreType.DMA((n, 4)), pltpu.SemaphoreType.DMA((n, 4))],
    )(*parts)
    return list(outs)


def rs_chip_exchange(sums, name):
    n = len(sums)

    def body(*refs):
        ins, outs = refs[:n], refs[n:2 * n]
        send_sems, recv_sems = refs[2 * n:]
        x, y, c = lax.axis_index("x"), lax.axis_index("y"), lax.axis_index("c")
        peers = [(x, 1 - y), (1 - x, y), (1 - x, 1 - y)]
        started = []
        for i in range(n):
            for k, (px, py) in enumerate(peers):
                cp = pltpu.make_async_remote_copy(src_ref=ins[i].at[2 * px + py], dst_ref=outs[i].at[k],
                                                  send_sem=send_sems.at[i, k], recv_sem=recv_sems.at[i, k],
                                                  device_id=(px, py, c), device_id_type=_MESH)
                cp.start()
                started.append(cp)
        for cp in started:
            cp.wait()

    outs = pl.pallas_call(
        body, name=name, out_shape=[_sds((3,) + s.shape[1:], s.dtype) for s in sums],
        in_specs=[_ANY] * n, out_specs=[_ANY] * n,
        scratch_shapes=[pltpu.SemaphoreType.DMA((n, 3)), pltpu.SemaphoreType.DMA((n, 3))],
    )(*sums)
    return list(outs)


def _row_tile(r, c, itemsize=4, budget=1 << 20):
    cap = max(SUBLANES, (budget // (c * itemsize)) // SUBLANES * SUBLANES)
    if r <= cap:
        return r
    best = 0
    for t in range(SUBLANES, cap + 1, SUBLANES):
        if r % t == 0:
            best = t
    return best if best else r


def rs_core_add(part, recv, core, name):
    _, r, c = part.shape
    tr = _row_tile(r, c)

    def body(core_ref, p_ref, q_ref, o_ref):
        o_ref[...] = (p_ref[...].astype(F32) + q_ref[...].astype(F32)).astype(o_ref.dtype)

    in_specs = [pl.BlockSpec((None, tr, c), lambda ch, ri, core_ref: (2 * ch + core_ref[0], ri, 0)),
                pl.BlockSpec((None, tr, c), lambda ch, ri, core_ref: (ch, ri, 0))]
    out_spec = pl.BlockSpec((None, tr, c), lambda ch, ri, core_ref: (ch, ri, 0))
    return _call(body, name, (4, r // tr), in_specs, out_spec, _sds((4, r, c), part.dtype), prefetch=1)(core, part, recv)


def _adamw(w, g, m, v):
    m = ADAM_B1 * m + (1.0 - ADAM_B1) * g
    v = ADAM_B2 * v + (1.0 - ADAM_B2) * (g * g)
    m_hat = m / (1.0 - ADAM_B1 ** ADAM_STEP)
    v_hat = v / (1.0 - ADAM_B2 ** ADAM_STEP)
    delta = -ADAM_LR * (m_hat / (jnp.sqrt(v_hat) + ADAM_EPS) + ADAM_WD * w)
    return delta, m, v


def rs_finish(sums, recv, chip, name, adam=None):
    _, r, c = sums.shape
    tr = _row_tile(r, c, budget=1 << 19)

    def body(chip_ref, s_ref, q_ref, *refs):
        g = s_ref[...].astype(F32)
        for k in range(3):
            g = g + q_ref[k].astype(F32)
        if adam is None:
            refs[0][...] = g
        else:
            w_ref, m_ref, v_ref, g_ref, d_ref, nm_ref, nv_ref = refs
            g_ref[...] = g
            d_ref[...], nm_ref[...], nv_ref[...] = _adamw(w_ref[...], g, m_ref[...], v_ref[...])

    blk = pl.BlockSpec((tr, c), lambda ri, chip_ref: (ri, 0))
    in_specs = [pl.BlockSpec((None, tr, c), lambda ri, chip_ref: (chip_ref[0], ri, 0)),
                pl.BlockSpec((3, tr, c), lambda ri, chip_ref: (0, ri, 0))]
    args = [chip, sums, recv]
    if adam is None:
        out_specs, out_shape = blk, _sds((r, c), F32)
    else:
        in_specs += [blk] * 3
        args += list(adam)
        out_specs, out_shape = [blk] * 4, [_sds((r, c), F32)] * 4
    return _call(body, name, (r // tr,), in_specs, out_specs, out_shape, prefetch=1)(*args)


def adam_flat(w, g, m, v, name):
    r, c = w.shape
    tr = _row_tile(r, c, budget=1 << 19)

    def body(w_ref, g_ref, m_ref, v_ref, d_ref, nm_ref, nv_ref):
        d_ref[...], nm_ref[...], nv_ref[...] = _adamw(w_ref[...], g_ref[...], m_ref[...], v_ref[...])

    blk = pl.BlockSpec((tr, c), lambda ri: (ri, 0))
    return _call(body, name, (r // tr,), [blk] * 4, [blk] * 3, [_sds((r, c), F32)] * 3)(w, g, m, v)


def rms_fwd(x, g, name, add=None):
    t, d = x.shape
    tm = _tile(t, 256, SUBLANES)
    has_add = add is not None

    def body(*refs):
        if has_add:
            x_ref, a_ref, g_ref, h_ref, hn_ref = refs
            h = x_ref[...] + a_ref[...]
            h_ref[...] = h
        else:
            x_ref, g_ref, hn_ref = refs
            h = x_ref[...]
        r = lax.rsqrt(jnp.mean(h * h, axis=-1, keepdims=True) + NORM_EPS)
        hn_ref[...] = (h * r * g_ref[...]).astype(BF16)

    row = pl.BlockSpec((tm, d), lambda i: (i, 0))
    vec = pl.BlockSpec((1, d), lambda i: (0, 0))
    if has_add:
        return _call(body, name, (t // tm,), [row, row, vec], [row, row], [_sds((t, d), F32), _sds((t, d), BF16)])(
            x, add, g.reshape(1, d))
    return _call(body, name, (t // tm,), [row, vec], row, _sds((t, d), BF16))(x, g.reshape(1, d))


def _rms_bwd_math(x, g, dy):
    r = lax.rsqrt(jnp.mean(x * x, axis=-1, keepdims=True) + NORM_EPS)
    xh = x * r
    dyg = dy * g
    dx = r * (dyg - xh * jnp.mean(dyg * xh, axis=-1, keepdims=True))
    return dx, jnp.sum(dy * xh, axis=0, keepdims=True)


def rms_bwd(x, g, dhn, dres, name):
    t, d = x.shape
    tm = _tile(t, 256, SUBLANES)

    def body(x_ref, g_ref, dy_ref, dr_ref, dx_ref, dg_ref):
        dx, dg = _rms_bwd_math(x_ref[...], g_ref[...], dy_ref[...])
        dx_ref[...] = dx + dr_ref[...]

        @pl.when(pl.program_id(0) == 0)
        def _():
            dg_ref[...] = jnp.zeros_like(dg_ref)

        dg_ref[...] += dg

    row = pl.BlockSpec((tm, d), lambda i: (i, 0))
    vec = pl.BlockSpec((1, d), lambda i: (0, 0))
    return _call(body, name, (t // tm,), [row, vec, row, row], [row, vec], [_sds((t, d), F32), _sds((1, d), F32)])(
        x, g.reshape(1, d), dhn, dres)


def final_loss(h, add, g, target, name):
    t, d = h.shape
    tm = _tile(t, 256, SUBLANES)

    def body(x_ref, a_ref, g_ref, t_ref, loss_ref, dx_ref, dg_ref):
        x = x_ref[...] + a_ref[...]
        gain = g_ref[...]
        r = lax.rsqrt(jnp.mean(x * x, axis=-1, keepdims=True) + NORM_EPS)
        err = x * r * gain - t_ref[...]
        dx, dg = _rms_bwd_math(x, gain, err * (1.0 / d))
        dx_ref[...] = dx

        @pl.when(pl.program_id(0) == 0)
        def _():
            dg_ref[...] = jnp.zeros_like(dg_ref)
            loss_ref[...] = jnp.zeros_like(loss_ref)

        dg_ref[...] += dg
        part = jnp.sum(jnp.sum(err * err, axis=-1, keepdims=True), axis=0, keepdims=True) * (0.5 / d)
        loss_ref[...] += jnp.broadcast_to(part, loss_ref.shape)

    row = pl.BlockSpec((tm, d), lambda i: (i, 0))
    vec = pl.BlockSpec((1, d), lambda i: (0, 0))
    one = pl.BlockSpec((1, LANES), lambda i: (0, 0))
    return _call(body, name, (t // tm,), [row, row, vec, row], [one, row, vec],
                 [_sds((1, LANES), F32), _sds((t, d), F32), _sds((1, d), F32)])(h, add, g.reshape(1, d), target)


def _conv_causal(z, w_ref_rows, width):
    out = z * w_ref_rows(width - 1)
    for k in range(width - 1):
        out = out + _shift_down(z, width - 1 - k) * w_ref_rows(k)
    return out


def ffn_gate_fwd(z2, dw_w2, dw_b2, name, tc=LANES):
    _, b, s, f = z2.shape

    def body(z_ref, w_ref, b_ref, y_ref):
        g = _conv_causal(z_ref[0], lambda k: w_ref[0, k:k + 1, :], FFN_WIDTH) + b_ref[0]
        a = _conv_causal(z_ref[1], lambda k: w_ref[1, k:k + 1, :], FFN_WIDTH) + b_ref[1]
        y_ref[...] = (g * _sigmoid(g) * a).astype(BF16)

    in_specs = [pl.BlockSpec((2, None, s, tc), lambda bi, ci: (0, bi, 0, ci)),
                pl.BlockSpec((2, FFN_WIDTH, tc), lambda bi, ci: (0, 0, ci)),
                pl.BlockSpec((2, 1, tc), lambda bi, ci: (0, 0, ci))]
    out_spec = pl.BlockSpec((None, s, tc), lambda bi, ci: (bi, 0, ci))
    return _call(body, name, (b, f // tc), in_specs, out_spec, _sds((b, s, f), BF16))(z2, dw_w2, dw_b2)


def ffn_gate_bwd(z2, dy, dw_w2, dw_b2, name, tc=LANES):
    _, b, s, f = z2.shape

    def body(z_ref, dy_ref, w_ref, b_ref, dz_ref, dw_ref, db_ref):
        @pl.when(pl.program_id(1) == 0)
        def _():
            dw_ref[...] = jnp.zeros_like(dw_ref)
            db_ref[...] = jnp.zeros_like(db_ref)

        zs = (z_ref[0], z_ref[1])
        g = _conv_causal(zs[0], lambda k: w_ref[0, k:k + 1, :], FFN_WIDTH) + b_ref[0]
        a = _conv_causal(zs[1], lambda k: w_ref[1, k:k + 1, :], FFN_WIDTH) + b_ref[1]
        sg = _sigmoid(g)
        dy = dy_ref[...]
        dcs = (dy * a * (sg * (1.0 + g * (1.0 - sg))), dy * (g * sg))
        for p in range(2):
            dc = dcs[p]
            dz = dc * w_ref[p, FFN_WIDTH - 1:FFN_WIDTH, :]
            for k in range(FFN_WIDTH - 1):
                dz = dz + _shift_up(dc, FFN_WIDTH - 1 - k) * w_ref[p, k:k + 1, :]
            dz_ref[p] = dz.astype(BF16)
            for k in range(FFN_WIDTH):
                dw_ref[p, k:k + 1, :] += jnp.sum(dc * _shift_down(zs[p], FFN_WIDTH - 1 - k), axis=0, keepdims=True)
            db_ref[p] += jnp.sum(dc, axis=0, keepdims=True)

    in_specs = [pl.BlockSpec((2, None, s, tc), lambda ci, bi: (0, bi, 0, ci)),
                pl.BlockSpec((None, s, tc), lambda ci, bi: (bi, 0, ci)),
                pl.BlockSpec((2, FFN_WIDTH, tc), lambda ci, bi: (0, 0, ci)),
                pl.BlockSpec((2, 1, tc), lambda ci, bi: (0, 0, ci))]
    out_specs = [pl.BlockSpec((2, None, s, tc), lambda ci, bi: (0, bi, 0, ci)),
                 pl.BlockSpec((2, FFN_WIDTH, tc), lambda ci, bi: (0, 0, ci)),
                 pl.BlockSpec((2, 1, tc), lambda ci, bi: (0, 0, ci))]
    out_shape = [_sds((2, b, s, f), BF16), _sds((2, FFN_WIDTH, f), F32), _sds((2, 1, f), F32)]
    return _call(body, name, (f // tc, b), in_specs, out_specs, out_shape)(z2, dy, dw_w2, dw_b2)


def cc_conv_fwd(z2, dw_w, dw_b, name, tc=LANES):
    _, b, s, c = z2.shape

    def body(z_ref, w_ref, b_ref, o_ref):
        u = z_ref[0] * _sigmoid(z_ref[1])
        o_ref[...] = _conv_causal(u, lambda k: w_ref[k:k + 1, :], CC_WIDTH) + b_ref[...]

    in_specs = [pl.BlockSpec((2, None, s, tc), lambda bi, ci: (0, bi, 0, ci)),
                pl.BlockSpec((CC_WIDTH, tc), lambda bi, ci: (0, ci)),
                pl.BlockSpec((1, tc), lambda bi, ci: (0, ci))]
    out_spec = pl.BlockSpec((None, s, tc), lambda bi, ci: (bi, 0, ci))
    return _call(body, name, (b, c // tc), in_specs, out_spec, _sds((b, s, c), F32))(z2, dw_w, dw_b.reshape(1, c))


def cc_conv_bwd(z2, dcv, dw_w, name, tc=LANES):
    _, b, s, c = z2.shape

    def body(z_ref, dc_ref, w_ref, dz_ref, dbi_ref, dw_ref, db_ref):
        @pl.when(pl.program_id(1) == 0)
        def _():
            dbi_ref[...] = jnp.zeros_like(dbi_ref)
            dw_ref[...] = jnp.zeros_like(dw_ref)
            db_ref[...] = jnp.zeros_like(db_ref)

        a, gate = z_ref[0], z_ref[1]
        sg = _sigmoid(gate)
        u = a * sg
        dc = dc_ref[...]
        du = jnp.zeros_like(dc)
        for j in range(CC_WIDTH):
            sd = _shift_up(dc, j)
            k = CC_WIDTH - 1 - j
            du = du + sd * w_ref[k:k + 1, :]
            dw_ref[k:k + 1, :] += jnp.sum(sd * u, axis=0, keepdims=True)
        db_ref[...] += jnp.sum(dc, axis=0, keepdims=True)
        da = du * sg
        dg = du * a * sg * (1.0 - sg)
        dz_ref[0] = da.astype(BF16)
        dz_ref[1] = dg.astype(BF16)
        dbi_ref[0] += jnp.sum(da, axis=0, keepdims=True)
        dbi_ref[1] += jnp.sum(dg, axis=0, keepdims=True)

    in_specs = [pl.BlockSpec((2, None, s, tc), lambda ci, bi: (0, bi, 0, ci)),
                pl.BlockSpec((None, s, tc), lambda ci, bi: (bi, 0, ci)),
                pl.BlockSpec((CC_WIDTH, tc), lambda ci, bi: (0, ci))]
    out_specs = [pl.BlockSpec((2, None, s, tc), lambda ci, bi: (0, bi, 0, ci)),
                 pl.BlockSpec((2, 1, tc), lambda ci, bi: (0, 0, ci)),
                 pl.BlockSpec((CC_WIDTH, tc), lambda ci, bi: (0, ci)),
                 pl.BlockSpec((1, tc), lambda ci, bi: (0, ci))]
    out_shape = [_sds((2, b, s, c), BF16), _sds((2, 1, c), F32), _sds((CC_WIDTH, c), F32), _sds((1, c), F32)]
    return _call(body, name, (c // tc, b), in_specs, out_specs, out_shape)(z2, dcv, dw_w)


def _ln_stats(v):
    mu = jnp.mean(v, axis=-1, keepdims=True)
    vc = v - mu
    r = lax.rsqrt(jnp.mean(vc * vc, axis=-1, keepdims=True) + NORM_EPS)
    return vc * r, r


def _ln_bwd(dln, xh, r, g):
    dxh = dln * g
    dx = r * (dxh - jnp.mean(dxh, axis=-1, keepdims=True) - xh * jnp.mean(dxh * xh, axis=-1, keepdims=True))
    return dx, jnp.sum(dln * xh, axis=0, keepdims=True), jnp.sum(dln, axis=0, keepdims=True)


def cc_ln_fwd(cv, ln_g, ln_b, name):
    t, c = cv.shape
    tm = _tile(t, 256, SUBLANES)

    def body(x_ref, g_ref, b_ref, o_ref):
        xh, _ = _ln_stats(x_ref[...])
        ln = xh * g_ref[...] + b_ref[...]
        o_ref[...] = (ln * _sigmoid(ln)).astype(BF16)

    row = pl.BlockSpec((tm, c), lambda i: (i, 0))
    vec = pl.BlockSpec((1, c), lambda i: (0, 0))
    return _call(body, name, (t // tm,), [row, vec, vec], row, _sds((t, c), BF16))(
        cv, ln_g.reshape(1, c), ln_b.reshape(1, c))


def cc_ln_bwd(cv, ln_g, ln_b, ds, name):
    t, c = cv.shape
    tm = _tile(t, 256, SUBLANES)

    def body(x_ref, g_ref, b_ref, ds_ref, dx_ref, dg_ref, db_ref):
        @pl.when(pl.program_id(0) == 0)
        def _():
            dg_ref[...] = jnp.zeros_like(dg_ref)
            db_ref[...] = jnp.zeros_like(db_ref)

        xh, r = _ln_stats(x_ref[...])
        ln = xh * g_ref[...] + b_ref[...]
        sg = _sigmoid(ln)
        dln = ds_ref[...] * (sg * (1.0 + ln * (1.0 - sg)))
        dx, dg, db = _ln_bwd(dln, xh, r, g_ref[...])
        dx_ref[...] = dx
        dg_ref[...] += dg
        db_ref[...] += db

    row = pl.BlockSpec((tm, c), lambda i: (i, 0))
    vec = pl.BlockSpec((1, c), lambda i: (0, 0))
    return _call(body, name, (t // tm,), [row, vec, vec, row], [row, vec, vec],
                 [_sds((t, c), F32), _sds((1, c), F32), _sds((1, c), F32)])(
        cv, ln_g.reshape(1, c), ln_b.reshape(1, c), ds)


def col_sum(v, name):
    t, n = v.shape
    tm = _tile(t, 256, SUBLANES)

    def body(v_ref, o_ref):
        @pl.when(pl.program_id(0) == 0)
        def _():
            o_ref[...] = jnp.zeros_like(o_ref)

        o_ref[...] += jnp.sum(v_ref[...], axis=0, keepdims=True)

    return _call(body, name, (t // tm,), [pl.BlockSpec((tm, n), lambda i: (i, 0))],
                 pl.BlockSpec((1, n), lambda i: (0, 0)), _sds((1, n), F32))(v)


def rope_tables_full(seq, dim):
    pos = jnp.arange(seq, dtype=F32)
    inv = ROPE_THETA ** (-(jnp.arange(0, dim, 2, dtype=F32) / dim))
    ang = pos[:, None] * inv[None, :]
    cos, sin = jnp.cos(ang), jnp.sin(ang)
    return jnp.concatenate([cos, cos], axis=-1), jnp.concatenate([-sin, sin], axis=-1)


def dsa_rope_fwd(qkv, cos, sin, seq, name):
    t, w = qkv.shape
    tm = _tile(seq, 256, SUBLANES)
    nsb = seq // tm
    hb = DSA_HEADS

    def body(x_ref, c_ref, s_ref, o_ref):
        cos_t, sin_t = c_ref[...], s_ref[...]
        for blk in range(w // LANES):
            v = x_ref[:, blk * LANES:(blk + 1) * LANES]
            if (blk // hb) % 3 < 2:
                v = _rope(v, cos_t, sin_t)
            o_ref[:, blk * LANES:(blk + 1) * LANES] = v.astype(BF16)

    row = pl.BlockSpec((tm, w), lambda i: (i, 0))
    tab = pl.BlockSpec((tm, LANES), lambda i: (i % nsb, 0))
    return _call(body, name, (t // tm,), [row, tab, tab], row, _sds((t, w), BF16))(qkv, cos, sin)


def dsa_rope_bwd(grads, cos, sin, seq, name):
    flat = [a for grp in grads for a in grp]
    t, hw = flat[0].shape
    w = hw * len(flat)
    tm = _tile(seq, 256, SUBLANES)
    nsb = seq // tm

    def body(*refs):
        ins, (c_ref, s_ref, o_ref) = refs[:len(flat)], refs[len(flat):]
        cos_t, sin_t = c_ref[...], s_ref[...]
        for idx, ref in enumerate(ins):
            for h in range(hw // LANES):
                v = ref[:, h * LANES:(h + 1) * LANES]
                if idx % 3 < 2:
                    v = _rope_t(v, cos_t, sin_t)
                col = idx * hw + h * LANES
                o_ref[:, col:col + LANES] = v.astype(BF16)

    part = pl.BlockSpec((tm, hw), lambda i: (i, 0))
    tab = pl.BlockSpec((tm, LANES), lambda i: (i % nsb, 0))
    out = _call(body, name, (t // tm,), [part] * len(flat) + [tab, tab], pl.BlockSpec((tm, w), lambda i: (i, 0)),
                _sds((t, w), BF16))(*flat, cos, sin)
    return out[None]


def _dsa_mask(n):
    qi = lax.broadcasted_iota(jnp.int32, (DSA_BLOCK, (1 if n == 0 else 2) * DSA_BLOCK), 0)
    kj = lax.broadcasted_iota(jnp.int32, (DSA_BLOCK, (1 if n == 0 else 2) * DSA_BLOCK), 1)
    if n == 0:
        return qi >= kj
    dist = DSA_BLOCK + qi - kj
    return (dist >= 0) & (dist <= DSA_BLOCK)


def dsa_attn_fwd(qkv_b, grp, dil, batch, seq, name):
    t, w = qkv_b.shape
    ell = seq // dil
    nb = ell // DSA_BLOCK
    nw, h_n = w // LANES, DSA_HEADS
    base = grp * 3 * h_n
    scale = DSA_BLOCK ** -0.5

    def body(q_ref, k_ref, v_ref, o_ref, lse_ref):
        for n in range(nb):
            lo = 0 if n == 0 else (n - 1) * DSA_BLOCK
            hi = (n + 1) * DSA_BLOCK
            q = q_ref[n * DSA_BLOCK:hi, :]
            s = _dot_nt(q, k_ref[lo:hi, :]) * scale
            s = jnp.where(_dsa_mask(n), s, NEG_INF)
            m = jnp.max(s, axis=-1, keepdims=True)
            p = jnp.exp(s - m)
            l = jnp.sum(p, axis=-1, keepdims=True)
            o_ref[n * DSA_BLOCK:hi, :] = _dot(p.astype(BF16), v_ref[lo:hi, :]) / l
            lse_ref[n * DSA_BLOCK:hi, :] = jnp.broadcast_to(m + jnp.log(l), (DSA_BLOCK, LANES))

    def spec(off):
        return pl.BlockSpec((None, ell, LANES), lambda b, r, h: (b, 0, r * nw + base + off + h))

    out_spec = pl.BlockSpec((None, ell, LANES), lambda b, r, h: (b, 0, r * h_n + h))
    view = qkv_b.reshape(batch, ell, dil * w)
    o, lse = _call(body, name, (batch, dil, h_n), [spec(0), spec(h_n), spec(2 * h_n)], [out_spec, out_spec],
                   [_sds((batch, ell, dil * h_n * LANES), F32)] * 2)(view, view, view)
    return o.reshape(t, h_n * LANES), lse.reshape(t, h_n * LANES)


def dsa_merge(outs, lses, name):
    t, hw = outs[0].shape
    tm = _tile(t, 256, SUBLANES)
    ng = len(outs)

    def body(*refs):
        o_refs, l_refs, (o_ref, lse_ref) = refs[:ng], refs[ng:2 * ng], refs[2 * ng:]
        ls = [r[...] for r in l_refs]
        m = functools.reduce(jnp.maximum, ls)
        es = [jnp.exp(v - m) for v in ls]
        tot = functools.reduce(lambda a, b: a + b, es)
        acc = jnp.zeros_like(m)
        for e, r in zip(es, o_refs):
            acc = acc + (e / tot) * r[...]
        o_ref[...] = acc
        lse_ref[...] = m + jnp.log(tot)

    row = pl.BlockSpec((tm, hw), lambda i: (i, 0))
    return _call(body, name, (t // tm,), [row] * (2 * ng), [row, row], [_sds((t, hw), F32)] * 2)(*outs, *lses)


def head_delta(do, o, name):
    t, hw = do.shape
    tm = _tile(t, 256, SUBLANES)

    def body(do_ref, o_ref, d_ref):
        for h in range(hw // LANES):
            sl = slice(h * LANES, (h + 1) * LANES)
            d = jnp.sum(do_ref[:, sl] * o_ref[:, sl], axis=-1, keepdims=True)
            d_ref[:, sl] = jnp.broadcast_to(d, (tm, LANES))

    row = pl.BlockSpec((tm, hw), lambda i: (i, 0))
    return _call(body, name, (t // tm,), [row, row], row, _sds((t, hw), F32))(do, o)


def dsa_attn_bwd(qkv_b, do, lse, delta, grp, dil, batch, seq, name):
    t, w = qkv_b.shape
    ell = seq // dil
    nb = ell // DSA_BLOCK
    nw, h_n = w // LANES, DSA_HEADS
    base = grp * 3 * h_n
    scale = DSA_BLOCK ** -0.5

    def body(q_ref, k_ref, v_ref, do_ref, lse_ref, dl_ref, dq_ref, dk_ref, dv_ref):
        dk_ref[...] = jnp.zeros_like(dk_ref)
        dv_ref[...] = jnp.zeros_like(dv_ref)
        for n in range(nb):
            lo = 0 if n == 0 else (n - 1) * DSA_BLOCK
            hi = (n + 1) * DSA_BLOCK
            rows = slice(n * DSA_BLOCK, hi)
            q, k, v = q_ref[rows, :], k_ref[lo:hi, :], v_ref[lo:hi, :]
            dout = do_ref[rows, :].astype(BF16)
            s = jnp.where(_dsa_mask(n), _dot_nt(q, k) * scale, NEG_INF)
            p = jnp.exp(s - lse_ref[rows, 0:1])
            ds = (p * (_dot_nt(dout, v) - dl_ref[rows, 0:1]) * scale).astype(BF16)
            dq_ref[rows, :] = _dot(ds, k)
            dk_ref[lo:hi, :] += _dot_tn(ds, q)
            dv_ref[lo:hi, :] += _dot_tn(p.astype(BF16), dout)

    def spec(off):
        return pl.BlockSpec((None, ell, LANES), lambda b, r, h: (b, 0, r * nw + base + off + h))

    hspec = pl.BlockSpec((None, ell, LANES), lambda b, r, h: (b, 0, r * h_n + h))
    view = qkv_b.reshape(batch, ell, dil * w)
    hv = [a.reshape(batch, ell, dil * h_n * LANES) for a in (do, lse, delta)]
    outs = _call(body, name, (batch, dil, h_n), [spec(0), spec(h_n), spec(2 * h_n), hspec, hspec, hspec],
                 [hspec] * 3, [_sds((batch, ell, dil * h_n * LANES), F32)] * 3)(view, view, view, *hv)
    return [a.reshape(t, h_n * LANES) for a in outs]


def mla_rope_tables(seq):
    pos = jnp.arange(seq, dtype=F32)
    inv = ROPE_THETA ** (-(jnp.arange(0, MLA_ROPE, 2, dtype=F32) / MLA_ROPE))
    ang = pos[:, None] * inv[None, :]
    cos, sin, zero = jnp.cos(ang), jnp.sin(ang), jnp.zeros_like(ang)
    return jnp.concatenate([cos, zero, cos, zero], axis=-1), jnp.concatenate([-sin, zero, sin, zero], axis=-1)


def _spread_rope_cols(w_pe):
    half = MLA_ROPE // 2
    zero = jnp.zeros(w_pe.shape[:-1] + (half,), w_pe.dtype)
    return jnp.concatenate([w_pe[..., :half], zero, w_pe[..., half:], zero], axis=-1)


def _gather_rope_cols(g_pe):
    half = MLA_ROPE // 2
    return jnp.concatenate([g_pe[..., :half], g_pe[..., 2 * half:3 * half]], axis=-1)


def mla_low_fwd(c, q_norm, kv_norm, cos, sin, seq, name):
    t, w = c.shape
    rk = MLA_RANK
    tm = _tile(seq, 256, SUBLANES)
    nsb = seq // tm

    def body(c_ref, qg_ref, kg_ref, cs_ref, sn_ref, qn_ref, kn_ref, kp_ref):
        for lo, g_ref, o_ref in ((0, qg_ref, qn_ref), (rk, kg_ref, kn_ref)):
            v = c_ref[:, lo:lo + rk]
            r = lax.rsqrt(jnp.mean(v * v, axis=-1, keepdims=True) + NORM_EPS)
            o_ref[...] = (v * r * g_ref[...]).astype(BF16)
        kp_ref[...] = _rope(c_ref[:, 2 * rk:], cs_ref[...], sn_ref[...]).astype(BF16)

    row = lambda n: pl.BlockSpec((tm, n), lambda i: (i, 0))
    vec = pl.BlockSpec((1, rk), lambda i: (0, 0))
    tab = pl.BlockSpec((tm, LANES), lambda i: (i % nsb, 0))
    return _call(body, name, (t // tm,), [row(w), vec, vec, tab, tab], [row(rk), row(rk), row(LANES)],
                 [_sds((t, rk), BF16), _sds((t, rk), BF16), _sds((t, LANES), BF16)])(
        c, q_norm.reshape(1, rk), kv_norm.reshape(1, rk), cos, sin)


def mla_low_bwd(c, q_norm, kv_norm, dqn, dkn, dkp, cos, sin, seq, name):
    t, w = c.shape
    rk = MLA_RANK
    tm = _tile(seq, 256, SUBLANES)
    nsb = seq // tm

    def body(c_ref, qg_ref, kg_ref, dq_ref, dk_ref, dp_ref, cs_ref, sn_ref, dc_ref, dqg_ref, dkg_ref):
        @pl.when(pl.program_id(0) == 0)
        def _():
            dqg_ref[...] = jnp.zeros_like(dqg_ref)
            dkg_ref[...] = jnp.zeros_like(dkg_ref)

        for lo, g_ref, d_ref, dg_ref in ((0, qg_ref, dq_ref, dqg_ref), (rk, kg_ref, dk_ref, dkg_ref)):
            dx, dg = _rms_bwd_math(c_ref[:, lo:lo + rk], g_ref[...], d_ref[...])
            dc_ref[:, lo:lo + rk] = dx.astype(BF16)
            dg_ref[...] += dg
        dc_ref[:, 2 * rk:] = _rope_t(dp_ref[...], cs_ref[...], sn_ref[...]).astype(BF16)

    row = lambda n: pl.BlockSpec((tm, n), lambda i: (i, 0))
    vec = pl.BlockSpec((1, rk), lambda i: (0, 0))
    tab = pl.BlockSpec((tm, LANES), lambda i: (i % nsb, 0))
    dc, dqg, dkg = _call(body, name, (t // tm,), [row(w), vec, vec, row(rk), row(rk), row(LANES), tab, tab],
                         [row(w), vec, vec], [_sds((t, w), BF16), _sds((1, rk), F32), _sds((1, rk), F32)])(
        c, q_norm.reshape(1, rk), kv_norm.reshape(1, rk), dqn, dkn, dkp, cos, sin)
    return dc[None], dqg, dkg


def mla_rope_cast(q, kv, cos, sin, seq, name, transpose=False):
    t, w = q.shape
    tm = _tile(seq, 256, SUBLANES)
    nsb = seq // tm
    fn = _rope_t if transpose else _rope

    def body(q_ref, kv_ref, cs_ref, sn_ref, qo_ref, kvo_ref):
        cos_t, sin_t = cs_ref[...], sn_ref[...]
        for blk in range(w // LANES):
            sl = slice(blk * LANES, (blk + 1) * LANES)
            v = q_ref[:, sl]
            if blk % 2 == 1:
                v = fn(v, cos_t, sin_t)
            qo_ref[:, sl] = v.astype(BF16)
        kvo_ref[...] = kv_ref[...].astype(BF16)

    row = pl.BlockSpec((tm, w), lambda i: (i, 0))
    tab = pl.BlockSpec((tm, LANES), lambda i: (i % nsb, 0))
    return _call(body, name, (t // tm,), [row, row, tab, tab], [row, row], [_sds((t, w), BF16)] * 2)(q, kv, cos, sin)


def _causal_mask(i, j, tq, tk):
    qpos = i * tq + lax.broadcasted_iota(jnp.int32, (tq, tk), 0)
    kpos = j * tk + lax.broadcasted_iota(jnp.int32, (tq, tk), 1)
    return kpos <= qpos


def mla_attn_fwd(q_b, kv_b, kp_b, batch, seq, name, tq=256):
    t, w = q_b.shape
    h_n = w // (2 * LANES)
    tq = _tile(seq, tq, SUBLANES)
    nq = seq // tq
    scale = (MLA_NOPE + MLA_ROPE) ** -0.5

    def body(q_ref, kv_ref, kp_ref, o_ref, lse_ref):
        i = pl.program_id(2)
        qn, qp = q_ref[:, :LANES], q_ref[:, LANES:]

        def step(j, carry):
            m, l, acc = carry
            rows = pl.ds(pl.multiple_of(j * tq, tq), tq)
            s = (_dot_nt(qn, kv_ref[rows, :LANES]) + _dot_nt(qp, kp_ref[rows, :])) * scale
            s = jnp.where(_causal_mask(i, j, tq, tq), s, NEG_INF)
            mn = jnp.maximum(m, jnp.max(s, axis=-1, keepdims=True))
            p = jnp.exp(s - mn)
            a = jnp.exp(m - mn)
            return mn, a * l + jnp.sum(p, axis=-1, keepdims=True), a * acc + _dot(p.astype(BF16), kv_ref[rows, LANES:])

        init = (jnp.full((tq, 1), NEG_INF, F32), jnp.zeros((tq, 1), F32), jnp.zeros((tq, LANES), F32))
        m, l, acc = lax.fori_loop(0, i + 1, step, init)
        o_ref[...] = acc / l
        lse_ref[...] = jnp.broadcast_to(m + jnp.log(l), (tq, LANES))

    in_specs = [pl.BlockSpec((None, tq, 2 * LANES), lambda b, h, i: (b, i, h)),
                pl.BlockSpec((None, seq, 2 * LANES), lambda b, h, i: (b, 0, h)),
                pl.BlockSpec((None, seq, LANES), lambda b, h, i: (b, 0, 0))]
    out_spec = pl.BlockSpec((None, tq, LANES), lambda b, h, i: (b, i, h))
    o, lse = _call(body, name, (batch, h_n, nq), in_specs, [out_spec, out_spec],
                   [_sds((batch, seq, h_n * LANES), F32)] * 2)(
        q_b.reshape(batch, seq, w), kv_b.reshape(batch, seq, w), kp_b.reshape(batch, seq, LANES))
    return o.reshape(t, h_n * LANES), lse.reshape(t, h_n * LANES)


def mla_attn_bwd(q_b, kv_b, kp_b, do, lse, delta, batch, seq, name, tq=256):
    t, w = q_b.shape
    h_n = w // (2 * LANES)
    tq = _tile(seq, tq, SUBLANES)
    nq = seq // tq
    scale = (MLA_NOPE + MLA_ROPE) ** -0.5

    def body(q_ref, kv_ref, kp_ref, do_ref, lse_ref, dl_ref, dq_ref, dkv_ref, dkp_ref):
        dq_ref[...] = jnp.zeros_like(dq_ref)
        dkv_ref[...] = jnp.zeros_like(dkv_ref)

        @pl.when(pl.program_id(1) == 0)
        def _():
            dkp_ref[...] = jnp.zeros_like(dkp_ref)

        def outer(j, _):
            krows = pl.ds(pl.multiple_of(j * tq, tq), tq)
            kn, v, kp = kv_ref[krows, :LANES], kv_ref[krows, LANES:], kp_ref[krows, :]

            def inner(i, _):
                qrows = pl.ds(pl.multiple_of(i * tq, tq), tq)
                qn, qp = q_ref[qrows, :LANES], q_ref[qrows, LANES:]
                dout = do_ref[qrows, :].astype(BF16)
                s = (_dot_nt(qn, kn) + _dot_nt(qp, kp)) * scale
                s = jnp.where(_causal_mask(i, j, tq, tq), s, NEG_INF)
                p = jnp.exp(s - lse_ref[qrows, 0:1])
                ds = (p * (_dot_nt(dout, v) - dl_ref[qrows, 0:1]) * scale).astype(BF16)
                dq_ref[qrows, :LANES] += _dot(ds, kn)
                dq_ref[qrows, LANES:] += _dot(ds, kp)
                dkv_ref[krows, :LANES] += _dot_tn(ds, qn)
                dkv_ref[krows, LANES:] += _dot_tn(p.astype(BF16), dout)
                dkp_ref[krows, :] += _dot_tn(ds, qp)
                return 0

            lax.fori_loop(j, nq, inner, 0)
            return 0

        lax.fori_loop(0, nq, outer, 0)

    wide = pl.BlockSpec((None, seq, 2 * LANES), lambda b, h: (b, 0, h))
    head = pl.BlockSpec((None, seq, LANES), lambda b, h: (b, 0, h))
    shared = pl.BlockSpec((None, seq, LANES), lambda b, h: (b, 0, 0))
    hv = [a.reshape(batch, seq, h_n * LANES) for a in (do, lse, delta)]
    dq, dkv, dkp = _call(body, name, (batch, h_n), [wide, wide, shared, head, head, head], [wide, wide, shared],
                         [_sds((batch, seq, w), F32), _sds((batch, seq, w), F32), _sds((batch, seq, LANES), F32)])(
        q_b.reshape(batch, seq, w), kv_b.reshape(batch, seq, w), kp_b.reshape(batch, seq, LANES), *hv)
    return dq.reshape(t, w), dkv.reshape(t, w), dkp.reshape(t, LANES)


def _sgu_common(z_ref, g_ref, b_ref, ws_ref, bs_ref):
    e = z_ref.shape[-1]
    ge = e // SG_GROUPS
    zu, zv = z_ref[0], z_ref[1]
    u = _gelu(zu)
    xh, r = _ln_stats(_gelu(zv))
    vn = (xh * g_ref[...] + b_ref[...]).astype(BF16)
    tri = (lax.broadcasted_iota(jnp.int32, (SG_CHUNK, SG_CHUNK), 0)
           >= lax.broadcasted_iota(jnp.int32, (SG_CHUNK, SG_CHUNK), 1))
    ws = [jnp.where(tri, ws_ref[g], 0.0).astype(BF16) for g in range(SG_GROUPS)]
    v2 = [_dot(ws[g], vn[:, g * ge:(g + 1) * ge]) + bs_ref[:, g:g + 1] for g in range(SG_GROUPS)]
    return zu, zv, u, xh, r, vn, tri, ws, v2


def sgu_fwd(z2, ln_g, ln_b, w_s, b_s_t, name):
    _, t, e = z2.shape
    ge = e // SG_GROUPS

    def body(z_ref, g_ref, b_ref, ws_ref, bs_ref, y_ref):
        _, _, u, _, _, _, _, _, v2 = _sgu_common(z_ref, g_ref, b_ref, ws_ref, bs_ref)
        for g in range(SG_GROUPS):
            y_ref[:, g * ge:(g + 1) * ge] = (u[:, g * ge:(g + 1) * ge] * v2[g]).astype(BF16)

    vec = pl.BlockSpec((1, e), lambda i: (0, 0))
    in_specs = [pl.BlockSpec((2, SG_CHUNK, e), lambda i: (0, i, 0)), vec, vec,
                pl.BlockSpec((SG_GROUPS, SG_CHUNK, SG_CHUNK), lambda i: (0, 0, 0)),
                pl.BlockSpec((SG_CHUNK, SG_GROUPS), lambda i: (0, 0))]
    return _call(body, name, (t // SG_CHUNK,), in_specs, pl.BlockSpec((SG_CHUNK, e), lambda i: (i, 0)),
                 _sds((t, e), BF16))(z2, ln_g.reshape(1, e), ln_b.reshape(1, e), w_s, b_s_t)


def sgu_bwd(z2, dy, ln_g, ln_b, w_s, b_s_t, name):
    _, t, e = z2.shape
    ge = e // SG_GROUPS

    def body(z_ref, dy_ref, g_ref, b_ref, ws_ref, bs_ref, dz_ref, dbi_ref, dg_ref, db_ref, dws_ref, dbs_ref):
        @pl.when(pl.program_id(0) == 0)
        def _():
            for ref in (dbi_ref, dg_ref, db_ref, dws_ref, dbs_ref):
                ref[...] = jnp.zeros_like(ref)

        zu, zv, u, xh, r, vn, tri, ws, v2 = _sgu_common(z_ref, g_ref, b_ref, ws_ref, bs_ref)
        dy = dy_ref[...]
        dvn = []
        for g in range(SG_GROUPS):
            sl = slice(g * ge, (g + 1) * ge)
            dyg = dy[:, sl]
            du = dyg * v2[g] * _gelu_grad(zu[:, sl])
            dz_ref[0, :, sl] = du.astype(BF16)
            dbi_ref[0, :, sl] += jnp.sum(du, axis=0, keepdims=True)
            dv2 = dyg * u[:, sl]
            dbs_ref[:, g:g + 1] += jnp.sum(dv2, axis=-1, keepdims=True)
            dv2b = dv2.astype(BF16)
            dws_ref[g] += jnp.where(tri, _dot_nt(dv2b, vn[:, sl]), 0.0)
            dvn.append(_dot_tn(ws[g], dv2b))
        dvn = jnp.concatenate(dvn, axis=-1)
        dv, dg, db = _ln_bwd(dvn, xh, r, g_ref[...])
        dzv = dv * _gelu_grad(zv)
        dz_ref[1] = dzv.astype(BF16)
        dbi_ref[1] += jnp.sum(dzv, axis=0, keepdims=True)
        dg_ref[...] += dg
        db_ref[...] += db

    vec = pl.BlockSpec((1, e), lambda i: (0, 0))
    wsb = pl.BlockSpec((SG_GROUPS, SG_CHUNK, SG_CHUNK), lambda i: (0, 0, 0))
    bsb = pl.BlockSpec((SG_CHUNK, SG_GROUPS), lambda i: (0, 0))
    zblk = pl.BlockSpec((2, SG_CHUNK, e), lambda i: (0, i, 0))
    in_specs = [zblk, pl.BlockSpec((SG_CHUNK, e), lambda i: (i, 0)), vec, vec, wsb, bsb]
    out_specs = [zblk, pl.BlockSpec((2, 1, e), lambda i: (0, 0, 0)), vec, vec, wsb, bsb]
    out_shape = [_sds((2, t, e), BF16), _sds((2, 1, e), F32), _sds((1, e), F32), _sds((1, e), F32),
                 _sds((SG_GROUPS, SG_CHUNK, SG_CHUNK), F32), _sds((SG_CHUNK, SG_GROUPS), F32)]
    return _call(body, name, (t // SG_CHUNK,), in_specs, out_specs, out_shape)(
        z2, dy, ln_g.reshape(1, e), ln_b.reshape(1, e), w_s, b_s_t)


_COL = ("cc_w_in", "ffn_w_up", "dsa_w_qkv", "dsa_w_o", "mla_w_qb", "mla_w_kvb", "sg_w_in")
_ROW = ("cc_w_out", "ffn_w_down", "mla_w_in", "mla_w_o", "sg_w_out")
_SMALL_COL = ("cc_dw_w", "ffn_dw_w")
_MIXERS = ("cc", "dsa", "mla", "sg")


def _base(name):
    return name.split("_", 1)[1] if name[0] == "l" and name[1].isdigit() else name


def _ffn_fwd(pre, p, full, h_prev, mix_out, bsz, seq):
    t, d = h_prev.shape
    h, hn = rms_fwd(h_prev, p[pre + "norm_ffn"], pre + "ffn_norm", add=mix_out)
    w_up, w_down = full[pre + "ffn_w_up"], full[pre + "ffn_w_down"]
    f = w_down.shape[1]
    z2r = mm_nn(hn[None], w_up, pre + "ffn_up", out_parts=2).reshape(2, bsz, seq, f)
    dw_w2 = jnp.transpose(full[pre + "ffn_dw_w"].reshape(FFN_WIDTH, 2, f), (1, 0, 2))
    dw_b2 = p[pre + "ffn_dw_b"].reshape(2, 1, f)
    y = ffn_gate_fwd(z2r, dw_w2, dw_b2, pre + "ffn_gate").reshape(1, t, f)
    out = mm_nn(y, w_down, pre + "ffn_down")[0]
    return out, dict(h=h, hn=hn, z2r=z2r, y=y, dw_w2=dw_w2, dw_b2=dw_b2)


def _ffn_bwd(pre, p, full, ctx, dh_out, big_g, small_g, bsz, seq):
    t, d = dh_out.shape
    w_up, w_down = full[pre + "ffn_w_up"], full[pre + "ffn_w_down"]
    f = w_down.shape[1]
    dh3 = dh_out[None]
    dyv = mm_nt(dh3, w_down, pre + "ffn_down_dx")
    big_g[pre + "ffn_w_down"] = mm_tn(ctx["y"], dh3, 1, pre + "ffn_down_dw").reshape(N_DEV, f // N_DEV, d)
    dz2, ddw2, ddb2 = ffn_gate_bwd(ctx["z2r"], dyv.reshape(bsz, seq, f), ctx["dw_w2"], ctx["dw_b2"],
                                   pre + "ffn_gate_bwd")
    dz2 = dz2.reshape(2, t, f)
    big_g[pre + "ffn_w_up"] = mm_tn(ctx["hn"][None], dz2, N_DEV, pre + "ffn_up_dw")
    dhn = mm_nt(dz2, w_up, pre + "ffn_up_dx")
    small_g[pre + "ffn_dw_w"] = jnp.transpose(ddw2, (1, 0, 2)).reshape(FFN_WIDTH, 2 * f)
    small_g[pre + "ffn_dw_b"] = ddb2.reshape(2 * f)
    dh, dg = rms_bwd(ctx["h"], p[pre + "norm_ffn"], dhn, dh_out, pre + "ffn_norm_bwd")
    small_g[pre + "norm_ffn"] = dg.reshape(d)
    return dh


def _cc_fwd(pre, p, full, hn, bsz, seq, aux):
    t, c = hn.shape
    z2r = mm_nn(hn[None], full[pre + "cc_w_in"], pre + "cc_in", bias=p[pre + "cc_b_in"], out_parts=2).reshape(
        2, bsz, seq, c)
    cv = cc_conv_fwd(z2r, full[pre + "cc_dw_w"], p[pre + "cc_dw_b"], pre + "cc_conv").reshape(t, c)
    s = cc_ln_fwd(cv, p[pre + "cc_ln_g"], p[pre + "cc_ln_b"], pre + "cc_ln")
    out = mm_nn(s[None], full[pre + "cc_w_out"], pre + "cc_out", bias=p[pre + "cc_b_out"])[0]
    return out, dict(z2r=z2r, cv=cv, s=s)


def _cc_bwd(pre, p, full, ctx, hn, dm, big_g, small_g, bsz, seq, aux):
    t, c = dm.shape
    dm3 = dm[None]
    small_g[pre + "cc_b_out"] = col_sum(dm, pre + "cc_bout_g").reshape(c)
    ds = mm_nt(dm3, full[pre + "cc_w_out"], pre + "cc_out_dx")
    big_g[pre + "cc_w_out"] = mm_tn(ctx["s"][None], dm3, 1, pre + "cc_out_dw").reshape(N_DEV, c // N_DEV, c)
    dcv, dlg, dlb = cc_ln_bwd(ctx["cv"], p[pre + "cc_ln_g"], p[pre + "cc_ln_b"], ds, pre + "cc_ln_bwd")
    dz2, dbi, ddw, ddb = cc_conv_bwd(ctx["z2r"], dcv.reshape(bsz, seq, c), full[pre + "cc_dw_w"], pre + "cc_conv_bwd")
    dz2 = dz2.reshape(2, t, c)
    big_g[pre + "cc_w_in"] = mm_tn(hn[None], dz2, N_DEV, pre + "cc_in_dw")
    small_g[pre + "cc_ln_g"], small_g[pre + "cc_ln_b"] = dlg.reshape(c), dlb.reshape(c)
    small_g[pre + "cc_b_in"], small_g[pre + "cc_dw_w"], small_g[pre + "cc_dw_b"] = dbi.reshape(2 * c), ddw, ddb.reshape(c)
    return mm_nt(dz2, full[pre + "cc_w_in"], pre + "cc_in_dx")


def _dsa_fwd(pre, p, full, hn, bsz, seq, aux):
    cos, sin = aux["dsa_tables"]
    qkv = mm_nn(hn[None], full[pre + "dsa_w_qkv"], pre + "dsa_qkv")[0]
    qkv_b = dsa_rope_fwd(qkv, cos, sin, seq, pre + "dsa_rope")
    outs, lses = [], []
    for grp, (window, dil) in enumerate(DSA_CONFIGS):
        assert window // dil == DSA_BLOCK and (seq // dil) % DSA_BLOCK == 0
        o, lse = dsa_attn_fwd(qkv_b, grp, dil, bsz, seq, pre + "dsa_attn%d" % grp)
        outs.append(o)
        lses.append(lse)
    o, lse = dsa_merge(outs, lses, pre + "dsa_merge")
    out = mm_nn(o[None], full[pre + "dsa_w_o"], pre + "dsa_o")[0]
    return out, dict(qkv_b=qkv_b, o=o, lse=lse)


def _dsa_bwd(pre, p, full, ctx, hn, dm, big_g, small_g, bsz, seq, aux):
    cos, sin = aux["dsa_tables"]
    dm3 = dm[None]
    do = mm_nt(dm3, full[pre + "dsa_w_o"], pre + "dsa_o_dx")
    big_g[pre + "dsa_w_o"] = mm_tn(ctx["o"][None], dm3, N_DEV, pre + "dsa_o_dw")
    delta = head_delta(do, ctx["o"], pre + "dsa_delta")
    grads = [dsa_attn_bwd(ctx["qkv_b"], do, ctx["lse"], delta, grp, dil, bsz, seq, pre + "dsa_attn_bwd%d" % grp)
             for grp, (_, dil) in enumerate(DSA_CONFIGS)]
    dqkv = dsa_rope_bwd(grads, cos, sin, seq, pre + "dsa_rope_bwd")
    big_g[pre + "dsa_w_qkv"] = mm_tn(hn[None], dqkv, N_DEV, pre + "dsa_qkv_dw")
    return mm_nt(dqkv, full[pre + "dsa_w_qkv"], pre + "dsa_qkv_dx")


def _mla_weights(pre, full):
    rk, hd = MLA_RANK, MLA_NOPE + MLA_ROPE
    w_in = full[pre + "mla_w_in"][0]
    w_in_p = jnp.concatenate([w_in[:, :2 * rk], _spread_rope_cols(w_in[:, 2 * rk:])], axis=1)[None]
    w_qb = jnp.transpose(full[pre + "mla_w_qb"], (1, 0, 2)).reshape(rk, MLA_HEADS, hd)
    w_qb_p = jnp.concatenate([w_qb[..., :MLA_NOPE], _spread_rope_cols(w_qb[..., MLA_NOPE:])], axis=-1)
    return w_in_p, w_qb_p.reshape(1, rk, MLA_HEADS * 2 * LANES)


def _mla_fwd(pre, p, full, hn, bsz, seq, aux):
    cos, sin = aux["mla_tables"]
    w_in_p, w_qb_p = _mla_weights(pre, full)
    c = mm_nn(hn[None], w_in_p, pre + "mla_in")[0]
    qn, kn, kp = mla_low_fwd(c, p[pre + "mla_q_norm"], p[pre + "mla_kv_norm"], cos, sin, seq, pre + "mla_low")
    q = mm_nn(qn[None], w_qb_p, pre + "mla_qb")[0]
    kv = mm_nn(kn[None], full[pre + "mla_w_kvb"], pre + "mla_kvb")[0]
    q_b, kv_b = mla_rope_cast(q, kv, cos, sin, seq, pre + "mla_rope")
    o, lse = mla_attn_fwd(q_b, kv_b, kp, bsz, seq, pre + "mla_attn")
    out = mm_nn(o[None], full[pre + "mla_w_o"], pre + "mla_o")[0]
    return out, dict(c=c, qn=qn, kn=kn, kp=kp, q_b=q_b, kv_b=kv_b, o=o, lse=lse, w_in_p=w_in_p, w_qb_p=w_qb_p)


def _mla_bwd(pre, p, full, ctx, hn, dm, big_g, small_g, bsz, seq, aux):
    cos, sin = aux["mla_tables"]
    t, d = dm.shape
    rk = MLA_RANK
    dm3 = dm[None]
    do = mm_nt(dm3, full[pre + "mla_w_o"], pre + "mla_o_dx")
    big_g[pre + "mla_w_o"] = mm_tn(ctx["o"][None], dm3, 1, pre + "mla_o_dw").reshape(N_DEV, -1, d)
    delta = head_delta(do, ctx["o"], pre + "mla_delta")
    dq, dkv, dkp = mla_attn_bwd(ctx["q_b"], ctx["kv_b"], ctx["kp"], do, ctx["lse"], delta, bsz, seq, pre + "mla_attn_bwd")
    dq_b, dkv_b = mla_rope_cast(dq, dkv, cos, sin, seq, pre + "mla_rope_bwd", transpose=True)
    g_qb = mm_tn(ctx["qn"][None], dq_b[None], 1, pre + "mla_qb_dw")[0].reshape(rk, MLA_HEADS, 2 * LANES)
    g_qb = jnp.concatenate([g_qb[..., :MLA_NOPE], _gather_rope_cols(g_qb[..., MLA_NOPE:])], axis=-1)
    big_g[pre + "mla_w_qb"] = jnp.transpose(g_qb.reshape(rk, N_DEV, -1), (1, 0, 2))
    dqn = mm_nt(dq_b[None], ctx["w_qb_p"], pre + "mla_qb_dx")
    big_g[pre + "mla_w_kvb"] = mm_tn(ctx["kn"][None], dkv_b[None], N_DEV, pre + "mla_kvb_dw")
    dkn = mm_nt(dkv_b[None], full[pre + "mla_w_kvb"], pre + "mla_kvb_dx")
    dc, dqg, dkg = mla_low_bwd(ctx["c"], p[pre + "mla_q_norm"], p[pre + "mla_kv_norm"], dqn, dkn, dkp, cos, sin, seq,
                               pre + "mla_low_bwd")
    small_g[pre + "mla_q_norm"], small_g[pre + "mla_kv_norm"] = dqg.reshape(rk), dkg.reshape(rk)
    g_in = mm_tn(hn[None], dc, 1, pre + "mla_in_dw")[0]
    g_in = jnp.concatenate([g_in[:, :2 * rk], _gather_rope_cols(g_in[:, 2 * rk:])], axis=1)
    big_g[pre + "mla_w_in"] = g_in.reshape(N_DEV, d // N_DEV, -1)
    return mm_nt(dc, ctx["w_in_p"], pre + "mla_in_dx")


def _sg_fwd(pre, p, full, hn, bsz, seq, aux):
    z2 = mm_nn(hn[None], full[pre + "sg_w_in"], pre + "sg_in", bias=p[pre + "sg_b_in"], out_parts=2)
    b_s_t = jnp.transpose(p[pre + "sg_b_s"])
    y = sgu_fwd(z2, p[pre + "sg_ln_g"], p[pre + "sg_ln_b"], p[pre + "sg_w_s"], b_s_t, pre + "sg_mix")
    out = mm_nn(y[None], full[pre + "sg_w_out"], pre + "sg_out", bias=p[pre + "sg_b_out"])[0]
    return out, dict(z2=z2, y=y, b_s_t=b_s_t)


def _sg_bwd(pre, p, full, ctx, hn, dm, big_g, small_g, bsz, seq, aux):
    t, d = dm.shape
    dm3 = dm[None]
    small_g[pre + "sg_b_out"] = col_sum(dm, pre + "sg_bout_g").reshape(d)
    dy = mm_nt(dm3, full[pre + "sg_w_out"], pre + "sg_out_dx")
    big_g[pre + "sg_w_out"] = mm_tn(ctx["y"][None], dm3, 1, pre + "sg_out_dw").reshape(N_DEV, -1, d)
    dz2, dbi, dlg, dlb, dws, dbs_t = sgu_bwd(ctx["z2"], dy, p[pre + "sg_ln_g"], p[pre + "sg_ln_b"], p[pre + "sg_w_s"],
                                             ctx["b_s_t"], pre + "sg_mix_bwd")
    big_g[pre + "sg_w_in"] = mm_tn(hn[None], dz2, N_DEV, pre + "sg_in_dw")
    small_g[pre + "sg_b_in"] = dbi.reshape(-1)
    small_g[pre + "sg_ln_g"], small_g[pre + "sg_ln_b"] = dlg.reshape(-1), dlb.reshape(-1)
    small_g[pre + "sg_w_s"], small_g[pre + "sg_b_s"] = dws, jnp.transpose(dbs_t)
    return mm_nt(dz2, full[pre + "sg_w_in"], pre + "sg_in_dx")


_MIX_FWD = dict(cc=_cc_fwd, dsa=_dsa_fwd, mla=_mla_fwd, sg=_sg_fwd)
_MIX_BWD = dict(cc=_cc_bwd, dsa=_dsa_bwd, mla=_mla_bwd, sg=_sg_bwd)


def _reduce_scatter(parts, core, chip, tag, adams=None):
    recv = rs_core_exchange(parts, tag + "_cores")
    sums = [rs_core_add(pt, rc, core, "%s_add%d" % (tag, i)) for i, (pt, rc) in enumerate(zip(parts, recv))]
    recv2 = rs_chip_exchange(sums, tag + "_chips")
    return [rs_finish(s, r2, chip, "%s_fin%d" % (tag, i), adam=None if adams is None else adams[i])
            for i, (s, r2) in enumerate(zip(sums, recv2))]


def _pack(arrays):
    flat = jnp.concatenate([a.reshape(-1) for a in arrays])
    unit = N_DEV * SUBLANES * LANES
    padded = -(-flat.shape[0] // unit) * unit
    return jnp.pad(flat, (0, padded - flat.shape[0]))


def _unpack(flat, like):
    out, pos = [], 0
    for a in like:
        out.append(flat[pos:pos + a.size].reshape(a.shape))
        pos += a.size
    return out


def _train_step(p):
    names = list(p)
    wnames = names[1:names.index("loss_target")]
    x = p["x"]
    bsz, seq, d = x.shape
    t = bsz * seq
    xi, yi, ci = lax.axis_index("x"), lax.axis_index("y"), lax.axis_index("c")
    core = jnp.reshape(ci, (1,)).astype(jnp.int32)
    chip = jnp.reshape(2 * xi + yi, (1,)).astype(jnp.int32)
    dev = 4 * xi + 2 * yi + ci
    n_layers = 1 + max(int(n[1]) for n in wnames if n[0] == "l" and n[1].isdigit())
    big = [n for n in wnames if _base(n) in _COL + _ROW]
    aux = dict(dsa_tables=rope_tables_full(seq, LANES), mla_tables=mla_rope_tables(seq))

    full = {}
    for layer in range(n_layers):
        pre = "l%d_" % layer
        grp = [n for n in wnames if n.startswith(pre) and _base(n) in _COL + _ROW + _SMALL_COL]
        shards = [p[n] if _base(n) in _SMALL_COL else p[n].astype(BF16) for n in grp]
        for n, g in zip(grp, all_gather(shards, "ag_l%d" % layer)):
            if _base(n) in _COL:
                full[n] = g
            elif _base(n) in _ROW:
                full[n] = g.reshape(1, N_DEV * g.shape[1], g.shape[2])
            else:
                full[n] = jnp.transpose(g, (1, 0, 2)).reshape(g.shape[1], N_DEV * g.shape[2])

    h = x.reshape(t, d)
    ffn_out = None
    ctxs = []
    for layer in range(n_layers):
        pre = "l%d_" % layer
        kind = _MIXERS[layer % len(_MIXERS)]
        if ffn_out is None:
            hn = rms_fwd(h, p[pre + "norm_mix"], pre + "mix_norm")
        else:
            h, hn = rms_fwd(h, p[pre + "norm_mix"], pre + "mix_norm", add=ffn_out)
        mix_out, mctx = _MIX_FWD[kind](pre, p, full, hn, bsz, seq, aux)
        ffn_out, fctx = _ffn_fwd(pre, p, full, h, mix_out, bsz, seq)
        ctxs.append((h, hn, mctx, fctx))
        h = fctx["h"]

    loss_row, dh, dfin = final_loss(h, ffn_out, p["final_norm"], p["loss_target"].reshape(t, d), "final_loss")
    loss = lax.psum(loss_row[0, 0], ("x", "y", "c"))

    small_g = {"final_norm": dfin.reshape(d)}
    results = {}
    for layer in reversed(range(n_layers)):
        pre = "l%d_" % layer
        kind = _MIXERS[layer % len(_MIXERS)]
        h_in, hn, mctx, fctx = ctxs[layer]
        big_g = {}
        dh = _ffn_bwd(pre, p, full, fctx, dh, big_g, small_g, bsz, seq)
        dhn = _MIX_BWD[kind](pre, p, full, mctx, hn, dh, big_g, small_g, bsz, seq, aux)
        dh, dg = rms_bwd(h_in, p[pre + "norm_mix"], dhn, dh, pre + "mix_norm_bwd")
        small_g[pre + "norm_mix"] = dg.reshape(d)
        lnames = [n for n in big if n.startswith(pre)]
        outs = _reduce_scatter([big_g[n] for n in lnames], core, chip, "rs_l%d" % layer,
                               adams=[(p[n], p["m_" + n], p["v_" + n]) for n in lnames])
        for n, o in zip(lnames, outs):
            results[n] = o

    small = [n for n in wnames if n not in big]
    packed = _pack([small_g[n] for n in small]).reshape(N_DEV, -1, LANES)
    reduced = _reduce_scatter([packed], core, chip, "rs_small")[0]
    gathered = all_gather([reduced], "ag_small")[0].reshape(-1)
    g_local = []
    for n, g in zip(small, _unpack(gathered, [small_g[n] for n in small])):
        if _base(n) in _SMALL_COL:
            width = p[n].shape[1]
            g = lax.dynamic_slice_in_dim(g, dev * width, width, axis=1)
        g_local.append(g)
    stacked = [_pack(v).reshape(-1, LANES) for v in
               ([p[n] for n in small], g_local, [p["m_" + n] for n in small], [p["v_" + n] for n in small])]
    upd = adam_flat(*stacked, "adam_small")
    upd = [_unpack(u.reshape(-1), g_local) for u in upd]
    for i, n in enumerate(small):
        results[n] = (g_local[i], upd[0][i], upd[1][i], upd[2][i])

    grad_x = dh.reshape(bsz, seq, d)
    cols = [[results[n][k] for n in wnames] for k in range(4)]
    return (loss, grad_x, *cols[0], *cols[1], *cols[2], *cols[3])


def kernel(x, l0_norm_mix, l0_cc_w_in, l0_cc_b_in, l0_cc_dw_w, l0_cc_dw_b, l0_cc_ln_g, l0_cc_ln_b, l0_cc_w_out, l0_cc_b_out, l0_norm_ffn, l0_ffn_w_up, l0_ffn_dw_w, l0_ffn_dw_b, l0_ffn_w_down, l1_norm_mix, l1_dsa_w_qkv, l1_dsa_w_o, l1_norm_ffn, l1_ffn_w_up, l1_ffn_dw_w, l1_ffn_dw_b, l1_ffn_w_down, l2_norm_mix, l2_mla_w_in, l2_mla_q_norm, l2_mla_w_qb, l2_mla_kv_norm, l2_mla_w_kvb, l2_mla_w_o, l2_norm_ffn, l2_ffn_w_up, l2_ffn_dw_w, l2_ffn_dw_b, l2_ffn_w_down, l3_norm_mix, l3_sg_w_in, l3_sg_b_in, l3_sg_ln_g, l3_sg_ln_b, l3_sg_w_s, l3_sg_b_s, l3_sg_w_out, l3_sg_b_out, l3_norm_ffn, l3_ffn_w_up, l3_ffn_dw_w, l3_ffn_dw_b, l3_ffn_w_down, final_norm, loss_target, m_l0_norm_mix, m_l0_cc_w_in, m_l0_cc_b_in, m_l0_cc_dw_w, m_l0_cc_dw_b, m_l0_cc_ln_g, m_l0_cc_ln_b, m_l0_cc_w_out, m_l0_cc_b_out, m_l0_norm_ffn, m_l0_ffn_w_up, m_l0_ffn_dw_w, m_l0_ffn_dw_b, m_l0_ffn_w_down, m_l1_norm_mix, m_l1_dsa_w_qkv, m_l1_dsa_w_o, m_l1_norm_ffn, m_l1_ffn_w_up, m_l1_ffn_dw_w, m_l1_ffn_dw_b, m_l1_ffn_w_down, m_l2_norm_mix, m_l2_mla_w_in, m_l2_mla_q_norm, m_l2_mla_w_qb, m_l2_mla_kv_norm, m_l2_mla_w_kvb, m_l2_mla_w_o, m_l2_norm_ffn, m_l2_ffn_w_up, m_l2_ffn_dw_w, m_l2_ffn_dw_b, m_l2_ffn_w_down, m_l3_norm_mix, m_l3_sg_w_in, m_l3_sg_b_in, m_l3_sg_ln_g, m_l3_sg_ln_b, m_l3_sg_w_s, m_l3_sg_b_s, m_l3_sg_w_out, m_l3_sg_b_out, m_l3_norm_ffn, m_l3_ffn_w_up, m_l3_ffn_dw_w, m_l3_ffn_dw_b, m_l3_ffn_w_down, m_final_norm, v_l0_norm_mix, v_l0_cc_w_in, v_l0_cc_b_in, v_l0_cc_dw_w, v_l0_cc_dw_b, v_l0_cc_ln_g, v_l0_cc_ln_b, v_l0_cc_w_out, v_l0_cc_b_out, v_l0_norm_ffn, v_l0_ffn_w_up, v_l0_ffn_dw_w, v_l0_ffn_dw_b, v_l0_ffn_w_down, v_l1_norm_mix, v_l1_dsa_w_qkv, v_l1_dsa_w_o, v_l1_norm_ffn, v_l1_ffn_w_up, v_l1_ffn_dw_w, v_l1_ffn_dw_b, v_l1_ffn_w_down, v_l2_norm_mix, v_l2_mla_w_in, v_l2_mla_q_norm, v_l2_mla_w_qb, v_l2_mla_kv_norm, v_l2_mla_w_kvb, v_l2_mla_w_o, v_l2_norm_ffn, v_l2_ffn_w_up, v_l2_ffn_dw_w, v_l2_ffn_dw_b, v_l2_ffn_w_down, v_l3_norm_mix, v_l3_sg_w_in, v_l3_sg_b_in, v_l3_sg_ln_g, v_l3_sg_ln_b, v_l3_sg_w_s, v_l3_sg_b_s, v_l3_sg_w_out, v_l3_sg_b_out, v_l3_norm_ffn, v_l3_ffn_w_up, v_l3_ffn_dw_w, v_l3_ffn_dw_b, v_l3_ffn_w_down, v_final_norm):
    return _train_step(dict(locals()))
```

```python
import functools
import inspect
import math

import jax
import jax.numpy as jnp
from jax import lax
from jax.experimental import pallas as pl
from jax.experimental.pallas import tpu as pltpu

F32 = jnp.float32
BF16 = jnp.bfloat16
NORM_EPS = 1e-6
NEG_INF = -1e30
ROPE_THETA = 10000.0
LANES = 128
SUBLANES = 8
VMEM_LIMIT = 56 * 1024 * 1024
N_DEV = 8

DSA_CONFIGS = ((128, 1), (512, 4), (2048, 16))
DSA_HEADS = 8
DSA_BLOCK = 128
MLA_HEADS = 16
MLA_RANK = 512
MLA_NOPE = 128
MLA_ROPE = 64
SG_CHUNK = 128
SG_GROUPS = 8
CC_WIDTH = 31
FFN_WIDTH = 3
ADAM_LR, ADAM_B1, ADAM_B2, ADAM_EPS, ADAM_WD, ADAM_STEP = 0.001, 0.9, 0.999, 1e-08, 0.01, 10


def _tile(n, cap, unit=LANES):
    if n <= cap:
        return n
    best = 0
    for t in range(unit, cap + 1, unit):
        if n % t == 0:
            best = t
    assert best, (n, cap, unit)
    return best


_PENDING = []


def _call(body, name, grid, in_specs, out_specs, out_shape, scratch=(), prefetch=0):
    params = pltpu.CompilerParams(vmem_limit_bytes=VMEM_LIMIT)
    deps = list(_PENDING)
    _PENDING.clear()
    if deps:
        inner, n_in = body, prefetch + len(in_specs)

        def body(*refs):
            return inner(*refs[:n_in], *refs[n_in + len(deps):])

        in_specs = list(in_specs) + [pl.BlockSpec(memory_space=pl.ANY)] * len(deps)
    if prefetch:
        spec = pltpu.PrefetchScalarGridSpec(num_scalar_prefetch=prefetch, grid=grid, in_specs=in_specs,
                                            out_specs=out_specs, scratch_shapes=list(scratch))
        fn = pl.pallas_call(body, out_shape=out_shape, grid_spec=spec, name=name, compiler_params=params)
    else:
        fn = pl.pallas_call(body, out_shape=out_shape, grid=grid, in_specs=in_specs, out_specs=out_specs,
                            scratch_shapes=list(scratch), name=name, compiler_params=params)
    return lambda *args: fn(*args, *deps)


def _sds(shape, dtype):
    return jax.ShapeDtypeStruct(tuple(shape), dtype)


def _bf(v):
    return v if v.dtype == BF16 else v.astype(BF16)


def _dot(a, b):
    return jnp.dot(a, b, preferred_element_type=F32)


def _dot_nt(a, b):
    return lax.dot_general(a, b, (((1,), (1,)), ((), ())), preferred_element_type=F32)


def _dot_tn(a, b):
    return lax.dot_general(a, b, (((0,), (0,)), ((), ())), preferred_element_type=F32)


def _sigmoid(v):
    return 1.0 / (1.0 + jnp.exp(-v))


_ERF_A = (-2.72614225801306e-10, 2.77068142495902e-08, -2.10102402082508e-06, -5.69250639462346e-05,
          -7.34990630326855e-04, -2.95459980854025e-03, -1.60960333262415e-02)
_ERF_B = (-1.45660718464996e-05, -2.13374055278905e-04, -1.68282697438203e-03, -7.37332916720468e-03,
          -1.42647390514189e-02)


def _erf(v):
    v = jnp.clip(v, -4.0, 4.0)
    v2 = v * v
    p = jnp.full_like(v, _ERF_A[0])
    for c in _ERF_A[1:]:
        p = p * v2 + c
    q = jnp.full_like(v, _ERF_B[0])
    for c in _ERF_B[1:]:
        q = q * v2 + c
    return v * p / q


def _gelu(v):
    return 0.5 * v * (1.0 + _erf(v * (2.0 ** -0.5)))


def _gelu_grad(v):
    return 0.5 * (1.0 + _erf(v * (2.0 ** -0.5))) + v * jnp.exp(-0.5 * v * v) * ((2.0 * math.pi) ** -0.5)


def _shift_down(v, k):
    if k == 0:
        return v
    rows = lax.broadcasted_iota(jnp.int32, v.shape, 0)
    return jnp.where(rows >= k, pltpu.roll(v, k, 0), 0.0)


def _shift_up(v, k):
    if k == 0:
        return v
    n = v.shape[0]
    rows = lax.broadcasted_iota(jnp.int32, v.shape, 0)
    return jnp.where(rows < n - k, pltpu.roll(v, n - k, 0), 0.0)


def _rope(v, cos, sin):
    return v * cos + pltpu.roll(v, 64, 1) * sin


def _rope_t(dv, cos, sin):
    return dv * cos + pltpu.roll(dv * sin, 64, 1)


def mm_nn(a, w, name, bias=None, out_parts=1, out_dtype=F32, tm_cap=512, tn_cap=1536, tk_cap=2816):
    pa, m, kp = a.shape
    j, k, n = w.shape
    assert pa * kp == k
    big_n = j * n
    npo = big_n // out_parts
    tm = _tile(m, tm_cap, SUBLANES)
    tn = _tile(math.gcd(n, npo), tn_cap)
    tk = _tile(kp, tk_cap)
    npj, nbo, kbp, nk = n // tn, npo // tn, kp // tk, k // tk
    has_bias = bias is not None

    def body(*refs):
        a_ref, w_ref = refs[0], refs[1]
        b_ref = refs[2] if has_bias else None
        o_ref = refs[2 + has_bias]

        def finish(acc):
            if has_bias:
                acc = acc + b_ref[...]
            o_ref[...] = acc.astype(o_ref.dtype)

        part = _dot(_bf(a_ref[...]), _bf(w_ref[...]))
        if nk == 1:
            finish(part)
        else:
            acc_ref = refs[3 + has_bias]
            ki = pl.program_id(2)

            @pl.when(ki == 0)
            def _():
                acc_ref[...] = part

            @pl.when(ki > 0)
            def _():
                acc_ref[...] += part

            @pl.when(ki == nk - 1)
            def _():
                finish(acc_ref[...])

    in_specs = [pl.BlockSpec((None, tm, tk), lambda ni, mi, ki: (ki // kbp, mi, ki % kbp)),
                pl.BlockSpec((None, tk, tn), lambda ni, mi, ki: (ni // npj, ki, ni % npj))]
    args = [a, w]
    if has_bias:
        in_specs.append(pl.BlockSpec((1, tn), lambda ni, mi, ki: (0, ni)))
        args.append(bias.reshape(1, big_n))
    out_spec = pl.BlockSpec((None, tm, tn), lambda ni, mi, ki: (ni // nbo, mi, ni % nbo))
    scratch = [pltpu.VMEM((tm, tn), F32)] if nk > 1 else []
    return _call(body, name, (big_n // tn, m // tm, nk), in_specs, out_spec,
                 _sds((out_parts, m, npo), out_dtype), scratch)(*args)


def mm_nt(a, w, name, out_dtype=F32, tm_cap=1024, tko_cap=2048, tc_cap=1536):
    pa, m, npa = a.shape
    j, k, n = w.shape
    assert pa * npa == j * n
    tm = _tile(m, tm_cap, SUBLANES)
    tko = _tile(k, tko_cap)
    tc = _tile(math.gcd(n, npa), tc_cap)
    npj, nba, nc = n // tc, npa // tc, (j * n) // tc

    def body(a_ref, w_ref, o_ref, *scr):
        part = _dot_nt(_bf(a_ref[...]), _bf(w_ref[...]))
        if nc == 1:
            o_ref[...] = part.astype(o_ref.dtype)
        else:
            acc_ref = scr[0]
            ci = pl.program_id(2)

            @pl.when(ci == 0)
            def _():
                acc_ref[...] = part

            @pl.when(ci > 0)
            def _():
                acc_ref[...] += part

            @pl.when(ci == nc - 1)
            def _():
                o_ref[...] = acc_ref[...].astype(o_ref.dtype)

    in_specs = [pl.BlockSpec((None, tm, tc), lambda mi, ko, ci: (ci // nba, mi, ci % nba)),
                pl.BlockSpec((None, tko, tc), lambda mi, ko, ci: (ci // npj, ko, ci % npj))]
    out_spec = pl.BlockSpec((tm, tko), lambda mi, ko, ci: (mi, ko))
    scratch = [pltpu.VMEM((tm, tko), F32)] if nc > 1 else []
    return _call(body, name, (m // tm, k // tko, nc), in_specs, out_spec, _sds((m, k), out_dtype), scratch)(a, w)


def mm_tn(a, b, n_shards, name, out_dtype=BF16, tt_cap=2048, tkk_cap=512, tn_cap=1536):
    pa, m, kp = a.shape
    pb, m2, npb = b.shape
    assert m == m2
    k, big_n = pa * kp, pb * npb
    n = big_n // n_shards
    tt = _tile(m, tt_cap, SUBLANES)
    tkk = _tile(kp, tkk_cap)
    tn = _tile(math.gcd(n, npb), tn_cap)
    kbp, nbb, npj, nt = kp // tkk, npb // tn, n // tn, m // tt

    def body(a_ref, b_ref, o_ref, *scr):
        part = _dot_tn(_bf(a_ref[...]), _bf(b_ref[...]))
        if nt == 1:
            o_ref[...] = part.astype(o_ref.dtype)
        else:
            acc_ref = scr[0]
            ti = pl.program_id(2)

            @pl.when(ti == 0)
            def _():
                acc_ref[...] = part

            @pl.when(ti > 0)
            def _():
                acc_ref[...] += part

            @pl.when(ti == nt - 1)
            def _():
                o_ref[...] = acc_ref[...].astype(o_ref.dtype)

    in_specs = [pl.BlockSpec((None, tt, tkk), lambda ni, ki, ti: (ki // kbp, ti, ki % kbp)),
                pl.BlockSpec((None, tt, tn), lambda ni, ki, ti: (ni // nbb, ti, ni % nbb))]
    out_spec = pl.BlockSpec((None, tkk, tn), lambda ni, ki, ti: (ni // npj, ki, ni % npj))
    scratch = [pltpu.VMEM((tkk, tn), F32)] if nt > 1 else []
    return _call(body, name, (big_n // tn, k // tkk, nt), in_specs, out_spec,
                 _sds((n_shards, k, n), out_dtype), scratch)(a, b)


_ANY = pl.BlockSpec(memory_space=pl.ANY)
_MESH = pl.DeviceIdType.MESH


def all_gather(shards, name):
    n = len(shards)

    def body(*refs):
        ins, outs = refs[:n], refs[n:2 * n]
        send_sems, recv_sems, local_sems = refs[2 * n:]
        x, y, c = lax.axis_index("x"), lax.axis_index("y"), lax.axis_index("c")
        me, sibling = (x, y, c), (x, y, 1 - c)
        chips = [(1 - x, y), (x, 1 - y), (1 - x, 1 - y)]

        def copy(i, k, block, to, src=None):
            px, py, pc = block
            dst = outs[i].at[4 * px + 2 * py + pc]
            return pltpu.make_async_remote_copy(src_ref=dst if src is None else src, dst_ref=dst,
                                                send_sem=send_sems.at[i, k], recv_sem=recv_sems.at[i, k],
                                                device_id=to, device_id_type=_MESH)

        mine = [pltpu.make_async_copy(ins[i], outs[i].at[4 * x + 2 * y + c], local_sems.at[i]) for i in range(n)]
        for cp in mine:
            cp.start()
        first = []
        for i in range(n):
            first.append(copy(i, 0, me, sibling, src=ins[i]))
            first += [copy(i, 1 + q, me, (*chip, c), src=ins[i]) for q, chip in enumerate(chips)]
        for cp in first:
            cp.start()
        passed = []
        for q, chip in enumerate(chips):
            for i in range(n):
                copy(i, 1 + q, (*chip, c), me).wait_recv()
                cp = copy(i, 4 + q, (*chip, c), sibling)
                cp.start()
                passed.append(cp)
        for i in range(n):
            copy(i, 0, sibling, me).wait_recv()
            for q, chip in enumerate(chips):
                copy(i, 4 + q, (*chip, 1 - c), me).wait_recv()
        for cp in first + passed:
            cp.wait_send()
        for cp in mine:
            cp.wait()

    outs = pl.pallas_call(
        body, name=name, out_shape=[_sds((N_DEV,) + s.shape, s.dtype) for s in shards],
        in_specs=[_ANY] * n, out_specs=[_ANY] * n,
        scratch_shapes=[pltpu.SemaphoreType.DMA((n, 7)), pltpu.SemaphoreType.DMA((n, 7)),
                        pltpu.SemaphoreType.DMA((n,))],
    )(*shards)
    return list(outs)


def rs_core_exchange(parts, name):
    n = len(parts)

    def body(*refs):
        ins, outs = refs[:n], refs[n:2 * n]
        send_sems, recv_sems = refs[2 * n:]
        x, y, c = lax.axis_index("x"), lax.axis_index("y"), lax.axis_index("c")
        sibling = (x, y, 1 - c)
        started = []
        for i in range(n):
            for chip in range(4):
                cp = pltpu.make_async_remote_copy(src_ref=ins[i].at[2 * chip + (1 - c)], dst_ref=outs[i].at[chip],
                                                  send_sem=send_sems.at[i, chip], recv_sem=recv_sems.at[i, chip],
                                                  device_id=sibling, device_id_type=_MESH)
                cp.start()
                started.append(cp)
        for cp in started:
            cp.wait()

    outs = pl.pallas_call(
        body, name=name, out_shape=[_sds((4,) + p.shape[1:], p.dtype) for p in parts],
        in_specs=[_ANY] * n, out_specs=[_ANY] * n,
        scratch_shapes=[pltpu.SemaphoreType.DMA((n, 4)), pltpu.SemaphoreType.DMA((n, 4))],
    )(*parts)
    return list(outs)


def rs_chip_exchange(sums, name):
    n = len(sums)

    def body(*refs):
        ins, outs = refs[:n], refs[n:2 * n]
        send_sems, recv_sems = refs[2 * n:]
        x, y, c = lax.axis_index("x"), lax.axis_index("y"), lax.axis_index("c")
        peers = [(x, 1 - y), (1 - x, y), (1 - x, 1 - y)]
        started = []
        for i in range(n):
            for k, (px, py) in enumerate(peers):
                cp = pltpu.make_async_remote_copy(src_ref=ins[i].at[2 * px + py], dst_ref=outs[i].at[k],
                                                  send_sem=send_sems.at[i, k], recv_sem=recv_sems.at[i, k],
                                                  device_id=(px, py, c), device_id_type=_MESH)
                cp.start()
                started.append(cp)
        for cp in started:
            cp.wait()

    outs = pl.pallas_call(
        body, name=name, out_shape=[_sds((3,) + s.shape[1:], s.dtype) for s in sums],
        in_specs=[_ANY] * n, out_specs=[_ANY] * n,
        scratch_shapes=[pltpu.SemaphoreType.DMA((n, 3)), pltpu.SemaphoreType.DMA((n, 3))],
    )(*sums)
    return list(outs)


_HBM = pl.BlockSpec(memory_space=pltpu.HBM)
_SEM = pl.BlockSpec(memory_space=pltpu.SEMAPHORE)
_EFFECT = pltpu.SideEffectType.DATAFLOW_SIDE_EFFECTING


def _in_hbm(a):
    return pltpu.with_memory_space_constraint(a, pltpu.HBM)


def _plan_copies(plan, srcs, dsts, send_sems, recv_sems):
    x, y, c = lax.axis_index("x"), lax.axis_index("y"), lax.axis_index("c")
    return [pltpu.make_async_remote_copy(src_ref=s, dst_ref=d, send_sem=send_sems.at[q], recv_sem=recv_sems.at[q],
                                         device_id=to, device_id_type=_MESH)
            for q, (s, d, to) in enumerate(plan(x, y, c, srcs, dsts))]


def exchange_start(plan, n_copies, srcs, dsts, after, name):
    ns, nd, na = len(srcs), len(dsts), len(after)

    def body(*refs):
        ins = refs[:ns + nd + na]
        send_sems, recv_sems = refs[ns + nd + na], refs[ns + nd + na + 1]
        token = refs[-1]
        for cp in _plan_copies(plan, ins[:ns], ins[ns:ns + nd], send_sems, recv_sems):
            cp.start()
        token[...] = jnp.zeros_like(token)

    thru = [pltpu.HBM(a.shape, a.dtype) for a in list(srcs) + list(dsts)]
    out = pl.pallas_call(
        body, name=name,
        out_shape=[pltpu.SemaphoreType.DMA((n_copies,)), pltpu.SemaphoreType.DMA((n_copies,))] + thru
        + [_sds((SUBLANES, LANES), F32)],
        in_specs=[_HBM] * (ns + nd) + [_ANY] * na,
        out_specs=[_SEM, _SEM] + [_HBM] * (ns + nd) + [pl.BlockSpec(memory_space=pltpu.VMEM)],
        input_output_aliases={i: 2 + i for i in range(ns + nd)},
        compiler_params=pltpu.CompilerParams(has_side_effects=_EFFECT),
    )(*[_in_hbm(a) for a in list(srcs) + list(dsts)], *after)
    return out[0], out[1], list(out[2:2 + ns]), list(out[2 + ns:2 + ns + nd]), out[-1]


def exchange_wait(plan, started, after, name):
    send_sems, recv_sems, srcs, dsts, _ = started
    ns, nd, na = len(srcs), len(dsts), len(after)

    def body(*refs):
        ins = refs[:ns + nd]
        send_sems_ref, recv_sems_ref = refs[ns + nd], refs[ns + nd + 1]
        for cp in _plan_copies(plan, ins[:ns], ins[ns:ns + nd], send_sems_ref, recv_sems_ref):
            cp.wait_send()
            cp.wait_recv()

    thru = [pltpu.HBM(a.shape, a.dtype) for a in list(srcs) + list(dsts)]
    out = pl.pallas_call(
        body, name=name, out_shape=thru,
        in_specs=[_HBM] * (ns + nd) + [_SEM, _SEM] + [_ANY] * na,
        out_specs=[_HBM] * (ns + nd),
        input_output_aliases={i: i for i in range(ns + nd)},
        compiler_params=pltpu.CompilerParams(has_side_effects=_EFFECT),
    )(*srcs, *dsts, send_sems, recv_sems, *after)
    return list(out[:ns]), list(out[ns:])


def _dev_index(px, py, pc):
    return 4 * px + 2 * py + pc


def _ag_plan_first(x, y, c, srcs, dsts):
    me = _dev_index(x, y, c)
    peers = [(x, y, 1 - c), (1 - x, y, c), (x, 1 - y, c), (1 - x, 1 - y, c)]
    return [(s, d.at[me], to) for s, d in zip(srcs, dsts) for to in peers]


def _ag_plan_pass(x, y, c, srcs, dsts):
    chips = [(1 - x, y), (x, 1 - y), (1 - x, 1 - y)]
    return [(d.at[_dev_index(px, py, c)], d.at[_dev_index(px, py, c)], (x, y, 1 - c)) for d in dsts for px, py in chips]


def _rs_plan_cores(x, y, c, srcs, dsts):
    return [(s.at[2 * chip + (1 - c)], d.at[chip], (x, y, 1 - c)) for s, d in zip(srcs, dsts) for chip in range(4)]


def _rs_plan_chips(x, y, c, srcs, dsts):
    peers = [(x, 1 - y), (1 - x, y), (1 - x, 1 - y)]
    return [(s.at[2 * px + py], d.at[k], (px, py, c)) for s, d in zip(srcs, dsts) for k, (px, py) in enumerate(peers)]


def _row_tile(r, c, itemsize=4, budget=1 << 20):
    cap = max(SUBLANES, (budget // (c * itemsize)) // SUBLANES * SUBLANES)
    if r <= cap:
        return r
    best = 0
    for t in range(SUBLANES, cap + 1, SUBLANES):
        if r % t == 0:
            best = t
    return best if best else r


def rs_core_add(part, recv, core, name):
    _, r, c = part.shape
    tr = _row_tile(r, c)

    def body(core_ref, p_ref, q_ref, o_ref):
        o_ref[...] = (p_ref[...].astype(F32) + q_ref[...].astype(F32)).astype(o_ref.dtype)

    in_specs = [pl.BlockSpec((None, tr, c), lambda ch, ri, core_ref: (2 * ch + core_ref[0], ri, 0)),
                pl.BlockSpec((None, tr, c), lambda ch, ri, core_ref: (ch, ri, 0))]
    out_spec = pl.BlockSpec((None, tr, c), lambda ch, ri, core_ref: (ch, ri, 0))
    return _call(body, name, (4, r // tr), in_specs, out_spec, _sds((4, r, c), part.dtype), prefetch=1)(core, part, recv)


def _adamw(w, g, m, v):
    m = ADAM_B1 * m + (1.0 - ADAM_B1) * g
    v = ADAM_B2 * v + (1.0 - ADAM_B2) * (g * g)
    m_hat = m / (1.0 - ADAM_B1 ** ADAM_STEP)
    v_hat = v / (1.0 - ADAM_B2 ** ADAM_STEP)
    delta = -ADAM_LR * (m_hat / (jnp.sqrt(v_hat) + ADAM_EPS) + ADAM_WD * w)
    return delta, m, v


def rs_finish(sums, recv, chip, name, adam=None):
    _, r, c = sums.shape
    tr = _row_tile(r, c, budget=1 << 19)

    def body(chip_ref, s_ref, q_ref, *refs):
        g = s_ref[...].astype(F32)
        for k in range(3):
            g = g + q_ref[k].astype(F32)
        if adam is None:
            refs[0][...] = g
        else:
            w_ref, m_ref, v_ref, g_ref, d_ref, nm_ref, nv_ref = refs
            g_ref[...] = g
            d_ref[...], nm_ref[...], nv_ref[...] = _adamw(w_ref[...], g, m_ref[...], v_ref[...])

    blk = pl.BlockSpec((tr, c), lambda ri, chip_ref: (ri, 0))
    in_specs = [pl.BlockSpec((None, tr, c), lambda ri, chip_ref: (chip_ref[0], ri, 0)),
                pl.BlockSpec((3, tr, c), lambda ri, chip_ref: (0, ri, 0))]
    args = [chip, sums, recv]
    if adam is None:
        out_specs, out_shape = blk, _sds((r, c), F32)
    else:
        in_specs += [blk] * 3
        args += list(adam)
        out_specs, out_shape = [blk] * 4, [_sds((r, c), F32)] * 4
    return _call(body, name, (r // tr,), in_specs, out_specs, out_shape, prefetch=1)(*args)


def adam_flat(w, g, m, v, name):
    r, c = w.shape
    tr = _row_tile(r, c, budget=1 << 19)

    def body(w_ref, g_ref, m_ref, v_ref, d_ref, nm_ref, nv_ref):
        d_ref[...], nm_ref[...], nv_ref[...] = _adamw(w_ref[...], g_ref[...], m_ref[...], v_ref[...])

    blk = pl.BlockSpec((tr, c), lambda ri: (ri, 0))
    return _call(body, name, (r // tr,), [blk] * 4, [blk] * 3, [_sds((r, c), F32)] * 3)(w, g, m, v)


def rms_fwd(x, g, name, add=None):
    t, d = x.shape
    tm = _tile(t, 256, SUBLANES)
    has_add = add is not None

    def body(*refs):
        if has_add:
            x_ref, a_ref, g_ref, h_ref, hn_ref = refs
            h = x_ref[...] + a_ref[...]
            h_ref[...] = h
        else:
            x_ref, g_ref, hn_ref = refs
            h = x_ref[...]
        r = lax.rsqrt(jnp.mean(h * h, axis=-1, keepdims=True) + NORM_EPS)
        hn_ref[...] = (h * r * g_ref[...]).astype(BF16)

    row = pl.BlockSpec((tm, d), lambda i: (i, 0))
    vec = pl.BlockSpec((1, d), lambda i: (0, 0))
    if has_add:
        return _call(body, name, (t // tm,), [row, row, vec], [row, row], [_sds((t, d), F32), _sds((t, d), BF16)])(
            x, add, g.reshape(1, d))
    return _call(body, name, (t // tm,), [row, vec], row, _sds((t, d), BF16))(x, g.reshape(1, d))


def _rms_bwd_math(x, g, dy):
    r = lax.rsqrt(jnp.mean(x * x, axis=-1, keepdims=True) + NORM_EPS)
    xh = x * r
    dyg = dy * g
    dx = r * (dyg - xh * jnp.mean(dyg * xh, axis=-1, keepdims=True))
    return dx, jnp.sum(dy * xh, axis=0, keepdims=True)


def rms_bwd(x, g, dhn, dres, name):
    t, d = x.shape
    tm = _tile(t, 256, SUBLANES)

    def body(x_ref, g_ref, dy_ref, dr_ref, dx_ref, dg_ref):
        dx, dg = _rms_bwd_math(x_ref[...], g_ref[...], dy_ref[...])
        dx_ref[...] = dx + dr_ref[...]

        @pl.when(pl.program_id(0) == 0)
        def _():
            dg_ref[...] = jnp.zeros_like(dg_ref)

        dg_ref[...] += dg

    row = pl.BlockSpec((tm, d), lambda i: (i, 0))
    vec = pl.BlockSpec((1, d), lambda i: (0, 0))
    return _call(body, name, (t // tm,), [row, vec, row, row], [row, vec], [_sds((t, d), F32), _sds((1, d), F32)])(
        x, g.reshape(1, d), dhn, dres)


def final_loss(h, add, g, target, name):
    t, d = h.shape
    tm = _tile(t, 256, SUBLANES)

    def body(x_ref, a_ref, g_ref, t_ref, loss_ref, dx_ref, dg_ref):
        x = x_ref[...] + a_ref[...]
        gain = g_ref[...]
        r = lax.rsqrt(jnp.mean(x * x, axis=-1, keepdims=True) + NORM_EPS)
        err = x * r * gain - t_ref[...]
        dx, dg = _rms_bwd_math(x, gain, err * (1.0 / d))
        dx_ref[...] = dx

        @pl.when(pl.program_id(0) == 0)
        def _():
            dg_ref[...] = jnp.zeros_like(dg_ref)
            loss_ref[...] = jnp.zeros_like(loss_ref)

        dg_ref[...] += dg
        part = jnp.sum(jnp.sum(err * err, axis=-1, keepdims=True), axis=0, keepdims=True) * (0.5 / d)
        loss_ref[...] += jnp.broadcast_to(part, loss_ref.shape)

    row = pl.BlockSpec((tm, d), lambda i: (i, 0))
    vec = pl.BlockSpec((1, d), lambda i: (0, 0))
    one = pl.BlockSpec((1, LANES), lambda i: (0, 0))
    return _call(body, name, (t // tm,), [row, row, vec, row], [one, row, vec],
                 [_sds((1, LANES), F32), _sds((t, d), F32), _sds((1, d), F32)])(h, add, g.reshape(1, d), target)


def _conv_causal(z, w_ref_rows, width):
    out = z * w_ref_rows(width - 1)
    for k in range(width - 1):
        out = out + _shift_down(z, width - 1 - k) * w_ref_rows(k)
    return out


def ffn_gate_fwd(z2, dw_w2, dw_b2, name, tc=LANES):
    _, b, s, f = z2.shape

    def body(z_ref, w_ref, b_ref, y_ref):
        g = _conv_causal(z_ref[0], lambda k: w_ref[0, k:k + 1, :], FFN_WIDTH) + b_ref[0]
        a = _conv_causal(z_ref[1], lambda k: w_ref[1, k:k + 1, :], FFN_WIDTH) + b_ref[1]
        y_ref[...] = (g * _sigmoid(g) * a).astype(BF16)

    in_specs = [pl.BlockSpec((2, None, s, tc), lambda bi, ci: (0, bi, 0, ci)),
                pl.BlockSpec((2, FFN_WIDTH, tc), lambda bi, ci: (0, 0, ci)),
                pl.BlockSpec((2, 1, tc), lambda bi, ci: (0, 0, ci))]
    out_spec = pl.BlockSpec((None, s, tc), lambda bi, ci: (bi, 0, ci))
    return _call(body, name, (b, f // tc), in_specs, out_spec, _sds((b, s, f), BF16))(z2, dw_w2, dw_b2)


def ffn_gate_bwd(z2, dy, dw_w2, dw_b2, name, tc=LANES):
    _, b, s, f = z2.shape

    def body(z_ref, dy_ref, w_ref, b_ref, dz_ref, dw_ref, db_ref):
        @pl.when(pl.program_id(1) == 0)
        def _():
            dw_ref[...] = jnp.zeros_like(dw_ref)
            db_ref[...] = jnp.zeros_like(db_ref)

        zs = (z_ref[0], z_ref[1])
        g = _conv_causal(zs[0], lambda k: w_ref[0, k:k + 1, :], FFN_WIDTH) + b_ref[0]
        a = _conv_causal(zs[1], lambda k: w_ref[1, k:k + 1, :], FFN_WIDTH) + b_ref[1]
        sg = _sigmoid(g)
        dy = dy_ref[...]
        dcs = (dy * a * (sg * (1.0 + g * (1.0 - sg))), dy * (g * sg))
        for p in range(2):
            dc = dcs[p]
            dz = dc * w_ref[p, FFN_WIDTH - 1:FFN_WIDTH, :]
            for k in range(FFN_WIDTH - 1):
                dz = dz + _shift_up(dc, FFN_WIDTH - 1 - k) * w_ref[p, k:k + 1, :]
            dz_ref[p] = dz.astype(BF16)
            for k in range(FFN_WIDTH):
                dw_ref[p, k:k + 1, :] += jnp.sum(dc * _shift_down(zs[p], FFN_WIDTH - 1 - k), axis=0, keepdims=True)
            db_ref[p] += jnp.sum(dc, axis=0, keepdims=True)

    in_specs = [pl.BlockSpec((2, None, s, tc), lambda ci, bi: (0, bi, 0, ci)),
                pl.BlockSpec((None, s, tc), lambda ci, bi: (bi, 0, ci)),
                pl.BlockSpec((2, FFN_WIDTH, tc), lambda ci, bi: (0, 0, ci)),
                pl.BlockSpec((2, 1, tc), lambda ci, bi: (0, 0, ci))]
    out_specs = [pl.BlockSpec((2, None, s, tc), lambda ci, bi: (0, bi, 0, ci)),
                 pl.BlockSpec((2, FFN_WIDTH, tc), lambda ci, bi: (0, 0, ci)),
                 pl.BlockSpec((2, 1, tc), lambda ci, bi: (0, 0, ci))]
    out_shape = [_sds((2, b, s, f), BF16), _sds((2, FFN_WIDTH, f), F32), _sds((2, 1, f), F32)]
    return _call(body, name, (f // tc, b), in_specs, out_specs, out_shape)(z2, dy, dw_w2, dw_b2)


def cc_conv_fwd(z2, dw_w, dw_b, name, tc=LANES):
    _, b, s, c = z2.shape

    def body(z_ref, w_ref, b_ref, o_ref):
        u = z_ref[0] * _sigmoid(z_ref[1])
        o_ref[...] = _conv_causal(u, lambda k: w_ref[k:k + 1, :], CC_WIDTH) + b_ref[...]

    in_specs = [pl.BlockSpec((2, None, s, tc), lambda bi, ci: (0, bi, 0, ci)),
                pl.BlockSpec((CC_WIDTH, tc), lambda bi, ci: (0, ci)),
                pl.BlockSpec((1, tc), lambda bi, ci: (0, ci))]
    out_spec = pl.BlockSpec((None, s, tc), lambda bi, ci: (bi, 0, ci))
    return _call(body, name, (b, c // tc), in_specs, out_spec, _sds((b, s, c), F32))(z2, dw_w, dw_b.reshape(1, c))


def cc_conv_bwd(z2, dcv, dw_w, name, tc=LANES):
    _, b, s, c = z2.shape

    def body(z_ref, dc_ref, w_ref, dz_ref, dbi_ref, dw_ref, db_ref):
        @pl.when(pl.program_id(1) == 0)
        def _():
            dbi_ref[...] = jnp.zeros_like(dbi_ref)
            dw_ref[...] = jnp.zeros_like(dw_ref)
            db_ref[...] = jnp.zeros_like(db_ref)

        a, gate = z_ref[0], z_ref[1]
        sg = _sigmoid(gate)
        u = a * sg
        dc = dc_ref[...]
        du = jnp.zeros_like(dc)
        for j in range(CC_WIDTH):
            sd = _shift_up(dc, j)
            k = CC_WIDTH - 1 - j
            du = du + sd * w_ref[k:k + 1, :]
            dw_ref[k:k + 1, :] += jnp.sum(sd * u, axis=0, keepdims=True)
        db_ref[...] += jnp.sum(dc, axis=0, keepdims=True)
        da = du * sg
        dg = du * a * sg * (1.0 - sg)
        dz_ref[0] = da.astype(BF16)
        dz_ref[1] = dg.astype(BF16)
        dbi_ref[0] += jnp.sum(da, axis=0, keepdims=True)
        dbi_ref[1] += jnp.sum(dg, axis=0, keepdims=True)

    in_specs = [pl.BlockSpec((2, None, s, tc), lambda ci, bi: (0, bi, 0, ci)),
                pl.BlockSpec((None, s, tc), lambda ci, bi: (bi, 0, ci)),
                pl.BlockSpec((CC_WIDTH, tc), lambda ci, bi: (0, ci))]
    out_specs = [pl.BlockSpec((2, None, s, tc), lambda ci, bi: (0, bi, 0, ci)),
                 pl.BlockSpec((2, 1, tc), lambda ci, bi: (0, 0, ci)),
                 pl.BlockSpec((CC_WIDTH, tc), lambda ci, bi: (0, ci)),
                 pl.BlockSpec((1, tc), lambda ci, bi: (0, ci))]
    out_shape = [_sds((2, b, s, c), BF16), _sds((2, 1, c), F32), _sds((CC_WIDTH, c), F32), _sds((1, c), F32)]
    return _call(body, name, (c // tc, b), in_specs, out_specs, out_shape)(z2, dcv, dw_w)


def _ln_stats(v):
    mu = jnp.mean(v, axis=-1, keepdims=True)
    vc = v - mu
    r = lax.rsqrt(jnp.mean(vc * vc, axis=-1, keepdims=True) + NORM_EPS)
    return vc * r, r


def _ln_bwd(dln, xh, r, g):
    dxh = dln * g
    dx = r * (dxh - jnp.mean(dxh, axis=-1, keepdims=True) - xh * jnp.mean(dxh * xh, axis=-1, keepdims=True))
    return dx, jnp.sum(dln * xh, axis=0, keepdims=True), jnp.sum(dln, axis=0, keepdims=True)


def cc_ln_fwd(cv, ln_g, ln_b, name):
    t, c = cv.shape
    tm = _tile(t, 256, SUBLANES)

    def body(x_ref, g_ref, b_ref, o_ref):
        xh, _ = _ln_stats(x_ref[...])
        ln = xh * g_ref[...] + b_ref[...]
        o_ref[...] = (ln * _sigmoid(ln)).astype(BF16)

    row = pl.BlockSpec((tm, c), lambda i: (i, 0))
    vec = pl.BlockSpec((1, c), lambda i: (0, 0))
    return _call(body, name, (t // tm,), [row, vec, vec], row, _sds((t, c), BF16))(
        cv, ln_g.reshape(1, c), ln_b.reshape(1, c))


def cc_ln_bwd(cv, ln_g, ln_b, ds, name):
    t, c = cv.shape
    tm = _tile(t, 256, SUBLANES)

    def body(x_ref, g_ref, b_ref, ds_ref, dx_ref, dg_ref, db_ref):
        @pl.when(pl.program_id(0) == 0)
        def _():
            dg_ref[...] = jnp.zeros_like(dg_ref)
            db_ref[...] = jnp.zeros_like(db_ref)

        xh, r = _ln_stats(x_ref[...])
        ln = xh * g_ref[...] + b_ref[...]
        sg = _sigmoid(ln)
        dln = ds_ref[...] * (sg * (1.0 + ln * (1.0 - sg)))
        dx, dg, db = _ln_bwd(dln, xh, r, g_ref[...])
        dx_ref[...] = dx
        dg_ref[...] += dg
        db_ref[...] += db

    row = pl.BlockSpec((tm, c), lambda i: (i, 0))
    vec = pl.BlockSpec((1, c), lambda i: (0, 0))
    return _call(body, name, (t // tm,), [row, vec, vec, row], [row, vec, vec],
                 [_sds((t, c), F32), _sds((1, c), F32), _sds((1, c), F32)])(
        cv, ln_g.reshape(1, c), ln_b.reshape(1, c), ds)


def col_sum(v, name):
    t, n = v.shape
    tm = _tile(t, 256, SUBLANES)

    def body(v_ref, o_ref):
        @pl.when(pl.program_id(0) == 0)
        def _():
            o_ref[...] = jnp.zeros_like(o_ref)

        o_ref[...] += jnp.sum(v_ref[...], axis=0, keepdims=True)

    return _call(body, name, (t // tm,), [pl.BlockSpec((tm, n), lambda i: (i, 0))],
                 pl.BlockSpec((1, n), lambda i: (0, 0)), _sds((1, n), F32))(v)


def rope_tables_full(seq, dim):
    pos = jnp.arange(seq, dtype=F32)
    inv = ROPE_THETA ** (-(jnp.arange(0, dim, 2, dtype=F32) / dim))
    ang = pos[:, None] * inv[None, :]
    cos, sin = jnp.cos(ang), jnp.sin(ang)
    return jnp.concatenate([cos, cos], axis=-1), jnp.concatenate([-sin, sin], axis=-1)


def dsa_rope_fwd(qkv, cos, sin, seq, name):
    t, w = qkv.shape
    tm = _tile(seq, 256, SUBLANES)
    nsb = seq // tm
    hb = DSA_HEADS

    def body(x_ref, c_ref, s_ref, o_ref):
        cos_t, sin_t = c_ref[...], s_ref[...]
        for blk in range(w // LANES):
            v = x_ref[:, blk * LANES:(blk + 1) * LANES]
            if (blk // hb) % 3 < 2:
                v = _rope(v, cos_t, sin_t)
            o_ref[:, blk * LANES:(blk + 1) * LANES] = v.astype(BF16)

    row = pl.BlockSpec((tm, w), lambda i: (i, 0))
    tab = pl.BlockSpec((tm, LANES), lambda i: (i % nsb, 0))
    return _call(body, name, (t // tm,), [row, tab, tab], row, _sds((t, w), BF16))(qkv, cos, sin)


def dsa_rope_bwd(grads, cos, sin, seq, name):
    flat = [a for grp in grads for a in grp]
    t, hw = flat[0].shape
    w = hw * len(flat)
    tm = _tile(seq, 256, SUBLANES)
    nsb = seq // tm

    def body(*refs):
        ins, (c_ref, s_ref, o_ref) = refs[:len(flat)], refs[len(flat):]
        cos_t, sin_t = c_ref[...], s_ref[...]
        for idx, ref in enumerate(ins):
            for h in range(hw // LANES):
                v = ref[:, h * LANES:(h + 1) * LANES]
                if idx % 3 < 2:
                    v = _rope_t(v, cos_t, sin_t)
                col = idx * hw + h * LANES
                o_ref[:, col:col + LANES] = v.astype(BF16)

    part = pl.BlockSpec((tm, hw), lambda i: (i, 0))
    tab = pl.BlockSpec((tm, LANES), lambda i: (i % nsb, 0))
    out = _call(body, name, (t // tm,), [part] * len(flat) + [tab, tab], pl.BlockSpec((tm, w), lambda i: (i, 0)),
                _sds((t, w), BF16))(*flat, cos, sin)
    return out[None]


def _dsa_mask(n):
    qi = lax.broadcasted_iota(jnp.int32, (DSA_BLOCK, (1 if n == 0 else 2) * DSA_BLOCK), 0)
    kj = lax.broadcasted_iota(jnp.int32, (DSA_BLOCK, (1 if n == 0 else 2) * DSA_BLOCK), 1)
    if n == 0:
        return qi >= kj
    dist = DSA_BLOCK + qi - kj
    return (dist >= 0) & (dist <= DSA_BLOCK)


def dsa_attn_fwd(qkv_b, grp, dil, batch, seq, name):
    t, w = qkv_b.shape
    ell = seq // dil
    nb = ell // DSA_BLOCK
    nw, h_n = w // LANES, DSA_HEADS
    base = grp * 3 * h_n
    scale = DSA_BLOCK ** -0.5

    def body(q_ref, k_ref, v_ref, o_ref, lse_ref):
        for n in range(nb):
            lo = 0 if n == 0 else (n - 1) * DSA_BLOCK
            hi = (n + 1) * DSA_BLOCK
            q = q_ref[n * DSA_BLOCK:hi, :]
            s = _dot_nt(q, k_ref[lo:hi, :]) * scale
            s = jnp.where(_dsa_mask(n), s, NEG_INF)
            m = jnp.max(s, axis=-1, keepdims=True)
            p = jnp.exp(s - m)
            l = jnp.sum(p, axis=-1, keepdims=True)
            o_ref[n * DSA_BLOCK:hi, :] = _dot(p.astype(BF16), v_ref[lo:hi, :]) / l
            lse_ref[n * DSA_BLOCK:hi, :] = jnp.broadcast_to(m + jnp.log(l), (DSA_BLOCK, LANES))

    def spec(off):
        return pl.BlockSpec((None, ell, LANES), lambda b, r, h: (b, 0, r * nw + base + off + h))

    out_spec = pl.BlockSpec((None, ell, LANES), lambda b, r, h: (b, 0, r * h_n + h))
    view = qkv_b.reshape(batch, ell, dil * w)
    o, lse = _call(body, name, (batch, dil, h_n), [spec(0), spec(h_n), spec(2 * h_n)], [out_spec, out_spec],
                   [_sds((batch, ell, dil * h_n * LANES), F32)] * 2)(view, view, view)
    return o.reshape(t, h_n * LANES), lse.reshape(t, h_n * LANES)


def dsa_merge(outs, lses, name):
    t, hw = outs[0].shape
    tm = _tile(t, 256, SUBLANES)
    ng = len(outs)

    def body(*refs):
        o_refs, l_refs, (o_ref, lse_ref) = refs[:ng], refs[ng:2 * ng], refs[2 * ng:]
        ls = [r[...] for r in l_refs]
        m = functools.reduce(jnp.maximum, ls)
        es = [jnp.exp(v - m) for v in ls]
        tot = functools.reduce(lambda a, b: a + b, es)
        acc = jnp.zeros_like(m)
        for e, r in zip(es, o_refs):
            acc = acc + (e / tot) * r[...]
        o_ref[...] = acc
        lse_ref[...] = m + jnp.log(tot)

    row = pl.BlockSpec((tm, hw), lambda i: (i, 0))
    return _call(body, name, (t // tm,), [row] * (2 * ng), [row, row], [_sds((t, hw), F32)] * 2)(*outs, *lses)


def head_delta(do, o, name):
    t, hw = do.shape
    tm = _tile(t, 256, SUBLANES)

    def body(do_ref, o_ref, d_ref):
        for h in range(hw // LANES):
            sl = slice(h * LANES, (h + 1) * LANES)
            d = jnp.sum(do_ref[:, sl] * o_ref[:, sl], axis=-1, keepdims=True)
            d_ref[:, sl] = jnp.broadcast_to(d, (tm, LANES))

    row = pl.BlockSpec((tm, hw), lambda i: (i, 0))
    return _call(body, name, (t // tm,), [row, row], row, _sds((t, hw), F32))(do, o)


def dsa_attn_bwd(qkv_b, do, lse, delta, grp, dil, batch, seq, name):
    t, w = qkv_b.shape
    ell = seq // dil
    nb = ell // DSA_BLOCK
    nw, h_n = w // LANES, DSA_HEADS
    base = grp * 3 * h_n
    scale = DSA_BLOCK ** -0.5

    def body(q_ref, k_ref, v_ref, do_ref, lse_ref, dl_ref, dq_ref, dk_ref, dv_ref):
        dk_ref[...] = jnp.zeros_like(dk_ref)
        dv_ref[...] = jnp.zeros_like(dv_ref)
        for n in range(nb):
            lo = 0 if n == 0 else (n - 1) * DSA_BLOCK
            hi = (n + 1) * DSA_BLOCK
            rows = slice(n * DSA_BLOCK, hi)
            q, k, v = q_ref[rows, :], k_ref[lo:hi, :], v_ref[lo:hi, :]
            dout = do_ref[rows, :].astype(BF16)
            s = jnp.where(_dsa_mask(n), _dot_nt(q, k) * scale, NEG_INF)
            p = jnp.exp(s - lse_ref[rows, 0:1])
            ds = (p * (_dot_nt(dout, v) - dl_ref[rows, 0:1]) * scale).astype(BF16)
            dq_ref[rows, :] = _dot(ds, k)
            dk_ref[lo:hi, :] += _dot_tn(ds, q)
            dv_ref[lo:hi, :] += _dot_tn(p.astype(BF16), dout)

    def spec(off):
        return pl.BlockSpec((None, ell, LANES), lambda b, r, h: (b, 0, r * nw + base + off + h))

    hspec = pl.BlockSpec((None, ell, LANES), lambda b, r, h: (b, 0, r * h_n + h))
    view = qkv_b.reshape(batch, ell, dil * w)
    hv = [a.reshape(batch, ell, dil * h_n * LANES) for a in (do, lse, delta)]
    outs = _call(body, name, (batch, dil, h_n), [spec(0), spec(h_n), spec(2 * h_n), hspec, hspec, hspec],
                 [hspec] * 3, [_sds((batch, ell, dil * h_n * LANES), F32)] * 3)(view, view, view, *hv)
    return [a.reshape(t, h_n * LANES) for a in outs]


def mla_rope_tables(seq):
    pos = jnp.arange(seq, dtype=F32)
    inv = ROPE_THETA ** (-(jnp.arange(0, MLA_ROPE, 2, dtype=F32) / MLA_ROPE))
    ang = pos[:, None] * inv[None, :]
    cos, sin, zero = jnp.cos(ang), jnp.sin(ang), jnp.zeros_like(ang)
    return jnp.concatenate([cos, zero, cos, zero], axis=-1), jnp.concatenate([-sin, zero, sin, zero], axis=-1)


def _spread_rope_cols(w_pe):
    half = MLA_ROPE // 2
    zero = jnp.zeros(w_pe.shape[:-1] + (half,), w_pe.dtype)
    return jnp.concatenate([w_pe[..., :half], zero, w_pe[..., half:], zero], axis=-1)


def _gather_rope_cols(g_pe):
    half = MLA_ROPE // 2
    return jnp.concatenate([g_pe[..., :half], g_pe[..., 2 * half:3 * half]], axis=-1)


def mla_low_fwd(c, q_norm, kv_norm, cos, sin, seq, name):
    t, w = c.shape
    rk = MLA_RANK
    tm = _tile(seq, 256, SUBLANES)
    nsb = seq // tm

    def body(c_ref, qg_ref, kg_ref, cs_ref, sn_ref, qn_ref, kn_ref, kp_ref):
        for lo, g_ref, o_ref in ((0, qg_ref, qn_ref), (rk, kg_ref, kn_ref)):
            v = c_ref[:, lo:lo + rk]
            r = lax.rsqrt(jnp.mean(v * v, axis=-1, keepdims=True) + NORM_EPS)
            o_ref[...] = (v * r * g_ref[...]).astype(BF16)
        kp_ref[...] = _rope(c_ref[:, 2 * rk:], cs_ref[...], sn_ref[...]).astype(BF16)

    row = lambda n: pl.BlockSpec((tm, n), lambda i: (i, 0))
    vec = pl.BlockSpec((1, rk), lambda i: (0, 0))
    tab = pl.BlockSpec((tm, LANES), lambda i: (i % nsb, 0))
    return _call(body, name, (t // tm,), [row(w), vec, vec, tab, tab], [row(rk), row(rk), row(LANES)],
                 [_sds((t, rk), BF16), _sds((t, rk), BF16), _sds((t, LANES), BF16)])(
        c, q_norm.reshape(1, rk), kv_norm.reshape(1, rk), cos, sin)


def mla_low_bwd(c, q_norm, kv_norm, dqn, dkn, dkp, cos, sin, seq, name):
    t, w = c.shape
    rk = MLA_RANK
    tm = _tile(seq, 256, SUBLANES)
    nsb = seq // tm

    def body(c_ref, qg_ref, kg_ref, dq_ref, dk_ref, dp_ref, cs_ref, sn_ref, dc_ref, dqg_ref, dkg_ref):
        @pl.when(pl.program_id(0) == 0)
        def _():
            dqg_ref[...] = jnp.zeros_like(dqg_ref)
            dkg_ref[...] = jnp.zeros_like(dkg_ref)

        for lo, g_ref, d_ref, dg_ref in ((0, qg_ref, dq_ref, dqg_ref), (rk, kg_ref, dk_ref, dkg_ref)):
            dx, dg = _rms_bwd_math(c_ref[:, lo:lo + rk], g_ref[...], d_ref[...])
            dc_ref[:, lo:lo + rk] = dx.astype(BF16)
            dg_ref[...] += dg
        dc_ref[:, 2 * rk:] = _rope_t(dp_ref[...], cs_ref[...], sn_ref[...]).astype(BF16)

    row = lambda n: pl.BlockSpec((tm, n), lambda i: (i, 0))
    vec = pl.BlockSpec((1, rk), lambda i: (0, 0))
    tab = pl.BlockSpec((tm, LANES), lambda i: (i % nsb, 0))
    dc, dqg, dkg = _call(body, name, (t // tm,), [row(w), vec, vec, row(rk), row(rk), row(LANES), tab, tab],
                         [row(w), vec, vec], [_sds((t, w), BF16), _sds((1, rk), F32), _sds((1, rk), F32)])(
        c, q_norm.reshape(1, rk), kv_norm.reshape(1, rk), dqn, dkn, dkp, cos, sin)
    return dc[None], dqg, dkg


def mla_rope_cast(q, kv, cos, sin, seq, name, transpose=False):
    t, w = q.shape
    tm = _tile(seq, 256, SUBLANES)
    nsb = seq // tm
    fn = _rope_t if transpose else _rope

    def body(q_ref, kv_ref, cs_ref, sn_ref, qo_ref, kvo_ref):
        cos_t, sin_t = cs_ref[...], sn_ref[...]
        for blk in range(w // LANES):
            sl = slice(blk * LANES, (blk + 1) * LANES)
            v = q_ref[:, sl]
            if blk % 2 == 1:
                v = fn(v, cos_t, sin_t)
            qo_ref[:, sl] = v.astype(BF16)
        kvo_ref[...] = kv_ref[...].astype(BF16)

    row = pl.BlockSpec((tm, w), lambda i: (i, 0))
    tab = pl.BlockSpec((tm, LANES), lambda i: (i % nsb, 0))
    return _call(body, name, (t // tm,), [row, row, tab, tab], [row, row], [_sds((t, w), BF16)] * 2)(q, kv, cos, sin)


def _causal_mask(i, j, tq, tk):
    qpos = i * tq + lax.broadcasted_iota(jnp.int32, (tq, tk), 0)
    kpos = j * tk + lax.broadcasted_iota(jnp.int32, (tq, tk), 1)
    return kpos <= qpos


def mla_attn_fwd(q_b, kv_b, kp_b, batch, seq, name, tq=256):
    t, w = q_b.shape
    h_n = w // (2 * LANES)
    tq = _tile(seq, tq, SUBLANES)
    nq = seq // tq
    scale = (MLA_NOPE + MLA_ROPE) ** -0.5

    def body(q_ref, kv_ref, kp_ref, o_ref, lse_ref):
        i = pl.program_id(2)
        qn, qp = q_ref[:, :LANES], q_ref[:, LANES:]

        def step(j, carry):
            m, l, acc = carry
            rows = pl.ds(pl.multiple_of(j * tq, tq), tq)
            s = (_dot_nt(qn, kv_ref[rows, :LANES]) + _dot_nt(qp, kp_ref[rows, :])) * scale
            s = jnp.where(_causal_mask(i, j, tq, tq), s, NEG_INF)
            mn = jnp.maximum(m, jnp.max(s, axis=-1, keepdims=True))
            p = jnp.exp(s - mn)
            a = jnp.exp(m - mn)
            return mn, a * l + jnp.sum(p, axis=-1, keepdims=True), a * acc + _dot(p.astype(BF16), kv_ref[rows, LANES:])

        init = (jnp.full((tq, 1), NEG_INF, F32), jnp.zeros((tq, 1), F32), jnp.zeros((tq, LANES), F32))
        m, l, acc = lax.fori_loop(0, i + 1, step, init)
        o_ref[...] = acc / l
        lse_ref[...] = jnp.broadcast_to(m + jnp.log(l), (tq, LANES))

    in_specs = [pl.BlockSpec((None, tq, 2 * LANES), lambda b, h, i: (b, i, h)),
                pl.BlockSpec((None, seq, 2 * LANES), lambda b, h, i: (b, 0, h)),
                pl.BlockSpec((None, seq, LANES), lambda b, h, i: (b, 0, 0))]
    out_spec = pl.BlockSpec((None, tq, LANES), lambda b, h, i: (b, i, h))
    o, lse = _call(body, name, (batch, h_n, nq), in_specs, [out_spec, out_spec],
                   [_sds((batch, seq, h_n * LANES), F32)] * 2)(
        q_b.reshape(batch, seq, w), kv_b.reshape(batch, seq, w), kp_b.reshape(batch, seq, LANES))
    return o.reshape(t, h_n * LANES), lse.reshape(t, h_n * LANES)


def mla_attn_bwd(q_b, kv_b, kp_b, do, lse, delta, batch, seq, name, tq=256):
    t, w = q_b.shape
    h_n = w // (2 * LANES)
    tq = _tile(seq, tq, SUBLANES)
    nq = seq // tq
    scale = (MLA_NOPE + MLA_ROPE) ** -0.5

    def body(q_ref, kv_ref, kp_ref, do_ref, lse_ref, dl_ref, dq_ref, dkv_ref, dkp_ref):
        dq_ref[...] = jnp.zeros_like(dq_ref)
        dkv_ref[...] = jnp.zeros_like(dkv_ref)

        @pl.when(pl.program_id(1) == 0)
        def _():
            dkp_ref[...] = jnp.zeros_like(dkp_ref)

        def outer(j, _):
            krows = pl.ds(pl.multiple_of(j * tq, tq), tq)
            kn, v, kp = kv_ref[krows, :LANES], kv_ref[krows, LANES:], kp_ref[krows, :]

            def inner(i, _):
                qrows = pl.ds(pl.multiple_of(i * tq, tq), tq)
                qn, qp = q_ref[qrows, :LANES], q_ref[qrows, LANES:]
                dout = do_ref[qrows, :].astype(BF16)
                s = (_dot_nt(qn, kn) + _dot_nt(qp, kp)) * scale
                s = jnp.where(_causal_mask(i, j, tq, tq), s, NEG_INF)
                p = jnp.exp(s - lse_ref[qrows, 0:1])
                ds = (p * (_dot_nt(dout, v) - dl_ref[qrows, 0:1]) * scale).astype(BF16)
                dq_ref[qrows, :LANES] += _dot(ds, kn)
                dq_ref[qrows, LANES:] += _dot(ds, kp)
                dkv_ref[krows, :LANES] += _dot_tn(ds, qn)
                dkv_ref[krows, LANES:] += _dot_tn(p.astype(BF16), dout)
                dkp_ref[krows, :] += _dot_tn(ds, qp)
                return 0

            lax.fori_loop(j, nq, inner, 0)
            return 0

        lax.fori_loop(0, nq, outer, 0)

    wide = pl.BlockSpec((None, seq, 2 * LANES), lambda b, h: (b, 0, h))
    head = pl.BlockSpec((None, seq, LANES), lambda b, h: (b, 0, h))
    shared = pl.BlockSpec((None, seq, LANES), lambda b, h: (b, 0, 0))
    hv = [a.reshape(batch, seq, h_n * LANES) for a in (do, lse, delta)]
    dq, dkv, dkp = _call(body, name, (batch, h_n), [wide, wide, shared, head, head, head], [wide, wide, shared],
                         [_sds((batch, seq, w), F32), _sds((batch, seq, w), F32), _sds((batch, seq, LANES), F32)])(
        q_b.reshape(batch, seq, w), kv_b.reshape(batch, seq, w), kp_b.reshape(batch, seq, LANES), *hv)
    return dq.reshape(t, w), dkv.reshape(t, w), dkp.reshape(t, LANES)


def _sgu_common(z_ref, g_ref, b_ref, ws_ref, bs_ref):
    e = z_ref.shape[-1]
    ge = e // SG_GROUPS
    zu, zv = z_ref[0], z_ref[1]
    u = _gelu(zu)
    xh, r = _ln_stats(_gelu(zv))
    vn = (xh * g_ref[...] + b_ref[...]).astype(BF16)
    tri = (lax.broadcasted_iota(jnp.int32, (SG_CHUNK, SG_CHUNK), 0)
           >= lax.broadcasted_iota(jnp.int32, (SG_CHUNK, SG_CHUNK), 1))
    ws = [jnp.where(tri, ws_ref[g], 0.0).astype(BF16) for g in range(SG_GROUPS)]
    v2 = [_dot(ws[g], vn[:, g * ge:(g + 1) * ge]) + bs_ref[:, g:g + 1] for g in range(SG_GROUPS)]
    return zu, zv, u, xh, r, vn, tri, ws, v2


def sgu_fwd(z2, ln_g, ln_b, w_s, b_s_t, name):
    _, t, e = z2.shape
    ge = e // SG_GROUPS

    def body(z_ref, g_ref, b_ref, ws_ref, bs_ref, y_ref):
        _, _, u, _, _, _, _, _, v2 = _sgu_common(z_ref, g_ref, b_ref, ws_ref, bs_ref)
        for g in range(SG_GROUPS):
            y_ref[:, g * ge:(g + 1) * ge] = (u[:, g * ge:(g + 1) * ge] * v2[g]).astype(BF16)

    vec = pl.BlockSpec((1, e), lambda i: (0, 0))
    in_specs = [pl.BlockSpec((2, SG_CHUNK, e), lambda i: (0, i, 0)), vec, vec,
                pl.BlockSpec((SG_GROUPS, SG_CHUNK, SG_CHUNK), lambda i: (0, 0, 0)),
                pl.BlockSpec((SG_CHUNK, SG_GROUPS), lambda i: (0, 0))]
    return _call(body, name, (t // SG_CHUNK,), in_specs, pl.BlockSpec((SG_CHUNK, e), lambda i: (i, 0)),
                 _sds((t, e), BF16))(z2, ln_g.reshape(1, e), ln_b.reshape(1, e), w_s, b_s_t)


def sgu_bwd(z2, dy, ln_g, ln_b, w_s, b_s_t, name):
    _, t, e = z2.shape
    ge = e // SG_GROUPS

    def body(z_ref, dy_ref, g_ref, b_ref, ws_ref, bs_ref, dz_ref, dbi_ref, dg_ref, db_ref, dws_ref, dbs_ref):
        @pl.when(pl.program_id(0) == 0)
        def _():
            for ref in (dbi_ref, dg_ref, db_ref, dws_ref, dbs_ref):
                ref[...] = jnp.zeros_like(ref)

        zu, zv, u, xh, r, vn, tri, ws, v2 = _sgu_common(z_ref, g_ref, b_ref, ws_ref, bs_ref)
        dy = dy_ref[...]
        dvn = []
        for g in range(SG_GROUPS):
            sl = slice(g * ge, (g + 1) * ge)
            dyg = dy[:, sl]
            du = dyg * v2[g] * _gelu_grad(zu[:, sl])
            dz_ref[0, :, sl] = du.astype(BF16)
            dbi_ref[0, :, sl] += jnp.sum(du, axis=0, keepdims=True)
            dv2 = dyg * u[:, sl]
            dbs_ref[:, g:g + 1] += jnp.sum(dv2, axis=-1, keepdims=True)
            dv2b = dv2.astype(BF16)
            dws_ref[g] += jnp.where(tri, _dot_nt(dv2b, vn[:, sl]), 0.0)
            dvn.append(_dot_tn(ws[g], dv2b))
        dvn = jnp.concatenate(dvn, axis=-1)
        dv, dg, db = _ln_bwd(dvn, xh, r, g_ref[...])
        dzv = dv * _gelu_grad(zv)
        dz_ref[1] = dzv.astype(BF16)
        dbi_ref[1] += jnp.sum(dzv, axis=0, keepdims=True)
        dg_ref[...] += dg
        db_ref[...] += db

    vec = pl.BlockSpec((1, e), lambda i: (0, 0))
    wsb = pl.BlockSpec((SG_GROUPS, SG_CHUNK, SG_CHUNK), lambda i: (0, 0, 0))
    bsb = pl.BlockSpec((SG_CHUNK, SG_GROUPS), lambda i: (0, 0))
    zblk = pl.BlockSpec((2, SG_CHUNK, e), lambda i: (0, i, 0))
    in_specs = [zblk, pl.BlockSpec((SG_CHUNK, e), lambda i: (i, 0)), vec, vec, wsb, bsb]
    out_specs = [zblk, pl.BlockSpec((2, 1, e), lambda i: (0, 0, 0)), vec, vec, wsb, bsb]
    out_shape = [_sds((2, t, e), BF16), _sds((2, 1, e), F32), _sds((1, e), F32), _sds((1, e), F32),
                 _sds((SG_GROUPS, SG_CHUNK, SG_CHUNK), F32), _sds((SG_CHUNK, SG_GROUPS), F32)]
    return _call(body, name, (t // SG_CHUNK,), in_specs, out_specs, out_shape)(
        z2, dy, ln_g.reshape(1, e), ln_b.reshape(1, e), w_s, b_s_t)


_COL = ("cc_w_in", "ffn_w_up", "dsa_w_qkv", "dsa_w_o", "mla_w_qb", "mla_w_kvb", "sg_w_in")
_ROW = ("cc_w_out", "ffn_w_down", "mla_w_in", "mla_w_o", "sg_w_out")
_SMALL_COL = ("cc_dw_w", "ffn_dw_w")
_MIXERS = ("cc", "dsa", "mla", "sg")


def _base(name):
    return name.split("_", 1)[1] if name[0] == "l" and name[1].isdigit() else name


def _ffn_fwd(pre, p, full, h_prev, mix_out, bsz, seq):
    t, d = h_prev.shape
    h, hn = rms_fwd(h_prev, p[pre + "norm_ffn"], pre + "ffn_norm", add=mix_out)
    w_up, w_down = full[pre + "ffn_w_up"], full[pre + "ffn_w_down"]
    f = w_down.shape[1]
    z2r = mm_nn(hn[None], w_up, pre + "ffn_up", out_parts=2).reshape(2, bsz, seq, f)
    dw_w2 = jnp.transpose(full[pre + "ffn_dw_w"].reshape(FFN_WIDTH, 2, f), (1, 0, 2))
    dw_b2 = p[pre + "ffn_dw_b"].reshape(2, 1, f)
    y = ffn_gate_fwd(z2r, dw_w2, dw_b2, pre + "ffn_gate").reshape(1, t, f)
    out = mm_nn(y, w_down, pre + "ffn_down")[0]
    return out, dict(h=h, hn=hn, z2r=z2r, y=y, dw_w2=dw_w2, dw_b2=dw_b2)


def _ffn_bwd(pre, p, full, ctx, dh_out, big_g, small_g, bsz, seq, mid):
    t, d = dh_out.shape
    w_up, w_down = full[pre + "ffn_w_up"], full[pre + "ffn_w_down"]
    f = w_down.shape[1]
    dh3 = dh_out[None]
    dyv = mm_nt(dh3, w_down, pre + "ffn_down_dx")
    big_g[pre + "ffn_w_down"] = mm_tn(ctx["y"], dh3, 1, pre + "ffn_down_dw").reshape(N_DEV, f // N_DEV, d)
    mid(dyv)
    dz2, ddw2, ddb2 = ffn_gate_bwd(ctx["z2r"], dyv.reshape(bsz, seq, f), ctx["dw_w2"], ctx["dw_b2"],
                                   pre + "ffn_gate_bwd")
    dz2 = dz2.reshape(2, t, f)
    big_g[pre + "ffn_w_up"] = mm_tn(ctx["hn"][None], dz2, N_DEV, pre + "ffn_up_dw")
    dhn = mm_nt(dz2, w_up, pre + "ffn_up_dx")
    small_g[pre + "ffn_dw_w"] = jnp.transpose(ddw2, (1, 0, 2)).reshape(FFN_WIDTH, 2 * f)
    small_g[pre + "ffn_dw_b"] = ddb2.reshape(2 * f)
    dh, dg = rms_bwd(ctx["h"], p[pre + "norm_ffn"], dhn, dh_out, pre + "ffn_norm_bwd")
    small_g[pre + "norm_ffn"] = dg.reshape(d)
    return dh


def _cc_fwd(pre, p, full, hn, bsz, seq, aux):
    t, c = hn.shape
    z2r = mm_nn(hn[None], full[pre + "cc_w_in"], pre + "cc_in", bias=p[pre + "cc_b_in"], out_parts=2).reshape(
        2, bsz, seq, c)
    cv = cc_conv_fwd(z2r, full[pre + "cc_dw_w"], p[pre + "cc_dw_b"], pre + "cc_conv").reshape(t, c)
    s = cc_ln_fwd(cv, p[pre + "cc_ln_g"], p[pre + "cc_ln_b"], pre + "cc_ln")
    out = mm_nn(s[None], full[pre + "cc_w_out"], pre + "cc_out", bias=p[pre + "cc_b_out"])[0]
    return out, dict(z2r=z2r, cv=cv, s=s)


def _cc_bwd(pre, p, full, ctx, hn, dm, big_g, small_g, bsz, seq, aux, mid):
    t, c = dm.shape
    dm3 = dm[None]
    small_g[pre + "cc_b_out"] = col_sum(dm, pre + "cc_bout_g").reshape(c)
    ds = mm_nt(dm3, full[pre + "cc_w_out"], pre + "cc_out_dx")
    big_g[pre + "cc_w_out"] = mm_tn(ctx["s"][None], dm3, 1, pre + "cc_out_dw").reshape(N_DEV, c // N_DEV, c)
    mid(ds)
    dcv, dlg, dlb = cc_ln_bwd(ctx["cv"], p[pre + "cc_ln_g"], p[pre + "cc_ln_b"], ds, pre + "cc_ln_bwd")
    dz2, dbi, ddw, ddb = cc_conv_bwd(ctx["z2r"], dcv.reshape(bsz, seq, c), full[pre + "cc_dw_w"], pre + "cc_conv_bwd")
    dz2 = dz2.reshape(2, t, c)
    big_g[pre + "cc_w_in"] = mm_tn(hn[None], dz2, N_DEV, pre + "cc_in_dw")
    small_g[pre + "cc_ln_g"], small_g[pre + "cc_ln_b"] = dlg.reshape(c), dlb.reshape(c)
    small_g[pre + "cc_b_in"], small_g[pre + "cc_dw_w"], small_g[pre + "cc_dw_b"] = dbi.reshape(2 * c), ddw, ddb.reshape(c)
    return mm_nt(dz2, full[pre + "cc_w_in"], pre + "cc_in_dx")


def _dsa_fwd(pre, p, full, hn, bsz, seq, aux):
    cos, sin = aux["dsa_tables"]
    qkv = mm_nn(hn[None], full[pre + "dsa_w_qkv"], pre + "dsa_qkv")[0]
    qkv_b = dsa_rope_fwd(qkv, cos, sin, seq, pre + "dsa_rope")
    outs, lses = [], []
    for grp, (window, dil) in enumerate(DSA_CONFIGS):
        assert window // dil == DSA_BLOCK and (seq // dil) % DSA_BLOCK == 0
        o, lse = dsa_attn_fwd(qkv_b, grp, dil, bsz, seq, pre + "dsa_attn%d" % grp)
        outs.append(o)
        lses.append(lse)
    o, lse = dsa_merge(outs, lses, pre + "dsa_merge")
    out = mm_nn(o[None], full[pre + "dsa_w_o"], pre + "dsa_o")[0]
    return out, dict(qkv_b=qkv_b, o=o, lse=lse)


def _dsa_bwd(pre, p, full, ctx, hn, dm, big_g, small_g, bsz, seq, aux, mid):
    cos, sin = aux["dsa_tables"]
    dm3 = dm[None]
    do = mm_nt(dm3, full[pre + "dsa_w_o"], pre + "dsa_o_dx")
    big_g[pre + "dsa_w_o"] = mm_tn(ctx["o"][None], dm3, N_DEV, pre + "dsa_o_dw")
    mid(do)
    delta = head_delta(do, ctx["o"], pre + "dsa_delta")
    grads = [dsa_attn_bwd(ctx["qkv_b"], do, ctx["lse"], delta, grp, dil, bsz, seq, pre + "dsa_attn_bwd%d" % grp)
             for grp, (_, dil) in enumerate(DSA_CONFIGS)]
    dqkv = dsa_rope_bwd(grads, cos, sin, seq, pre + "dsa_rope_bwd")
    big_g[pre + "dsa_w_qkv"] = mm_tn(hn[None], dqkv, N_DEV, pre + "dsa_qkv_dw")
    return mm_nt(dqkv, full[pre + "dsa_w_qkv"], pre + "dsa_qkv_dx")


def _mla_weights(pre, full):
    rk, hd = MLA_RANK, MLA_NOPE + MLA_ROPE
    w_in = full[pre + "mla_w_in"][0]
    w_in_p = jnp.concatenate([w_in[:, :2 * rk], _spread_rope_cols(w_in[:, 2 * rk:])], axis=1)[None]
    w_qb = jnp.transpose(full[pre + "mla_w_qb"], (1, 0, 2)).reshape(rk, MLA_HEADS, hd)
    w_qb_p = jnp.concatenate([w_qb[..., :MLA_NOPE], _spread_rope_cols(w_qb[..., MLA_NOPE:])], axis=-1)
    return w_in_p, w_qb_p.reshape(1, rk, MLA_HEADS * 2 * LANES)


def _mla_fwd(pre, p, full, hn, bsz, seq, aux):
    cos, sin = aux["mla_tables"]
    w_in_p, w_qb_p = _mla_weights(pre, full)
    c = mm_nn(hn[None], w_in_p, pre + "mla_in")[0]
    qn, kn, kp = mla_low_fwd(c, p[pre + "mla_q_norm"], p[pre + "mla_kv_norm"], cos, sin, seq, pre + "mla_low")
    q = mm_nn(qn[None], w_qb_p, pre + "mla_qb")[0]
    kv = mm_nn(kn[None], full[pre + "mla_w_kvb"], pre + "mla_kvb")[0]
    q_b, kv_b = mla_rope_cast(q, kv, cos, sin, seq, pre + "mla_rope")
    o, lse = mla_attn_fwd(q_b, kv_b, kp, bsz, seq, pre + "mla_attn")
    out = mm_nn(o[None], full[pre + "mla_w_o"], pre + "mla_o")[0]
    return out, dict(c=c, qn=qn, kn=kn, kp=kp, q_b=q_b, kv_b=kv_b, o=o, lse=lse, w_in_p=w_in_p, w_qb_p=w_qb_p)


def _mla_bwd(pre, p, full, ctx, hn, dm, big_g, small_g, bsz, seq, aux, mid):
    cos, sin = aux["mla_tables"]
    t, d = dm.shape
    rk = MLA_RANK
    dm3 = dm[None]
    do = mm_nt(dm3, full[pre + "mla_w_o"], pre + "mla_o_dx")
    big_g[pre + "mla_w_o"] = mm_tn(ctx["o"][None], dm3, 1, pre + "mla_o_dw").reshape(N_DEV, -1, d)
    mid(do)
    delta = head_delta(do, ctx["o"], pre + "mla_delta")
    dq, dkv, dkp = mla_attn_bwd(ctx["q_b"], ctx["kv_b"], ctx["kp"], do, ctx["lse"], delta, bsz, seq, pre + "mla_attn_bwd")
    dq_b, dkv_b = mla_rope_cast(dq, dkv, cos, sin, seq, pre + "mla_rope_bwd", transpose=True)
    g_qb = mm_tn(ctx["qn"][None], dq_b[None], 1, pre + "mla_qb_dw")[0].reshape(rk, MLA_HEADS, 2 * LANES)
    g_qb = jnp.concatenate([g_qb[..., :MLA_NOPE], _gather_rope_cols(g_qb[..., MLA_NOPE:])], axis=-1)
    big_g[pre + "mla_w_qb"] = jnp.transpose(g_qb.reshape(rk, N_DEV, -1), (1, 0, 2))
    dqn = mm_nt(dq_b[None], ctx["w_qb_p"], pre + "mla_qb_dx")
    big_g[pre + "mla_w_kvb"] = mm_tn(ctx["kn"][None], dkv_b[None], N_DEV, pre + "mla_kvb_dw")
    dkn = mm_nt(dkv_b[None], full[pre + "mla_w_kvb"], pre + "mla_kvb_dx")
    dc, dqg, dkg = mla_low_bwd(ctx["c"], p[pre + "mla_q_norm"], p[pre + "mla_kv_norm"], dqn, dkn, dkp, cos, sin, seq,
                               pre + "mla_low_bwd")
    small_g[pre + "mla_q_norm"], small_g[pre + "mla_kv_norm"] = dqg.reshape(rk), dkg.reshape(rk)
    g_in = mm_tn(hn[None], dc, 1, pre + "mla_in_dw")[0]
    g_in = jnp.concatenate([g_in[:, :2 * rk], _gather_rope_cols(g_in[:, 2 * rk:])], axis=1)
    big_g[pre + "mla_w_in"] = g_in.reshape(N_DEV, d // N_DEV, -1)
    return mm_nt(dc, ctx["w_in_p"], pre + "mla_in_dx")


def _sg_fwd(pre, p, full, hn, bsz, seq, aux):
    z2 = mm_nn(hn[None], full[pre + "sg_w_in"], pre + "sg_in", bias=p[pre + "sg_b_in"], out_parts=2)
    b_s_t = jnp.transpose(p[pre + "sg_b_s"])
    y = sgu_fwd(z2, p[pre + "sg_ln_g"], p[pre + "sg_ln_b"], p[pre + "sg_w_s"], b_s_t, pre + "sg_mix")
    out = mm_nn(y[None], full[pre + "sg_w_out"], pre + "sg_out", bias=p[pre + "sg_b_out"])[0]
    return out, dict(z2=z2, y=y, b_s_t=b_s_t)


def _sg_bwd(pre, p, full, ctx, hn, dm, big_g, small_g, bsz, seq, aux, mid):
    t, d = dm.shape
    dm3 = dm[None]
    small_g[pre + "sg_b_out"] = col_sum(dm, pre + "sg_bout_g").reshape(d)
    dy = mm_nt(dm3, full[pre + "sg_w_out"], pre + "sg_out_dx")
    big_g[pre + "sg_w_out"] = mm_tn(ctx["y"][None], dm3, 1, pre + "sg_out_dw").reshape(N_DEV, -1, d)
    mid(dy)
    dz2, dbi, dlg, dlb, dws, dbs_t = sgu_bwd(ctx["z2"], dy, p[pre + "sg_ln_g"], p[pre + "sg_ln_b"], p[pre + "sg_w_s"],
                                             ctx["b_s_t"], pre + "sg_mix_bwd")
    big_g[pre + "sg_w_in"] = mm_tn(hn[None], dz2, N_DEV, pre + "sg_in_dw")
    small_g[pre + "sg_b_in"] = dbi.reshape(-1)
    small_g[pre + "sg_ln_g"], small_g[pre + "sg_ln_b"] = dlg.reshape(-1), dlb.reshape(-1)
    small_g[pre + "sg_w_s"], small_g[pre + "sg_b_s"] = dws, jnp.transpose(dbs_t)
    return mm_nt(dz2, full[pre + "sg_w_in"], pre + "sg_in_dx")


_MIX_FWD = dict(cc=_cc_fwd, dsa=_dsa_fwd, mla=_mla_fwd, sg=_sg_fwd)
_MIX_BWD = dict(cc=_cc_bwd, dsa=_dsa_bwd, mla=_mla_bwd, sg=_sg_bwd)


def _reduce_scatter(parts, core, chip, tag, adams=None):
    recv = rs_core_exchange(parts, tag + "_cores")
    sums = [rs_core_add(pt, rc, core, "%s_add%d" % (tag, i)) for i, (pt, rc) in enumerate(zip(parts, recv))]
    recv2 = rs_chip_exchange(sums, tag + "_chips")
    return [rs_finish(s, r2, chip, "%s_fin%d" % (tag, i), adam=None if adams is None else adams[i])
            for i, (s, r2) in enumerate(zip(sums, recv2))]


class _Schedule:
    def __init__(self):
        self.actions = {}

    def at(self, key, fn):
        self.actions.setdefault(key, []).append(fn)

    def run(self, key, x):
        for fn in self.actions.pop(key, []):
            fn(x)


def _aside(fn):
    saved = list(_PENDING)
    _PENDING.clear()
    out = fn()
    _PENDING[:0] = saved
    return out


def _gather_overlapped(sched, keys, shards, dev, tag, deliver):
    n = len(shards)
    state = {}

    def start(x):
        lands = [lax.dynamic_update_slice(lax.empty((N_DEV,) + s.shape, s.dtype), s[None], (dev,) + (0,) * s.ndim)
                 for s in shards]
        state["first"] = exchange_start(_ag_plan_first, 4 * n, shards, lands, [x], tag + "_start")
        _PENDING.append(state["first"][4])

    def pass_on(x):
        _, lands = exchange_wait(_ag_plan_first, state["first"], [x], tag + "_wait")
        state["pass"] = exchange_start(_ag_plan_pass, 3 * n, [], lands, [], tag + "_pass")
        _PENDING.append(state["pass"][4])

    def done(x):
        deliver(exchange_wait(_ag_plan_pass, state["pass"], [x], tag + "_done")[1])

    for key, fn in zip(keys, (start, pass_on, done)):
        sched.at(key, fn)


def _reduce_overlapped(sched, keys, parts, adams, core, chip, tag, deliver):
    n = len(parts)
    state = {}

    def start(x):
        lands = [lax.empty((4,) + pt.shape[1:], pt.dtype) for pt in parts]
        state["cores"] = exchange_start(_rs_plan_cores, 4 * n, parts, lands, [x], tag + "_start")
        _PENDING.append(state["cores"][4])

    def middle(x):
        mine, recv = exchange_wait(_rs_plan_cores, state["cores"], [x], tag + "_wait")
        sums = _aside(lambda: [rs_core_add(pt, rc, core, "%s_add%d" % (tag, i))
                               for i, (pt, rc) in enumerate(zip(mine, recv))])
        lands = [lax.empty((3,) + s.shape[1:], s.dtype) for s in sums]
        state["chips"] = exchange_start(_rs_plan_chips, 3 * n, sums, lands, [], tag + "_send")
        _PENDING.append(state["chips"][4])

    def done(x):
        sums, recv = exchange_wait(_rs_plan_chips, state["chips"], [x], tag + "_done")
        deliver(_aside(lambda: [rs_finish(s, r2, chip, "%s_fin%d" % (tag, i), adam=adams[i])
                                for i, (s, r2) in enumerate(zip(sums, recv))]))

    for key, fn in zip(keys, (start, middle, done)):
        sched.at(key, fn)


def _pack(arrays):
    flat = jnp.concatenate([a.reshape(-1) for a in arrays])
    unit = N_DEV * SUBLANES * LANES
    padded = -(-flat.shape[0] // unit) * unit
    return jnp.pad(flat, (0, padded - flat.shape[0]))


def _unpack(flat, like):
    out, pos = [], 0
    for a in like:
        out.append(flat[pos:pos + a.size].reshape(a.shape))
        pos += a.size
    return out


def _train_step(p):
    names = list(p)
    wnames = names[1:names.index("loss_target")]
    x = p["x"]
    bsz, seq, d = x.shape
    t = bsz * seq
    xi, yi, ci = lax.axis_index("x"), lax.axis_index("y"), lax.axis_index("c")
    core = jnp.reshape(ci, (1,)).astype(jnp.int32)
    chip = jnp.reshape(2 * xi + yi, (1,)).astype(jnp.int32)
    dev = 4 * xi + 2 * yi + ci
    n_layers = 1 + max(int(n[1]) for n in wnames if n[0] == "l" and n[1].isdigit())
    big = [n for n in wnames if _base(n) in _COL + _ROW]
    aux = dict(dsa_tables=rope_tables_full(seq, LANES), mla_tables=mla_rope_tables(seq))

    sched = _Schedule()
    full = {}

    def deliver_weights(grp):
        def deliver(lands):
            for n, g in zip(grp, lands):
                if _base(n) in _COL:
                    full[n] = g
                elif _base(n) in _ROW:
                    full[n] = g.reshape(1, N_DEV * g.shape[1], g.shape[2])
                else:
                    full[n] = jnp.transpose(g, (1, 0, 2)).reshape(g.shape[1], N_DEV * g.shape[2])
        return deliver

    groups = []
    for layer in range(n_layers):
        lw = [n for n in wnames if n.startswith("l%d_" % layer) and _base(n) in _COL + _ROW + _SMALL_COL]
        groups += [[n for n in lw if "_ffn_" not in n], [n for n in lw if "_ffn_" in n]]

    def shards_of(grp):
        return [p[n] if _base(n) in _SMALL_COL else p[n].astype(BF16) for n in grp]

    deliver_weights(groups[0])(all_gather(shards_of(groups[0]), "ag_first"))
    for s in range(1, len(groups)):
        first = max(s - 2, 0)
        keys = [("f", first), ("f", s - 1 if s - 1 > first else s), ("f", s)]
        _gather_overlapped(sched, keys, shards_of(groups[s]), dev, "ag_s%d" % s, deliver_weights(groups[s]))

    h = x.reshape(t, d)
    ffn_out = None
    ctxs = []
    for layer in range(n_layers):
        pre = "l%d_" % layer
        kind = _MIXERS[layer % len(_MIXERS)]
        if ffn_out is None:
            hn = rms_fwd(h, p[pre + "norm_mix"], pre + "mix_norm")
        else:
            h, hn = rms_fwd(h, p[pre + "norm_mix"], pre + "mix_norm", add=ffn_out)
        sched.run(("f", 2 * layer), hn)
        mix_out, mctx = _MIX_FWD[kind](pre, p, full, hn, bsz, seq, aux)
        sched.run(("f", 2 * layer + 1), mix_out)
        ffn_out, fctx = _ffn_fwd(pre, p, full, h, mix_out, bsz, seq)
        ctxs.append((h, hn, mctx, fctx))
        h = fctx["h"]

    loss_row, dh, dfin = final_loss(h, ffn_out, p["final_norm"], p["loss_target"].reshape(t, d), "final_loss")
    loss = lax.psum(loss_row[0, 0], ("x", "y", "c"))

    small_g = {"final_norm": dfin.reshape(d)}
    results = {}

    def deliver_grads(lnames):
        def deliver(outs):
            for n, o in zip(lnames, outs):
                results[n] = o
        return deliver

    def reduce_later(q, lnames, big_g, tag):
        keys = [("b", q + 2), ("b", q + 3), ("b", q + 5)]
        _reduce_overlapped(sched, keys, [big_g[n] for n in lnames], [(p[n], p["m_" + n], p["v_" + n]) for n in lnames],
                           core, chip, tag, deliver_grads(lnames))

    q = 0
    for layer in reversed(range(n_layers)):
        pre = "l%d_" % layer
        kind = _MIXERS[layer % len(_MIXERS)]
        h_in, hn, mctx, fctx = ctxs[layer]
        big_g = {}
        sched.run(("b", q), dh)
        dh = _ffn_bwd(pre, p, full, fctx, dh, big_g, small_g, bsz, seq, lambda v, q=q: sched.run(("b", q + 1), v))
        reduce_later(q, [n for n in big if n.startswith(pre) and "_ffn_" in n], big_g, "rs_l%d_ffn" % layer)
        q += 2
        sched.run(("b", q), dh)
        dhn = _MIX_BWD[kind](pre, p, full, mctx, hn, dh, big_g, small_g, bsz, seq, aux,
                             lambda v, q=q: sched.run(("b", q + 1), v))
        dh, dg = rms_bwd(h_in, p[pre + "norm_mix"], dhn, dh, pre + "mix_norm_bwd")
        small_g[pre + "norm_mix"] = dg.reshape(d)
        reduce_later(q, [n for n in big if n.startswith(pre) and "_ffn_" not in n], big_g, "rs_l%d_mix" % layer)
        q += 2
    for tail in range(q, q + 4):
        sched.run(("b", tail), dh)
    assert not sched.actions, sched.actions.keys()

    small = [n for n in wnames if n not in big]
    packed = _pack([small_g[n] for n in small]).reshape(N_DEV, -1, LANES)
    reduced = _reduce_scatter([packed], core, chip, "rs_small")[0]
    gathered = all_gather([reduced], "ag_small")[0].reshape(-1)
    g_local = []
    for n, g in zip(small, _unpack(gathered, [small_g[n] for n in small])):
        if _base(n) in _SMALL_COL:
            width = p[n].shape[1]
            g = lax.dynamic_slice_in_dim(g, dev * width, width, axis=1)
        g_local.append(g)
    stacked = [_pack(v).reshape(-1, LANES) for v in
               ([p[n] for n in small], g_local, [p["m_" + n] for n in small], [p["v_" + n] for n in small])]
    upd = adam_flat(*stacked, "adam_small")
    upd = [_unpack(u.reshape(-1), g_local) for u in upd]
    for i, n in enumerate(small):
        results[n] = (g_local[i], upd[0][i], upd[1][i], upd[2][i])

    grad_x = dh.reshape(bsz, seq, d)
    cols = [[results[n][k] for n in wnames] for k in range(4)]
    return (loss, grad_x, *cols[0], *cols[1], *cols[2], *cols[3])


def kernel(x, l0_norm_mix, l0_cc_w_in, l0_cc_b_in, l0_cc_dw_w, l0_cc_dw_b, l0_cc_ln_g, l0_cc_ln_b, l0_cc_w_out, l0_cc_b_out, l0_norm_ffn, l0_ffn_w_up, l0_ffn_dw_w, l0_ffn_dw_b, l0_ffn_w_down, l1_norm_mix, l1_dsa_w_qkv, l1_dsa_w_o, l1_norm_ffn, l1_ffn_w_up, l1_ffn_dw_w, l1_ffn_dw_b, l1_ffn_w_down, l2_norm_mix, l2_mla_w_in, l2_mla_q_norm, l2_mla_w_qb, l2_mla_kv_norm, l2_mla_w_kvb, l2_mla_w_o, l2_norm_ffn, l2_ffn_w_up, l2_ffn_dw_w, l2_ffn_dw_b, l2_ffn_w_down, l3_norm_mix, l3_sg_w_in, l3_sg_b_in, l3_sg_ln_g, l3_sg_ln_b, l3_sg_w_s, l3_sg_b_s, l3_sg_w_out, l3_sg_b_out, l3_norm_ffn, l3_ffn_w_up, l3_ffn_dw_w, l3_ffn_dw_b, l3_ffn_w_down, final_norm, loss_target, m_l0_norm_mix, m_l0_cc_w_in, m_l0_cc_b_in, m_l0_cc_dw_w, m_l0_cc_dw_b, m_l0_cc_ln_g, m_l0_cc_ln_b, m_l0_cc_w_out, m_l0_cc_b_out, m_l0_norm_ffn, m_l0_ffn_w_up, m_l0_ffn_dw_w, m_l0_ffn_dw_b, m_l0_ffn_w_down, m_l1_norm_mix, m_l1_dsa_w_qkv, m_l1_dsa_w_o, m_l1_norm_ffn, m_l1_ffn_w_up, m_l1_ffn_dw_w, m_l1_ffn_dw_b, m_l1_ffn_w_down, m_l2_norm_mix, m_l2_mla_w_in, m_l2_mla_q_norm, m_l2_mla_w_qb, m_l2_mla_kv_norm, m_l2_mla_w_kvb, m_l2_mla_w_o, m_l2_norm_ffn, m_l2_ffn_w_up, m_l2_ffn_dw_w, m_l2_ffn_dw_b, m_l2_ffn_w_down, m_l3_norm_mix, m_l3_sg_w_in, m_l3_sg_b_in, m_l3_sg_ln_g, m_l3_sg_ln_b, m_l3_sg_w_s, m_l3_sg_b_s, m_l3_sg_w_out, m_l3_sg_b_out, m_l3_norm_ffn, m_l3_ffn_w_up, m_l3_ffn_dw_w, m_l3_ffn_dw_b, m_l3_ffn_w_down, m_final_norm, v_l0_norm_mix, v_l0_cc_w_in, v_l0_cc_b_in, v_l0_cc_dw_w, v_l0_cc_dw_b, v_l0_cc_ln_g, v_l0_cc_ln_b, v_l0_cc_w_out, v_l0_cc_b_out, v_l0_norm_ffn, v_l0_ffn_w_up, v_l0_ffn_dw_w, v_l0_ffn_dw_b, v_l0_ffn_w_down, v_l1_norm_mix, v_l1_dsa_w_qkv, v_l1_dsa_w_o, v_l1_norm_ffn, v_l1_ffn_w_up, v_l1_ffn_dw_w, v_l1_ffn_dw_b, v_l1_ffn_w_down, v_l2_norm_mix, v_l2_mla_w_in, v_l2_mla_q_norm, v_l2_mla_w_qb, v_l2_mla_kv_norm, v_l2_mla_w_kvb, v_l2_mla_w_o, v_l2_norm_ffn, v_l2_ffn_w_up, v_l2_ffn_dw_w, v_l2_ffn_dw_b, v_l2_ffn_w_down, v_l3_norm_mix, v_l3_sg_w_in, v_l3_sg_b_in, v_l3_sg_ln_g, v_l3_sg_ln_b, v_l3_sg_w_s, v_l3_sg_b_s, v_l3_sg_w_out, v_l3_sg_b_out, v_l3_norm_ffn, v_l3_ffn_w_up, v_l3_ffn_dw_w, v_l3_ffn_dw_b, v_l3_ffn_w_down, v_final_norm):
    return _train_step(dict(locals()))
```

```python
import functools
import inspect
import math

import jax
import jax.numpy as jnp
from jax import lax
from jax.experimental import pallas as pl
from jax.experimental.pallas import tpu as pltpu

F32 = jnp.float32
BF16 = jnp.bfloat16
NORM_EPS = 1e-6
NEG_INF = -1e30
ROPE_THETA = 10000.0
LANES = 128
SUBLANES = 8
VMEM_LIMIT = 56 * 1024 * 1024
MM_VMEM_BUDGET = 44 * 1024 * 1024
N_DEV = 8

DSA_CONFIGS = ((128, 1), (512, 4), (2048, 16))
DSA_HEADS = 8
DSA_BLOCK = 128
MLA_HEADS = 16
MLA_RANK = 512
MLA_NOPE = 128
MLA_ROPE = 64
SG_CHUNK = 128
SG_GROUPS = 8
CC_WIDTH = 31
FFN_WIDTH = 3
ADAM_LR, ADAM_B1, ADAM_B2, ADAM_EPS, ADAM_WD, ADAM_STEP = 0.001, 0.9, 0.999, 1e-08, 0.01, 10


def _tile(n, cap, unit=LANES):
    if n <= cap:
        return n
    best = 0
    for t in range(unit, cap + 1, unit):
        if n % t == 0:
            best = t
    assert best, (n, cap, unit)
    return best


_PENDING = []


def _call(body, name, grid, in_specs, out_specs, out_shape, scratch=(), prefetch=0):
    params = pltpu.CompilerParams(vmem_limit_bytes=VMEM_LIMIT)
    deps = list(_PENDING)
    _PENDING.clear()
    if deps:
        inner, n_in = body, prefetch + len(in_specs)

        def body(*refs):
            return inner(*refs[:n_in], *refs[n_in + len(deps):])

        in_specs = list(in_specs) + [pl.BlockSpec(memory_space=pl.ANY)] * len(deps)
    if prefetch:
        spec = pltpu.PrefetchScalarGridSpec(num_scalar_prefetch=prefetch, grid=grid, in_specs=in_specs,
                                            out_specs=out_specs, scratch_shapes=list(scratch))
        fn = pl.pallas_call(body, out_shape=out_shape, grid_spec=spec, name=name, compiler_params=params)
    else:
        fn = pl.pallas_call(body, out_shape=out_shape, grid=grid, in_specs=in_specs, out_specs=out_specs,
                            scratch_shapes=list(scratch), name=name, compiler_params=params)
    return lambda *args: fn(*args, *deps)


def _sds(shape, dtype):
    return jax.ShapeDtypeStruct(tuple(shape), dtype)


def _bf(v):
    return v if v.dtype == BF16 else v.astype(BF16)


def _dot(a, b):
    return jnp.dot(a, b, preferred_element_type=F32)


def _dot_nt(a, b):
    return lax.dot_general(a, b, (((1,), (1,)), ((), ())), preferred_element_type=F32)


def _dot_tn(a, b):
    return lax.dot_general(a, b, (((0,), (0,)), ((), ())), preferred_element_type=F32)


def _sigmoid(v):
    return 1.0 / (1.0 + jnp.exp(-v))


_ERF_A = (-2.72614225801306e-10, 2.77068142495902e-08, -2.10102402082508e-06, -5.69250639462346e-05,
          -7.34990630326855e-04, -2.95459980854025e-03, -1.60960333262415e-02)
_ERF_B = (-1.45660718464996e-05, -2.13374055278905e-04, -1.68282697438203e-03, -7.37332916720468e-03,
          -1.42647390514189e-02)


def _erf(v):
    v = jnp.clip(v, -4.0, 4.0)
    v2 = v * v
    p = jnp.full_like(v, _ERF_A[0])
    for c in _ERF_A[1:]:
        p = p * v2 + c
    q = jnp.full_like(v, _ERF_B[0])
    for c in _ERF_B[1:]:
        q = q * v2 + c
    return v * p / q


def _gelu(v):
    return 0.5 * v * (1.0 + _erf(v * (2.0 ** -0.5)))


def _gelu_grad(v):
    return 0.5 * (1.0 + _erf(v * (2.0 ** -0.5))) + v * jnp.exp(-0.5 * v * v) * ((2.0 * math.pi) ** -0.5)


def _shift_down(v, k):
    if k == 0:
        return v
    rows = lax.broadcasted_iota(jnp.int32, v.shape, 0)
    return jnp.where(rows >= k, pltpu.roll(v, k, 0), 0.0)


def _shift_up(v, k):
    if k == 0:
        return v
    n = v.shape[0]
    rows = lax.broadcasted_iota(jnp.int32, v.shape, 0)
    return jnp.where(rows < n - k, pltpu.roll(v, n - k, 0), 0.0)


def _rope(v, cos, sin):
    return v * cos + pltpu.roll(v, 64, 1) * sin


def _rope_t(dv, cos, sin):
    return dv * cos + pltpu.roll(dv * sin, 64, 1)


def mm_nn(a, w, name, bias=None, out_parts=1, out_dtype=F32, tm_cap=512, tn_cap=1536, tk_cap=6144):
    pa, m, kp = a.shape
    j, k, n = w.shape
    assert pa * kp == k
    big_n = j * n
    npo = big_n // out_parts
    tm = _tile(m, tm_cap, SUBLANES)
    tk = _tile(kp, tk_cap)
    tn = _tile(math.gcd(n, npo), tn_cap if tk <= 2816 else 1024)
    npj, nbo, kbp, nk = n // tn, npo // tn, kp // tk, k // tk
    has_bias = bias is not None

    def body(*refs):
        a_ref, w_ref = refs[0], refs[1]
        b_ref = refs[2] if has_bias else None
        o_ref = refs[2 + has_bias]

        def finish(acc):
            if has_bias:
                acc = acc + b_ref[...]
            o_ref[...] = acc.astype(o_ref.dtype)

        part = _dot(_bf(a_ref[...]), _bf(w_ref[...]))
        if nk == 1:
            finish(part)
        else:
            acc_ref = refs[3 + has_bias]
            ki = pl.program_id(2)

            @pl.when(ki == 0)
            def _():
                acc_ref[...] = part

            @pl.when(ki > 0)
            def _():
                acc_ref[...] += part

            @pl.when(ki == nk - 1)
            def _():
                finish(acc_ref[...])

    in_specs = [pl.BlockSpec((None, tm, tk), lambda ni, mi, ki: (ki // kbp, mi, ki % kbp)),
                pl.BlockSpec((None, tk, tn), lambda ni, mi, ki: (ni // npj, ki, ni % npj))]
    args = [a, w]
    if has_bias:
        in_specs.append(pl.BlockSpec((1, tn), lambda ni, mi, ki: (0, ni)))
        args.append(bias.reshape(1, big_n))
    out_spec = pl.BlockSpec((None, tm, tn), lambda ni, mi, ki: (ni // nbo, mi, ni % nbo))
    scratch = [pltpu.VMEM((tm, tn), F32)] if nk > 1 else []
    return _call(body, name, (big_n // tn, m // tm, nk), in_specs, out_spec,
                 _sds((out_parts, m, npo), out_dtype), scratch)(*args)


def mm_nt(a, w, name, out_dtype=F32, tm_cap=1024, tko_cap=2048, tc_cap=2048):
    pa, m, npa = a.shape
    j, k, n = w.shape
    assert pa * npa == j * n
    tm = _tile(m, tm_cap, SUBLANES)
    tko = _tile(k, tko_cap)
    tc = _tile(math.gcd(n, npa), tc_cap)
    out_bytes = jnp.dtype(out_dtype).itemsize

    def vmem_bytes(rows):
        return 2 * (rows * tc * 2 + tko * tc * 2 + rows * tko * out_bytes) + (rows * tko * 4 if j * n > tc else 0)

    while vmem_bytes(tm) > MM_VMEM_BUDGET and tm % (2 * SUBLANES) == 0:
        tm //= 2
    npj, nba, nc = n // tc, npa // tc, (j * n) // tc

    def body(a_ref, w_ref, o_ref, *scr):
        part = _dot_nt(_bf(a_ref[...]), _bf(w_ref[...]))
        if nc == 1:
            o_ref[...] = part.astype(o_ref.dtype)
        else:
            acc_ref = scr[0]
            ci = pl.program_id(2)

            @pl.when(ci == 0)
            def _():
                acc_ref[...] = part

            @pl.when(ci > 0)
            def _():
                acc_ref[...] += part

            @pl.when(ci == nc - 1)
            def _():
                o_ref[...] = acc_ref[...].astype(o_ref.dtype)

    in_specs = [pl.BlockSpec((None, tm, tc), lambda mi, ko, ci: (ci // nba, mi, ci % nba)),
                pl.BlockSpec((None, tko, tc), lambda mi, ko, ci: (ci // npj, ko, ci % npj))]
    out_spec = pl.BlockSpec((tm, tko), lambda mi, ko, ci: (mi, ko))
    scratch = [pltpu.VMEM((tm, tko), F32)] if nc > 1 else []
    return _call(body, name, (m // tm, k // tko, nc), in_specs, out_spec, _sds((m, k), out_dtype), scratch)(a, w)


def mm_tn(a, b, n_shards, name, out_dtype=BF16, tt_cap=4096, tkk_cap=512, tn_cap=1536):
    pa, m, kp = a.shape
    pb, m2, npb = b.shape
    assert m == m2
    k, big_n = pa * kp, pb * npb
    n = big_n // n_shards
    tt = _tile(m, tt_cap, SUBLANES)
    tkk = _tile(kp, tkk_cap)
    tn = _tile(math.gcd(n, npb), tn_cap)
    kbp, nbb, npj, nt = kp // tkk, npb // tn, n // tn, m // tt

    def body(a_ref, b_ref, o_ref, *scr):
        part = _dot_tn(_bf(a_ref[...]), _bf(b_ref[...]))
        if nt == 1:
            o_ref[...] = part.astype(o_ref.dtype)
        else:
            acc_ref = scr[0]
            ti = pl.program_id(2)

            @pl.when(ti == 0)
            def _():
                acc_ref[...] = part

            @pl.when(ti > 0)
            def _():
                acc_ref[...] += part

            @pl.when(ti == nt - 1)
            def _():
                o_ref[...] = acc_ref[...].astype(o_ref.dtype)

    in_specs = [pl.BlockSpec((None, tt, tkk), lambda ni, ki, ti: (ki // kbp, ti, ki % kbp)),
                pl.BlockSpec((None, tt, tn), lambda ni, ki, ti: (ni // nbb, ti, ni % nbb))]
    out_spec = pl.BlockSpec((None, tkk, tn), lambda ni, ki, ti: (ni // npj, ki, ni % npj))
    scratch = [pltpu.VMEM((tkk, tn), F32)] if nt > 1 else []
    return _call(body, name, (big_n // tn, k // tkk, nt), in_specs, out_spec,
                 _sds((n_shards, k, n), out_dtype), scratch)(a, b)


_ANY = pl.BlockSpec(memory_space=pl.ANY)
_MESH = pl.DeviceIdType.MESH


def all_gather(shards, name):
    n = len(shards)

    def body(*refs):
        ins, outs = refs[:n], refs[n:2 * n]
        send_sems, recv_sems, local_sems = refs[2 * n:]
        x, y, c = lax.axis_index("x"), lax.axis_index("y"), lax.axis_index("c")
        me, sibling = (x, y, c), (x, y, 1 - c)
        chips = [(1 - x, y), (x, 1 - y), (1 - x, 1 - y)]

        def copy(i, k, block, to, src=None):
            px, py, pc = block
            dst = outs[i].at[4 * px + 2 * py + pc]
            return pltpu.make_async_remote_copy(src_ref=dst if src is None else src, dst_ref=dst,
                                                send_sem=send_sems.at[i, k], recv_sem=recv_sems.at[i, k],
                                                device_id=to, device_id_type=_MESH)

        mine = [pltpu.make_async_copy(ins[i], outs[i].at[4 * x + 2 * y + c], local_sems.at[i]) for i in range(n)]
        for cp in mine:
            cp.start()
        first = []
        for i in range(n):
            first.append(copy(i, 0, me, sibling, src=ins[i]))
            first += [copy(i, 1 + q, me, (*chip, c), src=ins[i]) for q, chip in enumerate(chips)]
        for cp in first:
            cp.start()
        passed = []
        for q, chip in enumerate(chips):
            for i in range(n):
                copy(i, 1 + q, (*chip, c), me).wait_recv()
                cp = copy(i, 4 + q, (*chip, c), sibling)
                cp.start()
                passed.append(cp)
        for i in range(n):
            copy(i, 0, sibling, me).wait_recv()
            for q, chip in enumerate(chips):
                copy(i, 4 + q, (*chip, 1 - c), me).wait_recv()
        for cp in first + passed:
            cp.wait_send()
        for cp in mine:
            cp.wait()

    outs = pl.pallas_call(
        body, name=name, out_shape=[_sds((N_DEV,) + s.shape, s.dtype) for s in shards],
        in_specs=[_ANY] * n, out_specs=[_ANY] * n,
        scratch_shapes=[pltpu.SemaphoreType.DMA((n, 7)), pltpu.SemaphoreType.DMA((n, 7)),
                        pltpu.SemaphoreType.DMA((n,))],
    )(*shards)
    return list(outs)


def rs_core_exchange(parts, name):
    n = len(parts)

    def body(*refs):
        ins, outs = refs[:n], refs[n:2 * n]
        send_sems, recv_sems = refs[2 * n:]
        x, y, c = lax.axis_index("x"), lax.axis_index("y"), lax.axis_index("c")
        sibling = (x, y, 1 - c)
        started = []
        for i in range(n):
            for chip in range(4):
                cp = pltpu.make_async_remote_copy(src_ref=ins[i].at[2 * chip + (1 - c)], dst_ref=outs[i].at[chip],
                                                  send_sem=send_sems.at[i, chip], recv_sem=recv_sems.at[i, chip],
                                                  device_id=sibling, device_id_type=_MESH)
                cp.start()
                started.append(cp)
        for cp in started:
            cp.wait()

    outs = pl.pallas_call(
        body, name=name, out_shape=[_sds((4,) + p.shape[1:], p.dtype) for p in parts],
        in_specs=[_ANY] * n, out_specs=[_ANY] * n,
        scratch_shapes=[pltpu.SemaphoreType.DMA((n, 4)), pltpu.SemaphoreType.DMA((n, 4))],
    )(*parts)
    return list(outs)


def rs_chip_exchange(sums, name):
    n = len(sums)

    def body(*refs):
        ins, outs = refs[:n], refs[n:2 * n]
        send_sems, recv_sems = refs[2 * n:]
        x, y, c = lax.axis_index("x"), lax.axis_index("y"), lax.axis_index("c")
        peers = [(x, 1 - y), (1 - x, y), (1 - x, 1 - y)]
        started = []
        for i in range(n):
            for k, (px, py) in enumerate(peers):
                cp = pltpu.make_async_remote_copy(src_ref=ins[i].at[2 * px + py], dst_ref=outs[i].at[k],
                                                  send_sem=send_sems.at[i, k], recv_sem=recv_sems.at[i, k],
                                                  device_id=(px, py, c), device_id_type=_MESH)
                cp.start()
                started.append(cp)
        for cp in started:
            cp.wait()

    outs = pl.pallas_call(
        body, name=name, out_shape=[_sds((3,) + s.shape[1:], s.dtype) for s in sums],
        in_specs=[_ANY] * n, out_specs=[_ANY] * n,
        scratch_shapes=[pltpu.SemaphoreType.DMA((n, 3)), pltpu.SemaphoreType.DMA((n, 3))],
    )(*sums)
    return list(outs)


_HBM = pl.BlockSpec(memory_space=pltpu.HBM)
_SEM = pl.BlockSpec(memory_space=pltpu.SEMAPHORE)
_EFFECT = pltpu.SideEffectType.DATAFLOW_SIDE_EFFECTING


def _in_hbm(a):
    return pltpu.with_memory_space_constraint(a, pltpu.HBM)


def _plan_copies(plan, srcs, dsts, send_sems, recv_sems):
    x, y, c = lax.axis_index("x"), lax.axis_index("y"), lax.axis_index("c")
    return [pltpu.make_async_remote_copy(src_ref=s, dst_ref=d, send_sem=send_sems.at[q], recv_sem=recv_sems.at[q],
                                         device_id=to, device_id_type=_MESH)
            for q, (s, d, to) in enumerate(plan(x, y, c, srcs, dsts))]


def exchange_start(plan, n_copies, srcs, dsts, after, name):
    ns, nd, na = len(srcs), len(dsts), len(after)

    def body(*refs):
        ins = refs[:ns + nd + na]
        send_sems, recv_sems = refs[ns + nd + na], refs[ns + nd + na + 1]
        token = refs[-1]
        for cp in _plan_copies(plan, ins[:ns], ins[ns:ns + nd], send_sems, recv_sems):
            cp.start()
        token[...] = jnp.zeros_like(token)

    thru = [pltpu.HBM(a.shape, a.dtype) for a in list(srcs) + list(dsts)]
    out = pl.pallas_call(
        body, name=name,
        out_shape=[pltpu.SemaphoreType.DMA((n_copies,)), pltpu.SemaphoreType.DMA((n_copies,))] + thru
        + [_sds((SUBLANES, LANES), F32)],
        in_specs=[_HBM] * (ns + nd) + [_ANY] * na,
        out_specs=[_SEM, _SEM] + [_HBM] * (ns + nd) + [pl.BlockSpec(memory_space=pltpu.VMEM)],
        input_output_aliases={i: 2 + i for i in range(ns + nd)},
        compiler_params=pltpu.CompilerParams(has_side_effects=_EFFECT),
    )(*[_in_hbm(a) for a in list(srcs) + list(dsts)], *after)
    return out[0], out[1], list(out[2:2 + ns]), list(out[2 + ns:2 + ns + nd]), out[-1]


def exchange_wait(plan, started, after, name):
    send_sems, recv_sems, srcs, dsts, _ = started
    ns, nd, na = len(srcs), len(dsts), len(after)

    def body(*refs):
        ins = refs[:ns + nd]
        send_sems_ref, recv_sems_ref = refs[ns + nd], refs[ns + nd + 1]
        for cp in _plan_copies(plan, ins[:ns], ins[ns:ns + nd], send_sems_ref, recv_sems_ref):
            cp.wait_send()
            cp.wait_recv()

    thru = [pltpu.HBM(a.shape, a.dtype) for a in list(srcs) + list(dsts)]
    out = pl.pallas_call(
        body, name=name, out_shape=thru,
        in_specs=[_HBM] * (ns + nd) + [_SEM, _SEM] + [_ANY] * na,
        out_specs=[_HBM] * (ns + nd),
        input_output_aliases={i: i for i in range(ns + nd)},
        compiler_params=pltpu.CompilerParams(has_side_effects=_EFFECT),
    )(*srcs, *dsts, send_sems, recv_sems, *after)
    return list(out[:ns]), list(out[ns:])


def _dev_index(px, py, pc):
    return 4 * px + 2 * py + pc


def _ag_plan_first(x, y, c, srcs, dsts):
    me = _dev_index(x, y, c)
    peers = [(x, y, 1 - c), (1 - x, y, c), (x, 1 - y, c), (1 - x, 1 - y, c)]
    return [(s, d.at[me], to) for s, d in zip(srcs, dsts) for to in peers]


def _ag_plan_pass(x, y, c, srcs, dsts):
    chips = [(1 - x, y), (x, 1 - y), (1 - x, 1 - y)]
    return [(d.at[_dev_index(px, py, c)], d.at[_dev_index(px, py, c)], (x, y, 1 - c)) for d in dsts for px, py in chips]


def _rs_plan_cores(x, y, c, srcs, dsts):
    return [(s.at[2 * chip + (1 - c)], d.at[chip], (x, y, 1 - c)) for s, d in zip(srcs, dsts) for chip in range(4)]


def _rs_plan_chips(x, y, c, srcs, dsts):
    peers = [(x, 1 - y), (1 - x, y), (1 - x, 1 - y)]
    return [(s.at[2 * px + py], d.at[k], (px, py, c)) for s, d in zip(srcs, dsts) for k, (px, py) in enumerate(peers)]


def _row_tile(r, c, itemsize=4, budget=1 << 20):
    cap = max(SUBLANES, (budget // (c * itemsize)) // SUBLANES * SUBLANES)
    if r <= cap:
        return r
    best = 0
    for t in range(SUBLANES, cap + 1, SUBLANES):
        if r % t == 0:
            best = t
    return best if best else r


def rs_core_add(part, recv, core, name):
    _, r, c = part.shape
    tr = _row_tile(r, c, budget=1 << 22)

    def body(core_ref, p_ref, q_ref, o_ref):
        o_ref[...] = (p_ref[...].astype(F32) + q_ref[...].astype(F32)).astype(o_ref.dtype)

    in_specs = [pl.BlockSpec((None, tr, c), lambda ch, ri, core_ref: (2 * ch + core_ref[0], ri, 0)),
                pl.BlockSpec((None, tr, c), lambda ch, ri, core_ref: (ch, ri, 0))]
    out_spec = pl.BlockSpec((None, tr, c), lambda ch, ri, core_ref: (ch, ri, 0))
    return _call(body, name, (4, r // tr), in_specs, out_spec, _sds((4, r, c), part.dtype), prefetch=1)(core, part, recv)


def _adamw(w, g, m, v):
    m = ADAM_B1 * m + (1.0 - ADAM_B1) * g
    v = ADAM_B2 * v + (1.0 - ADAM_B2) * (g * g)
    m_hat = m / (1.0 - ADAM_B1 ** ADAM_STEP)
    v_hat = v / (1.0 - ADAM_B2 ** ADAM_STEP)
    delta = -ADAM_LR * (m_hat / (jnp.sqrt(v_hat) + ADAM_EPS) + ADAM_WD * w)
    return delta, m, v


def rs_finish(sums, recv, chip, name, adam=None):
    _, r, c = sums.shape
    tr = _row_tile(r, c, budget=1 << 20)

    def body(chip_ref, s_ref, q_ref, *refs):
        g = s_ref[...].astype(F32)
        for k in range(3):
            g = g + q_ref[k].astype(F32)
        if adam is None:
            refs[0][...] = g
        else:
            w_ref, m_ref, v_ref, g_ref, d_ref, nm_ref, nv_ref = refs
            g_ref[...] = g
            d_ref[...], nm_ref[...], nv_ref[...] = _adamw(w_ref[...], g, m_ref[...], v_ref[...])

    blk = pl.BlockSpec((tr, c), lambda ri, chip_ref: (ri, 0))
    in_specs = [pl.BlockSpec((None, tr, c), lambda ri, chip_ref: (chip_ref[0], ri, 0)),
                pl.BlockSpec((3, tr, c), lambda ri, chip_ref: (0, ri, 0))]
    args = [chip, sums, recv]
    if adam is None:
        out_specs, out_shape = blk, _sds((r, c), F32)
    else:
        in_specs += [blk] * 3
        args += list(adam)
        out_specs, out_shape = [blk] * 4, [_sds((r, c), F32)] * 4
    return _call(body, name, (r // tr,), in_specs, out_specs, out_shape, prefetch=1)(*args)


def adam_flat(w, g, m, v, name):
    r, c = w.shape
    tr = _row_tile(r, c, budget=1 << 19)

    def body(w_ref, g_ref, m_ref, v_ref, d_ref, nm_ref, nv_ref):
        d_ref[...], nm_ref[...], nv_ref[...] = _adamw(w_ref[...], g_ref[...], m_ref[...], v_ref[...])

    blk = pl.BlockSpec((tr, c), lambda ri: (ri, 0))
    return _call(body, name, (r // tr,), [blk] * 4, [blk] * 3, [_sds((r, c), F32)] * 3)(w, g, m, v)


def rms_fwd(x, g, name, add=None):
    t, d = x.shape
    tm = _tile(t, 256, SUBLANES)
    has_add = add is not None

    def body(*refs):
        if has_add:
            x_ref, a_ref, g_ref, h_ref, hn_ref = refs
            h = x_ref[...] + a_ref[...]
            h_ref[...] = h
        else:
            x_ref, g_ref, hn_ref = refs
            h = x_ref[...]
        r = lax.rsqrt(jnp.mean(h * h, axis=-1, keepdims=True) + NORM_EPS)
        hn_ref[...] = (h * r * g_ref[...]).astype(BF16)

    row = pl.BlockSpec((tm, d), lambda i: (i, 0))
    vec = pl.BlockSpec((1, d), lambda i: (0, 0))
    if has_add:
        return _call(body, name, (t // tm,), [row, row, vec], [row, row], [_sds((t, d), F32), _sds((t, d), BF16)])(
            x, add, g.reshape(1, d))
    return _call(body, name, (t // tm,), [row, vec], row, _sds((t, d), BF16))(x, g.reshape(1, d))


def _rms_bwd_math(x, g, dy):
    r = lax.rsqrt(jnp.mean(x * x, axis=-1, keepdims=True) + NORM_EPS)
    xh = x * r
    dyg = dy * g
    dx = r * (dyg - xh * jnp.mean(dyg * xh, axis=-1, keepdims=True))
    return dx, jnp.sum(dy * xh, axis=0, keepdims=True)


def rms_bwd(x, g, dhn, dres, name):
    t, d = x.shape
    tm = _tile(t, 256, SUBLANES)

    def body(x_ref, g_ref, dy_ref, dr_ref, dx_ref, dxb_ref, dg_ref):
        dx, dg = _rms_bwd_math(x_ref[...], g_ref[...], dy_ref[...])
        dx = dx + dr_ref[...]
        dx_ref[...] = dx
        dxb_ref[...] = dx.astype(BF16)

        @pl.when(pl.program_id(0) == 0)
        def _():
            dg_ref[...] = jnp.zeros_like(dg_ref)

        dg_ref[...] += dg

    row = pl.BlockSpec((tm, d), lambda i: (i, 0))
    vec = pl.BlockSpec((1, d), lambda i: (0, 0))
    return _call(body, name, (t // tm,), [row, vec, row, row], [row, row, vec],
                 [_sds((t, d), F32), _sds((t, d), BF16), _sds((1, d), F32)])(x, g.reshape(1, d), dhn, dres)


def final_loss(h, add, g, target, name):
    t, d = h.shape
    tm = _tile(t, 256, SUBLANES)

    def body(x_ref, a_ref, g_ref, t_ref, loss_ref, dx_ref, dxb_ref, dg_ref):
        x = x_ref[...] + a_ref[...]
        gain = g_ref[...]
        r = lax.rsqrt(jnp.mean(x * x, axis=-1, keepdims=True) + NORM_EPS)
        err = x * r * gain - t_ref[...]
        dx, dg = _rms_bwd_math(x, gain, err * (1.0 / d))
        dx_ref[...] = dx
        dxb_ref[...] = dx.astype(BF16)

        @pl.when(pl.program_id(0) == 0)
        def _():
            dg_ref[...] = jnp.zeros_like(dg_ref)
            loss_ref[...] = jnp.zeros_like(loss_ref)

        dg_ref[...] += dg
        part = jnp.sum(jnp.sum(err * err, axis=-1, keepdims=True), axis=0, keepdims=True) * (0.5 / d)
        loss_ref[...] += jnp.broadcast_to(part, loss_ref.shape)

    row = pl.BlockSpec((tm, d), lambda i: (i, 0))
    vec = pl.BlockSpec((1, d), lambda i: (0, 0))
    one = pl.BlockSpec((1, LANES), lambda i: (0, 0))
    return _call(body, name, (t // tm,), [row, row, vec, row], [one, row, row, vec],
                 [_sds((1, LANES), F32), _sds((t, d), F32), _sds((t, d), BF16), _sds((1, d), F32)])(
        h, add, g.reshape(1, d), target)


def _conv_causal(z, w_ref_rows, width):
    out = z * w_ref_rows(width - 1)
    for k in range(width - 1):
        out = out + _shift_down(z, width - 1 - k) * w_ref_rows(k)
    return out


def ffn_gate_fwd(z2, dw_w2, dw_b2, name, tc=LANES):
    _, b, s, f = z2.shape

    def body(z_ref, w_ref, b_ref, y_ref):
        g = _conv_causal(z_ref[0], lambda k: w_ref[0, k:k + 1, :], FFN_WIDTH) + b_ref[0]
        a = _conv_causal(z_ref[1], lambda k: w_ref[1, k:k + 1, :], FFN_WIDTH) + b_ref[1]
        y_ref[...] = (g * _sigmoid(g) * a).astype(BF16)

    in_specs = [pl.BlockSpec((2, None, s, tc), lambda bi, ci: (0, bi, 0, ci)),
                pl.BlockSpec((2, FFN_WIDTH, tc), lambda bi, ci: (0, 0, ci)),
                pl.BlockSpec((2, 1, tc), lambda bi, ci: (0, 0, ci))]
    out_spec = pl.BlockSpec((None, s, tc), lambda bi, ci: (bi, 0, ci))
    return _call(body, name, (b, f // tc), in_specs, out_spec, _sds((b, s, f), BF16))(z2, dw_w2, dw_b2)


def ffn_gate_bwd(z2, dy, dw_w2, dw_b2, name, tc=LANES):
    _, b, s, f = z2.shape

    def body(z_ref, dy_ref, w_ref, b_ref, dz_ref, dw_ref, db_ref):
        @pl.when(pl.program_id(1) == 0)
        def _():
            dw_ref[...] = jnp.zeros_like(dw_ref)
            db_ref[...] = jnp.zeros_like(db_ref)

        zs = (z_ref[0], z_ref[1])
        g = _conv_causal(zs[0], lambda k: w_ref[0, k:k + 1, :], FFN_WIDTH) + b_ref[0]
        a = _conv_causal(zs[1], lambda k: w_ref[1, k:k + 1, :], FFN_WIDTH) + b_ref[1]
        sg = _sigmoid(g)
        dy = dy_ref[...]
        dcs = (dy * a * (sg * (1.0 + g * (1.0 - sg))), dy * (g * sg))
        for p in range(2):
            dc = dcs[p]
            dz = dc * w_ref[p, FFN_WIDTH - 1:FFN_WIDTH, :]
            for k in range(FFN_WIDTH - 1):
                dz = dz + _shift_up(dc, FFN_WIDTH - 1 - k) * w_ref[p, k:k + 1, :]
            dz_ref[p] = dz.astype(BF16)
            for k in range(FFN_WIDTH):
                dw_ref[p, k:k + 1, :] += jnp.sum(dc * _shift_down(zs[p], FFN_WIDTH - 1 - k), axis=0, keepdims=True)
            db_ref[p] += jnp.sum(dc, axis=0, keepdims=True)

    in_specs = [pl.BlockSpec((2, None, s, tc), lambda ci, bi: (0, bi, 0, ci)),
                pl.BlockSpec((None, s, tc), lambda ci, bi: (bi, 0, ci)),
                pl.BlockSpec((2, FFN_WIDTH, tc), lambda ci, bi: (0, 0, ci)),
                pl.BlockSpec((2, 1, tc), lambda ci, bi: (0, 0, ci))]
    out_specs = [pl.BlockSpec((2, None, s, tc), lambda ci, bi: (0, bi, 0, ci)),
                 pl.BlockSpec((2, FFN_WIDTH, tc), lambda ci, bi: (0, 0, ci)),
                 pl.BlockSpec((2, 1, tc), lambda ci, bi: (0, 0, ci))]
    out_shape = [_sds((2, b, s, f), BF16), _sds((2, FFN_WIDTH, f), F32), _sds((2, 1, f), F32)]
    return _call(body, name, (f // tc, b), in_specs, out_specs, out_shape)(z2, dy, dw_w2, dw_b2)


def cc_conv_fwd(z2, dw_w, dw_b, name, tc=LANES):
    _, b, s, c = z2.shape

    def body(z_ref, w_ref, b_ref, o_ref):
        u = z_ref[0] * _sigmoid(z_ref[1])
        o_ref[...] = _conv_causal(u, lambda k: w_ref[k:k + 1, :], CC_WIDTH) + b_ref[...]

    in_specs = [pl.BlockSpec((2, None, s, tc), lambda bi, ci: (0, bi, 0, ci)),
                pl.BlockSpec((CC_WIDTH, tc), lambda bi, ci: (0, ci)),
                pl.BlockSpec((1, tc), lambda bi, ci: (0, ci))]
    out_spec = pl.BlockSpec((None, s, tc), lambda bi, ci: (bi, 0, ci))
    return _call(body, name, (b, c // tc), in_specs, out_spec, _sds((b, s, c), F32))(z2, dw_w, dw_b.reshape(1, c))


def cc_conv_bwd(z2, dcv, dw_w, name, tc=LANES):
    _, b, s, c = z2.shape

    def body(z_ref, dc_ref, w_ref, dz_ref, dbi_ref, dw_ref, db_ref):
        @pl.when(pl.program_id(1) == 0)
        def _():
            dbi_ref[...] = jnp.zeros_like(dbi_ref)
            dw_ref[...] = jnp.zeros_like(dw_ref)
            db_ref[...] = jnp.zeros_like(db_ref)

        a, gate = z_ref[0], z_ref[1]
        sg = _sigmoid(gate)
        u = a * sg
        dc = dc_ref[...]
        du = jnp.zeros_like(dc)
        for j in range(CC_WIDTH):
            sd = _shift_up(dc, j)
            k = CC_WIDTH - 1 - j
            du = du + sd * w_ref[k:k + 1, :]
            dw_ref[k:k + 1, :] += jnp.sum(sd * u, axis=0, keepdims=True)
        db_ref[...] += jnp.sum(dc, axis=0, keepdims=True)
        da = du * sg
        dg = du * a * sg * (1.0 - sg)
        dz_ref[0] = da.astype(BF16)
        dz_ref[1] = dg.astype(BF16)
        dbi_ref[0] += jnp.sum(da, axis=0, keepdims=True)
        dbi_ref[1] += jnp.sum(dg, axis=0, keepdims=True)

    in_specs = [pl.BlockSpec((2, None, s, tc), lambda ci, bi: (0, bi, 0, ci)),
                pl.BlockSpec((None, s, tc), lambda ci, bi: (bi, 0, ci)),
                pl.BlockSpec((CC_WIDTH, tc), lambda ci, bi: (0, ci))]
    out_specs = [pl.BlockSpec((2, None, s, tc), lambda ci, bi: (0, bi, 0, ci)),
                 pl.BlockSpec((2, 1, tc), lambda ci, bi: (0, 0, ci)),
                 pl.BlockSpec((CC_WIDTH, tc), lambda ci, bi: (0, ci)),
                 pl.BlockSpec((1, tc), lambda ci, bi: (0, ci))]
    out_shape = [_sds((2, b, s, c), BF16), _sds((2, 1, c), F32), _sds((CC_WIDTH, c), F32), _sds((1, c), F32)]
    return _call(body, name, (c // tc, b), in_specs, out_specs, out_shape)(z2, dcv, dw_w)


def _ln_stats(v):
    mu = jnp.mean(v, axis=-1, keepdims=True)
    vc = v - mu
    r = lax.rsqrt(jnp.mean(vc * vc, axis=-1, keepdims=True) + NORM_EPS)
    return vc * r, r


def _ln_bwd(dln, xh, r, g):
    dxh = dln * g
    dx = r * (dxh - jnp.mean(dxh, axis=-1, keepdims=True) - xh * jnp.mean(dxh * xh, axis=-1, keepdims=True))
    return dx, jnp.sum(dln * xh, axis=0, keepdims=True), jnp.sum(dln, axis=0, keepdims=True)


def cc_ln_fwd(cv, ln_g, ln_b, name):
    t, c = cv.shape
    tm = _tile(t, 256, SUBLANES)

    def body(x_ref, g_ref, b_ref, o_ref):
        xh, _ = _ln_stats(x_ref[...])
        ln = xh * g_ref[...] + b_ref[...]
        o_ref[...] = (ln * _sigmoid(ln)).astype(BF16)

    row = pl.BlockSpec((tm, c), lambda i: (i, 0))
    vec = pl.BlockSpec((1, c), lambda i: (0, 0))
    return _call(body, name, (t // tm,), [row, vec, vec], row, _sds((t, c), BF16))(
        cv, ln_g.reshape(1, c), ln_b.reshape(1, c))


def cc_ln_bwd(cv, ln_g, ln_b, ds, name):
    t, c = cv.shape
    tm = _tile(t, 256, SUBLANES)

    def body(x_ref, g_ref, b_ref, ds_ref, dx_ref, dg_ref, db_ref):
        @pl.when(pl.program_id(0) == 0)
        def _():
            dg_ref[...] = jnp.zeros_like(dg_ref)
            db_ref[...] = jnp.zeros_like(db_ref)

        xh, r = _ln_stats(x_ref[...])
        ln = xh * g_ref[...] + b_ref[...]
        sg = _sigmoid(ln)
        dln = ds_ref[...] * (sg * (1.0 + ln * (1.0 - sg)))
        dx, dg, db = _ln_bwd(dln, xh, r, g_ref[...])
        dx_ref[...] = dx
        dg_ref[...] += dg
        db_ref[...] += db

    row = pl.BlockSpec((tm, c), lambda i: (i, 0))
    vec = pl.BlockSpec((1, c), lambda i: (0, 0))
    return _call(body, name, (t // tm,), [row, vec, vec, row], [row, vec, vec],
                 [_sds((t, c), F32), _sds((1, c), F32), _sds((1, c), F32)])(
        cv, ln_g.reshape(1, c), ln_b.reshape(1, c), ds)


def col_sum(v, name):
    t, n = v.shape
    tm = _tile(t, 256, SUBLANES)

    def body(v_ref, o_ref):
        @pl.when(pl.program_id(0) == 0)
        def _():
            o_ref[...] = jnp.zeros_like(o_ref)

        o_ref[...] += jnp.sum(v_ref[...], axis=0, keepdims=True)

    return _call(body, name, (t // tm,), [pl.BlockSpec((tm, n), lambda i: (i, 0))],
                 pl.BlockSpec((1, n), lambda i: (0, 0)), _sds((1, n), F32))(v)


def rope_tables_full(seq, dim):
    pos = jnp.arange(seq, dtype=F32)
    inv = ROPE_THETA ** (-(jnp.arange(0, dim, 2, dtype=F32) / dim))
    ang = pos[:, None] * inv[None, :]
    cos, sin = jnp.cos(ang), jnp.sin(ang)
    return jnp.concatenate([cos, cos], axis=-1), jnp.concatenate([-sin, sin], axis=-1)


def dsa_rope_fwd(qkv, cos, sin, seq, name):
    t, w = qkv.shape
    tm = _tile(seq, 256, SUBLANES)
    nsb = seq // tm
    hb = DSA_HEADS

    def body(x_ref, c_ref, s_ref, o_ref):
        cos_t, sin_t = c_ref[...], s_ref[...]
        for blk in range(w // LANES):
            v = x_ref[:, blk * LANES:(blk + 1) * LANES]
            if (blk // hb) % 3 < 2:
                v = _rope(v, cos_t, sin_t)
            o_ref[:, blk * LANES:(blk + 1) * LANES] = v.astype(BF16)

    row = pl.BlockSpec((tm, w), lambda i: (i, 0))
    tab = pl.BlockSpec((tm, LANES), lambda i: (i % nsb, 0))
    return _call(body, name, (t // tm,), [row, tab, tab], row, _sds((t, w), BF16))(qkv, cos, sin)


def dsa_rope_bwd(grads, cos, sin, seq, name):
    flat = [a for grp in grads for a in grp]
    t, hw = flat[0].shape
    w = hw * len(flat)
    tm = _tile(seq, 256, SUBLANES)
    nsb = seq // tm

    def body(*refs):
        ins, (c_ref, s_ref, o_ref) = refs[:len(flat)], refs[len(flat):]
        cos_t, sin_t = c_ref[...], s_ref[...]
        for idx, ref in enumerate(ins):
            for h in range(hw // LANES):
                v = ref[:, h * LANES:(h + 1) * LANES]
                if idx % 3 < 2:
                    v = _rope_t(v, cos_t, sin_t)
                col = idx * hw + h * LANES
                o_ref[:, col:col + LANES] = v.astype(BF16)

    part = pl.BlockSpec((tm, hw), lambda i: (i, 0))
    tab = pl.BlockSpec((tm, LANES), lambda i: (i % nsb, 0))
    out = _call(body, name, (t // tm,), [part] * len(flat) + [tab, tab], pl.BlockSpec((tm, w), lambda i: (i, 0)),
                _sds((t, w), BF16))(*flat, cos, sin)
    return out[None]


def _dsa_mask(n):
    qi = lax.broadcasted_iota(jnp.int32, (DSA_BLOCK, (1 if n == 0 else 2) * DSA_BLOCK), 0)
    kj = lax.broadcasted_iota(jnp.int32, (DSA_BLOCK, (1 if n == 0 else 2) * DSA_BLOCK), 1)
    if n == 0:
        return qi >= kj
    dist = DSA_BLOCK + qi - kj
    return (dist >= 0) & (dist <= DSA_BLOCK)


def dsa_attn_fwd(qkv_b, grp, dil, batch, seq, name):
    t, w = qkv_b.shape
    ell = seq // dil
    nb = ell // DSA_BLOCK
    nw, h_n = w // LANES, DSA_HEADS
    base = grp * 3 * h_n
    scale = DSA_BLOCK ** -0.5

    def body(q_ref, k_ref, v_ref, o_ref, lse_ref):
        for n in range(nb):
            lo = 0 if n == 0 else (n - 1) * DSA_BLOCK
            hi = (n + 1) * DSA_BLOCK
            q = q_ref[n * DSA_BLOCK:hi, :]
            s = _dot_nt(q, k_ref[lo:hi, :]) * scale
            s = jnp.where(_dsa_mask(n), s, NEG_INF)
            m = jnp.max(s, axis=-1, keepdims=True)
            p = jnp.exp(s - m)
            l = jnp.sum(p, axis=-1, keepdims=True)
            o_ref[n * DSA_BLOCK:hi, :] = _dot(p.astype(BF16), v_ref[lo:hi, :]) / l
            lse_ref[n * DSA_BLOCK:hi, :] = jnp.broadcast_to(m + jnp.log(l), (DSA_BLOCK, LANES))

    def spec(off):
        return pl.BlockSpec((None, ell, LANES), lambda b, r, h: (b, 0, r * nw + base + off + h))

    out_spec = pl.BlockSpec((None, ell, LANES), lambda b, r, h: (b, 0, r * h_n + h))
    view = qkv_b.reshape(batch, ell, dil * w)
    o, lse = _call(body, name, (batch, dil, h_n), [spec(0), spec(h_n), spec(2 * h_n)], [out_spec, out_spec],
                   [_sds((batch, ell, dil * h_n * LANES), F32)] * 2)(view, view, view)
    return o.reshape(t, h_n * LANES), lse.reshape(t, h_n * LANES)


def dsa_merge(outs, lses, name):
    t, hw = outs[0].shape
    tm = _tile(t, 256, SUBLANES)
    ng = len(outs)

    def body(*refs):
        o_refs, l_refs, (o_ref, lse_ref) = refs[:ng], refs[ng:2 * ng], refs[2 * ng:]
        ls = [r[...] for r in l_refs]
        m = functools.reduce(jnp.maximum, ls)
        es = [jnp.exp(v - m) for v in ls]
        tot = functools.reduce(lambda a, b: a + b, es)
        acc = jnp.zeros_like(m)
        for e, r in zip(es, o_refs):
            acc = acc + (e / tot) * r[...]
        o_ref[...] = acc
        lse_ref[...] = m + jnp.log(tot)

    row = pl.BlockSpec((tm, hw), lambda i: (i, 0))
    return _call(body, name, (t // tm,), [row] * (2 * ng), [row, row], [_sds((t, hw), F32)] * 2)(*outs, *lses)


def head_delta(do, o, name):
    t, hw = do.shape
    tm = _tile(t, 256, SUBLANES)

    def body(do_ref, o_ref, d_ref):
        for h in range(hw // LANES):
            sl = slice(h * LANES, (h + 1) * LANES)
            d = jnp.sum(do_ref[:, sl] * o_ref[:, sl], axis=-1, keepdims=True)
            d_ref[:, sl] = jnp.broadcast_to(d, (tm, LANES))

    row = pl.BlockSpec((tm, hw), lambda i: (i, 0))
    return _call(body, name, (t // tm,), [row, row], row, _sds((t, hw), F32))(do, o)


def dsa_attn_bwd(qkv_b, do, lse, delta, grp, dil, batch, seq, name):
    t, w = qkv_b.shape
    ell = seq // dil
    nb = ell // DSA_BLOCK
    nw, h_n = w // LANES, DSA_HEADS
    base = grp * 3 * h_n
    scale = DSA_BLOCK ** -0.5

    def body(q_ref, k_ref, v_ref, do_ref, lse_ref, dl_ref, dq_ref, dk_ref, dv_ref):
        dk_ref[...] = jnp.zeros_like(dk_ref)
        dv_ref[...] = jnp.zeros_like(dv_ref)
        for n in range(nb):
            lo = 0 if n == 0 else (n - 1) * DSA_BLOCK
            hi = (n + 1) * DSA_BLOCK
            rows = slice(n * DSA_BLOCK, hi)
            q, k, v = q_ref[rows, :], k_ref[lo:hi, :], v_ref[lo:hi, :]
            dout = do_ref[rows, :].astype(BF16)
            s = jnp.where(_dsa_mask(n), _dot_nt(q, k) * scale, NEG_INF)
            p = jnp.exp(s - lse_ref[rows, 0:1])
            ds = (p * (_dot_nt(dout, v) - dl_ref[rows, 0:1]) * scale).astype(BF16)
            dq_ref[rows, :] = _dot(ds, k)
            dk_ref[lo:hi, :] += _dot_tn(ds, q)
            dv_ref[lo:hi, :] += _dot_tn(p.astype(BF16), dout)

    def spec(off):
        return pl.BlockSpec((None, ell, LANES), lambda b, r, h: (b, 0, r * nw + base + off + h))

    hspec = pl.BlockSpec((None, ell, LANES), lambda b, r, h: (b, 0, r * h_n + h))
    view = qkv_b.reshape(batch, ell, dil * w)
    hv = [a.reshape(batch, ell, dil * h_n * LANES) for a in (do, lse, delta)]
    outs = _call(body, name, (batch, dil, h_n), [spec(0), spec(h_n), spec(2 * h_n), hspec, hspec, hspec],
                 [hspec] * 3, [_sds((batch, ell, dil * h_n * LANES), F32)] * 3)(view, view, view, *hv)
    return [a.reshape(t, h_n * LANES) for a in outs]


def mla_rope_tables(seq):
    pos = jnp.arange(seq, dtype=F32)
    inv = ROPE_THETA ** (-(jnp.arange(0, MLA_ROPE, 2, dtype=F32) / MLA_ROPE))
    ang = pos[:, None] * inv[None, :]
    cos, sin, zero = jnp.cos(ang), jnp.sin(ang), jnp.zeros_like(ang)
    return jnp.concatenate([cos, zero, cos, zero], axis=-1), jnp.concatenate([-sin, zero, sin, zero], axis=-1)


def _spread_rope_cols(w_pe):
    half = MLA_ROPE // 2
    zero = jnp.zeros(w_pe.shape[:-1] + (half,), w_pe.dtype)
    return jnp.concatenate([w_pe[..., :half], zero, w_pe[..., half:], zero], axis=-1)


def _gather_rope_cols(g_pe):
    half = MLA_ROPE // 2
    return jnp.concatenate([g_pe[..., :half], g_pe[..., 2 * half:3 * half]], axis=-1)


def mla_low_fwd(c, q_norm, kv_norm, cos, sin, seq, name):
    t, w = c.shape
    rk = MLA_RANK
    tm = _tile(seq, 256, SUBLANES)
    nsb = seq // tm

    def body(c_ref, qg_ref, kg_ref, cs_ref, sn_ref, qn_ref, kn_ref, kp_ref):
        for lo, g_ref, o_ref in ((0, qg_ref, qn_ref), (rk, kg_ref, kn_ref)):
            v = c_ref[:, lo:lo + rk]
            r = lax.rsqrt(jnp.mean(v * v, axis=-1, keepdims=True) + NORM_EPS)
            o_ref[...] = (v * r * g_ref[...]).astype(BF16)
        kp_ref[...] = _rope(c_ref[:, 2 * rk:], cs_ref[...], sn_ref[...]).astype(BF16)

    row = lambda n: pl.BlockSpec((tm, n), lambda i: (i, 0))
    vec = pl.BlockSpec((1, rk), lambda i: (0, 0))
    tab = pl.BlockSpec((tm, LANES), lambda i: (i % nsb, 0))
    return _call(body, name, (t // tm,), [row(w), vec, vec, tab, tab], [row(rk), row(rk), row(LANES)],
                 [_sds((t, rk), BF16), _sds((t, rk), BF16), _sds((t, LANES), BF16)])(
        c, q_norm.reshape(1, rk), kv_norm.reshape(1, rk), cos, sin)


def mla_low_bwd(c, q_norm, kv_norm, dqn, dkn, dkp, cos, sin, seq, name):
    t, w = c.shape
    rk = MLA_RANK
    tm = _tile(seq, 256, SUBLANES)
    nsb = seq // tm

    def body(c_ref, qg_ref, kg_ref, dq_ref, dk_ref, dp_ref, cs_ref, sn_ref, dc_ref, dqg_ref, dkg_ref):
        @pl.when(pl.program_id(0) == 0)
        def _():
            dqg_ref[...] = jnp.zeros_like(dqg_ref)
            dkg_ref[...] = jnp.zeros_like(dkg_ref)

        for lo, g_ref, d_ref, dg_ref in ((0, qg_ref, dq_ref, dqg_ref), (rk, kg_ref, dk_ref, dkg_ref)):
            dx, dg = _rms_bwd_math(c_ref[:, lo:lo + rk], g_ref[...], d_ref[...])
            dc_ref[:, lo:lo + rk] = dx.astype(BF16)
            dg_ref[...] += dg
        dc_ref[:, 2 * rk:] = _rope_t(dp_ref[...], cs_ref[...], sn_ref[...]).astype(BF16)

    row = lambda n: pl.BlockSpec((tm, n), lambda i: (i, 0))
    vec = pl.BlockSpec((1, rk), lambda i: (0, 0))
    tab = pl.BlockSpec((tm, LANES), lambda i: (i % nsb, 0))
    dc, dqg, dkg = _call(body, name, (t // tm,), [row(w), vec, vec, row(rk), row(rk), row(LANES), tab, tab],
                         [row(w), vec, vec], [_sds((t, w), BF16), _sds((1, rk), F32), _sds((1, rk), F32)])(
        c, q_norm.reshape(1, rk), kv_norm.reshape(1, rk), dqn, dkn, dkp, cos, sin)
    return dc[None], dqg, dkg


def mla_rope_cast(q, kv, cos, sin, seq, name, transpose=False):
    t, w = q.shape
    tm = _tile(seq, 256, SUBLANES)
    nsb = seq // tm
    fn = _rope_t if transpose else _rope

    def body(q_ref, kv_ref, cs_ref, sn_ref, qo_ref, kvo_ref):
        cos_t, sin_t = cs_ref[...], sn_ref[...]
        for blk in range(w // LANES):
            sl = slice(blk * LANES, (blk + 1) * LANES)
            v = q_ref[:, sl]
            if blk % 2 == 1:
                v = fn(v, cos_t, sin_t)
            qo_ref[:, sl] = v.astype(BF16)
        kvo_ref[...] = kv_ref[...].astype(BF16)

    row = pl.BlockSpec((tm, w), lambda i: (i, 0))
    tab = pl.BlockSpec((tm, LANES), lambda i: (i % nsb, 0))
    return _call(body, name, (t // tm,), [row, row, tab, tab], [row, row], [_sds((t, w), BF16)] * 2)(q, kv, cos, sin)


def _causal_mask(i, j, tq, tk):
    qpos = i * tq + lax.broadcasted_iota(jnp.int32, (tq, tk), 0)
    kpos = j * tk + lax.broadcasted_iota(jnp.int32, (tq, tk), 1)
    return kpos <= qpos


def mla_attn_fwd(q_b, kv_b, kp_b, batch, seq, name, tq=256):
    t, w = q_b.shape
    h_n = w // (2 * LANES)
    tq = _tile(seq, tq, SUBLANES)
    nq = seq // tq
    scale = (MLA_NOPE + MLA_ROPE) ** -0.5

    def body(q_ref, kv_ref, kp_ref, o_ref, lse_ref):
        i = pl.program_id(2)
        qn, qp = q_ref[:, :LANES], q_ref[:, LANES:]

        def step(j, carry):
            m, l, acc = carry
            rows = pl.ds(pl.multiple_of(j * tq, tq), tq)
            s = (_dot_nt(qn, kv_ref[rows, :LANES]) + _dot_nt(qp, kp_ref[rows, :])) * scale
            s = jnp.where(_causal_mask(i, j, tq, tq), s, NEG_INF)
            mn = jnp.maximum(m, jnp.max(s, axis=-1, keepdims=True))
            p = jnp.exp(s - mn)
            a = jnp.exp(m - mn)
            return mn, a * l + jnp.sum(p, axis=-1, keepdims=True), a * acc + _dot(p.astype(BF16), kv_ref[rows, LANES:])

        init = (jnp.full((tq, 1), NEG_INF, F32), jnp.zeros((tq, 1), F32), jnp.zeros((tq, LANES), F32))
        m, l, acc = lax.fori_loop(0, i + 1, step, init)
        o_ref[...] = acc / l
        lse_ref[...] = jnp.broadcast_to(m + jnp.log(l), (tq, LANES))

    in_specs = [pl.BlockSpec((None, tq, 2 * LANES), lambda b, h, i: (b, i, h)),
                pl.BlockSpec((None, seq, 2 * LANES), lambda b, h, i: (b, 0, h)),
                pl.BlockSpec((None, seq, LANES), lambda b, h, i: (b, 0, 0))]
    out_spec = pl.BlockSpec((None, tq, LANES), lambda b, h, i: (b, i, h))
    o, lse = _call(body, name, (batch, h_n, nq), in_specs, [out_spec, out_spec],
                   [_sds((batch, seq, h_n * LANES), F32)] * 2)(
        q_b.reshape(batch, seq, w), kv_b.reshape(batch, seq, w), kp_b.reshape(batch, seq, LANES))
    return o.reshape(t, h_n * LANES), lse.reshape(t, h_n * LANES)


def mla_attn_bwd(q_b, kv_b, kp_b, do, lse, delta, batch, seq, name, tq=256):
    t, w = q_b.shape
    h_n = w // (2 * LANES)
    tq = _tile(seq, tq, SUBLANES)
    nq = seq // tq
    scale = (MLA_NOPE + MLA_ROPE) ** -0.5

    def body(q_ref, kv_ref, kp_ref, do_ref, lse_ref, dl_ref, dq_ref, dkv_ref, dkp_ref):
        dq_ref[...] = jnp.zeros_like(dq_ref)
        dkv_ref[...] = jnp.zeros_like(dkv_ref)

        @pl.when(pl.program_id(1) == 0)
        def _():
            dkp_ref[...] = jnp.zeros_like(dkp_ref)

        def outer(j, _):
            krows = pl.ds(pl.multiple_of(j * tq, tq), tq)
            kn, v, kp = kv_ref[krows, :LANES], kv_ref[krows, LANES:], kp_ref[krows, :]

            def inner(i, _):
                qrows = pl.ds(pl.multiple_of(i * tq, tq), tq)
                qn, qp = q_ref[qrows, :LANES], q_ref[qrows, LANES:]
                dout = do_ref[qrows, :].astype(BF16)
                s = (_dot_nt(qn, kn) + _dot_nt(qp, kp)) * scale
                s = jnp.where(_causal_mask(i, j, tq, tq), s, NEG_INF)
                p = jnp.exp(s - lse_ref[qrows, 0:1])
                ds = (p * (_dot_nt(dout, v) - dl_ref[qrows, 0:1]) * scale).astype(BF16)
                dq_ref[qrows, :LANES] += _dot(ds, kn)
                dq_ref[qrows, LANES:] += _dot(ds, kp)
                dkv_ref[krows, :LANES] += _dot_tn(ds, qn)
                dkv_ref[krows, LANES:] += _dot_tn(p.astype(BF16), dout)
                dkp_ref[krows, :] += _dot_tn(ds, qp)
                return 0

            lax.fori_loop(j, nq, inner, 0)
            return 0

        lax.fori_loop(0, nq, outer, 0)

    wide = pl.BlockSpec((None, seq, 2 * LANES), lambda b, h: (b, 0, h))
    head = pl.BlockSpec((None, seq, LANES), lambda b, h: (b, 0, h))
    shared = pl.BlockSpec((None, seq, LANES), lambda b, h: (b, 0, 0))
    hv = [a.reshape(batch, seq, h_n * LANES) for a in (do, lse, delta)]
    dq, dkv, dkp = _call(body, name, (batch, h_n), [wide, wide, shared, head, head, head], [wide, wide, shared],
                         [_sds((batch, seq, w), F32), _sds((batch, seq, w), F32), _sds((batch, seq, LANES), F32)])(
        q_b.reshape(batch, seq, w), kv_b.reshape(batch, seq, w), kp_b.reshape(batch, seq, LANES), *hv)
    return dq.reshape(t, w), dkv.reshape(t, w), dkp.reshape(t, LANES)


def _sgu_common(z_ref, g_ref, b_ref, ws_ref, bs_ref):
    e = z_ref.shape[-1]
    ge = e // SG_GROUPS
    zu, zv = z_ref[0], z_ref[1]
    u = _gelu(zu)
    xh, r = _ln_stats(_gelu(zv))
    vn = (xh * g_ref[...] + b_ref[...]).astype(BF16)
    tri = (lax.broadcasted_iota(jnp.int32, (SG_CHUNK, SG_CHUNK), 0)
           >= lax.broadcasted_iota(jnp.int32, (SG_CHUNK, SG_CHUNK), 1))
    ws = [jnp.where(tri, ws_ref[g], 0.0).astype(BF16) for g in range(SG_GROUPS)]
    v2 = [_dot(ws[g], vn[:, g * ge:(g + 1) * ge]) + bs_ref[:, g:g + 1] for g in range(SG_GROUPS)]
    return zu, zv, u, xh, r, vn, tri, ws, v2


def sgu_fwd(z2, ln_g, ln_b, w_s, b_s_t, name):
    _, t, e = z2.shape
    ge = e // SG_GROUPS

    def body(z_ref, g_ref, b_ref, ws_ref, bs_ref, y_ref):
        _, _, u, _, _, _, _, _, v2 = _sgu_common(z_ref, g_ref, b_ref, ws_ref, bs_ref)
        for g in range(SG_GROUPS):
            y_ref[:, g * ge:(g + 1) * ge] = (u[:, g * ge:(g + 1) * ge] * v2[g]).astype(BF16)

    vec = pl.BlockSpec((1, e), lambda i: (0, 0))
    in_specs = [pl.BlockSpec((2, SG_CHUNK, e), lambda i: (0, i, 0)), vec, vec,
                pl.BlockSpec((SG_GROUPS, SG_CHUNK, SG_CHUNK), lambda i: (0, 0, 0)),
                pl.BlockSpec((SG_CHUNK, SG_GROUPS), lambda i: (0, 0))]
    return _call(body, name, (t // SG_CHUNK,), in_specs, pl.BlockSpec((SG_CHUNK, e), lambda i: (i, 0)),
                 _sds((t, e), BF16))(z2, ln_g.reshape(1, e), ln_b.reshape(1, e), w_s, b_s_t)


def sgu_bwd(z2, dy, ln_g, ln_b, w_s, b_s_t, name):
    _, t, e = z2.shape
    ge = e // SG_GROUPS

    def body(z_ref, dy_ref, g_ref, b_ref, ws_ref, bs_ref, dz_ref, dbi_ref, dg_ref, db_ref, dws_ref, dbs_ref):
        @pl.when(pl.program_id(0) == 0)
        def _():
            for ref in (dbi_ref, dg_ref, db_ref, dws_ref, dbs_ref):
                ref[...] = jnp.zeros_like(ref)

        zu, zv, u, xh, r, vn, tri, ws, v2 = _sgu_common(z_ref, g_ref, b_ref, ws_ref, bs_ref)
        dy = dy_ref[...]
        dvn = []
        for g in range(SG_GROUPS):
            sl = slice(g * ge, (g + 1) * ge)
            dyg = dy[:, sl]
            du = dyg * v2[g] * _gelu_grad(zu[:, sl])
            dz_ref[0, :, sl] = du.astype(BF16)
            dbi_ref[0, :, sl] += jnp.sum(du, axis=0, keepdims=True)
            dv2 = dyg * u[:, sl]
            dbs_ref[:, g:g + 1] += jnp.sum(dv2, axis=-1, keepdims=True)
            dv2b = dv2.astype(BF16)
            dws_ref[g] += jnp.where(tri, _dot_nt(dv2b, vn[:, sl]), 0.0)
            dvn.append(_dot_tn(ws[g], dv2b))
        dvn = jnp.concatenate(dvn, axis=-1)
        dv, dg, db = _ln_bwd(dvn, xh, r, g_ref[...])
        dzv = dv * _gelu_grad(zv)
        dz_ref[1] = dzv.astype(BF16)
        dbi_ref[1] += jnp.sum(dzv, axis=0, keepdims=True)
        dg_ref[...] += dg
        db_ref[...] += db

    vec = pl.BlockSpec((1, e), lambda i: (0, 0))
    wsb = pl.BlockSpec((SG_GROUPS, SG_CHUNK, SG_CHUNK), lambda i: (0, 0, 0))
    bsb = pl.BlockSpec((SG_CHUNK, SG_GROUPS), lambda i: (0, 0))
    zblk = pl.BlockSpec((2, SG_CHUNK, e), lambda i: (0, i, 0))
    in_specs = [zblk, pl.BlockSpec((SG_CHUNK, e), lambda i: (i, 0)), vec, vec, wsb, bsb]
    out_specs = [zblk, pl.BlockSpec((2, 1, e), lambda i: (0, 0, 0)), vec, vec, wsb, bsb]
    out_shape = [_sds((2, t, e), BF16), _sds((2, 1, e), F32), _sds((1, e), F32), _sds((1, e), F32),
                 _sds((SG_GROUPS, SG_CHUNK, SG_CHUNK), F32), _sds((SG_CHUNK, SG_GROUPS), F32)]
    return _call(body, name, (t // SG_CHUNK,), in_specs, out_specs, out_shape)(
        z2, dy, ln_g.reshape(1, e), ln_b.reshape(1, e), w_s, b_s_t)


_COL = ("cc_w_in", "ffn_w_up", "dsa_w_qkv", "dsa_w_o", "mla_w_qb", "mla_w_kvb", "sg_w_in")
_ROW = ("cc_w_out", "ffn_w_down", "mla_w_in", "mla_w_o", "sg_w_out")
_SMALL_COL = ("cc_dw_w", "ffn_dw_w")
_RELAID = ("cc_w_in", "dsa_w_o", "mla_w_kvb")
_MIXERS = ("cc", "dsa", "mla", "sg")


def _base(name):
    return name.split("_", 1)[1] if name[0] == "l" and name[1].isdigit() else name


def _ffn_fwd(pre, p, full, h_prev, mix_out, bsz, seq):
    t, d = h_prev.shape
    h, hn = rms_fwd(h_prev, p[pre + "norm_ffn"], pre + "ffn_norm", add=mix_out)
    w_up, w_down = full[pre + "ffn_w_up"], full[pre + "ffn_w_down"]
    f = w_down.shape[1]
    z2r = mm_nn(hn[None], w_up, pre + "ffn_up", out_parts=2).reshape(2, bsz, seq, f)
    dw_w2 = jnp.transpose(full[pre + "ffn_dw_w"].reshape(FFN_WIDTH, 2, f), (1, 0, 2))
    dw_b2 = p[pre + "ffn_dw_b"].reshape(2, 1, f)
    y = ffn_gate_fwd(z2r, dw_w2, dw_b2, pre + "ffn_gate").reshape(1, t, f)
    out = mm_nn(y, w_down, pre + "ffn_down")[0]
    return out, dict(h=h, hn=hn, z2r=z2r, y=y, dw_w2=dw_w2, dw_b2=dw_b2)


def relayout_cols(g, name, to_shards):
    k = g.shape[1]
    n = g.shape[2] // N_DEV if to_shards else g.shape[2]
    tk = _row_tile(k, n, itemsize=g.dtype.itemsize, budget=1 << 21)

    def body(i_ref, o_ref):
        o_ref[...] = i_ref[...]

    shard = pl.BlockSpec((None, tk, n), lambda j, ki: (j, ki, 0))
    whole = pl.BlockSpec((None, tk, n), lambda j, ki: (0, ki, j))
    if to_shards:
        return _call(body, name, (N_DEV, k // tk), [whole], shard, _sds((N_DEV, k, n), g.dtype))(g)
    return _call(body, name, (N_DEV, k // tk), [shard], whole, _sds((1, k, N_DEV * n), g.dtype))(g)


def _col_grad(a, b, w, name):
    g = mm_tn(a, b, w.shape[0], name)
    return g if w.shape[0] == N_DEV else relayout_cols(g, name + "_shards", True)


def _ffn_bwd(pre, p, full, ctx, dh_out, big_g, small_g, bsz, seq, mid):
    w_up, w_down = full[pre + "ffn_w_up"], full[pre + "ffn_w_down"]
    f = w_down.shape[1]
    dh_out, dh_b = dh_out
    t, d = dh_out.shape
    dh3 = dh_b[None]
    dyv = mm_nt(dh3, w_down, pre + "ffn_down_dx")
    big_g[pre + "ffn_w_down"] = mm_tn(ctx["y"], dh3, 1, pre + "ffn_down_dw").reshape(N_DEV, f // N_DEV, d)
    mid(dyv)
    dz2, ddw2, ddb2 = ffn_gate_bwd(ctx["z2r"], dyv.reshape(bsz, seq, f), ctx["dw_w2"], ctx["dw_b2"],
                                   pre + "ffn_gate_bwd")
    dz2 = dz2.reshape(2, t, f)
    big_g[pre + "ffn_w_up"] = mm_tn(ctx["hn"][None], dz2, N_DEV, pre + "ffn_up_dw")
    dhn = mm_nt(dz2, w_up, pre + "ffn_up_dx")
    small_g[pre + "ffn_dw_w"] = jnp.transpose(ddw2, (1, 0, 2)).reshape(FFN_WIDTH, 2 * f)
    small_g[pre + "ffn_dw_b"] = ddb2.reshape(2 * f)
    dh, dh_b, dg = rms_bwd(ctx["h"], p[pre + "norm_ffn"], dhn, dh_out, pre + "ffn_norm_bwd")
    small_g[pre + "norm_ffn"] = dg.reshape(d)
    return dh, dh_b


def _cc_fwd(pre, p, full, hn, bsz, seq, aux):
    t, c = hn.shape
    z2r = mm_nn(hn[None], full[pre + "cc_w_in"], pre + "cc_in", bias=p[pre + "cc_b_in"], out_parts=2).reshape(
        2, bsz, seq, c)
    cv = cc_conv_fwd(z2r, full[pre + "cc_dw_w"], p[pre + "cc_dw_b"], pre + "cc_conv").reshape(t, c)
    s = cc_ln_fwd(cv, p[pre + "cc_ln_g"], p[pre + "cc_ln_b"], pre + "cc_ln")
    out = mm_nn(s[None], full[pre + "cc_w_out"], pre + "cc_out", bias=p[pre + "cc_b_out"])[0]
    return out, dict(z2r=z2r, cv=cv, s=s)


def _cc_bwd(pre, p, full, ctx, hn, dm, big_g, small_g, bsz, seq, aux, mid):
    dm, dm_b = dm
    t, c = dm.shape
    dm3 = dm_b[None]
    small_g[pre + "cc_b_out"] = col_sum(dm, pre + "cc_bout_g").reshape(c)
    ds = mm_nt(dm3, full[pre + "cc_w_out"], pre + "cc_out_dx")
    big_g[pre + "cc_w_out"] = mm_tn(ctx["s"][None], dm3, 1, pre + "cc_out_dw").reshape(N_DEV, c // N_DEV, c)
    mid(ds)
    dcv, dlg, dlb = cc_ln_bwd(ctx["cv"], p[pre + "cc_ln_g"], p[pre + "cc_ln_b"], ds, pre + "cc_ln_bwd")
    dz2, dbi, ddw, ddb = cc_conv_bwd(ctx["z2r"], dcv.reshape(bsz, seq, c), full[pre + "cc_dw_w"], pre + "cc_conv_bwd")
    dz2 = dz2.reshape(2, t, c)
    big_g[pre + "cc_w_in"] = _col_grad(hn[None], dz2, full[pre + "cc_w_in"], pre + "cc_in_dw")
    small_g[pre + "cc_ln_g"], small_g[pre + "cc_ln_b"] = dlg.reshape(c), dlb.reshape(c)
    small_g[pre + "cc_b_in"], small_g[pre + "cc_dw_w"], small_g[pre + "cc_dw_b"] = dbi.reshape(2 * c), ddw, ddb.reshape(c)
    return mm_nt(dz2, full[pre + "cc_w_in"], pre + "cc_in_dx")


def _dsa_fwd(pre, p, full, hn, bsz, seq, aux):
    cos, sin = aux["dsa_tables"]
    qkv = mm_nn(hn[None], full[pre + "dsa_w_qkv"], pre + "dsa_qkv")[0]
    qkv_b = dsa_rope_fwd(qkv, cos, sin, seq, pre + "dsa_rope")
    outs, lses = [], []
    for grp, (window, dil) in enumerate(DSA_CONFIGS):
        assert window // dil == DSA_BLOCK and (seq // dil) % DSA_BLOCK == 0
        o, lse = dsa_attn_fwd(qkv_b, grp, dil, bsz, seq, pre + "dsa_attn%d" % grp)
        outs.append(o)
        lses.append(lse)
    o, lse = dsa_merge(outs, lses, pre + "dsa_merge")
    o_b = o.astype(BF16)
    out = mm_nn(o_b[None], full[pre + "dsa_w_o"], pre + "dsa_o")[0]
    return out, dict(qkv_b=qkv_b, o=o, o_b=o_b, lse=lse)


def _dsa_bwd(pre, p, full, ctx, hn, dm, big_g, small_g, bsz, seq, aux, mid):
    cos, sin = aux["dsa_tables"]
    dm, dm_b = dm
    dm3 = dm_b[None]
    do = mm_nt(dm3, full[pre + "dsa_w_o"], pre + "dsa_o_dx")
    big_g[pre + "dsa_w_o"] = _col_grad(ctx["o_b"][None], dm3, full[pre + "dsa_w_o"], pre + "dsa_o_dw")
    mid(do)
    delta = head_delta(do, ctx["o"], pre + "dsa_delta")
    grads = [dsa_attn_bwd(ctx["qkv_b"], do, ctx["lse"], delta, grp, dil, bsz, seq, pre + "dsa_attn_bwd%d" % grp)
             for grp, (_, dil) in enumerate(DSA_CONFIGS)]
    dqkv = dsa_rope_bwd(grads, cos, sin, seq, pre + "dsa_rope_bwd")
    big_g[pre + "dsa_w_qkv"] = mm_tn(hn[None], dqkv, N_DEV, pre + "dsa_qkv_dw")
    return mm_nt(dqkv, full[pre + "dsa_w_qkv"], pre + "dsa_qkv_dx")


def _mla_weights(pre, full):
    rk, hd = MLA_RANK, MLA_NOPE + MLA_ROPE
    w_in = full[pre + "mla_w_in"][0]
    w_in_p = jnp.concatenate([w_in[:, :2 * rk], _spread_rope_cols(w_in[:, 2 * rk:])], axis=1)[None]
    w_qb = jnp.transpose(full[pre + "mla_w_qb"], (1, 0, 2)).reshape(rk, MLA_HEADS, hd)
    w_qb_p = jnp.concatenate([w_qb[..., :MLA_NOPE], _spread_rope_cols(w_qb[..., MLA_NOPE:])], axis=-1)
    return w_in_p, w_qb_p.reshape(1, rk, MLA_HEADS * 2 * LANES)


def _mla_fwd(pre, p, full, hn, bsz, seq, aux):
    cos, sin = aux["mla_tables"]
    w_in_p, w_qb_p = _mla_weights(pre, full)
    c = mm_nn(hn[None], w_in_p, pre + "mla_in")[0]
    qn, kn, kp = mla_low_fwd(c, p[pre + "mla_q_norm"], p[pre + "mla_kv_norm"], cos, sin, seq, pre + "mla_low")
    q = mm_nn(qn[None], w_qb_p, pre + "mla_qb")[0]
    kv = mm_nn(kn[None], full[pre + "mla_w_kvb"], pre + "mla_kvb")[0]
    q_b, kv_b = mla_rope_cast(q, kv, cos, sin, seq, pre + "mla_rope")
    o, lse = mla_attn_fwd(q_b, kv_b, kp, bsz, seq, pre + "mla_attn")
    o_b = o.astype(BF16)
    out = mm_nn(o_b[None], full[pre + "mla_w_o"], pre + "mla_o")[0]
    return out, dict(c=c, qn=qn, kn=kn, kp=kp, q_b=q_b, kv_b=kv_b, o=o, o_b=o_b, lse=lse, w_in_p=w_in_p, w_qb_p=w_qb_p)


def _mla_bwd(pre, p, full, ctx, hn, dm, big_g, small_g, bsz, seq, aux, mid):
    cos, sin = aux["mla_tables"]
    dm, dm_b = dm
    t, d = dm.shape
    rk = MLA_RANK
    dm3 = dm_b[None]
    do = mm_nt(dm3, full[pre + "mla_w_o"], pre + "mla_o_dx")
    big_g[pre + "mla_w_o"] = mm_tn(ctx["o_b"][None], dm3, 1, pre + "mla_o_dw").reshape(N_DEV, -1, d)
    mid(do)
    delta = head_delta(do, ctx["o"], pre + "mla_delta")
    dq, dkv, dkp = mla_attn_bwd(ctx["q_b"], ctx["kv_b"], ctx["kp"], do, ctx["lse"], delta, bsz, seq, pre + "mla_attn_bwd")
    dq_b, dkv_b = mla_rope_cast(dq, dkv, cos, sin, seq, pre + "mla_rope_bwd", transpose=True)
    g_qb = mm_tn(ctx["qn"][None], dq_b[None], 1, pre + "mla_qb_dw")[0].reshape(rk, MLA_HEADS, 2 * LANES)
    g_qb = jnp.concatenate([g_qb[..., :MLA_NOPE], _gather_rope_cols(g_qb[..., MLA_NOPE:])], axis=-1)
    big_g[pre + "mla_w_qb"] = jnp.transpose(g_qb.reshape(rk, N_DEV, -1), (1, 0, 2))
    dqn = mm_nt(dq_b[None], ctx["w_qb_p"], pre + "mla_qb_dx")
    big_g[pre + "mla_w_kvb"] = _col_grad(ctx["kn"][None], dkv_b[None], full[pre + "mla_w_kvb"], pre + "mla_kvb_dw")
    dkn = mm_nt(dkv_b[None], full[pre + "mla_w_kvb"], pre + "mla_kvb_dx")
    dc, dqg, dkg = mla_low_bwd(ctx["c"], p[pre + "mla_q_norm"], p[pre + "mla_kv_norm"], dqn, dkn, dkp, cos, sin, seq,
                               pre + "mla_low_bwd")
    small_g[pre + "mla_q_norm"], small_g[pre + "mla_kv_norm"] = dqg.reshape(rk), dkg.reshape(rk)
    g_in = mm_tn(hn[None], dc, 1, pre + "mla_in_dw")[0]
    g_in = jnp.concatenate([g_in[:, :2 * rk], _gather_rope_cols(g_in[:, 2 * rk:])], axis=1)
    big_g[pre + "mla_w_in"] = g_in.reshape(N_DEV, d // N_DEV, -1)
    return mm_nt(dc, ctx["w_in_p"], pre + "mla_in_dx")


def _sg_fwd(pre, p, full, hn, bsz, seq, aux):
    z2 = mm_nn(hn[None], full[pre + "sg_w_in"], pre + "sg_in", bias=p[pre + "sg_b_in"], out_parts=2)
    b_s_t = jnp.transpose(p[pre + "sg_b_s"])
    y = sgu_fwd(z2, p[pre + "sg_ln_g"], p[pre + "sg_ln_b"], p[pre + "sg_w_s"], b_s_t, pre + "sg_mix")
    out = mm_nn(y[None], full[pre + "sg_w_out"], pre + "sg_out", bias=p[pre + "sg_b_out"])[0]
    return out, dict(z2=z2, y=y, b_s_t=b_s_t)


def _sg_bwd(pre, p, full, ctx, hn, dm, big_g, small_g, bsz, seq, aux, mid):
    dm, dm_b = dm
    t, d = dm.shape
    dm3 = dm_b[None]
    small_g[pre + "sg_b_out"] = col_sum(dm, pre + "sg_bout_g").reshape(d)
    dy = mm_nt(dm3, full[pre + "sg_w_out"], pre + "sg_out_dx")
    big_g[pre + "sg_w_out"] = mm_tn(ctx["y"][None], dm3, 1, pre + "sg_out_dw").reshape(N_DEV, -1, d)
    mid(dy)
    dz2, dbi, dlg, dlb, dws, dbs_t = sgu_bwd(ctx["z2"], dy, p[pre + "sg_ln_g"], p[pre + "sg_ln_b"], p[pre + "sg_w_s"],
                                             ctx["b_s_t"], pre + "sg_mix_bwd")
    big_g[pre + "sg_w_in"] = mm_tn(hn[None], dz2, N_DEV, pre + "sg_in_dw")
    small_g[pre + "sg_b_in"] = dbi.reshape(-1)
    small_g[pre + "sg_ln_g"], small_g[pre + "sg_ln_b"] = dlg.reshape(-1), dlb.reshape(-1)
    small_g[pre + "sg_w_s"], small_g[pre + "sg_b_s"] = dws, jnp.transpose(dbs_t)
    return mm_nt(dz2, full[pre + "sg_w_in"], pre + "sg_in_dx")


_MIX_FWD = dict(cc=_cc_fwd, dsa=_dsa_fwd, mla=_mla_fwd, sg=_sg_fwd)
_MIX_BWD = dict(cc=_cc_bwd, dsa=_dsa_bwd, mla=_mla_bwd, sg=_sg_bwd)


def _reduce_scatter(parts, core, chip, tag, adams=None):
    recv = rs_core_exchange(parts, tag + "_cores")
    sums = [rs_core_add(pt, rc, core, "%s_add%d" % (tag, i)) for i, (pt, rc) in enumerate(zip(parts, recv))]
    recv2 = rs_chip_exchange(sums, tag + "_chips")
    return [rs_finish(s, r2, chip, "%s_fin%d" % (tag, i), adam=None if adams is None else adams[i])
            for i, (s, r2) in enumerate(zip(sums, recv2))]


class _Schedule:
    def __init__(self):
        self.actions = {}

    def at(self, key, fn):
        self.actions.setdefault(key, []).append(fn)

    def run(self, key, x):
        after = list(x) if isinstance(x, (list, tuple)) else [x]
        for fn in self.actions.pop(key, []):
            fn(after)


def _aside(fn):
    saved = list(_PENDING)
    _PENDING.clear()
    out = fn()
    _PENDING[:0] = saved
    return out


def _gather_overlapped(sched, keys, shards, dev, tag, deliver):
    n = len(shards)
    state = {}

    def start(x):
        lands = [lax.dynamic_update_slice(lax.empty((N_DEV,) + s.shape, s.dtype), s[None], (dev,) + (0,) * s.ndim)
                 for s in shards]
        state["first"] = exchange_start(_ag_plan_first, 4 * n, shards, lands, x, tag +"_start")
        _PENDING.append(state["first"][4])

    def pass_on(x):
        _, lands = exchange_wait(_ag_plan_first, state["first"], x, tag +"_wait")
        state["pass"] = exchange_start(_ag_plan_pass, 3 * n, [], lands, [], tag + "_pass")
        _PENDING.append(state["pass"][4])

    def done(x):
        deliver(exchange_wait(_ag_plan_pass, state["pass"], x, tag +"_done")[1])

    for key, fn in zip(keys, (start, pass_on, done)):
        sched.at(key, fn)


def _reduce_overlapped(sched, keys, parts, adams, core, chip, tag, deliver):
    n = len(parts)
    state = {}

    def start(x):
        lands = [lax.empty((4,) + pt.shape[1:], pt.dtype) for pt in parts]
        state["cores"] = exchange_start(_rs_plan_cores, 4 * n, parts, lands, x, tag +"_start")
        _PENDING.append(state["cores"][4])

    def middle(x):
        mine, recv = exchange_wait(_rs_plan_cores, state["cores"], x, tag +"_wait")
        sums = _aside(lambda: [rs_core_add(pt, rc, core, "%s_add%d" % (tag, i))
                               for i, (pt, rc) in enumerate(zip(mine, recv))])
        lands = [lax.empty((3,) + s.shape[1:], s.dtype) for s in sums]
        state["chips"] = exchange_start(_rs_plan_chips, 3 * n, sums, lands, [], tag + "_send")
        _PENDING.append(state["chips"][4])

    def done(x):
        sums, recv = exchange_wait(_rs_plan_chips, state["chips"], x, tag +"_done")
        deliver(_aside(lambda: [rs_finish(s, r2, chip, "%s_fin%d" % (tag, i), adam=adams[i])
                                for i, (s, r2) in enumerate(zip(sums, recv))]))

    for key, fn in zip(keys, (start, middle, done)):
        sched.at(key, fn)


def _pack(arrays):
    flat = jnp.concatenate([a.reshape(-1) for a in arrays])
    unit = N_DEV * SUBLANES * LANES
    padded = -(-flat.shape[0] // unit) * unit
    return jnp.pad(flat, (0, padded - flat.shape[0]))


def _unpack(flat, like):
    out, pos = [], 0
    for a in like:
        out.append(flat[pos:pos + a.size].reshape(a.shape))
        pos += a.size
    return out


def _train_step(p):
    names = list(p)
    wnames = names[1:names.index("loss_target")]
    x = p["x"]
    bsz, seq, d = x.shape
    t = bsz * seq
    xi, yi, ci = lax.axis_index("x"), lax.axis_index("y"), lax.axis_index("c")
    core = jnp.reshape(ci, (1,)).astype(jnp.int32)
    chip = jnp.reshape(2 * xi + yi, (1,)).astype(jnp.int32)
    dev = 4 * xi + 2 * yi + ci
    n_layers = 1 + max(int(n[1]) for n in wnames if n[0] == "l" and n[1].isdigit())
    big = [n for n in wnames if _base(n) in _COL + _ROW]
    aux = dict(dsa_tables=rope_tables_full(seq, LANES), mla_tables=mla_rope_tables(seq))

    sched = _Schedule()
    full = {}

    def deliver_weights(grp):
        def deliver(lands):
            for n, g in zip(grp, lands):
                if _base(n) in _RELAID:
                    full[n] = _aside(lambda: relayout_cols(g, n + "_whole", False))
                elif _base(n) in _COL:
                    full[n] = g
                elif _base(n) in _ROW:
                    full[n] = g.reshape(1, N_DEV * g.shape[1], g.shape[2])
                else:
                    full[n] = jnp.transpose(g, (1, 0, 2)).reshape(g.shape[1], N_DEV * g.shape[2])
        return deliver

    groups = []
    for layer in range(n_layers):
        lw = [n for n in wnames if n.startswith("l%d_" % layer) and _base(n) in _COL + _ROW + _SMALL_COL]
        groups += [[n for n in lw if "_ffn_" not in n], [n for n in lw if "_ffn_" in n]]

    def shards_of(grp):
        return [p[n] if _base(n) in _SMALL_COL else p[n].astype(BF16) for n in grp]

    deliver_weights(groups[0])(all_gather(shards_of(groups[0]), "ag_first"))
    for s in range(1, len(groups)):
        first = max(s - 2, 0)
        keys = [("f", first), ("f", s - 1 if s - 1 > first else s), ("f", s)]
        _gather_overlapped(sched, keys, shards_of(groups[s]), dev, "ag_s%d" % s, deliver_weights(groups[s]))

    h = x.reshape(t, d)
    ffn_out = None
    ctxs = []
    for layer in range(n_layers):
        pre = "l%d_" % layer
        kind = _MIXERS[layer % len(_MIXERS)]
        if ffn_out is None:
            hn = rms_fwd(h, p[pre + "norm_mix"], pre + "mix_norm")
        else:
            h, hn = rms_fwd(h, p[pre + "norm_mix"], pre + "mix_norm", add=ffn_out)
        sched.run(("f", 2 * layer), [hn] + (list(full.values()) if layer == 0 else []))
        mix_out, mctx = _MIX_FWD[kind](pre, p, full, hn, bsz, seq, aux)
        sched.run(("f", 2 * layer + 1), mix_out)
        ffn_out, fctx = _ffn_fwd(pre, p, full, h, mix_out, bsz, seq)
        ctxs.append((h, hn, mctx, fctx))
        h = fctx["h"]

    loss_row, dh, dh_b, dfin = final_loss(h, ffn_out, p["final_norm"], p["loss_target"].reshape(t, d), "final_loss")
    dh = (dh, dh_b)
    loss = lax.psum(loss_row[0, 0], ("x", "y", "c"))

    small_g = {"final_norm": dfin.reshape(d)}
    results = {}

    def deliver_grads(lnames):
        def deliver(outs):
            for n, o in zip(lnames, outs):
                results[n] = o
        return deliver

    def reduce_later(q, lnames, big_g, tag):
        keys = [("b", q + 2), ("b", q + 3), ("b", q + 5)]
        _reduce_overlapped(sched, keys, [big_g[n] for n in lnames], [(p[n], p["m_" + n], p["v_" + n]) for n in lnames],
                           core, chip, tag, deliver_grads(lnames))

    q = 0
    for layer in reversed(range(n_layers)):
        pre = "l%d_" % layer
        kind = _MIXERS[layer % len(_MIXERS)]
        h_in, hn, mctx, fctx = ctxs[layer]
        big_g = {}
        sched.run(("b", q), dh[0])
        dh = _ffn_bwd(pre, p, full, fctx, dh, big_g, small_g, bsz, seq, lambda v, q=q: sched.run(("b", q + 1), v))
        reduce_later(q, [n for n in big if n.startswith(pre) and "_ffn_" in n], big_g, "rs_l%d_ffn" % layer)
        q += 2
        sched.run(("b", q), dh[0])
        dhn = _MIX_BWD[kind](pre, p, full, mctx, hn, dh, big_g, small_g, bsz, seq, aux,
                             lambda v, q=q: sched.run(("b", q + 1), v))
        dh_f, dh_b, dg = rms_bwd(h_in, p[pre + "norm_mix"], dhn, dh[0], pre + "mix_norm_bwd")
        dh = (dh_f, dh_b)
        small_g[pre + "norm_mix"] = dg.reshape(d)
        reduce_later(q, [n for n in big if n.startswith(pre) and "_ffn_" not in n], big_g, "rs_l%d_mix" % layer)
        q += 2
    for tail in range(q, q + 4):
        sched.run(("b", tail), dh[0])
    assert not sched.actions, sched.actions.keys()

    small = [n for n in wnames if n not in big]
    packed = _pack([small_g[n] for n in small]).reshape(N_DEV, -1, LANES)
    reduced = _reduce_scatter([packed], core, chip, "rs_small")[0]
    gathered = all_gather([reduced], "ag_small")[0].reshape(-1)
    g_local = []
    for n, g in zip(small, _unpack(gathered, [small_g[n] for n in small])):
        if _base(n) in _SMALL_COL:
            width = p[n].shape[1]
            g = lax.dynamic_slice_in_dim(g, dev * width, width, axis=1)
        g_local.append(g)
    stacked = [_pack(v).reshape(-1, LANES) for v in
               ([p[n] for n in small], g_local, [p["m_" + n] for n in small], [p["v_" + n] for n in small])]
    upd = adam_flat(*stacked, "adam_small")
    upd = [_unpack(u.reshape(-1), g_local) for u in upd]
    for i, n in enumerate(small):
        results[n] = (g_local[i], upd[0][i], upd[1][i], upd[2][i])

    grad_x = dh[0].reshape(bsz, seq, d)
    cols = [[results[n][k] for n in wnames] for k in range(4)]
    return (loss, grad_x, *cols[0], *cols[1], *cols[2], *cols[3])


def kernel(x, l0_norm_mix, l0_cc_w_in, l0_cc_b_in, l0_cc_dw_w, l0_cc_dw_b, l0_cc_ln_g, l0_cc_ln_b, l0_cc_w_out, l0_cc_b_out, l0_norm_ffn, l0_ffn_w_up, l0_ffn_dw_w, l0_ffn_dw_b, l0_ffn_w_down, l1_norm_mix, l1_dsa_w_qkv, l1_dsa_w_o, l1_norm_ffn, l1_ffn_w_up, l1_ffn_dw_w, l1_ffn_dw_b, l1_ffn_w_down, l2_norm_mix, l2_mla_w_in, l2_mla_q_norm, l2_mla_w_qb, l2_mla_kv_norm, l2_mla_w_kvb, l2_mla_w_o, l2_norm_ffn, l2_ffn_w_up, l2_ffn_dw_w, l2_ffn_dw_b, l2_ffn_w_down, l3_norm_mix, l3_sg_w_in, l3_sg_b_in, l3_sg_ln_g, l3_sg_ln_b, l3_sg_w_s, l3_sg_b_s, l3_sg_w_out, l3_sg_b_out, l3_norm_ffn, l3_ffn_w_up, l3_ffn_dw_w, l3_ffn_dw_b, l3_ffn_w_down, final_norm, loss_target, m_l0_norm_mix, m_l0_cc_w_in, m_l0_cc_b_in, m_l0_cc_dw_w, m_l0_cc_dw_b, m_l0_cc_ln_g, m_l0_cc_ln_b, m_l0_cc_w_out, m_l0_cc_b_out, m_l0_norm_ffn, m_l0_ffn_w_up, m_l0_ffn_dw_w, m_l0_ffn_dw_b, m_l0_ffn_w_down, m_l1_norm_mix, m_l1_dsa_w_qkv, m_l1_dsa_w_o, m_l1_norm_ffn, m_l1_ffn_w_up, m_l1_ffn_dw_w, m_l1_ffn_dw_b, m_l1_ffn_w_down, m_l2_norm_mix, m_l2_mla_w_in, m_l2_mla_q_norm, m_l2_mla_w_qb, m_l2_mla_kv_norm, m_l2_mla_w_kvb, m_l2_mla_w_o, m_l2_norm_ffn, m_l2_ffn_w_up, m_l2_ffn_dw_w, m_l2_ffn_dw_b, m_l2_ffn_w_down, m_l3_norm_mix, m_l3_sg_w_in, m_l3_sg_b_in, m_l3_sg_ln_g, m_l3_sg_ln_b, m_l3_sg_w_s, m_l3_sg_b_s, m_l3_sg_w_out, m_l3_sg_b_out, m_l3_norm_ffn, m_l3_ffn_w_up, m_l3_ffn_dw_w, m_l3_ffn_dw_b, m_l3_ffn_w_down, m_final_norm, v_l0_norm_mix, v_l0_cc_w_in, v_l0_cc_b_in, v_l0_cc_dw_w, v_l0_cc_dw_b, v_l0_cc_ln_g, v_l0_cc_ln_b, v_l0_cc_w_out, v_l0_cc_b_out, v_l0_norm_ffn, v_l0_ffn_w_up, v_l0_ffn_dw_w, v_l0_ffn_dw_b, v_l0_ffn_w_down, v_l1_norm_mix, v_l1_dsa_w_qkv, v_l1_dsa_w_o, v_l1_norm_ffn, v_l1_ffn_w_up, v_l1_ffn_dw_w, v_l1_ffn_dw_b, v_l1_ffn_w_down, v_l2_norm_mix, v_l2_mla_w_in, v_l2_mla_q_norm, v_l2_mla_w_qb, v_l2_mla_kv_norm, v_l2_mla_w_kvb, v_l2_mla_w_o, v_l2_norm_ffn, v_l2_ffn_w_up, v_l2_ffn_dw_w, v_l2_ffn_dw_b, v_l2_ffn_w_down, v_l3_norm_mix, v_l3_sg_w_in, v_l3_sg_b_in, v_l3_sg_ln_g, v_l3_sg_ln_b, v_l3_sg_w_s, v_l3_sg_b_s, v_l3_sg_w_out, v_l3_sg_b_out, v_l3_norm_ffn, v_l3_ffn_w_up, v_l3_ffn_dw_w, v_l3_ffn_dw_b, v_l3_ffn_w_down, v_final_norm):
    return _train_step(dict(locals()))
```

```python
import functools
import inspect
import math

import jax
import jax.numpy as jnp
from jax import lax
from jax.experimental import pallas as pl
from jax.experimental.pallas import tpu as pltpu

F32 = jnp.float32
BF16 = jnp.bfloat16
NORM_EPS = 1e-6
NEG_INF = -1e30
ROPE_THETA = 10000.0
LANES = 128
SUBLANES = 8
VMEM_LIMIT = 56 * 1024 * 1024
MM_VMEM_BUDGET = 44 * 1024 * 1024
N_DEV = 8

DSA_CONFIGS = ((128, 1), (512, 4), (2048, 16))
DSA_HEADS = 8
DSA_BLOCK = 128
MLA_HEADS = 16
MLA_RANK = 512
MLA_NOPE = 128
MLA_ROPE = 64
SG_CHUNK = 128
SG_GROUPS = 8
CC_WIDTH = 31
FFN_WIDTH = 3
ADAM_LR, ADAM_B1, ADAM_B2, ADAM_EPS, ADAM_WD, ADAM_STEP = 0.001, 0.9, 0.999, 1e-08, 0.01, 10


def _tile(n, cap, unit=LANES):
    if n <= cap:
        return n
    best = 0
    for t in range(unit, cap + 1, unit):
        if n % t == 0:
            best = t
    assert best, (n, cap, unit)
    return best


_PENDING = []


def _call(body, name, grid, in_specs, out_specs, out_shape, scratch=(), prefetch=0):
    params = pltpu.CompilerParams(vmem_limit_bytes=VMEM_LIMIT)
    deps = list(_PENDING)
    _PENDING.clear()
    if deps:
        inner, n_in = body, prefetch + len(in_specs)

        def body(*refs):
            return inner(*refs[:n_in], *refs[n_in + len(deps):])

        in_specs = list(in_specs) + [pl.BlockSpec(memory_space=pl.ANY)] * len(deps)
    if prefetch:
        spec = pltpu.PrefetchScalarGridSpec(num_scalar_prefetch=prefetch, grid=grid, in_specs=in_specs,
                                            out_specs=out_specs, scratch_shapes=list(scratch))
        fn = pl.pallas_call(body, out_shape=out_shape, grid_spec=spec, name=name, compiler_params=params)
    else:
        fn = pl.pallas_call(body, out_shape=out_shape, grid=grid, in_specs=in_specs, out_specs=out_specs,
                            scratch_shapes=list(scratch), name=name, compiler_params=params)
    return lambda *args: fn(*args, *deps)


def _sds(shape, dtype):
    return jax.ShapeDtypeStruct(tuple(shape), dtype)


def _bf(v):
    return v if v.dtype == BF16 else v.astype(BF16)


def _dot(a, b):
    return jnp.dot(a, b, preferred_element_type=F32)


def _dot_nt(a, b):
    return lax.dot_general(a, b, (((1,), (1,)), ((), ())), preferred_element_type=F32)


def _dot_tn(a, b):
    return lax.dot_general(a, b, (((0,), (0,)), ((), ())), preferred_element_type=F32)


def _sigmoid(v):
    return 1.0 / (1.0 + jnp.exp(-v))


_ERF_A = (-2.72614225801306e-10, 2.77068142495902e-08, -2.10102402082508e-06, -5.69250639462346e-05,
          -7.34990630326855e-04, -2.95459980854025e-03, -1.60960333262415e-02)
_ERF_B = (-1.45660718464996e-05, -2.13374055278905e-04, -1.68282697438203e-03, -7.37332916720468e-03,
          -1.42647390514189e-02)


def _erf(v):
    v = jnp.clip(v, -4.0, 4.0)
    v2 = v * v
    p = jnp.full_like(v, _ERF_A[0])
    for c in _ERF_A[1:]:
        p = p * v2 + c
    q = jnp.full_like(v, _ERF_B[0])
    for c in _ERF_B[1:]:
        q = q * v2 + c
    return v * p / q


def _gelu_cdf(v):
    return 0.5 * (1.0 + _erf(v * (2.0 ** -0.5)))


def _gelu_grad(v, cdf):
    return cdf + v * jnp.exp(-0.5 * v * v) * ((2.0 * math.pi) ** -0.5)


def _shift_down(v, k):
    if k == 0:
        return v
    rows = lax.broadcasted_iota(jnp.int32, v.shape, 0)
    return jnp.where(rows >= k, pltpu.roll(v, k, 0), 0.0)


def _shift_up(v, k):
    if k == 0:
        return v
    n = v.shape[0]
    rows = lax.broadcasted_iota(jnp.int32, v.shape, 0)
    return jnp.where(rows < n - k, pltpu.roll(v, n - k, 0), 0.0)


def _rope(v, cos, sin):
    return v * cos + pltpu.roll(v, 64, 1) * sin


def _rope_t(dv, cos, sin):
    return dv * cos + pltpu.roll(dv * sin, 64, 1)


def mm_nn(a, w, name, bias=None, out_parts=1, out_dtype=F32, tm_cap=512, tn_cap=1536, tk_cap=6144):
    pa, m, kp = a.shape
    j, k, n = w.shape
    assert pa * kp == k
    big_n = j * n
    npo = big_n // out_parts
    tm = _tile(m, tm_cap, SUBLANES)
    tk = _tile(kp, tk_cap)
    tn = _tile(math.gcd(n, npo), tn_cap if tk <= 2816 else 1024)
    npj, nbo, kbp, nk = n // tn, npo // tn, kp // tk, k // tk
    has_bias = bias is not None

    def body(*refs):
        a_ref, w_ref = refs[0], refs[1]
        b_ref = refs[2] if has_bias else None
        o_ref = refs[2 + has_bias]

        def finish(acc):
            if has_bias:
                acc = acc + b_ref[...]
            o_ref[...] = acc.astype(o_ref.dtype)

        part = _dot(_bf(a_ref[...]), _bf(w_ref[...]))
        if nk == 1:
            finish(part)
        else:
            acc_ref = refs[3 + has_bias]
            ki = pl.program_id(2)

            @pl.when(ki == 0)
            def _():
                acc_ref[...] = part

            @pl.when(ki > 0)
            def _():
                acc_ref[...] += part

            @pl.when(ki == nk - 1)
            def _():
                finish(acc_ref[...])

    in_specs = [pl.BlockSpec((None, tm, tk), lambda ni, mi, ki: (ki // kbp, mi, ki % kbp)),
                pl.BlockSpec((None, tk, tn), lambda ni, mi, ki: (ni // npj, ki, ni % npj))]
    args = [a, w]
    if has_bias:
        in_specs.append(pl.BlockSpec((1, tn), lambda ni, mi, ki: (0, ni)))
        args.append(bias.reshape(1, big_n))
    out_spec = pl.BlockSpec((None, tm, tn), lambda ni, mi, ki: (ni // nbo, mi, ni % nbo))
    scratch = [pltpu.VMEM((tm, tn), F32)] if nk > 1 else []
    return _call(body, name, (big_n // tn, m // tm, nk), in_specs, out_spec,
                 _sds((out_parts, m, npo), out_dtype), scratch)(*args)


def mm_nt(a, w, name, out_dtype=F32, tm_cap=1024, tko_cap=2048, tc_cap=2048):
    pa, m, npa = a.shape
    j, k, n = w.shape
    assert pa * npa == j * n
    tm = _tile(m, tm_cap, SUBLANES)
    tko = _tile(k, tko_cap)
    tc = _tile(math.gcd(n, npa), tc_cap)
    out_bytes = jnp.dtype(out_dtype).itemsize

    def vmem_bytes(rows):
        return 2 * (rows * tc * 2 + tko * tc * 2 + rows * tko * out_bytes) + (rows * tko * 4 if j * n > tc else 0)

    while vmem_bytes(tm) > MM_VMEM_BUDGET and tm % (2 * SUBLANES) == 0:
        tm //= 2
    npj, nba, nc = n // tc, npa // tc, (j * n) // tc

    def body(a_ref, w_ref, o_ref, *scr):
        part = _dot_nt(_bf(a_ref[...]), _bf(w_ref[...]))
        if nc == 1:
            o_ref[...] = part.astype(o_ref.dtype)
        else:
            acc_ref = scr[0]
            ci = pl.program_id(2)

            @pl.when(ci == 0)
            def _():
                acc_ref[...] = part

            @pl.when(ci > 0)
            def _():
                acc_ref[...] += part

            @pl.when(ci == nc - 1)
            def _():
                o_ref[...] = acc_ref[...].astype(o_ref.dtype)

    in_specs = [pl.BlockSpec((None, tm, tc), lambda mi, ko, ci: (ci // nba, mi, ci % nba)),
                pl.BlockSpec((None, tko, tc), lambda mi, ko, ci: (ci // npj, ko, ci % npj))]
    out_spec = pl.BlockSpec((tm, tko), lambda mi, ko, ci: (mi, ko))
    scratch = [pltpu.VMEM((tm, tko), F32)] if nc > 1 else []
    return _call(body, name, (m // tm, k // tko, nc), in_specs, out_spec, _sds((m, k), out_dtype), scratch)(a, w)


def mm_tn(a, b, n_shards, name, out_dtype=BF16, tt_cap=4096, tkk_cap=512, tn_cap=1536):
    pa, m, kp = a.shape
    pb, m2, npb = b.shape
    assert m == m2
    k, big_n = pa * kp, pb * npb
    n = big_n // n_shards
    tt = _tile(m, tt_cap, SUBLANES)
    tkk = _tile(kp, tkk_cap)
    tn = _tile(math.gcd(n, npb), tn_cap)
    kbp, nbb, npj, nt = kp // tkk, npb // tn, n // tn, m // tt

    def body(a_ref, b_ref, o_ref, *scr):
        part = _dot_tn(_bf(a_ref[...]), _bf(b_ref[...]))
        if nt == 1:
            o_ref[...] = part.astype(o_ref.dtype)
        else:
            acc_ref = scr[0]
            ti = pl.program_id(2)

            @pl.when(ti == 0)
            def _():
                acc_ref[...] = part

            @pl.when(ti > 0)
            def _():
                acc_ref[...] += part

            @pl.when(ti == nt - 1)
            def _():
                o_ref[...] = acc_ref[...].astype(o_ref.dtype)

    in_specs = [pl.BlockSpec((None, tt, tkk), lambda ni, ki, ti: (ki // kbp, ti, ki % kbp)),
                pl.BlockSpec((None, tt, tn), lambda ni, ki, ti: (ni // nbb, ti, ni % nbb))]
    out_spec = pl.BlockSpec((None, tkk, tn), lambda ni, ki, ti: (ni // npj, ki, ni % npj))
    scratch = [pltpu.VMEM((tkk, tn), F32)] if nt > 1 else []
    return _call(body, name, (big_n // tn, k // tkk, nt), in_specs, out_spec,
                 _sds((n_shards, k, n), out_dtype), scratch)(a, b)


_ANY = pl.BlockSpec(memory_space=pl.ANY)
_MESH = pl.DeviceIdType.MESH


def all_gather(shards, name):
    n = len(shards)

    def body(*refs):
        ins, outs = refs[:n], refs[n:2 * n]
        send_sems, recv_sems, local_sems = refs[2 * n:]
        x, y, c = lax.axis_index("x"), lax.axis_index("y"), lax.axis_index("c")
        me, sibling = (x, y, c), (x, y, 1 - c)
        chips = [(1 - x, y), (x, 1 - y), (1 - x, 1 - y)]

        def copy(i, k, block, to, src=None):
            px, py, pc = block
            dst = outs[i].at[4 * px + 2 * py + pc]
            return pltpu.make_async_remote_copy(src_ref=dst if src is None else src, dst_ref=dst,
                                                send_sem=send_sems.at[i, k], recv_sem=recv_sems.at[i, k],
                                                device_id=to, device_id_type=_MESH)

        mine = [pltpu.make_async_copy(ins[i], outs[i].at[4 * x + 2 * y + c], local_sems.at[i]) for i in range(n)]
        for cp in mine:
            cp.start()
        first = []
        for i in range(n):
            first.append(copy(i, 0, me, sibling, src=ins[i]))
            first += [copy(i, 1 + q, me, (*chip, c), src=ins[i]) for q, chip in enumerate(chips)]
        for cp in first:
            cp.start()
        passed = []
        for q, chip in enumerate(chips):
            for i in range(n):
                copy(i, 1 + q, (*chip, c), me).wait_recv()
                cp = copy(i, 4 + q, (*chip, c), sibling)
                cp.start()
                passed.append(cp)
        for i in range(n):
            copy(i, 0, sibling, me).wait_recv()
            for q, chip in enumerate(chips):
                copy(i, 4 + q, (*chip, 1 - c), me).wait_recv()
        for cp in first + passed:
            cp.wait_send()
        for cp in mine:
            cp.wait()

    outs = pl.pallas_call(
        body, name=name, out_shape=[_sds((N_DEV,) + s.shape, s.dtype) for s in shards],
        in_specs=[_ANY] * n, out_specs=[_ANY] * n,
        scratch_shapes=[pltpu.SemaphoreType.DMA((n, 7)), pltpu.SemaphoreType.DMA((n, 7)),
                        pltpu.SemaphoreType.DMA((n,))],
    )(*shards)
    return list(outs)


def rs_core_exchange(parts, name):
    n = len(parts)

    def body(*refs):
        ins, outs = refs[:n], refs[n:2 * n]
        send_sems, recv_sems = refs[2 * n:]
        x, y, c = lax.axis_index("x"), lax.axis_index("y"), lax.axis_index("c")
        sibling = (x, y, 1 - c)
        started = []
        for i in range(n):
            for chip in range(4):
                cp = pltpu.make_async_remote_copy(src_ref=ins[i].at[2 * chip + (1 - c)], dst_ref=outs[i].at[chip],
                                                  send_sem=send_sems.at[i, chip], recv_sem=recv_sems.at[i, chip],
                                                  device_id=sibling, device_id_type=_MESH)
                cp.start()
                started.append(cp)
        for cp in started:
            cp.wait()

    outs = pl.pallas_call(
        body, name=name, out_shape=[_sds((4,) + p.shape[1:], p.dtype) for p in parts],
        in_specs=[_ANY] * n, out_specs=[_ANY] * n,
        scratch_shapes=[pltpu.SemaphoreType.DMA((n, 4)), pltpu.SemaphoreType.DMA((n, 4))],
    )(*parts)
    return list(outs)


def rs_chip_exchange(sums, name):
    n = len(sums)

    def body(*refs):
        ins, outs = refs[:n], refs[n:2 * n]
        send_sems, recv_sems = refs[2 * n:]
        x, y, c = lax.axis_index("x"), lax.axis_index("y"), lax.axis_index("c")
        peers = [(x, 1 - y), (1 - x, y), (1 - x, 1 - y)]
        started = []
        for i in range(n):
            for k, (px, py) in enumerate(peers):
                cp = pltpu.make_async_remote_copy(src_ref=ins[i].at[2 * px + py], dst_ref=outs[i].at[k],
                                                  send_sem=send_sems.at[i, k], recv_sem=recv_sems.at[i, k],
                                                  device_id=(px, py, c), device_id_type=_MESH)
                cp.start()
                started.append(cp)
        for cp in started:
            cp.wait()

    outs = pl.pallas_call(
        body, name=name, out_shape=[_sds((3,) + s.shape[1:], s.dtype) for s in sums],
        in_specs=[_ANY] * n, out_specs=[_ANY] * n,
        scratch_shapes=[pltpu.SemaphoreType.DMA((n, 3)), pltpu.SemaphoreType.DMA((n, 3))],
    )(*sums)
    return list(outs)


_HBM = pl.BlockSpec(memory_space=pltpu.HBM)
_SEM = pl.BlockSpec(memory_space=pltpu.SEMAPHORE)
_EFFECT = pltpu.SideEffectType.DATAFLOW_SIDE_EFFECTING


def _in_hbm(a):
    return pltpu.with_memory_space_constraint(a, pltpu.HBM)


def _plan_copies(plan, srcs, dsts, send_sems, recv_sems):
    x, y, c = lax.axis_index("x"), lax.axis_index("y"), lax.axis_index("c")
    return [pltpu.make_async_remote_copy(src_ref=s, dst_ref=d, send_sem=send_sems.at[q], recv_sem=recv_sems.at[q],
                                         device_id=to, device_id_type=_MESH)
            for q, (s, d, to) in enumerate(plan(x, y, c, srcs, dsts))]


def exchange_start(plan, n_copies, srcs, dsts, after, name):
    ns, nd, na = len(srcs), len(dsts), len(after)

    def body(*refs):
        ins = refs[:ns + nd + na]
        send_sems, recv_sems = refs[ns + nd + na], refs[ns + nd + na + 1]
        token = refs[-1]
        for cp in _plan_copies(plan, ins[:ns], ins[ns:ns + nd], send_sems, recv_sems):
            cp.start()
        token[...] = jnp.zeros_like(token)

    thru = [pltpu.HBM(a.shape, a.dtype) for a in list(srcs) + list(dsts)]
    out = pl.pallas_call(
        body, name=name,
        out_shape=[pltpu.SemaphoreType.DMA((n_copies,)), pltpu.SemaphoreType.DMA((n_copies,))] + thru
        + [_sds((SUBLANES, LANES), F32)],
        in_specs=[_HBM] * (ns + nd) + [_ANY] * na,
        out_specs=[_SEM, _SEM] + [_HBM] * (ns + nd) + [pl.BlockSpec(memory_space=pltpu.VMEM)],
        input_output_aliases={i: 2 + i for i in range(ns + nd)},
        compiler_params=pltpu.CompilerParams(has_side_effects=_EFFECT),
    )(*[_in_hbm(a) for a in list(srcs) + list(dsts)], *after)
    return out[0], out[1], list(out[2:2 + ns]), list(out[2 + ns:2 + ns + nd]), out[-1]


def exchange_wait(plan, started, after, name):
    send_sems, recv_sems, srcs, dsts, _ = started
    ns, nd, na = len(srcs), len(dsts), len(after)

    def body(*refs):
        ins = refs[:ns + nd]
        send_sems_ref, recv_sems_ref = refs[ns + nd], refs[ns + nd + 1]
        for cp in _plan_copies(plan, ins[:ns], ins[ns:ns + nd], send_sems_ref, recv_sems_ref):
            cp.wait_send()
            cp.wait_recv()

    thru = [pltpu.HBM(a.shape, a.dtype) for a in list(srcs) + list(dsts)]
    out = pl.pallas_call(
        body, name=name, out_shape=thru,
        in_specs=[_HBM] * (ns + nd) + [_SEM, _SEM] + [_ANY] * na,
        out_specs=[_HBM] * (ns + nd),
        input_output_aliases={i: i for i in range(ns + nd)},
        compiler_params=pltpu.CompilerParams(has_side_effects=_EFFECT),
    )(*srcs, *dsts, send_sems, recv_sems, *after)
    return list(out[:ns]), list(out[ns:])


def _dev_index(px, py, pc):
    return 4 * px + 2 * py + pc


def _ag_plan_first(x, y, c, srcs, dsts):
    me = _dev_index(x, y, c)
    peers = [(x, y, 1 - c), (1 - x, y, c), (x, 1 - y, c), (1 - x, 1 - y, c)]
    return [(s, d.at[me], to) for s, d in zip(srcs, dsts) for to in peers]


def _ag_plan_pass(x, y, c, srcs, dsts):
    chips = [(1 - x, y), (x, 1 - y), (1 - x, 1 - y)]
    return [(d.at[_dev_index(px, py, c)], d.at[_dev_index(px, py, c)], (x, y, 1 - c)) for d in dsts for px, py in chips]


def _rs_plan_cores(x, y, c, srcs, dsts):
    return [(s.at[2 * chip + (1 - c)], d.at[chip], (x, y, 1 - c)) for s, d in zip(srcs, dsts) for chip in range(4)]


def _rs_plan_chips(x, y, c, srcs, dsts):
    peers = [(x, 1 - y), (1 - x, y), (1 - x, 1 - y)]
    return [(s.at[2 * px + py], d.at[k], (px, py, c)) for s, d in zip(srcs, dsts) for k, (px, py) in enumerate(peers)]


def _row_tile(r, c, itemsize=4, budget=1 << 20):
    cap = max(SUBLANES, (budget // (c * itemsize)) // SUBLANES * SUBLANES)
    if r <= cap:
        return r
    best = 0
    for t in range(SUBLANES, cap + 1, SUBLANES):
        if r % t == 0:
            best = t
    return best if best else r


def rs_core_add(part, recv, core, name):
    _, r, c = part.shape
    tr = _row_tile(r, c, budget=1 << 22)

    def body(core_ref, p_ref, q_ref, o_ref):
        o_ref[...] = (p_ref[...].astype(F32) + q_ref[...].astype(F32)).astype(o_ref.dtype)

    in_specs = [pl.BlockSpec((None, tr, c), lambda ch, ri, core_ref: (2 * ch + core_ref[0], ri, 0)),
                pl.BlockSpec((None, tr, c), lambda ch, ri, core_ref: (ch, ri, 0))]
    out_spec = pl.BlockSpec((None, tr, c), lambda ch, ri, core_ref: (ch, ri, 0))
    return _call(body, name, (4, r // tr), in_specs, out_spec, _sds((4, r, c), part.dtype), prefetch=1)(core, part, recv)


def _adamw(w, g, m, v):
    m = ADAM_B1 * m + (1.0 - ADAM_B1) * g
    v = ADAM_B2 * v + (1.0 - ADAM_B2) * (g * g)
    m_hat = m / (1.0 - ADAM_B1 ** ADAM_STEP)
    v_hat = v / (1.0 - ADAM_B2 ** ADAM_STEP)
    delta = -ADAM_LR * (m_hat / (jnp.sqrt(v_hat) + ADAM_EPS) + ADAM_WD * w)
    return delta, m, v


def rs_finish(sums, recv, chip, name, adam=None):
    _, r, c = sums.shape
    tr = _row_tile(r, c, budget=1 << 20)

    def body(chip_ref, s_ref, q_ref, *refs):
        g = s_ref[...].astype(F32)
        for k in range(3):
            g = g + q_ref[k].astype(F32)
        if adam is None:
            refs[0][...] = g
        else:
            w_ref, m_ref, v_ref, g_ref, d_ref, nm_ref, nv_ref = refs
            g_ref[...] = g
            d_ref[...], nm_ref[...], nv_ref[...] = _adamw(w_ref[...], g, m_ref[...], v_ref[...])

    blk = pl.BlockSpec((tr, c), lambda ri, chip_ref: (ri, 0))
    in_specs = [pl.BlockSpec((None, tr, c), lambda ri, chip_ref: (chip_ref[0], ri, 0)),
                pl.BlockSpec((3, tr, c), lambda ri, chip_ref: (0, ri, 0))]
    args = [chip, sums, recv]
    if adam is None:
        out_specs, out_shape = blk, _sds((r, c), F32)
    else:
        in_specs += [blk] * 3
        args += list(adam)
        out_specs, out_shape = [blk] * 4, [_sds((r, c), F32)] * 4
    return _call(body, name, (r // tr,), in_specs, out_specs, out_shape, prefetch=1)(*args)


def adam_flat(w, g, m, v, name):
    r, c = w.shape
    tr = _row_tile(r, c, budget=1 << 19)

    def body(w_ref, g_ref, m_ref, v_ref, d_ref, nm_ref, nv_ref):
        d_ref[...], nm_ref[...], nv_ref[...] = _adamw(w_ref[...], g_ref[...], m_ref[...], v_ref[...])

    blk = pl.BlockSpec((tr, c), lambda ri: (ri, 0))
    return _call(body, name, (r // tr,), [blk] * 4, [blk] * 3, [_sds((r, c), F32)] * 3)(w, g, m, v)


def rms_fwd(x, g, name, add=None):
    t, d = x.shape
    tm = _tile(t, 256, SUBLANES)
    has_add = add is not None

    def body(*refs):
        if has_add:
            x_ref, a_ref, g_ref, h_ref, hn_ref = refs
            h = x_ref[...] + a_ref[...]
            h_ref[...] = h
        else:
            x_ref, g_ref, hn_ref = refs
            h = x_ref[...]
        r = lax.rsqrt(jnp.mean(h * h, axis=-1, keepdims=True) + NORM_EPS)
        hn_ref[...] = (h * r * g_ref[...]).astype(BF16)

    row = pl.BlockSpec((tm, d), lambda i: (i, 0))
    vec = pl.BlockSpec((1, d), lambda i: (0, 0))
    if has_add:
        return _call(body, name, (t // tm,), [row, row, vec], [row, row], [_sds((t, d), F32), _sds((t, d), BF16)])(
            x, add, g.reshape(1, d))
    return _call(body, name, (t // tm,), [row, vec], row, _sds((t, d), BF16))(x, g.reshape(1, d))


def _rms_bwd_math(x, g, dy):
    r = lax.rsqrt(jnp.mean(x * x, axis=-1, keepdims=True) + NORM_EPS)
    xh = x * r
    dyg = dy * g
    dx = r * (dyg - xh * jnp.mean(dyg * xh, axis=-1, keepdims=True))
    return dx, jnp.sum(dy * xh, axis=0, keepdims=True)


def rms_bwd(x, g, dhn, dres, name):
    t, d = x.shape
    tm = _tile(t, 256, SUBLANES)

    def body(x_ref, g_ref, dy_ref, dr_ref, dx_ref, dxb_ref, dg_ref):
        dx, dg = _rms_bwd_math(x_ref[...], g_ref[...], dy_ref[...])
        dx = dx + dr_ref[...]
        dx_ref[...] = dx
        dxb_ref[...] = dx.astype(BF16)

        @pl.when(pl.program_id(0) == 0)
        def _():
            dg_ref[...] = jnp.zeros_like(dg_ref)

        dg_ref[...] += dg

    row = pl.BlockSpec((tm, d), lambda i: (i, 0))
    vec = pl.BlockSpec((1, d), lambda i: (0, 0))
    return _call(body, name, (t // tm,), [row, vec, row, row], [row, row, vec],
                 [_sds((t, d), F32), _sds((t, d), BF16), _sds((1, d), F32)])(x, g.reshape(1, d), dhn, dres)


def final_loss(h, add, g, target, name):
    t, d = h.shape
    tm = _tile(t, 256, SUBLANES)

    def body(x_ref, a_ref, g_ref, t_ref, loss_ref, dx_ref, dxb_ref, dg_ref):
        x = x_ref[...] + a_ref[...]
        gain = g_ref[...]
        r = lax.rsqrt(jnp.mean(x * x, axis=-1, keepdims=True) + NORM_EPS)
        err = x * r * gain - t_ref[...]
        dx, dg = _rms_bwd_math(x, gain, err * (1.0 / d))
        dx_ref[...] = dx
        dxb_ref[...] = dx.astype(BF16)

        @pl.when(pl.program_id(0) == 0)
        def _():
            dg_ref[...] = jnp.zeros_like(dg_ref)
            loss_ref[...] = jnp.zeros_like(loss_ref)

        dg_ref[...] += dg
        part = jnp.sum(jnp.sum(err * err, axis=-1, keepdims=True), axis=0, keepdims=True) * (0.5 / d)
        loss_ref[...] += jnp.broadcast_to(part, loss_ref.shape)

    row = pl.BlockSpec((tm, d), lambda i: (i, 0))
    vec = pl.BlockSpec((1, d), lambda i: (0, 0))
    one = pl.BlockSpec((1, LANES), lambda i: (0, 0))
    return _call(body, name, (t // tm,), [row, row, vec, row], [one, row, row, vec],
                 [_sds((1, LANES), F32), _sds((t, d), F32), _sds((t, d), BF16), _sds((1, d), F32)])(
        h, add, g.reshape(1, d), target)


def _conv_causal(z, w_ref_rows, width):
    out = z * w_ref_rows(width - 1)
    for k in range(width - 1):
        out = out + _shift_down(z, width - 1 - k) * w_ref_rows(k)
    return out


def ffn_gate_fwd(z2, dw_w2, dw_b2, name, tc=LANES):
    _, b, s, f = z2.shape

    def body(z_ref, w_ref, b_ref, y_ref):
        g = _conv_causal(z_ref[0], lambda k: w_ref[0, k:k + 1, :], FFN_WIDTH) + b_ref[0]
        a = _conv_causal(z_ref[1], lambda k: w_ref[1, k:k + 1, :], FFN_WIDTH) + b_ref[1]
        y_ref[...] = (g * _sigmoid(g) * a).astype(BF16)

    in_specs = [pl.BlockSpec((2, None, s, tc), lambda bi, ci: (0, bi, 0, ci)),
                pl.BlockSpec((2, FFN_WIDTH, tc), lambda bi, ci: (0, 0, ci)),
                pl.BlockSpec((2, 1, tc), lambda bi, ci: (0, 0, ci))]
    out_spec = pl.BlockSpec((None, s, tc), lambda bi, ci: (bi, 0, ci))
    return _call(body, name, (b, f // tc), in_specs, out_spec, _sds((b, s, f), BF16))(z2, dw_w2, dw_b2)


def ffn_gate_bwd(z2, dy, dw_w2, dw_b2, name, tc=LANES):
    _, b, s, f = z2.shape

    def body(z_ref, dy_ref, w_ref, b_ref, dz_ref, dw_ref, db_ref):
        @pl.when(pl.program_id(1) == 0)
        def _():
            dw_ref[...] = jnp.zeros_like(dw_ref)
            db_ref[...] = jnp.zeros_like(db_ref)

        zs = (z_ref[0], z_ref[1])
        g = _conv_causal(zs[0], lambda k: w_ref[0, k:k + 1, :], FFN_WIDTH) + b_ref[0]
        a = _conv_causal(zs[1], lambda k: w_ref[1, k:k + 1, :], FFN_WIDTH) + b_ref[1]
        sg = _sigmoid(g)
        dy = dy_ref[...]
        dcs = (dy * a * (sg * (1.0 + g * (1.0 - sg))), dy * (g * sg))
        for p in range(2):
            dc = dcs[p]
            dz = dc * w_ref[p, FFN_WIDTH - 1:FFN_WIDTH, :]
            for k in range(FFN_WIDTH - 1):
                dz = dz + _shift_up(dc, FFN_WIDTH - 1 - k) * w_ref[p, k:k + 1, :]
            dz_ref[p] = dz.astype(BF16)
            for k in range(FFN_WIDTH):
                dw_ref[p, k:k + 1, :] += jnp.sum(dc * _shift_down(zs[p], FFN_WIDTH - 1 - k), axis=0, keepdims=True)
            db_ref[p] += jnp.sum(dc, axis=0, keepdims=True)

    in_specs = [pl.BlockSpec((2, None, s, tc), lambda ci, bi: (0, bi, 0, ci)),
                pl.BlockSpec((None, s, tc), lambda ci, bi: (bi, 0, ci)),
                pl.BlockSpec((2, FFN_WIDTH, tc), lambda ci, bi: (0, 0, ci)),
                pl.BlockSpec((2, 1, tc), lambda ci, bi: (0, 0, ci))]
    out_specs = [pl.BlockSpec((2, None, s, tc), lambda ci, bi: (0, bi, 0, ci)),
                 pl.BlockSpec((2, FFN_WIDTH, tc), lambda ci, bi: (0, 0, ci)),
                 pl.BlockSpec((2, 1, tc), lambda ci, bi: (0, 0, ci))]
    out_shape = [_sds((2, b, s, f), BF16), _sds((2, FFN_WIDTH, f), F32), _sds((2, 1, f), F32)]
    return _call(body, name, (f // tc, b), in_specs, out_specs, out_shape)(z2, dy, dw_w2, dw_b2)


def cc_conv_fwd(z2, dw_w, dw_b, name, tc=LANES):
    _, b, s, c = z2.shape

    def body(z_ref, w_ref, b_ref, o_ref):
        u = z_ref[0] * _sigmoid(z_ref[1])
        o_ref[...] = _conv_causal(u, lambda k: w_ref[k:k + 1, :], CC_WIDTH) + b_ref[...]

    in_specs = [pl.BlockSpec((2, None, s, tc), lambda bi, ci: (0, bi, 0, ci)),
                pl.BlockSpec((CC_WIDTH, tc), lambda bi, ci: (0, ci)),
                pl.BlockSpec((1, tc), lambda bi, ci: (0, ci))]
    out_spec = pl.BlockSpec((None, s, tc), lambda bi, ci: (bi, 0, ci))
    return _call(body, name, (b, c // tc), in_specs, out_spec, _sds((b, s, c), F32))(z2, dw_w, dw_b.reshape(1, c))


def cc_conv_bwd(z2, dcv, dw_w, name, tc=LANES):
    _, b, s, c = z2.shape

    def body(z_ref, dc_ref, w_ref, dz_ref, dbi_ref, dw_ref, db_ref):
        @pl.when(pl.program_id(1) == 0)
        def _():
            dbi_ref[...] = jnp.zeros_like(dbi_ref)
            dw_ref[...] = jnp.zeros_like(dw_ref)
            db_ref[...] = jnp.zeros_like(db_ref)

        a, gate = z_ref[0], z_ref[1]
        sg = _sigmoid(gate)
        u = a * sg
        dc = dc_ref[...]
        du = jnp.zeros_like(dc)
        for j in range(CC_WIDTH):
            sd = _shift_up(dc, j)
            k = CC_WIDTH - 1 - j
            du = du + sd * w_ref[k:k + 1, :]
            dw_ref[k:k + 1, :] += jnp.sum(sd * u, axis=0, keepdims=True)
        db_ref[...] += jnp.sum(dc, axis=0, keepdims=True)
        da = du * sg
        dg = du * a * sg * (1.0 - sg)
        dz_ref[0] = da.astype(BF16)
        dz_ref[1] = dg.astype(BF16)
        dbi_ref[0] += jnp.sum(da, axis=0, keepdims=True)
        dbi_ref[1] += jnp.sum(dg, axis=0, keepdims=True)

    in_specs = [pl.BlockSpec((2, None, s, tc), lambda ci, bi: (0, bi, 0, ci)),
                pl.BlockSpec((None, s, tc), lambda ci, bi: (bi, 0, ci)),
                pl.BlockSpec((CC_WIDTH, tc), lambda ci, bi: (0, ci))]
    out_specs = [pl.BlockSpec((2, None, s, tc), lambda ci, bi: (0, bi, 0, ci)),
                 pl.BlockSpec((2, 1, tc), lambda ci, bi: (0, 0, ci)),
                 pl.BlockSpec((CC_WIDTH, tc), lambda ci, bi: (0, ci)),
                 pl.BlockSpec((1, tc), lambda ci, bi: (0, ci))]
    out_shape = [_sds((2, b, s, c), BF16), _sds((2, 1, c), F32), _sds((CC_WIDTH, c), F32), _sds((1, c), F32)]
    return _call(body, name, (c // tc, b), in_specs, out_specs, out_shape)(z2, dcv, dw_w)


def _ln_stats(v):
    mu = jnp.mean(v, axis=-1, keepdims=True)
    vc = v - mu
    r = lax.rsqrt(jnp.mean(vc * vc, axis=-1, keepdims=True) + NORM_EPS)
    return vc * r, r


def _ln_bwd(dln, xh, r, g):
    dxh = dln * g
    dx = r * (dxh - jnp.mean(dxh, axis=-1, keepdims=True) - xh * jnp.mean(dxh * xh, axis=-1, keepdims=True))
    return dx, jnp.sum(dln * xh, axis=0, keepdims=True), jnp.sum(dln, axis=0, keepdims=True)


def cc_ln_fwd(cv, ln_g, ln_b, name):
    t, c = cv.shape
    tm = _tile(t, 256, SUBLANES)

    def body(x_ref, g_ref, b_ref, o_ref):
        xh, _ = _ln_stats(x_ref[...])
        ln = xh * g_ref[...] + b_ref[...]
        o_ref[...] = (ln * _sigmoid(ln)).astype(BF16)

    row = pl.BlockSpec((tm, c), lambda i: (i, 0))
    vec = pl.BlockSpec((1, c), lambda i: (0, 0))
    return _call(body, name, (t // tm,), [row, vec, vec], row, _sds((t, c), BF16))(
        cv, ln_g.reshape(1, c), ln_b.reshape(1, c))


def cc_ln_bwd(cv, ln_g, ln_b, ds, name):
    t, c = cv.shape
    tm = _tile(t, 256, SUBLANES)

    def body(x_ref, g_ref, b_ref, ds_ref, dx_ref, dg_ref, db_ref):
        @pl.when(pl.program_id(0) == 0)
        def _():
            dg_ref[...] = jnp.zeros_like(dg_ref)
            db_ref[...] = jnp.zeros_like(db_ref)

        xh, r = _ln_stats(x_ref[...])
        ln = xh * g_ref[...] + b_ref[...]
        sg = _sigmoid(ln)
        dln = ds_ref[...] * (sg * (1.0 + ln * (1.0 - sg)))
        dx, dg, db = _ln_bwd(dln, xh, r, g_ref[...])
        dx_ref[...] = dx
        dg_ref[...] += dg
        db_ref[...] += db

    row = pl.BlockSpec((tm, c), lambda i: (i, 0))
    vec = pl.BlockSpec((1, c), lambda i: (0, 0))
    return _call(body, name, (t // tm,), [row, vec, vec, row], [row, vec, vec],
                 [_sds((t, c), F32), _sds((1, c), F32), _sds((1, c), F32)])(
        cv, ln_g.reshape(1, c), ln_b.reshape(1, c), ds)


def col_sum(v, name):
    t, n = v.shape
    tm = _tile(t, 256, SUBLANES)

    def body(v_ref, o_ref):
        @pl.when(pl.program_id(0) == 0)
        def _():
            o_ref[...] = jnp.zeros_like(o_ref)

        o_ref[...] += jnp.sum(v_ref[...], axis=0, keepdims=True)

    return _call(body, name, (t // tm,), [pl.BlockSpec((tm, n), lambda i: (i, 0))],
                 pl.BlockSpec((1, n), lambda i: (0, 0)), _sds((1, n), F32))(v)


def rope_tables_full(seq, dim):
    pos = jnp.arange(seq, dtype=F32)
    inv = ROPE_THETA ** (-(jnp.arange(0, dim, 2, dtype=F32) / dim))
    ang = pos[:, None] * inv[None, :]
    cos, sin = jnp.cos(ang), jnp.sin(ang)
    return jnp.concatenate([cos, cos], axis=-1), jnp.concatenate([-sin, sin], axis=-1)


def dsa_rope_fwd(qkv, cos, sin, seq, name):
    t, w = qkv.shape
    tm = _tile(seq, 256, SUBLANES)
    nsb = seq // tm
    hb = DSA_HEADS

    def body(x_ref, c_ref, s_ref, o_ref):
        cos_t, sin_t = c_ref[...], s_ref[...]
        for blk in range(w // LANES):
            v = x_ref[:, blk * LANES:(blk + 1) * LANES]
            if (blk // hb) % 3 < 2:
                v = _rope(v, cos_t, sin_t)
            o_ref[:, blk * LANES:(blk + 1) * LANES] = v

    row = pl.BlockSpec((tm, w), lambda i: (i, 0))
    tab = pl.BlockSpec((tm, LANES), lambda i: (i % nsb, 0))
    return _call(body, name, (t // tm,), [row, tab, tab], row, _sds((t, w), F32))(qkv, cos, sin)


def _dsa_rows(r, n, dil):
    start = r + n * DSA_BLOCK * dil
    return pl.ds(start, DSA_BLOCK) if dil == 1 else pl.ds(start, DSA_BLOCK, stride=dil)


def _dsa_load(ref, r, blocks, dil):
    parts = [ref[_dsa_rows(r, m, dil), :] for m in blocks]
    return parts[0] if len(parts) == 1 else jnp.concatenate(parts, axis=0)


def dsa_attn_all_fwd(qkv_r, batch, seq, name):
    t, w = qkv_r.shape
    h_n, n_g = DSA_HEADS, len(DSA_CONFIGS)
    scale = DSA_BLOCK ** -0.5
    chunk = _tile(seq, 256, SUBLANES)

    def body(q_ref, k_ref, v_ref, o_ref, ob_ref, lse_ref, og_ref, lg_ref):
        g = pl.program_id(2)
        for gi, (_, dil) in enumerate(DSA_CONFIGS):
            @pl.when(g == gi)
            def _(gi=gi, dil=dil):
                for r in range(dil):
                    for n in range(seq // dil // DSA_BLOCK):
                        blocks = (n - 1, n) if n else (n,)
                        rows = _dsa_rows(r, n, dil)
                        q = q_ref[rows, :].astype(BF16)
                        k = _dsa_load(k_ref, r, blocks, dil).astype(BF16)
                        v = _dsa_load(v_ref, r, blocks, dil).astype(BF16)
                        s = jnp.where(_dsa_mask(n), _dot_nt(q, k) * scale, NEG_INF)
                        m = jnp.max(s, axis=-1, keepdims=True)
                        p = jnp.exp(s - m)
                        l = jnp.sum(p, axis=-1, keepdims=True)
                        og_ref[gi, rows, :] = _dot(p.astype(BF16), v) / l
                        lg_ref[gi, rows, :] = jnp.broadcast_to(m + jnp.log(l), (DSA_BLOCK, LANES))

        @pl.when(g == n_g - 1)
        def _():
            for c0 in range(0, seq, chunk):
                sl = slice(c0, c0 + chunk)
                ls = [lg_ref[gi, sl, :] for gi in range(n_g)]
                m = functools.reduce(jnp.maximum, ls)
                es = [jnp.exp(v - m) for v in ls]
                tot = functools.reduce(lambda a, b: a + b, es)
                acc = jnp.zeros_like(m)
                for gi in range(n_g):
                    acc = acc + (es[gi] / tot) * og_ref[gi, sl, :]
                o_ref[sl, :] = acc
                ob_ref[sl, :] = acc.astype(BF16)
                lse_ref[sl, :] = m + jnp.log(tot)

    def spec(off):
        return pl.BlockSpec((None, seq, LANES), lambda b, h, g: (b, 0, (3 * g + off) * h_n + h))

    out_spec = pl.BlockSpec((None, seq, LANES), lambda b, h, g: (b, 0, h))
    view = qkv_r.reshape(batch, seq, w)
    shape = (batch, seq, h_n * LANES)
    o, o_b, lse = _call(body, name, (batch, h_n, n_g), [spec(0), spec(1), spec(2)], [out_spec] * 3,
                        [_sds(shape, F32), _sds(shape, BF16), _sds(shape, F32)],
                        scratch=[pltpu.VMEM((n_g, seq, LANES), F32)] * 2)(view, view, view)
    return o.reshape(t, -1), o_b.reshape(t, -1), lse.reshape(t, -1)


def dsa_attn_all_bwd(qkv_r, do, lse, delta, batch, seq, name):
    t, w = qkv_r.shape
    h_n, n_g = DSA_HEADS, len(DSA_CONFIGS)
    scale = DSA_BLOCK ** -0.5

    def body(q_ref, k_ref, v_ref, do_ref, lse_ref, dl_ref, d_ref):
        g = pl.program_id(2)
        d_ref[...] = jnp.zeros_like(d_ref)
        for gi, (_, dil) in enumerate(DSA_CONFIGS):
            @pl.when(g == gi)
            def _(dil=dil):
                for r in range(dil):
                    for n in range(seq // dil // DSA_BLOCK):
                        blocks = (n - 1, n) if n else (n,)
                        rows = _dsa_rows(r, n, dil)
                        q = q_ref[rows, :].astype(BF16)
                        k = _dsa_load(k_ref, r, blocks, dil).astype(BF16)
                        v = _dsa_load(v_ref, r, blocks, dil).astype(BF16)
                        dout = do_ref[rows, :].astype(BF16)
                        s = jnp.where(_dsa_mask(n), _dot_nt(q, k) * scale, NEG_INF)
                        p = jnp.exp(s - lse_ref[rows, :][:, 0:1])
                        ds = (p * (_dot_nt(dout, v) - dl_ref[rows, :][:, 0:1]) * scale).astype(BF16)
                        d_ref[0, rows, :] = _dot(ds, k)
                        dk = _dot_tn(ds, q)
                        dv = _dot_tn(p.astype(BF16), dout)
                        for i, blk in enumerate(blocks):
                            brows = _dsa_rows(r, blk, dil)
                            d_ref[1, brows, :] += dk[i * DSA_BLOCK:(i + 1) * DSA_BLOCK]
                            d_ref[2, brows, :] += dv[i * DSA_BLOCK:(i + 1) * DSA_BLOCK]

    def spec(off):
        return pl.BlockSpec((None, seq, LANES), lambda b, h, g: (b, 0, (3 * g + off) * h_n + h))

    head = pl.BlockSpec((None, seq, LANES), lambda b, h, g: (b, 0, h))
    out_spec = pl.BlockSpec((3, None, seq, LANES), lambda b, h, g: (g, b, 0, h))
    view = qkv_r.reshape(batch, seq, w)
    hv = [a.reshape(batch, seq, h_n * LANES) for a in (do, lse, delta)]
    out = _call(body, name, (batch, h_n, n_g), [spec(0), spec(1), spec(2), head, head, head], out_spec,
                _sds((3 * n_g, batch, seq, h_n * LANES), F32))(view, view, view, *hv)
    return out.reshape(3 * n_g, t, h_n * LANES)


def dsa_rope_bwd_all(grads, cos, sin, seq, name):
    n_parts, t, hw = grads.shape
    w = hw * n_parts
    tm = _tile(seq, 128, SUBLANES)
    nsb = seq // tm

    def body(g_ref, c_ref, s_ref, o_ref):
        cos_t, sin_t = c_ref[...], s_ref[...]
        for idx in range(n_parts):
            for h in range(hw // LANES):
                v = g_ref[idx, :, h * LANES:(h + 1) * LANES]
                if idx % 3 < 2:
                    v = _rope_t(v, cos_t, sin_t)
                col = idx * hw + h * LANES
                o_ref[:, col:col + LANES] = v.astype(BF16)

    tab = pl.BlockSpec((tm, LANES), lambda i: (i % nsb, 0))
    out = _call(body, name, (t // tm,), [pl.BlockSpec((n_parts, tm, hw), lambda i: (0, i, 0)), tab, tab],
                pl.BlockSpec((tm, w), lambda i: (i, 0)), _sds((t, w), BF16))(grads, cos, sin)
    return out[None]


def _dsa_mask(n):
    qi = lax.broadcasted_iota(jnp.int32, (DSA_BLOCK, (1 if n == 0 else 2) * DSA_BLOCK), 0)
    kj = lax.broadcasted_iota(jnp.int32, (DSA_BLOCK, (1 if n == 0 else 2) * DSA_BLOCK), 1)
    if n == 0:
        return qi >= kj
    dist = DSA_BLOCK + qi - kj
    return (dist >= 0) & (dist <= DSA_BLOCK)


def head_delta(do, o, name):
    t, hw = do.shape
    tm = _tile(t, 256, SUBLANES)

    def body(do_ref, o_ref, d_ref):
        for h in range(hw // LANES):
            sl = slice(h * LANES, (h + 1) * LANES)
            d = jnp.sum(do_ref[:, sl] * o_ref[:, sl], axis=-1, keepdims=True)
            d_ref[:, sl] = jnp.broadcast_to(d, (tm, LANES))

    row = pl.BlockSpec((tm, hw), lambda i: (i, 0))
    return _call(body, name, (t // tm,), [row, row], row, _sds((t, hw), F32))(do, o)


def mla_rope_tables(seq):
    pos = jnp.arange(seq, dtype=F32)
    inv = ROPE_THETA ** (-(jnp.arange(0, MLA_ROPE, 2, dtype=F32) / MLA_ROPE))
    ang = pos[:, None] * inv[None, :]
    cos, sin, zero = jnp.cos(ang), jnp.sin(ang), jnp.zeros_like(ang)
    return jnp.concatenate([cos, zero, cos, zero], axis=-1), jnp.concatenate([-sin, zero, sin, zero], axis=-1)


def _spread_rope_cols(w_pe):
    half = MLA_ROPE // 2
    zero = jnp.zeros(w_pe.shape[:-1] + (half,), w_pe.dtype)
    return jnp.concatenate([w_pe[..., :half], zero, w_pe[..., half:], zero], axis=-1)


def _gather_rope_cols(g_pe):
    half = MLA_ROPE // 2
    return jnp.concatenate([g_pe[..., :half], g_pe[..., 2 * half:3 * half]], axis=-1)


def mla_low_fwd(c, q_norm, kv_norm, cos, sin, seq, name):
    t, w = c.shape
    rk = MLA_RANK
    tm = _tile(seq, 256, SUBLANES)
    nsb = seq // tm

    def body(c_ref, qg_ref, kg_ref, cs_ref, sn_ref, qn_ref, kn_ref, kp_ref):
        for lo, g_ref, o_ref in ((0, qg_ref, qn_ref), (rk, kg_ref, kn_ref)):
            v = c_ref[:, lo:lo + rk]
            r = lax.rsqrt(jnp.mean(v * v, axis=-1, keepdims=True) + NORM_EPS)
            o_ref[...] = (v * r * g_ref[...]).astype(BF16)
        kp_ref[...] = _rope(c_ref[:, 2 * rk:], cs_ref[...], sn_ref[...]).astype(BF16)

    row = lambda n: pl.BlockSpec((tm, n), lambda i: (i, 0))
    vec = pl.BlockSpec((1, rk), lambda i: (0, 0))
    tab = pl.BlockSpec((tm, LANES), lambda i: (i % nsb, 0))
    return _call(body, name, (t // tm,), [row(w), vec, vec, tab, tab], [row(rk), row(rk), row(LANES)],
                 [_sds((t, rk), BF16), _sds((t, rk), BF16), _sds((t, LANES), BF16)])(
        c, q_norm.reshape(1, rk), kv_norm.reshape(1, rk), cos, sin)


def mla_low_bwd(c, q_norm, kv_norm, dqn, dkn, dkp, cos, sin, seq, name):
    t, w = c.shape
    rk = MLA_RANK
    tm = _tile(seq, 256, SUBLANES)
    nsb = seq // tm

    def body(c_ref, qg_ref, kg_ref, dq_ref, dk_ref, dp_ref, cs_ref, sn_ref, dc_ref, dqg_ref, dkg_ref):
        @pl.when(pl.program_id(0) == 0)
        def _():
            dqg_ref[...] = jnp.zeros_like(dqg_ref)
            dkg_ref[...] = jnp.zeros_like(dkg_ref)

        for lo, g_ref, d_ref, dg_ref in ((0, qg_ref, dq_ref, dqg_ref), (rk, kg_ref, dk_ref, dkg_ref)):
            dx, dg = _rms_bwd_math(c_ref[:, lo:lo + rk], g_ref[...], d_ref[...])
            dc_ref[:, lo:lo + rk] = dx.astype(BF16)
            dg_ref[...] += dg
        dc_ref[:, 2 * rk:] = _rope_t(dp_ref[...], cs_ref[...], sn_ref[...]).astype(BF16)

    row = lambda n: pl.BlockSpec((tm, n), lambda i: (i, 0))
    vec = pl.BlockSpec((1, rk), lambda i: (0, 0))
    tab = pl.BlockSpec((tm, LANES), lambda i: (i % nsb, 0))
    dc, dqg, dkg = _call(body, name, (t // tm,), [row(w), vec, vec, row(rk), row(rk), row(LANES), tab, tab],
                         [row(w), vec, vec], [_sds((t, w), BF16), _sds((1, rk), F32), _sds((1, rk), F32)])(
        c, q_norm.reshape(1, rk), kv_norm.reshape(1, rk), dqn, dkn, dkp, cos, sin)
    return dc[None], dqg, dkg


def mla_rope_cast(q, kv, cos, sin, seq, name, transpose=False):
    t, w = q.shape
    tm = _tile(seq, 256, SUBLANES)
    nsb = seq // tm
    fn = _rope_t if transpose else _rope

    def body(q_ref, kv_ref, cs_ref, sn_ref, qo_ref, kvo_ref):
        cos_t, sin_t = cs_ref[...], sn_ref[...]
        for blk in range(w // LANES):
            sl = slice(blk * LANES, (blk + 1) * LANES)
            v = q_ref[:, sl]
            if blk % 2 == 1:
                v = fn(v, cos_t, sin_t)
            qo_ref[:, sl] = v.astype(BF16)
        kvo_ref[...] = kv_ref[...].astype(BF16)

    row = pl.BlockSpec((tm, w), lambda i: (i, 0))
    tab = pl.BlockSpec((tm, LANES), lambda i: (i % nsb, 0))
    return _call(body, name, (t // tm,), [row, row, tab, tab], [row, row], [_sds((t, w), BF16)] * 2)(q, kv, cos, sin)


def _causal_mask(i, j, tq, tk):
    qpos = i * tq + lax.broadcasted_iota(jnp.int32, (tq, tk), 0)
    kpos = j * tk + lax.broadcasted_iota(jnp.int32, (tq, tk), 1)
    return kpos <= qpos


def mla_attn_fwd(q_b, kv_b, kp_b, batch, seq, name, tq=512):
    t, w = q_b.shape
    h_n = w // (2 * LANES)
    tq = _tile(seq, tq, SUBLANES)
    nq = seq // tq
    scale = (MLA_NOPE + MLA_ROPE) ** -0.5

    def body(q_ref, kv_ref, kp_ref, o_ref, lse_ref):
        i = pl.program_id(2)
        qn, qp = q_ref[:, :LANES], q_ref[:, LANES:]

        def step(j, carry):
            m, l, acc = carry
            rows = pl.ds(pl.multiple_of(j * tq, tq), tq)
            s = (_dot_nt(qn, kv_ref[rows, :LANES]) + _dot_nt(qp, kp_ref[rows, :])) * scale
            s = jnp.where(_causal_mask(i, j, tq, tq), s, NEG_INF)
            mn = jnp.maximum(m, jnp.max(s, axis=-1, keepdims=True))
            p = jnp.exp(s - mn)
            a = jnp.exp(m - mn)
            return mn, a * l + jnp.sum(p, axis=-1, keepdims=True), a * acc + _dot(p.astype(BF16), kv_ref[rows, LANES:])

        init = (jnp.full((tq, 1), NEG_INF, F32), jnp.zeros((tq, 1), F32), jnp.zeros((tq, LANES), F32))
        m, l, acc = lax.fori_loop(0, i + 1, step, init)
        o_ref[...] = acc / l
        lse_ref[...] = jnp.broadcast_to(m + jnp.log(l), (tq, LANES))

    in_specs = [pl.BlockSpec((None, tq, 2 * LANES), lambda b, h, i: (b, i, h)),
                pl.BlockSpec((None, seq, 2 * LANES), lambda b, h, i: (b, 0, h)),
                pl.BlockSpec((None, seq, LANES), lambda b, h, i: (b, 0, 0))]
    out_spec = pl.BlockSpec((None, tq, LANES), lambda b, h, i: (b, i, h))
    o, lse = _call(body, name, (batch, h_n, nq), in_specs, [out_spec, out_spec],
                   [_sds((batch, seq, h_n * LANES), F32)] * 2)(
        q_b.reshape(batch, seq, w), kv_b.reshape(batch, seq, w), kp_b.reshape(batch, seq, LANES))
    return o.reshape(t, h_n * LANES), lse.reshape(t, h_n * LANES)


def mla_attn_bwd(q_b, kv_b, kp_b, do, lse, delta, batch, seq, name, tq=512):
    t, w = q_b.shape
    h_n = w // (2 * LANES)
    tq = _tile(seq, tq, SUBLANES)
    nq = seq // tq
    scale = (MLA_NOPE + MLA_ROPE) ** -0.5

    def body(q_ref, kv_ref, kp_ref, do_ref, lse_ref, dl_ref, dq_ref, dkv_ref, dkp_ref):
        dq_ref[...] = jnp.zeros_like(dq_ref)
        dkv_ref[...] = jnp.zeros_like(dkv_ref)

        @pl.when(pl.program_id(1) == 0)
        def _():
            dkp_ref[...] = jnp.zeros_like(dkp_ref)

        def outer(j, _):
            krows = pl.ds(pl.multiple_of(j * tq, tq), tq)
            kn, v, kp = kv_ref[krows, :LANES], kv_ref[krows, LANES:], kp_ref[krows, :]

            def inner(i, _):
                qrows = pl.ds(pl.multiple_of(i * tq, tq), tq)
                qn, qp = q_ref[qrows, :LANES], q_ref[qrows, LANES:]
                dout = do_ref[qrows, :].astype(BF16)
                s = (_dot_nt(qn, kn) + _dot_nt(qp, kp)) * scale
                s = jnp.where(_causal_mask(i, j, tq, tq), s, NEG_INF)
                p = jnp.exp(s - lse_ref[qrows, 0:1])
                ds = (p * (_dot_nt(dout, v) - dl_ref[qrows, 0:1]) * scale).astype(BF16)
                dq_ref[qrows, :LANES] += _dot(ds, kn)
                dq_ref[qrows, LANES:] += _dot(ds, kp)
                dkv_ref[krows, :LANES] += _dot_tn(ds, qn)
                dkv_ref[krows, LANES:] += _dot_tn(p.astype(BF16), dout)
                dkp_ref[krows, :] += _dot_tn(ds, qp)
                return 0

            lax.fori_loop(j, nq, inner, 0)
            return 0

        lax.fori_loop(0, nq, outer, 0)

    wide = pl.BlockSpec((None, seq, 2 * LANES), lambda b, h: (b, 0, h))
    head = pl.BlockSpec((None, seq, LANES), lambda b, h: (b, 0, h))
    shared = pl.BlockSpec((None, seq, LANES), lambda b, h: (b, 0, 0))
    hv = [a.reshape(batch, seq, h_n * LANES) for a in (do, lse, delta)]
    dq, dkv, dkp = _call(body, name, (batch, h_n), [wide, wide, shared, head, head, head], [wide, wide, shared],
                         [_sds((batch, seq, w), F32), _sds((batch, seq, w), F32), _sds((batch, seq, LANES), F32)])(
        q_b.reshape(batch, seq, w), kv_b.reshape(batch, seq, w), kp_b.reshape(batch, seq, LANES), *hv)
    return dq.reshape(t, w), dkv.reshape(t, w), dkp.reshape(t, LANES)


def _sgu_common(z_ref, g_ref, b_ref, ws_ref, bs_ref):
    e = z_ref.shape[-1]
    ge = e // SG_GROUPS
    zu, zv = z_ref[0], z_ref[1]
    cu, cv = _gelu_cdf(zu), _gelu_cdf(zv)
    u = zu * cu
    xh, r = _ln_stats(zv * cv)
    vn = (xh * g_ref[...] + b_ref[...]).astype(BF16)
    tri = (lax.broadcasted_iota(jnp.int32, (SG_CHUNK, SG_CHUNK), 0)
           >= lax.broadcasted_iota(jnp.int32, (SG_CHUNK, SG_CHUNK), 1))
    ws = [jnp.where(tri, ws_ref[g], 0.0).astype(BF16) for g in range(SG_GROUPS)]
    v2 = [_dot(ws[g], vn[:, g * ge:(g + 1) * ge]) + bs_ref[:, g:g + 1] for g in range(SG_GROUPS)]
    return zu, zv, cu, cv, u, xh, r, vn, tri, ws, v2


def sgu_fwd(z2, ln_g, ln_b, w_s, b_s_t, name):
    _, t, e = z2.shape
    ge = e // SG_GROUPS

    def body(z_ref, g_ref, b_ref, ws_ref, bs_ref, y_ref):
        _, _, _, _, u, _, _, _, _, _, v2 = _sgu_common(z_ref, g_ref, b_ref, ws_ref, bs_ref)
        for g in range(SG_GROUPS):
            y_ref[:, g * ge:(g + 1) * ge] = (u[:, g * ge:(g + 1) * ge] * v2[g]).astype(BF16)

    vec = pl.BlockSpec((1, e), lambda i: (0, 0))
    in_specs = [pl.BlockSpec((2, SG_CHUNK, e), lambda i: (0, i, 0)), vec, vec,
                pl.BlockSpec((SG_GROUPS, SG_CHUNK, SG_CHUNK), lambda i: (0, 0, 0)),
                pl.BlockSpec((SG_CHUNK, SG_GROUPS), lambda i: (0, 0))]
    return _call(body, name, (t // SG_CHUNK,), in_specs, pl.BlockSpec((SG_CHUNK, e), lambda i: (i, 0)),
                 _sds((t, e), BF16))(z2, ln_g.reshape(1, e), ln_b.reshape(1, e), w_s, b_s_t)


def sgu_bwd(z2, dy, ln_g, ln_b, w_s, b_s_t, name):
    _, t, e = z2.shape
    ge = e // SG_GROUPS

    def body(z_ref, dy_ref, g_ref, b_ref, ws_ref, bs_ref, dz_ref, dbi_ref, dg_ref, db_ref, dws_ref, dbs_ref):
        @pl.when(pl.program_id(0) == 0)
        def _():
            for ref in (dbi_ref, dg_ref, db_ref, dws_ref, dbs_ref):
                ref[...] = jnp.zeros_like(ref)

        zu, zv, cu, cv, u, xh, r, vn, tri, ws, v2 = _sgu_common(z_ref, g_ref, b_ref, ws_ref, bs_ref)
        dy = dy_ref[...]
        dvn = []
        for g in range(SG_GROUPS):
            sl = slice(g * ge, (g + 1) * ge)
            dyg = dy[:, sl]
            du = dyg * v2[g] * _gelu_grad(zu[:, sl], cu[:, sl])
            dz_ref[0, :, sl] = du.astype(BF16)
            dbi_ref[0, :, sl] += jnp.sum(du, axis=0, keepdims=True)
            dv2 = dyg * u[:, sl]
            dbs_ref[:, g:g + 1] += jnp.sum(dv2, axis=-1, keepdims=True)
            dv2b = dv2.astype(BF16)
            dws_ref[g] += jnp.where(tri, _dot_nt(dv2b, vn[:, sl]), 0.0)
            dvn.append(_dot_tn(ws[g], dv2b))
        dvn = jnp.concatenate(dvn, axis=-1)
        dv, dg, db = _ln_bwd(dvn, xh, r, g_ref[...])
        dzv = dv * _gelu_grad(zv, cv)
        dz_ref[1] = dzv.astype(BF16)
        dbi_ref[1] += jnp.sum(dzv, axis=0, keepdims=True)
        dg_ref[...] += dg
        db_ref[...] += db

    vec = pl.BlockSpec((1, e), lambda i: (0, 0))
    wsb = pl.BlockSpec((SG_GROUPS, SG_CHUNK, SG_CHUNK), lambda i: (0, 0, 0))
    bsb = pl.BlockSpec((SG_CHUNK, SG_GROUPS), lambda i: (0, 0))
    zblk = pl.BlockSpec((2, SG_CHUNK, e), lambda i: (0, i, 0))
    in_specs = [zblk, pl.BlockSpec((SG_CHUNK, e), lambda i: (i, 0)), vec, vec, wsb, bsb]
    out_specs = [zblk, pl.BlockSpec((2, 1, e), lambda i: (0, 0, 0)), vec, vec, wsb, bsb]
    out_shape = [_sds((2, t, e), BF16), _sds((2, 1, e), F32), _sds((1, e), F32), _sds((1, e), F32),
                 _sds((SG_GROUPS, SG_CHUNK, SG_CHUNK), F32), _sds((SG_CHUNK, SG_GROUPS), F32)]
    return _call(body, name, (t // SG_CHUNK,), in_specs, out_specs, out_shape)(
        z2, dy, ln_g.reshape(1, e), ln_b.reshape(1, e), w_s, b_s_t)


_COL = ("cc_w_in", "ffn_w_up", "dsa_w_qkv", "dsa_w_o", "mla_w_qb", "mla_w_kvb", "sg_w_in")
_ROW = ("cc_w_out", "ffn_w_down", "mla_w_in", "mla_w_o", "sg_w_out")
_SMALL_COL = ("cc_dw_w", "ffn_dw_w")
_RELAID = ("cc_w_in", "dsa_w_o", "mla_w_kvb")
_MIXERS = ("cc", "dsa", "mla", "sg")


def _base(name):
    return name.split("_", 1)[1] if name[0] == "l" and name[1].isdigit() else name


def _ffn_fwd(pre, p, full, h_prev, mix_out, bsz, seq):
    t, d = h_prev.shape
    h, hn = rms_fwd(h_prev, p[pre + "norm_ffn"], pre + "ffn_norm", add=mix_out)
    w_up, w_down = full[pre + "ffn_w_up"], full[pre + "ffn_w_down"]
    f = w_down.shape[1]
    z2r = mm_nn(hn[None], w_up, pre + "ffn_up", out_parts=2).reshape(2, bsz, seq, f)
    dw_w2 = jnp.transpose(full[pre + "ffn_dw_w"].reshape(FFN_WIDTH, 2, f), (1, 0, 2))
    dw_b2 = p[pre + "ffn_dw_b"].reshape(2, 1, f)
    y = ffn_gate_fwd(z2r, dw_w2, dw_b2, pre + "ffn_gate").reshape(1, t, f)
    out = mm_nn(y, w_down, pre + "ffn_down")[0]
    return out, dict(h=h, hn=hn, z2r=z2r, y=y, dw_w2=dw_w2, dw_b2=dw_b2)


def relayout_cols(g, name, to_shards):
    k = g.shape[1]
    n = g.shape[2] // N_DEV if to_shards else g.shape[2]
    tk = _row_tile(k, n, itemsize=g.dtype.itemsize, budget=1 << 21)

    def body(i_ref, o_ref):
        o_ref[...] = i_ref[...]

    shard = pl.BlockSpec((None, tk, n), lambda j, ki: (j, ki, 0))
    whole = pl.BlockSpec((None, tk, n), lambda j, ki: (0, ki, j))
    if to_shards:
        return _call(body, name, (N_DEV, k // tk), [whole], shard, _sds((N_DEV, k, n), g.dtype))(g)
    return _call(body, name, (N_DEV, k // tk), [shard], whole, _sds((1, k, N_DEV * n), g.dtype))(g)


def _col_grad(a, b, w, name):
    g = mm_tn(a, b, w.shape[0], name)
    return g if w.shape[0] == N_DEV else relayout_cols(g, name + "_shards", True)


def _ffn_bwd(pre, p, full, ctx, dh_out, big_g, small_g, bsz, seq, mid):
    w_up, w_down = full[pre + "ffn_w_up"], full[pre + "ffn_w_down"]
    f = w_down.shape[1]
    dh_out, dh_b = dh_out
    t, d = dh_out.shape
    dh3 = dh_b[None]
    dyv = mm_nt(dh3, w_down, pre + "ffn_down_dx")
    big_g[pre + "ffn_w_down"] = mm_tn(ctx["y"], dh3, 1, pre + "ffn_down_dw").reshape(N_DEV, f // N_DEV, d)
    mid(dyv)
    dz2, ddw2, ddb2 = ffn_gate_bwd(ctx["z2r"], dyv.reshape(bsz, seq, f), ctx["dw_w2"], ctx["dw_b2"],
                                   pre + "ffn_gate_bwd")
    dz2 = dz2.reshape(2, t, f)
    big_g[pre + "ffn_w_up"] = mm_tn(ctx["hn"][None], dz2, N_DEV, pre + "ffn_up_dw")
    dhn = mm_nt(dz2, w_up, pre + "ffn_up_dx")
    small_g[pre + "ffn_dw_w"] = jnp.transpose(ddw2, (1, 0, 2)).reshape(FFN_WIDTH, 2 * f)
    small_g[pre + "ffn_dw_b"] = ddb2.reshape(2 * f)
    dh, dh_b, dg = rms_bwd(ctx["h"], p[pre + "norm_ffn"], dhn, dh_out, pre + "ffn_norm_bwd")
    small_g[pre + "norm_ffn"] = dg.reshape(d)
    return dh, dh_b


def _cc_fwd(pre, p, full, hn, bsz, seq, aux):
    t, c = hn.shape
    z2r = mm_nn(hn[None], full[pre + "cc_w_in"], pre + "cc_in", bias=p[pre + "cc_b_in"], out_parts=2).reshape(
        2, bsz, seq, c)
    cv = cc_conv_fwd(z2r, full[pre + "cc_dw_w"], p[pre + "cc_dw_b"], pre + "cc_conv").reshape(t, c)
    s = cc_ln_fwd(cv, p[pre + "cc_ln_g"], p[pre + "cc_ln_b"], pre + "cc_ln")
    out = mm_nn(s[None], full[pre + "cc_w_out"], pre + "cc_out", bias=p[pre + "cc_b_out"])[0]
    return out, dict(z2r=z2r, cv=cv, s=s)


def _cc_bwd(pre, p, full, ctx, hn, dm, big_g, small_g, bsz, seq, aux, mid):
    dm, dm_b = dm
    t, c = dm.shape
    dm3 = dm_b[None]
    small_g[pre + "cc_b_out"] = col_sum(dm, pre + "cc_bout_g").reshape(c)
    ds = mm_nt(dm3, full[pre + "cc_w_out"], pre + "cc_out_dx")
    big_g[pre + "cc_w_out"] = mm_tn(ctx["s"][None], dm3, 1, pre + "cc_out_dw").reshape(N_DEV, c // N_DEV, c)
    mid(ds)
    dcv, dlg, dlb = cc_ln_bwd(ctx["cv"], p[pre + "cc_ln_g"], p[pre + "cc_ln_b"], ds, pre + "cc_ln_bwd")
    dz2, dbi, ddw, ddb = cc_conv_bwd(ctx["z2r"], dcv.reshape(bsz, seq, c), full[pre + "cc_dw_w"], pre + "cc_conv_bwd")
    dz2 = dz2.reshape(2, t, c)
    big_g[pre + "cc_w_in"] = _col_grad(hn[None], dz2, full[pre + "cc_w_in"], pre + "cc_in_dw")
    small_g[pre + "cc_ln_g"], small_g[pre + "cc_ln_b"] = dlg.reshape(c), dlb.reshape(c)
    small_g[pre + "cc_b_in"], small_g[pre + "cc_dw_w"], small_g[pre + "cc_dw_b"] = dbi.reshape(2 * c), ddw, ddb.reshape(c)
    return mm_nt(dz2, full[pre + "cc_w_in"], pre + "cc_in_dx")


def _dsa_fwd(pre, p, full, hn, bsz, seq, aux):
    cos, sin = aux["dsa_tables"]
    qkv = mm_nn(hn[None], full[pre + "dsa_w_qkv"], pre + "dsa_qkv")[0]
    qkv_r = dsa_rope_fwd(qkv, cos, sin, seq, pre + "dsa_rope")
    for window, dil in DSA_CONFIGS:
        assert window // dil == DSA_BLOCK and (seq // dil) % DSA_BLOCK == 0
    o, o_b, lse = dsa_attn_all_fwd(qkv_r, bsz, seq, pre + "dsa_attn")
    out = mm_nn(o_b[None], full[pre + "dsa_w_o"], pre + "dsa_o")[0]
    return out, dict(qkv_r=qkv_r, o=o, o_b=o_b, lse=lse)


def _dsa_bwd(pre, p, full, ctx, hn, dm, big_g, small_g, bsz, seq, aux, mid):
    cos, sin = aux["dsa_tables"]
    dm, dm_b = dm
    dm3 = dm_b[None]
    do = mm_nt(dm3, full[pre + "dsa_w_o"], pre + "dsa_o_dx")
    big_g[pre + "dsa_w_o"] = _col_grad(ctx["o_b"][None], dm3, full[pre + "dsa_w_o"], pre + "dsa_o_dw")
    mid(do)
    delta = head_delta(do, ctx["o"], pre + "dsa_delta")
    grads = dsa_attn_all_bwd(ctx["qkv_r"], do, ctx["lse"], delta, bsz, seq, pre + "dsa_attn_bwd")
    dqkv = dsa_rope_bwd_all(grads, cos, sin, seq, pre + "dsa_rope_bwd")
    big_g[pre + "dsa_w_qkv"] = mm_tn(hn[None], dqkv, N_DEV, pre + "dsa_qkv_dw")
    return mm_nt(dqkv, full[pre + "dsa_w_qkv"], pre + "dsa_qkv_dx")


def _mla_weights(pre, full):
    rk, hd = MLA_RANK, MLA_NOPE + MLA_ROPE
    w_in = full[pre + "mla_w_in"][0]
    w_in_p = jnp.concatenate([w_in[:, :2 * rk], _spread_rope_cols(w_in[:, 2 * rk:])], axis=1)[None]
    w_qb = jnp.transpose(full[pre + "mla_w_qb"], (1, 0, 2)).reshape(rk, MLA_HEADS, hd)
    w_qb_p = jnp.concatenate([w_qb[..., :MLA_NOPE], _spread_rope_cols(w_qb[..., MLA_NOPE:])], axis=-1)
    return w_in_p, w_qb_p.reshape(1, rk, MLA_HEADS * 2 * LANES)


def _mla_fwd(pre, p, full, hn, bsz, seq, aux):
    cos, sin = aux["mla_tables"]
    w_in_p, w_qb_p = _mla_weights(pre, full)
    c = mm_nn(hn[None], w_in_p, pre + "mla_in")[0]
    qn, kn, kp = mla_low_fwd(c, p[pre + "mla_q_norm"], p[pre + "mla_kv_norm"], cos, sin, seq, pre + "mla_low")
    q = mm_nn(qn[None], w_qb_p, pre + "mla_qb")[0]
    kv = mm_nn(kn[None], full[pre + "mla_w_kvb"], pre + "mla_kvb")[0]
    q_b, kv_b = mla_rope_cast(q, kv, cos, sin, seq, pre + "mla_rope")
    o, lse = mla_attn_fwd(q_b, kv_b, kp, bsz, seq, pre + "mla_attn")
    o_b = o.astype(BF16)
    out = mm_nn(o_b[None], full[pre + "mla_w_o"], pre + "mla_o")[0]
    return out, dict(c=c, qn=qn, kn=kn, kp=kp, q_b=q_b, kv_b=kv_b, o=o, o_b=o_b, lse=lse, w_in_p=w_in_p, w_qb_p=w_qb_p)


def _mla_bwd(pre, p, full, ctx, hn, dm, big_g, small_g, bsz, seq, aux, mid):
    cos, sin = aux["mla_tables"]
    dm, dm_b = dm
    t, d = dm.shape
    rk = MLA_RANK
    dm3 = dm_b[None]
    do = mm_nt(dm3, full[pre + "mla_w_o"], pre + "mla_o_dx")
    big_g[pre + "mla_w_o"] = mm_tn(ctx["o_b"][None], dm3, 1, pre + "mla_o_dw").reshape(N_DEV, -1, d)
    mid(do)
    delta = head_delta(do, ctx["o"], pre + "mla_delta")
    dq, dkv, dkp = mla_attn_bwd(ctx["q_b"], ctx["kv_b"], ctx["kp"], do, ctx["lse"], delta, bsz, seq, pre + "mla_attn_bwd")
    dq_b, dkv_b = mla_rope_cast(dq, dkv, cos, sin, seq, pre + "mla_rope_bwd", transpose=True)
    g_qb = mm_tn(ctx["qn"][None], dq_b[None], 1, pre + "mla_qb_dw")[0].reshape(rk, MLA_HEADS, 2 * LANES)
    g_qb = jnp.concatenate([g_qb[..., :MLA_NOPE], _gather_rope_cols(g_qb[..., MLA_NOPE:])], axis=-1)
    big_g[pre + "mla_w_qb"] = jnp.transpose(g_qb.reshape(rk, N_DEV, -1), (1, 0, 2))
    dqn = mm_nt(dq_b[None], ctx["w_qb_p"], pre + "mla_qb_dx")
    big_g[pre + "mla_w_kvb"] = _col_grad(ctx["kn"][None], dkv_b[None], full[pre + "mla_w_kvb"], pre + "mla_kvb_dw")
    dkn = mm_nt(dkv_b[None], full[pre + "mla_w_kvb"], pre + "mla_kvb_dx")
    dc, dqg, dkg = mla_low_bwd(ctx["c"], p[pre + "mla_q_norm"], p[pre + "mla_kv_norm"], dqn, dkn, dkp, cos, sin, seq,
                               pre + "mla_low_bwd")
    small_g[pre + "mla_q_norm"], small_g[pre + "mla_kv_norm"] = dqg.reshape(rk), dkg.reshape(rk)
    g_in = mm_tn(hn[None], dc, 1, pre + "mla_in_dw")[0]
    g_in = jnp.concatenate([g_in[:, :2 * rk], _gather_rope_cols(g_in[:, 2 * rk:])], axis=1)
    big_g[pre + "mla_w_in"] = g_in.reshape(N_DEV, d // N_DEV, -1)
    return mm_nt(dc, ctx["w_in_p"], pre + "mla_in_dx")


def _sg_fwd(pre, p, full, hn, bsz, seq, aux):
    z2 = mm_nn(hn[None], full[pre + "sg_w_in"], pre + "sg_in", bias=p[pre + "sg_b_in"], out_parts=2)
    b_s_t = jnp.transpose(p[pre + "sg_b_s"])
    y = sgu_fwd(z2, p[pre + "sg_ln_g"], p[pre + "sg_ln_b"], p[pre + "sg_w_s"], b_s_t, pre + "sg_mix")
    out = mm_nn(y[None], full[pre + "sg_w_out"], pre + "sg_out", bias=p[pre + "sg_b_out"])[0]
    return out, dict(z2=z2, y=y, b_s_t=b_s_t)


def _sg_bwd(pre, p, full, ctx, hn, dm, big_g, small_g, bsz, seq, aux, mid):
    dm, dm_b = dm
    t, d = dm.shape
    dm3 = dm_b[None]
    small_g[pre + "sg_b_out"] = col_sum(dm, pre + "sg_bout_g").reshape(d)
    dy = mm_nt(dm3, full[pre + "sg_w_out"], pre + "sg_out_dx")
    big_g[pre + "sg_w_out"] = mm_tn(ctx["y"][None], dm3, 1, pre + "sg_out_dw").reshape(N_DEV, -1, d)
    mid(dy)
    dz2, dbi, dlg, dlb, dws, dbs_t = sgu_bwd(ctx["z2"], dy, p[pre + "sg_ln_g"], p[pre + "sg_ln_b"], p[pre + "sg_w_s"],
                                             ctx["b_s_t"], pre + "sg_mix_bwd")
    big_g[pre + "sg_w_in"] = mm_tn(hn[None], dz2, N_DEV, pre + "sg_in_dw")
    small_g[pre + "sg_b_in"] = dbi.reshape(-1)
    small_g[pre + "sg_ln_g"], small_g[pre + "sg_ln_b"] = dlg.reshape(-1), dlb.reshape(-1)
    small_g[pre + "sg_w_s"], small_g[pre + "sg_b_s"] = dws, jnp.transpose(dbs_t)
    return mm_nt(dz2, full[pre + "sg_w_in"], pre + "sg_in_dx")


_MIX_FWD = dict(cc=_cc_fwd, dsa=_dsa_fwd, mla=_mla_fwd, sg=_sg_fwd)
_MIX_BWD = dict(cc=_cc_bwd, dsa=_dsa_bwd, mla=_mla_bwd, sg=_sg_bwd)


def _reduce_scatter(parts, core, chip, tag, adams=None):
    recv = rs_core_exchange(parts, tag + "_cores")
    sums = [rs_core_add(pt, rc, core, "%s_add%d" % (tag, i)) for i, (pt, rc) in enumerate(zip(parts, recv))]
    recv2 = rs_chip_exchange(sums, tag + "_chips")
    return [rs_finish(s, r2, chip, "%s_fin%d" % (tag, i), adam=None if adams is None else adams[i])
            for i, (s, r2) in enumerate(zip(sums, recv2))]


class _Schedule:
    def __init__(self):
        self.actions = {}

    def at(self, key, fn):
        self.actions.setdefault(key, []).append(fn)

    def run(self, key, x):
        after = list(x) if isinstance(x, (list, tuple)) else [x]
        for fn in self.actions.pop(key, []):
            fn(after)


def _aside(fn):
    saved = list(_PENDING)
    _PENDING.clear()
    out = fn()
    _PENDING[:0] = saved
    return out


def _gather_overlapped(sched, keys, shards, dev, tag, deliver):
    n = len(shards)
    state = {}

    def start(x):
        lands = [lax.dynamic_update_slice(lax.empty((N_DEV,) + s.shape, s.dtype), s[None], (dev,) + (0,) * s.ndim)
                 for s in shards]
        state["first"] = exchange_start(_ag_plan_first, 4 * n, shards, lands, x, tag +"_start")
        _PENDING.append(state["first"][4])

    def pass_on(x):
        _, lands = exchange_wait(_ag_plan_first, state["first"], x, tag +"_wait")
        state["pass"] = exchange_start(_ag_plan_pass, 3 * n, [], lands, [], tag + "_pass")
        _PENDING.append(state["pass"][4])

    def done(x):
        deliver(exchange_wait(_ag_plan_pass, state["pass"], x, tag +"_done")[1])

    for key, fn in zip(keys, (start, pass_on, done)):
        sched.at(key, fn)


def _reduce_overlapped(sched, keys, parts, adams, core, chip, tag, deliver):
    n = len(parts)
    state = {}

    def start(x):
        lands = [lax.empty((4,) + pt.shape[1:], pt.dtype) for pt in parts]
        state["cores"] = exchange_start(_rs_plan_cores, 4 * n, parts, lands, x, tag +"_start")
        _PENDING.append(state["cores"][4])

    def middle(x):
        mine, recv = exchange_wait(_rs_plan_cores, state["cores"], x, tag +"_wait")
        sums = _aside(lambda: [rs_core_add(pt, rc, core, "%s_add%d" % (tag, i))
                               for i, (pt, rc) in enumerate(zip(mine, recv))])
        lands = [lax.empty((3,) + s.shape[1:], s.dtype) for s in sums]
        state["chips"] = exchange_start(_rs_plan_chips, 3 * n, sums, lands, [], tag + "_send")
        _PENDING.append(state["chips"][4])

    def done(x):
        sums, recv = exchange_wait(_rs_plan_chips, state["chips"], x, tag +"_done")
        deliver(_aside(lambda: [rs_finish(s, r2, chip, "%s_fin%d" % (tag, i), adam=adams[i])
                                for i, (s, r2) in enumerate(zip(sums, recv))]))

    for key, fn in zip(keys, (start, middle, done)):
        sched.at(key, fn)


def _pack(arrays):
    flat = jnp.concatenate([a.reshape(-1) for a in arrays])
    unit = N_DEV * SUBLANES * LANES
    padded = -(-flat.shape[0] // unit) * unit
    return jnp.pad(flat, (0, padded - flat.shape[0]))


def _unpack(flat, like):
    out, pos = [], 0
    for a in like:
        out.append(flat[pos:pos + a.size].reshape(a.shape))
        pos += a.size
    return out


def _train_step(p):
    names = list(p)
    wnames = names[1:names.index("loss_target")]
    x = p["x"]
    bsz, seq, d = x.shape
    t = bsz * seq
    xi, yi, ci = lax.axis_index("x"), lax.axis_index("y"), lax.axis_index("c")
    core = jnp.reshape(ci, (1,)).astype(jnp.int32)
    chip = jnp.reshape(2 * xi + yi, (1,)).astype(jnp.int32)
    dev = 4 * xi + 2 * yi + ci
    n_layers = 1 + max(int(n[1]) for n in wnames if n[0] == "l" and n[1].isdigit())
    big = [n for n in wnames if _base(n) in _COL + _ROW]
    aux = dict(dsa_tables=rope_tables_full(seq, LANES), mla_tables=mla_rope_tables(seq))

    sched = _Schedule()
    full = {}

    def deliver_weights(grp):
        def deliver(lands):
            for n, g in zip(grp, lands):
                if _base(n) in _RELAID:
                    full[n] = _aside(lambda: relayout_cols(g, n + "_whole", False))
                elif _base(n) in _COL:
                    full[n] = g
                elif _base(n) in _ROW:
                    full[n] = g.reshape(1, N_DEV * g.shape[1], g.shape[2])
                else:
                    full[n] = jnp.transpose(g, (1, 0, 2)).reshape(g.shape[1], N_DEV * g.shape[2])
        return deliver

    groups = []
    for layer in range(n_layers):
        lw = [n for n in wnames if n.startswith("l%d_" % layer) and _base(n) in _COL + _ROW + _SMALL_COL]
        groups += [[n for n in lw if "_ffn_" not in n], [n for n in lw if "_ffn_" in n]]

    def shards_of(grp):
        return [p[n] if _base(n) in _SMALL_COL else p[n].astype(BF16) for n in grp]

    deliver_weights(groups[0])(all_gather(shards_of(groups[0]), "ag_first"))
    for s in range(1, len(groups)):
        first = max(s - 2, 0)
        keys = [("f", first), ("f", s - 1 if s > 2 else s), ("f", s)]
        _gather_overlapped(sched, keys, shards_of(groups[s]), dev, "ag_s%d" % s, deliver_weights(groups[s]))

    h = x.reshape(t, d)
    ffn_out = None
    ctxs = []
    for layer in range(n_layers):
        pre = "l%d_" % layer
        kind = _MIXERS[layer % len(_MIXERS)]
        if ffn_out is None:
            hn = rms_fwd(h, p[pre + "norm_mix"], pre + "mix_norm")
        else:
            h, hn = rms_fwd(h, p[pre + "norm_mix"], pre + "mix_norm", add=ffn_out)
        sched.run(("f", 2 * layer), [hn] + (list(full.values()) if layer == 0 else []))
        mix_out, mctx = _MIX_FWD[kind](pre, p, full, hn, bsz, seq, aux)
        sched.run(("f", 2 * layer + 1), mix_out)
        ffn_out, fctx = _ffn_fwd(pre, p, full, h, mix_out, bsz, seq)
        ctxs.append((h, hn, mctx, fctx))
        h = fctx["h"]

    loss_row, dh, dh_b, dfin = final_loss(h, ffn_out, p["final_norm"], p["loss_target"].reshape(t, d), "final_loss")
    dh = (dh, dh_b)
    loss = lax.psum(loss_row[0, 0], ("x", "y", "c"))

    small_g = {"final_norm": dfin.reshape(d)}
    results = {}

    def deliver_grads(lnames):
        def deliver(outs):
            for n, o in zip(lnames, outs):
                results[n] = o
        return deliver

    def reduce_later(q, lnames, big_g, tag):
        keys = [("b", q + 2), ("b", q + 3), ("b", q + 5)]
        _reduce_overlapped(sched, keys, [big_g[n] for n in lnames], [(p[n], p["m_" + n], p["v_" + n]) for n in lnames],
                           core, chip, tag, deliver_grads(lnames))

    q = 0
    for layer in reversed(range(n_layers)):
        pre = "l%d_" % layer
        kind = _MIXERS[layer % len(_MIXERS)]
        h_in, hn, mctx, fctx = ctxs[layer]
        big_g = {}
        sched.run(("b", q), dh[0])
        dh = _ffn_bwd(pre, p, full, fctx, dh, big_g, small_g, bsz, seq, lambda v, q=q: sched.run(("b", q + 1), v))
        reduce_later(q, [n for n in big if n.startswith(pre) and "_ffn_" in n], big_g, "rs_l%d_ffn" % layer)
        q += 2
        sched.run(("b", q), dh[0])
        dhn = _MIX_BWD[kind](pre, p, full, mctx, hn, dh, big_g, small_g, bsz, seq, aux,
                             lambda v, q=q: sched.run(("b", q + 1), v))
        dh_f, dh_b, dg = rms_bwd(h_in, p[pre + "norm_mix"], dhn, dh[0], pre + "mix_norm_bwd")
        dh = (dh_f, dh_b)
        small_g[pre + "norm_mix"] = dg.reshape(d)
        reduce_later(q, [n for n in big if n.startswith(pre) and "_ffn_" not in n], big_g, "rs_l%d_mix" % layer)
        q += 2
    for tail in range(q, q + 4):
        sched.run(("b", tail), dh[0])
    assert not sched.actions, sched.actions.keys()

    small = [n for n in wnames if n not in big]
    packed = _pack([small_g[n] for n in small]).reshape(N_DEV, -1, LANES)
    reduced = _reduce_scatter([packed], core, chip, "rs_small")[0]
    gathered = all_gather([reduced], "ag_small")[0].reshape(-1)
    g_local = []
    for n, g in zip(small, _unpack(gathered, [small_g[n] for n in small])):
        if _base(n) in _SMALL_COL:
            width = p[n].shape[1]
            g = lax.dynamic_slice_in_dim(g, dev * width, width, axis=1)
        g_local.append(g)
    stacked = [_pack(v).reshape(-1, LANES) for v in
               ([p[n] for n in small], g_local, [p["m_" + n] for n in small], [p["v_" + n] for n in small])]
    upd = adam_flat(*stacked, "adam_small")
    upd = [_unpack(u.reshape(-1), g_local) for u in upd]
    for i, n in enumerate(small):
        results[n] = (g_local[i], upd[0][i], upd[1][i], upd[2][i])

    grad_x = dh[0].reshape(bsz, seq, d)
    cols = [[results[n][k] for n in wnames] for k in range(4)]
    return (loss, grad_x, *cols[0], *cols[1], *cols[2], *cols[3])


def kernel(x, l0_norm_mix, l0_cc_w_in, l0_cc_b_in, l0_cc_dw_w, l0_cc_dw_b, l0_cc_ln_g, l0_cc_ln_b, l0_cc_w_out, l0_cc_b_out, l0_norm_ffn, l0_ffn_w_up, l0_ffn_dw_w, l0_ffn_dw_b, l0_ffn_w_down, l1_norm_mix, l1_dsa_w_qkv, l1_dsa_w_o, l1_norm_ffn, l1_ffn_w_up, l1_ffn_dw_w, l1_ffn_dw_b, l1_ffn_w_down, l2_norm_mix, l2_mla_w_in, l2_mla_q_norm, l2_mla_w_qb, l2_mla_kv_norm, l2_mla_w_kvb, l2_mla_w_o, l2_norm_ffn, l2_ffn_w_up, l2_ffn_dw_w, l2_ffn_dw_b, l2_ffn_w_down, l3_norm_mix, l3_sg_w_in, l3_sg_b_in, l3_sg_ln_g, l3_sg_ln_b, l3_sg_w_s, l3_sg_b_s, l3_sg_w_out, l3_sg_b_out, l3_norm_ffn, l3_ffn_w_up, l3_ffn_dw_w, l3_ffn_dw_b, l3_ffn_w_down, final_norm, loss_target, m_l0_norm_mix, m_l0_cc_w_in, m_l0_cc_b_in, m_l0_cc_dw_w, m_l0_cc_dw_b, m_l0_cc_ln_g, m_l0_cc_ln_b, m_l0_cc_w_out, m_l0_cc_b_out, m_l0_norm_ffn, m_l0_ffn_w_up, m_l0_ffn_dw_w, m_l0_ffn_dw_b, m_l0_ffn_w_down, m_l1_norm_mix, m_l1_dsa_w_qkv, m_l1_dsa_w_o, m_l1_norm_ffn, m_l1_ffn_w_up, m_l1_ffn_dw_w, m_l1_ffn_dw_b, m_l1_ffn_w_down, m_l2_norm_mix, m_l2_mla_w_in, m_l2_mla_q_norm, m_l2_mla_w_qb, m_l2_mla_kv_norm, m_l2_mla_w_kvb, m_l2_mla_w_o, m_l2_norm_ffn, m_l2_ffn_w_up, m_l2_ffn_dw_w, m_l2_ffn_dw_b, m_l2_ffn_w_down, m_l3_norm_mix, m_l3_sg_w_in, m_l3_sg_b_in, m_l3_sg_ln_g, m_l3_sg_ln_b, m_l3_sg_w_s, m_l3_sg_b_s, m_l3_sg_w_out, m_l3_sg_b_out, m_l3_norm_ffn, m_l3_ffn_w_up, m_l3_ffn_dw_w, m_l3_ffn_dw_b, m_l3_ffn_w_down, m_final_norm, v_l0_norm_mix, v_l0_cc_w_in, v_l0_cc_b_in, v_l0_cc_dw_w, v_l0_cc_dw_b, v_l0_cc_ln_g, v_l0_cc_ln_b, v_l0_cc_w_out, v_l0_cc_b_out, v_l0_norm_ffn, v_l0_ffn_w_up, v_l0_ffn_dw_w, v_l0_ffn_dw_b, v_l0_ffn_w_down, v_l1_norm_mix, v_l1_dsa_w_qkv, v_l1_dsa_w_o, v_l1_norm_ffn, v_l1_ffn_w_up, v_l1_ffn_dw_w, v_l1_ffn_dw_b, v_l1_ffn_w_down, v_l2_norm_mix, v_l2_mla_w_in, v_l2_mla_q_norm, v_l2_mla_w_qb, v_l2_mla_kv_norm, v_l2_mla_w_kvb, v_l2_mla_w_o, v_l2_norm_ffn, v_l2_ffn_w_up, v_l2_ffn_dw_w, v_l2_ffn_dw_b, v_l2_ffn_w_down, v_l3_norm_mix, v_l3_sg_w_in, v_l3_sg_b_in, v_l3_sg_ln_g, v_l3_sg_ln_b, v_l3_sg_w_s, v_l3_sg_b_s, v_l3_sg_w_out, v_l3_sg_b_out, v_l3_norm_ffn, v_l3_ffn_w_up, v_l3_ffn_dw_w, v_l3_ffn_dw_b, v_l3_ffn_w_down, v_final_norm):
    return _train_step(dict(locals()))
```

```python
import functools
import inspect
import math

import jax
import jax.numpy as jnp
from jax import lax
from jax.experimental import pallas as pl
from jax.experimental.pallas import tpu as pltpu

F32 = jnp.float32
BF16 = jnp.bfloat16
NORM_EPS = 1e-6
NEG_INF = -1e30
ROPE_THETA = 10000.0
LANES = 128
SUBLANES = 8
VMEM_LIMIT = 56 * 1024 * 1024
MM_VMEM_BUDGET = 44 * 1024 * 1024
N_DEV = 8

DSA_CONFIGS = ((128, 1), (512, 4), (2048, 16))
DSA_HEADS = 8
DSA_BLOCK = 128
MLA_HEADS = 16
MLA_RANK = 512
MLA_NOPE = 128
MLA_ROPE = 64
SG_CHUNK = 128
SG_GROUPS = 8
CC_WIDTH = 31
FFN_WIDTH = 3
ADAM_LR, ADAM_B1, ADAM_B2, ADAM_EPS, ADAM_WD, ADAM_STEP = 0.001, 0.9, 0.999, 1e-08, 0.01, 10


def _tile(n, cap, unit=LANES):
    if n <= cap:
        return n
    best = 0
    for t in range(unit, cap + 1, unit):
        if n % t == 0:
            best = t
    assert best, (n, cap, unit)
    return best


_PENDING = []


def _call(body, name, grid, in_specs, out_specs, out_shape, scratch=(), prefetch=0):
    params = pltpu.CompilerParams(vmem_limit_bytes=VMEM_LIMIT)
    deps = list(_PENDING)
    _PENDING.clear()
    if deps:
        inner, n_in = body, prefetch + len(in_specs)

        def body(*refs):
            return inner(*refs[:n_in], *refs[n_in + len(deps):])

        in_specs = list(in_specs) + [pl.BlockSpec(memory_space=pl.ANY)] * len(deps)
    if prefetch:
        spec = pltpu.PrefetchScalarGridSpec(num_scalar_prefetch=prefetch, grid=grid, in_specs=in_specs,
                                            out_specs=out_specs, scratch_shapes=list(scratch))
        fn = pl.pallas_call(body, out_shape=out_shape, grid_spec=spec, name=name, compiler_params=params)
    else:
        fn = pl.pallas_call(body, out_shape=out_shape, grid=grid, in_specs=in_specs, out_specs=out_specs,
                            scratch_shapes=list(scratch), name=name, compiler_params=params)
    return lambda *args: fn(*args, *deps)


def _sds(shape, dtype):
    return jax.ShapeDtypeStruct(tuple(shape), dtype)


def _bf(v):
    return v if v.dtype == BF16 else v.astype(BF16)


def _dot(a, b):
    return jnp.dot(a, b, preferred_element_type=F32)


def _dot_nt(a, b):
    return lax.dot_general(a, b, (((1,), (1,)), ((), ())), preferred_element_type=F32)


def _dot_tn(a, b):
    return lax.dot_general(a, b, (((0,), (0,)), ((), ())), preferred_element_type=F32)


def _sigmoid(v):
    return 1.0 / (1.0 + jnp.exp(-v))


_ERF_A = (-2.72614225801306e-10, 2.77068142495902e-08, -2.10102402082508e-06, -5.69250639462346e-05,
          -7.34990630326855e-04, -2.95459980854025e-03, -1.60960333262415e-02)
_ERF_B = (-1.45660718464996e-05, -2.13374055278905e-04, -1.68282697438203e-03, -7.37332916720468e-03,
          -1.42647390514189e-02)


def _erf(v):
    v = jnp.clip(v, -4.0, 4.0)
    v2 = v * v
    p = jnp.full_like(v, _ERF_A[0])
    for c in _ERF_A[1:]:
        p = p * v2 + c
    q = jnp.full_like(v, _ERF_B[0])
    for c in _ERF_B[1:]:
        q = q * v2 + c
    return v * p / q


def _gelu_cdf(v):
    return 0.5 * (1.0 + _erf(v * (2.0 ** -0.5)))


def _gelu_grad(v, cdf):
    return cdf + v * jnp.exp(-0.5 * v * v) * ((2.0 * math.pi) ** -0.5)


def _edge_rows(width):
    return -(-(width - 1) // SUBLANES) * SUBLANES


def _conv_causal(z, w_row, width):
    hr = _edge_rows(width)
    rows = lax.broadcasted_iota(jnp.int32, (hr, z.shape[1]), 0)
    out = z * w_row(width - 1)
    head = z[:hr] * w_row(width - 1)
    for j in range(1, width):
        tap = pltpu.roll(z, j, 0)
        out = out + tap * w_row(width - 1 - j)
        head = head + jnp.where(rows >= j, tap[:hr], 0.0) * w_row(width - 1 - j)
    return jnp.concatenate([head, out[hr:]], axis=0)


def _conv_causal_bwd(x, dc, w_row, width):
    n = dc.shape[0]
    hr = _edge_rows(width)
    rows = lax.broadcasted_iota(jnp.int32, (hr, dc.shape[1]), 0)
    x_tail, dc_tail = x[n - hr:], dc[n - hr:]
    dx = dc * w_row(width - 1)
    tail = dc_tail * w_row(width - 1)
    dws = {width - 1: jnp.sum(dc * x, axis=0, keepdims=True)}
    for j in range(1, width):
        up = pltpu.roll(dc, n - j, 0)
        up_tail = jnp.where(rows < hr - j, up[n - hr:], 0.0)
        dx = dx + up * w_row(width - 1 - j)
        tail = tail + up_tail * w_row(width - 1 - j)
        wrapped = jnp.sum((up[n - hr:] - up_tail) * x_tail, axis=0, keepdims=True)
        dws[width - 1 - j] = jnp.sum(up * x, axis=0, keepdims=True) - wrapped
    return jnp.concatenate([dx[:n - hr], tail], axis=0), dws, jnp.sum(dc, axis=0, keepdims=True)


def _rope(v, cos, sin):
    return v * cos + pltpu.roll(v, 64, 1) * sin


def _rope_t(dv, cos, sin):
    return dv * cos + pltpu.roll(dv * sin, 64, 1)


def mm_nn(a, w, name, bias=None, out_parts=1, out_dtype=F32, tm_cap=1024, tn_cap=1536, tk_cap=6144):
    pa, m, kp = a.shape
    j, k, n = w.shape
    assert pa * kp == k
    big_n = j * n
    npo = big_n // out_parts
    tm = _tile(m, tm_cap, SUBLANES)
    tk = _tile(kp, tk_cap)
    tn = _tile(math.gcd(n, npo), tn_cap if tk <= 2816 else 1024)
    out_bytes = jnp.dtype(out_dtype).itemsize

    def vmem_bytes(rows):
        return 2 * (rows * tk * 2 + tk * tn * 2 + rows * tn * out_bytes) + (rows * tn * 4 if k > tk else 0)

    while vmem_bytes(tm) > MM_VMEM_BUDGET and tm % (2 * SUBLANES) == 0:
        tm //= 2
    npj, nbo, kbp, nk = n // tn, npo // tn, kp // tk, k // tk
    has_bias = bias is not None

    def body(*refs):
        a_ref, w_ref = refs[0], refs[1]
        b_ref = refs[2] if has_bias else None
        o_ref = refs[2 + has_bias]

        def finish(acc):
            if has_bias:
                acc = acc + b_ref[...]
            o_ref[...] = acc.astype(o_ref.dtype)

        part = _dot(_bf(a_ref[...]), _bf(w_ref[...]))
        if nk == 1:
            finish(part)
        else:
            acc_ref = refs[3 + has_bias]
            ki = pl.program_id(2)

            @pl.when(ki == 0)
            def _():
                acc_ref[...] = part

            @pl.when(ki > 0)
            def _():
                acc_ref[...] += part

            @pl.when(ki == nk - 1)
            def _():
                finish(acc_ref[...])

    in_specs = [pl.BlockSpec((None, tm, tk), lambda ni, mi, ki: (ki // kbp, mi, ki % kbp)),
                pl.BlockSpec((None, tk, tn), lambda ni, mi, ki: (ni // npj, ki, ni % npj))]
    args = [a, w]
    if has_bias:
        in_specs.append(pl.BlockSpec((1, tn), lambda ni, mi, ki: (0, ni)))
        args.append(bias.reshape(1, big_n))
    out_spec = pl.BlockSpec((None, tm, tn), lambda ni, mi, ki: (ni // nbo, mi, ni % nbo))
    scratch = [pltpu.VMEM((tm, tn), F32)] if nk > 1 else []
    return _call(body, name, (big_n // tn, m // tm, nk), in_specs, out_spec,
                 _sds((out_parts, m, npo), out_dtype), scratch)(*args)


def mm_nt(a, w, name, out_dtype=F32, tm_cap=1024, tko_cap=2048, tc_cap=2048):
    pa, m, npa = a.shape
    j, k, n = w.shape
    assert pa * npa == j * n
    tm = _tile(m, tm_cap, SUBLANES)
    tko = _tile(k, tko_cap)
    tc = _tile(math.gcd(n, npa), tc_cap)
    out_bytes = jnp.dtype(out_dtype).itemsize

    def vmem_bytes(rows):
        return 2 * (rows * tc * 2 + tko * tc * 2 + rows * tko * out_bytes) + (rows * tko * 4 if j * n > tc else 0)

    while vmem_bytes(tm) > MM_VMEM_BUDGET and tm % (2 * SUBLANES) == 0:
        tm //= 2
    npj, nba, nc = n // tc, npa // tc, (j * n) // tc

    def body(a_ref, w_ref, o_ref, *scr):
        part = _dot_nt(_bf(a_ref[...]), _bf(w_ref[...]))
        if nc == 1:
            o_ref[...] = part.astype(o_ref.dtype)
        else:
            acc_ref = scr[0]
            ci = pl.program_id(2)

            @pl.when(ci == 0)
            def _():
                acc_ref[...] = part

            @pl.when(ci > 0)
            def _():
                acc_ref[...] += part

            @pl.when(ci == nc - 1)
            def _():
                o_ref[...] = acc_ref[...].astype(o_ref.dtype)

    in_specs = [pl.BlockSpec((None, tm, tc), lambda mi, ko, ci: (ci // nba, mi, ci % nba)),
                pl.BlockSpec((None, tko, tc), lambda mi, ko, ci: (ci // npj, ko, ci % npj))]
    out_spec = pl.BlockSpec((tm, tko), lambda mi, ko, ci: (mi, ko))
    scratch = [pltpu.VMEM((tm, tko), F32)] if nc > 1 else []
    return _call(body, name, (m // tm, k // tko, nc), in_specs, out_spec, _sds((m, k), out_dtype), scratch)(a, w)


def mm_tn(a, b, n_shards, name, out_dtype=BF16, tt_cap=4096, tkk_cap=512, tn_cap=1536):
    pa, m, kp = a.shape
    pb, m2, npb = b.shape
    assert m == m2
    k, big_n = pa * kp, pb * npb
    n = big_n // n_shards
    tt = _tile(m, tt_cap, SUBLANES)
    tkk = _tile(kp, tkk_cap)
    tn = _tile(math.gcd(n, npb), tn_cap)
    kbp, nbb, npj, nt = kp // tkk, npb // tn, n // tn, m // tt

    def body(a_ref, b_ref, o_ref, *scr):
        part = _dot_tn(_bf(a_ref[...]), _bf(b_ref[...]))
        if nt == 1:
            o_ref[...] = part.astype(o_ref.dtype)
        else:
            acc_ref = scr[0]
            ti = pl.program_id(2)

            @pl.when(ti == 0)
            def _():
                acc_ref[...] = part

            @pl.when(ti > 0)
            def _():
                acc_ref[...] += part

            @pl.when(ti == nt - 1)
            def _():
                o_ref[...] = acc_ref[...].astype(o_ref.dtype)

    in_specs = [pl.BlockSpec((None, tt, tkk), lambda ni, ki, ti: (ki // kbp, ti, ki % kbp)),
                pl.BlockSpec((None, tt, tn), lambda ni, ki, ti: (ni // nbb, ti, ni % nbb))]
    out_spec = pl.BlockSpec((None, tkk, tn), lambda ni, ki, ti: (ni // npj, ki, ni % npj))
    scratch = [pltpu.VMEM((tkk, tn), F32)] if nt > 1 else []
    return _call(body, name, (big_n // tn, k // tkk, nt), in_specs, out_spec,
                 _sds((n_shards, k, n), out_dtype), scratch)(a, b)


_ANY = pl.BlockSpec(memory_space=pl.ANY)
_MESH = pl.DeviceIdType.MESH


def all_gather(shards, name):
    n = len(shards)

    def body(*refs):
        ins, outs = refs[:n], refs[n:2 * n]
        send_sems, recv_sems, local_sems = refs[2 * n:]
        x, y, c = lax.axis_index("x"), lax.axis_index("y"), lax.axis_index("c")
        me, sibling = (x, y, c), (x, y, 1 - c)
        chips = [(1 - x, y), (x, 1 - y), (1 - x, 1 - y)]

        def copy(i, k, block, to, src=None):
            px, py, pc = block
            dst = outs[i].at[4 * px + 2 * py + pc]
            return pltpu.make_async_remote_copy(src_ref=dst if src is None else src, dst_ref=dst,
                                                send_sem=send_sems.at[i, k], recv_sem=recv_sems.at[i, k],
                                                device_id=to, device_id_type=_MESH)

        mine = [pltpu.make_async_copy(ins[i], outs[i].at[4 * x + 2 * y + c], local_sems.at[i]) for i in range(n)]
        for cp in mine:
            cp.start()
        first = []
        for i in range(n):
            first.append(copy(i, 0, me, sibling, src=ins[i]))
            first += [copy(i, 1 + q, me, (*chip, c), src=ins[i]) for q, chip in enumerate(chips)]
        for cp in first:
            cp.start()
        passed = []
        for q, chip in enumerate(chips):
            for i in range(n):
                copy(i, 1 + q, (*chip, c), me).wait_recv()
                cp = copy(i, 4 + q, (*chip, c), sibling)
                cp.start()
                passed.append(cp)
        for i in range(n):
            copy(i, 0, sibling, me).wait_recv()
            for q, chip in enumerate(chips):
                copy(i, 4 + q, (*chip, 1 - c), me).wait_recv()
        for cp in first + passed:
            cp.wait_send()
        for cp in mine:
            cp.wait()

    outs = pl.pallas_call(
        body, name=name, out_shape=[_sds((N_DEV,) + s.shape, s.dtype) for s in shards],
        in_specs=[_ANY] * n, out_specs=[_ANY] * n,
        scratch_shapes=[pltpu.SemaphoreType.DMA((n, 7)), pltpu.SemaphoreType.DMA((n, 7)),
                        pltpu.SemaphoreType.DMA((n,))],
    )(*shards)
    return list(outs)


def rs_core_exchange(parts, name):
    n = len(parts)

    def body(*refs):
        ins, outs = refs[:n], refs[n:2 * n]
        send_sems, recv_sems = refs[2 * n:]
        x, y, c = lax.axis_index("x"), lax.axis_index("y"), lax.axis_index("c")
        sibling = (x, y, 1 - c)
        started = []
        for i in range(n):
            for chip in range(4):
                cp = pltpu.make_async_remote_copy(src_ref=ins[i].at[2 * chip + (1 - c)], dst_ref=outs[i].at[chip],
                                                  send_sem=send_sems.at[i, chip], recv_sem=recv_sems.at[i, chip],
                                                  device_id=sibling, device_id_type=_MESH)
                cp.start()
                started.append(cp)
        for cp in started:
            cp.wait()

    outs = pl.pallas_call(
        body, name=name, out_shape=[_sds((4,) + p.shape[1:], p.dtype) for p in parts],
        in_specs=[_ANY] * n, out_specs=[_ANY] * n,
        scratch_shapes=[pltpu.SemaphoreType.DMA((n, 4)), pltpu.SemaphoreType.DMA((n, 4))],
    )(*parts)
    return list(outs)


def rs_chip_exchange(sums, name):
    n = len(sums)

    def body(*refs):
        ins, outs = refs[:n], refs[n:2 * n]
        send_sems, recv_sems = refs[2 * n:]
        x, y, c = lax.axis_index("x"), lax.axis_index("y"), lax.axis_index("c")
        peers = [(x, 1 - y), (1 - x, y), (1 - x, 1 - y)]
        started = []
        for i in range(n):
            for k, (px, py) in enumerate(peers):
                cp = pltpu.make_async_remote_copy(src_ref=ins[i].at[2 * px + py], dst_ref=outs[i].at[k],
                                                  send_sem=send_sems.at[i, k], recv_sem=recv_sems.at[i, k],
                                                  device_id=(px, py, c), device_id_type=_MESH)
                cp.start()
                started.append(cp)
        for cp in started:
            cp.wait()

    outs = pl.pallas_call(
        body, name=name, out_shape=[_sds((3,) + s.shape[1:], s.dtype) for s in sums],
        in_specs=[_ANY] * n, out_specs=[_ANY] * n,
        scratch_shapes=[pltpu.SemaphoreType.DMA((n, 3)), pltpu.SemaphoreType.DMA((n, 3))],
    )(*sums)
    return list(outs)


_HBM = pl.BlockSpec(memory_space=pltpu.HBM)
_SEM = pl.BlockSpec(memory_space=pltpu.SEMAPHORE)
_EFFECT = pltpu.SideEffectType.DATAFLOW_SIDE_EFFECTING


def _in_hbm(a):
    return pltpu.with_memory_space_constraint(a, pltpu.HBM)


def _plan_copies(plan, srcs, dsts, send_sems, recv_sems):
    x, y, c = lax.axis_index("x"), lax.axis_index("y"), lax.axis_index("c")
    return [pltpu.make_async_remote_copy(src_ref=s, dst_ref=d, send_sem=send_sems.at[q], recv_sem=recv_sems.at[q],
                                         device_id=to, device_id_type=_MESH)
            for q, (s, d, to) in enumerate(plan(x, y, c, srcs, dsts))]


def exchange_start(plan, n_copies, srcs, dsts, after, name):
    ns, nd, na = len(srcs), len(dsts), len(after)

    def body(*refs):
        ins = refs[:ns + nd + na]
        send_sems, recv_sems = refs[ns + nd + na], refs[ns + nd + na + 1]
        token = refs[-1]
        for cp in _plan_copies(plan, ins[:ns], ins[ns:ns + nd], send_sems, recv_sems):
            cp.start()
        token[...] = jnp.zeros_like(token)

    thru = [pltpu.HBM(a.shape, a.dtype) for a in list(srcs) + list(dsts)]
    out = pl.pallas_call(
        body, name=name,
        out_shape=[pltpu.SemaphoreType.DMA((n_copies,)), pltpu.SemaphoreType.DMA((n_copies,))] + thru
        + [_sds((SUBLANES, LANES), F32)],
        in_specs=[_HBM] * (ns + nd) + [_ANY] * na,
        out_specs=[_SEM, _SEM] + [_HBM] * (ns + nd) + [pl.BlockSpec(memory_space=pltpu.VMEM)],
        input_output_aliases={i: 2 + i for i in range(ns + nd)},
        compiler_params=pltpu.CompilerParams(has_side_effects=_EFFECT),
    )(*[_in_hbm(a) for a in list(srcs) + list(dsts)], *after)
    return out[0], out[1], list(out[2:2 + ns]), list(out[2 + ns:2 + ns + nd]), out[-1]


def exchange_wait(plan, started, after, name):
    send_sems, recv_sems, srcs, dsts, _ = started
    ns, nd, na = len(srcs), len(dsts), len(after)

    def body(*refs):
        ins = refs[:ns + nd]
        send_sems_ref, recv_sems_ref = refs[ns + nd], refs[ns + nd + 1]
        for cp in _plan_copies(plan, ins[:ns], ins[ns:ns + nd], send_sems_ref, recv_sems_ref):
            cp.wait_send()
            cp.wait_recv()

    thru = [pltpu.HBM(a.shape, a.dtype) for a in list(srcs) + list(dsts)]
    out = pl.pallas_call(
        body, name=name, out_shape=thru,
        in_specs=[_HBM] * (ns + nd) + [_SEM, _SEM] + [_ANY] * na,
        out_specs=[_HBM] * (ns + nd),
        input_output_aliases={i: i for i in range(ns + nd)},
        compiler_params=pltpu.CompilerParams(has_side_effects=_EFFECT),
    )(*srcs, *dsts, send_sems, recv_sems, *after)
    return list(out[:ns]), list(out[ns:])


def _dev_index(px, py, pc):
    return 4 * px + 2 * py + pc


def _ag_plan_first(x, y, c, srcs, dsts):
    me = _dev_index(x, y, c)
    peers = [(x, y, 1 - c), (1 - x, y, c), (x, 1 - y, c), (1 - x, 1 - y, c)]
    return [(s, d.at[me], to) for s, d in zip(srcs, dsts) for to in peers]


def _ag_plan_pass(x, y, c, srcs, dsts):
    chips = [(1 - x, y), (x, 1 - y), (1 - x, 1 - y)]
    return [(d.at[_dev_index(px, py, c)], d.at[_dev_index(px, py, c)], (x, y, 1 - c)) for d in dsts for px, py in chips]


def _rs_plan_cores(x, y, c, srcs, dsts):
    return [(s.at[2 * chip + (1 - c)], d.at[chip], (x, y, 1 - c)) for s, d in zip(srcs, dsts) for chip in range(4)]


def _rs_plan_chips(x, y, c, srcs, dsts):
    peers = [(x, 1 - y), (1 - x, y), (1 - x, 1 - y)]
    return [(s.at[2 * px + py], d.at[k], (px, py, c)) for s, d in zip(srcs, dsts) for k, (px, py) in enumerate(peers)]


def _row_tile(r, c, itemsize=4, budget=1 << 20):
    cap = max(SUBLANES, (budget // (c * itemsize)) // SUBLANES * SUBLANES)
    if r <= cap:
        return r
    best = 0
    for t in range(SUBLANES, cap + 1, SUBLANES):
        if r % t == 0:
            best = t
    return best if best else r


def rs_core_add(part, recv, core, name):
    _, r, c = part.shape
    tr = _row_tile(r, c, budget=1 << 22)

    def body(core_ref, p_ref, q_ref, o_ref):
        o_ref[...] = (p_ref[...].astype(F32) + q_ref[...].astype(F32)).astype(o_ref.dtype)

    in_specs = [pl.BlockSpec((None, tr, c), lambda ch, ri, core_ref: (2 * ch + core_ref[0], ri, 0)),
                pl.BlockSpec((None, tr, c), lambda ch, ri, core_ref: (ch, ri, 0))]
    out_spec = pl.BlockSpec((None, tr, c), lambda ch, ri, core_ref: (ch, ri, 0))
    return _call(body, name, (4, r // tr), in_specs, out_spec, _sds((4, r, c), part.dtype), prefetch=1)(core, part, recv)


def _adamw(w, g, m, v):
    m = ADAM_B1 * m + (1.0 - ADAM_B1) * g
    v = ADAM_B2 * v + (1.0 - ADAM_B2) * (g * g)
    m_hat = m / (1.0 - ADAM_B1 ** ADAM_STEP)
    v_hat = v / (1.0 - ADAM_B2 ** ADAM_STEP)
    delta = -ADAM_LR * (m_hat / (jnp.sqrt(v_hat) + ADAM_EPS) + ADAM_WD * w)
    return delta, m, v


def rs_finish(sums, recv, chip, name, adam=None):
    _, r, c = sums.shape
    tr = _row_tile(r, c, budget=1 << 20)

    def body(chip_ref, s_ref, q_ref, *refs):
        g = s_ref[...].astype(F32)
        for k in range(3):
            g = g + q_ref[k].astype(F32)
        if adam is None:
            refs[0][...] = g
        else:
            w_ref, m_ref, v_ref, g_ref, d_ref, nm_ref, nv_ref = refs
            g_ref[...] = g
            d_ref[...], nm_ref[...], nv_ref[...] = _adamw(w_ref[...], g, m_ref[...], v_ref[...])

    blk = pl.BlockSpec((tr, c), lambda ri, chip_ref: (ri, 0))
    in_specs = [pl.BlockSpec((None, tr, c), lambda ri, chip_ref: (chip_ref[0], ri, 0)),
                pl.BlockSpec((3, tr, c), lambda ri, chip_ref: (0, ri, 0))]
    args = [chip, sums, recv]
    if adam is None:
        out_specs, out_shape = blk, _sds((r, c), F32)
    else:
        in_specs += [blk] * 3
        args += list(adam)
        out_specs, out_shape = [blk] * 4, [_sds((r, c), F32)] * 4
    return _call(body, name, (r // tr,), in_specs, out_specs, out_shape, prefetch=1)(*args)


def adam_flat(w, g, m, v, name):
    r, c = w.shape
    tr = _row_tile(r, c, budget=1 << 19)

    def body(w_ref, g_ref, m_ref, v_ref, d_ref, nm_ref, nv_ref):
        d_ref[...], nm_ref[...], nv_ref[...] = _adamw(w_ref[...], g_ref[...], m_ref[...], v_ref[...])

    blk = pl.BlockSpec((tr, c), lambda ri: (ri, 0))
    return _call(body, name, (r // tr,), [blk] * 4, [blk] * 3, [_sds((r, c), F32)] * 3)(w, g, m, v)


def rms_fwd(x, g, name, add=None):
    t, d = x.shape
    tm = _tile(t, 256, SUBLANES)
    has_add = add is not None

    def body(*refs):
        if has_add:
            x_ref, a_ref, g_ref, h_ref, hn_ref = refs
            h = x_ref[...] + a_ref[...]
            h_ref[...] = h
        else:
            x_ref, g_ref, hn_ref = refs
            h = x_ref[...]
        r = lax.rsqrt(jnp.mean(h * h, axis=-1, keepdims=True) + NORM_EPS)
        hn_ref[...] = (h * r * g_ref[...]).astype(BF16)

    row = pl.BlockSpec((tm, d), lambda i: (i, 0))
    vec = pl.BlockSpec((1, d), lambda i: (0, 0))
    if has_add:
        return _call(body, name, (t // tm,), [row, row, vec], [row, row], [_sds((t, d), F32), _sds((t, d), BF16)])(
            x, add, g.reshape(1, d))
    return _call(body, name, (t // tm,), [row, vec], row, _sds((t, d), BF16))(x, g.reshape(1, d))


def _rms_bwd_math(x, g, dy):
    r = lax.rsqrt(jnp.mean(x * x, axis=-1, keepdims=True) + NORM_EPS)
    xh = x * r
    dyg = dy * g
    dx = r * (dyg - xh * jnp.mean(dyg * xh, axis=-1, keepdims=True))
    return dx, jnp.sum(dy * xh, axis=0, keepdims=True)


def rms_bwd(x, g, dhn, dres, name):
    t, d = x.shape
    tm = _tile(t, 256, SUBLANES)

    def body(x_ref, g_ref, dy_ref, dr_ref, dx_ref, dxb_ref, dg_ref):
        dx, dg = _rms_bwd_math(x_ref[...], g_ref[...], dy_ref[...])
        dx = dx + dr_ref[...]
        dx_ref[...] = dx
        dxb_ref[...] = dx.astype(BF16)

        @pl.when(pl.program_id(0) == 0)
        def _():
            dg_ref[...] = jnp.zeros_like(dg_ref)

        dg_ref[...] += dg

    row = pl.BlockSpec((tm, d), lambda i: (i, 0))
    vec = pl.BlockSpec((1, d), lambda i: (0, 0))
    return _call(body, name, (t // tm,), [row, vec, row, row], [row, row, vec],
                 [_sds((t, d), F32), _sds((t, d), BF16), _sds((1, d), F32)])(x, g.reshape(1, d), dhn, dres)


def final_loss(h, add, g, target, name):
    t, d = h.shape
    tm = _tile(t, 256, SUBLANES)

    def body(x_ref, a_ref, g_ref, t_ref, loss_ref, dx_ref, dxb_ref, dg_ref):
        x = x_ref[...] + a_ref[...]
        gain = g_ref[...]
        r = lax.rsqrt(jnp.mean(x * x, axis=-1, keepdims=True) + NORM_EPS)
        err = x * r * gain - t_ref[...]
        dx, dg = _rms_bwd_math(x, gain, err * (1.0 / d))
        dx_ref[...] = dx
        dxb_ref[...] = dx.astype(BF16)

        @pl.when(pl.program_id(0) == 0)
        def _():
            dg_ref[...] = jnp.zeros_like(dg_ref)
            loss_ref[...] = jnp.zeros_like(loss_ref)

        dg_ref[...] += dg
        part = jnp.sum(jnp.sum(err * err, axis=-1, keepdims=True), axis=0, keepdims=True) * (0.5 / d)
        loss_ref[...] += jnp.broadcast_to(part, loss_ref.shape)

    row = pl.BlockSpec((tm, d), lambda i: (i, 0))
    vec = pl.BlockSpec((1, d), lambda i: (0, 0))
    one = pl.BlockSpec((1, LANES), lambda i: (0, 0))
    return _call(body, name, (t // tm,), [row, row, vec, row], [one, row, row, vec],
                 [_sds((1, LANES), F32), _sds((t, d), F32), _sds((t, d), BF16), _sds((1, d), F32)])(
        h, add, g.reshape(1, d), target)


def ffn_gate_fwd(z2, dw_w2, dw_b2, name, tc=LANES):
    _, b, s, f = z2.shape

    def body(z_ref, w_ref, b_ref, y_ref):
        g = _conv_causal(z_ref[0], lambda k: w_ref[0, k:k + 1, :], FFN_WIDTH) + b_ref[0]
        a = _conv_causal(z_ref[1], lambda k: w_ref[1, k:k + 1, :], FFN_WIDTH) + b_ref[1]
        y_ref[...] = (g * _sigmoid(g) * a).astype(BF16)

    in_specs = [pl.BlockSpec((2, None, s, tc), lambda bi, ci: (0, bi, 0, ci)),
                pl.BlockSpec((2, FFN_WIDTH, tc), lambda bi, ci: (0, 0, ci)),
                pl.BlockSpec((2, 1, tc), lambda bi, ci: (0, 0, ci))]
    out_spec = pl.BlockSpec((None, s, tc), lambda bi, ci: (bi, 0, ci))
    return _call(body, name, (b, f // tc), in_specs, out_spec, _sds((b, s, f), BF16))(z2, dw_w2, dw_b2)


def ffn_gate_bwd(z2, dy, dw_w2, dw_b2, name, tc=LANES):
    _, b, s, f = z2.shape

    def body(z_ref, dy_ref, w_ref, b_ref, dz_ref, dw_ref, db_ref):
        @pl.when(pl.program_id(1) == 0)
        def _():
            dw_ref[...] = jnp.zeros_like(dw_ref)
            db_ref[...] = jnp.zeros_like(db_ref)

        zs = (z_ref[0], z_ref[1])
        g = _conv_causal(zs[0], lambda k: w_ref[0, k:k + 1, :], FFN_WIDTH) + b_ref[0]
        a = _conv_causal(zs[1], lambda k: w_ref[1, k:k + 1, :], FFN_WIDTH) + b_ref[1]
        sg = _sigmoid(g)
        dy = dy_ref[...]
        dcs = (dy * a * (sg * (1.0 + g * (1.0 - sg))), dy * (g * sg))
        for p in range(2):
            dz, dws, db = _conv_causal_bwd(zs[p], dcs[p], lambda k, p=p: w_ref[p, k:k + 1, :], FFN_WIDTH)
            dz_ref[p] = dz.astype(BF16)
            for k in range(FFN_WIDTH):
                dw_ref[p, k:k + 1, :] += dws[k]
            db_ref[p] += db

    in_specs = [pl.BlockSpec((2, None, s, tc), lambda ci, bi: (0, bi, 0, ci)),
                pl.BlockSpec((None, s, tc), lambda ci, bi: (bi, 0, ci)),
                pl.BlockSpec((2, FFN_WIDTH, tc), lambda ci, bi: (0, 0, ci)),
                pl.BlockSpec((2, 1, tc), lambda ci, bi: (0, 0, ci))]
    out_specs = [pl.BlockSpec((2, None, s, tc), lambda ci, bi: (0, bi, 0, ci)),
                 pl.BlockSpec((2, FFN_WIDTH, tc), lambda ci, bi: (0, 0, ci)),
                 pl.BlockSpec((2, 1, tc), lambda ci, bi: (0, 0, ci))]
    out_shape = [_sds((2, b, s, f), BF16), _sds((2, FFN_WIDTH, f), F32), _sds((2, 1, f), F32)]
    return _call(body, name, (f // tc, b), in_specs, out_specs, out_shape)(z2, dy, dw_w2, dw_b2)


def cc_conv_fwd(z2, dw_w, dw_b, name, tc=LANES):
    _, b, s, c = z2.shape

    def body(z_ref, w_ref, b_ref, o_ref):
        u = z_ref[0] * _sigmoid(z_ref[1])
        o_ref[...] = _conv_causal(u, lambda k: w_ref[k:k + 1, :], CC_WIDTH) + b_ref[...]

    in_specs = [pl.BlockSpec((2, None, s, tc), lambda bi, ci: (0, bi, 0, ci)),
                pl.BlockSpec((CC_WIDTH, tc), lambda bi, ci: (0, ci)),
                pl.BlockSpec((1, tc), lambda bi, ci: (0, ci))]
    out_spec = pl.BlockSpec((None, s, tc), lambda bi, ci: (bi, 0, ci))
    return _call(body, name, (b, c // tc), in_specs, out_spec, _sds((b, s, c), F32))(z2, dw_w, dw_b.reshape(1, c))


def cc_conv_bwd(z2, dcv, dw_w, name, tc=LANES):
    _, b, s, c = z2.shape

    def body(z_ref, dc_ref, w_ref, dz_ref, dbi_ref, dw_ref, db_ref):
        @pl.when(pl.program_id(1) == 0)
        def _():
            dbi_ref[...] = jnp.zeros_like(dbi_ref)
            dw_ref[...] = jnp.zeros_like(dw_ref)
            db_ref[...] = jnp.zeros_like(db_ref)

        a, gate = z_ref[0], z_ref[1]
        sg = _sigmoid(gate)
        u = a * sg
        du, dws, db = _conv_causal_bwd(u, dc_ref[...], lambda k: w_ref[k:k + 1, :], CC_WIDTH)
        for k in range(CC_WIDTH):
            dw_ref[k:k + 1, :] += dws[k]
        db_ref[...] += db
        da = du * sg
        dg = du * a * sg * (1.0 - sg)
        dz_ref[0] = da.astype(BF16)
        dz_ref[1] = dg.astype(BF16)
        dbi_ref[0] += jnp.sum(da, axis=0, keepdims=True)
        dbi_ref[1] += jnp.sum(dg, axis=0, keepdims=True)

    in_specs = [pl.BlockSpec((2, None, s, tc), lambda ci, bi: (0, bi, 0, ci)),
                pl.BlockSpec((None, s, tc), lambda ci, bi: (bi, 0, ci)),
                pl.BlockSpec((CC_WIDTH, tc), lambda ci, bi: (0, ci))]
    out_specs = [pl.BlockSpec((2, None, s, tc), lambda ci, bi: (0, bi, 0, ci)),
                 pl.BlockSpec((2, 1, tc), lambda ci, bi: (0, 0, ci)),
                 pl.BlockSpec((CC_WIDTH, tc), lambda ci, bi: (0, ci)),
                 pl.BlockSpec((1, tc), lambda ci, bi: (0, ci))]
    out_shape = [_sds((2, b, s, c), BF16), _sds((2, 1, c), F32), _sds((CC_WIDTH, c), F32), _sds((1, c), F32)]
    return _call(body, name, (c // tc, b), in_specs, out_specs, out_shape)(z2, dcv, dw_w)


def _ln_stats(v):
    mu = jnp.mean(v, axis=-1, keepdims=True)
    vc = v - mu
    r = lax.rsqrt(jnp.mean(vc * vc, axis=-1, keepdims=True) + NORM_EPS)
    return vc * r, r


def _ln_bwd(dln, xh, r, g):
    dxh = dln * g
    dx = r * (dxh - jnp.mean(dxh, axis=-1, keepdims=True) - xh * jnp.mean(dxh * xh, axis=-1, keepdims=True))
    return dx, jnp.sum(dln * xh, axis=0, keepdims=True), jnp.sum(dln, axis=0, keepdims=True)


def cc_ln_fwd(cv, ln_g, ln_b, name):
    t, c = cv.shape
    tm = _tile(t, 256, SUBLANES)

    def body(x_ref, g_ref, b_ref, o_ref):
        xh, _ = _ln_stats(x_ref[...])
        ln = xh * g_ref[...] + b_ref[...]
        o_ref[...] = (ln * _sigmoid(ln)).astype(BF16)

    row = pl.BlockSpec((tm, c), lambda i: (i, 0))
    vec = pl.BlockSpec((1, c), lambda i: (0, 0))
    return _call(body, name, (t // tm,), [row, vec, vec], row, _sds((t, c), BF16))(
        cv, ln_g.reshape(1, c), ln_b.reshape(1, c))


def cc_ln_bwd(cv, ln_g, ln_b, ds, name):
    t, c = cv.shape
    tm = _tile(t, 256, SUBLANES)

    def body(x_ref, g_ref, b_ref, ds_ref, dx_ref, dg_ref, db_ref):
        @pl.when(pl.program_id(0) == 0)
        def _():
            dg_ref[...] = jnp.zeros_like(dg_ref)
            db_ref[...] = jnp.zeros_like(db_ref)

        xh, r = _ln_stats(x_ref[...])
        ln = xh * g_ref[...] + b_ref[...]
        sg = _sigmoid(ln)
        dln = ds_ref[...] * (sg * (1.0 + ln * (1.0 - sg)))
        dx, dg, db = _ln_bwd(dln, xh, r, g_ref[...])
        dx_ref[...] = dx
        dg_ref[...] += dg
        db_ref[...] += db

    row = pl.BlockSpec((tm, c), lambda i: (i, 0))
    vec = pl.BlockSpec((1, c), lambda i: (0, 0))
    return _call(body, name, (t // tm,), [row, vec, vec, row], [row, vec, vec],
                 [_sds((t, c), F32), _sds((1, c), F32), _sds((1, c), F32)])(
        cv, ln_g.reshape(1, c), ln_b.reshape(1, c), ds)


def col_sum(v, name):
    t, n = v.shape
    tm = _tile(t, 256, SUBLANES)

    def body(v_ref, o_ref):
        @pl.when(pl.program_id(0) == 0)
        def _():
            o_ref[...] = jnp.zeros_like(o_ref)

        o_ref[...] += jnp.sum(v_ref[...], axis=0, keepdims=True)

    return _call(body, name, (t // tm,), [pl.BlockSpec((tm, n), lambda i: (i, 0))],
                 pl.BlockSpec((1, n), lambda i: (0, 0)), _sds((1, n), F32))(v)


def rope_tables_full(seq, dim):
    pos = jnp.arange(seq, dtype=F32)
    inv = ROPE_THETA ** (-(jnp.arange(0, dim, 2, dtype=F32) / dim))
    ang = pos[:, None] * inv[None, :]
    cos, sin = jnp.cos(ang), jnp.sin(ang)
    return jnp.concatenate([cos, cos], axis=-1), jnp.concatenate([-sin, sin], axis=-1)


def dsa_rope_fwd(qkv, cos, sin, seq, name):
    t, w = qkv.shape
    tm = _tile(seq, 256, SUBLANES)
    nsb = seq // tm
    hb = DSA_HEADS

    def body(x_ref, c_ref, s_ref, o_ref):
        cos_t, sin_t = c_ref[...], s_ref[...]
        for blk in range(w // LANES):
            v = x_ref[:, blk * LANES:(blk + 1) * LANES]
            if (blk // hb) % 3 < 2:
                v = _rope(v, cos_t, sin_t)
            o_ref[:, blk * LANES:(blk + 1) * LANES] = v

    row = pl.BlockSpec((tm, w), lambda i: (i, 0))
    tab = pl.BlockSpec((tm, LANES), lambda i: (i % nsb, 0))
    return _call(body, name, (t // tm,), [row, tab, tab], row, _sds((t, w), F32))(qkv, cos, sin)


def _dsa_rows(r, n, dil):
    start = r + n * DSA_BLOCK * dil
    return pl.ds(start, DSA_BLOCK) if dil == 1 else pl.ds(start, DSA_BLOCK, stride=dil)


def _dsa_load(ref, r, blocks, dil):
    parts = [ref[_dsa_rows(r, m, dil), :] for m in blocks]
    return parts[0] if len(parts) == 1 else jnp.concatenate(parts, axis=0)


def dsa_attn_all_fwd(qkv_r, batch, seq, name):
    t, w = qkv_r.shape
    h_n, n_g = DSA_HEADS, len(DSA_CONFIGS)
    scale = DSA_BLOCK ** -0.5
    chunk = _tile(seq, 256, SUBLANES)

    def body(q_ref, k_ref, v_ref, o_ref, ob_ref, lse_ref, og_ref, lg_ref):
        g = pl.program_id(2)
        for gi, (_, dil) in enumerate(DSA_CONFIGS):
            @pl.when(g == gi)
            def _(gi=gi, dil=dil):
                for r in range(dil):
                    for n in range(seq // dil // DSA_BLOCK):
                        blocks = (n - 1, n) if n else (n,)
                        rows = _dsa_rows(r, n, dil)
                        q = q_ref[rows, :].astype(BF16)
                        k = _dsa_load(k_ref, r, blocks, dil).astype(BF16)
                        v = _dsa_load(v_ref, r, blocks, dil).astype(BF16)
                        s = jnp.where(_dsa_mask(n), _dot_nt(q, k) * scale, NEG_INF)
                        m = jnp.max(s, axis=-1, keepdims=True)
                        p = jnp.exp(s - m)
                        l = jnp.sum(p, axis=-1, keepdims=True)
                        og_ref[gi, rows, :] = _dot(p.astype(BF16), v) / l
                        lg_ref[gi, rows, :] = jnp.broadcast_to(m + jnp.log(l), (DSA_BLOCK, LANES))

        @pl.when(g == n_g - 1)
        def _():
            for c0 in range(0, seq, chunk):
                sl = slice(c0, c0 + chunk)
                ls = [lg_ref[gi, sl, :] for gi in range(n_g)]
                m = functools.reduce(jnp.maximum, ls)
                es = [jnp.exp(v - m) for v in ls]
                tot = functools.reduce(lambda a, b: a + b, es)
                acc = jnp.zeros_like(m)
                for gi in range(n_g):
                    acc = acc + (es[gi] / tot) * og_ref[gi, sl, :]
                o_ref[sl, :] = acc
                ob_ref[sl, :] = acc.astype(BF16)
                lse_ref[sl, :] = m + jnp.log(tot)

    def spec(off):
        return pl.BlockSpec((None, seq, LANES), lambda b, h, g: (b, 0, (3 * g + off) * h_n + h))

    out_spec = pl.BlockSpec((None, seq, LANES), lambda b, h, g: (b, 0, h))
    view = qkv_r.reshape(batch, seq, w)
    shape = (batch, seq, h_n * LANES)
    o, o_b, lse = _call(body, name, (batch, h_n, n_g), [spec(0), spec(1), spec(2)], [out_spec] * 3,
                        [_sds(shape, F32), _sds(shape, BF16), _sds(shape, F32)],
                        scratch=[pltpu.VMEM((n_g, seq, LANES), F32)] * 2)(view, view, view)
    return o.reshape(t, -1), o_b.reshape(t, -1), lse.reshape(t, -1)


def dsa_attn_all_bwd(qkv_r, do, lse, delta, batch, seq, name):
    t, w = qkv_r.shape
    h_n, n_g = DSA_HEADS, len(DSA_CONFIGS)
    scale = DSA_BLOCK ** -0.5

    def body(q_ref, k_ref, v_ref, do_ref, lse_ref, dl_ref, d_ref):
        g = pl.program_id(2)
        d_ref[...] = jnp.zeros_like(d_ref)
        for gi, (_, dil) in enumerate(DSA_CONFIGS):
            @pl.when(g == gi)
            def _(dil=dil):
                for r in range(dil):
                    for n in range(seq // dil // DSA_BLOCK):
                        blocks = (n - 1, n) if n else (n,)
                        rows = _dsa_rows(r, n, dil)
                        q = q_ref[rows, :].astype(BF16)
                        k = _dsa_load(k_ref, r, blocks, dil).astype(BF16)
                        v = _dsa_load(v_ref, r, blocks, dil).astype(BF16)
                        dout = do_ref[rows, :].astype(BF16)
                        s = jnp.where(_dsa_mask(n), _dot_nt(q, k) * scale, NEG_INF)
                        p = jnp.exp(s - lse_ref[rows, :][:, 0:1])
                        ds = (p * (_dot_nt(dout, v) - dl_ref[rows, :][:, 0:1]) * scale).astype(BF16)
                        d_ref[0, rows, :] = _dot(ds, k)
                        dk = _dot_tn(ds, q)
                        dv = _dot_tn(p.astype(BF16), dout)
                        for i, blk in enumerate(blocks):
                            brows = _dsa_rows(r, blk, dil)
                            d_ref[1, brows, :] += dk[i * DSA_BLOCK:(i + 1) * DSA_BLOCK]
                            d_ref[2, brows, :] += dv[i * DSA_BLOCK:(i + 1) * DSA_BLOCK]

    def spec(off):
        return pl.BlockSpec((None, seq, LANES), lambda b, h, g: (b, 0, (3 * g + off) * h_n + h))

    head = pl.BlockSpec((None, seq, LANES), lambda b, h, g: (b, 0, h))
    out_spec = pl.BlockSpec((3, None, seq, LANES), lambda b, h, g: (g, b, 0, h))
    view = qkv_r.reshape(batch, seq, w)
    hv = [a.reshape(batch, seq, h_n * LANES) for a in (do, lse, delta)]
    out = _call(body, name, (batch, h_n, n_g), [spec(0), spec(1), spec(2), head, head, head], out_spec,
                _sds((3 * n_g, batch, seq, h_n * LANES), F32))(view, view, view, *hv)
    return out.reshape(3 * n_g, t, h_n * LANES)


def dsa_rope_bwd_all(grads, cos, sin, seq, name):
    n_parts, t, hw = grads.shape
    w = hw * n_parts
    tm = _tile(seq, 128, SUBLANES)
    nsb = seq // tm

    def body(g_ref, c_ref, s_ref, o_ref):
        cos_t, sin_t = c_ref[...], s_ref[...]
        for idx in range(n_parts):
            for h in range(hw // LANES):
                v = g_ref[idx, :, h * LANES:(h + 1) * LANES]
                if idx % 3 < 2:
                    v = _rope_t(v, cos_t, sin_t)
                col = idx * hw + h * LANES
                o_ref[:, col:col + LANES] = v.astype(BF16)

    tab = pl.BlockSpec((tm, LANES), lambda i: (i % nsb, 0))
    out = _call(body, name, (t // tm,), [pl.BlockSpec((n_parts, tm, hw), lambda i: (0, i, 0)), tab, tab],
                pl.BlockSpec((tm, w), lambda i: (i, 0)), _sds((t, w), BF16))(grads, cos, sin)
    return out[None]


def _dsa_mask(n):
    qi = lax.broadcasted_iota(jnp.int32, (DSA_BLOCK, (1 if n == 0 else 2) * DSA_BLOCK), 0)
    kj = lax.broadcasted_iota(jnp.int32, (DSA_BLOCK, (1 if n == 0 else 2) * DSA_BLOCK), 1)
    if n == 0:
        return qi >= kj
    dist = DSA_BLOCK + qi - kj
    return (dist >= 0) & (dist <= DSA_BLOCK)


def head_delta(do, o, name):
    t, hw = do.shape
    tm = _tile(t, 256, SUBLANES)

    def body(do_ref, o_ref, d_ref):
        for h in range(hw // LANES):
            sl = slice(h * LANES, (h + 1) * LANES)
            d = jnp.sum(do_ref[:, sl] * o_ref[:, sl], axis=-1, keepdims=True)
            d_ref[:, sl] = jnp.broadcast_to(d, (tm, LANES))

    row = pl.BlockSpec((tm, hw), lambda i: (i, 0))
    return _call(body, name, (t // tm,), [row, row], row, _sds((t, hw), F32))(do, o)


def mla_rope_tables(seq):
    pos = jnp.arange(seq, dtype=F32)
    inv = ROPE_THETA ** (-(jnp.arange(0, MLA_ROPE, 2, dtype=F32) / MLA_ROPE))
    ang = pos[:, None] * inv[None, :]
    cos, sin, zero = jnp.cos(ang), jnp.sin(ang), jnp.zeros_like(ang)
    return jnp.concatenate([cos, zero, cos, zero], axis=-1), jnp.concatenate([-sin, zero, sin, zero], axis=-1)


def _spread_rope_cols(w_pe):
    half = MLA_ROPE // 2
    zero = jnp.zeros(w_pe.shape[:-1] + (half,), w_pe.dtype)
    return jnp.concatenate([w_pe[..., :half], zero, w_pe[..., half:], zero], axis=-1)


def _gather_rope_cols(g_pe):
    half = MLA_ROPE // 2
    return jnp.concatenate([g_pe[..., :half], g_pe[..., 2 * half:3 * half]], axis=-1)


def mla_low_fwd(c, q_norm, kv_norm, cos, sin, seq, name):
    t, w = c.shape
    rk = MLA_RANK
    tm = _tile(seq, 256, SUBLANES)
    nsb = seq // tm

    def body(c_ref, qg_ref, kg_ref, cs_ref, sn_ref, qn_ref, kn_ref, kp_ref):
        for lo, g_ref, o_ref in ((0, qg_ref, qn_ref), (rk, kg_ref, kn_ref)):
            v = c_ref[:, lo:lo + rk]
            r = lax.rsqrt(jnp.mean(v * v, axis=-1, keepdims=True) + NORM_EPS)
            o_ref[...] = (v * r * g_ref[...]).astype(BF16)
        kp_ref[...] = _rope(c_ref[:, 2 * rk:], cs_ref[...], sn_ref[...]).astype(BF16)

    row = lambda n: pl.BlockSpec((tm, n), lambda i: (i, 0))
    vec = pl.BlockSpec((1, rk), lambda i: (0, 0))
    tab = pl.BlockSpec((tm, LANES), lambda i: (i % nsb, 0))
    return _call(body, name, (t // tm,), [row(w), vec, vec, tab, tab], [row(rk), row(rk), row(LANES)],
                 [_sds((t, rk), BF16), _sds((t, rk), BF16), _sds((t, LANES), BF16)])(
        c, q_norm.reshape(1, rk), kv_norm.reshape(1, rk), cos, sin)


def mla_low_bwd(c, q_norm, kv_norm, dqn, dkn, dkp, cos, sin, seq, name):
    t, w = c.shape
    rk = MLA_RANK
    tm = _tile(seq, 256, SUBLANES)
    nsb = seq // tm

    def body(c_ref, qg_ref, kg_ref, dq_ref, dk_ref, dp_ref, cs_ref, sn_ref, dc_ref, dqg_ref, dkg_ref):
        @pl.when(pl.program_id(0) == 0)
        def _():
            dqg_ref[...] = jnp.zeros_like(dqg_ref)
            dkg_ref[...] = jnp.zeros_like(dkg_ref)

        for lo, g_ref, d_ref, dg_ref in ((0, qg_ref, dq_ref, dqg_ref), (rk, kg_ref, dk_ref, dkg_ref)):
            dx, dg = _rms_bwd_math(c_ref[:, lo:lo + rk], g_ref[...], d_ref[...])
            dc_ref[:, lo:lo + rk] = dx.astype(BF16)
            dg_ref[...] += dg
        dc_ref[:, 2 * rk:] = _rope_t(dp_ref[...], cs_ref[...], sn_ref[...]).astype(BF16)

    row = lambda n: pl.BlockSpec((tm, n), lambda i: (i, 0))
    vec = pl.BlockSpec((1, rk), lambda i: (0, 0))
    tab = pl.BlockSpec((tm, LANES), lambda i: (i % nsb, 0))
    dc, dqg, dkg = _call(body, name, (t // tm,), [row(w), vec, vec, row(rk), row(rk), row(LANES), tab, tab],
                         [row(w), vec, vec], [_sds((t, w), BF16), _sds((1, rk), F32), _sds((1, rk), F32)])(
        c, q_norm.reshape(1, rk), kv_norm.reshape(1, rk), dqn, dkn, dkp, cos, sin)
    return dc[None], dqg, dkg


def mla_rope_cast(q, kv, cos, sin, seq, name, transpose=False):
    t, w = q.shape
    tm = _tile(seq, 256, SUBLANES)
    nsb = seq // tm
    fn = _rope_t if transpose else _rope

    def body(q_ref, kv_ref, cs_ref, sn_ref, qo_ref, kvo_ref):
        cos_t, sin_t = cs_ref[...], sn_ref[...]
        for blk in range(w // LANES):
            sl = slice(blk * LANES, (blk + 1) * LANES)
            v = q_ref[:, sl]
            if blk % 2 == 1:
                v = fn(v, cos_t, sin_t)
            qo_ref[:, sl] = v.astype(BF16)
        kvo_ref[...] = kv_ref[...].astype(BF16)

    row = pl.BlockSpec((tm, w), lambda i: (i, 0))
    tab = pl.BlockSpec((tm, LANES), lambda i: (i % nsb, 0))
    return _call(body, name, (t // tm,), [row, row, tab, tab], [row, row], [_sds((t, w), BF16)] * 2)(q, kv, cos, sin)


def _causal_mask(i, j, tq, tk):
    qpos = i * tq + lax.broadcasted_iota(jnp.int32, (tq, tk), 0)
    kpos = j * tk + lax.broadcasted_iota(jnp.int32, (tq, tk), 1)
    return kpos <= qpos


def mla_attn_fwd(q_b, kv_b, kp_b, batch, seq, name, tq=512):
    t, w = q_b.shape
    h_n = w // (2 * LANES)
    tq = _tile(seq, tq, SUBLANES)
    nq = seq // tq
    scale = (MLA_NOPE + MLA_ROPE) ** -0.5

    def body(q_ref, kv_ref, kp_ref, o_ref, lse_ref):
        i = pl.program_id(2)
        qn, qp = q_ref[:, :LANES], q_ref[:, LANES:]

        def step(j, carry):
            m, l, acc = carry
            rows = pl.ds(pl.multiple_of(j * tq, tq), tq)
            s = (_dot_nt(qn, kv_ref[rows, :LANES]) + _dot_nt(qp, kp_ref[rows, :])) * scale
            s = jnp.where(_causal_mask(i, j, tq, tq), s, NEG_INF)
            mn = jnp.maximum(m, jnp.max(s, axis=-1, keepdims=True))
            p = jnp.exp(s - mn)
            a = jnp.exp(m - mn)
            return mn, a * l + jnp.sum(p, axis=-1, keepdims=True), a * acc + _dot(p.astype(BF16), kv_ref[rows, LANES:])

        init = (jnp.full((tq, 1), NEG_INF, F32), jnp.zeros((tq, 1), F32), jnp.zeros((tq, LANES), F32))
        m, l, acc = lax.fori_loop(0, i + 1, step, init)
        o_ref[...] = acc / l
        lse_ref[...] = jnp.broadcast_to(m + jnp.log(l), (tq, LANES))

    in_specs = [pl.BlockSpec((None, tq, 2 * LANES), lambda b, h, i: (b, i, h)),
                pl.BlockSpec((None, seq, 2 * LANES), lambda b, h, i: (b, 0, h)),
                pl.BlockSpec((None, seq, LANES), lambda b, h, i: (b, 0, 0))]
    out_spec = pl.BlockSpec((None, tq, LANES), lambda b, h, i: (b, i, h))
    o, lse = _call(body, name, (batch, h_n, nq), in_specs, [out_spec, out_spec],
                   [_sds((batch, seq, h_n * LANES), F32)] * 2)(
        q_b.reshape(batch, seq, w), kv_b.reshape(batch, seq, w), kp_b.reshape(batch, seq, LANES))
    return o.reshape(t, h_n * LANES), lse.reshape(t, h_n * LANES)


def mla_attn_bwd(q_b, kv_b, kp_b, do, lse, delta, batch, seq, name, tq=512):
    t, w = q_b.shape
    h_n = w // (2 * LANES)
    tq = _tile(seq, tq, SUBLANES)
    nq = seq // tq
    scale = (MLA_NOPE + MLA_ROPE) ** -0.5

    def body(q_ref, kv_ref, kp_ref, do_ref, lse_ref, dl_ref, dq_ref, dkv_ref, dkp_ref):
        dq_ref[...] = jnp.zeros_like(dq_ref)
        dkv_ref[...] = jnp.zeros_like(dkv_ref)

        @pl.when(pl.program_id(1) == 0)
        def _():
            dkp_ref[...] = jnp.zeros_like(dkp_ref)

        def outer(j, _):
            krows = pl.ds(pl.multiple_of(j * tq, tq), tq)
            kn, v, kp = kv_ref[krows, :LANES], kv_ref[krows, LANES:], kp_ref[krows, :]

            def inner(i, _):
                qrows = pl.ds(pl.multiple_of(i * tq, tq), tq)
                qn, qp = q_ref[qrows, :LANES], q_ref[qrows, LANES:]
                dout = do_ref[qrows, :].astype(BF16)
                s = (_dot_nt(qn, kn) + _dot_nt(qp, kp)) * scale
                s = jnp.where(_causal_mask(i, j, tq, tq), s, NEG_INF)
                p = jnp.exp(s - lse_ref[qrows, 0:1])
                ds = (p * (_dot_nt(dout, v) - dl_ref[qrows, 0:1]) * scale).astype(BF16)
                dq_ref[qrows, :LANES] += _dot(ds, kn)
                dq_ref[qrows, LANES:] += _dot(ds, kp)
                dkv_ref[krows, :LANES] += _dot_tn(ds, qn)
                dkv_ref[krows, LANES:] += _dot_tn(p.astype(BF16), dout)
                dkp_ref[krows, :] += _dot_tn(ds, qp)
                return 0

            lax.fori_loop(j, nq, inner, 0)
            return 0

        lax.fori_loop(0, nq, outer, 0)

    wide = pl.BlockSpec((None, seq, 2 * LANES), lambda b, h: (b, 0, h))
    head = pl.BlockSpec((None, seq, LANES), lambda b, h: (b, 0, h))
    shared = pl.BlockSpec((None, seq, LANES), lambda b, h: (b, 0, 0))
    hv = [a.reshape(batch, seq, h_n * LANES) for a in (do, lse, delta)]
    dq, dkv, dkp = _call(body, name, (batch, h_n), [wide, wide, shared, head, head, head], [wide, wide, shared],
                         [_sds((batch, seq, w), F32), _sds((batch, seq, w), F32), _sds((batch, seq, LANES), F32)])(
        q_b.reshape(batch, seq, w), kv_b.reshape(batch, seq, w), kp_b.reshape(batch, seq, LANES), *hv)
    return dq.reshape(t, w), dkv.reshape(t, w), dkp.reshape(t, LANES)


def _sgu_common(z_ref, g_ref, b_ref, ws_ref, bs_ref):
    e = z_ref.shape[-1]
    ge = e // SG_GROUPS
    zu, zv = z_ref[0], z_ref[1]
    cu, cv = _gelu_cdf(zu), _gelu_cdf(zv)
    u = zu * cu
    xh, r = _ln_stats(zv * cv)
    vn = (xh * g_ref[...] + b_ref[...]).astype(BF16)
    tri = (lax.broadcasted_iota(jnp.int32, (SG_CHUNK, SG_CHUNK), 0)
           >= lax.broadcasted_iota(jnp.int32, (SG_CHUNK, SG_CHUNK), 1))
    ws = [jnp.where(tri, ws_ref[g], 0.0).astype(BF16) for g in range(SG_GROUPS)]
    v2 = [_dot(ws[g], vn[:, g * ge:(g + 1) * ge]) + bs_ref[:, g:g + 1] for g in range(SG_GROUPS)]
    return zu, zv, cu, cv, u, xh, r, vn, tri, ws, v2


def sgu_fwd(z2, ln_g, ln_b, w_s, b_s_t, name):
    _, t, e = z2.shape
    ge = e // SG_GROUPS

    def body(z_ref, g_ref, b_ref, ws_ref, bs_ref, y_ref):
        _, _, _, _, u, _, _, _, _, _, v2 = _sgu_common(z_ref, g_ref, b_ref, ws_ref, bs_ref)
        for g in range(SG_GROUPS):
            y_ref[:, g * ge:(g + 1) * ge] = (u[:, g * ge:(g + 1) * ge] * v2[g]).astype(BF16)

    vec = pl.BlockSpec((1, e), lambda i: (0, 0))
    in_specs = [pl.BlockSpec((2, SG_CHUNK, e), lambda i: (0, i, 0)), vec, vec,
                pl.BlockSpec((SG_GROUPS, SG_CHUNK, SG_CHUNK), lambda i: (0, 0, 0)),
                pl.BlockSpec((SG_CHUNK, SG_GROUPS), lambda i: (0, 0))]
    return _call(body, name, (t // SG_CHUNK,), in_specs, pl.BlockSpec((SG_CHUNK, e), lambda i: (i, 0)),
                 _sds((t, e), BF16))(z2, ln_g.reshape(1, e), ln_b.reshape(1, e), w_s, b_s_t)


def sgu_bwd(z2, dy, ln_g, ln_b, w_s, b_s_t, name):
    _, t, e = z2.shape
    ge = e // SG_GROUPS

    def body(z_ref, dy_ref, g_ref, b_ref, ws_ref, bs_ref, dz_ref, dbi_ref, dg_ref, db_ref, dws_ref, dbs_ref):
        @pl.when(pl.program_id(0) == 0)
        def _():
            for ref in (dbi_ref, dg_ref, db_ref, dws_ref, dbs_ref):
                ref[...] = jnp.zeros_like(ref)

        zu, zv, cu, cv, u, xh, r, vn, tri, ws, v2 = _sgu_common(z_ref, g_ref, b_ref, ws_ref, bs_ref)
        dy = dy_ref[...]
        dvn = []
        for g in range(SG_GROUPS):
            sl = slice(g * ge, (g + 1) * ge)
            dyg = dy[:, sl]
            du = dyg * v2[g] * _gelu_grad(zu[:, sl], cu[:, sl])
            dz_ref[0, :, sl] = du.astype(BF16)
            dbi_ref[0, :, sl] += jnp.sum(du, axis=0, keepdims=True)
            dv2 = dyg * u[:, sl]
            dbs_ref[:, g:g + 1] += jnp.sum(dv2, axis=-1, keepdims=True)
            dv2b = dv2.astype(BF16)
            dws_ref[g] += jnp.where(tri, _dot_nt(dv2b, vn[:, sl]), 0.0)
            dvn.append(_dot_tn(ws[g], dv2b))
        dvn = jnp.concatenate(dvn, axis=-1)
        dv, dg, db = _ln_bwd(dvn, xh, r, g_ref[...])
        dzv = dv * _gelu_grad(zv, cv)
        dz_ref[1] = dzv.astype(BF16)
        dbi_ref[1] += jnp.sum(dzv, axis=0, keepdims=True)
        dg_ref[...] += dg
        db_ref[...] += db

    vec = pl.BlockSpec((1, e), lambda i: (0, 0))
    wsb = pl.BlockSpec((SG_GROUPS, SG_CHUNK, SG_CHUNK), lambda i: (0, 0, 0))
    bsb = pl.BlockSpec((SG_CHUNK, SG_GROUPS), lambda i: (0, 0))
    zblk = pl.BlockSpec((2, SG_CHUNK, e), lambda i: (0, i, 0))
    in_specs = [zblk, pl.BlockSpec((SG_CHUNK, e), lambda i: (i, 0)), vec, vec, wsb, bsb]
    out_specs = [zblk, pl.BlockSpec((2, 1, e), lambda i: (0, 0, 0)), vec, vec, wsb, bsb]
    out_shape = [_sds((2, t, e), BF16), _sds((2, 1, e), F32), _sds((1, e), F32), _sds((1, e), F32),
                 _sds((SG_GROUPS, SG_CHUNK, SG_CHUNK), F32), _sds((SG_CHUNK, SG_GROUPS), F32)]
    return _call(body, name, (t // SG_CHUNK,), in_specs, out_specs, out_shape)(
        z2, dy, ln_g.reshape(1, e), ln_b.reshape(1, e), w_s, b_s_t)


_COL = ("cc_w_in", "ffn_w_up", "dsa_w_qkv", "dsa_w_o", "mla_w_qb", "mla_w_kvb", "sg_w_in")
_ROW = ("cc_w_out", "ffn_w_down", "mla_w_in", "mla_w_o", "sg_w_out")
_SMALL_COL = ("cc_dw_w", "ffn_dw_w")
_RELAID = ("cc_w_in", "dsa_w_o", "mla_w_kvb")
_MIXERS = ("cc", "dsa", "mla", "sg")


def _base(name):
    return name.split("_", 1)[1] if name[0] == "l" and name[1].isdigit() else name


def _ffn_fwd(pre, p, full, h_prev, mix_out, bsz, seq):
    t, d = h_prev.shape
    h, hn = rms_fwd(h_prev, p[pre + "norm_ffn"], pre + "ffn_norm", add=mix_out)
    w_up, w_down = full[pre + "ffn_w_up"], full[pre + "ffn_w_down"]
    f = w_down.shape[1]
    z2r = mm_nn(hn[None], w_up, pre + "ffn_up", out_parts=2).reshape(2, bsz, seq, f)
    dw_w2 = jnp.transpose(full[pre + "ffn_dw_w"].reshape(FFN_WIDTH, 2, f), (1, 0, 2))
    dw_b2 = p[pre + "ffn_dw_b"].reshape(2, 1, f)
    y = ffn_gate_fwd(z2r, dw_w2, dw_b2, pre + "ffn_gate").reshape(1, t, f)
    out = mm_nn(y, w_down, pre + "ffn_down")[0]
    return out, dict(h=h, hn=hn, z2r=z2r, y=y, dw_w2=dw_w2, dw_b2=dw_b2)


def relayout_cols(g, name, to_shards):
    k = g.shape[1]
    n = g.shape[2] // N_DEV if to_shards else g.shape[2]
    tk = _row_tile(k, n, itemsize=g.dtype.itemsize, budget=1 << 21)

    def body(i_ref, o_ref):
        o_ref[...] = i_ref[...]

    shard = pl.BlockSpec((None, tk, n), lambda j, ki: (j, ki, 0))
    whole = pl.BlockSpec((None, tk, n), lambda j, ki: (0, ki, j))
    if to_shards:
        return _call(body, name, (N_DEV, k // tk), [whole], shard, _sds((N_DEV, k, n), g.dtype))(g)
    return _call(body, name, (N_DEV, k // tk), [shard], whole, _sds((1, k, N_DEV * n), g.dtype))(g)


def _col_grad(a, b, w, name):
    g = mm_tn(a, b, w.shape[0], name)
    return g if w.shape[0] == N_DEV else relayout_cols(g, name + "_shards", True)


def _ffn_bwd(pre, p, full, ctx, dh_out, big_g, small_g, bsz, seq, mid):
    w_up, w_down = full[pre + "ffn_w_up"], full[pre + "ffn_w_down"]
    f = w_down.shape[1]
    dh_out, dh_b = dh_out
    t, d = dh_out.shape
    dh3 = dh_b[None]
    dyv = mm_nt(dh3, w_down, pre + "ffn_down_dx")
    big_g[pre + "ffn_w_down"] = mm_tn(ctx["y"], dh3, 1, pre + "ffn_down_dw").reshape(N_DEV, f // N_DEV, d)
    mid(dyv)
    dz2, ddw2, ddb2 = ffn_gate_bwd(ctx["z2r"], dyv.reshape(bsz, seq, f), ctx["dw_w2"], ctx["dw_b2"],
                                   pre + "ffn_gate_bwd")
    dz2 = dz2.reshape(2, t, f)
    big_g[pre + "ffn_w_up"] = mm_tn(ctx["hn"][None], dz2, N_DEV, pre + "ffn_up_dw")
    dhn = mm_nt(dz2, w_up, pre + "ffn_up_dx")
    small_g[pre + "ffn_dw_w"] = jnp.transpose(ddw2, (1, 0, 2)).reshape(FFN_WIDTH, 2 * f)
    small_g[pre + "ffn_dw_b"] = ddb2.reshape(2 * f)
    dh, dh_b, dg = rms_bwd(ctx["h"], p[pre + "norm_ffn"], dhn, dh_out, pre + "ffn_norm_bwd")
    small_g[pre + "norm_ffn"] = dg.reshape(d)
    return dh, dh_b


def _cc_fwd(pre, p, full, hn, bsz, seq, aux):
    t, c = hn.shape
    z2r = mm_nn(hn[None], full[pre + "cc_w_in"], pre + "cc_in", bias=p[pre + "cc_b_in"], out_parts=2).reshape(
        2, bsz, seq, c)
    cv = cc_conv_fwd(z2r, full[pre + "cc_dw_w"], p[pre + "cc_dw_b"], pre + "cc_conv").reshape(t, c)
    s = cc_ln_fwd(cv, p[pre + "cc_ln_g"], p[pre + "cc_ln_b"], pre + "cc_ln")
    out = mm_nn(s[None], full[pre + "cc_w_out"], pre + "cc_out", bias=p[pre + "cc_b_out"])[0]
    return out, dict(z2r=z2r, cv=cv, s=s)


def _cc_bwd(pre, p, full, ctx, hn, dm, big_g, small_g, bsz, seq, aux, mid):
    dm, dm_b = dm
    t, c = dm.shape
    dm3 = dm_b[None]
    small_g[pre + "cc_b_out"] = col_sum(dm, pre + "cc_bout_g").reshape(c)
    ds = mm_nt(dm3, full[pre + "cc_w_out"], pre + "cc_out_dx")
    big_g[pre + "cc_w_out"] = mm_tn(ctx["s"][None], dm3, 1, pre + "cc_out_dw").reshape(N_DEV, c // N_DEV, c)
    mid(ds)
    dcv, dlg, dlb = cc_ln_bwd(ctx["cv"], p[pre + "cc_ln_g"], p[pre + "cc_ln_b"], ds, pre + "cc_ln_bwd")
    dz2, dbi, ddw, ddb = cc_conv_bwd(ctx["z2r"], dcv.reshape(bsz, seq, c), full[pre + "cc_dw_w"], pre + "cc_conv_bwd")
    dz2 = dz2.reshape(2, t, c)
    big_g[pre + "cc_w_in"] = _col_grad(hn[None], dz2, full[pre + "cc_w_in"], pre + "cc_in_dw")
    small_g[pre + "cc_ln_g"], small_g[pre + "cc_ln_b"] = dlg.reshape(c), dlb.reshape(c)
    small_g[pre + "cc_b_in"], small_g[pre + "cc_dw_w"], small_g[pre + "cc_dw_b"] = dbi.reshape(2 * c), ddw, ddb.reshape(c)
    return mm_nt(dz2, full[pre + "cc_w_in"], pre + "cc_in_dx")


def _dsa_fwd(pre, p, full, hn, bsz, seq, aux):
    cos, sin = aux["dsa_tables"]
    qkv = mm_nn(hn[None], full[pre + "dsa_w_qkv"], pre + "dsa_qkv")[0]
    qkv_r = dsa_rope_fwd(qkv, cos, sin, seq, pre + "dsa_rope")
    for window, dil in DSA_CONFIGS:
        assert window // dil == DSA_BLOCK and (seq // dil) % DSA_BLOCK == 0
    o, o_b, lse = dsa_attn_all_fwd(qkv_r, bsz, seq, pre + "dsa_attn")
    out = mm_nn(o_b[None], full[pre + "dsa_w_o"], pre + "dsa_o")[0]
    return out, dict(qkv_r=qkv_r, o=o, o_b=o_b, lse=lse)


def _dsa_bwd(pre, p, full, ctx, hn, dm, big_g, small_g, bsz, seq, aux, mid):
    cos, sin = aux["dsa_tables"]
    dm, dm_b = dm
    dm3 = dm_b[None]
    do = mm_nt(dm3, full[pre + "dsa_w_o"], pre + "dsa_o_dx")
    big_g[pre + "dsa_w_o"] = _col_grad(ctx["o_b"][None], dm3, full[pre + "dsa_w_o"], pre + "dsa_o_dw")
    mid(do)
    delta = head_delta(do, ctx["o"], pre + "dsa_delta")
    grads = dsa_attn_all_bwd(ctx["qkv_r"], do, ctx["lse"], delta, bsz, seq, pre + "dsa_attn_bwd")
    dqkv = dsa_rope_bwd_all(grads, cos, sin, seq, pre + "dsa_rope_bwd")
    big_g[pre + "dsa_w_qkv"] = mm_tn(hn[None], dqkv, N_DEV, pre + "dsa_qkv_dw")
    return mm_nt(dqkv, full[pre + "dsa_w_qkv"], pre + "dsa_qkv_dx")


def _mla_weights(pre, full):
    rk, hd = MLA_RANK, MLA_NOPE + MLA_ROPE
    w_in = full[pre + "mla_w_in"][0]
    w_in_p = jnp.concatenate([w_in[:, :2 * rk], _spread_rope_cols(w_in[:, 2 * rk:])], axis=1)[None]
    w_qb = jnp.transpose(full[pre + "mla_w_qb"], (1, 0, 2)).reshape(rk, MLA_HEADS, hd)
    w_qb_p = jnp.concatenate([w_qb[..., :MLA_NOPE], _spread_rope_cols(w_qb[..., MLA_NOPE:])], axis=-1)
    return w_in_p, w_qb_p.reshape(1, rk, MLA_HEADS * 2 * LANES)


def _mla_fwd(pre, p, full, hn, bsz, seq, aux):
    cos, sin = aux["mla_tables"]
    w_in_p, w_qb_p = _mla_weights(pre, full)
    c = mm_nn(hn[None], w_in_p, pre + "mla_in")[0]
    qn, kn, kp = mla_low_fwd(c, p[pre + "mla_q_norm"], p[pre + "mla_kv_norm"], cos, sin, seq, pre + "mla_low")
    q = mm_nn(qn[None], w_qb_p, pre + "mla_qb")[0]
    kv = mm_nn(kn[None], full[pre + "mla_w_kvb"], pre + "mla_kvb")[0]
    q_b, kv_b = mla_rope_cast(q, kv, cos, sin, seq, pre + "mla_rope")
    o, lse = mla_attn_fwd(q_b, kv_b, kp, bsz, seq, pre + "mla_attn")
    o_b = o.astype(BF16)
    out = mm_nn(o_b[None], full[pre + "mla_w_o"], pre + "mla_o")[0]
    return out, dict(c=c, qn=qn, kn=kn, kp=kp, q_b=q_b, kv_b=kv_b, o=o, o_b=o_b, lse=lse, w_in_p=w_in_p, w_qb_p=w_qb_p)


def _mla_bwd(pre, p, full, ctx, hn, dm, big_g, small_g, bsz, seq, aux, mid):
    cos, sin = aux["mla_tables"]
    dm, dm_b = dm
    t, d = dm.shape
    rk = MLA_RANK
    dm3 = dm_b[None]
    do = mm_nt(dm3, full[pre + "mla_w_o"], pre + "mla_o_dx")
    big_g[pre + "mla_w_o"] = mm_tn(ctx["o_b"][None], dm3, 1, pre + "mla_o_dw").reshape(N_DEV, -1, d)
    mid(do)
    delta = head_delta(do, ctx["o"], pre + "mla_delta")
    dq, dkv, dkp = mla_attn_bwd(ctx["q_b"], ctx["kv_b"], ctx["kp"], do, ctx["lse"], delta, bsz, seq, pre + "mla_attn_bwd")
    dq_b, dkv_b = mla_rope_cast(dq, dkv, cos, sin, seq, pre + "mla_rope_bwd", transpose=True)
    g_qb = mm_tn(ctx["qn"][None], dq_b[None], 1, pre + "mla_qb_dw")[0].reshape(rk, MLA_HEADS, 2 * LANES)
    g_qb = jnp.concatenate([g_qb[..., :MLA_NOPE], _gather_rope_cols(g_qb[..., MLA_NOPE:])], axis=-1)
    big_g[pre + "mla_w_qb"] = jnp.transpose(g_qb.reshape(rk, N_DEV, -1), (1, 0, 2))
    dqn = mm_nt(dq_b[None], ctx["w_qb_p"], pre + "mla_qb_dx")
    big_g[pre + "mla_w_kvb"] = _col_grad(ctx["kn"][None], dkv_b[None], full[pre + "mla_w_kvb"], pre + "mla_kvb_dw")
    dkn = mm_nt(dkv_b[None], full[pre + "mla_w_kvb"], pre + "mla_kvb_dx")
    dc, dqg, dkg = mla_low_bwd(ctx["c"], p[pre + "mla_q_norm"], p[pre + "mla_kv_norm"], dqn, dkn, dkp, cos, sin, seq,
                               pre + "mla_low_bwd")
    small_g[pre + "mla_q_norm"], small_g[pre + "mla_kv_norm"] = dqg.reshape(rk), dkg.reshape(rk)
    g_in = mm_tn(hn[None], dc, 1, pre + "mla_in_dw")[0]
    g_in = jnp.concatenate([g_in[:, :2 * rk], _gather_rope_cols(g_in[:, 2 * rk:])], axis=1)
    big_g[pre + "mla_w_in"] = g_in.reshape(N_DEV, d // N_DEV, -1)
    return mm_nt(dc, ctx["w_in_p"], pre + "mla_in_dx")


def _sg_fwd(pre, p, full, hn, bsz, seq, aux):
    z2 = mm_nn(hn[None], full[pre + "sg_w_in"], pre + "sg_in", bias=p[pre + "sg_b_in"], out_parts=2)
    b_s_t = jnp.transpose(p[pre + "sg_b_s"])
    y = sgu_fwd(z2, p[pre + "sg_ln_g"], p[pre + "sg_ln_b"], p[pre + "sg_w_s"], b_s_t, pre + "sg_mix")
    out = mm_nn(y[None], full[pre + "sg_w_out"], pre + "sg_out", bias=p[pre + "sg_b_out"])[0]
    return out, dict(z2=z2, y=y, b_s_t=b_s_t)


def _sg_bwd(pre, p, full, ctx, hn, dm, big_g, small_g, bsz, seq, aux, mid):
    dm, dm_b = dm
    t, d = dm.shape
    dm3 = dm_b[None]
    small_g[pre + "sg_b_out"] = col_sum(dm, pre + "sg_bout_g").reshape(d)
    dy = mm_nt(dm3, full[pre + "sg_w_out"], pre + "sg_out_dx")
    big_g[pre + "sg_w_out"] = mm_tn(ctx["y"][None], dm3, 1, pre + "sg_out_dw").reshape(N_DEV, -1, d)
    mid(dy)
    dz2, dbi, dlg, dlb, dws, dbs_t = sgu_bwd(ctx["z2"], dy, p[pre + "sg_ln_g"], p[pre + "sg_ln_b"], p[pre + "sg_w_s"],
                                             ctx["b_s_t"], pre + "sg_mix_bwd")
    big_g[pre + "sg_w_in"] = mm_tn(hn[None], dz2, N_DEV, pre + "sg_in_dw")
    small_g[pre + "sg_b_in"] = dbi.reshape(-1)
    small_g[pre + "sg_ln_g"], small_g[pre + "sg_ln_b"] = dlg.reshape(-1), dlb.reshape(-1)
    small_g[pre + "sg_w_s"], small_g[pre + "sg_b_s"] = dws, jnp.transpose(dbs_t)
    return mm_nt(dz2, full[pre + "sg_w_in"], pre + "sg_in_dx")


_MIX_FWD = dict(cc=_cc_fwd, dsa=_dsa_fwd, mla=_mla_fwd, sg=_sg_fwd)
_MIX_BWD = dict(cc=_cc_bwd, dsa=_dsa_bwd, mla=_mla_bwd, sg=_sg_bwd)


def _reduce_scatter(parts, core, chip, tag, adams=None):
    recv = rs_core_exchange(parts, tag + "_cores")
    sums = [rs_core_add(pt, rc, core, "%s_add%d" % (tag, i)) for i, (pt, rc) in enumerate(zip(parts, recv))]
    recv2 = rs_chip_exchange(sums, tag + "_chips")
    return [rs_finish(s, r2, chip, "%s_fin%d" % (tag, i), adam=None if adams is None else adams[i])
            for i, (s, r2) in enumerate(zip(sums, recv2))]


class _Schedule:
    def __init__(self):
        self.actions = {}

    def at(self, key, fn):
        self.actions.setdefault(key, []).append(fn)

    def run(self, key, x):
        after = list(x) if isinstance(x, (list, tuple)) else [x]
        for fn in self.actions.pop(key, []):
            fn(after)


def _aside(fn):
    saved = list(_PENDING)
    _PENDING.clear()
    out = fn()
    _PENDING[:0] = saved
    return out


def _gather_overlapped(sched, keys, shards, dev, tag, deliver):
    n = len(shards)
    state = {}

    def start(x):
        lands = [lax.dynamic_update_slice(lax.empty((N_DEV,) + s.shape, s.dtype), s[None], (dev,) + (0,) * s.ndim)
                 for s in shards]
        state["first"] = exchange_start(_ag_plan_first, 4 * n, shards, lands, x, tag +"_start")
        _PENDING.append(state["first"][4])

    def pass_on(x):
        _, lands = exchange_wait(_ag_plan_first, state["first"], x, tag +"_wait")
        state["pass"] = exchange_start(_ag_plan_pass, 3 * n, [], lands, [], tag + "_pass")
        _PENDING.append(state["pass"][4])

    def done(x):
        deliver(exchange_wait(_ag_plan_pass, state["pass"], x, tag +"_done")[1])

    for key, fn in zip(keys, (start, pass_on, done)):
        sched.at(key, fn)


def _reduce_overlapped(sched, keys, parts, adams, core, chip, tag, deliver):
    n = len(parts)
    state = {}

    def start(x):
        lands = [lax.empty((4,) + pt.shape[1:], pt.dtype) for pt in parts]
        state["cores"] = exchange_start(_rs_plan_cores, 4 * n, parts, lands, x, tag +"_start")
        _PENDING.append(state["cores"][4])

    def middle(x):
        mine, recv = exchange_wait(_rs_plan_cores, state["cores"], x, tag +"_wait")
        sums = _aside(lambda: [rs_core_add(pt, rc, core, "%s_add%d" % (tag, i))
                               for i, (pt, rc) in enumerate(zip(mine, recv))])
        lands = [lax.empty((3,) + s.shape[1:], s.dtype) for s in sums]
        state["chips"] = exchange_start(_rs_plan_chips, 3 * n, sums, lands, [], tag + "_send")
        _PENDING.append(state["chips"][4])

    def done(x):
        sums, recv = exchange_wait(_rs_plan_chips, state["chips"], x, tag +"_done")
        deliver(_aside(lambda: [rs_finish(s, r2, chip, "%s_fin%d" % (tag, i), adam=adams[i])
                                for i, (s, r2) in enumerate(zip(sums, recv))]))

    for key, fn in zip(keys, (start, middle, done)):
        sched.at(key, fn)


def _pack(arrays):
    flat = jnp.concatenate([a.reshape(-1) for a in arrays])
    unit = N_DEV * SUBLANES * LANES
    padded = -(-flat.shape[0] // unit) * unit
    return jnp.pad(flat, (0, padded - flat.shape[0]))


def _unpack(flat, like):
    out, pos = [], 0
    for a in like:
        out.append(flat[pos:pos + a.size].reshape(a.shape))
        pos += a.size
    return out


def _train_step(p):
    names = list(p)
    wnames = names[1:names.index("loss_target")]
    x = p["x"]
    bsz, seq, d = x.shape
    t = bsz * seq
    xi, yi, ci = lax.axis_index("x"), lax.axis_index("y"), lax.axis_index("c")
    core = jnp.reshape(ci, (1,)).astype(jnp.int32)
    chip = jnp.reshape(2 * xi + yi, (1,)).astype(jnp.int32)
    dev = 4 * xi + 2 * yi + ci
    n_layers = 1 + max(int(n[1]) for n in wnames if n[0] == "l" and n[1].isdigit())
    big = [n for n in wnames if _base(n) in _COL + _ROW]
    aux = dict(dsa_tables=rope_tables_full(seq, LANES), mla_tables=mla_rope_tables(seq))

    sched = _Schedule()
    full = {}

    def deliver_weights(grp):
        def deliver(lands):
            for n, g in zip(grp, lands):
                if _base(n) in _RELAID:
                    full[n] = _aside(lambda: relayout_cols(g, n + "_whole", False))
                elif _base(n) in _COL:
                    full[n] = g
                elif _base(n) in _ROW:
                    full[n] = g.reshape(1, N_DEV * g.shape[1], g.shape[2])
                else:
                    full[n] = jnp.transpose(g, (1, 0, 2)).reshape(g.shape[1], N_DEV * g.shape[2])
        return deliver

    groups = []
    for layer in range(n_layers):
        lw = [n for n in wnames if n.startswith("l%d_" % layer) and _base(n) in _COL + _ROW + _SMALL_COL]
        groups += [[n for n in lw if "_ffn_" not in n], [n for n in lw if "_ffn_" in n]]

    def shards_of(grp):
        return [p[n] if _base(n) in _SMALL_COL else p[n].astype(BF16) for n in grp]

    deliver_weights(groups[0])(all_gather(shards_of(groups[0]), "ag_first"))
    for s in range(1, len(groups)):
        keys = [("f", s - 2 if s > 2 else s - 1), ("f", s - 1 if s > 2 else s), ("f", s)]
        _gather_overlapped(sched, keys, shards_of(groups[s]), dev, "ag_s%d" % s, deliver_weights(groups[s]))

    h = x.reshape(t, d)
    ffn_out = None
    ctxs = []
    for layer in range(n_layers):
        pre = "l%d_" % layer
        kind = _MIXERS[layer % len(_MIXERS)]
        if ffn_out is None:
            hn = rms_fwd(h, p[pre + "norm_mix"], pre + "mix_norm")
        else:
            h, hn = rms_fwd(h, p[pre + "norm_mix"], pre + "mix_norm", add=ffn_out)
        sched.run(("f", 2 * layer), [hn] + (list(full.values()) if layer == 0 else []))
        mix_out, mctx = _MIX_FWD[kind](pre, p, full, hn, bsz, seq, aux)
        sched.run(("f", 2 * layer + 1), mix_out)
        ffn_out, fctx = _ffn_fwd(pre, p, full, h, mix_out, bsz, seq)
        ctxs.append((h, hn, mctx, fctx))
        h = fctx["h"]

    loss_row, dh, dh_b, dfin = final_loss(h, ffn_out, p["final_norm"], p["loss_target"].reshape(t, d), "final_loss")
    dh = (dh, dh_b)
    loss = lax.psum(loss_row[0, 0], ("x", "y", "c"))

    small_g = {"final_norm": dfin.reshape(d)}
    results = {}

    def deliver_grads(lnames):
        def deliver(outs):
            for n, o in zip(lnames, outs):
                results[n] = o
        return deliver

    def reduce_later(q, lnames, big_g, tag):
        keys = [("b", q + 2), ("b", q + 3), ("b", q + 5)]
        _reduce_overlapped(sched, keys, [big_g[n] for n in lnames], [(p[n], p["m_" + n], p["v_" + n]) for n in lnames],
                           core, chip, tag, deliver_grads(lnames))

    q = 0
    for layer in reversed(range(n_layers)):
        pre = "l%d_" % layer
        kind = _MIXERS[layer % len(_MIXERS)]
        h_in, hn, mctx, fctx = ctxs[layer]
        big_g = {}
        sched.run(("b", q), dh[0])
        dh = _ffn_bwd(pre, p, full, fctx, dh, big_g, small_g, bsz, seq, lambda v, q=q: sched.run(("b", q + 1), v))
        reduce_later(q, [n for n in big if n.startswith(pre) and "_ffn_" in n], big_g, "rs_l%d_ffn" % layer)
        q += 2
        sched.run(("b", q), dh[0])
        dhn = _MIX_BWD[kind](pre, p, full, mctx, hn, dh, big_g, small_g, bsz, seq, aux,
                             lambda v, q=q: sched.run(("b", q + 1), v))
        dh_f, dh_b, dg = rms_bwd(h_in, p[pre + "norm_mix"], dhn, dh[0], pre + "mix_norm_bwd")
        dh = (dh_f, dh_b)
        small_g[pre + "norm_mix"] = dg.reshape(d)
        reduce_later(q, [n for n in big if n.startswith(pre) and "_ffn_" not in n], big_g, "rs_l%d_mix" % layer)
        q += 2
    for tail in range(q, q + 4):
        sched.run(("b", tail), dh[0])
    assert not sched.actions, sched.actions.keys()

    small = [n for n in wnames if n not in big]
    packed = _pack([small_g[n] for n in small]).reshape(N_DEV, -1, LANES)
    reduced = _reduce_scatter([packed], core, chip, "rs_small")[0]
    gathered = all_gather([reduced], "ag_small")[0].reshape(-1)
    g_local = []
    for n, g in zip(small, _unpack(gathered, [small_g[n] for n in small])):
        if _base(n) in _SMALL_COL:
            width = p[n].shape[1]
            g = lax.dynamic_slice_in_dim(g, dev * width, width, axis=1)
        g_local.append(g)
    stacked = [_pack(v).reshape(-1, LANES) for v in
               ([p[n] for n in small], g_local, [p["m_" + n] for n in small], [p["v_" + n] for n in small])]
    upd = adam_flat(*stacked, "adam_small")
    upd = [_unpack(u.reshape(-1), g_local) for u in upd]
    for i, n in enumerate(small):
        results[n] = (g_local[i], upd[0][i], upd[1][i], upd[2][i])

    grad_x = dh[0].reshape(bsz, seq, d)
    cols = [[results[n][k] for n in wnames] for k in range(4)]
    return (loss, grad_x, *cols[0], *cols[1], *cols[2], *cols[3])


def kernel(x, l0_norm_mix, l0_cc_w_in, l0_cc_b_in, l0_cc_dw_w, l0_cc_dw_b, l0_cc_ln_g, l0_cc_ln_b, l0_cc_w_out, l0_cc_b_out, l0_norm_ffn, l0_ffn_w_up, l0_ffn_dw_w, l0_ffn_dw_b, l0_ffn_w_down, l1_norm_mix, l1_dsa_w_qkv, l1_dsa_w_o, l1_norm_ffn, l1_ffn_w_up, l1_ffn_dw_w, l1_ffn_dw_b, l1_ffn_w_down, l2_norm_mix, l2_mla_w_in, l2_mla_q_norm, l2_mla_w_qb, l2_mla_kv_norm, l2_mla_w_kvb, l2_mla_w_o, l2_norm_ffn, l2_ffn_w_up, l2_ffn_dw_w, l2_ffn_dw_b, l2_ffn_w_down, l3_norm_mix, l3_sg_w_in, l3_sg_b_in, l3_sg_ln_g, l3_sg_ln_b, l3_sg_w_s, l3_sg_b_s, l3_sg_w_out, l3_sg_b_out, l3_norm_ffn, l3_ffn_w_up, l3_ffn_dw_w, l3_ffn_dw_b, l3_ffn_w_down, final_norm, loss_target, m_l0_norm_mix, m_l0_cc_w_in, m_l0_cc_b_in, m_l0_cc_dw_w, m_l0_cc_dw_b, m_l0_cc_ln_g, m_l0_cc_ln_b, m_l0_cc_w_out, m_l0_cc_b_out, m_l0_norm_ffn, m_l0_ffn_w_up, m_l0_ffn_dw_w, m_l0_ffn_dw_b, m_l0_ffn_w_down, m_l1_norm_mix, m_l1_dsa_w_qkv, m_l1_dsa_w_o, m_l1_norm_ffn, m_l1_ffn_w_up, m_l1_ffn_dw_w, m_l1_ffn_dw_b, m_l1_ffn_w_down, m_l2_norm_mix, m_l2_mla_w_in, m_l2_mla_q_norm, m_l2_mla_w_qb, m_l2_mla_kv_norm, m_l2_mla_w_kvb, m_l2_mla_w_o, m_l2_norm_ffn, m_l2_ffn_w_up, m_l2_ffn_dw_w, m_l2_ffn_dw_b, m_l2_ffn_w_down, m_l3_norm_mix, m_l3_sg_w_in, m_l3_sg_b_in, m_l3_sg_ln_g, m_l3_sg_ln_b, m_l3_sg_w_s, m_l3_sg_b_s, m_l3_sg_w_out, m_l3_sg_b_out, m_l3_norm_ffn, m_l3_ffn_w_up, m_l3_ffn_dw_w, m_l3_ffn_dw_b, m_l3_ffn_w_down, m_final_norm, v_l0_norm_mix, v_l0_cc_w_in, v_l0_cc_b_in, v_l0_cc_dw_w, v_l0_cc_dw_b, v_l0_cc_ln_g, v_l0_cc_ln_b, v_l0_cc_w_out, v_l0_cc_b_out, v_l0_norm_ffn, v_l0_ffn_w_up, v_l0_ffn_dw_w, v_l0_ffn_dw_b, v_l0_ffn_w_down, v_l1_norm_mix, v_l1_dsa_w_qkv, v_l1_dsa_w_o, v_l1_norm_ffn, v_l1_ffn_w_up, v_l1_ffn_dw_w, v_l1_ffn_dw_b, v_l1_ffn_w_down, v_l2_norm_mix, v_l2_mla_w_in, v_l2_mla_q_norm, v_l2_mla_w_qb, v_l2_mla_kv_norm, v_l2_mla_w_kvb, v_l2_mla_w_o, v_l2_norm_ffn, v_l2_ffn_w_up, v_l2_ffn_dw_w, v_l2_ffn_dw_b, v_l2_ffn_w_down, v_l3_norm_mix, v_l3_sg_w_in, v_l3_sg_b_in, v_l3_sg_ln_g, v_l3_sg_ln_b, v_l3_sg_w_s, v_l3_sg_b_s, v_l3_sg_w_out, v_l3_sg_b_out, v_l3_norm_ffn, v_l3_ffn_w_up, v_l3_ffn_dw_w, v_l3_ffn_dw_b, v_l3_ffn_w_down, v_final_norm):
    return _train_step(dict(locals()))
```

```python
import functools
import inspect
import math

import jax
import jax.numpy as jnp
from jax import lax
from jax.experimental import pallas as pl
from jax.experimental.pallas import tpu as pltpu

F32 = jnp.float32
BF16 = jnp.bfloat16
NORM_EPS = 1e-6
NEG_INF = -1e30
ROPE_THETA = 10000.0
LANES = 128
SUBLANES = 8
VMEM_LIMIT = 56 * 1024 * 1024
MM_VMEM_BUDGET = 44 * 1024 * 1024
N_DEV = 8

DSA_CONFIGS = ((128, 1), (512, 4), (2048, 16))
DSA_HEADS = 8
DSA_BLOCK = 128
MLA_HEADS = 16
MLA_RANK = 512
MLA_NOPE = 128
MLA_ROPE = 64
SG_CHUNK = 128
SG_GROUPS = 8
CC_WIDTH = 31
FFN_WIDTH = 3
ADAM_LR, ADAM_B1, ADAM_B2, ADAM_EPS, ADAM_WD, ADAM_STEP = 0.001, 0.9, 0.999, 1e-08, 0.01, 10


def _tile(n, cap, unit=LANES):
    if n <= cap:
        return n
    best = 0
    for t in range(unit, cap + 1, unit):
        if n % t == 0:
            best = t
    assert best, (n, cap, unit)
    return best


_PENDING = []


def _call(body, name, grid, in_specs, out_specs, out_shape, scratch=(), prefetch=0):
    params = pltpu.CompilerParams(vmem_limit_bytes=VMEM_LIMIT)
    deps = list(_PENDING)
    _PENDING.clear()
    if deps:
        inner, n_in = body, prefetch + len(in_specs)

        def body(*refs):
            return inner(*refs[:n_in], *refs[n_in + len(deps):])

        in_specs = list(in_specs) + [pl.BlockSpec(memory_space=pl.ANY)] * len(deps)
    if prefetch:
        spec = pltpu.PrefetchScalarGridSpec(num_scalar_prefetch=prefetch, grid=grid, in_specs=in_specs,
                                            out_specs=out_specs, scratch_shapes=list(scratch))
        fn = pl.pallas_call(body, out_shape=out_shape, grid_spec=spec, name=name, compiler_params=params)
    else:
        fn = pl.pallas_call(body, out_shape=out_shape, grid=grid, in_specs=in_specs, out_specs=out_specs,
                            scratch_shapes=list(scratch), name=name, compiler_params=params)
    return lambda *args: fn(*args, *deps)


def _sds(shape, dtype):
    return jax.ShapeDtypeStruct(tuple(shape), dtype)


def _bf(v):
    return v if v.dtype == BF16 else v.astype(BF16)


def _dot(a, b):
    return jnp.dot(a, b, preferred_element_type=F32)


def _dot_nt(a, b):
    return lax.dot_general(a, b, (((1,), (1,)), ((), ())), preferred_element_type=F32)


def _dot_tn(a, b):
    return lax.dot_general(a, b, (((0,), (0,)), ((), ())), preferred_element_type=F32)


def _sigmoid(v):
    return 1.0 / (1.0 + jnp.exp(-v))


_ERF_A = (-2.72614225801306e-10, 2.77068142495902e-08, -2.10102402082508e-06, -5.69250639462346e-05,
          -7.34990630326855e-04, -2.95459980854025e-03, -1.60960333262415e-02)
_ERF_B = (-1.45660718464996e-05, -2.13374055278905e-04, -1.68282697438203e-03, -7.37332916720468e-03,
          -1.42647390514189e-02)


def _erf(v):
    v = jnp.clip(v, -4.0, 4.0)
    v2 = v * v
    p = jnp.full_like(v, _ERF_A[0])
    for c in _ERF_A[1:]:
        p = p * v2 + c
    q = jnp.full_like(v, _ERF_B[0])
    for c in _ERF_B[1:]:
        q = q * v2 + c
    return v * p / q


def _gelu_cdf(v):
    return 0.5 * (1.0 + _erf(v * (2.0 ** -0.5)))


def _gelu_grad(v, cdf):
    return cdf + v * jnp.exp(-0.5 * v * v) * ((2.0 * math.pi) ** -0.5)


def _edge_rows(width):
    return -(-(width - 1) // SUBLANES) * SUBLANES


def _conv_causal(z, w_row, width):
    hr = _edge_rows(width)
    rows = lax.broadcasted_iota(jnp.int32, (hr, z.shape[1]), 0)
    out = z * w_row(width - 1)
    head = z[:hr] * w_row(width - 1)
    for j in range(1, width):
        tap = pltpu.roll(z, j, 0)
        out = out + tap * w_row(width - 1 - j)
        head = head + jnp.where(rows >= j, tap[:hr], 0.0) * w_row(width - 1 - j)
    return jnp.concatenate([head, out[hr:]], axis=0)


def _conv_causal_bwd(x, dc, w_row, width):
    n = dc.shape[0]
    hr = _edge_rows(width)
    rows = lax.broadcasted_iota(jnp.int32, (hr, dc.shape[1]), 0)
    x_tail, dc_tail = x[n - hr:], dc[n - hr:]
    dx = dc * w_row(width - 1)
    tail = dc_tail * w_row(width - 1)
    dws = {width - 1: jnp.sum(dc * x, axis=0, keepdims=True)}
    for j in range(1, width):
        up = pltpu.roll(dc, n - j, 0)
        up_tail = jnp.where(rows < hr - j, up[n - hr:], 0.0)
        dx = dx + up * w_row(width - 1 - j)
        tail = tail + up_tail * w_row(width - 1 - j)
        wrapped = jnp.sum((up[n - hr:] - up_tail) * x_tail, axis=0, keepdims=True)
        dws[width - 1 - j] = jnp.sum(up * x, axis=0, keepdims=True) - wrapped
    return jnp.concatenate([dx[:n - hr], tail], axis=0), dws, jnp.sum(dc, axis=0, keepdims=True)


def _rope(v, cos, sin):
    return v * cos + pltpu.roll(v, 64, 1) * sin


def _rope_t(dv, cos, sin):
    return dv * cos + pltpu.roll(dv * sin, 64, 1)


def mm_nn(a, w, name, bias=None, out_parts=1, out_dtype=F32, tm_cap=1024, tn_cap=1536, tk_cap=6144):
    pa, m, kp = a.shape
    j, k, n = w.shape
    assert pa * kp == k
    big_n = j * n
    npo = big_n // out_parts
    tm = _tile(m, tm_cap, SUBLANES)
    tk = _tile(kp, tk_cap)
    tn = _tile(math.gcd(n, npo), tn_cap if tk <= 2816 else 1024)
    out_bytes = jnp.dtype(out_dtype).itemsize

    def vmem_bytes(rows):
        return 2 * (rows * tk * 2 + tk * tn * 2 + rows * tn * out_bytes) + (rows * tn * 4 if k > tk else 0)

    while vmem_bytes(tm) > MM_VMEM_BUDGET and tm % (2 * SUBLANES) == 0:
        tm //= 2
    npj, nbo, kbp, nk = n // tn, npo // tn, kp // tk, k // tk
    has_bias = bias is not None

    def body(*refs):
        a_ref, w_ref = refs[0], refs[1]
        b_ref = refs[2] if has_bias else None
        o_ref = refs[2 + has_bias]

        def finish(acc):
            if has_bias:
                acc = acc + b_ref[...]
            o_ref[...] = acc.astype(o_ref.dtype)

        part = _dot(_bf(a_ref[...]), _bf(w_ref[...]))
        if nk == 1:
            finish(part)
        else:
            acc_ref = refs[3 + has_bias]
            ki = pl.program_id(2)

            @pl.when(ki == 0)
            def _():
                acc_ref[...] = part

            @pl.when(ki > 0)
            def _():
                acc_ref[...] += part

            @pl.when(ki == nk - 1)
            def _():
                finish(acc_ref[...])

    in_specs = [pl.BlockSpec((None, tm, tk), lambda ni, mi, ki: (ki // kbp, mi, ki % kbp)),
                pl.BlockSpec((None, tk, tn), lambda ni, mi, ki: (ni // npj, ki, ni % npj))]
    args = [a, w]
    if has_bias:
        in_specs.append(pl.BlockSpec((1, tn), lambda ni, mi, ki: (0, ni)))
        args.append(bias.reshape(1, big_n))
    out_spec = pl.BlockSpec((None, tm, tn), lambda ni, mi, ki: (ni // nbo, mi, ni % nbo))
    scratch = [pltpu.VMEM((tm, tn), F32)] if nk > 1 else []
    return _call(body, name, (big_n // tn, m // tm, nk), in_specs, out_spec,
                 _sds((out_parts, m, npo), out_dtype), scratch)(*args)


def mm_nt(a, w, name, out_dtype=F32, tm_cap=1024, tko_cap=2048, tc_cap=2048):
    pa, m, npa = a.shape
    j, k, n = w.shape
    assert pa * npa == j * n
    tm = _tile(m, tm_cap, SUBLANES)
    tko = _tile(k, tko_cap)
    tc = _tile(math.gcd(n, npa), tc_cap)
    out_bytes = jnp.dtype(out_dtype).itemsize

    def vmem_bytes(rows):
        return 2 * (rows * tc * 2 + tko * tc * 2 + rows * tko * out_bytes) + (rows * tko * 4 if j * n > tc else 0)

    while vmem_bytes(tm) > MM_VMEM_BUDGET and tm % (2 * SUBLANES) == 0:
        tm //= 2
    npj, nba, nc = n // tc, npa // tc, (j * n) // tc

    def body(a_ref, w_ref, o_ref, *scr):
        part = _dot_nt(_bf(a_ref[...]), _bf(w_ref[...]))
        if nc == 1:
            o_ref[...] = part.astype(o_ref.dtype)
        else:
            acc_ref = scr[0]
            ci = pl.program_id(2)

            @pl.when(ci == 0)
            def _():
                acc_ref[...] = part

            @pl.when(ci > 0)
            def _():
                acc_ref[...] += part

            @pl.when(ci == nc - 1)
            def _():
                o_ref[...] = acc_ref[...].astype(o_ref.dtype)

    in_specs = [pl.BlockSpec((None, tm, tc), lambda mi, ko, ci: (ci // nba, mi, ci % nba)),
                pl.BlockSpec((None, tko, tc), lambda mi, ko, ci: (ci // npj, ko, ci % npj))]
    out_spec = pl.BlockSpec((tm, tko), lambda mi, ko, ci: (mi, ko))
    scratch = [pltpu.VMEM((tm, tko), F32)] if nc > 1 else []
    return _call(body, name, (m // tm, k // tko, nc), in_specs, out_spec, _sds((m, k), out_dtype), scratch)(a, w)


def mm_tn(a, b, n_shards, name, out_dtype=BF16, tt_cap=4096, tkk_cap=512, tn_cap=1536):
    pa, m, kp = a.shape
    pb, m2, npb = b.shape
    assert m == m2
    k, big_n = pa * kp, pb * npb
    n = big_n // n_shards
    tt = _tile(m, tt_cap, SUBLANES)
    tkk = _tile(kp, tkk_cap)
    tn = _tile(math.gcd(n, npb), tn_cap)
    kbp, nbb, npj, nt = kp // tkk, npb // tn, n // tn, m // tt

    def body(a_ref, b_ref, o_ref, *scr):
        part = _dot_tn(_bf(a_ref[...]), _bf(b_ref[...]))
        if nt == 1:
            o_ref[...] = part.astype(o_ref.dtype)
        else:
            acc_ref = scr[0]
            ti = pl.program_id(2)

            @pl.when(ti == 0)
            def _():
                acc_ref[...] = part

            @pl.when(ti > 0)
            def _():
                acc_ref[...] += part

            @pl.when(ti == nt - 1)
            def _():
                o_ref[...] = acc_ref[...].astype(o_ref.dtype)

    in_specs = [pl.BlockSpec((None, tt, tkk), lambda ni, ki, ti: (ki // kbp, ti, ki % kbp)),
                pl.BlockSpec((None, tt, tn), lambda ni, ki, ti: (ni // nbb, ti, ni % nbb))]
    out_spec = pl.BlockSpec((None, tkk, tn), lambda ni, ki, ti: (ni // npj, ki, ni % npj))
    scratch = [pltpu.VMEM((tkk, tn), F32)] if nt > 1 else []
    return _call(body, name, (big_n // tn, k // tkk, nt), in_specs, out_spec,
                 _sds((n_shards, k, n), out_dtype), scratch)(a, b)


_ANY = pl.BlockSpec(memory_space=pl.ANY)
_MESH = pl.DeviceIdType.MESH


def all_gather(shards, name):
    n = len(shards)

    def body(*refs):
        ins, outs = refs[:n], refs[n:2 * n]
        send_sems, recv_sems, local_sems = refs[2 * n:]
        x, y, c = lax.axis_index("x"), lax.axis_index("y"), lax.axis_index("c")
        me, sibling = (x, y, c), (x, y, 1 - c)
        chips = [(1 - x, y), (x, 1 - y), (1 - x, 1 - y)]

        def copy(i, k, block, to, src=None):
            px, py, pc = block
            dst = outs[i].at[4 * px + 2 * py + pc]
            return pltpu.make_async_remote_copy(src_ref=dst if src is None else src, dst_ref=dst,
                                                send_sem=send_sems.at[i, k], recv_sem=recv_sems.at[i, k],
                                                device_id=to, device_id_type=_MESH)

        mine = [pltpu.make_async_copy(ins[i], outs[i].at[4 * x + 2 * y + c], local_sems.at[i]) for i in range(n)]
        for cp in mine:
            cp.start()
        first = []
        for i in range(n):
            first.append(copy(i, 0, me, sibling, src=ins[i]))
            first += [copy(i, 1 + q, me, (*chip, c), src=ins[i]) for q, chip in enumerate(chips)]
        for cp in first:
            cp.start()
        passed = []
        for q, chip in enumerate(chips):
            for i in range(n):
                copy(i, 1 + q, (*chip, c), me).wait_recv()
                cp = copy(i, 4 + q, (*chip, c), sibling)
                cp.start()
                passed.append(cp)
        for i in range(n):
            copy(i, 0, sibling, me).wait_recv()
            for q, chip in enumerate(chips):
                copy(i, 4 + q, (*chip, 1 - c), me).wait_recv()
        for cp in first + passed:
            cp.wait_send()
        for cp in mine:
            cp.wait()

    outs = pl.pallas_call(
        body, name=name, out_shape=[_sds((N_DEV,) + s.shape, s.dtype) for s in shards],
        in_specs=[_ANY] * n, out_specs=[_ANY] * n,
        scratch_shapes=[pltpu.SemaphoreType.DMA((n, 7)), pltpu.SemaphoreType.DMA((n, 7)),
                        pltpu.SemaphoreType.DMA((n,))],
    )(*shards)
    return list(outs)


def rs_core_exchange(parts, name, after=()):
    n, na = len(parts), len(after)

    def body(*refs):
        ins, outs = refs[:n], refs[n + na:2 * n + na]
        send_sems, recv_sems = refs[2 * n + na:]
        x, y, c = lax.axis_index("x"), lax.axis_index("y"), lax.axis_index("c")
        sibling = (x, y, 1 - c)
        started = []
        for i in range(n):
            for chip in range(4):
                cp = pltpu.make_async_remote_copy(src_ref=ins[i].at[2 * chip + (1 - c)], dst_ref=outs[i].at[chip],
                                                  send_sem=send_sems.at[i, chip], recv_sem=recv_sems.at[i, chip],
                                                  device_id=sibling, device_id_type=_MESH)
                cp.start()
                started.append(cp)
        for cp in started:
            cp.wait()

    outs = pl.pallas_call(
        body, name=name, out_shape=[_sds((4,) + p.shape[1:], p.dtype) for p in parts],
        in_specs=[_ANY] * (n + na), out_specs=[_ANY] * n,
        scratch_shapes=[pltpu.SemaphoreType.DMA((n, 4)), pltpu.SemaphoreType.DMA((n, 4))],
    )(*parts, *after)
    return list(outs)


def rs_chip_exchange(sums, name):
    n = len(sums)

    def body(*refs):
        ins, outs = refs[:n], refs[n:2 * n]
        send_sems, recv_sems = refs[2 * n:]
        x, y, c = lax.axis_index("x"), lax.axis_index("y"), lax.axis_index("c")
        peers = [(x, 1 - y), (1 - x, y), (1 - x, 1 - y)]
        started = []
        for i in range(n):
            for k, (px, py) in enumerate(peers):
                cp = pltpu.make_async_remote_copy(src_ref=ins[i].at[2 * px + py], dst_ref=outs[i].at[k],
                                                  send_sem=send_sems.at[i, k], recv_sem=recv_sems.at[i, k],
                                                  device_id=(px, py, c), device_id_type=_MESH)
                cp.start()
                started.append(cp)
        for cp in started:
            cp.wait()

    outs = pl.pallas_call(
        body, name=name, out_shape=[_sds((3,) + s.shape[1:], s.dtype) for s in sums],
        in_specs=[_ANY] * n, out_specs=[_ANY] * n,
        scratch_shapes=[pltpu.SemaphoreType.DMA((n, 3)), pltpu.SemaphoreType.DMA((n, 3))],
    )(*sums)
    return list(outs)


_HBM = pl.BlockSpec(memory_space=pltpu.HBM)
_SEM = pl.BlockSpec(memory_space=pltpu.SEMAPHORE)
_EFFECT = pltpu.SideEffectType.DATAFLOW_SIDE_EFFECTING


def _in_hbm(a):
    return pltpu.with_memory_space_constraint(a, pltpu.HBM)


def _plan_copies(plan, srcs, dsts, send_sems, recv_sems):
    x, y, c = lax.axis_index("x"), lax.axis_index("y"), lax.axis_index("c")
    return [pltpu.make_async_remote_copy(src_ref=s, dst_ref=d, send_sem=send_sems.at[q], recv_sem=recv_sems.at[q],
                                         device_id=to, device_id_type=_MESH)
            for q, (s, d, to) in enumerate(plan(x, y, c, srcs, dsts))]


def exchange_start(plan, n_copies, srcs, dsts, after, name):
    ns, nd, na = len(srcs), len(dsts), len(after)

    def body(*refs):
        ins = refs[:ns + nd + na]
        send_sems, recv_sems = refs[ns + nd + na], refs[ns + nd + na + 1]
        token = refs[-1]
        for cp in _plan_copies(plan, ins[:ns], ins[ns:ns + nd], send_sems, recv_sems):
            cp.start()
        token[...] = jnp.zeros_like(token)

    thru = [pltpu.HBM(a.shape, a.dtype) for a in list(srcs) + list(dsts)]
    out = pl.pallas_call(
        body, name=name,
        out_shape=[pltpu.SemaphoreType.DMA((n_copies,)), pltpu.SemaphoreType.DMA((n_copies,))] + thru
        + [_sds((SUBLANES, LANES), F32)],
        in_specs=[_HBM] * (ns + nd) + [_ANY] * na,
        out_specs=[_SEM, _SEM] + [_HBM] * (ns + nd) + [pl.BlockSpec(memory_space=pltpu.VMEM)],
        input_output_aliases={i: 2 + i for i in range(ns + nd)},
        compiler_params=pltpu.CompilerParams(has_side_effects=_EFFECT),
    )(*[_in_hbm(a) for a in list(srcs) + list(dsts)], *after)
    return out[0], out[1], list(out[2:2 + ns]), list(out[2 + ns:2 + ns + nd]), out[-1]


def exchange_wait(plan, started, after, name):
    send_sems, recv_sems, srcs, dsts, _ = started
    ns, nd, na = len(srcs), len(dsts), len(after)

    def body(*refs):
        ins = refs[:ns + nd]
        send_sems_ref, recv_sems_ref = refs[ns + nd], refs[ns + nd + 1]
        for cp in _plan_copies(plan, ins[:ns], ins[ns:ns + nd], send_sems_ref, recv_sems_ref):
            cp.wait_send()
            cp.wait_recv()

    thru = [pltpu.HBM(a.shape, a.dtype) for a in list(srcs) + list(dsts)]
    out = pl.pallas_call(
        body, name=name, out_shape=thru,
        in_specs=[_HBM] * (ns + nd) + [_SEM, _SEM] + [_ANY] * na,
        out_specs=[_HBM] * (ns + nd),
        input_output_aliases={i: i for i in range(ns + nd)},
        compiler_params=pltpu.CompilerParams(has_side_effects=_EFFECT),
    )(*srcs, *dsts, send_sems, recv_sems, *after)
    return list(out[:ns]), list(out[ns:])


def _dev_index(px, py, pc):
    return 4 * px + 2 * py + pc


def _ag_plan_first(x, y, c, srcs, dsts):
    me = _dev_index(x, y, c)
    peers = [(x, y, 1 - c), (1 - x, y, c), (x, 1 - y, c), (1 - x, 1 - y, c)]
    return [(s, d.at[me], to) for s, d in zip(srcs, dsts) for to in peers]


def _ag_plan_pass(x, y, c, srcs, dsts):
    chips = [(1 - x, y), (x, 1 - y), (1 - x, 1 - y)]
    return [(d.at[_dev_index(px, py, c)], d.at[_dev_index(px, py, c)], (x, y, 1 - c)) for d in dsts for px, py in chips]


def _rs_plan_cores(x, y, c, srcs, dsts):
    return [(s.at[2 * chip + (1 - c)], d.at[chip], (x, y, 1 - c)) for s, d in zip(srcs, dsts) for chip in range(4)]


def _rs_plan_chips(x, y, c, srcs, dsts):
    peers = [(x, 1 - y), (1 - x, y), (1 - x, 1 - y)]
    return [(s.at[2 * px + py], d.at[k], (px, py, c)) for s, d in zip(srcs, dsts) for k, (px, py) in enumerate(peers)]


def _row_tile(r, c, itemsize=4, budget=1 << 20):
    cap = max(SUBLANES, (budget // (c * itemsize)) // SUBLANES * SUBLANES)
    if r <= cap:
        return r
    best = 0
    for t in range(SUBLANES, cap + 1, SUBLANES):
        if r % t == 0:
            best = t
    return best if best else r


def rs_core_add(part, recv, core, name):
    _, r, c = part.shape
    tr = _row_tile(r, c, budget=1 << 22)

    def body(core_ref, p_ref, q_ref, o_ref):
        o_ref[...] = (p_ref[...].astype(F32) + q_ref[...].astype(F32)).astype(o_ref.dtype)

    in_specs = [pl.BlockSpec((None, tr, c), lambda ch, ri, core_ref: (2 * ch + core_ref[0], ri, 0)),
                pl.BlockSpec((None, tr, c), lambda ch, ri, core_ref: (ch, ri, 0))]
    out_spec = pl.BlockSpec((None, tr, c), lambda ch, ri, core_ref: (ch, ri, 0))
    return _call(body, name, (4, r // tr), in_specs, out_spec, _sds((4, r, c), part.dtype), prefetch=1)(core, part, recv)


def _adamw(w, g, m, v):
    m = ADAM_B1 * m + (1.0 - ADAM_B1) * g
    v = ADAM_B2 * v + (1.0 - ADAM_B2) * (g * g)
    m_hat = m / (1.0 - ADAM_B1 ** ADAM_STEP)
    v_hat = v / (1.0 - ADAM_B2 ** ADAM_STEP)
    delta = -ADAM_LR * (m_hat / (jnp.sqrt(v_hat) + ADAM_EPS) + ADAM_WD * w)
    return delta, m, v


def rs_finish(sums, recv, chip, name, adam=None):
    _, r, c = sums.shape
    tr = _row_tile(r, c, budget=1 << 20)

    def body(chip_ref, s_ref, q_ref, *refs):
        g = s_ref[...].astype(F32)
        for k in range(3):
            g = g + q_ref[k].astype(F32)
        if adam is None:
            refs[0][...] = g
        else:
            w_ref, m_ref, v_ref, g_ref, d_ref, nm_ref, nv_ref = refs
            g_ref[...] = g
            d_ref[...], nm_ref[...], nv_ref[...] = _adamw(w_ref[...], g, m_ref[...], v_ref[...])

    blk = pl.BlockSpec((tr, c), lambda ri, chip_ref: (ri, 0))
    in_specs = [pl.BlockSpec((None, tr, c), lambda ri, chip_ref: (chip_ref[0], ri, 0)),
                pl.BlockSpec((3, tr, c), lambda ri, chip_ref: (0, ri, 0))]
    args = [chip, sums, recv]
    if adam is None:
        out_specs, out_shape = blk, _sds((r, c), F32)
    else:
        in_specs += [blk] * 3
        args += list(adam)
        out_specs, out_shape = [blk] * 4, [_sds((r, c), F32)] * 4
    return _call(body, name, (r // tr,), in_specs, out_specs, out_shape, prefetch=1)(*args)


def adam_flat(w, g, m, v, name):
    r, c = w.shape
    tr = _row_tile(r, c, budget=1 << 19)

    def body(w_ref, g_ref, m_ref, v_ref, d_ref, nm_ref, nv_ref):
        d_ref[...], nm_ref[...], nv_ref[...] = _adamw(w_ref[...], g_ref[...], m_ref[...], v_ref[...])

    blk = pl.BlockSpec((tr, c), lambda ri: (ri, 0))
    return _call(body, name, (r // tr,), [blk] * 4, [blk] * 3, [_sds((r, c), F32)] * 3)(w, g, m, v)


def rms_fwd(x, g, name, add=None):
    t, d = x.shape
    tm = _tile(t, 256, SUBLANES)
    has_add = add is not None

    def body(*refs):
        if has_add:
            x_ref, a_ref, g_ref, h_ref, hn_ref = refs
            h = x_ref[...] + a_ref[...]
            h_ref[...] = h
        else:
            x_ref, g_ref, hn_ref = refs
            h = x_ref[...]
        r = lax.rsqrt(jnp.mean(h * h, axis=-1, keepdims=True) + NORM_EPS)
        hn_ref[...] = (h * r * g_ref[...]).astype(BF16)

    row = pl.BlockSpec((tm, d), lambda i: (i, 0))
    vec = pl.BlockSpec((1, d), lambda i: (0, 0))
    if has_add:
        return _call(body, name, (t // tm,), [row, row, vec], [row, row], [_sds((t, d), F32), _sds((t, d), BF16)])(
            x, add, g.reshape(1, d))
    return _call(body, name, (t // tm,), [row, vec], row, _sds((t, d), BF16))(x, g.reshape(1, d))


def _rms_bwd_math(x, g, dy):
    r = lax.rsqrt(jnp.mean(x * x, axis=-1, keepdims=True) + NORM_EPS)
    xh = x * r
    dyg = dy * g
    dx = r * (dyg - xh * jnp.mean(dyg * xh, axis=-1, keepdims=True))
    return dx, jnp.sum(dy * xh, axis=0, keepdims=True)


def rms_bwd(x, g, dhn, dres, name):
    t, d = x.shape
    tm = _tile(t, 256, SUBLANES)

    def body(x_ref, g_ref, dy_ref, dr_ref, dx_ref, dxb_ref, dg_ref):
        dx, dg = _rms_bwd_math(x_ref[...], g_ref[...], dy_ref[...])
        dx = dx + dr_ref[...]
        dx_ref[...] = dx
        dxb_ref[...] = dx.astype(BF16)

        @pl.when(pl.program_id(0) == 0)
        def _():
            dg_ref[...] = jnp.zeros_like(dg_ref)

        dg_ref[...] += dg

    row = pl.BlockSpec((tm, d), lambda i: (i, 0))
    vec = pl.BlockSpec((1, d), lambda i: (0, 0))
    return _call(body, name, (t // tm,), [row, vec, row, row], [row, row, vec],
                 [_sds((t, d), F32), _sds((t, d), BF16), _sds((1, d), F32)])(x, g.reshape(1, d), dhn, dres)


def final_loss(h, add, g, target, name):
    t, d = h.shape
    tm = _tile(t, 256, SUBLANES)

    def body(x_ref, a_ref, g_ref, t_ref, loss_ref, dx_ref, dxb_ref, dg_ref):
        x = x_ref[...] + a_ref[...]
        gain = g_ref[...]
        r = lax.rsqrt(jnp.mean(x * x, axis=-1, keepdims=True) + NORM_EPS)
        err = x * r * gain - t_ref[...]
        dx, dg = _rms_bwd_math(x, gain, err * (1.0 / d))
        dx_ref[...] = dx
        dxb_ref[...] = dx.astype(BF16)

        @pl.when(pl.program_id(0) == 0)
        def _():
            dg_ref[...] = jnp.zeros_like(dg_ref)
            loss_ref[...] = jnp.zeros_like(loss_ref)

        dg_ref[...] += dg
        part = jnp.sum(jnp.sum(err * err, axis=-1, keepdims=True), axis=0, keepdims=True) * (0.5 / d)
        loss_ref[...] += jnp.broadcast_to(part, loss_ref.shape)

    row = pl.BlockSpec((tm, d), lambda i: (i, 0))
    vec = pl.BlockSpec((1, d), lambda i: (0, 0))
    one = pl.BlockSpec((1, LANES), lambda i: (0, 0))
    return _call(body, name, (t // tm,), [row, row, vec, row], [one, row, row, vec],
                 [_sds((1, LANES), F32), _sds((t, d), F32), _sds((t, d), BF16), _sds((1, d), F32)])(
        h, add, g.reshape(1, d), target)


def ffn_gate_fwd(z2, dw_w2, dw_b2, name, tc=2 * LANES):
    _, b, s, f = z2.shape

    def body(z_ref, w_ref, b_ref, y_ref):
        g = _conv_causal(z_ref[0], lambda k: w_ref[0, k:k + 1, :], FFN_WIDTH) + b_ref[0]
        a = _conv_causal(z_ref[1], lambda k: w_ref[1, k:k + 1, :], FFN_WIDTH) + b_ref[1]
        y_ref[...] = (g * _sigmoid(g) * a).astype(BF16)

    in_specs = [pl.BlockSpec((2, None, s, tc), lambda bi, ci: (0, bi, 0, ci)),
                pl.BlockSpec((2, FFN_WIDTH, tc), lambda bi, ci: (0, 0, ci)),
                pl.BlockSpec((2, 1, tc), lambda bi, ci: (0, 0, ci))]
    out_spec = pl.BlockSpec((None, s, tc), lambda bi, ci: (bi, 0, ci))
    return _call(body, name, (b, f // tc), in_specs, out_spec, _sds((b, s, f), BF16))(z2, dw_w2, dw_b2)


def ffn_gate_bwd(z2, dy, dw_w2, dw_b2, name, tc=2 * LANES):
    _, b, s, f = z2.shape

    def body(z_ref, dy_ref, w_ref, b_ref, dz_ref, dw_ref, db_ref):
        @pl.when(pl.program_id(1) == 0)
        def _():
            dw_ref[...] = jnp.zeros_like(dw_ref)
            db_ref[...] = jnp.zeros_like(db_ref)

        zs = (z_ref[0], z_ref[1])
        g = _conv_causal(zs[0], lambda k: w_ref[0, k:k + 1, :], FFN_WIDTH) + b_ref[0]
        a = _conv_causal(zs[1], lambda k: w_ref[1, k:k + 1, :], FFN_WIDTH) + b_ref[1]
        sg = _sigmoid(g)
        dy = dy_ref[...]
        dcs = (dy * a * (sg * (1.0 + g * (1.0 - sg))), dy * (g * sg))
        for p in range(2):
            dz, dws, db = _conv_causal_bwd(zs[p], dcs[p], lambda k, p=p: w_ref[p, k:k + 1, :], FFN_WIDTH)
            dz_ref[p] = dz.astype(BF16)
            for k in range(FFN_WIDTH):
                dw_ref[p, k:k + 1, :] += dws[k]
            db_ref[p] += db

    in_specs = [pl.BlockSpec((2, None, s, tc), lambda ci, bi: (0, bi, 0, ci)),
                pl.BlockSpec((None, s, tc), lambda ci, bi: (bi, 0, ci)),
                pl.BlockSpec((2, FFN_WIDTH, tc), lambda ci, bi: (0, 0, ci)),
                pl.BlockSpec((2, 1, tc), lambda ci, bi: (0, 0, ci))]
    out_specs = [pl.BlockSpec((2, None, s, tc), lambda ci, bi: (0, bi, 0, ci)),
                 pl.BlockSpec((2, FFN_WIDTH, tc), lambda ci, bi: (0, 0, ci)),
                 pl.BlockSpec((2, 1, tc), lambda ci, bi: (0, 0, ci))]
    out_shape = [_sds((2, b, s, f), BF16), _sds((2, FFN_WIDTH, f), F32), _sds((2, 1, f), F32)]
    return _call(body, name, (f // tc, b), in_specs, out_specs, out_shape)(z2, dy, dw_w2, dw_b2)


def cc_conv_fwd(z2, dw_w, dw_b, name, tc=2 * LANES):
    _, b, s, c = z2.shape

    def body(z_ref, w_ref, b_ref, o_ref):
        u = z_ref[0] * _sigmoid(z_ref[1])
        o_ref[...] = _conv_causal(u, lambda k: w_ref[k:k + 1, :], CC_WIDTH) + b_ref[...]

    in_specs = [pl.BlockSpec((2, None, s, tc), lambda bi, ci: (0, bi, 0, ci)),
                pl.BlockSpec((CC_WIDTH, tc), lambda bi, ci: (0, ci)),
                pl.BlockSpec((1, tc), lambda bi, ci: (0, ci))]
    out_spec = pl.BlockSpec((None, s, tc), lambda bi, ci: (bi, 0, ci))
    return _call(body, name, (b, c // tc), in_specs, out_spec, _sds((b, s, c), F32))(z2, dw_w, dw_b.reshape(1, c))


def cc_conv_bwd(z2, dcv, dw_w, name, tc=2 * LANES):
    _, b, s, c = z2.shape

    def body(z_ref, dc_ref, w_ref, dz_ref, dbi_ref, dw_ref, db_ref):
        @pl.when(pl.program_id(1) == 0)
        def _():
            dbi_ref[...] = jnp.zeros_like(dbi_ref)
            dw_ref[...] = jnp.zeros_like(dw_ref)
            db_ref[...] = jnp.zeros_like(db_ref)

        a, gate = z_ref[0], z_ref[1]
        sg = _sigmoid(gate)
        u = a * sg
        du, dws, db = _conv_causal_bwd(u, dc_ref[...], lambda k: w_ref[k:k + 1, :], CC_WIDTH)
        for k in range(CC_WIDTH):
            dw_ref[k:k + 1, :] += dws[k]
        db_ref[...] += db
        da = du * sg
        dg = du * a * sg * (1.0 - sg)
        dz_ref[0] = da.astype(BF16)
        dz_ref[1] = dg.astype(BF16)
        dbi_ref[0] += jnp.sum(da, axis=0, keepdims=True)
        dbi_ref[1] += jnp.sum(dg, axis=0, keepdims=True)

    in_specs = [pl.BlockSpec((2, None, s, tc), lambda ci, bi: (0, bi, 0, ci)),
                pl.BlockSpec((None, s, tc), lambda ci, bi: (bi, 0, ci)),
                pl.BlockSpec((CC_WIDTH, tc), lambda ci, bi: (0, ci))]
    out_specs = [pl.BlockSpec((2, None, s, tc), lambda ci, bi: (0, bi, 0, ci)),
                 pl.BlockSpec((2, 1, tc), lambda ci, bi: (0, 0, ci)),
                 pl.BlockSpec((CC_WIDTH, tc), lambda ci, bi: (0, ci)),
                 pl.BlockSpec((1, tc), lambda ci, bi: (0, ci))]
    out_shape = [_sds((2, b, s, c), BF16), _sds((2, 1, c), F32), _sds((CC_WIDTH, c), F32), _sds((1, c), F32)]
    return _call(body, name, (c // tc, b), in_specs, out_specs, out_shape)(z2, dcv, dw_w)


def _ln_stats(v):
    mu = jnp.mean(v, axis=-1, keepdims=True)
    vc = v - mu
    r = lax.rsqrt(jnp.mean(vc * vc, axis=-1, keepdims=True) + NORM_EPS)
    return vc * r, r


def _ln_bwd(dln, xh, r, g):
    dxh = dln * g
    dx = r * (dxh - jnp.mean(dxh, axis=-1, keepdims=True) - xh * jnp.mean(dxh * xh, axis=-1, keepdims=True))
    return dx, jnp.sum(dln * xh, axis=0, keepdims=True), jnp.sum(dln, axis=0, keepdims=True)


def cc_ln_fwd(cv, ln_g, ln_b, name):
    t, c = cv.shape
    tm = _tile(t, 256, SUBLANES)

    def body(x_ref, g_ref, b_ref, o_ref):
        xh, _ = _ln_stats(x_ref[...])
        ln = xh * g_ref[...] + b_ref[...]
        o_ref[...] = (ln * _sigmoid(ln)).astype(BF16)

    row = pl.BlockSpec((tm, c), lambda i: (i, 0))
    vec = pl.BlockSpec((1, c), lambda i: (0, 0))
    return _call(body, name, (t // tm,), [row, vec, vec], row, _sds((t, c), BF16))(
        cv, ln_g.reshape(1, c), ln_b.reshape(1, c))


def cc_ln_bwd(cv, ln_g, ln_b, ds, name):
    t, c = cv.shape
    tm = _tile(t, 256, SUBLANES)

    def body(x_ref, g_ref, b_ref, ds_ref, dx_ref, dg_ref, db_ref):
        @pl.when(pl.program_id(0) == 0)
        def _():
            dg_ref[...] = jnp.zeros_like(dg_ref)
            db_ref[...] = jnp.zeros_like(db_ref)

        xh, r = _ln_stats(x_ref[...])
        ln = xh * g_ref[...] + b_ref[...]
        sg = _sigmoid(ln)
        dln = ds_ref[...] * (sg * (1.0 + ln * (1.0 - sg)))
        dx, dg, db = _ln_bwd(dln, xh, r, g_ref[...])
        dx_ref[...] = dx
        dg_ref[...] += dg
        db_ref[...] += db

    row = pl.BlockSpec((tm, c), lambda i: (i, 0))
    vec = pl.BlockSpec((1, c), lambda i: (0, 0))
    return _call(body, name, (t // tm,), [row, vec, vec, row], [row, vec, vec],
                 [_sds((t, c), F32), _sds((1, c), F32), _sds((1, c), F32)])(
        cv, ln_g.reshape(1, c), ln_b.reshape(1, c), ds)


def col_sum(v, name):
    t, n = v.shape
    tm = _tile(t, 256, SUBLANES)

    def body(v_ref, o_ref):
        @pl.when(pl.program_id(0) == 0)
        def _():
            o_ref[...] = jnp.zeros_like(o_ref)

        o_ref[...] += jnp.sum(v_ref[...], axis=0, keepdims=True)

    return _call(body, name, (t // tm,), [pl.BlockSpec((tm, n), lambda i: (i, 0))],
                 pl.BlockSpec((1, n), lambda i: (0, 0)), _sds((1, n), F32))(v)


def rope_tables_full(seq, dim):
    pos = jnp.arange(seq, dtype=F32)
    inv = ROPE_THETA ** (-(jnp.arange(0, dim, 2, dtype=F32) / dim))
    ang = pos[:, None] * inv[None, :]
    cos, sin = jnp.cos(ang), jnp.sin(ang)
    return jnp.concatenate([cos, cos], axis=-1), jnp.concatenate([-sin, sin], axis=-1)


def dsa_rope_fwd(qkv, cos, sin, seq, name):
    t, w = qkv.shape
    tm = _tile(seq, 256, SUBLANES)
    nsb = seq // tm
    hb = DSA_HEADS

    def body(x_ref, c_ref, s_ref, o_ref):
        cos_t, sin_t = c_ref[...], s_ref[...]
        for blk in range(w // LANES):
            v = x_ref[:, blk * LANES:(blk + 1) * LANES]
            if (blk // hb) % 3 < 2:
                v = _rope(v, cos_t, sin_t)
            o_ref[:, blk * LANES:(blk + 1) * LANES] = v

    row = pl.BlockSpec((tm, w), lambda i: (i, 0))
    tab = pl.BlockSpec((tm, LANES), lambda i: (i % nsb, 0))
    return _call(body, name, (t // tm,), [row, tab, tab], row, _sds((t, w), F32))(qkv, cos, sin)


def _dsa_rows(r, n, dil):
    start = r + n * DSA_BLOCK * dil
    return pl.ds(start, DSA_BLOCK) if dil == 1 else pl.ds(start, DSA_BLOCK, stride=dil)


def _dsa_load(ref, r, blocks, dil):
    parts = [ref[_dsa_rows(r, m, dil), :] for m in blocks]
    return parts[0] if len(parts) == 1 else jnp.concatenate(parts, axis=0)


def dsa_attn_all_fwd(qkv_r, batch, seq, name):
    t, w = qkv_r.shape
    h_n, n_g = DSA_HEADS, len(DSA_CONFIGS)
    scale = DSA_BLOCK ** -0.5
    chunk = _tile(seq, 256, SUBLANES)

    def body(q_ref, k_ref, v_ref, o_ref, ob_ref, lse_ref, og_ref, lg_ref):
        g = pl.program_id(2)
        for gi, (_, dil) in enumerate(DSA_CONFIGS):
            @pl.when(g == gi)
            def _(gi=gi, dil=dil):
                for r in range(dil):
                    for n in range(seq // dil // DSA_BLOCK):
                        blocks = (n - 1, n) if n else (n,)
                        rows = _dsa_rows(r, n, dil)
                        q = q_ref[rows, :].astype(BF16)
                        k = _dsa_load(k_ref, r, blocks, dil).astype(BF16)
                        v = _dsa_load(v_ref, r, blocks, dil).astype(BF16)
                        s = jnp.where(_dsa_mask(n), _dot_nt(q, k) * scale, NEG_INF)
                        m = jnp.max(s, axis=-1, keepdims=True)
                        p = jnp.exp(s - m)
                        l = jnp.sum(p, axis=-1, keepdims=True)
                        og_ref[gi, rows, :] = _dot(p.astype(BF16), v) / l
                        lg_ref[gi, rows, :] = jnp.broadcast_to(m + jnp.log(l), (DSA_BLOCK, LANES))

        @pl.when(g == n_g - 1)
        def _():
            for c0 in range(0, seq, chunk):
                sl = slice(c0, c0 + chunk)
                ls = [lg_ref[gi, sl, :] for gi in range(n_g)]
                m = functools.reduce(jnp.maximum, ls)
                es = [jnp.exp(v - m) for v in ls]
                tot = functools.reduce(lambda a, b: a + b, es)
                acc = jnp.zeros_like(m)
                for gi in range(n_g):
                    acc = acc + (es[gi] / tot) * og_ref[gi, sl, :]
                o_ref[sl, :] = acc
                ob_ref[sl, :] = acc.astype(BF16)
                lse_ref[sl, :] = m + jnp.log(tot)

    def spec(off):
        return pl.BlockSpec((None, seq, LANES), lambda b, h, g: (b, 0, (3 * g + off) * h_n + h))

    out_spec = pl.BlockSpec((None, seq, LANES), lambda b, h, g: (b, 0, h))
    view = qkv_r.reshape(batch, seq, w)
    shape = (batch, seq, h_n * LANES)
    o, o_b, lse = _call(body, name, (batch, h_n, n_g), [spec(0), spec(1), spec(2)], [out_spec] * 3,
                        [_sds(shape, F32), _sds(shape, BF16), _sds(shape, F32)],
                        scratch=[pltpu.VMEM((n_g, seq, LANES), F32)] * 2)(view, view, view)
    return o.reshape(t, -1), o_b.reshape(t, -1), lse.reshape(t, -1)


def dsa_attn_all_bwd(qkv_r, do, lse, delta, batch, seq, name):
    t, w = qkv_r.shape
    h_n, n_g = DSA_HEADS, len(DSA_CONFIGS)
    scale = DSA_BLOCK ** -0.5

    def body(q_ref, k_ref, v_ref, do_ref, lse_ref, dl_ref, d_ref):
        g = pl.program_id(2)
        d_ref[...] = jnp.zeros_like(d_ref)
        for gi, (_, dil) in enumerate(DSA_CONFIGS):
            @pl.when(g == gi)
            def _(dil=dil):
                for r in range(dil):
                    for n in range(seq // dil // DSA_BLOCK):
                        blocks = (n - 1, n) if n else (n,)
                        rows = _dsa_rows(r, n, dil)
                        q = q_ref[rows, :].astype(BF16)
                        k = _dsa_load(k_ref, r, blocks, dil).astype(BF16)
                        v = _dsa_load(v_ref, r, blocks, dil).astype(BF16)
                        dout = do_ref[rows, :].astype(BF16)
                        s = jnp.where(_dsa_mask(n), _dot_nt(q, k) * scale, NEG_INF)
                        p = jnp.exp(s - lse_ref[rows, :][:, 0:1])
                        ds = (p * (_dot_nt(dout, v) - dl_ref[rows, :][:, 0:1]) * scale).astype(BF16)
                        d_ref[0, rows, :] = _dot(ds, k)
                        dk = _dot_tn(ds, q)
                        dv = _dot_tn(p.astype(BF16), dout)
                        for i, blk in enumerate(blocks):
                            brows = _dsa_rows(r, blk, dil)
                            d_ref[1, brows, :] += dk[i * DSA_BLOCK:(i + 1) * DSA_BLOCK]
                            d_ref[2, brows, :] += dv[i * DSA_BLOCK:(i + 1) * DSA_BLOCK]

    def spec(off):
        return pl.BlockSpec((None, seq, LANES), lambda b, h, g: (b, 0, (3 * g + off) * h_n + h))

    head = pl.BlockSpec((None, seq, LANES), lambda b, h, g: (b, 0, h))
    out_spec = pl.BlockSpec((3, None, seq, LANES), lambda b, h, g: (g, b, 0, h))
    view = qkv_r.reshape(batch, seq, w)
    hv = [a.reshape(batch, seq, h_n * LANES) for a in (do, lse, delta)]
    out = _call(body, name, (batch, h_n, n_g), [spec(0), spec(1), spec(2), head, head, head], out_spec,
                _sds((3 * n_g, batch, seq, h_n * LANES), F32))(view, view, view, *hv)
    return out.reshape(3 * n_g, t, h_n * LANES)


def dsa_rope_bwd_all(grads, cos, sin, seq, name):
    n_parts, t, hw = grads.shape
    w = hw * n_parts
    tm = _tile(seq, 128, SUBLANES)
    nsb = seq // tm

    def body(g_ref, c_ref, s_ref, o_ref):
        cos_t, sin_t = c_ref[...], s_ref[...]
        for idx in range(n_parts):
            for h in range(hw // LANES):
                v = g_ref[idx, :, h * LANES:(h + 1) * LANES]
                if idx % 3 < 2:
                    v = _rope_t(v, cos_t, sin_t)
                col = idx * hw + h * LANES
                o_ref[:, col:col + LANES] = v.astype(BF16)

    tab = pl.BlockSpec((tm, LANES), lambda i: (i % nsb, 0))
    out = _call(body, name, (t // tm,), [pl.BlockSpec((n_parts, tm, hw), lambda i: (0, i, 0)), tab, tab],
                pl.BlockSpec((tm, w), lambda i: (i, 0)), _sds((t, w), BF16))(grads, cos, sin)
    return out[None]


def _dsa_mask(n):
    qi = lax.broadcasted_iota(jnp.int32, (DSA_BLOCK, (1 if n == 0 else 2) * DSA_BLOCK), 0)
    kj = lax.broadcasted_iota(jnp.int32, (DSA_BLOCK, (1 if n == 0 else 2) * DSA_BLOCK), 1)
    if n == 0:
        return qi >= kj
    dist = DSA_BLOCK + qi - kj
    return (dist >= 0) & (dist <= DSA_BLOCK)


def head_delta(do, o, name):
    t, hw = do.shape
    tm = _tile(t, 256, SUBLANES)

    def body(do_ref, o_ref, d_ref):
        for h in range(hw // LANES):
            sl = slice(h * LANES, (h + 1) * LANES)
            d = jnp.sum(do_ref[:, sl] * o_ref[:, sl], axis=-1, keepdims=True)
            d_ref[:, sl] = jnp.broadcast_to(d, (tm, LANES))

    row = pl.BlockSpec((tm, hw), lambda i: (i, 0))
    return _call(body, name, (t // tm,), [row, row], row, _sds((t, hw), F32))(do, o)


def mla_rope_tables(seq):
    pos = jnp.arange(seq, dtype=F32)
    inv = ROPE_THETA ** (-(jnp.arange(0, MLA_ROPE, 2, dtype=F32) / MLA_ROPE))
    ang = pos[:, None] * inv[None, :]
    cos, sin, zero = jnp.cos(ang), jnp.sin(ang), jnp.zeros_like(ang)
    return jnp.concatenate([cos, zero, cos, zero], axis=-1), jnp.concatenate([-sin, zero, sin, zero], axis=-1)


def _spread_rope_cols(w_pe):
    half = MLA_ROPE // 2
    zero = jnp.zeros(w_pe.shape[:-1] + (half,), w_pe.dtype)
    return jnp.concatenate([w_pe[..., :half], zero, w_pe[..., half:], zero], axis=-1)


def _gather_rope_cols(g_pe):
    half = MLA_ROPE // 2
    return jnp.concatenate([g_pe[..., :half], g_pe[..., 2 * half:3 * half]], axis=-1)


def mla_low_fwd(c, q_norm, kv_norm, cos, sin, seq, name):
    t, w = c.shape
    rk = MLA_RANK
    tm = _tile(seq, 256, SUBLANES)
    nsb = seq // tm

    def body(c_ref, qg_ref, kg_ref, cs_ref, sn_ref, qn_ref, kn_ref, kp_ref):
        for lo, g_ref, o_ref in ((0, qg_ref, qn_ref), (rk, kg_ref, kn_ref)):
            v = c_ref[:, lo:lo + rk]
            r = lax.rsqrt(jnp.mean(v * v, axis=-1, keepdims=True) + NORM_EPS)
            o_ref[...] = (v * r * g_ref[...]).astype(BF16)
        kp_ref[...] = _rope(c_ref[:, 2 * rk:], cs_ref[...], sn_ref[...]).astype(BF16)

    row = lambda n: pl.BlockSpec((tm, n), lambda i: (i, 0))
    vec = pl.BlockSpec((1, rk), lambda i: (0, 0))
    tab = pl.BlockSpec((tm, LANES), lambda i: (i % nsb, 0))
    return _call(body, name, (t // tm,), [row(w), vec, vec, tab, tab], [row(rk), row(rk), row(LANES)],
                 [_sds((t, rk), BF16), _sds((t, rk), BF16), _sds((t, LANES), BF16)])(
        c, q_norm.reshape(1, rk), kv_norm.reshape(1, rk), cos, sin)


def mla_low_bwd(c, q_norm, kv_norm, dqn, dkn, dkp, cos, sin, seq, name):
    t, w = c.shape
    rk = MLA_RANK
    tm = _tile(seq, 256, SUBLANES)
    nsb = seq // tm

    def body(c_ref, qg_ref, kg_ref, dq_ref, dk_ref, dp_ref, cs_ref, sn_ref, dc_ref, dqg_ref, dkg_ref):
        @pl.when(pl.program_id(0) == 0)
        def _():
            dqg_ref[...] = jnp.zeros_like(dqg_ref)
            dkg_ref[...] = jnp.zeros_like(dkg_ref)

        for lo, g_ref, d_ref, dg_ref in ((0, qg_ref, dq_ref, dqg_ref), (rk, kg_ref, dk_ref, dkg_ref)):
            dx, dg = _rms_bwd_math(c_ref[:, lo:lo + rk], g_ref[...], d_ref[...])
            dc_ref[:, lo:lo + rk] = dx.astype(BF16)
            dg_ref[...] += dg
        dc_ref[:, 2 * rk:] = _rope_t(dp_ref[...], cs_ref[...], sn_ref[...]).astype(BF16)

    row = lambda n: pl.BlockSpec((tm, n), lambda i: (i, 0))
    vec = pl.BlockSpec((1, rk), lambda i: (0, 0))
    tab = pl.BlockSpec((tm, LANES), lambda i: (i % nsb, 0))
    dc, dqg, dkg = _call(body, name, (t // tm,), [row(w), vec, vec, row(rk), row(rk), row(LANES), tab, tab],
                         [row(w), vec, vec], [_sds((t, w), BF16), _sds((1, rk), F32), _sds((1, rk), F32)])(
        c, q_norm.reshape(1, rk), kv_norm.reshape(1, rk), dqn, dkn, dkp, cos, sin)
    return dc[None], dqg, dkg


def mla_rope_cast(q, kv, cos, sin, seq, name, transpose=False):
    t, w = q.shape
    tm = _tile(seq, 256, SUBLANES)
    nsb = seq // tm
    fn = _rope_t if transpose else _rope

    def body(q_ref, kv_ref, cs_ref, sn_ref, qo_ref, kvo_ref):
        cos_t, sin_t = cs_ref[...], sn_ref[...]
        for blk in range(w // LANES):
            sl = slice(blk * LANES, (blk + 1) * LANES)
            v = q_ref[:, sl]
            if blk % 2 == 1:
                v = fn(v, cos_t, sin_t)
            qo_ref[:, sl] = v.astype(BF16)
        kvo_ref[...] = kv_ref[...].astype(BF16)

    row = pl.BlockSpec((tm, w), lambda i: (i, 0))
    tab = pl.BlockSpec((tm, LANES), lambda i: (i % nsb, 0))
    return _call(body, name, (t // tm,), [row, row, tab, tab], [row, row], [_sds((t, w), BF16)] * 2)(q, kv, cos, sin)


def _causal_mask(i, j, tq, tk):
    qpos = i * tq + lax.broadcasted_iota(jnp.int32, (tq, tk), 0)
    kpos = j * tk + lax.broadcasted_iota(jnp.int32, (tq, tk), 1)
    return kpos <= qpos


def mla_attn_fwd(q_b, kv_b, kp_b, batch, seq, name, tq=512):
    t, w = q_b.shape
    h_n = w // (2 * LANES)
    tq = _tile(seq, tq, SUBLANES)
    nq = seq // tq
    scale = (MLA_NOPE + MLA_ROPE) ** -0.5

    def body(q_ref, kv_ref, kp_ref, o_ref, lse_ref):
        i = pl.program_id(2)
        qn, qp = q_ref[:, :LANES], q_ref[:, LANES:]

        def step(j, carry):
            m, l, acc = carry
            rows = pl.ds(pl.multiple_of(j * tq, tq), tq)
            s = (_dot_nt(qn, kv_ref[rows, :LANES]) + _dot_nt(qp, kp_ref[rows, :])) * scale
            s = jnp.where(_causal_mask(i, j, tq, tq), s, NEG_INF)
            mn = jnp.maximum(m, jnp.max(s, axis=-1, keepdims=True))
            p = jnp.exp(s - mn)
            a = jnp.exp(m - mn)
            return mn, a * l + jnp.sum(p, axis=-1, keepdims=True), a * acc + _dot(p.astype(BF16), kv_ref[rows, LANES:])

        init = (jnp.full((tq, 1), NEG_INF, F32), jnp.zeros((tq, 1), F32), jnp.zeros((tq, LANES), F32))
        m, l, acc = lax.fori_loop(0, i + 1, step, init)
        o_ref[...] = acc / l
        lse_ref[...] = jnp.broadcast_to(m + jnp.log(l), (tq, LANES))

    in_specs = [pl.BlockSpec((None, tq, 2 * LANES), lambda b, h, i: (b, i, h)),
                pl.BlockSpec((None, seq, 2 * LANES), lambda b, h, i: (b, 0, h)),
                pl.BlockSpec((None, seq, LANES), lambda b, h, i: (b, 0, 0))]
    out_spec = pl.BlockSpec((None, tq, LANES), lambda b, h, i: (b, i, h))
    o, lse = _call(body, name, (batch, h_n, nq), in_specs, [out_spec, out_spec],
                   [_sds((batch, seq, h_n * LANES), F32)] * 2)(
        q_b.reshape(batch, seq, w), kv_b.reshape(batch, seq, w), kp_b.reshape(batch, seq, LANES))
    return o.reshape(t, h_n * LANES), lse.reshape(t, h_n * LANES)


def mla_attn_bwd(q_b, kv_b, kp_b, do, lse, delta, batch, seq, name, tq=512):
    t, w = q_b.shape
    h_n = w // (2 * LANES)
    tq = _tile(seq, tq, SUBLANES)
    nq = seq // tq
    scale = (MLA_NOPE + MLA_ROPE) ** -0.5

    def body(q_ref, kv_ref, kp_ref, do_ref, lse_ref, dl_ref, dq_ref, dkv_ref, dkp_ref):
        dq_ref[...] = jnp.zeros_like(dq_ref)
        dkv_ref[...] = jnp.zeros_like(dkv_ref)

        @pl.when(pl.program_id(1) == 0)
        def _():
            dkp_ref[...] = jnp.zeros_like(dkp_ref)

        def outer(j, _):
            krows = pl.ds(pl.multiple_of(j * tq, tq), tq)
            kn, v, kp = kv_ref[krows, :LANES], kv_ref[krows, LANES:], kp_ref[krows, :]

            def inner(i, _):
                qrows = pl.ds(pl.multiple_of(i * tq, tq), tq)
                qn, qp = q_ref[qrows, :LANES], q_ref[qrows, LANES:]
                dout = do_ref[qrows, :].astype(BF16)
                s = (_dot_nt(qn, kn) + _dot_nt(qp, kp)) * scale
                s = jnp.where(_causal_mask(i, j, tq, tq), s, NEG_INF)
                p = jnp.exp(s - lse_ref[qrows, 0:1])
                ds = (p * (_dot_nt(dout, v) - dl_ref[qrows, 0:1]) * scale).astype(BF16)
                dq_ref[qrows, :LANES] += _dot(ds, kn)
                dq_ref[qrows, LANES:] += _dot(ds, kp)
                dkv_ref[krows, :LANES] += _dot_tn(ds, qn)
                dkv_ref[krows, LANES:] += _dot_tn(p.astype(BF16), dout)
                dkp_ref[krows, :] += _dot_tn(ds, qp)
                return 0

            lax.fori_loop(j, nq, inner, 0)
            return 0

        lax.fori_loop(0, nq, outer, 0)

    wide = pl.BlockSpec((None, seq, 2 * LANES), lambda b, h: (b, 0, h))
    head = pl.BlockSpec((None, seq, LANES), lambda b, h: (b, 0, h))
    shared = pl.BlockSpec((None, seq, LANES), lambda b, h: (b, 0, 0))
    hv = [a.reshape(batch, seq, h_n * LANES) for a in (do, lse, delta)]
    dq, dkv, dkp = _call(body, name, (batch, h_n), [wide, wide, shared, head, head, head], [wide, wide, shared],
                         [_sds((batch, seq, w), F32), _sds((batch, seq, w), F32), _sds((batch, seq, LANES), F32)])(
        q_b.reshape(batch, seq, w), kv_b.reshape(batch, seq, w), kp_b.reshape(batch, seq, LANES), *hv)
    return dq.reshape(t, w), dkv.reshape(t, w), dkp.reshape(t, LANES)


def _sgu_common(z_ref, g_ref, b_ref, ws_ref, bs_ref):
    e = z_ref.shape[-1]
    ge = e // SG_GROUPS
    zu, zv = z_ref[0], z_ref[1]
    cu, cv = _gelu_cdf(zu), _gelu_cdf(zv)
    u = zu * cu
    xh, r = _ln_stats(zv * cv)
    vn = (xh * g_ref[...] + b_ref[...]).astype(BF16)
    tri = (lax.broadcasted_iota(jnp.int32, (SG_CHUNK, SG_CHUNK), 0)
           >= lax.broadcasted_iota(jnp.int32, (SG_CHUNK, SG_CHUNK), 1))
    ws = [jnp.where(tri, ws_ref[g], 0.0).astype(BF16) for g in range(SG_GROUPS)]
    v2 = [_dot(ws[g], vn[:, g * ge:(g + 1) * ge]) + bs_ref[:, g:g + 1] for g in range(SG_GROUPS)]
    return zu, zv, cu, cv, u, xh, r, vn, tri, ws, v2


def sgu_fwd(z2, ln_g, ln_b, w_s, b_s_t, name):
    _, t, e = z2.shape
    ge = e // SG_GROUPS

    def body(z_ref, g_ref, b_ref, ws_ref, bs_ref, y_ref):
        _, _, _, _, u, _, _, _, _, _, v2 = _sgu_common(z_ref, g_ref, b_ref, ws_ref, bs_ref)
        for g in range(SG_GROUPS):
            y_ref[:, g * ge:(g + 1) * ge] = (u[:, g * ge:(g + 1) * ge] * v2[g]).astype(BF16)

    vec = pl.BlockSpec((1, e), lambda i: (0, 0))
    in_specs = [pl.BlockSpec((2, SG_CHUNK, e), lambda i: (0, i, 0)), vec, vec,
                pl.BlockSpec((SG_GROUPS, SG_CHUNK, SG_CHUNK), lambda i: (0, 0, 0)),
                pl.BlockSpec((SG_CHUNK, SG_GROUPS), lambda i: (0, 0))]
    return _call(body, name, (t // SG_CHUNK,), in_specs, pl.BlockSpec((SG_CHUNK, e), lambda i: (i, 0)),
                 _sds((t, e), BF16))(z2, ln_g.reshape(1, e), ln_b.reshape(1, e), w_s, b_s_t)


def sgu_bwd(z2, dy, ln_g, ln_b, w_s, b_s_t, name):
    _, t, e = z2.shape
    ge = e // SG_GROUPS

    def body(z_ref, dy_ref, g_ref, b_ref, ws_ref, bs_ref, dz_ref, dbi_ref, dg_ref, db_ref, dws_ref, dbs_ref):
        @pl.when(pl.program_id(0) == 0)
        def _():
            for ref in (dbi_ref, dg_ref, db_ref, dws_ref, dbs_ref):
                ref[...] = jnp.zeros_like(ref)

        zu, zv, cu, cv, u, xh, r, vn, tri, ws, v2 = _sgu_common(z_ref, g_ref, b_ref, ws_ref, bs_ref)
        dy = dy_ref[...]
        dvn = []
        for g in range(SG_GROUPS):
            sl = slice(g * ge, (g + 1) * ge)
            dyg = dy[:, sl]
            du = dyg * v2[g] * _gelu_grad(zu[:, sl], cu[:, sl])
            dz_ref[0, :, sl] = du.astype(BF16)
            dbi_ref[0, :, sl] += jnp.sum(du, axis=0, keepdims=True)
            dv2 = dyg * u[:, sl]
            dbs_ref[:, g:g + 1] += jnp.sum(dv2, axis=-1, keepdims=True)
            dv2b = dv2.astype(BF16)
            dws_ref[g] += jnp.where(tri, _dot_nt(dv2b, vn[:, sl]), 0.0)
            dvn.append(_dot_tn(ws[g], dv2b))
        dvn = jnp.concatenate(dvn, axis=-1)
        dv, dg, db = _ln_bwd(dvn, xh, r, g_ref[...])
        dzv = dv * _gelu_grad(zv, cv)
        dz_ref[1] = dzv.astype(BF16)
        dbi_ref[1] += jnp.sum(dzv, axis=0, keepdims=True)
        dg_ref[...] += dg
        db_ref[...] += db

    vec = pl.BlockSpec((1, e), lambda i: (0, 0))
    wsb = pl.BlockSpec((SG_GROUPS, SG_CHUNK, SG_CHUNK), lambda i: (0, 0, 0))
    bsb = pl.BlockSpec((SG_CHUNK, SG_GROUPS), lambda i: (0, 0))
    zblk = pl.BlockSpec((2, SG_CHUNK, e), lambda i: (0, i, 0))
    in_specs = [zblk, pl.BlockSpec((SG_CHUNK, e), lambda i: (i, 0)), vec, vec, wsb, bsb]
    out_specs = [zblk, pl.BlockSpec((2, 1, e), lambda i: (0, 0, 0)), vec, vec, wsb, bsb]
    out_shape = [_sds((2, t, e), BF16), _sds((2, 1, e), F32), _sds((1, e), F32), _sds((1, e), F32),
                 _sds((SG_GROUPS, SG_CHUNK, SG_CHUNK), F32), _sds((SG_CHUNK, SG_GROUPS), F32)]
    return _call(body, name, (t // SG_CHUNK,), in_specs, out_specs, out_shape)(
        z2, dy, ln_g.reshape(1, e), ln_b.reshape(1, e), w_s, b_s_t)


_COL = ("cc_w_in", "ffn_w_up", "dsa_w_qkv", "dsa_w_o", "mla_w_qb", "mla_w_kvb", "sg_w_in")
_ROW = ("cc_w_out", "ffn_w_down", "mla_w_in", "mla_w_o", "sg_w_out")
_SMALL_COL = ("cc_dw_w", "ffn_dw_w")
_RELAID = ("cc_w_in", "dsa_w_o", "mla_w_kvb")
_MIXERS = ("cc", "dsa", "mla", "sg")


def _base(name):
    return name.split("_", 1)[1] if name[0] == "l" and name[1].isdigit() else name


def _ffn_fwd(pre, p, full, h_prev, mix_out, bsz, seq):
    t, d = h_prev.shape
    h, hn = rms_fwd(h_prev, p[pre + "norm_ffn"], pre + "ffn_norm", add=mix_out)
    w_up, w_down = full[pre + "ffn_w_up"], full[pre + "ffn_w_down"]
    f = w_down.shape[1]
    z2r = mm_nn(hn[None], w_up, pre + "ffn_up", out_parts=2).reshape(2, bsz, seq, f)
    dw_w2 = jnp.transpose(full[pre + "ffn_dw_w"].reshape(FFN_WIDTH, 2, f), (1, 0, 2))
    dw_b2 = p[pre + "ffn_dw_b"].reshape(2, 1, f)
    y = ffn_gate_fwd(z2r, dw_w2, dw_b2, pre + "ffn_gate").reshape(1, t, f)
    out = mm_nn(y, w_down, pre + "ffn_down")[0]
    return out, dict(h=h, hn=hn, z2r=z2r, y=y, dw_w2=dw_w2, dw_b2=dw_b2)


def relayout_cols(g, name, to_shards):
    k = g.shape[1]
    n = g.shape[2] // N_DEV if to_shards else g.shape[2]
    tk = _row_tile(k, n, itemsize=g.dtype.itemsize, budget=1 << 21)

    def body(i_ref, o_ref):
        o_ref[...] = i_ref[...]

    shard = pl.BlockSpec((None, tk, n), lambda j, ki: (j, ki, 0))
    whole = pl.BlockSpec((None, tk, n), lambda j, ki: (0, ki, j))
    if to_shards:
        return _call(body, name, (N_DEV, k // tk), [whole], shard, _sds((N_DEV, k, n), g.dtype))(g)
    return _call(body, name, (N_DEV, k // tk), [shard], whole, _sds((1, k, N_DEV * n), g.dtype))(g)


def _col_grad(a, b, w, name):
    g = mm_tn(a, b, w.shape[0], name)
    return g if w.shape[0] == N_DEV else relayout_cols(g, name + "_shards", True)


def _ffn_bwd(pre, p, full, ctx, dh_out, big_g, small_g, bsz, seq, mid):
    w_up, w_down = full[pre + "ffn_w_up"], full[pre + "ffn_w_down"]
    f = w_down.shape[1]
    dh_out, dh_b = dh_out
    t, d = dh_out.shape
    dh3 = dh_b[None]
    dyv = mm_nt(dh3, w_down, pre + "ffn_down_dx")
    big_g[pre + "ffn_w_down"] = mm_tn(ctx["y"], dh3, 1, pre + "ffn_down_dw").reshape(N_DEV, f // N_DEV, d)
    mid(dyv)
    dz2, ddw2, ddb2 = ffn_gate_bwd(ctx["z2r"], dyv.reshape(bsz, seq, f), ctx["dw_w2"], ctx["dw_b2"],
                                   pre + "ffn_gate_bwd")
    dz2 = dz2.reshape(2, t, f)
    big_g[pre + "ffn_w_up"] = mm_tn(ctx["hn"][None], dz2, N_DEV, pre + "ffn_up_dw")
    dhn = mm_nt(dz2, w_up, pre + "ffn_up_dx")
    small_g[pre + "ffn_dw_w"] = jnp.transpose(ddw2, (1, 0, 2)).reshape(FFN_WIDTH, 2 * f)
    small_g[pre + "ffn_dw_b"] = ddb2.reshape(2 * f)
    dh, dh_b, dg = rms_bwd(ctx["h"], p[pre + "norm_ffn"], dhn, dh_out, pre + "ffn_norm_bwd")
    small_g[pre + "norm_ffn"] = dg.reshape(d)
    return dh, dh_b


def _cc_fwd(pre, p, full, hn, bsz, seq, aux):
    t, c = hn.shape
    z2r = mm_nn(hn[None], full[pre + "cc_w_in"], pre + "cc_in", bias=p[pre + "cc_b_in"], out_parts=2).reshape(
        2, bsz, seq, c)
    cv = cc_conv_fwd(z2r, full[pre + "cc_dw_w"], p[pre + "cc_dw_b"], pre + "cc_conv").reshape(t, c)
    s = cc_ln_fwd(cv, p[pre + "cc_ln_g"], p[pre + "cc_ln_b"], pre + "cc_ln")
    out = mm_nn(s[None], full[pre + "cc_w_out"], pre + "cc_out", bias=p[pre + "cc_b_out"])[0]
    return out, dict(z2r=z2r, cv=cv, s=s)


def _cc_bwd(pre, p, full, ctx, hn, dm, big_g, small_g, bsz, seq, aux, mid):
    dm, dm_b = dm
    t, c = dm.shape
    dm3 = dm_b[None]
    small_g[pre + "cc_b_out"] = col_sum(dm, pre + "cc_bout_g").reshape(c)
    ds = mm_nt(dm3, full[pre + "cc_w_out"], pre + "cc_out_dx")
    big_g[pre + "cc_w_out"] = mm_tn(ctx["s"][None], dm3, 1, pre + "cc_out_dw").reshape(N_DEV, c // N_DEV, c)
    mid(ds)
    dcv, dlg, dlb = cc_ln_bwd(ctx["cv"], p[pre + "cc_ln_g"], p[pre + "cc_ln_b"], ds, pre + "cc_ln_bwd")
    dz2, dbi, ddw, ddb = cc_conv_bwd(ctx["z2r"], dcv.reshape(bsz, seq, c), full[pre + "cc_dw_w"], pre + "cc_conv_bwd")
    dz2 = dz2.reshape(2, t, c)
    big_g[pre + "cc_w_in"] = _col_grad(hn[None], dz2, full[pre + "cc_w_in"], pre + "cc_in_dw")
    small_g[pre + "cc_ln_g"], small_g[pre + "cc_ln_b"] = dlg.reshape(c), dlb.reshape(c)
    small_g[pre + "cc_b_in"], small_g[pre + "cc_dw_w"], small_g[pre + "cc_dw_b"] = dbi.reshape(2 * c), ddw, ddb.reshape(c)
    return mm_nt(dz2, full[pre + "cc_w_in"], pre + "cc_in_dx")


def _dsa_fwd(pre, p, full, hn, bsz, seq, aux):
    cos, sin = aux["dsa_tables"]
    qkv = mm_nn(hn[None], full[pre + "dsa_w_qkv"], pre + "dsa_qkv")[0]
    qkv_r = dsa_rope_fwd(qkv, cos, sin, seq, pre + "dsa_rope")
    for window, dil in DSA_CONFIGS:
        assert window // dil == DSA_BLOCK and (seq // dil) % DSA_BLOCK == 0
    o, o_b, lse = dsa_attn_all_fwd(qkv_r, bsz, seq, pre + "dsa_attn")
    out = mm_nn(o_b[None], full[pre + "dsa_w_o"], pre + "dsa_o")[0]
    return out, dict(qkv_r=qkv_r, o=o, o_b=o_b, lse=lse)


def _dsa_bwd(pre, p, full, ctx, hn, dm, big_g, small_g, bsz, seq, aux, mid):
    cos, sin = aux["dsa_tables"]
    dm, dm_b = dm
    dm3 = dm_b[None]
    do = mm_nt(dm3, full[pre + "dsa_w_o"], pre + "dsa_o_dx")
    big_g[pre + "dsa_w_o"] = _col_grad(ctx["o_b"][None], dm3, full[pre + "dsa_w_o"], pre + "dsa_o_dw")
    mid(do)
    delta = head_delta(do, ctx["o"], pre + "dsa_delta")
    grads = dsa_attn_all_bwd(ctx["qkv_r"], do, ctx["lse"], delta, bsz, seq, pre + "dsa_attn_bwd")
    dqkv = dsa_rope_bwd_all(grads, cos, sin, seq, pre + "dsa_rope_bwd")
    big_g[pre + "dsa_w_qkv"] = mm_tn(hn[None], dqkv, N_DEV, pre + "dsa_qkv_dw")
    return mm_nt(dqkv, full[pre + "dsa_w_qkv"], pre + "dsa_qkv_dx")


def _mla_weights(pre, full):
    rk, hd = MLA_RANK, MLA_NOPE + MLA_ROPE
    w_in = full[pre + "mla_w_in"][0]
    w_in_p = jnp.concatenate([w_in[:, :2 * rk], _spread_rope_cols(w_in[:, 2 * rk:])], axis=1)[None]
    w_qb = jnp.transpose(full[pre + "mla_w_qb"], (1, 0, 2)).reshape(rk, MLA_HEADS, hd)
    w_qb_p = jnp.concatenate([w_qb[..., :MLA_NOPE], _spread_rope_cols(w_qb[..., MLA_NOPE:])], axis=-1)
    return w_in_p, w_qb_p.reshape(1, rk, MLA_HEADS * 2 * LANES)


def _mla_fwd(pre, p, full, hn, bsz, seq, aux):
    cos, sin = aux["mla_tables"]
    w_in_p, w_qb_p = _mla_weights(pre, full)
    c = mm_nn(hn[None], w_in_p, pre + "mla_in")[0]
    qn, kn, kp = mla_low_fwd(c, p[pre + "mla_q_norm"], p[pre + "mla_kv_norm"], cos, sin, seq, pre + "mla_low")
    q = mm_nn(qn[None], w_qb_p, pre + "mla_qb")[0]
    kv = mm_nn(kn[None], full[pre + "mla_w_kvb"], pre + "mla_kvb")[0]
    q_b, kv_b = mla_rope_cast(q, kv, cos, sin, seq, pre + "mla_rope")
    o, lse = mla_attn_fwd(q_b, kv_b, kp, bsz, seq, pre + "mla_attn")
    o_b = o.astype(BF16)
    out = mm_nn(o_b[None], full[pre + "mla_w_o"], pre + "mla_o")[0]
    return out, dict(c=c, qn=qn, kn=kn, kp=kp, q_b=q_b, kv_b=kv_b, o=o, o_b=o_b, lse=lse, w_in_p=w_in_p, w_qb_p=w_qb_p)


def _mla_bwd(pre, p, full, ctx, hn, dm, big_g, small_g, bsz, seq, aux, mid):
    cos, sin = aux["mla_tables"]
    dm, dm_b = dm
    t, d = dm.shape
    rk = MLA_RANK
    dm3 = dm_b[None]
    do = mm_nt(dm3, full[pre + "mla_w_o"], pre + "mla_o_dx")
    big_g[pre + "mla_w_o"] = mm_tn(ctx["o_b"][None], dm3, 1, pre + "mla_o_dw").reshape(N_DEV, -1, d)
    mid(do)
    delta = head_delta(do, ctx["o"], pre + "mla_delta")
    dq, dkv, dkp = mla_attn_bwd(ctx["q_b"], ctx["kv_b"], ctx["kp"], do, ctx["lse"], delta, bsz, seq, pre + "mla_attn_bwd")
    dq_b, dkv_b = mla_rope_cast(dq, dkv, cos, sin, seq, pre + "mla_rope_bwd", transpose=True)
    g_qb = mm_tn(ctx["qn"][None], dq_b[None], 1, pre + "mla_qb_dw")[0].reshape(rk, MLA_HEADS, 2 * LANES)
    g_qb = jnp.concatenate([g_qb[..., :MLA_NOPE], _gather_rope_cols(g_qb[..., MLA_NOPE:])], axis=-1)
    big_g[pre + "mla_w_qb"] = jnp.transpose(g_qb.reshape(rk, N_DEV, -1), (1, 0, 2))
    dqn = mm_nt(dq_b[None], ctx["w_qb_p"], pre + "mla_qb_dx")
    big_g[pre + "mla_w_kvb"] = _col_grad(ctx["kn"][None], dkv_b[None], full[pre + "mla_w_kvb"], pre + "mla_kvb_dw")
    dkn = mm_nt(dkv_b[None], full[pre + "mla_w_kvb"], pre + "mla_kvb_dx")
    dc, dqg, dkg = mla_low_bwd(ctx["c"], p[pre + "mla_q_norm"], p[pre + "mla_kv_norm"], dqn, dkn, dkp, cos, sin, seq,
                               pre + "mla_low_bwd")
    small_g[pre + "mla_q_norm"], small_g[pre + "mla_kv_norm"] = dqg.reshape(rk), dkg.reshape(rk)
    g_in = mm_tn(hn[None], dc, 1, pre + "mla_in_dw")[0]
    g_in = jnp.concatenate([g_in[:, :2 * rk], _gather_rope_cols(g_in[:, 2 * rk:])], axis=1)
    big_g[pre + "mla_w_in"] = g_in.reshape(N_DEV, d // N_DEV, -1)
    return mm_nt(dc, ctx["w_in_p"], pre + "mla_in_dx")


def _sg_fwd(pre, p, full, hn, bsz, seq, aux):
    z2 = mm_nn(hn[None], full[pre + "sg_w_in"], pre + "sg_in", bias=p[pre + "sg_b_in"], out_parts=2)
    b_s_t = jnp.transpose(p[pre + "sg_b_s"])
    y = sgu_fwd(z2, p[pre + "sg_ln_g"], p[pre + "sg_ln_b"], p[pre + "sg_w_s"], b_s_t, pre + "sg_mix")
    out = mm_nn(y[None], full[pre + "sg_w_out"], pre + "sg_out", bias=p[pre + "sg_b_out"])[0]
    return out, dict(z2=z2, y=y, b_s_t=b_s_t)


def _sg_bwd(pre, p, full, ctx, hn, dm, big_g, small_g, bsz, seq, aux, mid):
    dm, dm_b = dm
    t, d = dm.shape
    dm3 = dm_b[None]
    small_g[pre + "sg_b_out"] = col_sum(dm, pre + "sg_bout_g").reshape(d)
    dy = mm_nt(dm3, full[pre + "sg_w_out"], pre + "sg_out_dx")
    big_g[pre + "sg_w_out"] = mm_tn(ctx["y"][None], dm3, 1, pre + "sg_out_dw").reshape(N_DEV, -1, d)
    mid(dy)
    dz2, dbi, dlg, dlb, dws, dbs_t = sgu_bwd(ctx["z2"], dy, p[pre + "sg_ln_g"], p[pre + "sg_ln_b"], p[pre + "sg_w_s"],
                                             ctx["b_s_t"], pre + "sg_mix_bwd")
    big_g[pre + "sg_w_in"] = mm_tn(hn[None], dz2, N_DEV, pre + "sg_in_dw")
    small_g[pre + "sg_b_in"] = dbi.reshape(-1)
    small_g[pre + "sg_ln_g"], small_g[pre + "sg_ln_b"] = dlg.reshape(-1), dlb.reshape(-1)
    small_g[pre + "sg_w_s"], small_g[pre + "sg_b_s"] = dws, jnp.transpose(dbs_t)
    return mm_nt(dz2, full[pre + "sg_w_in"], pre + "sg_in_dx")


_MIX_FWD = dict(cc=_cc_fwd, dsa=_dsa_fwd, mla=_mla_fwd, sg=_sg_fwd)
_MIX_BWD = dict(cc=_cc_bwd, dsa=_dsa_bwd, mla=_mla_bwd, sg=_sg_bwd)


def _reduce_scatter(parts, core, chip, tag, adams=None, after=()):
    recv = rs_core_exchange(parts, tag + "_cores", after)
    sums = [rs_core_add(pt, rc, core, "%s_add%d" % (tag, i)) for i, (pt, rc) in enumerate(zip(parts, recv))]
    recv2 = rs_chip_exchange(sums, tag + "_chips")
    return [rs_finish(s, r2, chip, "%s_fin%d" % (tag, i), adam=None if adams is None else adams[i])
            for i, (s, r2) in enumerate(zip(sums, recv2))]


class _Schedule:
    def __init__(self):
        self.actions = {}

    def at(self, key, fn):
        self.actions.setdefault(key, []).append(fn)

    def run(self, key, x):
        after = list(x) if isinstance(x, (list, tuple)) else [x]
        for fn in self.actions.pop(key, []):
            fn(after)


def _aside(fn):
    saved = list(_PENDING)
    _PENDING.clear()
    out = fn()
    _PENDING[:0] = saved
    return out


def cast_bf16(w, after, name):
    r, c = w.shape
    tr = _row_tile(r, c, budget=1 << 21)
    na = len(after)

    def body(*refs):
        refs[1 + na][...] = refs[0][...].astype(BF16)

    blk = pl.BlockSpec((tr, c), lambda i: (i, 0))
    return pl.pallas_call(body, name=name, grid=(r // tr,), in_specs=[blk] + [_ANY] * na, out_specs=blk,
                          out_shape=_sds((r, c), BF16),
                          compiler_params=pltpu.CompilerParams(vmem_limit_bytes=VMEM_LIMIT))(w, *after)


def _gather_overlapped(sched, keys, shards_fn, n, dev, tag, deliver):
    state = {}

    def start(x):
        shards = shards_fn(x)
        lands = [lax.dynamic_update_slice(lax.empty((N_DEV,) + s.shape, s.dtype), s[None], (dev,) + (0,) * s.ndim)
                 for s in shards]
        state["first"] = exchange_start(_ag_plan_first, 4 * n, shards, lands, x, tag +"_start")
        _PENDING.append(state["first"][4])

    def pass_on(x):
        _, lands = exchange_wait(_ag_plan_first, state["first"], x, tag +"_wait")
        state["pass"] = exchange_start(_ag_plan_pass, 3 * n, [], lands, [], tag + "_pass")
        _PENDING.append(state["pass"][4])

    def done(x):
        deliver(exchange_wait(_ag_plan_pass, state["pass"], x, tag +"_done")[1])

    for key, fn in zip(keys, (start, pass_on, done)):
        sched.at(key, fn)


def _reduce_overlapped(sched, keys, parts, adams, core, chip, tag, deliver):
    n = len(parts)
    state = {}

    def start(x):
        lands = [lax.empty((4,) + pt.shape[1:], pt.dtype) for pt in parts]
        state["cores"] = exchange_start(_rs_plan_cores, 4 * n, parts, lands, x, tag +"_start")
        _PENDING.append(state["cores"][4])

    def middle(x):
        mine, recv = exchange_wait(_rs_plan_cores, state["cores"], x, tag +"_wait")
        sums = _aside(lambda: [rs_core_add(pt, rc, core, "%s_add%d" % (tag, i))
                               for i, (pt, rc) in enumerate(zip(mine, recv))])
        lands = [lax.empty((3,) + s.shape[1:], s.dtype) for s in sums]
        state["chips"] = exchange_start(_rs_plan_chips, 3 * n, sums, lands, [], tag + "_send")
        _PENDING.append(state["chips"][4])

    def done(x):
        sums, recv = exchange_wait(_rs_plan_chips, state["chips"], x, tag +"_done")
        deliver(_aside(lambda: [rs_finish(s, r2, chip, "%s_fin%d" % (tag, i), adam=adams[i])
                                for i, (s, r2) in enumerate(zip(sums, recv))]))

    for key, fn in zip(keys, (start, middle, done)):
        sched.at(key, fn)


def _pack(arrays):
    flat = jnp.concatenate([a.reshape(-1) for a in arrays])
    unit = N_DEV * SUBLANES * LANES
    padded = -(-flat.shape[0] // unit) * unit
    return jnp.pad(flat, (0, padded - flat.shape[0]))


def _unpack(flat, like):
    out, pos = [], 0
    for a in like:
        out.append(flat[pos:pos + a.size].reshape(a.shape))
        pos += a.size
    return out


def _train_step(p):
    names = list(p)
    wnames = names[1:names.index("loss_target")]
    x = p["x"]
    bsz, seq, d = x.shape
    t = bsz * seq
    xi, yi, ci = lax.axis_index("x"), lax.axis_index("y"), lax.axis_index("c")
    core = jnp.reshape(ci, (1,)).astype(jnp.int32)
    chip = jnp.reshape(2 * xi + yi, (1,)).astype(jnp.int32)
    dev = 4 * xi + 2 * yi + ci
    n_layers = 1 + max(int(n[1]) for n in wnames if n[0] == "l" and n[1].isdigit())
    big = [n for n in wnames if _base(n) in _COL + _ROW]
    aux = dict(dsa_tables=rope_tables_full(seq, LANES), mla_tables=mla_rope_tables(seq))

    sched = _Schedule()
    full = {}

    def deliver_weights(grp):
        def deliver(lands):
            for n, g in zip(grp, lands):
                if _base(n) in _RELAID:
                    full[n] = _aside(lambda: relayout_cols(g, n + "_whole", False))
                elif _base(n) in _COL:
                    full[n] = g
                elif _base(n) in _ROW:
                    full[n] = g.reshape(1, N_DEV * g.shape[1], g.shape[2])
                else:
                    full[n] = jnp.transpose(g, (1, 0, 2)).reshape(g.shape[1], N_DEV * g.shape[2])
        return deliver

    groups = []
    for layer in range(n_layers):
        lw = [n for n in wnames if n.startswith("l%d_" % layer) and _base(n) in _COL + _ROW + _SMALL_COL]
        groups += [[n for n in lw if "_ffn_" not in n], [n for n in lw if "_ffn_" in n]]

    def shards_of(grp):
        def shards(after):
            if not after:
                return [p[n] if _base(n) in _SMALL_COL else p[n].astype(BF16) for n in grp]
            return [p[n] if _base(n) in _SMALL_COL else cast_bf16(p[n], after, n + "_bf16") for n in grp]
        return shards

    for s in range(len(groups)):
        if s == 0:
            keys = [("f", -1), ("first", 0), ("first", 0)]
        else:
            keys = [("f", s - 2 if s > 2 else s - 1), ("f", s - 1 if s > 2 else s), ("f", s)]
        _gather_overlapped(sched, keys, shards_of(groups[s]), len(groups[s]), dev, "ag_s%d" % s,
                           deliver_weights(groups[s]))
    sched.run(("f", -1), [])

    h = x.reshape(t, d)
    ffn_out = None
    ctxs = []
    for layer in range(n_layers):
        pre = "l%d_" % layer
        kind = _MIXERS[layer % len(_MIXERS)]
        if ffn_out is None:
            hn = rms_fwd(h, p[pre + "norm_mix"], pre + "mix_norm")
        else:
            h, hn = rms_fwd(h, p[pre + "norm_mix"], pre + "mix_norm", add=ffn_out)
        if layer == 0:
            sched.run(("first", 0), hn)
        sched.run(("f", 2 * layer), [hn] + (list(full.values()) if layer == 0 else []))
        mix_out, mctx = _MIX_FWD[kind](pre, p, full, hn, bsz, seq, aux)
        sched.run(("f", 2 * layer + 1), mix_out)
        ffn_out, fctx = _ffn_fwd(pre, p, full, h, mix_out, bsz, seq)
        ctxs.append((h, hn, mctx, fctx))
        h = fctx["h"]

    loss_row, dh, dh_b, dfin = final_loss(h, ffn_out, p["final_norm"], p["loss_target"].reshape(t, d), "final_loss")
    dh = (dh, dh_b)
    loss = lax.psum(loss_row[0, 0], ("x", "y", "c"))

    small_g = {"final_norm": dfin.reshape(d)}
    results = {}

    def deliver_grads(lnames):
        def deliver(outs):
            for n, o in zip(lnames, outs):
                results[n] = o
        return deliver

    def reduce_later(q, lnames, big_g, tag):
        keys = [("b", q + 2), ("b", q + 3), ("b", q + 5)]
        _reduce_overlapped(sched, keys, [big_g[n] for n in lnames], [(p[n], p["m_" + n], p["v_" + n]) for n in lnames],
                           core, chip, tag, deliver_grads(lnames))

    q = 0
    for layer in reversed(range(n_layers)):
        pre = "l%d_" % layer
        kind = _MIXERS[layer % len(_MIXERS)]
        h_in, hn, mctx, fctx = ctxs[layer]
        big_g = {}
        sched.run(("b", q), dh[0])
        dh = _ffn_bwd(pre, p, full, fctx, dh, big_g, small_g, bsz, seq, lambda v, q=q: sched.run(("b", q + 1), v))
        reduce_later(q, [n for n in big if n.startswith(pre) and "_ffn_" in n], big_g, "rs_l%d_ffn" % layer)
        q += 2
        sched.run(("b", q), dh[0])
        dhn = _MIX_BWD[kind](pre, p, full, mctx, hn, dh, big_g, small_g, bsz, seq, aux,
                             lambda v, q=q: sched.run(("b", q + 1), v))
        dh_f, dh_b, dg = rms_bwd(h_in, p[pre + "norm_mix"], dhn, dh[0], pre + "mix_norm_bwd")
        dh = (dh_f, dh_b)
        small_g[pre + "norm_mix"] = dg.reshape(d)
        reduce_later(q, [n for n in big if n.startswith(pre) and "_ffn_" not in n], big_g, "rs_l%d_mix" % layer)
        q += 2
    for tail in range(q, q + 4):
        sched.run(("b", tail), dh[0])
    assert not sched.actions, sched.actions.keys()

    small = [n for n in wnames if n not in big]
    packed = _pack([small_g[n] for n in small]).reshape(N_DEV, -1, LANES)
    settled = [results[n][1] for n in big if not n.startswith("l0_")]
    reduced = _reduce_scatter([packed], core, chip, "rs_small", after=settled)[0]
    gathered = all_gather([reduced], "ag_small")[0].reshape(-1)
    g_local = []
    for n, g in zip(small, _unpack(gathered, [small_g[n] for n in small])):
        if _base(n) in _SMALL_COL:
            width = p[n].shape[1]
            g = lax.dynamic_slice_in_dim(g, dev * width, width, axis=1)
        g_local.append(g)
    stacked = [_pack(v).reshape(-1, LANES) for v in
               ([p[n] for n in small], g_local, [p["m_" + n] for n in small], [p["v_" + n] for n in small])]
    upd = adam_flat(*stacked, "adam_small")
    upd = [_unpack(u.reshape(-1), g_local) for u in upd]
    for i, n in enumerate(small):
        results[n] = (g_local[i], upd[0][i], upd[1][i], upd[2][i])

    grad_x = dh[0].reshape(bsz, seq, d)
    cols = [[results[n][k] for n in wnames] for k in range(4)]
    return (loss, grad_x, *cols[0], *cols[1], *cols[2], *cols[3])


def kernel(x, l0_norm_mix, l0_cc_w_in, l0_cc_b_in, l0_cc_dw_w, l0_cc_dw_b, l0_cc_ln_g, l0_cc_ln_b, l0_cc_w_out, l0_cc_b_out, l0_norm_ffn, l0_ffn_w_up, l0_ffn_dw_w, l0_ffn_dw_b, l0_ffn_w_down, l1_norm_mix, l1_dsa_w_qkv, l1_dsa_w_o, l1_norm_ffn, l1_ffn_w_up, l1_ffn_dw_w, l1_ffn_dw_b, l1_ffn_w_down, l2_norm_mix, l2_mla_w_in, l2_mla_q_norm, l2_mla_w_qb, l2_mla_kv_norm, l2_mla_w_kvb, l2_mla_w_o, l2_norm_ffn, l2_ffn_w_up, l2_ffn_dw_w, l2_ffn_dw_b, l2_ffn_w_down, l3_norm_mix, l3_sg_w_in, l3_sg_b_in, l3_sg_ln_g, l3_sg_ln_b, l3_sg_w_s, l3_sg_b_s, l3_sg_w_out, l3_sg_b_out, l3_norm_ffn, l3_ffn_w_up, l3_ffn_dw_w, l3_ffn_dw_b, l3_ffn_w_down, final_norm, loss_target, m_l0_norm_mix, m_l0_cc_w_in, m_l0_cc_b_in, m_l0_cc_dw_w, m_l0_cc_dw_b, m_l0_cc_ln_g, m_l0_cc_ln_b, m_l0_cc_w_out, m_l0_cc_b_out, m_l0_norm_ffn, m_l0_ffn_w_up, m_l0_ffn_dw_w, m_l0_ffn_dw_b, m_l0_ffn_w_down, m_l1_norm_mix, m_l1_dsa_w_qkv, m_l1_dsa_w_o, m_l1_norm_ffn, m_l1_ffn_w_up, m_l1_ffn_dw_w, m_l1_ffn_dw_b, m_l1_ffn_w_down, m_l2_norm_mix, m_l2_mla_w_in, m_l2_mla_q_norm, m_l2_mla_w_qb, m_l2_mla_kv_norm, m_l2_mla_w_kvb, m_l2_mla_w_o, m_l2_norm_ffn, m_l2_ffn_w_up, m_l2_ffn_dw_w, m_l2_ffn_dw_b, m_l2_ffn_w_down, m_l3_norm_mix, m_l3_sg_w_in, m_l3_sg_b_in, m_l3_sg_ln_g, m_l3_sg_ln_b, m_l3_sg_w_s, m_l3_sg_b_s, m_l3_sg_w_out, m_l3_sg_b_out, m_l3_norm_ffn, m_l3_ffn_w_up, m_l3_ffn_dw_w, m_l3_ffn_dw_b, m_l3_ffn_w_down, m_final_norm, v_l0_norm_mix, v_l0_cc_w_in, v_l0_cc_b_in, v_l0_cc_dw_w, v_l0_cc_dw_b, v_l0_cc_ln_g, v_l0_cc_ln_b, v_l0_cc_w_out, v_l0_cc_b_out, v_l0_norm_ffn, v_l0_ffn_w_up, v_l0_ffn_dw_w, v_l0_ffn_dw_b, v_l0_ffn_w_down, v_l1_norm_mix, v_l1_dsa_w_qkv, v_l1_dsa_w_o, v_l1_norm_ffn, v_l1_ffn_w_up, v_l1_ffn_dw_w, v_l1_ffn_dw_b, v_l1_ffn_w_down, v_l2_norm_mix, v_l2_mla_w_in, v_l2_mla_q_norm, v_l2_mla_w_qb, v_l2_mla_kv_norm, v_l2_mla_w_kvb, v_l2_mla_w_o, v_l2_norm_ffn, v_l2_ffn_w_up, v_l2_ffn_dw_w, v_l2_ffn_dw_b, v_l2_ffn_w_down, v_l3_norm_mix, v_l3_sg_w_in, v_l3_sg_b_in, v_l3_sg_ln_g, v_l3_sg_ln_b, v_l3_sg_w_s, v_l3_sg_b_s, v_l3_sg_w_out, v_l3_sg_b_out, v_l3_norm_ffn, v_l3_ffn_w_up, v_l3_ffn_dw_w, v_l3_ffn_dw_b, v_l3_ffn_w_down, v_final_norm):
    return _train_step(dict(locals()))
```

```python
import functools
import inspect
import math

import jax
import jax.numpy as jnp
from jax import lax
from jax.experimental import pallas as pl
from jax.experimental.pallas import tpu as pltpu

F32 = jnp.float32
BF16 = jnp.bfloat16
NORM_EPS = 1e-6
NEG_INF = -1e30
ROPE_THETA = 10000.0
LANES = 128
SUBLANES = 8
VMEM_LIMIT = 56 * 1024 * 1024
MM_VMEM_BUDGET = 44 * 1024 * 1024
N_DEV = 8

DSA_CONFIGS = ((128, 1), (512, 4), (2048, 16))
DSA_HEADS = 8
DSA_BLOCK = 128
MLA_HEADS = 16
MLA_RANK = 512
MLA_NOPE = 128
MLA_ROPE = 64
SG_CHUNK = 128
SG_GROUPS = 8
CC_WIDTH = 31
FFN_WIDTH = 3
ADAM_LR, ADAM_B1, ADAM_B2, ADAM_EPS, ADAM_WD, ADAM_STEP = 0.001, 0.9, 0.999, 1e-08, 0.01, 10


def _tile(n, cap, unit=LANES):
    if n <= cap:
        return n
    best = 0
    for t in range(unit, cap + 1, unit):
        if n % t == 0:
            best = t
    assert best, (n, cap, unit)
    return best


_PENDING = []


def _call(body, name, grid, in_specs, out_specs, out_shape, scratch=(), prefetch=0):
    params = pltpu.CompilerParams(vmem_limit_bytes=VMEM_LIMIT)
    deps = list(_PENDING)
    _PENDING.clear()
    if deps:
        inner, n_in = body, prefetch + len(in_specs)

        def body(*refs):
            return inner(*refs[:n_in], *refs[n_in + len(deps):])

        in_specs = list(in_specs) + [pl.BlockSpec(memory_space=pl.ANY)] * len(deps)
    if prefetch:
        spec = pltpu.PrefetchScalarGridSpec(num_scalar_prefetch=prefetch, grid=grid, in_specs=in_specs,
                                            out_specs=out_specs, scratch_shapes=list(scratch))
        fn = pl.pallas_call(body, out_shape=out_shape, grid_spec=spec, name=name, compiler_params=params)
    else:
        fn = pl.pallas_call(body, out_shape=out_shape, grid=grid, in_specs=in_specs, out_specs=out_specs,
                            scratch_shapes=list(scratch), name=name, compiler_params=params)
    return lambda *args: fn(*args, *deps)


def _sds(shape, dtype):
    return jax.ShapeDtypeStruct(tuple(shape), dtype)


def _bf(v):
    return v if v.dtype == BF16 else v.astype(BF16)


def _dot(a, b):
    return jnp.dot(a, b, preferred_element_type=F32)


def _dot_nt(a, b):
    return lax.dot_general(a, b, (((1,), (1,)), ((), ())), preferred_element_type=F32)


def _dot_tn(a, b):
    return lax.dot_general(a, b, (((0,), (0,)), ((), ())), preferred_element_type=F32)


def _sigmoid(v):
    return 1.0 / (1.0 + jnp.exp(-v))


_ERF_A = (-2.72614225801306e-10, 2.77068142495902e-08, -2.10102402082508e-06, -5.69250639462346e-05,
          -7.34990630326855e-04, -2.95459980854025e-03, -1.60960333262415e-02)
_ERF_B = (-1.45660718464996e-05, -2.13374055278905e-04, -1.68282697438203e-03, -7.37332916720468e-03,
          -1.42647390514189e-02)


def _erf(v):
    v = jnp.clip(v, -4.0, 4.0)
    v2 = v * v
    p = jnp.full_like(v, _ERF_A[0])
    for c in _ERF_A[1:]:
        p = p * v2 + c
    q = jnp.full_like(v, _ERF_B[0])
    for c in _ERF_B[1:]:
        q = q * v2 + c
    return v * p / q


def _gelu_cdf(v):
    return 0.5 * (1.0 + _erf(v * (2.0 ** -0.5)))


def _gelu_grad(v, cdf):
    return cdf + v * jnp.exp(-0.5 * v * v) * ((2.0 * math.pi) ** -0.5)


def _edge_rows(width):
    return -(-(width - 1) // SUBLANES) * SUBLANES


def _conv_causal(z, w_row, width):
    hr = _edge_rows(width)
    rows = lax.broadcasted_iota(jnp.int32, (hr, z.shape[1]), 0)
    out = z * w_row(width - 1)
    head = z[:hr] * w_row(width - 1)
    for j in range(1, width):
        tap = pltpu.roll(z, j, 0)
        out = out + tap * w_row(width - 1 - j)
        head = head + jnp.where(rows >= j, tap[:hr], 0.0) * w_row(width - 1 - j)
    return jnp.concatenate([head, out[hr:]], axis=0)


def _conv_causal_bwd(x, dc, w_row, width):
    n = dc.shape[0]
    hr = _edge_rows(width)
    rows = lax.broadcasted_iota(jnp.int32, (hr, dc.shape[1]), 0)
    x_tail, dc_tail = x[n - hr:], dc[n - hr:]
    dx = dc * w_row(width - 1)
    tail = dc_tail * w_row(width - 1)
    dws = {width - 1: jnp.sum(dc * x, axis=0, keepdims=True)}
    for j in range(1, width):
        up = pltpu.roll(dc, n - j, 0)
        up_tail = jnp.where(rows < hr - j, up[n - hr:], 0.0)
        dx = dx + up * w_row(width - 1 - j)
        tail = tail + up_tail * w_row(width - 1 - j)
        wrapped = jnp.sum((up[n - hr:] - up_tail) * x_tail, axis=0, keepdims=True)
        dws[width - 1 - j] = jnp.sum(up * x, axis=0, keepdims=True) - wrapped
    return jnp.concatenate([dx[:n - hr], tail], axis=0), dws, jnp.sum(dc, axis=0, keepdims=True)


def _rope(v, cos, sin):
    return v * cos + pltpu.roll(v, 64, 1) * sin


def _rope_t(dv, cos, sin):
    return dv * cos + pltpu.roll(dv * sin, 64, 1)


def mm_nn(a, w, name, bias=None, out_parts=1, out_dtype=F32, tm_cap=1024, tn_cap=1536, tk_cap=6144):
    pa, m, kp = a.shape
    j, k, n = w.shape
    assert pa * kp == k
    big_n = j * n
    npo = big_n // out_parts
    tm = _tile(m, tm_cap, SUBLANES)
    tk = _tile(kp, tk_cap)
    tn = _tile(math.gcd(n, npo), tn_cap if tk <= 2816 else 1024)
    out_bytes = jnp.dtype(out_dtype).itemsize

    def vmem_bytes(rows):
        return 2 * (rows * tk * 2 + tk * tn * 2 + rows * tn * out_bytes) + (rows * tn * 4 if k > tk else 0)

    while vmem_bytes(tm) > MM_VMEM_BUDGET and tm % (2 * SUBLANES) == 0:
        tm //= 2
    npj, nbo, kbp, nk = n // tn, npo // tn, kp // tk, k // tk
    has_bias = bias is not None

    def body(*refs):
        a_ref, w_ref = refs[0], refs[1]
        b_ref = refs[2] if has_bias else None
        o_ref = refs[2 + has_bias]

        def finish(acc):
            if has_bias:
                acc = acc + b_ref[...]
            o_ref[...] = acc.astype(o_ref.dtype)

        part = _dot(_bf(a_ref[...]), _bf(w_ref[...]))
        if nk == 1:
            finish(part)
        else:
            acc_ref = refs[3 + has_bias]
            ki = pl.program_id(2)

            @pl.when(ki == 0)
            def _():
                acc_ref[...] = part

            @pl.when(ki > 0)
            def _():
                acc_ref[...] += part

            @pl.when(ki == nk - 1)
            def _():
                finish(acc_ref[...])

    in_specs = [pl.BlockSpec((None, tm, tk), lambda ni, mi, ki: (ki // kbp, mi, ki % kbp)),
                pl.BlockSpec((None, tk, tn), lambda ni, mi, ki: (ni // npj, ki, ni % npj))]
    args = [a, w]
    if has_bias:
        in_specs.append(pl.BlockSpec((1, tn), lambda ni, mi, ki: (0, ni)))
        args.append(bias.reshape(1, big_n))
    out_spec = pl.BlockSpec((None, tm, tn), lambda ni, mi, ki: (ni // nbo, mi, ni % nbo))
    scratch = [pltpu.VMEM((tm, tn), F32)] if nk > 1 else []
    return _call(body, name, (big_n // tn, m // tm, nk), in_specs, out_spec,
                 _sds((out_parts, m, npo), out_dtype), scratch)(*args)


def mm_nt(a, w, name, out_dtype=F32, tm_cap=1024, tko_cap=2048, tc_cap=2048):
    pa, m, npa = a.shape
    j, k, n = w.shape
    assert pa * npa == j * n
    tm = _tile(m, tm_cap, SUBLANES)
    tko = _tile(k, tko_cap)
    tc = _tile(math.gcd(n, npa), tc_cap)
    out_bytes = jnp.dtype(out_dtype).itemsize

    def vmem_bytes(rows):
        return 2 * (rows * tc * 2 + tko * tc * 2 + rows * tko * out_bytes) + (rows * tko * 4 if j * n > tc else 0)

    while vmem_bytes(tm) > MM_VMEM_BUDGET and tm % (2 * SUBLANES) == 0:
        tm //= 2
    npj, nba, nc = n // tc, npa // tc, (j * n) // tc

    def body(a_ref, w_ref, o_ref, *scr):
        part = _dot_nt(_bf(a_ref[...]), _bf(w_ref[...]))
        if nc == 1:
            o_ref[...] = part.astype(o_ref.dtype)
        else:
            acc_ref = scr[0]
            ci = pl.program_id(2)

            @pl.when(ci == 0)
            def _():
                acc_ref[...] = part

            @pl.when(ci > 0)
            def _():
                acc_ref[...] += part

            @pl.when(ci == nc - 1)
            def _():
                o_ref[...] = acc_ref[...].astype(o_ref.dtype)

    in_specs = [pl.BlockSpec((None, tm, tc), lambda mi, ko, ci: (ci // nba, mi, ci % nba)),
                pl.BlockSpec((None, tko, tc), lambda mi, ko, ci: (ci // npj, ko, ci % npj))]
    out_spec = pl.BlockSpec((tm, tko), lambda mi, ko, ci: (mi, ko))
    scratch = [pltpu.VMEM((tm, tko), F32)] if nc > 1 else []
    return _call(body, name, (m // tm, k // tko, nc), in_specs, out_spec, _sds((m, k), out_dtype), scratch)(a, w)


def mm_tn(a, b, n_shards, name, out_dtype=BF16, tt_cap=4096, tkk_cap=512, tn_cap=1536):
    pa, m, kp = a.shape
    pb, m2, npb = b.shape
    assert m == m2
    k, big_n = pa * kp, pb * npb
    n = big_n // n_shards
    tt = _tile(m, tt_cap, SUBLANES)
    tkk = _tile(kp, tkk_cap)
    tn = _tile(math.gcd(n, npb), tn_cap)
    kbp, nbb, npj, nt = kp // tkk, npb // tn, n // tn, m // tt

    def body(a_ref, b_ref, o_ref, *scr):
        part = _dot_tn(_bf(a_ref[...]), _bf(b_ref[...]))
        if nt == 1:
            o_ref[...] = part.astype(o_ref.dtype)
        else:
            acc_ref = scr[0]
            ti = pl.program_id(2)

            @pl.when(ti == 0)
            def _():
                acc_ref[...] = part

            @pl.when(ti > 0)
            def _():
                acc_ref[...] += part

            @pl.when(ti == nt - 1)
            def _():
                o_ref[...] = acc_ref[...].astype(o_ref.dtype)

    in_specs = [pl.BlockSpec((None, tt, tkk), lambda ni, ki, ti: (ki // kbp, ti, ki % kbp)),
                pl.BlockSpec((None, tt, tn), lambda ni, ki, ti: (ni // nbb, ti, ni % nbb))]
    out_spec = pl.BlockSpec((None, tkk, tn), lambda ni, ki, ti: (ni // npj, ki, ni % npj))
    scratch = [pltpu.VMEM((tkk, tn), F32)] if nt > 1 else []
    return _call(body, name, (big_n // tn, k // tkk, nt), in_specs, out_spec,
                 _sds((n_shards, k, n), out_dtype), scratch)(a, b)


_ANY = pl.BlockSpec(memory_space=pl.ANY)
_MESH = pl.DeviceIdType.MESH


def all_gather(shards, name):
    n = len(shards)

    def body(*refs):
        ins, outs = refs[:n], refs[n:2 * n]
        send_sems, recv_sems, local_sems = refs[2 * n:]
        x, y, c = lax.axis_index("x"), lax.axis_index("y"), lax.axis_index("c")
        me, sibling = (x, y, c), (x, y, 1 - c)
        chips = [(1 - x, y), (x, 1 - y), (1 - x, 1 - y)]

        def copy(i, k, block, to, src=None):
            px, py, pc = block
            dst = outs[i].at[4 * px + 2 * py + pc]
            return pltpu.make_async_remote_copy(src_ref=dst if src is None else src, dst_ref=dst,
                                                send_sem=send_sems.at[i, k], recv_sem=recv_sems.at[i, k],
                                                device_id=to, device_id_type=_MESH)

        mine = [pltpu.make_async_copy(ins[i], outs[i].at[4 * x + 2 * y + c], local_sems.at[i]) for i in range(n)]
        for cp in mine:
            cp.start()
        first = []
        for i in range(n):
            first.append(copy(i, 0, me, sibling, src=ins[i]))
            first += [copy(i, 1 + q, me, (*chip, c), src=ins[i]) for q, chip in enumerate(chips)]
        for cp in first:
            cp.start()
        passed = []
        for q, chip in enumerate(chips):
            for i in range(n):
                copy(i, 1 + q, (*chip, c), me).wait_recv()
                cp = copy(i, 4 + q, (*chip, c), sibling)
                cp.start()
                passed.append(cp)
        for i in range(n):
            copy(i, 0, sibling, me).wait_recv()
            for q, chip in enumerate(chips):
                copy(i, 4 + q, (*chip, 1 - c), me).wait_recv()
        for cp in first + passed:
            cp.wait_send()
        for cp in mine:
            cp.wait()

    outs = pl.pallas_call(
        body, name=name, out_shape=[_sds((N_DEV,) + s.shape, s.dtype) for s in shards],
        in_specs=[_ANY] * n, out_specs=[_ANY] * n,
        scratch_shapes=[pltpu.SemaphoreType.DMA((n, 7)), pltpu.SemaphoreType.DMA((n, 7)),
                        pltpu.SemaphoreType.DMA((n,))],
    )(*shards)
    return list(outs)


def rs_core_exchange(parts, name, after=()):
    n, na = len(parts), len(after)

    def body(*refs):
        ins, outs = refs[:n], refs[n + na:2 * n + na]
        send_sems, recv_sems = refs[2 * n + na:]
        x, y, c = lax.axis_index("x"), lax.axis_index("y"), lax.axis_index("c")
        sibling = (x, y, 1 - c)
        started = []
        for i in range(n):
            for chip in range(4):
                cp = pltpu.make_async_remote_copy(src_ref=ins[i].at[2 * chip + (1 - c)], dst_ref=outs[i].at[chip],
                                                  send_sem=send_sems.at[i, chip], recv_sem=recv_sems.at[i, chip],
                                                  device_id=sibling, device_id_type=_MESH)
                cp.start()
                started.append(cp)
        for cp in started:
            cp.wait()

    outs = pl.pallas_call(
        body, name=name, out_shape=[_sds((4,) + p.shape[1:], p.dtype) for p in parts],
        in_specs=[_ANY] * (n + na), out_specs=[_ANY] * n,
        scratch_shapes=[pltpu.SemaphoreType.DMA((n, 4)), pltpu.SemaphoreType.DMA((n, 4))],
    )(*parts, *after)
    return list(outs)


def rs_chip_exchange(sums, name):
    n = len(sums)

    def body(*refs):
        ins, outs = refs[:n], refs[n:2 * n]
        send_sems, recv_sems = refs[2 * n:]
        x, y, c = lax.axis_index("x"), lax.axis_index("y"), lax.axis_index("c")
        peers = [(x, 1 - y), (1 - x, y), (1 - x, 1 - y)]
        started = []
        for i in range(n):
            for k, (px, py) in enumerate(peers):
                cp = pltpu.make_async_remote_copy(src_ref=ins[i].at[2 * px + py], dst_ref=outs[i].at[k],
                                                  send_sem=send_sems.at[i, k], recv_sem=recv_sems.at[i, k],
                                                  device_id=(px, py, c), device_id_type=_MESH)
                cp.start()
                started.append(cp)
        for cp in started:
            cp.wait()

    outs = pl.pallas_call(
        body, name=name, out_shape=[_sds((3,) + s.shape[1:], s.dtype) for s in sums],
        in_specs=[_ANY] * n, out_specs=[_ANY] * n,
        scratch_shapes=[pltpu.SemaphoreType.DMA((n, 3)), pltpu.SemaphoreType.DMA((n, 3))],
    )(*sums)
    return list(outs)


_HBM = pl.BlockSpec(memory_space=pltpu.HBM)
_SEM = pl.BlockSpec(memory_space=pltpu.SEMAPHORE)
_EFFECT = pltpu.SideEffectType.DATAFLOW_SIDE_EFFECTING


def _in_hbm(a):
    return pltpu.with_memory_space_constraint(a, pltpu.HBM)


def _plan_copies(plan, srcs, dsts, send_sems, recv_sems):
    x, y, c = lax.axis_index("x"), lax.axis_index("y"), lax.axis_index("c")
    return [pltpu.make_async_remote_copy(src_ref=s, dst_ref=d, send_sem=send_sems.at[q], recv_sem=recv_sems.at[q],
                                         device_id=to, device_id_type=_MESH)
            for q, (s, d, to) in enumerate(plan(x, y, c, srcs, dsts))]


def exchange_start(plan, n_copies, srcs, dsts, after, name):
    ns, nd, na = len(srcs), len(dsts), len(after)

    def body(*refs):
        ins = refs[:ns + nd + na]
        send_sems, recv_sems = refs[ns + nd + na], refs[ns + nd + na + 1]
        token = refs[-1]
        for cp in _plan_copies(plan, ins[:ns], ins[ns:ns + nd], send_sems, recv_sems):
            cp.start()
        token[...] = jnp.zeros_like(token)

    thru = [pltpu.HBM(a.shape, a.dtype) for a in list(srcs) + list(dsts)]
    out = pl.pallas_call(
        body, name=name,
        out_shape=[pltpu.SemaphoreType.DMA((n_copies,)), pltpu.SemaphoreType.DMA((n_copies,))] + thru
        + [_sds((SUBLANES, LANES), F32)],
        in_specs=[_HBM] * (ns + nd) + [_ANY] * na,
        out_specs=[_SEM, _SEM] + [_HBM] * (ns + nd) + [pl.BlockSpec(memory_space=pltpu.VMEM)],
        input_output_aliases={i: 2 + i for i in range(ns + nd)},
        compiler_params=pltpu.CompilerParams(has_side_effects=_EFFECT),
    )(*[_in_hbm(a) for a in list(srcs) + list(dsts)], *after)
    return out[0], out[1], list(out[2:2 + ns]), list(out[2 + ns:2 + ns + nd]), out[-1]


def exchange_wait(plan, started, after, name):
    send_sems, recv_sems, srcs, dsts, _ = started
    ns, nd, na = len(srcs), len(dsts), len(after)

    def body(*refs):
        ins = refs[:ns + nd]
        send_sems_ref, recv_sems_ref = refs[ns + nd], refs[ns + nd + 1]
        for cp in _plan_copies(plan, ins[:ns], ins[ns:ns + nd], send_sems_ref, recv_sems_ref):
            cp.wait_send()
            cp.wait_recv()

    thru = [pltpu.HBM(a.shape, a.dtype) for a in list(srcs) + list(dsts)]
    out = pl.pallas_call(
        body, name=name, out_shape=thru,
        in_specs=[_HBM] * (ns + nd) + [_SEM, _SEM] + [_ANY] * na,
        out_specs=[_HBM] * (ns + nd),
        input_output_aliases={i: i for i in range(ns + nd)},
        compiler_params=pltpu.CompilerParams(has_side_effects=_EFFECT),
    )(*srcs, *dsts, send_sems, recv_sems, *after)
    return list(out[:ns]), list(out[ns:])


def _dev_index(px, py, pc):
    return 4 * px + 2 * py + pc


def _ag_plan_first(x, y, c, srcs, dsts):
    me = _dev_index(x, y, c)
    peers = [(x, y, 1 - c), (1 - x, y, c), (x, 1 - y, c), (1 - x, 1 - y, c)]
    return [(s, d.at[me], to) for s, d in zip(srcs, dsts) for to in peers]


def _ag_plan_pass(x, y, c, srcs, dsts):
    chips = [(1 - x, y), (x, 1 - y), (1 - x, 1 - y)]
    return [(d.at[_dev_index(px, py, c)], d.at[_dev_index(px, py, c)], (x, y, 1 - c)) for d in dsts for px, py in chips]


def _rs_plan_cores(x, y, c, srcs, dsts):
    return [(s.at[2 * chip + (1 - c)], d.at[chip], (x, y, 1 - c)) for s, d in zip(srcs, dsts) for chip in range(4)]


def _rs_plan_chips(x, y, c, srcs, dsts):
    peers = [(x, 1 - y), (1 - x, y), (1 - x, 1 - y)]
    return [(s.at[2 * px + py], d.at[k], (px, py, c)) for s, d in zip(srcs, dsts) for k, (px, py) in enumerate(peers)]


def _row_tile(r, c, itemsize=4, budget=1 << 20):
    cap = max(SUBLANES, (budget // (c * itemsize)) // SUBLANES * SUBLANES)
    if r <= cap:
        return r
    best = 0
    for t in range(SUBLANES, cap + 1, SUBLANES):
        if r % t == 0:
            best = t
    return best if best else r


def rs_core_add(part, recv, core, name):
    _, r, c = part.shape
    tr = _row_tile(r, c, budget=1 << 22)

    def body(core_ref, p_ref, q_ref, o_ref):
        o_ref[...] = (p_ref[...].astype(F32) + q_ref[...].astype(F32)).astype(o_ref.dtype)

    in_specs = [pl.BlockSpec((None, tr, c), lambda ch, ri, core_ref: (2 * ch + core_ref[0], ri, 0)),
                pl.BlockSpec((None, tr, c), lambda ch, ri, core_ref: (ch, ri, 0))]
    out_spec = pl.BlockSpec((None, tr, c), lambda ch, ri, core_ref: (ch, ri, 0))
    return _call(body, name, (4, r // tr), in_specs, out_spec, _sds((4, r, c), part.dtype), prefetch=1)(core, part, recv)


def _adamw(w, g, m, v):
    m = ADAM_B1 * m + (1.0 - ADAM_B1) * g
    v = ADAM_B2 * v + (1.0 - ADAM_B2) * (g * g)
    m_hat = m / (1.0 - ADAM_B1 ** ADAM_STEP)
    v_hat = v / (1.0 - ADAM_B2 ** ADAM_STEP)
    delta = -ADAM_LR * (m_hat / (jnp.sqrt(v_hat) + ADAM_EPS) + ADAM_WD * w)
    return delta, m, v


def rs_finish(sums, recv, chip, name, adam=None):
    _, r, c = sums.shape
    tr = _row_tile(r, c, budget=1 << 20)

    def body(chip_ref, s_ref, q_ref, *refs):
        g = s_ref[...].astype(F32)
        for k in range(3):
            g = g + q_ref[k].astype(F32)
        if adam is None:
            refs[0][...] = g
        else:
            w_ref, m_ref, v_ref, g_ref, d_ref, nm_ref, nv_ref = refs
            g_ref[...] = g
            d_ref[...], nm_ref[...], nv_ref[...] = _adamw(w_ref[...], g, m_ref[...], v_ref[...])

    blk = pl.BlockSpec((tr, c), lambda ri, chip_ref: (ri, 0))
    in_specs = [pl.BlockSpec((None, tr, c), lambda ri, chip_ref: (chip_ref[0], ri, 0)),
                pl.BlockSpec((3, tr, c), lambda ri, chip_ref: (0, ri, 0))]
    args = [chip, sums, recv]
    if adam is None:
        out_specs, out_shape = blk, _sds((r, c), F32)
    else:
        in_specs += [blk] * 3
        args += list(adam)
        out_specs, out_shape = [blk] * 4, [_sds((r, c), F32)] * 4
    return _call(body, name, (r // tr,), in_specs, out_specs, out_shape, prefetch=1)(*args)


def adam_flat(w, g, m, v, name):
    r, c = w.shape
    tr = _row_tile(r, c, budget=1 << 19)

    def body(w_ref, g_ref, m_ref, v_ref, d_ref, nm_ref, nv_ref):
        d_ref[...], nm_ref[...], nv_ref[...] = _adamw(w_ref[...], g_ref[...], m_ref[...], v_ref[...])

    blk = pl.BlockSpec((tr, c), lambda ri: (ri, 0))
    return _call(body, name, (r // tr,), [blk] * 4, [blk] * 3, [_sds((r, c), F32)] * 3)(w, g, m, v)


def rms_fwd(x, g, name, add=None):
    t, d = x.shape
    tm = _tile(t, 256, SUBLANES)
    has_add = add is not None

    def body(*refs):
        if has_add:
            x_ref, a_ref, g_ref, h_ref, hn_ref = refs
            h = x_ref[...] + a_ref[...]
            h_ref[...] = h
        else:
            x_ref, g_ref, hn_ref = refs
            h = x_ref[...]
        r = lax.rsqrt(jnp.mean(h * h, axis=-1, keepdims=True) + NORM_EPS)
        hn_ref[...] = (h * r * g_ref[...]).astype(BF16)

    row = pl.BlockSpec((tm, d), lambda i: (i, 0))
    vec = pl.BlockSpec((1, d), lambda i: (0, 0))
    if has_add:
        return _call(body, name, (t // tm,), [row, row, vec], [row, row], [_sds((t, d), F32), _sds((t, d), BF16)])(
            x, add, g.reshape(1, d))
    return _call(body, name, (t // tm,), [row, vec], row, _sds((t, d), BF16))(x, g.reshape(1, d))


def _rms_bwd_math(x, g, dy):
    r = lax.rsqrt(jnp.mean(x * x, axis=-1, keepdims=True) + NORM_EPS)
    xh = x * r
    dyg = dy * g
    dx = r * (dyg - xh * jnp.mean(dyg * xh, axis=-1, keepdims=True))
    return dx, jnp.sum(dy * xh, axis=0, keepdims=True)


def rms_bwd(x, g, dhn, dres, name):
    t, d = x.shape
    tm = _tile(t, 256, SUBLANES)

    def body(x_ref, g_ref, dy_ref, dr_ref, dx_ref, dxb_ref, dg_ref):
        dx, dg = _rms_bwd_math(x_ref[...], g_ref[...], dy_ref[...])
        dx = dx + dr_ref[...]
        dx_ref[...] = dx
        dxb_ref[...] = dx.astype(BF16)

        @pl.when(pl.program_id(0) == 0)
        def _():
            dg_ref[...] = jnp.zeros_like(dg_ref)

        dg_ref[...] += dg

    row = pl.BlockSpec((tm, d), lambda i: (i, 0))
    vec = pl.BlockSpec((1, d), lambda i: (0, 0))
    return _call(body, name, (t // tm,), [row, vec, row, row], [row, row, vec],
                 [_sds((t, d), F32), _sds((t, d), BF16), _sds((1, d), F32)])(x, g.reshape(1, d), dhn, dres)


def final_loss(h, add, g, target, name):
    t, d = h.shape
    tm = _tile(t, 256, SUBLANES)

    def body(x_ref, a_ref, g_ref, t_ref, loss_ref, dx_ref, dxb_ref, dg_ref):
        x = x_ref[...] + a_ref[...]
        gain = g_ref[...]
        r = lax.rsqrt(jnp.mean(x * x, axis=-1, keepdims=True) + NORM_EPS)
        err = x * r * gain - t_ref[...]
        dx, dg = _rms_bwd_math(x, gain, err * (1.0 / d))
        dx_ref[...] = dx
        dxb_ref[...] = dx.astype(BF16)

        @pl.when(pl.program_id(0) == 0)
        def _():
            dg_ref[...] = jnp.zeros_like(dg_ref)
            loss_ref[...] = jnp.zeros_like(loss_ref)

        dg_ref[...] += dg
        part = jnp.sum(jnp.sum(err * err, axis=-1, keepdims=True), axis=0, keepdims=True) * (0.5 / d)
        loss_ref[...] += jnp.broadcast_to(part, loss_ref.shape)

    row = pl.BlockSpec((tm, d), lambda i: (i, 0))
    vec = pl.BlockSpec((1, d), lambda i: (0, 0))
    one = pl.BlockSpec((1, LANES), lambda i: (0, 0))
    return _call(body, name, (t // tm,), [row, row, vec, row], [one, row, row, vec],
                 [_sds((1, LANES), F32), _sds((t, d), F32), _sds((t, d), BF16), _sds((1, d), F32)])(
        h, add, g.reshape(1, d), target)


def ffn_gate_fwd(z2, dw_w2, dw_b2, name, tc=2 * LANES):
    _, b, s, f = z2.shape

    def body(z_ref, w_ref, b_ref, y_ref):
        g = _conv_causal(z_ref[0], lambda k: w_ref[0, k:k + 1, :], FFN_WIDTH) + b_ref[0]
        a = _conv_causal(z_ref[1], lambda k: w_ref[1, k:k + 1, :], FFN_WIDTH) + b_ref[1]
        y_ref[...] = (g * _sigmoid(g) * a).astype(BF16)

    in_specs = [pl.BlockSpec((2, None, s, tc), lambda bi, ci: (0, bi, 0, ci)),
                pl.BlockSpec((2, FFN_WIDTH, tc), lambda bi, ci: (0, 0, ci)),
                pl.BlockSpec((2, 1, tc), lambda bi, ci: (0, 0, ci))]
    out_spec = pl.BlockSpec((None, s, tc), lambda bi, ci: (bi, 0, ci))
    return _call(body, name, (b, f // tc), in_specs, out_spec, _sds((b, s, f), BF16))(z2, dw_w2, dw_b2)


def ffn_gate_bwd(z2, dy, dw_w2, dw_b2, name, tc=2 * LANES):
    _, b, s, f = z2.shape

    def body(z_ref, dy_ref, w_ref, b_ref, dz_ref, dw_ref, db_ref):
        @pl.when(pl.program_id(1) == 0)
        def _():
            dw_ref[...] = jnp.zeros_like(dw_ref)
            db_ref[...] = jnp.zeros_like(db_ref)

        zs = (z_ref[0], z_ref[1])
        g = _conv_causal(zs[0], lambda k: w_ref[0, k:k + 1, :], FFN_WIDTH) + b_ref[0]
        a = _conv_causal(zs[1], lambda k: w_ref[1, k:k + 1, :], FFN_WIDTH) + b_ref[1]
        sg = _sigmoid(g)
        dy = dy_ref[...]
        dcs = (dy * a * (sg * (1.0 + g * (1.0 - sg))), dy * (g * sg))
        for p in range(2):
            dz, dws, db = _conv_causal_bwd(zs[p], dcs[p], lambda k, p=p: w_ref[p, k:k + 1, :], FFN_WIDTH)
            dz_ref[p] = dz.astype(BF16)
            for k in range(FFN_WIDTH):
                dw_ref[p, k:k + 1, :] += dws[k]
            db_ref[p] += db

    in_specs = [pl.BlockSpec((2, None, s, tc), lambda ci, bi: (0, bi, 0, ci)),
                pl.BlockSpec((None, s, tc), lambda ci, bi: (bi, 0, ci)),
                pl.BlockSpec((2, FFN_WIDTH, tc), lambda ci, bi: (0, 0, ci)),
                pl.BlockSpec((2, 1, tc), lambda ci, bi: (0, 0, ci))]
    out_specs = [pl.BlockSpec((2, None, s, tc), lambda ci, bi: (0, bi, 0, ci)),
                 pl.BlockSpec((2, FFN_WIDTH, tc), lambda ci, bi: (0, 0, ci)),
                 pl.BlockSpec((2, 1, tc), lambda ci, bi: (0, 0, ci))]
    out_shape = [_sds((2, b, s, f), BF16), _sds((2, FFN_WIDTH, f), F32), _sds((2, 1, f), F32)]
    return _call(body, name, (f // tc, b), in_specs, out_specs, out_shape)(z2, dy, dw_w2, dw_b2)


def cc_conv_fwd(z2, dw_w, dw_b, name, tc=LANES):
    _, b, s, c = z2.shape

    def body(z_ref, w_ref, b_ref, o_ref):
        u = z_ref[0] * _sigmoid(z_ref[1])
        o_ref[...] = _conv_causal(u, lambda k: w_ref[k:k + 1, :], CC_WIDTH) + b_ref[...]

    in_specs = [pl.BlockSpec((2, None, s, tc), lambda bi, ci: (0, bi, 0, ci)),
                pl.BlockSpec((CC_WIDTH, tc), lambda bi, ci: (0, ci)),
                pl.BlockSpec((1, tc), lambda bi, ci: (0, ci))]
    out_spec = pl.BlockSpec((None, s, tc), lambda bi, ci: (bi, 0, ci))
    return _call(body, name, (b, c // tc), in_specs, out_spec, _sds((b, s, c), F32))(z2, dw_w, dw_b.reshape(1, c))


def cc_conv_bwd(z2, dcv, dw_w, name, tc=LANES):
    _, b, s, c = z2.shape

    def body(z_ref, dc_ref, w_ref, dz_ref, dbi_ref, dw_ref, db_ref):
        @pl.when(pl.program_id(1) == 0)
        def _():
            dbi_ref[...] = jnp.zeros_like(dbi_ref)
            dw_ref[...] = jnp.zeros_like(dw_ref)
            db_ref[...] = jnp.zeros_like(db_ref)

        a, gate = z_ref[0], z_ref[1]
        sg = _sigmoid(gate)
        u = a * sg
        du, dws, db = _conv_causal_bwd(u, dc_ref[...], lambda k: w_ref[k:k + 1, :], CC_WIDTH)
        for k in range(CC_WIDTH):
            dw_ref[k:k + 1, :] += dws[k]
        db_ref[...] += db
        da = du * sg
        dg = du * a * sg * (1.0 - sg)
        dz_ref[0] = da.astype(BF16)
        dz_ref[1] = dg.astype(BF16)
        dbi_ref[0] += jnp.sum(da, axis=0, keepdims=True)
        dbi_ref[1] += jnp.sum(dg, axis=0, keepdims=True)

    in_specs = [pl.BlockSpec((2, None, s, tc), lambda ci, bi: (0, bi, 0, ci)),
                pl.BlockSpec((None, s, tc), lambda ci, bi: (bi, 0, ci)),
                pl.BlockSpec((CC_WIDTH, tc), lambda ci, bi: (0, ci))]
    out_specs = [pl.BlockSpec((2, None, s, tc), lambda ci, bi: (0, bi, 0, ci)),
                 pl.BlockSpec((2, 1, tc), lambda ci, bi: (0, 0, ci)),
                 pl.BlockSpec((CC_WIDTH, tc), lambda ci, bi: (0, ci)),
                 pl.BlockSpec((1, tc), lambda ci, bi: (0, ci))]
    out_shape = [_sds((2, b, s, c), BF16), _sds((2, 1, c), F32), _sds((CC_WIDTH, c), F32), _sds((1, c), F32)]
    return _call(body, name, (c // tc, b), in_specs, out_specs, out_shape)(z2, dcv, dw_w)


def _ln_stats(v):
    mu = jnp.mean(v, axis=-1, keepdims=True)
    vc = v - mu
    r = lax.rsqrt(jnp.mean(vc * vc, axis=-1, keepdims=True) + NORM_EPS)
    return vc * r, r


def _ln_bwd(dln, xh, r, g):
    dxh = dln * g
    dx = r * (dxh - jnp.mean(dxh, axis=-1, keepdims=True) - xh * jnp.mean(dxh * xh, axis=-1, keepdims=True))
    return dx, jnp.sum(dln * xh, axis=0, keepdims=True), jnp.sum(dln, axis=0, keepdims=True)


def cc_ln_fwd(cv, ln_g, ln_b, name):
    t, c = cv.shape
    tm = _tile(t, 256, SUBLANES)

    def body(x_ref, g_ref, b_ref, o_ref):
        xh, _ = _ln_stats(x_ref[...])
        ln = xh * g_ref[...] + b_ref[...]
        o_ref[...] = (ln * _sigmoid(ln)).astype(BF16)

    row = pl.BlockSpec((tm, c), lambda i: (i, 0))
    vec = pl.BlockSpec((1, c), lambda i: (0, 0))
    return _call(body, name, (t // tm,), [row, vec, vec], row, _sds((t, c), BF16))(
        cv, ln_g.reshape(1, c), ln_b.reshape(1, c))


def cc_ln_bwd(cv, ln_g, ln_b, ds, name):
    t, c = cv.shape
    tm = _tile(t, 256, SUBLANES)

    def body(x_ref, g_ref, b_ref, ds_ref, dx_ref, dg_ref, db_ref):
        @pl.when(pl.program_id(0) == 0)
        def _():
            dg_ref[...] = jnp.zeros_like(dg_ref)
            db_ref[...] = jnp.zeros_like(db_ref)

        xh, r = _ln_stats(x_ref[...])
        ln = xh * g_ref[...] + b_ref[...]
        sg = _sigmoid(ln)
        dln = ds_ref[...] * (sg * (1.0 + ln * (1.0 - sg)))
        dx, dg, db = _ln_bwd(dln, xh, r, g_ref[...])
        dx_ref[...] = dx
        dg_ref[...] += dg
        db_ref[...] += db

    row = pl.BlockSpec((tm, c), lambda i: (i, 0))
    vec = pl.BlockSpec((1, c), lambda i: (0, 0))
    return _call(body, name, (t // tm,), [row, vec, vec, row], [row, vec, vec],
                 [_sds((t, c), F32), _sds((1, c), F32), _sds((1, c), F32)])(
        cv, ln_g.reshape(1, c), ln_b.reshape(1, c), ds)


def col_sum(v, name):
    t, n = v.shape
    tm = _tile(t, 256, SUBLANES)

    def body(v_ref, o_ref):
        @pl.when(pl.program_id(0) == 0)
        def _():
            o_ref[...] = jnp.zeros_like(o_ref)

        o_ref[...] += jnp.sum(v_ref[...], axis=0, keepdims=True)

    return _call(body, name, (t // tm,), [pl.BlockSpec((tm, n), lambda i: (i, 0))],
                 pl.BlockSpec((1, n), lambda i: (0, 0)), _sds((1, n), F32))(v)


def rope_tables_full(seq, dim):
    pos = jnp.arange(seq, dtype=F32)
    inv = ROPE_THETA ** (-(jnp.arange(0, dim, 2, dtype=F32) / dim))
    ang = pos[:, None] * inv[None, :]
    cos, sin = jnp.cos(ang), jnp.sin(ang)
    return jnp.concatenate([cos, cos], axis=-1), jnp.concatenate([-sin, sin], axis=-1)


def dsa_rope_fwd(qkv, cos, sin, seq, name):
    t, w = qkv.shape
    tm = _tile(seq, 256, SUBLANES)
    nsb = seq // tm
    hb = DSA_HEADS

    def body(x_ref, c_ref, s_ref, o_ref):
        cos_t, sin_t = c_ref[...], s_ref[...]
        for blk in range(w // LANES):
            v = x_ref[:, blk * LANES:(blk + 1) * LANES]
            if (blk // hb) % 3 < 2:
                v = _rope(v, cos_t, sin_t)
            o_ref[:, blk * LANES:(blk + 1) * LANES] = v

    row = pl.BlockSpec((tm, w), lambda i: (i, 0))
    tab = pl.BlockSpec((tm, LANES), lambda i: (i % nsb, 0))
    return _call(body, name, (t // tm,), [row, tab, tab], row, _sds((t, w), F32))(qkv, cos, sin)


def _dsa_rows(r, n, dil):
    start = r + n * DSA_BLOCK * dil
    return pl.ds(start, DSA_BLOCK) if dil == 1 else pl.ds(start, DSA_BLOCK, stride=dil)


def _dsa_load(ref, r, blocks, dil):
    parts = [ref[_dsa_rows(r, m, dil), :] for m in blocks]
    return parts[0] if len(parts) == 1 else jnp.concatenate(parts, axis=0)


def dsa_attn_all_fwd(qkv_r, batch, seq, name):
    t, w = qkv_r.shape
    h_n, n_g = DSA_HEADS, len(DSA_CONFIGS)
    scale = DSA_BLOCK ** -0.5
    chunk = _tile(seq, 256, SUBLANES)

    def body(q_ref, k_ref, v_ref, o_ref, ob_ref, lse_ref, og_ref, lg_ref):
        g = pl.program_id(2)
        for gi, (_, dil) in enumerate(DSA_CONFIGS):
            @pl.when(g == gi)
            def _(gi=gi, dil=dil):
                for r in range(dil):
                    for n in range(seq // dil // DSA_BLOCK):
                        blocks = (n - 1, n) if n else (n,)
                        rows = _dsa_rows(r, n, dil)
                        q = q_ref[rows, :].astype(BF16)
                        k = _dsa_load(k_ref, r, blocks, dil).astype(BF16)
                        v = _dsa_load(v_ref, r, blocks, dil).astype(BF16)
                        s = jnp.where(_dsa_mask(n), _dot_nt(q, k) * scale, NEG_INF)
                        m = jnp.max(s, axis=-1, keepdims=True)
                        p = jnp.exp(s - m)
                        l = jnp.sum(p, axis=-1, keepdims=True)
                        og_ref[gi, rows, :] = _dot(p.astype(BF16), v) / l
                        lg_ref[gi, rows, :] = jnp.broadcast_to(m + jnp.log(l), (DSA_BLOCK, LANES))

        @pl.when(g == n_g - 1)
        def _():
            for c0 in range(0, seq, chunk):
                sl = slice(c0, c0 + chunk)
                ls = [lg_ref[gi, sl, :] for gi in range(n_g)]
                m = functools.reduce(jnp.maximum, ls)
                es = [jnp.exp(v - m) for v in ls]
                tot = functools.reduce(lambda a, b: a + b, es)
                acc = jnp.zeros_like(m)
                for gi in range(n_g):
                    acc = acc + (es[gi] / tot) * og_ref[gi, sl, :]
                o_ref[sl, :] = acc
                ob_ref[sl, :] = acc.astype(BF16)
                lse_ref[sl, :] = m + jnp.log(tot)

    def spec(off):
        return pl.BlockSpec((None, seq, LANES), lambda b, h, g: (b, 0, (3 * g + off) * h_n + h))

    out_spec = pl.BlockSpec((None, seq, LANES), lambda b, h, g: (b, 0, h))
    view = qkv_r.reshape(batch, seq, w)
    shape = (batch, seq, h_n * LANES)
    o, o_b, lse = _call(body, name, (batch, h_n, n_g), [spec(0), spec(1), spec(2)], [out_spec] * 3,
                        [_sds(shape, F32), _sds(shape, BF16), _sds(shape, F32)],
                        scratch=[pltpu.VMEM((n_g, seq, LANES), F32)] * 2)(view, view, view)
    return o.reshape(t, -1), o_b.reshape(t, -1), lse.reshape(t, -1)


def dsa_attn_all_bwd(qkv_r, do, lse, delta, batch, seq, name):
    t, w = qkv_r.shape
    h_n, n_g = DSA_HEADS, len(DSA_CONFIGS)
    scale = DSA_BLOCK ** -0.5

    def body(q_ref, k_ref, v_ref, do_ref, lse_ref, dl_ref, d_ref):
        g = pl.program_id(2)
        d_ref[...] = jnp.zeros_like(d_ref)
        for gi, (_, dil) in enumerate(DSA_CONFIGS):
            @pl.when(g == gi)
            def _(dil=dil):
                for r in range(dil):
                    for n in range(seq // dil // DSA_BLOCK):
                        blocks = (n - 1, n) if n else (n,)
                        rows = _dsa_rows(r, n, dil)
                        q = q_ref[rows, :].astype(BF16)
                        k = _dsa_load(k_ref, r, blocks, dil).astype(BF16)
                        v = _dsa_load(v_ref, r, blocks, dil).astype(BF16)
                        dout = do_ref[rows, :].astype(BF16)
                        s = jnp.where(_dsa_mask(n), _dot_nt(q, k) * scale, NEG_INF)
                        p = jnp.exp(s - lse_ref[rows, :][:, 0:1])
                        ds = (p * (_dot_nt(dout, v) - dl_ref[rows, :][:, 0:1]) * scale).astype(BF16)
                        d_ref[0, rows, :] = _dot(ds, k)
                        dk = _dot_tn(ds, q)
                        dv = _dot_tn(p.astype(BF16), dout)
                        for i, blk in enumerate(blocks):
                            brows = _dsa_rows(r, blk, dil)
                            d_ref[1, brows, :] += dk[i * DSA_BLOCK:(i + 1) * DSA_BLOCK]
                            d_ref[2, brows, :] += dv[i * DSA_BLOCK:(i + 1) * DSA_BLOCK]

    def spec(off):
        return pl.BlockSpec((None, seq, LANES), lambda b, h, g: (b, 0, (3 * g + off) * h_n + h))

    head = pl.BlockSpec((None, seq, LANES), lambda b, h, g: (b, 0, h))
    out_spec = pl.BlockSpec((3, None, seq, LANES), lambda b, h, g: (g, b, 0, h))
    view = qkv_r.reshape(batch, seq, w)
    hv = [a.reshape(batch, seq, h_n * LANES) for a in (do, lse, delta)]
    out = _call(body, name, (batch, h_n, n_g), [spec(0), spec(1), spec(2), head, head, head], out_spec,
                _sds((3 * n_g, batch, seq, h_n * LANES), F32))(view, view, view, *hv)
    return out.reshape(3 * n_g, t, h_n * LANES)


def dsa_rope_bwd_all(grads, cos, sin, seq, name):
    n_parts, t, hw = grads.shape
    w = hw * n_parts
    tm = _tile(seq, 128, SUBLANES)
    nsb = seq // tm

    def body(g_ref, c_ref, s_ref, o_ref):
        cos_t, sin_t = c_ref[...], s_ref[...]
        for idx in range(n_parts):
            for h in range(hw // LANES):
                v = g_ref[idx, :, h * LANES:(h + 1) * LANES]
                if idx % 3 < 2:
                    v = _rope_t(v, cos_t, sin_t)
                col = idx * hw + h * LANES
                o_ref[:, col:col + LANES] = v.astype(BF16)

    tab = pl.BlockSpec((tm, LANES), lambda i: (i % nsb, 0))
    out = _call(body, name, (t // tm,), [pl.BlockSpec((n_parts, tm, hw), lambda i: (0, i, 0)), tab, tab],
                pl.BlockSpec((tm, w), lambda i: (i, 0)), _sds((t, w), BF16))(grads, cos, sin)
    return out[None]


def _dsa_mask(n):
    qi = lax.broadcasted_iota(jnp.int32, (DSA_BLOCK, (1 if n == 0 else 2) * DSA_BLOCK), 0)
    kj = lax.broadcasted_iota(jnp.int32, (DSA_BLOCK, (1 if n == 0 else 2) * DSA_BLOCK), 1)
    if n == 0:
        return qi >= kj
    dist = DSA_BLOCK + qi - kj
    return (dist >= 0) & (dist <= DSA_BLOCK)


def head_delta(do, o, name):
    t, hw = do.shape
    tm = _tile(t, 256, SUBLANES)

    def body(do_ref, o_ref, d_ref):
        for h in range(hw // LANES):
            sl = slice(h * LANES, (h + 1) * LANES)
            d = jnp.sum(do_ref[:, sl] * o_ref[:, sl], axis=-1, keepdims=True)
            d_ref[:, sl] = jnp.broadcast_to(d, (tm, LANES))

    row = pl.BlockSpec((tm, hw), lambda i: (i, 0))
    return _call(body, name, (t // tm,), [row, row], row, _sds((t, hw), F32))(do, o)


def mla_rope_tables(seq):
    pos = jnp.arange(seq, dtype=F32)
    inv = ROPE_THETA ** (-(jnp.arange(0, MLA_ROPE, 2, dtype=F32) / MLA_ROPE))
    ang = pos[:, None] * inv[None, :]
    cos, sin, zero = jnp.cos(ang), jnp.sin(ang), jnp.zeros_like(ang)
    return jnp.concatenate([cos, zero, cos, zero], axis=-1), jnp.concatenate([-sin, zero, sin, zero], axis=-1)


def _spread_rope_cols(w_pe):
    half = MLA_ROPE // 2
    zero = jnp.zeros(w_pe.shape[:-1] + (half,), w_pe.dtype)
    return jnp.concatenate([w_pe[..., :half], zero, w_pe[..., half:], zero], axis=-1)


def _gather_rope_cols(g_pe):
    half = MLA_ROPE // 2
    return jnp.concatenate([g_pe[..., :half], g_pe[..., 2 * half:3 * half]], axis=-1)


def mla_low_fwd(c, q_norm, kv_norm, cos, sin, seq, name):
    t, w = c.shape
    rk = MLA_RANK
    tm = _tile(seq, 256, SUBLANES)
    nsb = seq // tm

    def body(c_ref, qg_ref, kg_ref, cs_ref, sn_ref, qn_ref, kn_ref, kp_ref):
        for lo, g_ref, o_ref in ((0, qg_ref, qn_ref), (rk, kg_ref, kn_ref)):
            v = c_ref[:, lo:lo + rk]
            r = lax.rsqrt(jnp.mean(v * v, axis=-1, keepdims=True) + NORM_EPS)
            o_ref[...] = (v * r * g_ref[...]).astype(BF16)
        kp_ref[...] = _rope(c_ref[:, 2 * rk:], cs_ref[...], sn_ref[...]).astype(BF16)

    row = lambda n: pl.BlockSpec((tm, n), lambda i: (i, 0))
    vec = pl.BlockSpec((1, rk), lambda i: (0, 0))
    tab = pl.BlockSpec((tm, LANES), lambda i: (i % nsb, 0))
    return _call(body, name, (t // tm,), [row(w), vec, vec, tab, tab], [row(rk), row(rk), row(LANES)],
                 [_sds((t, rk), BF16), _sds((t, rk), BF16), _sds((t, LANES), BF16)])(
        c, q_norm.reshape(1, rk), kv_norm.reshape(1, rk), cos, sin)


def mla_low_bwd(c, q_norm, kv_norm, dqn, dkn, dkp, cos, sin, seq, name):
    t, w = c.shape
    rk = MLA_RANK
    tm = _tile(seq, 256, SUBLANES)
    nsb = seq // tm

    def body(c_ref, qg_ref, kg_ref, dq_ref, dk_ref, dp_ref, cs_ref, sn_ref, dc_ref, dqg_ref, dkg_ref):
        @pl.when(pl.program_id(0) == 0)
        def _():
            dqg_ref[...] = jnp.zeros_like(dqg_ref)
            dkg_ref[...] = jnp.zeros_like(dkg_ref)

        for lo, g_ref, d_ref, dg_ref in ((0, qg_ref, dq_ref, dqg_ref), (rk, kg_ref, dk_ref, dkg_ref)):
            dx, dg = _rms_bwd_math(c_ref[:, lo:lo + rk], g_ref[...], d_ref[...])
            dc_ref[:, lo:lo + rk] = dx.astype(BF16)
            dg_ref[...] += dg
        dc_ref[:, 2 * rk:] = _rope_t(dp_ref[...], cs_ref[...], sn_ref[...]).astype(BF16)

    row = lambda n: pl.BlockSpec((tm, n), lambda i: (i, 0))
    vec = pl.BlockSpec((1, rk), lambda i: (0, 0))
    tab = pl.BlockSpec((tm, LANES), lambda i: (i % nsb, 0))
    dc, dqg, dkg = _call(body, name, (t // tm,), [row(w), vec, vec, row(rk), row(rk), row(LANES), tab, tab],
                         [row(w), vec, vec], [_sds((t, w), BF16), _sds((1, rk), F32), _sds((1, rk), F32)])(
        c, q_norm.reshape(1, rk), kv_norm.reshape(1, rk), dqn, dkn, dkp, cos, sin)
    return dc[None], dqg, dkg


def mla_rope_cast(q, kv, cos, sin, seq, name, transpose=False):
    t, w = q.shape
    tm = _tile(seq, 256, SUBLANES)
    nsb = seq // tm
    fn = _rope_t if transpose else _rope

    def body(q_ref, kv_ref, cs_ref, sn_ref, qo_ref, kvo_ref):
        cos_t, sin_t = cs_ref[...], sn_ref[...]
        for blk in range(w // LANES):
            sl = slice(blk * LANES, (blk + 1) * LANES)
            v = q_ref[:, sl]
            if blk % 2 == 1:
                v = fn(v, cos_t, sin_t)
            qo_ref[:, sl] = v.astype(BF16)
        kvo_ref[...] = kv_ref[...].astype(BF16)

    row = pl.BlockSpec((tm, w), lambda i: (i, 0))
    tab = pl.BlockSpec((tm, LANES), lambda i: (i % nsb, 0))
    return _call(body, name, (t // tm,), [row, row, tab, tab], [row, row], [_sds((t, w), BF16)] * 2)(q, kv, cos, sin)


def _causal_mask(i, j, tq, tk):
    qpos = i * tq + lax.broadcasted_iota(jnp.int32, (tq, tk), 0)
    kpos = j * tk + lax.broadcasted_iota(jnp.int32, (tq, tk), 1)
    return kpos <= qpos


def mla_attn_fwd(q_b, kv_b, kp_b, batch, seq, name, tq=512):
    t, w = q_b.shape
    h_n = w // (2 * LANES)
    tq = _tile(seq, tq, SUBLANES)
    nq = seq // tq
    scale = (MLA_NOPE + MLA_ROPE) ** -0.5

    def body(q_ref, kv_ref, kp_ref, o_ref, lse_ref):
        i = pl.program_id(2)
        qn, qp = q_ref[:, :LANES], q_ref[:, LANES:]

        def step(j, carry):
            m, l, acc = carry
            rows = pl.ds(pl.multiple_of(j * tq, tq), tq)
            s = (_dot_nt(qn, kv_ref[rows, :LANES]) + _dot_nt(qp, kp_ref[rows, :])) * scale
            s = jnp.where(_causal_mask(i, j, tq, tq), s, NEG_INF)
            mn = jnp.maximum(m, jnp.max(s, axis=-1, keepdims=True))
            p = jnp.exp(s - mn)
            a = jnp.exp(m - mn)
            return mn, a * l + jnp.sum(p, axis=-1, keepdims=True), a * acc + _dot(p.astype(BF16), kv_ref[rows, LANES:])

        init = (jnp.full((tq, 1), NEG_INF, F32), jnp.zeros((tq, 1), F32), jnp.zeros((tq, LANES), F32))
        m, l, acc = lax.fori_loop(0, i + 1, step, init)
        o_ref[...] = acc / l
        lse_ref[...] = jnp.broadcast_to(m + jnp.log(l), (tq, LANES))

    in_specs = [pl.BlockSpec((None, tq, 2 * LANES), lambda b, h, i: (b, i, h)),
                pl.BlockSpec((None, seq, 2 * LANES), lambda b, h, i: (b, 0, h)),
                pl.BlockSpec((None, seq, LANES), lambda b, h, i: (b, 0, 0))]
    out_spec = pl.BlockSpec((None, tq, LANES), lambda b, h, i: (b, i, h))
    o, lse = _call(body, name, (batch, h_n, nq), in_specs, [out_spec, out_spec],
                   [_sds((batch, seq, h_n * LANES), F32)] * 2)(
        q_b.reshape(batch, seq, w), kv_b.reshape(batch, seq, w), kp_b.reshape(batch, seq, LANES))
    return o.reshape(t, h_n * LANES), lse.reshape(t, h_n * LANES)


def mla_attn_bwd(q_b, kv_b, kp_b, do, lse, delta, batch, seq, name, tq=512):
    t, w = q_b.shape
    h_n = w // (2 * LANES)
    tq = _tile(seq, tq, SUBLANES)
    nq = seq // tq
    scale = (MLA_NOPE + MLA_ROPE) ** -0.5

    def body(q_ref, kv_ref, kp_ref, do_ref, lse_ref, dl_ref, dq_ref, dkv_ref, dkp_ref):
        dq_ref[...] = jnp.zeros_like(dq_ref)
        dkv_ref[...] = jnp.zeros_like(dkv_ref)

        @pl.when(pl.program_id(1) == 0)
        def _():
            dkp_ref[...] = jnp.zeros_like(dkp_ref)

        def outer(j, _):
            krows = pl.ds(pl.multiple_of(j * tq, tq), tq)
            kn, v, kp = kv_ref[krows, :LANES], kv_ref[krows, LANES:], kp_ref[krows, :]

            def inner(i, _):
                qrows = pl.ds(pl.multiple_of(i * tq, tq), tq)
                qn, qp = q_ref[qrows, :LANES], q_ref[qrows, LANES:]
                dout = do_ref[qrows, :].astype(BF16)
                s = (_dot_nt(qn, kn) + _dot_nt(qp, kp)) * scale
                s = jnp.where(_causal_mask(i, j, tq, tq), s, NEG_INF)
                p = jnp.exp(s - lse_ref[qrows, 0:1])
                ds = (p * (_dot_nt(dout, v) - dl_ref[qrows, 0:1]) * scale).astype(BF16)
                dq_ref[qrows, :LANES] += _dot(ds, kn)
                dq_ref[qrows, LANES:] += _dot(ds, kp)
                dkv_ref[krows, :LANES] += _dot_tn(ds, qn)
                dkv_ref[krows, LANES:] += _dot_tn(p.astype(BF16), dout)
                dkp_ref[krows, :] += _dot_tn(ds, qp)
                return 0

            lax.fori_loop(j, nq, inner, 0)
            return 0

        lax.fori_loop(0, nq, outer, 0)

    wide = pl.BlockSpec((None, seq, 2 * LANES), lambda b, h: (b, 0, h))
    head = pl.BlockSpec((None, seq, LANES), lambda b, h: (b, 0, h))
    shared = pl.BlockSpec((None, seq, LANES), lambda b, h: (b, 0, 0))
    hv = [a.reshape(batch, seq, h_n * LANES) for a in (do, lse, delta)]
    dq, dkv, dkp = _call(body, name, (batch, h_n), [wide, wide, shared, head, head, head], [wide, wide, shared],
                         [_sds((batch, seq, w), F32), _sds((batch, seq, w), F32), _sds((batch, seq, LANES), F32)])(
        q_b.reshape(batch, seq, w), kv_b.reshape(batch, seq, w), kp_b.reshape(batch, seq, LANES), *hv)
    return dq.reshape(t, w), dkv.reshape(t, w), dkp.reshape(t, LANES)


def _sgu_common(z_ref, g_ref, b_ref, ws_ref, bs_ref):
    e = z_ref.shape[-1]
    ge = e // SG_GROUPS
    zu, zv = z_ref[0], z_ref[1]
    cu, cv = _gelu_cdf(zu), _gelu_cdf(zv)
    u = zu * cu
    xh, r = _ln_stats(zv * cv)
    vn = (xh * g_ref[...] + b_ref[...]).astype(BF16)
    tri = (lax.broadcasted_iota(jnp.int32, (SG_CHUNK, SG_CHUNK), 0)
           >= lax.broadcasted_iota(jnp.int32, (SG_CHUNK, SG_CHUNK), 1))
    ws = [jnp.where(tri, ws_ref[g], 0.0).astype(BF16) for g in range(SG_GROUPS)]
    v2 = [_dot(ws[g], vn[:, g * ge:(g + 1) * ge]) + bs_ref[:, g:g + 1] for g in range(SG_GROUPS)]
    return zu, zv, cu, cv, u, xh, r, vn, tri, ws, v2


def sgu_fwd(z2, ln_g, ln_b, w_s, b_s_t, name):
    _, t, e = z2.shape
    ge = e // SG_GROUPS

    def body(z_ref, g_ref, b_ref, ws_ref, bs_ref, y_ref):
        _, _, _, _, u, _, _, _, _, _, v2 = _sgu_common(z_ref, g_ref, b_ref, ws_ref, bs_ref)
        for g in range(SG_GROUPS):
            y_ref[:, g * ge:(g + 1) * ge] = (u[:, g * ge:(g + 1) * ge] * v2[g]).astype(BF16)

    vec = pl.BlockSpec((1, e), lambda i: (0, 0))
    in_specs = [pl.BlockSpec((2, SG_CHUNK, e), lambda i: (0, i, 0)), vec, vec,
                pl.BlockSpec((SG_GROUPS, SG_CHUNK, SG_CHUNK), lambda i: (0, 0, 0)),
                pl.BlockSpec((SG_CHUNK, SG_GROUPS), lambda i: (0, 0))]
    return _call(body, name, (t // SG_CHUNK,), in_specs, pl.BlockSpec((SG_CHUNK, e), lambda i: (i, 0)),
                 _sds((t, e), BF16))(z2, ln_g.reshape(1, e), ln_b.reshape(1, e), w_s, b_s_t)


def sgu_bwd(z2, dy, ln_g, ln_b, w_s, b_s_t, name):
    _, t, e = z2.shape
    ge = e // SG_GROUPS

    def body(z_ref, dy_ref, g_ref, b_ref, ws_ref, bs_ref, dz_ref, dbi_ref, dg_ref, db_ref, dws_ref, dbs_ref):
        @pl.when(pl.program_id(0) == 0)
        def _():
            for ref in (dbi_ref, dg_ref, db_ref, dws_ref, dbs_ref):
                ref[...] = jnp.zeros_like(ref)

        zu, zv, cu, cv, u, xh, r, vn, tri, ws, v2 = _sgu_common(z_ref, g_ref, b_ref, ws_ref, bs_ref)
        dy = dy_ref[...]
        dvn = []
        for g in range(SG_GROUPS):
            sl = slice(g * ge, (g + 1) * ge)
            dyg = dy[:, sl]
            du = dyg * v2[g] * _gelu_grad(zu[:, sl], cu[:, sl])
            dz_ref[0, :, sl] = du.astype(BF16)
            dbi_ref[0, :, sl] += jnp.sum(du, axis=0, keepdims=True)
            dv2 = dyg * u[:, sl]
            dbs_ref[:, g:g + 1] += jnp.sum(dv2, axis=-1, keepdims=True)
            dv2b = dv2.astype(BF16)
            dws_ref[g] += jnp.where(tri, _dot_nt(dv2b, vn[:, sl]), 0.0)
            dvn.append(_dot_tn(ws[g], dv2b))
        dvn = jnp.concatenate(dvn, axis=-1)
        dv, dg, db = _ln_bwd(dvn, xh, r, g_ref[...])
        dzv = dv * _gelu_grad(zv, cv)
        dz_ref[1] = dzv.astype(BF16)
        dbi_ref[1] += jnp.sum(dzv, axis=0, keepdims=True)
        dg_ref[...] += dg
        db_ref[...] += db

    vec = pl.BlockSpec((1, e), lambda i: (0, 0))
    wsb = pl.BlockSpec((SG_GROUPS, SG_CHUNK, SG_CHUNK), lambda i: (0, 0, 0))
    bsb = pl.BlockSpec((SG_CHUNK, SG_GROUPS), lambda i: (0, 0))
    zblk = pl.BlockSpec((2, SG_CHUNK, e), lambda i: (0, i, 0))
    in_specs = [zblk, pl.BlockSpec((SG_CHUNK, e), lambda i: (i, 0)), vec, vec, wsb, bsb]
    out_specs = [zblk, pl.BlockSpec((2, 1, e), lambda i: (0, 0, 0)), vec, vec, wsb, bsb]
    out_shape = [_sds((2, t, e), BF16), _sds((2, 1, e), F32), _sds((1, e), F32), _sds((1, e), F32),
                 _sds((SG_GROUPS, SG_CHUNK, SG_CHUNK), F32), _sds((SG_CHUNK, SG_GROUPS), F32)]
    return _call(body, name, (t // SG_CHUNK,), in_specs, out_specs, out_shape)(
        z2, dy, ln_g.reshape(1, e), ln_b.reshape(1, e), w_s, b_s_t)


_COL = ("cc_w_in", "ffn_w_up", "dsa_w_qkv", "dsa_w_o", "mla_w_qb", "mla_w_kvb", "sg_w_in")
_ROW = ("cc_w_out", "ffn_w_down", "mla_w_in", "mla_w_o", "sg_w_out")
_SMALL_COL = ("cc_dw_w", "ffn_dw_w")
_RELAID = ("cc_w_in", "dsa_w_o", "mla_w_kvb")
_MIXERS = ("cc", "dsa", "mla", "sg")


def _base(name):
    return name.split("_", 1)[1] if name[0] == "l" and name[1].isdigit() else name


def _ffn_fwd(pre, p, full, h_prev, mix_out, bsz, seq):
    t, d = h_prev.shape
    h, hn = rms_fwd(h_prev, p[pre + "norm_ffn"], pre + "ffn_norm", add=mix_out)
    w_up, w_down = full[pre + "ffn_w_up"], full[pre + "ffn_w_down"]
    f = w_down.shape[1]
    z2r = mm_nn(hn[None], w_up, pre + "ffn_up", out_parts=2).reshape(2, bsz, seq, f)
    dw_w2 = jnp.transpose(full[pre + "ffn_dw_w"].reshape(FFN_WIDTH, 2, f), (1, 0, 2))
    dw_b2 = p[pre + "ffn_dw_b"].reshape(2, 1, f)
    y = ffn_gate_fwd(z2r, dw_w2, dw_b2, pre + "ffn_gate").reshape(1, t, f)
    out = mm_nn(y, w_down, pre + "ffn_down")[0]
    return out, dict(h=h, hn=hn, z2r=z2r, y=y, dw_w2=dw_w2, dw_b2=dw_b2)


def relayout_cols(g, name, to_shards):
    k = g.shape[1]
    n = g.shape[2] // N_DEV if to_shards else g.shape[2]
    tk = _row_tile(k, n, itemsize=g.dtype.itemsize, budget=1 << 21)

    def body(i_ref, o_ref):
        o_ref[...] = i_ref[...]

    shard = pl.BlockSpec((None, tk, n), lambda j, ki: (j, ki, 0))
    whole = pl.BlockSpec((None, tk, n), lambda j, ki: (0, ki, j))
    if to_shards:
        return _call(body, name, (N_DEV, k // tk), [whole], shard, _sds((N_DEV, k, n), g.dtype))(g)
    return _call(body, name, (N_DEV, k // tk), [shard], whole, _sds((1, k, N_DEV * n), g.dtype))(g)


def _col_grad(a, b, w, name):
    g = mm_tn(a, b, w.shape[0], name)
    return g if w.shape[0] == N_DEV else relayout_cols(g, name + "_shards", True)


def _ffn_bwd(pre, p, full, ctx, dh_out, big_g, small_g, bsz, seq, mid):
    w_up, w_down = full[pre + "ffn_w_up"], full[pre + "ffn_w_down"]
    f = w_down.shape[1]
    dh_out, dh_b = dh_out
    t, d = dh_out.shape
    dh3 = dh_b[None]
    dyv = mm_nt(dh3, w_down, pre + "ffn_down_dx")
    big_g[pre + "ffn_w_down"] = mm_tn(ctx["y"], dh3, 1, pre + "ffn_down_dw").reshape(N_DEV, f // N_DEV, d)
    mid(dyv)
    dz2, ddw2, ddb2 = ffn_gate_bwd(ctx["z2r"], dyv.reshape(bsz, seq, f), ctx["dw_w2"], ctx["dw_b2"],
                                   pre + "ffn_gate_bwd")
    dz2 = dz2.reshape(2, t, f)
    big_g[pre + "ffn_w_up"] = mm_tn(ctx["hn"][None], dz2, N_DEV, pre + "ffn_up_dw")
    dhn = mm_nt(dz2, w_up, pre + "ffn_up_dx")
    small_g[pre + "ffn_dw_w"] = jnp.transpose(ddw2, (1, 0, 2)).reshape(FFN_WIDTH, 2 * f)
    small_g[pre + "ffn_dw_b"] = ddb2.reshape(2 * f)
    dh, dh_b, dg = rms_bwd(ctx["h"], p[pre + "norm_ffn"], dhn, dh_out, pre + "ffn_norm_bwd")
    small_g[pre + "norm_ffn"] = dg.reshape(d)
    return dh, dh_b


def _cc_fwd(pre, p, full, hn, bsz, seq, aux):
    t, c = hn.shape
    z2r = mm_nn(hn[None], full[pre + "cc_w_in"], pre + "cc_in", bias=p[pre + "cc_b_in"], out_parts=2).reshape(
        2, bsz, seq, c)
    cv = cc_conv_fwd(z2r, full[pre + "cc_dw_w"], p[pre + "cc_dw_b"], pre + "cc_conv").reshape(t, c)
    s = cc_ln_fwd(cv, p[pre + "cc_ln_g"], p[pre + "cc_ln_b"], pre + "cc_ln")
    out = mm_nn(s[None], full[pre + "cc_w_out"], pre + "cc_out", bias=p[pre + "cc_b_out"])[0]
    return out, dict(z2r=z2r, cv=cv, s=s)


def _cc_bwd(pre, p, full, ctx, hn, dm, big_g, small_g, bsz, seq, aux, mid):
    dm, dm_b = dm
    t, c = dm.shape
    dm3 = dm_b[None]
    small_g[pre + "cc_b_out"] = col_sum(dm, pre + "cc_bout_g").reshape(c)
    ds = mm_nt(dm3, full[pre + "cc_w_out"], pre + "cc_out_dx")
    big_g[pre + "cc_w_out"] = mm_tn(ctx["s"][None], dm3, 1, pre + "cc_out_dw").reshape(N_DEV, c // N_DEV, c)
    mid(ds)
    dcv, dlg, dlb = cc_ln_bwd(ctx["cv"], p[pre + "cc_ln_g"], p[pre + "cc_ln_b"], ds, pre + "cc_ln_bwd")
    dz2, dbi, ddw, ddb = cc_conv_bwd(ctx["z2r"], dcv.reshape(bsz, seq, c), full[pre + "cc_dw_w"], pre + "cc_conv_bwd")
    dz2 = dz2.reshape(2, t, c)
    big_g[pre + "cc_w_in"] = _col_grad(hn[None], dz2, full[pre + "cc_w_in"], pre + "cc_in_dw")
    small_g[pre + "cc_ln_g"], small_g[pre + "cc_ln_b"] = dlg.reshape(c), dlb.reshape(c)
    small_g[pre + "cc_b_in"], small_g[pre + "cc_dw_w"], small_g[pre + "cc_dw_b"] = dbi.reshape(2 * c), ddw, ddb.reshape(c)
    return mm_nt(dz2, full[pre + "cc_w_in"], pre + "cc_in_dx")


def _dsa_fwd(pre, p, full, hn, bsz, seq, aux):
    cos, sin = aux["dsa_tables"]
    qkv = mm_nn(hn[None], full[pre + "dsa_w_qkv"], pre + "dsa_qkv")[0]
    qkv_r = dsa_rope_fwd(qkv, cos, sin, seq, pre + "dsa_rope")
    for window, dil in DSA_CONFIGS:
        assert window // dil == DSA_BLOCK and (seq // dil) % DSA_BLOCK == 0
    o, o_b, lse = dsa_attn_all_fwd(qkv_r, bsz, seq, pre + "dsa_attn")
    out = mm_nn(o_b[None], full[pre + "dsa_w_o"], pre + "dsa_o")[0]
    return out, dict(qkv_r=qkv_r, o=o, o_b=o_b, lse=lse)


def _dsa_bwd(pre, p, full, ctx, hn, dm, big_g, small_g, bsz, seq, aux, mid):
    cos, sin = aux["dsa_tables"]
    dm, dm_b = dm
    dm3 = dm_b[None]
    do = mm_nt(dm3, full[pre + "dsa_w_o"], pre + "dsa_o_dx")
    big_g[pre + "dsa_w_o"] = _col_grad(ctx["o_b"][None], dm3, full[pre + "dsa_w_o"], pre + "dsa_o_dw")
    mid(do)
    delta = head_delta(do, ctx["o"], pre + "dsa_delta")
    grads = dsa_attn_all_bwd(ctx["qkv_r"], do, ctx["lse"], delta, bsz, seq, pre + "dsa_attn_bwd")
    dqkv = dsa_rope_bwd_all(grads, cos, sin, seq, pre + "dsa_rope_bwd")
    big_g[pre + "dsa_w_qkv"] = mm_tn(hn[None], dqkv, N_DEV, pre + "dsa_qkv_dw")
    return mm_nt(dqkv, full[pre + "dsa_w_qkv"], pre + "dsa_qkv_dx")


def _mla_weights(pre, full):
    rk, hd = MLA_RANK, MLA_NOPE + MLA_ROPE
    w_in = full[pre + "mla_w_in"][0]
    w_in_p = jnp.concatenate([w_in[:, :2 * rk], _spread_rope_cols(w_in[:, 2 * rk:])], axis=1)[None]
    w_qb = jnp.transpose(full[pre + "mla_w_qb"], (1, 0, 2)).reshape(rk, MLA_HEADS, hd)
    w_qb_p = jnp.concatenate([w_qb[..., :MLA_NOPE], _spread_rope_cols(w_qb[..., MLA_NOPE:])], axis=-1)
    return w_in_p, w_qb_p.reshape(1, rk, MLA_HEADS * 2 * LANES)


def _mla_fwd(pre, p, full, hn, bsz, seq, aux):
    cos, sin = aux["mla_tables"]
    w_in_p, w_qb_p = _mla_weights(pre, full)
    c = mm_nn(hn[None], w_in_p, pre + "mla_in")[0]
    qn, kn, kp = mla_low_fwd(c, p[pre + "mla_q_norm"], p[pre + "mla_kv_norm"], cos, sin, seq, pre + "mla_low")
    q = mm_nn(qn[None], w_qb_p, pre + "mla_qb")[0]
    kv = mm_nn(kn[None], full[pre + "mla_w_kvb"], pre + "mla_kvb")[0]
    q_b, kv_b = mla_rope_cast(q, kv, cos, sin, seq, pre + "mla_rope")
    o, lse = mla_attn_fwd(q_b, kv_b, kp, bsz, seq, pre + "mla_attn")
    o_b = o.astype(BF16)
    out = mm_nn(o_b[None], full[pre + "mla_w_o"], pre + "mla_o")[0]
    return out, dict(c=c, qn=qn, kn=kn, kp=kp, q_b=q_b, kv_b=kv_b, o=o, o_b=o_b, lse=lse, w_in_p=w_in_p, w_qb_p=w_qb_p)


def _mla_bwd(pre, p, full, ctx, hn, dm, big_g, small_g, bsz, seq, aux, mid):
    cos, sin = aux["mla_tables"]
    dm, dm_b = dm
    t, d = dm.shape
    rk = MLA_RANK
    dm3 = dm_b[None]
    do = mm_nt(dm3, full[pre + "mla_w_o"], pre + "mla_o_dx")
    big_g[pre + "mla_w_o"] = mm_tn(ctx["o_b"][None], dm3, 1, pre + "mla_o_dw").reshape(N_DEV, -1, d)
    mid(do)
    delta = head_delta(do, ctx["o"], pre + "mla_delta")
    dq, dkv, dkp = mla_attn_bwd(ctx["q_b"], ctx["kv_b"], ctx["kp"], do, ctx["lse"], delta, bsz, seq, pre + "mla_attn_bwd")
    dq_b, dkv_b = mla_rope_cast(dq, dkv, cos, sin, seq, pre + "mla_rope_bwd", transpose=True)
    g_qb = mm_tn(ctx["qn"][None], dq_b[None], 1, pre + "mla_qb_dw")[0].reshape(rk, MLA_HEADS, 2 * LANES)
    g_qb = jnp.concatenate([g_qb[..., :MLA_NOPE], _gather_rope_cols(g_qb[..., MLA_NOPE:])], axis=-1)
    big_g[pre + "mla_w_qb"] = jnp.transpose(g_qb.reshape(rk, N_DEV, -1), (1, 0, 2))
    dqn = mm_nt(dq_b[None], ctx["w_qb_p"], pre + "mla_qb_dx")
    big_g[pre + "mla_w_kvb"] = _col_grad(ctx["kn"][None], dkv_b[None], full[pre + "mla_w_kvb"], pre + "mla_kvb_dw")
    dkn = mm_nt(dkv_b[None], full[pre + "mla_w_kvb"], pre + "mla_kvb_dx")
    dc, dqg, dkg = mla_low_bwd(ctx["c"], p[pre + "mla_q_norm"], p[pre + "mla_kv_norm"], dqn, dkn, dkp, cos, sin, seq,
                               pre + "mla_low_bwd")
    small_g[pre + "mla_q_norm"], small_g[pre + "mla_kv_norm"] = dqg.reshape(rk), dkg.reshape(rk)
    g_in = mm_tn(hn[None], dc, 1, pre + "mla_in_dw")[0]
    g_in = jnp.concatenate([g_in[:, :2 * rk], _gather_rope_cols(g_in[:, 2 * rk:])], axis=1)
    big_g[pre + "mla_w_in"] = g_in.reshape(N_DEV, d // N_DEV, -1)
    return mm_nt(dc, ctx["w_in_p"], pre + "mla_in_dx")


def _sg_fwd(pre, p, full, hn, bsz, seq, aux):
    z2 = mm_nn(hn[None], full[pre + "sg_w_in"], pre + "sg_in", bias=p[pre + "sg_b_in"], out_parts=2)
    b_s_t = jnp.transpose(p[pre + "sg_b_s"])
    y = sgu_fwd(z2, p[pre + "sg_ln_g"], p[pre + "sg_ln_b"], p[pre + "sg_w_s"], b_s_t, pre + "sg_mix")
    out = mm_nn(y[None], full[pre + "sg_w_out"], pre + "sg_out", bias=p[pre + "sg_b_out"])[0]
    return out, dict(z2=z2, y=y, b_s_t=b_s_t)


def _sg_bwd(pre, p, full, ctx, hn, dm, big_g, small_g, bsz, seq, aux, mid):
    dm, dm_b = dm
    t, d = dm.shape
    dm3 = dm_b[None]
    small_g[pre + "sg_b_out"] = col_sum(dm, pre + "sg_bout_g").reshape(d)
    dy = mm_nt(dm3, full[pre + "sg_w_out"], pre + "sg_out_dx")
    big_g[pre + "sg_w_out"] = mm_tn(ctx["y"][None], dm3, 1, pre + "sg_out_dw").reshape(N_DEV, -1, d)
    mid(dy)
    dz2, dbi, dlg, dlb, dws, dbs_t = sgu_bwd(ctx["z2"], dy, p[pre + "sg_ln_g"], p[pre + "sg_ln_b"], p[pre + "sg_w_s"],
                                             ctx["b_s_t"], pre + "sg_mix_bwd")
    big_g[pre + "sg_w_in"] = mm_tn(hn[None], dz2, N_DEV, pre + "sg_in_dw")
    small_g[pre + "sg_b_in"] = dbi.reshape(-1)
    small_g[pre + "sg_ln_g"], small_g[pre + "sg_ln_b"] = dlg.reshape(-1), dlb.reshape(-1)
    small_g[pre + "sg_w_s"], small_g[pre + "sg_b_s"] = dws, jnp.transpose(dbs_t)
    return mm_nt(dz2, full[pre + "sg_w_in"], pre + "sg_in_dx")


_MIX_FWD = dict(cc=_cc_fwd, dsa=_dsa_fwd, mla=_mla_fwd, sg=_sg_fwd)
_MIX_BWD = dict(cc=_cc_bwd, dsa=_dsa_bwd, mla=_mla_bwd, sg=_sg_bwd)


def _reduce_scatter(parts, core, chip, tag, adams=None, after=()):
    recv = rs_core_exchange(parts, tag + "_cores", after)
    sums = [rs_core_add(pt, rc, core, "%s_add%d" % (tag, i)) for i, (pt, rc) in enumerate(zip(parts, recv))]
    recv2 = rs_chip_exchange(sums, tag + "_chips")
    return [rs_finish(s, r2, chip, "%s_fin%d" % (tag, i), adam=None if adams is None else adams[i])
            for i, (s, r2) in enumerate(zip(sums, recv2))]


class _Schedule:
    def __init__(self):
        self.actions = {}

    def at(self, key, fn):
        self.actions.setdefault(key, []).append(fn)

    def run(self, key, x):
        after = list(x) if isinstance(x, (list, tuple)) else [x]
        for fn in self.actions.pop(key, []):
            fn(after)


def _aside(fn):
    saved = list(_PENDING)
    _PENDING.clear()
    out = fn()
    _PENDING[:0] = saved
    return out


def cast_bf16(w, after, name):
    r, c = w.shape
    tr = _row_tile(r, c, budget=1 << 21)
    na = len(after)

    def body(*refs):
        refs[1 + na][...] = refs[0][...].astype(BF16)

    blk = pl.BlockSpec((tr, c), lambda i: (i, 0))
    return pl.pallas_call(body, name=name, grid=(r // tr,), in_specs=[blk] + [_ANY] * na, out_specs=blk,
                          out_shape=_sds((r, c), BF16),
                          compiler_params=pltpu.CompilerParams(vmem_limit_bytes=VMEM_LIMIT))(w, *after)


def _gather_overlapped(sched, keys, shards_fn, n, dev, tag, deliver):
    state = {}

    def start(x):
        x = list(x) + list(_PENDING)
        shards = shards_fn(x)
        lands = [lax.dynamic_update_slice(lax.empty((N_DEV,) + s.shape, s.dtype), s[None], (dev,) + (0,) * s.ndim)
                 for s in shards]
        state["first"] = exchange_start(_ag_plan_first, 4 * n, shards, lands, x, tag +"_start")
        _PENDING.append(state["first"][4])

    def pass_on(x):
        _, lands = exchange_wait(_ag_plan_first, state["first"], x, tag +"_wait")
        state["pass"] = exchange_start(_ag_plan_pass, 3 * n, [], lands, [], tag + "_pass")
        _PENDING.append(state["pass"][4])

    def done(x):
        deliver(exchange_wait(_ag_plan_pass, state["pass"], x, tag +"_done")[1])

    for key, fn in zip(keys, (start, pass_on, done)):
        sched.at(key, fn)


def _reduce_overlapped(sched, keys, parts, adams, core, chip, tag, deliver):
    n = len(parts)
    state = {}

    def start(x):
        lands = [lax.empty((4,) + pt.shape[1:], pt.dtype) for pt in parts]
        state["cores"] = exchange_start(_rs_plan_cores, 4 * n, parts, lands, x, tag +"_start")
        _PENDING.append(state["cores"][4])

    def middle(x):
        mine, recv = exchange_wait(_rs_plan_cores, state["cores"], x, tag +"_wait")
        sums = _aside(lambda: [rs_core_add(pt, rc, core, "%s_add%d" % (tag, i))
                               for i, (pt, rc) in enumerate(zip(mine, recv))])
        lands = [lax.empty((3,) + s.shape[1:], s.dtype) for s in sums]
        state["chips"] = exchange_start(_rs_plan_chips, 3 * n, sums, lands, [], tag + "_send")
        _PENDING.append(state["chips"][4])

    def done(x):
        sums, recv = exchange_wait(_rs_plan_chips, state["chips"], x, tag +"_done")
        deliver(_aside(lambda: [rs_finish(s, r2, chip, "%s_fin%d" % (tag, i), adam=adams[i])
                                for i, (s, r2) in enumerate(zip(sums, recv))]))

    for key, fn in zip(keys, (start, middle, done)):
        sched.at(key, fn)


def _pack(arrays):
    flat = jnp.concatenate([a.reshape(-1) for a in arrays])
    unit = N_DEV * SUBLANES * LANES
    padded = -(-flat.shape[0] // unit) * unit
    return jnp.pad(flat, (0, padded - flat.shape[0]))


def _unpack(flat, like):
    out, pos = [], 0
    for a in like:
        out.append(flat[pos:pos + a.size].reshape(a.shape))
        pos += a.size
    return out


def _train_step(p):
    names = list(p)
    wnames = names[1:names.index("loss_target")]
    x = p["x"]
    bsz, seq, d = x.shape
    t = bsz * seq
    xi, yi, ci = lax.axis_index("x"), lax.axis_index("y"), lax.axis_index("c")
    core = jnp.reshape(ci, (1,)).astype(jnp.int32)
    chip = jnp.reshape(2 * xi + yi, (1,)).astype(jnp.int32)
    dev = 4 * xi + 2 * yi + ci
    n_layers = 1 + max(int(n[1]) for n in wnames if n[0] == "l" and n[1].isdigit())
    big = [n for n in wnames if _base(n) in _COL + _ROW]
    aux = dict(dsa_tables=rope_tables_full(seq, LANES), mla_tables=mla_rope_tables(seq))

    sched = _Schedule()
    full = {}

    def deliver_weights(grp):
        def deliver(lands):
            for n, g in zip(grp, lands):
                if _base(n) in _RELAID:
                    full[n] = _aside(lambda: relayout_cols(g, n + "_whole", False))
                elif _base(n) in _COL:
                    full[n] = g
                elif _base(n) in _ROW:
                    full[n] = g.reshape(1, N_DEV * g.shape[1], g.shape[2])
                else:
                    full[n] = jnp.transpose(g, (1, 0, 2)).reshape(g.shape[1], N_DEV * g.shape[2])
        return deliver

    groups = []
    for layer in range(n_layers):
        lw = [n for n in wnames if n.startswith("l%d_" % layer) and _base(n) in _COL + _ROW + _SMALL_COL]
        groups += [[n for n in lw if "_ffn_" not in n], [n for n in lw if "_ffn_" in n]]

    def shards_of(grp):
        def shards(after):
            if not after:
                return [p[n] if _base(n) in _SMALL_COL else p[n].astype(BF16) for n in grp]
            return [p[n] if _base(n) in _SMALL_COL else cast_bf16(p[n], after, n + "_bf16") for n in grp]
        return shards

    for s in range(len(groups)):
        if s == 0:
            keys = [("f", -1), ("first", 0), ("first", 0)]
        elif s == 1:
            keys = [("f", -1), ("f", 1), ("f", 1)]
        else:
            keys = [("f", s - 2 if s > 2 else s - 1), ("f", s - 1 if s > 2 else s), ("f", s)]
        _gather_overlapped(sched, keys, shards_of(groups[s]), len(groups[s]), dev, "ag_s%d" % s,
                           deliver_weights(groups[s]))
    sched.run(("f", -1), [])

    h = x.reshape(t, d)
    ffn_out = None
    ctxs = []
    for layer in range(n_layers):
        pre = "l%d_" % layer
        kind = _MIXERS[layer % len(_MIXERS)]
        if ffn_out is None:
            hn = rms_fwd(h, p[pre + "norm_mix"], pre + "mix_norm")
        else:
            h, hn = rms_fwd(h, p[pre + "norm_mix"], pre + "mix_norm", add=ffn_out)
        if layer == 0:
            sched.run(("first", 0), hn)
        sched.run(("f", 2 * layer), [hn] + (list(full.values()) if layer == 0 else []))
        mix_out, mctx = _MIX_FWD[kind](pre, p, full, hn, bsz, seq, aux)
        sched.run(("f", 2 * layer + 1), mix_out)
        ffn_out, fctx = _ffn_fwd(pre, p, full, h, mix_out, bsz, seq)
        ctxs.append((h, hn, mctx, fctx))
        h = fctx["h"]

    loss_row, dh, dh_b, dfin = final_loss(h, ffn_out, p["final_norm"], p["loss_target"].reshape(t, d), "final_loss")
    dh = (dh, dh_b)
    loss = lax.psum(loss_row[0, 0], ("x", "y", "c"))

    small_g = {"final_norm": dfin.reshape(d)}
    results = {}

    def deliver_grads(lnames):
        def deliver(outs):
            for n, o in zip(lnames, outs):
                results[n] = o
        return deliver

    def reduce_later(q, lnames, big_g, tag):
        keys = [("b", q + 2), ("b", q + 3), ("b", q + 5)]
        _reduce_overlapped(sched, keys, [big_g[n] for n in lnames], [(p[n], p["m_" + n], p["v_" + n]) for n in lnames],
                           core, chip, tag, deliver_grads(lnames))

    q = 0
    for layer in reversed(range(n_layers)):
        pre = "l%d_" % layer
        kind = _MIXERS[layer % len(_MIXERS)]
        h_in, hn, mctx, fctx = ctxs[layer]
        big_g = {}
        sched.run(("b", q), dh[0])
        dh = _ffn_bwd(pre, p, full, fctx, dh, big_g, small_g, bsz, seq, lambda v, q=q: sched.run(("b", q + 1), v))
        reduce_later(q, [n for n in big if n.startswith(pre) and "_ffn_" in n], big_g, "rs_l%d_ffn" % layer)
        q += 2
        sched.run(("b", q), dh[0])
        dhn = _MIX_BWD[kind](pre, p, full, mctx, hn, dh, big_g, small_g, bsz, seq, aux,
                             lambda v, q=q: sched.run(("b", q + 1), v))
        dh_f, dh_b, dg = rms_bwd(h_in, p[pre + "norm_mix"], dhn, dh[0], pre + "mix_norm_bwd")
        dh = (dh_f, dh_b)
        small_g[pre + "norm_mix"] = dg.reshape(d)
        reduce_later(q, [n for n in big if n.startswith(pre) and "_ffn_" not in n], big_g, "rs_l%d_mix" % layer)
        q += 2
    for tail in range(q, q + 4):
        sched.run(("b", tail), dh[0])
    assert not sched.actions, sched.actions.keys()

    small = [n for n in wnames if n not in big]
    packed = _pack([small_g[n] for n in small]).reshape(N_DEV, -1, LANES)
    settled = [results[n][1] for n in big if not n.startswith("l0_")]
    reduced = _reduce_scatter([packed], core, chip, "rs_small", after=settled)[0]
    gathered = all_gather([reduced], "ag_small")[0].reshape(-1)
    g_local = []
    for n, g in zip(small, _unpack(gathered, [small_g[n] for n in small])):
        if _base(n) in _SMALL_COL:
            width = p[n].shape[1]
            g = lax.dynamic_slice_in_dim(g, dev * width, width, axis=1)
        g_local.append(g)
    stacked = [_pack(v).reshape(-1, LANES) for v in
               ([p[n] for n in small], g_local, [p["m_" + n] for n in small], [p["v_" + n] for n in small])]
    upd = adam_flat(*stacked, "adam_small")
    upd = [_unpack(u.reshape(-1), g_local) for u in upd]
    for i, n in enumerate(small):
        results[n] = (g_local[i], upd[0][i], upd[1][i], upd[2][i])

    grad_x = dh[0].reshape(bsz, seq, d)
    cols = [[results[n][k] for n in wnames] for k in range(4)]
    return (loss, grad_x, *cols[0], *cols[1], *cols[2], *cols[3])


def kernel(x, l0_norm_mix, l0_cc_w_in, l0_cc_b_in, l0_cc_dw_w, l0_cc_dw_b, l0_cc_ln_g, l0_cc_ln_b, l0_cc_w_out, l0_cc_b_out, l0_norm_ffn, l0_ffn_w_up, l0_ffn_dw_w, l0_ffn_dw_b, l0_ffn_w_down, l1_norm_mix, l1_dsa_w_qkv, l1_dsa_w_o, l1_norm_ffn, l1_ffn_w_up, l1_ffn_dw_w, l1_ffn_dw_b, l1_ffn_w_down, l2_norm_mix, l2_mla_w_in, l2_mla_q_norm, l2_mla_w_qb, l2_mla_kv_norm, l2_mla_w_kvb, l2_mla_w_o, l2_norm_ffn, l2_ffn_w_up, l2_ffn_dw_w, l2_ffn_dw_b, l2_ffn_w_down, l3_norm_mix, l3_sg_w_in, l3_sg_b_in, l3_sg_ln_g, l3_sg_ln_b, l3_sg_w_s, l3_sg_b_s, l3_sg_w_out, l3_sg_b_out, l3_norm_ffn, l3_ffn_w_up, l3_ffn_dw_w, l3_ffn_dw_b, l3_ffn_w_down, final_norm, loss_target, m_l0_norm_mix, m_l0_cc_w_in, m_l0_cc_b_in, m_l0_cc_dw_w, m_l0_cc_dw_b, m_l0_cc_ln_g, m_l0_cc_ln_b, m_l0_cc_w_out, m_l0_cc_b_out, m_l0_norm_ffn, m_l0_ffn_w_up, m_l0_ffn_dw_w, m_l0_ffn_dw_b, m_l0_ffn_w_down, m_l1_norm_mix, m_l1_dsa_w_qkv, m_l1_dsa_w_o, m_l1_norm_ffn, m_l1_ffn_w_up, m_l1_ffn_dw_w, m_l1_ffn_dw_b, m_l1_ffn_w_down, m_l2_norm_mix, m_l2_mla_w_in, m_l2_mla_q_norm, m_l2_mla_w_qb, m_l2_mla_kv_norm, m_l2_mla_w_kvb, m_l2_mla_w_o, m_l2_norm_ffn, m_l2_ffn_w_up, m_l2_ffn_dw_w, m_l2_ffn_dw_b, m_l2_ffn_w_down, m_l3_norm_mix, m_l3_sg_w_in, m_l3_sg_b_in, m_l3_sg_ln_g, m_l3_sg_ln_b, m_l3_sg_w_s, m_l3_sg_b_s, m_l3_sg_w_out, m_l3_sg_b_out, m_l3_norm_ffn, m_l3_ffn_w_up, m_l3_ffn_dw_w, m_l3_ffn_dw_b, m_l3_ffn_w_down, m_final_norm, v_l0_norm_mix, v_l0_cc_w_in, v_l0_cc_b_in, v_l0_cc_dw_w, v_l0_cc_dw_b, v_l0_cc_ln_g, v_l0_cc_ln_b, v_l0_cc_w_out, v_l0_cc_b_out, v_l0_norm_ffn, v_l0_ffn_w_up, v_l0_ffn_dw_w, v_l0_ffn_dw_b, v_l0_ffn_w_down, v_l1_norm_mix, v_l1_dsa_w_qkv, v_l1_dsa_w_o, v_l1_norm_ffn, v_l1_ffn_w_up, v_l1_ffn_dw_w, v_l1_ffn_dw_b, v_l1_ffn_w_down, v_l2_norm_mix, v_l2_mla_w_in, v_l2_mla_q_norm, v_l2_mla_w_qb, v_l2_mla_kv_norm, v_l2_mla_w_kvb, v_l2_mla_w_o, v_l2_norm_ffn, v_l2_ffn_w_up, v_l2_ffn_dw_w, v_l2_ffn_dw_b, v_l2_ffn_w_down, v_l3_norm_mix, v_l3_sg_w_in, v_l3_sg_b_in, v_l3_sg_ln_g, v_l3_sg_ln_b, v_l3_sg_w_s, v_l3_sg_b_s, v_l3_sg_w_out, v_l3_sg_b_out, v_l3_norm_ffn, v_l3_ffn_w_up, v_l3_ffn_dw_w, v_l3_ffn_dw_b, v_l3_ffn_w_down, v_final_norm):
    return _train_step(dict(locals()))
```

```python
import functools
import inspect
import math

import jax
import jax.numpy as jnp
from jax import lax
from jax.experimental import pallas as pl
from jax.experimental.pallas import tpu as pltpu

F32 = jnp.float32
BF16 = jnp.bfloat16
NORM_EPS = 1e-6
NEG_INF = -1e30
ROPE_THETA = 10000.0
LANES = 128
SUBLANES = 8
VMEM_LIMIT = 56 * 1024 * 1024
MM_VMEM_BUDGET = 44 * 1024 * 1024
N_DEV = 8

DSA_CONFIGS = ((128, 1), (512, 4), (2048, 16))
DSA_HEADS = 8
DSA_BLOCK = 128
MLA_HEADS = 16
MLA_RANK = 512
MLA_NOPE = 128
MLA_ROPE = 64
SG_CHUNK = 128
SG_GROUPS = 8
CC_WIDTH = 31
FFN_WIDTH = 3
ADAM_LR, ADAM_B1, ADAM_B2, ADAM_EPS, ADAM_WD, ADAM_STEP = 0.001, 0.9, 0.999, 1e-08, 0.01, 10


def _tile(n, cap, unit=LANES):
    if n <= cap:
        return n
    best = 0
    for t in range(unit, cap + 1, unit):
        if n % t == 0:
            best = t
    assert best, (n, cap, unit)
    return best


_PENDING = []


def _call(body, name, grid, in_specs, out_specs, out_shape, scratch=(), prefetch=0):
    params = pltpu.CompilerParams(vmem_limit_bytes=VMEM_LIMIT)
    deps = list(_PENDING)
    _PENDING.clear()
    if deps:
        inner, n_in = body, prefetch + len(in_specs)

        def body(*refs):
            return inner(*refs[:n_in], *refs[n_in + len(deps):])

        in_specs = list(in_specs) + [pl.BlockSpec(memory_space=pl.ANY)] * len(deps)
    if prefetch:
        spec = pltpu.PrefetchScalarGridSpec(num_scalar_prefetch=prefetch, grid=grid, in_specs=in_specs,
                                            out_specs=out_specs, scratch_shapes=list(scratch))
        fn = pl.pallas_call(body, out_shape=out_shape, grid_spec=spec, name=name, compiler_params=params)
    else:
        fn = pl.pallas_call(body, out_shape=out_shape, grid=grid, in_specs=in_specs, out_specs=out_specs,
                            scratch_shapes=list(scratch), name=name, compiler_params=params)
    return lambda *args: fn(*args, *deps)


def _sds(shape, dtype):
    return jax.ShapeDtypeStruct(tuple(shape), dtype)


def _bf(v):
    return v if v.dtype == BF16 else v.astype(BF16)


def _dot(a, b):
    return jnp.dot(a, b, preferred_element_type=F32)


def _dot_nt(a, b):
    return lax.dot_general(a, b, (((1,), (1,)), ((), ())), preferred_element_type=F32)


def _dot_tn(a, b):
    return lax.dot_general(a, b, (((0,), (0,)), ((), ())), preferred_element_type=F32)


def _sigmoid(v):
    return 1.0 / (1.0 + jnp.exp(-v))


_ERF_A = (-2.72614225801306e-10, 2.77068142495902e-08, -2.10102402082508e-06, -5.69250639462346e-05,
          -7.34990630326855e-04, -2.95459980854025e-03, -1.60960333262415e-02)
_ERF_B = (-1.45660718464996e-05, -2.13374055278905e-04, -1.68282697438203e-03, -7.37332916720468e-03,
          -1.42647390514189e-02)


def _erf(v):
    v = jnp.clip(v, -4.0, 4.0)
    v2 = v * v
    p = jnp.full_like(v, _ERF_A[0])
    for c in _ERF_A[1:]:
        p = p * v2 + c
    q = jnp.full_like(v, _ERF_B[0])
    for c in _ERF_B[1:]:
        q = q * v2 + c
    return v * p / q


def _gelu_cdf(v):
    return 0.5 * (1.0 + _erf(v * (2.0 ** -0.5)))


def _gelu_grad(v, cdf):
    return cdf + v * jnp.exp(-0.5 * v * v) * ((2.0 * math.pi) ** -0.5)


def _edge_rows(width):
    return -(-(width - 1) // SUBLANES) * SUBLANES


def _conv_causal(z, w_row, width):
    hr = _edge_rows(width)
    rows = lax.broadcasted_iota(jnp.int32, (hr, z.shape[1]), 0)
    out = z * w_row(width - 1)
    head = z[:hr] * w_row(width - 1)
    for j in range(1, width):
        tap = pltpu.roll(z, j, 0)
        out = out + tap * w_row(width - 1 - j)
        head = head + jnp.where(rows >= j, tap[:hr], 0.0) * w_row(width - 1 - j)
    return jnp.concatenate([head, out[hr:]], axis=0)


def _conv_causal_bwd(x, dc, w_row, width):
    n = dc.shape[0]
    hr = _edge_rows(width)
    rows = lax.broadcasted_iota(jnp.int32, (hr, dc.shape[1]), 0)
    x_tail, dc_tail = x[n - hr:], dc[n - hr:]
    dx = dc * w_row(width - 1)
    tail = dc_tail * w_row(width - 1)
    dws = {width - 1: jnp.sum(dc * x, axis=0, keepdims=True)}
    for j in range(1, width):
        up = pltpu.roll(dc, n - j, 0)
        up_tail = jnp.where(rows < hr - j, up[n - hr:], 0.0)
        dx = dx + up * w_row(width - 1 - j)
        tail = tail + up_tail * w_row(width - 1 - j)
        wrapped = jnp.sum((up[n - hr:] - up_tail) * x_tail, axis=0, keepdims=True)
        dws[width - 1 - j] = jnp.sum(up * x, axis=0, keepdims=True) - wrapped
    return jnp.concatenate([dx[:n - hr], tail], axis=0), dws, jnp.sum(dc, axis=0, keepdims=True)


def _rope(v, cos, sin):
    return v * cos + pltpu.roll(v, 64, 1) * sin


def _rope_t(dv, cos, sin):
    return dv * cos + pltpu.roll(dv * sin, 64, 1)


def mm_nn(a, w, name, bias=None, out_parts=1, out_dtype=F32, tm_cap=1024, tn_cap=1536, tk_cap=6144):
    pa, m, kp = a.shape
    j, k, n = w.shape
    assert pa * kp == k
    big_n = j * n
    npo = big_n // out_parts
    tm = _tile(m, tm_cap, SUBLANES)
    tk = _tile(kp, tk_cap)
    tn = _tile(math.gcd(n, npo), tn_cap if tk <= 2816 else 1024)
    out_bytes = jnp.dtype(out_dtype).itemsize

    def vmem_bytes(rows):
        return 2 * (rows * tk * 2 + tk * tn * 2 + rows * tn * out_bytes) + (rows * tn * 4 if k > tk else 0)

    while vmem_bytes(tm) > MM_VMEM_BUDGET and tm % (2 * SUBLANES) == 0:
        tm //= 2
    npj, nbo, kbp, nk = n // tn, npo // tn, kp // tk, k // tk
    has_bias = bias is not None

    def body(*refs):
        a_ref, w_ref = refs[0], refs[1]
        b_ref = refs[2] if has_bias else None
        o_ref = refs[2 + has_bias]

        def finish(acc):
            if has_bias:
                acc = acc + b_ref[...]
            o_ref[...] = acc.astype(o_ref.dtype)

        part = _dot(_bf(a_ref[...]), _bf(w_ref[...]))
        if nk == 1:
            finish(part)
        else:
            acc_ref = refs[3 + has_bias]
            ki = pl.program_id(2)

            @pl.when(ki == 0)
            def _():
                acc_ref[...] = part

            @pl.when(ki > 0)
            def _():
                acc_ref[...] += part

            @pl.when(ki == nk - 1)
            def _():
                finish(acc_ref[...])

    in_specs = [pl.BlockSpec((None, tm, tk), lambda ni, mi, ki: (ki // kbp, mi, ki % kbp)),
                pl.BlockSpec((None, tk, tn), lambda ni, mi, ki: (ni // npj, ki, ni % npj))]
    args = [a, w]
    if has_bias:
        in_specs.append(pl.BlockSpec((1, tn), lambda ni, mi, ki: (0, ni)))
        args.append(bias.reshape(1, big_n))
    out_spec = pl.BlockSpec((None, tm, tn), lambda ni, mi, ki: (ni // nbo, mi, ni % nbo))
    scratch = [pltpu.VMEM((tm, tn), F32)] if nk > 1 else []
    return _call(body, name, (big_n // tn, m // tm, nk), in_specs, out_spec,
                 _sds((out_parts, m, npo), out_dtype), scratch)(*args)


def mm_nt(a, w, name, out_dtype=F32, tm_cap=1024, tko_cap=2048, tc_cap=2048):
    pa, m, npa = a.shape
    j, k, n = w.shape
    assert pa * npa == j * n
    tm = _tile(m, tm_cap, SUBLANES)
    tko = _tile(k, tko_cap)
    tc = _tile(math.gcd(n, npa), tc_cap)
    out_bytes = jnp.dtype(out_dtype).itemsize

    def vmem_bytes(rows):
        return 2 * (rows * tc * 2 + tko * tc * 2 + rows * tko * out_bytes) + (rows * tko * 4 if j * n > tc else 0)

    while vmem_bytes(tm) > MM_VMEM_BUDGET and tm % (2 * SUBLANES) == 0:
        tm //= 2
    npj, nba, nc = n // tc, npa // tc, (j * n) // tc

    def body(a_ref, w_ref, o_ref, *scr):
        part = _dot_nt(_bf(a_ref[...]), _bf(w_ref[...]))
        if nc == 1:
            o_ref[...] = part.astype(o_ref.dtype)
        else:
            acc_ref = scr[0]
            ci = pl.program_id(2)

            @pl.when(ci == 0)
            def _():
                acc_ref[...] = part

            @pl.when(ci > 0)
            def _():
                acc_ref[...] += part

            @pl.when(ci == nc - 1)
            def _():
                o_ref[...] = acc_ref[...].astype(o_ref.dtype)

    in_specs = [pl.BlockSpec((None, tm, tc), lambda mi, ko, ci: (ci // nba, mi, ci % nba)),
                pl.BlockSpec((None, tko, tc), lambda mi, ko, ci: (ci // npj, ko, ci % npj))]
    out_spec = pl.BlockSpec((tm, tko), lambda mi, ko, ci: (mi, ko))
    scratch = [pltpu.VMEM((tm, tko), F32)] if nc > 1 else []
    return _call(body, name, (m // tm, k // tko, nc), in_specs, out_spec, _sds((m, k), out_dtype), scratch)(a, w)


def mm_tn(a, b, n_shards, name, out_dtype=BF16, tt_cap=4096, tkk_cap=512, tn_cap=1536):
    pa, m, kp = a.shape
    pb, m2, npb = b.shape
    assert m == m2
    k, big_n = pa * kp, pb * npb
    n = big_n // n_shards
    tt = _tile(m, tt_cap, SUBLANES)
    tkk = _tile(kp, tkk_cap)
    tn = _tile(math.gcd(n, npb), tn_cap)
    kbp, nbb, npj, nt = kp // tkk, npb // tn, n // tn, m // tt

    def body(a_ref, b_ref, o_ref, *scr):
        part = _dot_tn(_bf(a_ref[...]), _bf(b_ref[...]))
        if nt == 1:
            o_ref[...] = part.astype(o_ref.dtype)
        else:
            acc_ref = scr[0]
            ti = pl.program_id(2)

            @pl.when(ti == 0)
            def _():
                acc_ref[...] = part

            @pl.when(ti > 0)
            def _():
                acc_ref[...] += part

            @pl.when(ti == nt - 1)
            def _():
                o_ref[...] = acc_ref[...].astype(o_ref.dtype)

    in_specs = [pl.BlockSpec((None, tt, tkk), lambda ni, ki, ti: (ki // kbp, ti, ki % kbp)),
                pl.BlockSpec((None, tt, tn), lambda ni, ki, ti: (ni // nbb, ti, ni % nbb))]
    out_spec = pl.BlockSpec((None, tkk, tn), lambda ni, ki, ti: (ni // npj, ki, ni % npj))
    scratch = [pltpu.VMEM((tkk, tn), F32)] if nt > 1 else []
    return _call(body, name, (big_n // tn, k // tkk, nt), in_specs, out_spec,
                 _sds((n_shards, k, n), out_dtype), scratch)(a, b)


_ANY = pl.BlockSpec(memory_space=pl.ANY)
_MESH = pl.DeviceIdType.MESH


def all_gather(shards, name):
    n = len(shards)

    def body(*refs):
        ins, outs = refs[:n], refs[n:2 * n]
        send_sems, recv_sems, local_sems = refs[2 * n:]
        x, y, c = lax.axis_index("x"), lax.axis_index("y"), lax.axis_index("c")
        me, sibling = (x, y, c), (x, y, 1 - c)
        chips = [(1 - x, y), (x, 1 - y), (1 - x, 1 - y)]

        def copy(i, k, block, to, src=None):
            px, py, pc = block
            dst = outs[i].at[4 * px + 2 * py + pc]
            return pltpu.make_async_remote_copy(src_ref=dst if src is None else src, dst_ref=dst,
                                                send_sem=send_sems.at[i, k], recv_sem=recv_sems.at[i, k],
                                                device_id=to, device_id_type=_MESH)

        mine = [pltpu.make_async_copy(ins[i], outs[i].at[4 * x + 2 * y + c], local_sems.at[i]) for i in range(n)]
        for cp in mine:
            cp.start()
        first = []
        for i in range(n):
            first.append(copy(i, 0, me, sibling, src=ins[i]))
            first += [copy(i, 1 + q, me, (*chip, c), src=ins[i]) for q, chip in enumerate(chips)]
        for cp in first:
            cp.start()
        passed = []
        for q, chip in enumerate(chips):
            for i in range(n):
                copy(i, 1 + q, (*chip, c), me).wait_recv()
                cp = copy(i, 4 + q, (*chip, c), sibling)
                cp.start()
                passed.append(cp)
        for i in range(n):
            copy(i, 0, sibling, me).wait_recv()
            for q, chip in enumerate(chips):
                copy(i, 4 + q, (*chip, 1 - c), me).wait_recv()
        for cp in first + passed:
            cp.wait_send()
        for cp in mine:
            cp.wait()

    outs = pl.pallas_call(
        body, name=name, out_shape=[_sds((N_DEV,) + s.shape, s.dtype) for s in shards],
        in_specs=[_ANY] * n, out_specs=[_ANY] * n,
        scratch_shapes=[pltpu.SemaphoreType.DMA((n, 7)), pltpu.SemaphoreType.DMA((n, 7)),
                        pltpu.SemaphoreType.DMA((n,))],
    )(*shards)
    return list(outs)


def rs_core_exchange(parts, name, after=()):
    n, na = len(parts), len(after)

    def body(*refs):
        ins, outs = refs[:n], refs[n + na:2 * n + na]
        send_sems, recv_sems = refs[2 * n + na:]
        x, y, c = lax.axis_index("x"), lax.axis_index("y"), lax.axis_index("c")
        sibling = (x, y, 1 - c)
        started = []
        for i in range(n):
            for chip in range(4):
                cp = pltpu.make_async_remote_copy(src_ref=ins[i].at[2 * chip + (1 - c)], dst_ref=outs[i].at[chip],
                                                  send_sem=send_sems.at[i, chip], recv_sem=recv_sems.at[i, chip],
                                                  device_id=sibling, device_id_type=_MESH)
                cp.start()
                started.append(cp)
        for cp in started:
            cp.wait()

    outs = pl.pallas_call(
        body, name=name, out_shape=[_sds((4,) + p.shape[1:], p.dtype) for p in parts],
        in_specs=[_ANY] * (n + na), out_specs=[_ANY] * n,
        scratch_shapes=[pltpu.SemaphoreType.DMA((n, 4)), pltpu.SemaphoreType.DMA((n, 4))],
    )(*parts, *after)
    return list(outs)


def rs_chip_exchange(sums, name):
    n = len(sums)

    def body(*refs):
        ins, outs = refs[:n], refs[n:2 * n]
        send_sems, recv_sems = refs[2 * n:]
        x, y, c = lax.axis_index("x"), lax.axis_index("y"), lax.axis_index("c")
        peers = [(x, 1 - y), (1 - x, y), (1 - x, 1 - y)]
        started = []
        for i in range(n):
            for k, (px, py) in enumerate(peers):
                cp = pltpu.make_async_remote_copy(src_ref=ins[i].at[2 * px + py], dst_ref=outs[i].at[k],
                                                  send_sem=send_sems.at[i, k], recv_sem=recv_sems.at[i, k],
                                                  device_id=(px, py, c), device_id_type=_MESH)
                cp.start()
                started.append(cp)
        for cp in started:
            cp.wait()

    outs = pl.pallas_call(
        body, name=name, out_shape=[_sds((3,) + s.shape[1:], s.dtype) for s in sums],
        in_specs=[_ANY] * n, out_specs=[_ANY] * n,
        scratch_shapes=[pltpu.SemaphoreType.DMA((n, 3)), pltpu.SemaphoreType.DMA((n, 3))],
    )(*sums)
    return list(outs)


_HBM = pl.BlockSpec(memory_space=pltpu.HBM)
_SEM = pl.BlockSpec(memory_space=pltpu.SEMAPHORE)
_EFFECT = pltpu.SideEffectType.DATAFLOW_SIDE_EFFECTING


def _in_hbm(a):
    return pltpu.with_memory_space_constraint(a, pltpu.HBM)


def _plan_copies(plan, srcs, dsts, send_sems, recv_sems):
    x, y, c = lax.axis_index("x"), lax.axis_index("y"), lax.axis_index("c")
    return [pltpu.make_async_remote_copy(src_ref=s, dst_ref=d, send_sem=send_sems.at[q], recv_sem=recv_sems.at[q],
                                         device_id=to, device_id_type=_MESH)
            for q, (s, d, to) in enumerate(plan(x, y, c, srcs, dsts))]


def exchange_start(plan, n_copies, srcs, dsts, after, name):
    ns, nd, na = len(srcs), len(dsts), len(after)

    def body(*refs):
        ins = refs[:ns + nd + na]
        send_sems, recv_sems = refs[ns + nd + na], refs[ns + nd + na + 1]
        token = refs[-1]
        for cp in _plan_copies(plan, ins[:ns], ins[ns:ns + nd], send_sems, recv_sems):
            cp.start()
        token[...] = jnp.zeros_like(token)

    thru = [pltpu.HBM(a.shape, a.dtype) for a in list(srcs) + list(dsts)]
    out = pl.pallas_call(
        body, name=name,
        out_shape=[pltpu.SemaphoreType.DMA((n_copies,)), pltpu.SemaphoreType.DMA((n_copies,))] + thru
        + [_sds((SUBLANES, LANES), F32)],
        in_specs=[_HBM] * (ns + nd) + [_ANY] * na,
        out_specs=[_SEM, _SEM] + [_HBM] * (ns + nd) + [pl.BlockSpec(memory_space=pltpu.VMEM)],
        input_output_aliases={i: 2 + i for i in range(ns + nd)},
        compiler_params=pltpu.CompilerParams(has_side_effects=_EFFECT),
    )(*[_in_hbm(a) for a in list(srcs) + list(dsts)], *after)
    return out[0], out[1], list(out[2:2 + ns]), list(out[2 + ns:2 + ns + nd]), out[-1]


def exchange_wait(plan, started, after, name):
    send_sems, recv_sems, srcs, dsts, _ = started
    ns, nd, na = len(srcs), len(dsts), len(after)

    def body(*refs):
        ins = refs[:ns + nd]
        send_sems_ref, recv_sems_ref = refs[ns + nd], refs[ns + nd + 1]
        for cp in _plan_copies(plan, ins[:ns], ins[ns:ns + nd], send_sems_ref, recv_sems_ref):
            cp.wait_send()
            cp.wait_recv()

    thru = [pltpu.HBM(a.shape, a.dtype) for a in list(srcs) + list(dsts)]
    out = pl.pallas_call(
        body, name=name, out_shape=thru,
        in_specs=[_HBM] * (ns + nd) + [_SEM, _SEM] + [_ANY] * na,
        out_specs=[_HBM] * (ns + nd),
        input_output_aliases={i: i for i in range(ns + nd)},
        compiler_params=pltpu.CompilerParams(has_side_effects=_EFFECT),
    )(*srcs, *dsts, send_sems, recv_sems, *after)
    return list(out[:ns]), list(out[ns:])


def _dev_index(px, py, pc):
    return 4 * px + 2 * py + pc


def _ag_plan_first(x, y, c, srcs, dsts):
    me = _dev_index(x, y, c)
    peers = [(x, y, 1 - c), (1 - x, y, c), (x, 1 - y, c), (1 - x, 1 - y, c)]
    return [(s, d.at[me], to) for s, d in zip(srcs, dsts) for to in peers]


def _ag_plan_pass(x, y, c, srcs, dsts):
    chips = [(1 - x, y), (x, 1 - y), (1 - x, 1 - y)]
    return [(d.at[_dev_index(px, py, c)], d.at[_dev_index(px, py, c)], (x, y, 1 - c)) for d in dsts for px, py in chips]


def _rs_plan_cores(x, y, c, srcs, dsts):
    return [(s.at[2 * chip + (1 - c)], d.at[chip], (x, y, 1 - c)) for s, d in zip(srcs, dsts) for chip in range(4)]


def _rs_plan_chips(x, y, c, srcs, dsts):
    peers = [(x, 1 - y), (1 - x, y), (1 - x, 1 - y)]
    return [(s.at[2 * px + py], d.at[k], (px, py, c)) for s, d in zip(srcs, dsts) for k, (px, py) in enumerate(peers)]


def _row_tile(r, c, itemsize=4, budget=1 << 20):
    cap = max(SUBLANES, (budget // (c * itemsize)) // SUBLANES * SUBLANES)
    if r <= cap:
        return r
    best = 0
    for t in range(SUBLANES, cap + 1, SUBLANES):
        if r % t == 0:
            best = t
    return best if best else r


def rs_core_add(part, recv, core, name):
    _, r, c = part.shape
    tr = _row_tile(r, c, budget=1 << 22)

    def body(core_ref, p_ref, q_ref, o_ref):
        o_ref[...] = (p_ref[...].astype(F32) + q_ref[...].astype(F32)).astype(o_ref.dtype)

    in_specs = [pl.BlockSpec((None, tr, c), lambda ch, ri, core_ref: (2 * ch + core_ref[0], ri, 0)),
                pl.BlockSpec((None, tr, c), lambda ch, ri, core_ref: (ch, ri, 0))]
    out_spec = pl.BlockSpec((None, tr, c), lambda ch, ri, core_ref: (ch, ri, 0))
    return _call(body, name, (4, r // tr), in_specs, out_spec, _sds((4, r, c), part.dtype), prefetch=1)(core, part, recv)


def _adamw(w, g, m, v):
    m = ADAM_B1 * m + (1.0 - ADAM_B1) * g
    v = ADAM_B2 * v + (1.0 - ADAM_B2) * (g * g)
    m_hat = m / (1.0 - ADAM_B1 ** ADAM_STEP)
    v_hat = v / (1.0 - ADAM_B2 ** ADAM_STEP)
    delta = -ADAM_LR * (m_hat / (jnp.sqrt(v_hat) + ADAM_EPS) + ADAM_WD * w)
    return delta, m, v


def rs_finish(sums, recv, chip, name, adam=None):
    _, r, c = sums.shape
    tr = _row_tile(r, c, budget=1 << 20)

    def body(chip_ref, s_ref, q_ref, *refs):
        g = s_ref[...].astype(F32)
        for k in range(3):
            g = g + q_ref[k].astype(F32)
        if adam is None:
            refs[0][...] = g
        else:
            w_ref, m_ref, v_ref, g_ref, d_ref, nm_ref, nv_ref = refs
            g_ref[...] = g
            d_ref[...], nm_ref[...], nv_ref[...] = _adamw(w_ref[...], g, m_ref[...], v_ref[...])

    blk = pl.BlockSpec((tr, c), lambda ri, chip_ref: (ri, 0))
    in_specs = [pl.BlockSpec((None, tr, c), lambda ri, chip_ref: (chip_ref[0], ri, 0)),
                pl.BlockSpec((3, tr, c), lambda ri, chip_ref: (0, ri, 0))]
    args = [chip, sums, recv]
    if adam is None:
        out_specs, out_shape = blk, _sds((r, c), F32)
    else:
        in_specs += [blk] * 3
        args += list(adam)
        out_specs, out_shape = [blk] * 4, [_sds((r, c), F32)] * 4
    return _call(body, name, (r // tr,), in_specs, out_specs, out_shape, prefetch=1)(*args)


def adam_flat(w, g, m, v, name):
    r, c = w.shape
    tr = _row_tile(r, c, budget=1 << 19)

    def body(w_ref, g_ref, m_ref, v_ref, d_ref, nm_ref, nv_ref):
        d_ref[...], nm_ref[...], nv_ref[...] = _adamw(w_ref[...], g_ref[...], m_ref[...], v_ref[...])

    blk = pl.BlockSpec((tr, c), lambda ri: (ri, 0))
    return _call(body, name, (r // tr,), [blk] * 4, [blk] * 3, [_sds((r, c), F32)] * 3)(w, g, m, v)


def rms_fwd(x, g, name, add=None):
    t, d = x.shape
    tm = _tile(t, 256, SUBLANES)
    has_add = add is not None

    def body(*refs):
        if has_add:
            x_ref, a_ref, g_ref, h_ref, hn_ref = refs
            h = x_ref[...] + a_ref[...]
            h_ref[...] = h
        else:
            x_ref, g_ref, hn_ref = refs
            h = x_ref[...]
        r = lax.rsqrt(jnp.mean(h * h, axis=-1, keepdims=True) + NORM_EPS)
        hn_ref[...] = (h * r * g_ref[...]).astype(BF16)

    row = pl.BlockSpec((tm, d), lambda i: (i, 0))
    vec = pl.BlockSpec((1, d), lambda i: (0, 0))
    if has_add:
        return _call(body, name, (t // tm,), [row, row, vec], [row, row], [_sds((t, d), F32), _sds((t, d), BF16)])(
            x, add, g.reshape(1, d))
    return _call(body, name, (t // tm,), [row, vec], row, _sds((t, d), BF16))(x, g.reshape(1, d))


def _rms_bwd_math(x, g, dy):
    r = lax.rsqrt(jnp.mean(x * x, axis=-1, keepdims=True) + NORM_EPS)
    xh = x * r
    dyg = dy * g
    dx = r * (dyg - xh * jnp.mean(dyg * xh, axis=-1, keepdims=True))
    return dx, jnp.sum(dy * xh, axis=0, keepdims=True)


def rms_bwd(x, g, dhn, dres, name):
    t, d = x.shape
    tm = _tile(t, 256, SUBLANES)

    def body(x_ref, g_ref, dy_ref, dr_ref, dx_ref, dxb_ref, dg_ref):
        dx, dg = _rms_bwd_math(x_ref[...], g_ref[...], dy_ref[...])
        dx = dx + dr_ref[...]
        dx_ref[...] = dx
        dxb_ref[...] = dx.astype(BF16)

        @pl.when(pl.program_id(0) == 0)
        def _():
            dg_ref[...] = jnp.zeros_like(dg_ref)

        dg_ref[...] += dg

    row = pl.BlockSpec((tm, d), lambda i: (i, 0))
    vec = pl.BlockSpec((1, d), lambda i: (0, 0))
    return _call(body, name, (t // tm,), [row, vec, row, row], [row, row, vec],
                 [_sds((t, d), F32), _sds((t, d), BF16), _sds((1, d), F32)])(x, g.reshape(1, d), dhn, dres)


def final_loss(h, add, g, target, name):
    t, d = h.shape
    tm = _tile(t, 256, SUBLANES)

    def body(x_ref, a_ref, g_ref, t_ref, loss_ref, dx_ref, dxb_ref, dg_ref):
        x = x_ref[...] + a_ref[...]
        gain = g_ref[...]
        r = lax.rsqrt(jnp.mean(x * x, axis=-1, keepdims=True) + NORM_EPS)
        err = x * r * gain - t_ref[...]
        dx, dg = _rms_bwd_math(x, gain, err * (1.0 / d))
        dx_ref[...] = dx
        dxb_ref[...] = dx.astype(BF16)

        @pl.when(pl.program_id(0) == 0)
        def _():
            dg_ref[...] = jnp.zeros_like(dg_ref)
            loss_ref[...] = jnp.zeros_like(loss_ref)

        dg_ref[...] += dg
        part = jnp.sum(jnp.sum(err * err, axis=-1, keepdims=True), axis=0, keepdims=True) * (0.5 / d)
        loss_ref[...] += jnp.broadcast_to(part, loss_ref.shape)

    row = pl.BlockSpec((tm, d), lambda i: (i, 0))
    vec = pl.BlockSpec((1, d), lambda i: (0, 0))
    one = pl.BlockSpec((1, LANES), lambda i: (0, 0))
    return _call(body, name, (t // tm,), [row, row, vec, row], [one, row, row, vec],
                 [_sds((1, LANES), F32), _sds((t, d), F32), _sds((t, d), BF16), _sds((1, d), F32)])(
        h, add, g.reshape(1, d), target)


def ffn_gate_fwd(z2, dw_w2, dw_b2, name, tc=2 * LANES):
    _, b, s, f = z2.shape

    def body(z_ref, w_ref, b_ref, y_ref):
        g = _conv_causal(z_ref[0], lambda k: w_ref[0, k:k + 1, :], FFN_WIDTH) + b_ref[0]
        a = _conv_causal(z_ref[1], lambda k: w_ref[1, k:k + 1, :], FFN_WIDTH) + b_ref[1]
        y_ref[...] = (g * _sigmoid(g) * a).astype(BF16)

    in_specs = [pl.BlockSpec((2, None, s, tc), lambda bi, ci: (0, bi, 0, ci)),
                pl.BlockSpec((2, FFN_WIDTH, tc), lambda bi, ci: (0, 0, ci)),
                pl.BlockSpec((2, 1, tc), lambda bi, ci: (0, 0, ci))]
    out_spec = pl.BlockSpec((None, s, tc), lambda bi, ci: (bi, 0, ci))
    return _call(body, name, (b, f // tc), in_specs, out_spec, _sds((b, s, f), BF16))(z2, dw_w2, dw_b2)


def ffn_gate_bwd(z2, dy, dw_w2, dw_b2, name, tc=2 * LANES):
    _, b, s, f = z2.shape

    def body(z_ref, dy_ref, w_ref, b_ref, dz_ref, dw_ref, db_ref):
        @pl.when(pl.program_id(1) == 0)
        def _():
            dw_ref[...] = jnp.zeros_like(dw_ref)
            db_ref[...] = jnp.zeros_like(db_ref)

        zs = (z_ref[0], z_ref[1])
        g = _conv_causal(zs[0], lambda k: w_ref[0, k:k + 1, :], FFN_WIDTH) + b_ref[0]
        a = _conv_causal(zs[1], lambda k: w_ref[1, k:k + 1, :], FFN_WIDTH) + b_ref[1]
        sg = _sigmoid(g)
        dy = dy_ref[...]
        dcs = (dy * a * (sg * (1.0 + g * (1.0 - sg))), dy * (g * sg))
        for p in range(2):
            dz, dws, db = _conv_causal_bwd(zs[p], dcs[p], lambda k, p=p: w_ref[p, k:k + 1, :], FFN_WIDTH)
            dz_ref[p] = dz.astype(BF16)
            for k in range(FFN_WIDTH):
                dw_ref[p, k:k + 1, :] += dws[k]
            db_ref[p] += db

    in_specs = [pl.BlockSpec((2, None, s, tc), lambda ci, bi: (0, bi, 0, ci)),
                pl.BlockSpec((None, s, tc), lambda ci, bi: (bi, 0, ci)),
                pl.BlockSpec((2, FFN_WIDTH, tc), lambda ci, bi: (0, 0, ci)),
                pl.BlockSpec((2, 1, tc), lambda ci, bi: (0, 0, ci))]
    out_specs = [pl.BlockSpec((2, None, s, tc), lambda ci, bi: (0, bi, 0, ci)),
                 pl.BlockSpec((2, FFN_WIDTH, tc), lambda ci, bi: (0, 0, ci)),
                 pl.BlockSpec((2, 1, tc), lambda ci, bi: (0, 0, ci))]
    out_shape = [_sds((2, b, s, f), BF16), _sds((2, FFN_WIDTH, f), F32), _sds((2, 1, f), F32)]
    return _call(body, name, (f // tc, b), in_specs, out_specs, out_shape)(z2, dy, dw_w2, dw_b2)


def cc_conv_fwd(z2, dw_w, dw_b, name, tc=LANES):
    _, b, s, c = z2.shape

    def body(z_ref, w_ref, b_ref, o_ref):
        u = z_ref[0] * _sigmoid(z_ref[1])
        o_ref[...] = _conv_causal(u, lambda k: w_ref[k:k + 1, :], CC_WIDTH) + b_ref[...]

    in_specs = [pl.BlockSpec((2, None, s, tc), lambda bi, ci: (0, bi, 0, ci)),
                pl.BlockSpec((CC_WIDTH, tc), lambda bi, ci: (0, ci)),
                pl.BlockSpec((1, tc), lambda bi, ci: (0, ci))]
    out_spec = pl.BlockSpec((None, s, tc), lambda bi, ci: (bi, 0, ci))
    return _call(body, name, (b, c // tc), in_specs, out_spec, _sds((b, s, c), F32))(z2, dw_w, dw_b.reshape(1, c))


def cc_conv_bwd(z2, dcv, dw_w, name, tc=LANES):
    _, b, s, c = z2.shape

    def body(z_ref, dc_ref, w_ref, dz_ref, dbi_ref, dw_ref, db_ref):
        @pl.when(pl.program_id(1) == 0)
        def _():
            dbi_ref[...] = jnp.zeros_like(dbi_ref)
            dw_ref[...] = jnp.zeros_like(dw_ref)
            db_ref[...] = jnp.zeros_like(db_ref)

        a, gate = z_ref[0], z_ref[1]
        sg = _sigmoid(gate)
        u = a * sg
        du, dws, db = _conv_causal_bwd(u, dc_ref[...], lambda k: w_ref[k:k + 1, :], CC_WIDTH)
        for k in range(CC_WIDTH):
            dw_ref[k:k + 1, :] += dws[k]
        db_ref[...] += db
        da = du * sg
        dg = du * a * sg * (1.0 - sg)
        dz_ref[0] = da.astype(BF16)
        dz_ref[1] = dg.astype(BF16)
        dbi_ref[0] += jnp.sum(da, axis=0, keepdims=True)
        dbi_ref[1] += jnp.sum(dg, axis=0, keepdims=True)

    in_specs = [pl.BlockSpec((2, None, s, tc), lambda ci, bi: (0, bi, 0, ci)),
                pl.BlockSpec((None, s, tc), lambda ci, bi: (bi, 0, ci)),
                pl.BlockSpec((CC_WIDTH, tc), lambda ci, bi: (0, ci))]
    out_specs = [pl.BlockSpec((2, None, s, tc), lambda ci, bi: (0, bi, 0, ci)),
                 pl.BlockSpec((2, 1, tc), lambda ci, bi: (0, 0, ci)),
                 pl.BlockSpec((CC_WIDTH, tc), lambda ci, bi: (0, ci)),
                 pl.BlockSpec((1, tc), lambda ci, bi: (0, ci))]
    out_shape = [_sds((2, b, s, c), BF16), _sds((2, 1, c), F32), _sds((CC_WIDTH, c), F32), _sds((1, c), F32)]
    return _call(body, name, (c // tc, b), in_specs, out_specs, out_shape)(z2, dcv, dw_w)


def _ln_stats(v):
    mu = jnp.mean(v, axis=-1, keepdims=True)
    vc = v - mu
    r = lax.rsqrt(jnp.mean(vc * vc, axis=-1, keepdims=True) + NORM_EPS)
    return vc * r, r


def _ln_bwd(dln, xh, r, g):
    dxh = dln * g
    dx = r * (dxh - jnp.mean(dxh, axis=-1, keepdims=True) - xh * jnp.mean(dxh * xh, axis=-1, keepdims=True))
    return dx, jnp.sum(dln * xh, axis=0, keepdims=True), jnp.sum(dln, axis=0, keepdims=True)


def cc_ln_fwd(cv, ln_g, ln_b, name):
    t, c = cv.shape
    tm = _tile(t, 256, SUBLANES)

    def body(x_ref, g_ref, b_ref, o_ref):
        xh, _ = _ln_stats(x_ref[...])
        ln = xh * g_ref[...] + b_ref[...]
        o_ref[...] = (ln * _sigmoid(ln)).astype(BF16)

    row = pl.BlockSpec((tm, c), lambda i: (i, 0))
    vec = pl.BlockSpec((1, c), lambda i: (0, 0))
    return _call(body, name, (t // tm,), [row, vec, vec], row, _sds((t, c), BF16))(
        cv, ln_g.reshape(1, c), ln_b.reshape(1, c))


def cc_ln_bwd(cv, ln_g, ln_b, ds, name):
    t, c = cv.shape
    tm = _tile(t, 256, SUBLANES)

    def body(x_ref, g_ref, b_ref, ds_ref, dx_ref, dg_ref, db_ref):
        @pl.when(pl.program_id(0) == 0)
        def _():
            dg_ref[...] = jnp.zeros_like(dg_ref)
            db_ref[...] = jnp.zeros_like(db_ref)

        xh, r = _ln_stats(x_ref[...])
        ln = xh * g_ref[...] + b_ref[...]
        sg = _sigmoid(ln)
        dln = ds_ref[...] * (sg * (1.0 + ln * (1.0 - sg)))
        dx, dg, db = _ln_bwd(dln, xh, r, g_ref[...])
        dx_ref[...] = dx
        dg_ref[...] += dg
        db_ref[...] += db

    row = pl.BlockSpec((tm, c), lambda i: (i, 0))
    vec = pl.BlockSpec((1, c), lambda i: (0, 0))
    return _call(body, name, (t // tm,), [row, vec, vec, row], [row, vec, vec],
                 [_sds((t, c), F32), _sds((1, c), F32), _sds((1, c), F32)])(
        cv, ln_g.reshape(1, c), ln_b.reshape(1, c), ds)


def col_sum(v, name):
    t, n = v.shape
    tm = _tile(t, 256, SUBLANES)

    def body(v_ref, o_ref):
        @pl.when(pl.program_id(0) == 0)
        def _():
            o_ref[...] = jnp.zeros_like(o_ref)

        o_ref[...] += jnp.sum(v_ref[...], axis=0, keepdims=True)

    return _call(body, name, (t // tm,), [pl.BlockSpec((tm, n), lambda i: (i, 0))],
                 pl.BlockSpec((1, n), lambda i: (0, 0)), _sds((1, n), F32))(v)


def rope_tables_full(seq, dim):
    pos = jnp.arange(seq, dtype=F32)
    inv = ROPE_THETA ** (-(jnp.arange(0, dim, 2, dtype=F32) / dim))
    ang = pos[:, None] * inv[None, :]
    cos, sin = jnp.cos(ang), jnp.sin(ang)
    return jnp.concatenate([cos, cos], axis=-1), jnp.concatenate([-sin, sin], axis=-1)


def dsa_rope_fwd(qkv, cos, sin, seq, name):
    t, w = qkv.shape
    tm = _tile(seq, 256, SUBLANES)
    nsb = seq // tm
    hb = DSA_HEADS

    def body(x_ref, c_ref, s_ref, o_ref):
        cos_t, sin_t = c_ref[...], s_ref[...]
        for blk in range(w // LANES):
            v = x_ref[:, blk * LANES:(blk + 1) * LANES]
            if (blk // hb) % 3 < 2:
                v = _rope(v, cos_t, sin_t)
            o_ref[:, blk * LANES:(blk + 1) * LANES] = v

    row = pl.BlockSpec((tm, w), lambda i: (i, 0))
    tab = pl.BlockSpec((tm, LANES), lambda i: (i % nsb, 0))
    return _call(body, name, (t // tm,), [row, tab, tab], row, _sds((t, w), F32))(qkv, cos, sin)


def _dsa_rows(r, n, dil):
    start = r + n * DSA_BLOCK * dil
    return pl.ds(start, DSA_BLOCK) if dil == 1 else pl.ds(start, DSA_BLOCK, stride=dil)


def _dsa_load(ref, r, blocks, dil):
    parts = [ref[_dsa_rows(r, m, dil), :] for m in blocks]
    return parts[0] if len(parts) == 1 else jnp.concatenate(parts, axis=0)


def dsa_attn_all_fwd(qkv_r, batch, seq, name):
    t, w = qkv_r.shape
    h_n, n_g = DSA_HEADS, len(DSA_CONFIGS)
    scale = DSA_BLOCK ** -0.5
    chunk = _tile(seq, 256, SUBLANES)

    def body(q_ref, k_ref, v_ref, o_ref, ob_ref, lse_ref, og_ref, lg_ref):
        g = pl.program_id(2)
        for gi, (_, dil) in enumerate(DSA_CONFIGS):
            @pl.when(g == gi)
            def _(gi=gi, dil=dil):
                for r in range(dil):
                    for n in range(seq // dil // DSA_BLOCK):
                        blocks = (n - 1, n) if n else (n,)
                        rows = _dsa_rows(r, n, dil)
                        q = q_ref[rows, :].astype(BF16)
                        k = _dsa_load(k_ref, r, blocks, dil).astype(BF16)
                        v = _dsa_load(v_ref, r, blocks, dil).astype(BF16)
                        s = jnp.where(_dsa_mask(n), _dot_nt(q, k) * scale, NEG_INF)
                        m = jnp.max(s, axis=-1, keepdims=True)
                        p = jnp.exp(s - m)
                        l = jnp.sum(p, axis=-1, keepdims=True)
                        og_ref[gi, rows, :] = _dot(p.astype(BF16), v) / l
                        lg_ref[gi, rows, :] = jnp.broadcast_to(m + jnp.log(l), (DSA_BLOCK, LANES))

        @pl.when(g == n_g - 1)
        def _():
            for c0 in range(0, seq, chunk):
                sl = slice(c0, c0 + chunk)
                ls = [lg_ref[gi, sl, :] for gi in range(n_g)]
                m = functools.reduce(jnp.maximum, ls)
                es = [jnp.exp(v - m) for v in ls]
                tot = functools.reduce(lambda a, b: a + b, es)
                acc = jnp.zeros_like(m)
                for gi in range(n_g):
                    acc = acc + (es[gi] / tot) * og_ref[gi, sl, :]
                o_ref[sl, :] = acc
                ob_ref[sl, :] = acc.astype(BF16)
                lse_ref[sl, :] = m + jnp.log(tot)

    def spec(off):
        return pl.BlockSpec((None, seq, LANES), lambda b, h, g: (b, 0, (3 * g + off) * h_n + h))

    out_spec = pl.BlockSpec((None, seq, LANES), lambda b, h, g: (b, 0, h))
    view = qkv_r.reshape(batch, seq, w)
    shape = (batch, seq, h_n * LANES)
    o, o_b, lse = _call(body, name, (batch, h_n, n_g), [spec(0), spec(1), spec(2)], [out_spec] * 3,
                        [_sds(shape, F32), _sds(shape, BF16), _sds(shape, F32)],
                        scratch=[pltpu.VMEM((n_g, seq, LANES), F32)] * 2)(view, view, view)
    return o.reshape(t, -1), o_b.reshape(t, -1), lse.reshape(t, -1)


def dsa_attn_all_bwd(qkv_r, do, lse, delta, batch, seq, name):
    t, w = qkv_r.shape
    h_n, n_g = DSA_HEADS, len(DSA_CONFIGS)
    scale = DSA_BLOCK ** -0.5

    def body(q_ref, k_ref, v_ref, do_ref, lse_ref, dl_ref, d_ref):
        g = pl.program_id(2)
        d_ref[...] = jnp.zeros_like(d_ref)
        for gi, (_, dil) in enumerate(DSA_CONFIGS):
            @pl.when(g == gi)
            def _(dil=dil):
                for r in range(dil):
                    for n in range(seq // dil // DSA_BLOCK):
                        blocks = (n - 1, n) if n else (n,)
                        rows = _dsa_rows(r, n, dil)
                        q = q_ref[rows, :].astype(BF16)
                        k = _dsa_load(k_ref, r, blocks, dil).astype(BF16)
                        v = _dsa_load(v_ref, r, blocks, dil).astype(BF16)
                        dout = do_ref[rows, :].astype(BF16)
                        s = jnp.where(_dsa_mask(n), _dot_nt(q, k) * scale, NEG_INF)
                        p = jnp.exp(s - lse_ref[rows, :][:, 0:1])
                        ds = (p * (_dot_nt(dout, v) - dl_ref[rows, :][:, 0:1]) * scale).astype(BF16)
                        d_ref[0, rows, :] = _dot(ds, k)
                        dk = _dot_tn(ds, q)
                        dv = _dot_tn(p.astype(BF16), dout)
                        for i, blk in enumerate(blocks):
                            brows = _dsa_rows(r, blk, dil)
                            d_ref[1, brows, :] += dk[i * DSA_BLOCK:(i + 1) * DSA_BLOCK]
                            d_ref[2, brows, :] += dv[i * DSA_BLOCK:(i + 1) * DSA_BLOCK]

    def spec(off):
        return pl.BlockSpec((None, seq, LANES), lambda b, h, g: (b, 0, (3 * g + off) * h_n + h))

    head = pl.BlockSpec((None, seq, LANES), lambda b, h, g: (b, 0, h))
    out_spec = pl.BlockSpec((3, None, seq, LANES), lambda b, h, g: (g, b, 0, h))
    view = qkv_r.reshape(batch, seq, w)
    hv = [a.reshape(batch, seq, h_n * LANES) for a in (do, lse, delta)]
    out = _call(body, name, (batch, h_n, n_g), [spec(0), spec(1), spec(2), head, head, head], out_spec,
                _sds((3 * n_g, batch, seq, h_n * LANES), F32))(view, view, view, *hv)
    return out.reshape(3 * n_g, t, h_n * LANES)


def dsa_rope_bwd_all(grads, cos, sin, seq, name):
    n_parts, t, hw = grads.shape
    w = hw * n_parts
    tm = _tile(seq, 128, SUBLANES)
    nsb = seq // tm

    def body(g_ref, c_ref, s_ref, o_ref):
        cos_t, sin_t = c_ref[...], s_ref[...]
        for idx in range(n_parts):
            for h in range(hw // LANES):
                v = g_ref[idx, :, h * LANES:(h + 1) * LANES]
                if idx % 3 < 2:
                    v = _rope_t(v, cos_t, sin_t)
                col = idx * hw + h * LANES
                o_ref[:, col:col + LANES] = v.astype(BF16)

    tab = pl.BlockSpec((tm, LANES), lambda i: (i % nsb, 0))
    out = _call(body, name, (t // tm,), [pl.BlockSpec((n_parts, tm, hw), lambda i: (0, i, 0)), tab, tab],
                pl.BlockSpec((tm, w), lambda i: (i, 0)), _sds((t, w), BF16))(grads, cos, sin)
    return out[None]


def _dsa_mask(n):
    qi = lax.broadcasted_iota(jnp.int32, (DSA_BLOCK, (1 if n == 0 else 2) * DSA_BLOCK), 0)
    kj = lax.broadcasted_iota(jnp.int32, (DSA_BLOCK, (1 if n == 0 else 2) * DSA_BLOCK), 1)
    if n == 0:
        return qi >= kj
    dist = DSA_BLOCK + qi - kj
    return (dist >= 0) & (dist <= DSA_BLOCK)


def head_delta(do, o, name):
    t, hw = do.shape
    tm = _tile(t, 256, SUBLANES)

    def body(do_ref, o_ref, d_ref):
        for h in range(hw // LANES):
            sl = slice(h * LANES, (h + 1) * LANES)
            d = jnp.sum(do_ref[:, sl] * o_ref[:, sl], axis=-1, keepdims=True)
            d_ref[:, sl] = jnp.broadcast_to(d, (tm, LANES))

    row = pl.BlockSpec((tm, hw), lambda i: (i, 0))
    return _call(body, name, (t // tm,), [row, row], row, _sds((t, hw), F32))(do, o)


def mla_rope_tables(seq):
    pos = jnp.arange(seq, dtype=F32)
    inv = ROPE_THETA ** (-(jnp.arange(0, MLA_ROPE, 2, dtype=F32) / MLA_ROPE))
    ang = pos[:, None] * inv[None, :]
    cos, sin, zero = jnp.cos(ang), jnp.sin(ang), jnp.zeros_like(ang)
    return jnp.concatenate([cos, zero, cos, zero], axis=-1), jnp.concatenate([-sin, zero, sin, zero], axis=-1)


def _spread_rope_cols(w_pe):
    half = MLA_ROPE // 2
    zero = jnp.zeros(w_pe.shape[:-1] + (half,), w_pe.dtype)
    return jnp.concatenate([w_pe[..., :half], zero, w_pe[..., half:], zero], axis=-1)


def _gather_rope_cols(g_pe):
    half = MLA_ROPE // 2
    return jnp.concatenate([g_pe[..., :half], g_pe[..., 2 * half:3 * half]], axis=-1)


def mla_low_fwd(c, q_norm, kv_norm, cos, sin, seq, name):
    t, w = c.shape
    rk = MLA_RANK
    tm = _tile(seq, 256, SUBLANES)
    nsb = seq // tm

    def body(c_ref, qg_ref, kg_ref, cs_ref, sn_ref, qn_ref, kn_ref, kp_ref):
        for lo, g_ref, o_ref in ((0, qg_ref, qn_ref), (rk, kg_ref, kn_ref)):
            v = c_ref[:, lo:lo + rk]
            r = lax.rsqrt(jnp.mean(v * v, axis=-1, keepdims=True) + NORM_EPS)
            o_ref[...] = (v * r * g_ref[...]).astype(BF16)
        kp_ref[...] = _rope(c_ref[:, 2 * rk:], cs_ref[...], sn_ref[...]).astype(BF16)

    row = lambda n: pl.BlockSpec((tm, n), lambda i: (i, 0))
    vec = pl.BlockSpec((1, rk), lambda i: (0, 0))
    tab = pl.BlockSpec((tm, LANES), lambda i: (i % nsb, 0))
    return _call(body, name, (t // tm,), [row(w), vec, vec, tab, tab], [row(rk), row(rk), row(LANES)],
                 [_sds((t, rk), BF16), _sds((t, rk), BF16), _sds((t, LANES), BF16)])(
        c, q_norm.reshape(1, rk), kv_norm.reshape(1, rk), cos, sin)


def mla_low_bwd(c, q_norm, kv_norm, dqn, dkn, dkp, cos, sin, seq, name):
    t, w = c.shape
    rk = MLA_RANK
    tm = _tile(seq, 256, SUBLANES)
    nsb = seq // tm

    def body(c_ref, qg_ref, kg_ref, dq_ref, dk_ref, dp_ref, cs_ref, sn_ref, dc_ref, dqg_ref, dkg_ref):
        @pl.when(pl.program_id(0) == 0)
        def _():
            dqg_ref[...] = jnp.zeros_like(dqg_ref)
            dkg_ref[...] = jnp.zeros_like(dkg_ref)

        for lo, g_ref, d_ref, dg_ref in ((0, qg_ref, dq_ref, dqg_ref), (rk, kg_ref, dk_ref, dkg_ref)):
            dx, dg = _rms_bwd_math(c_ref[:, lo:lo + rk], g_ref[...], d_ref[...])
            dc_ref[:, lo:lo + rk] = dx.astype(BF16)
            dg_ref[...] += dg
        dc_ref[:, 2 * rk:] = _rope_t(dp_ref[...], cs_ref[...], sn_ref[...]).astype(BF16)

    row = lambda n: pl.BlockSpec((tm, n), lambda i: (i, 0))
    vec = pl.BlockSpec((1, rk), lambda i: (0, 0))
    tab = pl.BlockSpec((tm, LANES), lambda i: (i % nsb, 0))
    dc, dqg, dkg = _call(body, name, (t // tm,), [row(w), vec, vec, row(rk), row(rk), row(LANES), tab, tab],
                         [row(w), vec, vec], [_sds((t, w), BF16), _sds((1, rk), F32), _sds((1, rk), F32)])(
        c, q_norm.reshape(1, rk), kv_norm.reshape(1, rk), dqn, dkn, dkp, cos, sin)
    return dc[None], dqg, dkg


def mla_rope_cast(q, kv, cos, sin, seq, name, transpose=False):
    t, w = q.shape
    tm = _tile(seq, 256, SUBLANES)
    nsb = seq // tm
    fn = _rope_t if transpose else _rope

    def body(q_ref, kv_ref, cs_ref, sn_ref, qo_ref, kvo_ref):
        cos_t, sin_t = cs_ref[...], sn_ref[...]
        for blk in range(w // LANES):
            sl = slice(blk * LANES, (blk + 1) * LANES)
            v = q_ref[:, sl]
            if blk % 2 == 1:
                v = fn(v, cos_t, sin_t)
            qo_ref[:, sl] = v.astype(BF16)
        kvo_ref[...] = kv_ref[...].astype(BF16)

    row = pl.BlockSpec((tm, w), lambda i: (i, 0))
    tab = pl.BlockSpec((tm, LANES), lambda i: (i % nsb, 0))
    return _call(body, name, (t // tm,), [row, row, tab, tab], [row, row], [_sds((t, w), BF16)] * 2)(q, kv, cos, sin)


def _causal_mask(i, j, tq, tk):
    qpos = i * tq + lax.broadcasted_iota(jnp.int32, (tq, tk), 0)
    kpos = j * tk + lax.broadcasted_iota(jnp.int32, (tq, tk), 1)
    return kpos <= qpos


def mla_attn_fwd(q_b, kv_b, kp_b, batch, seq, name, tq=512):
    t, w = q_b.shape
    h_n = w // (2 * LANES)
    tq = _tile(seq, tq, SUBLANES)
    nq = seq // tq
    scale = (MLA_NOPE + MLA_ROPE) ** -0.5

    def body(q_ref, kv_ref, kp_ref, o_ref, lse_ref):
        i = pl.program_id(2)
        qn, qp = q_ref[:, :LANES], q_ref[:, LANES:]

        def step(j, carry, on_diagonal):
            m, l, acc = carry
            rows = pl.ds(pl.multiple_of(j * tq, tq), tq)
            s = (_dot_nt(qn, kv_ref[rows, :LANES]) + _dot_nt(qp, kp_ref[rows, :])) * scale
            if on_diagonal:
                s = jnp.where(_causal_mask(i, j, tq, tq), s, NEG_INF)
            mn = jnp.maximum(m, jnp.max(s, axis=-1, keepdims=True))
            p = jnp.exp(s - mn)
            a = jnp.exp(m - mn)
            return mn, a * l + jnp.sum(p, axis=-1, keepdims=True), a * acc + _dot(p.astype(BF16), kv_ref[rows, LANES:])

        init = (jnp.full((tq, 1), NEG_INF, F32), jnp.zeros((tq, 1), F32), jnp.zeros((tq, LANES), F32))
        before = lax.fori_loop(0, i, lambda j, carry: step(j, carry, False), init)
        m, l, acc = step(i, before, True)
        o_ref[...] = acc / l
        lse_ref[...] = jnp.broadcast_to(m + jnp.log(l), (tq, LANES))

    in_specs = [pl.BlockSpec((None, tq, 2 * LANES), lambda b, h, i: (b, i, h)),
                pl.BlockSpec((None, seq, 2 * LANES), lambda b, h, i: (b, 0, h)),
                pl.BlockSpec((None, seq, LANES), lambda b, h, i: (b, 0, 0))]
    out_spec = pl.BlockSpec((None, tq, LANES), lambda b, h, i: (b, i, h))
    o, lse = _call(body, name, (batch, h_n, nq), in_specs, [out_spec, out_spec],
                   [_sds((batch, seq, h_n * LANES), F32)] * 2)(
        q_b.reshape(batch, seq, w), kv_b.reshape(batch, seq, w), kp_b.reshape(batch, seq, LANES))
    return o.reshape(t, h_n * LANES), lse.reshape(t, h_n * LANES)


def mla_attn_bwd(q_b, kv_b, kp_b, do, lse, delta, batch, seq, name, tq=512):
    t, w = q_b.shape
    h_n = w // (2 * LANES)
    tq = _tile(seq, tq, SUBLANES)
    nq = seq // tq
    scale = (MLA_NOPE + MLA_ROPE) ** -0.5

    def body(q_ref, kv_ref, kp_ref, do_ref, lse_ref, dl_ref, dq_ref, dkv_ref, dkp_ref):
        dq_ref[...] = jnp.zeros_like(dq_ref)
        dkv_ref[...] = jnp.zeros_like(dkv_ref)

        @pl.when(pl.program_id(1) == 0)
        def _():
            dkp_ref[...] = jnp.zeros_like(dkp_ref)

        def outer(j, _):
            krows = pl.ds(pl.multiple_of(j * tq, tq), tq)
            kn, v, kp = kv_ref[krows, :LANES], kv_ref[krows, LANES:], kp_ref[krows, :]

            def pair(i, on_diagonal):
                qrows = pl.ds(pl.multiple_of(i * tq, tq), tq)
                qn, qp = q_ref[qrows, :LANES], q_ref[qrows, LANES:]
                dout = do_ref[qrows, :].astype(BF16)
                s = (_dot_nt(qn, kn) + _dot_nt(qp, kp)) * scale
                if on_diagonal:
                    s = jnp.where(_causal_mask(i, j, tq, tq), s, NEG_INF)
                p = jnp.exp(s - lse_ref[qrows, 0:1])
                ds = (p * (_dot_nt(dout, v) - dl_ref[qrows, 0:1]) * scale).astype(BF16)
                dq_ref[qrows, :LANES] += _dot(ds, kn)
                dq_ref[qrows, LANES:] += _dot(ds, kp)
                dkv_ref[krows, :LANES] += _dot_tn(ds, qn)
                dkv_ref[krows, LANES:] += _dot_tn(p.astype(BF16), dout)
                dkp_ref[krows, :] += _dot_tn(ds, qp)
                return 0

            pair(j, True)
            lax.fori_loop(j + 1, nq, lambda i, _: pair(i, False), 0)
            return 0

        lax.fori_loop(0, nq, outer, 0)

    wide = pl.BlockSpec((None, seq, 2 * LANES), lambda b, h: (b, 0, h))
    head = pl.BlockSpec((None, seq, LANES), lambda b, h: (b, 0, h))
    shared = pl.BlockSpec((None, seq, LANES), lambda b, h: (b, 0, 0))
    hv = [a.reshape(batch, seq, h_n * LANES) for a in (do, lse, delta)]
    dq, dkv, dkp = _call(body, name, (batch, h_n), [wide, wide, shared, head, head, head], [wide, wide, shared],
                         [_sds((batch, seq, w), F32), _sds((batch, seq, w), F32), _sds((batch, seq, LANES), F32)])(
        q_b.reshape(batch, seq, w), kv_b.reshape(batch, seq, w), kp_b.reshape(batch, seq, LANES), *hv)
    return dq.reshape(t, w), dkv.reshape(t, w), dkp.reshape(t, LANES)


def _sgu_common(z_ref, g_ref, b_ref, ws_ref, bs_ref):
    e = z_ref.shape[-1]
    ge = e // SG_GROUPS
    zu, zv = z_ref[0], z_ref[1]
    cu, cv = _gelu_cdf(zu), _gelu_cdf(zv)
    u = zu * cu
    xh, r = _ln_stats(zv * cv)
    vn = (xh * g_ref[...] + b_ref[...]).astype(BF16)
    tri = (lax.broadcasted_iota(jnp.int32, (SG_CHUNK, SG_CHUNK), 0)
           >= lax.broadcasted_iota(jnp.int32, (SG_CHUNK, SG_CHUNK), 1))
    ws = [jnp.where(tri, ws_ref[g], 0.0).astype(BF16) for g in range(SG_GROUPS)]
    v2 = [_dot(ws[g], vn[:, g * ge:(g + 1) * ge]) + bs_ref[:, g:g + 1] for g in range(SG_GROUPS)]
    return zu, zv, cu, cv, u, xh, r, vn, tri, ws, v2


def sgu_fwd(z2, ln_g, ln_b, w_s, b_s_t, name):
    _, t, e = z2.shape
    ge = e // SG_GROUPS

    def body(z_ref, g_ref, b_ref, ws_ref, bs_ref, y_ref):
        _, _, _, _, u, _, _, _, _, _, v2 = _sgu_common(z_ref, g_ref, b_ref, ws_ref, bs_ref)
        for g in range(SG_GROUPS):
            y_ref[:, g * ge:(g + 1) * ge] = (u[:, g * ge:(g + 1) * ge] * v2[g]).astype(BF16)

    vec = pl.BlockSpec((1, e), lambda i: (0, 0))
    in_specs = [pl.BlockSpec((2, SG_CHUNK, e), lambda i: (0, i, 0)), vec, vec,
                pl.BlockSpec((SG_GROUPS, SG_CHUNK, SG_CHUNK), lambda i: (0, 0, 0)),
                pl.BlockSpec((SG_CHUNK, SG_GROUPS), lambda i: (0, 0))]
    return _call(body, name, (t // SG_CHUNK,), in_specs, pl.BlockSpec((SG_CHUNK, e), lambda i: (i, 0)),
                 _sds((t, e), BF16))(z2, ln_g.reshape(1, e), ln_b.reshape(1, e), w_s, b_s_t)


def sgu_bwd(z2, dy, ln_g, ln_b, w_s, b_s_t, name):
    _, t, e = z2.shape
    ge = e // SG_GROUPS

    def body(z_ref, dy_ref, g_ref, b_ref, ws_ref, bs_ref, dz_ref, dbi_ref, dg_ref, db_ref, dws_ref, dbs_ref):
        @pl.when(pl.program_id(0) == 0)
        def _():
            for ref in (dbi_ref, dg_ref, db_ref, dws_ref, dbs_ref):
                ref[...] = jnp.zeros_like(ref)

        zu, zv, cu, cv, u, xh, r, vn, tri, ws, v2 = _sgu_common(z_ref, g_ref, b_ref, ws_ref, bs_ref)
        dy = dy_ref[...]
        dvn = []
        for g in range(SG_GROUPS):
            sl = slice(g * ge, (g + 1) * ge)
            dyg = dy[:, sl]
            du = dyg * v2[g] * _gelu_grad(zu[:, sl], cu[:, sl])
            dz_ref[0, :, sl] = du.astype(BF16)
            dbi_ref[0, :, sl] += jnp.sum(du, axis=0, keepdims=True)
            dv2 = dyg * u[:, sl]
            dbs_ref[:, g:g + 1] += jnp.sum(dv2, axis=-1, keepdims=True)
            dv2b = dv2.astype(BF16)
            dws_ref[g] += jnp.where(tri, _dot_nt(dv2b, vn[:, sl]), 0.0)
            dvn.append(_dot_tn(ws[g], dv2b))
        dvn = jnp.concatenate(dvn, axis=-1)
        dv, dg, db = _ln_bwd(dvn, xh, r, g_ref[...])
        dzv = dv * _gelu_grad(zv, cv)
        dz_ref[1] = dzv.astype(BF16)
        dbi_ref[1] += jnp.sum(dzv, axis=0, keepdims=True)
        dg_ref[...] += dg
        db_ref[...] += db

    vec = pl.BlockSpec((1, e), lambda i: (0, 0))
    wsb = pl.BlockSpec((SG_GROUPS, SG_CHUNK, SG_CHUNK), lambda i: (0, 0, 0))
    bsb = pl.BlockSpec((SG_CHUNK, SG_GROUPS), lambda i: (0, 0))
    zblk = pl.BlockSpec((2, SG_CHUNK, e), lambda i: (0, i, 0))
    in_specs = [zblk, pl.BlockSpec((SG_CHUNK, e), lambda i: (i, 0)), vec, vec, wsb, bsb]
    out_specs = [zblk, pl.BlockSpec((2, 1, e), lambda i: (0, 0, 0)), vec, vec, wsb, bsb]
    out_shape = [_sds((2, t, e), BF16), _sds((2, 1, e), F32), _sds((1, e), F32), _sds((1, e), F32),
                 _sds((SG_GROUPS, SG_CHUNK, SG_CHUNK), F32), _sds((SG_CHUNK, SG_GROUPS), F32)]
    return _call(body, name, (t // SG_CHUNK,), in_specs, out_specs, out_shape)(
        z2, dy, ln_g.reshape(1, e), ln_b.reshape(1, e), w_s, b_s_t)


_COL = ("cc_w_in", "ffn_w_up", "dsa_w_qkv", "dsa_w_o", "mla_w_qb", "mla_w_kvb", "sg_w_in")
_ROW = ("cc_w_out", "ffn_w_down", "mla_w_in", "mla_w_o", "sg_w_out")
_SMALL_COL = ("cc_dw_w", "ffn_dw_w")
_RELAID = ("cc_w_in", "dsa_w_o", "mla_w_kvb")
_MIXERS = ("cc", "dsa", "mla", "sg")


def _base(name):
    return name.split("_", 1)[1] if name[0] == "l" and name[1].isdigit() else name


def _ffn_fwd(pre, p, full, h_prev, mix_out, bsz, seq):
    t, d = h_prev.shape
    h, hn = rms_fwd(h_prev, p[pre + "norm_ffn"], pre + "ffn_norm", add=mix_out)
    w_up, w_down = full[pre + "ffn_w_up"], full[pre + "ffn_w_down"]
    f = w_down.shape[1]
    z2r = mm_nn(hn[None], w_up, pre + "ffn_up", out_parts=2).reshape(2, bsz, seq, f)
    dw_w2 = jnp.transpose(full[pre + "ffn_dw_w"].reshape(FFN_WIDTH, 2, f), (1, 0, 2))
    dw_b2 = p[pre + "ffn_dw_b"].reshape(2, 1, f)
    y = ffn_gate_fwd(z2r, dw_w2, dw_b2, pre + "ffn_gate").reshape(1, t, f)
    out = mm_nn(y, w_down, pre + "ffn_down")[0]
    return out, dict(h=h, hn=hn, z2r=z2r, y=y, dw_w2=dw_w2, dw_b2=dw_b2)


def relayout_cols(g, name, to_shards):
    k = g.shape[1]
    n = g.shape[2] // N_DEV if to_shards else g.shape[2]
    tk = _row_tile(k, n, itemsize=g.dtype.itemsize, budget=1 << 21)

    def body(i_ref, o_ref):
        o_ref[...] = i_ref[...]

    shard = pl.BlockSpec((None, tk, n), lambda j, ki: (j, ki, 0))
    whole = pl.BlockSpec((None, tk, n), lambda j, ki: (0, ki, j))
    if to_shards:
        return _call(body, name, (N_DEV, k // tk), [whole], shard, _sds((N_DEV, k, n), g.dtype))(g)
    return _call(body, name, (N_DEV, k // tk), [shard], whole, _sds((1, k, N_DEV * n), g.dtype))(g)


def _col_grad(a, b, w, name):
    g = mm_tn(a, b, w.shape[0], name)
    return g if w.shape[0] == N_DEV else relayout_cols(g, name + "_shards", True)


def _ffn_bwd(pre, p, full, ctx, dh_out, big_g, small_g, bsz, seq, mid):
    w_up, w_down = full[pre + "ffn_w_up"], full[pre + "ffn_w_down"]
    f = w_down.shape[1]
    dh_out, dh_b = dh_out
    t, d = dh_out.shape
    dh3 = dh_b[None]
    dyv = mm_nt(dh3, w_down, pre + "ffn_down_dx")
    big_g[pre + "ffn_w_down"] = mm_tn(ctx["y"], dh3, 1, pre + "ffn_down_dw").reshape(N_DEV, f // N_DEV, d)
    mid(dyv)
    dz2, ddw2, ddb2 = ffn_gate_bwd(ctx["z2r"], dyv.reshape(bsz, seq, f), ctx["dw_w2"], ctx["dw_b2"],
                                   pre + "ffn_gate_bwd")
    dz2 = dz2.reshape(2, t, f)
    big_g[pre + "ffn_w_up"] = mm_tn(ctx["hn"][None], dz2, N_DEV, pre + "ffn_up_dw")
    dhn = mm_nt(dz2, w_up, pre + "ffn_up_dx")
    small_g[pre + "ffn_dw_w"] = jnp.transpose(ddw2, (1, 0, 2)).reshape(FFN_WIDTH, 2 * f)
    small_g[pre + "ffn_dw_b"] = ddb2.reshape(2 * f)
    dh, dh_b, dg = rms_bwd(ctx["h"], p[pre + "norm_ffn"], dhn, dh_out, pre + "ffn_norm_bwd")
    small_g[pre + "norm_ffn"] = dg.reshape(d)
    return dh, dh_b


def _cc_fwd(pre, p, full, hn, bsz, seq, aux):
    t, c = hn.shape
    z2r = mm_nn(hn[None], full[pre + "cc_w_in"], pre + "cc_in", bias=p[pre + "cc_b_in"], out_parts=2).reshape(
        2, bsz, seq, c)
    cv = cc_conv_fwd(z2r, full[pre + "cc_dw_w"], p[pre + "cc_dw_b"], pre + "cc_conv").reshape(t, c)
    s = cc_ln_fwd(cv, p[pre + "cc_ln_g"], p[pre + "cc_ln_b"], pre + "cc_ln")
    out = mm_nn(s[None], full[pre + "cc_w_out"], pre + "cc_out", bias=p[pre + "cc_b_out"])[0]
    return out, dict(z2r=z2r, cv=cv, s=s)


def _cc_bwd(pre, p, full, ctx, hn, dm, big_g, small_g, bsz, seq, aux, mid):
    dm, dm_b = dm
    t, c = dm.shape
    dm3 = dm_b[None]
    small_g[pre + "cc_b_out"] = col_sum(dm, pre + "cc_bout_g").reshape(c)
    ds = mm_nt(dm3, full[pre + "cc_w_out"], pre + "cc_out_dx")
    big_g[pre + "cc_w_out"] = mm_tn(ctx["s"][None], dm3, 1, pre + "cc_out_dw").reshape(N_DEV, c // N_DEV, c)
    mid(ds)
    dcv, dlg, dlb = cc_ln_bwd(ctx["cv"], p[pre + "cc_ln_g"], p[pre + "cc_ln_b"], ds, pre + "cc_ln_bwd")
    dz2, dbi, ddw, ddb = cc_conv_bwd(ctx["z2r"], dcv.reshape(bsz, seq, c), full[pre + "cc_dw_w"], pre + "cc_conv_bwd")
    dz2 = dz2.reshape(2, t, c)
    big_g[pre + "cc_w_in"] = _col_grad(hn[None], dz2, full[pre + "cc_w_in"], pre + "cc_in_dw")
    small_g[pre + "cc_ln_g"], small_g[pre + "cc_ln_b"] = dlg.reshape(c), dlb.reshape(c)
    small_g[pre + "cc_b_in"], small_g[pre + "cc_dw_w"], small_g[pre + "cc_dw_b"] = dbi.reshape(2 * c), ddw, ddb.reshape(c)
    return mm_nt(dz2, full[pre + "cc_w_in"], pre + "cc_in_dx")


def _dsa_fwd(pre, p, full, hn, bsz, seq, aux):
    cos, sin = aux["dsa_tables"]
    qkv = mm_nn(hn[None], full[pre + "dsa_w_qkv"], pre + "dsa_qkv")[0]
    qkv_r = dsa_rope_fwd(qkv, cos, sin, seq, pre + "dsa_rope")
    for window, dil in DSA_CONFIGS:
        assert window // dil == DSA_BLOCK and (seq // dil) % DSA_BLOCK == 0
    o, o_b, lse = dsa_attn_all_fwd(qkv_r, bsz, seq, pre + "dsa_attn")
    out = mm_nn(o_b[None], full[pre + "dsa_w_o"], pre + "dsa_o")[0]
    return out, dict(qkv_r=qkv_r, o=o, o_b=o_b, lse=lse)


def _dsa_bwd(pre, p, full, ctx, hn, dm, big_g, small_g, bsz, seq, aux, mid):
    cos, sin = aux["dsa_tables"]
    dm, dm_b = dm
    dm3 = dm_b[None]
    do = mm_nt(dm3, full[pre + "dsa_w_o"], pre + "dsa_o_dx")
    big_g[pre + "dsa_w_o"] = _col_grad(ctx["o_b"][None], dm3, full[pre + "dsa_w_o"], pre + "dsa_o_dw")
    mid(do)
    delta = head_delta(do, ctx["o"], pre + "dsa_delta")
    grads = dsa_attn_all_bwd(ctx["qkv_r"], do, ctx["lse"], delta, bsz, seq, pre + "dsa_attn_bwd")
    dqkv = dsa_rope_bwd_all(grads, cos, sin, seq, pre + "dsa_rope_bwd")
    big_g[pre + "dsa_w_qkv"] = mm_tn(hn[None], dqkv, N_DEV, pre + "dsa_qkv_dw")
    return mm_nt(dqkv, full[pre + "dsa_w_qkv"], pre + "dsa_qkv_dx")


def _mla_weights(pre, full):
    rk, hd = MLA_RANK, MLA_NOPE + MLA_ROPE
    w_in = full[pre + "mla_w_in"][0]
    w_in_p = jnp.concatenate([w_in[:, :2 * rk], _spread_rope_cols(w_in[:, 2 * rk:])], axis=1)[None]
    w_qb = jnp.transpose(full[pre + "mla_w_qb"], (1, 0, 2)).reshape(rk, MLA_HEADS, hd)
    w_qb_p = jnp.concatenate([w_qb[..., :MLA_NOPE], _spread_rope_cols(w_qb[..., MLA_NOPE:])], axis=-1)
    return w_in_p, w_qb_p.reshape(1, rk, MLA_HEADS * 2 * LANES)


def _mla_fwd(pre, p, full, hn, bsz, seq, aux):
    cos, sin = aux["mla_tables"]
    w_in_p, w_qb_p = _mla_weights(pre, full)
    c = mm_nn(hn[None], w_in_p, pre + "mla_in")[0]
    qn, kn, kp = mla_low_fwd(c, p[pre + "mla_q_norm"], p[pre + "mla_kv_norm"], cos, sin, seq, pre + "mla_low")
    q = mm_nn(qn[None], w_qb_p, pre + "mla_qb")[0]
    kv = mm_nn(kn[None], full[pre + "mla_w_kvb"], pre + "mla_kvb")[0]
    q_b, kv_b = mla_rope_cast(q, kv, cos, sin, seq, pre + "mla_rope")
    o, lse = mla_attn_fwd(q_b, kv_b, kp, bsz, seq, pre + "mla_attn")
    o_b = o.astype(BF16)
    out = mm_nn(o_b[None], full[pre + "mla_w_o"], pre + "mla_o")[0]
    return out, dict(c=c, qn=qn, kn=kn, kp=kp, q_b=q_b, kv_b=kv_b, o=o, o_b=o_b, lse=lse, w_in_p=w_in_p, w_qb_p=w_qb_p)


def _mla_bwd(pre, p, full, ctx, hn, dm, big_g, small_g, bsz, seq, aux, mid):
    cos, sin = aux["mla_tables"]
    dm, dm_b = dm
    t, d = dm.shape
    rk = MLA_RANK
    dm3 = dm_b[None]
    do = mm_nt(dm3, full[pre + "mla_w_o"], pre + "mla_o_dx")
    big_g[pre + "mla_w_o"] = mm_tn(ctx["o_b"][None], dm3, 1, pre + "mla_o_dw").reshape(N_DEV, -1, d)
    mid(do)
    delta = head_delta(do, ctx["o"], pre + "mla_delta")
    dq, dkv, dkp = mla_attn_bwd(ctx["q_b"], ctx["kv_b"], ctx["kp"], do, ctx["lse"], delta, bsz, seq, pre + "mla_attn_bwd")
    dq_b, dkv_b = mla_rope_cast(dq, dkv, cos, sin, seq, pre + "mla_rope_bwd", transpose=True)
    g_qb = mm_tn(ctx["qn"][None], dq_b[None], 1, pre + "mla_qb_dw")[0].reshape(rk, MLA_HEADS, 2 * LANES)
    g_qb = jnp.concatenate([g_qb[..., :MLA_NOPE], _gather_rope_cols(g_qb[..., MLA_NOPE:])], axis=-1)
    big_g[pre + "mla_w_qb"] = jnp.transpose(g_qb.reshape(rk, N_DEV, -1), (1, 0, 2))
    dqn = mm_nt(dq_b[None], ctx["w_qb_p"], pre + "mla_qb_dx")
    big_g[pre + "mla_w_kvb"] = _col_grad(ctx["kn"][None], dkv_b[None], full[pre + "mla_w_kvb"], pre + "mla_kvb_dw")
    dkn = mm_nt(dkv_b[None], full[pre + "mla_w_kvb"], pre + "mla_kvb_dx")
    dc, dqg, dkg = mla_low_bwd(ctx["c"], p[pre + "mla_q_norm"], p[pre + "mla_kv_norm"], dqn, dkn, dkp, cos, sin, seq,
                               pre + "mla_low_bwd")
    small_g[pre + "mla_q_norm"], small_g[pre + "mla_kv_norm"] = dqg.reshape(rk), dkg.reshape(rk)
    g_in = mm_tn(hn[None], dc, 1, pre + "mla_in_dw")[0]
    g_in = jnp.concatenate([g_in[:, :2 * rk], _gather_rope_cols(g_in[:, 2 * rk:])], axis=1)
    big_g[pre + "mla_w_in"] = g_in.reshape(N_DEV, d // N_DEV, -1)
    return mm_nt(dc, ctx["w_in_p"], pre + "mla_in_dx")


def _sg_fwd(pre, p, full, hn, bsz, seq, aux):
    z2 = mm_nn(hn[None], full[pre + "sg_w_in"], pre + "sg_in", bias=p[pre + "sg_b_in"], out_parts=2)
    b_s_t = jnp.transpose(p[pre + "sg_b_s"])
    y = sgu_fwd(z2, p[pre + "sg_ln_g"], p[pre + "sg_ln_b"], p[pre + "sg_w_s"], b_s_t, pre + "sg_mix")
    out = mm_nn(y[None], full[pre + "sg_w_out"], pre + "sg_out", bias=p[pre + "sg_b_out"])[0]
    return out, dict(z2=z2, y=y, b_s_t=b_s_t)


def _sg_bwd(pre, p, full, ctx, hn, dm, big_g, small_g, bsz, seq, aux, mid):
    dm, dm_b = dm
    t, d = dm.shape
    dm3 = dm_b[None]
    small_g[pre + "sg_b_out"] = col_sum(dm, pre + "sg_bout_g").reshape(d)
    dy = mm_nt(dm3, full[pre + "sg_w_out"], pre + "sg_out_dx")
    big_g[pre + "sg_w_out"] = mm_tn(ctx["y"][None], dm3, 1, pre + "sg_out_dw").reshape(N_DEV, -1, d)
    mid(dy)
    dz2, dbi, dlg, dlb, dws, dbs_t = sgu_bwd(ctx["z2"], dy, p[pre + "sg_ln_g"], p[pre + "sg_ln_b"], p[pre + "sg_w_s"],
                                             ctx["b_s_t"], pre + "sg_mix_bwd")
    big_g[pre + "sg_w_in"] = mm_tn(hn[None], dz2, N_DEV, pre + "sg_in_dw")
    small_g[pre + "sg_b_in"] = dbi.reshape(-1)
    small_g[pre + "sg_ln_g"], small_g[pre + "sg_ln_b"] = dlg.reshape(-1), dlb.reshape(-1)
    small_g[pre + "sg_w_s"], small_g[pre + "sg_b_s"] = dws, jnp.transpose(dbs_t)
    return mm_nt(dz2, full[pre + "sg_w_in"], pre + "sg_in_dx")


_MIX_FWD = dict(cc=_cc_fwd, dsa=_dsa_fwd, mla=_mla_fwd, sg=_sg_fwd)
_MIX_BWD = dict(cc=_cc_bwd, dsa=_dsa_bwd, mla=_mla_bwd, sg=_sg_bwd)


def _reduce_scatter(parts, core, chip, tag, adams=None, after=()):
    recv = rs_core_exchange(parts, tag + "_cores", after)
    sums = [rs_core_add(pt, rc, core, "%s_add%d" % (tag, i)) for i, (pt, rc) in enumerate(zip(parts, recv))]
    recv2 = rs_chip_exchange(sums, tag + "_chips")
    return [rs_finish(s, r2, chip, "%s_fin%d" % (tag, i), adam=None if adams is None else adams[i])
            for i, (s, r2) in enumerate(zip(sums, recv2))]


class _Schedule:
    def __init__(self):
        self.actions = {}

    def at(self, key, fn):
        self.actions.setdefault(key, []).append(fn)

    def run(self, key, x):
        after = list(x) if isinstance(x, (list, tuple)) else [x]
        for fn in self.actions.pop(key, []):
            fn(after)


def _aside(fn):
    saved = list(_PENDING)
    _PENDING.clear()
    out = fn()
    _PENDING[:0] = saved
    return out


def cast_bf16(w, after, name):
    r, c = w.shape
    tr = _row_tile(r, c, budget=1 << 21)
    na = len(after)

    def body(*refs):
        refs[1 + na][...] = refs[0][...].astype(BF16)

    blk = pl.BlockSpec((tr, c), lambda i: (i, 0))
    return pl.pallas_call(body, name=name, grid=(r // tr,), in_specs=[blk] + [_ANY] * na, out_specs=blk,
                          out_shape=_sds((r, c), BF16),
                          compiler_params=pltpu.CompilerParams(vmem_limit_bytes=VMEM_LIMIT))(w, *after)


def _gather_overlapped(sched, keys, shards_fn, n, dev, tag, deliver):
    state = {}

    def start(x):
        x = list(x) + list(_PENDING)
        shards = shards_fn(x)
        lands = [lax.dynamic_update_slice(lax.empty((N_DEV,) + s.shape, s.dtype), s[None], (dev,) + (0,) * s.ndim)
                 for s in shards]
        state["first"] = exchange_start(_ag_plan_first, 4 * n, shards, lands, x, tag +"_start")
        _PENDING.append(state["first"][4])

    def pass_on(x):
        _, lands = exchange_wait(_ag_plan_first, state["first"], x, tag +"_wait")
        state["pass"] = exchange_start(_ag_plan_pass, 3 * n, [], lands, [], tag + "_pass")
        _PENDING.append(state["pass"][4])

    def done(x):
        deliver(exchange_wait(_ag_plan_pass, state["pass"], x, tag +"_done")[1])

    for key, fn in zip(keys, (start, pass_on, done)):
        sched.at(key, fn)


def _reduce_overlapped(sched, keys, parts, adams, core, chip, tag, deliver):
    n = len(parts)
    state = {}

    def start(x):
        lands = [lax.empty((4,) + pt.shape[1:], pt.dtype) for pt in parts]
        state["cores"] = exchange_start(_rs_plan_cores, 4 * n, parts, lands, x, tag +"_start")
        _PENDING.append(state["cores"][4])

    def middle(x):
        mine, recv = exchange_wait(_rs_plan_cores, state["cores"], x, tag +"_wait")
        sums = _aside(lambda: [rs_core_add(pt, rc, core, "%s_add%d" % (tag, i))
                               for i, (pt, rc) in enumerate(zip(mine, recv))])
        lands = [lax.empty((3,) + s.shape[1:], s.dtype) for s in sums]
        state["chips"] = exchange_start(_rs_plan_chips, 3 * n, sums, lands, [], tag + "_send")
        _PENDING.append(state["chips"][4])

    def done(x):
        sums, recv = exchange_wait(_rs_plan_chips, state["chips"], x, tag +"_done")
        deliver(_aside(lambda: [rs_finish(s, r2, chip, "%s_fin%d" % (tag, i), adam=adams[i])
                                for i, (s, r2) in enumerate(zip(sums, recv))]))

    for key, fn in zip(keys, (start, middle, done)):
        sched.at(key, fn)


def _pack(arrays):
    flat = jnp.concatenate([a.reshape(-1) for a in arrays])
    unit = N_DEV * SUBLANES * LANES
    padded = -(-flat.shape[0] // unit) * unit
    return jnp.pad(flat, (0, padded - flat.shape[0]))


def _unpack(flat, like):
    out, pos = [], 0
    for a in like:
        out.append(flat[pos:pos + a.size].reshape(a.shape))
        pos += a.size
    return out


def _train_step(p):
    names = list(p)
    wnames = names[1:names.index("loss_target")]
    x = p["x"]
    bsz, seq, d = x.shape
    t = bsz * seq
    xi, yi, ci = lax.axis_index("x"), lax.axis_index("y"), lax.axis_index("c")
    core = jnp.reshape(ci, (1,)).astype(jnp.int32)
    chip = jnp.reshape(2 * xi + yi, (1,)).astype(jnp.int32)
    dev = 4 * xi + 2 * yi + ci
    n_layers = 1 + max(int(n[1]) for n in wnames if n[0] == "l" and n[1].isdigit())
    big = [n for n in wnames if _base(n) in _COL + _ROW]
    aux = dict(dsa_tables=rope_tables_full(seq, LANES), mla_tables=mla_rope_tables(seq))

    sched = _Schedule()
    full = {}

    def deliver_weights(grp):
        def deliver(lands):
            for n, g in zip(grp, lands):
                if _base(n) in _RELAID:
                    full[n] = _aside(lambda: relayout_cols(g, n + "_whole", False))
                elif _base(n) in _COL:
                    full[n] = g
                elif _base(n) in _ROW:
                    full[n] = g.reshape(1, N_DEV * g.shape[1], g.shape[2])
                else:
                    full[n] = jnp.transpose(g, (1, 0, 2)).reshape(g.shape[1], N_DEV * g.shape[2])
        return deliver

    groups = []
    for layer in range(n_layers):
        lw = [n for n in wnames if n.startswith("l%d_" % layer) and _base(n) in _COL + _ROW + _SMALL_COL]
        groups += [[n for n in lw if "_ffn_" not in n], [n for n in lw if "_ffn_" in n]]

    def shards_of(grp):
        def shards(after):
            if not after:
                return [p[n] if _base(n) in _SMALL_COL else p[n].astype(BF16) for n in grp]
            return [p[n] if _base(n) in _SMALL_COL else cast_bf16(p[n], after, n + "_bf16") for n in grp]
        return shards

    for s in range(len(groups)):
        if s == 0:
            keys = [("f", -1), ("first", 0), ("first", 0)]
        elif s == 1:
            keys = [("f", -1), ("f", 1), ("f", 1)]
        else:
            keys = [("f", s - 2 if s > 2 else s - 1), ("f", s - 1 if s > 2 else s), ("f", s)]
        _gather_overlapped(sched, keys, shards_of(groups[s]), len(groups[s]), dev, "ag_s%d" % s,
                           deliver_weights(groups[s]))
    sched.run(("f", -1), [])

    h = x.reshape(t, d)
    ffn_out = None
    ctxs = []
    for layer in range(n_layers):
        pre = "l%d_" % layer
        kind = _MIXERS[layer % len(_MIXERS)]
        if ffn_out is None:
            hn = rms_fwd(h, p[pre + "norm_mix"], pre + "mix_norm")
        else:
            h, hn = rms_fwd(h, p[pre + "norm_mix"], pre + "mix_norm", add=ffn_out)
        if layer == 0:
            sched.run(("first", 0), hn)
        sched.run(("f", 2 * layer), [hn] + (list(full.values()) if layer == 0 else []))
        mix_out, mctx = _MIX_FWD[kind](pre, p, full, hn, bsz, seq, aux)
        sched.run(("f", 2 * layer + 1), mix_out)
        ffn_out, fctx = _ffn_fwd(pre, p, full, h, mix_out, bsz, seq)
        ctxs.append((h, hn, mctx, fctx))
        h = fctx["h"]

    loss_row, dh, dh_b, dfin = final_loss(h, ffn_out, p["final_norm"], p["loss_target"].reshape(t, d), "final_loss")
    dh = (dh, dh_b)
    loss = lax.psum(loss_row[0, 0], ("x", "y", "c"))

    small_g = {"final_norm": dfin.reshape(d)}
    results = {}

    def deliver_grads(lnames):
        def deliver(outs):
            for n, o in zip(lnames, outs):
                results[n] = o
        return deliver

    def reduce_later(q, lnames, big_g, tag):
        keys = [("b", q + 2), ("b", q + 3), ("b", q + 5)]
        _reduce_overlapped(sched, keys, [big_g[n] for n in lnames], [(p[n], p["m_" + n], p["v_" + n]) for n in lnames],
                           core, chip, tag, deliver_grads(lnames))

    q = 0
    for layer in reversed(range(n_layers)):
        pre = "l%d_" % layer
        kind = _MIXERS[layer % len(_MIXERS)]
        h_in, hn, mctx, fctx = ctxs[layer]
        big_g = {}
        sched.run(("b", q), dh[0])
        dh = _ffn_bwd(pre, p, full, fctx, dh, big_g, small_g, bsz, seq, lambda v, q=q: sched.run(("b", q + 1), v))
        reduce_later(q, [n for n in big if n.startswith(pre) and "_ffn_" in n], big_g, "rs_l%d_ffn" % layer)
        q += 2
        sched.run(("b", q), dh[0])
        dhn = _MIX_BWD[kind](pre, p, full, mctx, hn, dh, big_g, small_g, bsz, seq, aux,
                             lambda v, q=q: sched.run(("b", q + 1), v))
        dh_f, dh_b, dg = rms_bwd(h_in, p[pre + "norm_mix"], dhn, dh[0], pre + "mix_norm_bwd")
        dh = (dh_f, dh_b)
        small_g[pre + "norm_mix"] = dg.reshape(d)
        reduce_later(q, [n for n in big if n.startswith(pre) and "_ffn_" not in n], big_g, "rs_l%d_mix" % layer)
        q += 2
    for tail in range(q, q + 4):
        sched.run(("b", tail), dh[0])
    assert not sched.actions, sched.actions.keys()

    small = [n for n in wnames if n not in big]
    packed = _pack([small_g[n] for n in small]).reshape(N_DEV, -1, LANES)
    settled = [results[n][1] for n in big if not n.startswith("l0_")]
    reduced = _reduce_scatter([packed], core, chip, "rs_small", after=settled)[0]
    gathered = all_gather([reduced], "ag_small")[0].reshape(-1)
    g_local = []
    for n, g in zip(small, _unpack(gathered, [small_g[n] for n in small])):
        if _base(n) in _SMALL_COL:
            width = p[n].shape[1]
            g = lax.dynamic_slice_in_dim(g, dev * width, width, axis=1)
        g_local.append(g)
    stacked = [_pack(v).reshape(-1, LANES) for v in
               ([p[n] for n in small], g_local, [p["m_" + n] for n in small], [p["v_" + n] for n in small])]
    upd = adam_flat(*stacked, "adam_small")
    upd = [_unpack(u.reshape(-1), g_local) for u in upd]
    for i, n in enumerate(small):
        results[n] = (g_local[i], upd[0][i], upd[1][i], upd[2][i])

    grad_x = dh[0].reshape(bsz, seq, d)
    cols = [[results[n][k] for n in wnames] for k in range(4)]
    return (loss, grad_x, *cols[0], *cols[1], *cols[2], *cols[3])


def kernel(x, l0_norm_mix, l0_cc_w_in, l0_cc_b_in, l0_cc_dw_w, l0_cc_dw_b, l0_cc_ln_g, l0_cc_ln_b, l0_cc_w_out, l0_cc_b_out, l0_norm_ffn, l0_ffn_w_up, l0_ffn_dw_w, l0_ffn_dw_b, l0_ffn_w_down, l1_norm_mix, l1_dsa_w_qkv, l1_dsa_w_o, l1_norm_ffn, l1_ffn_w_up, l1_ffn_dw_w, l1_ffn_dw_b, l1_ffn_w_down, l2_norm_mix, l2_mla_w_in, l2_mla_q_norm, l2_mla_w_qb, l2_mla_kv_norm, l2_mla_w_kvb, l2_mla_w_o, l2_norm_ffn, l2_ffn_w_up, l2_ffn_dw_w, l2_ffn_dw_b, l2_ffn_w_down, l3_norm_mix, l3_sg_w_in, l3_sg_b_in, l3_sg_ln_g, l3_sg_ln_b, l3_sg_w_s, l3_sg_b_s, l3_sg_w_out, l3_sg_b_out, l3_norm_ffn, l3_ffn_w_up, l3_ffn_dw_w, l3_ffn_dw_b, l3_ffn_w_down, final_norm, loss_target, m_l0_norm_mix, m_l0_cc_w_in, m_l0_cc_b_in, m_l0_cc_dw_w, m_l0_cc_dw_b, m_l0_cc_ln_g, m_l0_cc_ln_b, m_l0_cc_w_out, m_l0_cc_b_out, m_l0_norm_ffn, m_l0_ffn_w_up, m_l0_ffn_dw_w, m_l0_ffn_dw_b, m_l0_ffn_w_down, m_l1_norm_mix, m_l1_dsa_w_qkv, m_l1_dsa_w_o, m_l1_norm_ffn, m_l1_ffn_w_up, m_l1_ffn_dw_w, m_l1_ffn_dw_b, m_l1_ffn_w_down, m_l2_norm_mix, m_l2_mla_w_in, m_l2_mla_q_norm, m_l2_mla_w_qb, m_l2_mla_kv_norm, m_l2_mla_w_kvb, m_l2_mla_w_o, m_l2_norm_ffn, m_l2_ffn_w_up, m_l2_ffn_dw_w, m_l2_ffn_dw_b, m_l2_ffn_w_down, m_l3_norm_mix, m_l3_sg_w_in, m_l3_sg_b_in, m_l3_sg_ln_g, m_l3_sg_ln_b, m_l3_sg_w_s, m_l3_sg_b_s, m_l3_sg_w_out, m_l3_sg_b_out, m_l3_norm_ffn, m_l3_ffn_w_up, m_l3_ffn_dw_w, m_l3_ffn_dw_b, m_l3_ffn_w_down, m_final_norm, v_l0_norm_mix, v_l0_cc_w_in, v_l0_cc_b_in, v_l0_cc_dw_w, v_l0_cc_dw_b, v_l0_cc_ln_g, v_l0_cc_ln_b, v_l0_cc_w_out, v_l0_cc_b_out, v_l0_norm_ffn, v_l0_ffn_w_up, v_l0_ffn_dw_w, v_l0_ffn_dw_b, v_l0_ffn_w_down, v_l1_norm_mix, v_l1_dsa_w_qkv, v_l1_dsa_w_o, v_l1_norm_ffn, v_l1_ffn_w_up, v_l1_ffn_dw_w, v_l1_ffn_dw_b, v_l1_ffn_w_down, v_l2_norm_mix, v_l2_mla_w_in, v_l2_mla_q_norm, v_l2_mla_w_qb, v_l2_mla_kv_norm, v_l2_mla_w_kvb, v_l2_mla_w_o, v_l2_norm_ffn, v_l2_ffn_w_up, v_l2_ffn_dw_w, v_l2_ffn_dw_b, v_l2_ffn_w_down, v_l3_norm_mix, v_l3_sg_w_in, v_l3_sg_b_in, v_l3_sg_ln_g, v_l3_sg_ln_b, v_l3_sg_w_s, v_l3_sg_b_s, v_l3_sg_w_out, v_l3_sg_b_out, v_l3_norm_ffn, v_l3_ffn_w_up, v_l3_ffn_dw_w, v_l3_ffn_dw_b, v_l3_ffn_w_down, v_final_norm):
    return _train_step(dict(locals()))
```

```python
import functools
import inspect
import math

import jax
import jax.numpy as jnp
from jax import lax
from jax.experimental import pallas as pl
from jax.experimental.pallas import tpu as pltpu

F32 = jnp.float32
BF16 = jnp.bfloat16
NORM_EPS = 1e-6
NEG_INF = -1e30
ROPE_THETA = 10000.0
LANES = 128
SUBLANES = 8
VMEM_LIMIT = 56 * 1024 * 1024
MM_VMEM_BUDGET = 44 * 1024 * 1024
N_DEV = 8

DSA_CONFIGS = ((128, 1), (512, 4), (2048, 16))
DSA_HEADS = 8
DSA_BLOCK = 128
MLA_HEADS = 16
MLA_RANK = 512
MLA_NOPE = 128
MLA_ROPE = 64
SG_CHUNK = 128
SG_GROUPS = 8
CC_WIDTH = 31
FFN_WIDTH = 3
ADAM_LR, ADAM_B1, ADAM_B2, ADAM_EPS, ADAM_WD, ADAM_STEP = 0.001, 0.9, 0.999, 1e-08, 0.01, 10


def _tile(n, cap, unit=LANES):
    if n <= cap:
        return n
    best = 0
    for t in range(unit, cap + 1, unit):
        if n % t == 0:
            best = t
    assert best, (n, cap, unit)
    return best


_PENDING = []


def _call(body, name, grid, in_specs, out_specs, out_shape, scratch=(), prefetch=0):
    params = pltpu.CompilerParams(vmem_limit_bytes=VMEM_LIMIT)
    deps = list(_PENDING)
    _PENDING.clear()
    if deps:
        inner, n_in = body, prefetch + len(in_specs)

        def body(*refs):
            return inner(*refs[:n_in], *refs[n_in + len(deps):])

        in_specs = list(in_specs) + [pl.BlockSpec(memory_space=pl.ANY)] * len(deps)
    if prefetch:
        spec = pltpu.PrefetchScalarGridSpec(num_scalar_prefetch=prefetch, grid=grid, in_specs=in_specs,
                                            out_specs=out_specs, scratch_shapes=list(scratch))
        fn = pl.pallas_call(body, out_shape=out_shape, grid_spec=spec, name=name, compiler_params=params)
    else:
        fn = pl.pallas_call(body, out_shape=out_shape, grid=grid, in_specs=in_specs, out_specs=out_specs,
                            scratch_shapes=list(scratch), name=name, compiler_params=params)
    return lambda *args: fn(*args, *deps)


def _sds(shape, dtype):
    return jax.ShapeDtypeStruct(tuple(shape), dtype)


def _bf(v):
    return v if v.dtype == BF16 else v.astype(BF16)


def _dot(a, b):
    return jnp.dot(a, b, preferred_element_type=F32)


def _dot_nt(a, b):
    return lax.dot_general(a, b, (((1,), (1,)), ((), ())), preferred_element_type=F32)


def _dot_tn(a, b):
    return lax.dot_general(a, b, (((0,), (0,)), ((), ())), preferred_element_type=F32)


def _sigmoid(v):
    return 1.0 / (1.0 + jnp.exp(-v))


_ERF_A = (-2.72614225801306e-10, 2.77068142495902e-08, -2.10102402082508e-06, -5.69250639462346e-05,
          -7.34990630326855e-04, -2.95459980854025e-03, -1.60960333262415e-02)
_ERF_B = (-1.45660718464996e-05, -2.13374055278905e-04, -1.68282697438203e-03, -7.37332916720468e-03,
          -1.42647390514189e-02)


def _erf(v):
    v = jnp.clip(v, -4.0, 4.0)
    v2 = v * v
    p = jnp.full_like(v, _ERF_A[0])
    for c in _ERF_A[1:]:
        p = p * v2 + c
    q = jnp.full_like(v, _ERF_B[0])
    for c in _ERF_B[1:]:
        q = q * v2 + c
    return v * p / q


def _gelu_cdf(v):
    return 0.5 * (1.0 + _erf(v * (2.0 ** -0.5)))


def _gelu_grad(v, cdf):
    return cdf + v * jnp.exp(-0.5 * v * v) * ((2.0 * math.pi) ** -0.5)


def _edge_rows(width):
    return -(-(width - 1) // SUBLANES) * SUBLANES


def _conv_causal(z, w_row, width):
    hr = _edge_rows(width)
    rows = lax.broadcasted_iota(jnp.int32, (hr, z.shape[1]), 0)
    out = z * w_row(width - 1)
    head = z[:hr] * w_row(width - 1)
    for j in range(1, width):
        tap = pltpu.roll(z, j, 0)
        out = out + tap * w_row(width - 1 - j)
        head = head + jnp.where(rows >= j, tap[:hr], 0.0) * w_row(width - 1 - j)
    return jnp.concatenate([head, out[hr:]], axis=0)


def _conv_causal_bwd(x, dc, w_row, width):
    n = dc.shape[0]
    hr = _edge_rows(width)
    rows = lax.broadcasted_iota(jnp.int32, (hr, dc.shape[1]), 0)
    x_tail, dc_tail = x[n - hr:], dc[n - hr:]
    dx = dc * w_row(width - 1)
    tail = dc_tail * w_row(width - 1)
    dws = {width - 1: jnp.sum(dc * x, axis=0, keepdims=True)}
    for j in range(1, width):
        up = pltpu.roll(dc, n - j, 0)
        up_tail = jnp.where(rows < hr - j, up[n - hr:], 0.0)
        dx = dx + up * w_row(width - 1 - j)
        tail = tail + up_tail * w_row(width - 1 - j)
        wrapped = jnp.sum((up[n - hr:] - up_tail) * x_tail, axis=0, keepdims=True)
        dws[width - 1 - j] = jnp.sum(up * x, axis=0, keepdims=True) - wrapped
    return jnp.concatenate([dx[:n - hr], tail], axis=0), dws, jnp.sum(dc, axis=0, keepdims=True)


def _rope(v, cos, sin):
    return v * cos + pltpu.roll(v, 64, 1) * sin


def _rope_t(dv, cos, sin):
    return dv * cos + pltpu.roll(dv * sin, 64, 1)


def mm_nn(a, w, name, bias=None, out_parts=1, out_dtype=F32, tm_cap=1024, tn_cap=1536, tk_cap=6144):
    pa, m, kp = a.shape
    j, k, n = w.shape
    assert pa * kp == k
    big_n = j * n
    npo = big_n // out_parts
    tm = _tile(m, tm_cap, SUBLANES)
    tk = _tile(kp, tk_cap)
    tn = _tile(math.gcd(n, npo), tn_cap if tk <= 2816 else 1024)
    out_bytes = jnp.dtype(out_dtype).itemsize

    def vmem_bytes(rows):
        return 2 * (rows * tk * 2 + tk * tn * 2 + rows * tn * out_bytes) + (rows * tn * 4 if k > tk else 0)

    while vmem_bytes(tm) > MM_VMEM_BUDGET and tm % (2 * SUBLANES) == 0:
        tm //= 2
    npj, nbo, kbp, nk = n // tn, npo // tn, kp // tk, k // tk
    has_bias = bias is not None

    def body(*refs):
        a_ref, w_ref = refs[0], refs[1]
        b_ref = refs[2] if has_bias else None
        o_ref = refs[2 + has_bias]

        def finish(acc):
            if has_bias:
                acc = acc + b_ref[...]
            o_ref[...] = acc.astype(o_ref.dtype)

        part = _dot(_bf(a_ref[...]), _bf(w_ref[...]))
        if nk == 1:
            finish(part)
        else:
            acc_ref = refs[3 + has_bias]
            ki = pl.program_id(2)

            @pl.when(ki == 0)
            def _():
                acc_ref[...] = part

            @pl.when(ki > 0)
            def _():
                acc_ref[...] += part

            @pl.when(ki == nk - 1)
            def _():
                finish(acc_ref[...])

    in_specs = [pl.BlockSpec((None, tm, tk), lambda ni, mi, ki: (ki // kbp, mi, ki % kbp)),
                pl.BlockSpec((None, tk, tn), lambda ni, mi, ki: (ni // npj, ki, ni % npj))]
    args = [a, w]
    if has_bias:
        in_specs.append(pl.BlockSpec((1, tn), lambda ni, mi, ki: (0, ni)))
        args.append(bias.reshape(1, big_n))
    out_spec = pl.BlockSpec((None, tm, tn), lambda ni, mi, ki: (ni // nbo, mi, ni % nbo))
    scratch = [pltpu.VMEM((tm, tn), F32)] if nk > 1 else []
    return _call(body, name, (big_n // tn, m // tm, nk), in_specs, out_spec,
                 _sds((out_parts, m, npo), out_dtype), scratch)(*args)


def mm_nt(a, w, name, out_dtype=F32, tm_cap=1024, tko_cap=2048, tc_cap=2048):
    pa, m, npa = a.shape
    j, k, n = w.shape
    assert pa * npa == j * n
    tm = _tile(m, tm_cap, SUBLANES)
    tko = _tile(k, tko_cap)
    tc = _tile(math.gcd(n, npa), tc_cap)
    out_bytes = jnp.dtype(out_dtype).itemsize

    def vmem_bytes(rows):
        return 2 * (rows * tc * 2 + tko * tc * 2 + rows * tko * out_bytes) + (rows * tko * 4 if j * n > tc else 0)

    while vmem_bytes(tm) > MM_VMEM_BUDGET and tm % (2 * SUBLANES) == 0:
        tm //= 2
    npj, nba, nc = n // tc, npa // tc, (j * n) // tc

    def body(a_ref, w_ref, o_ref, *scr):
        part = _dot_nt(_bf(a_ref[...]), _bf(w_ref[...]))
        if nc == 1:
            o_ref[...] = part.astype(o_ref.dtype)
        else:
            acc_ref = scr[0]
            ci = pl.program_id(2)

            @pl.when(ci == 0)
            def _():
                acc_ref[...] = part

            @pl.when(ci > 0)
            def _():
                acc_ref[...] += part

            @pl.when(ci == nc - 1)
            def _():
                o_ref[...] = acc_ref[...].astype(o_ref.dtype)

    in_specs = [pl.BlockSpec((None, tm, tc), lambda mi, ko, ci: (ci // nba, mi, ci % nba)),
                pl.BlockSpec((None, tko, tc), lambda mi, ko, ci: (ci // npj, ko, ci % npj))]
    out_spec = pl.BlockSpec((tm, tko), lambda mi, ko, ci: (mi, ko))
    scratch = [pltpu.VMEM((tm, tko), F32)] if nc > 1 else []
    return _call(body, name, (m // tm, k // tko, nc), in_specs, out_spec, _sds((m, k), out_dtype), scratch)(a, w)


def mm_tn(a, b, n_shards, name, out_dtype=BF16, tt_cap=4096, tkk_cap=512, tn_cap=1536):
    pa, m, kp = a.shape
    pb, m2, npb = b.shape
    assert m == m2
    k, big_n = pa * kp, pb * npb
    n = big_n // n_shards
    tt = _tile(m, tt_cap, SUBLANES)
    tkk = _tile(kp, tkk_cap)
    tn = _tile(math.gcd(n, npb), tn_cap)
    kbp, nbb, npj, nt = kp // tkk, npb // tn, n // tn, m // tt

    def body(a_ref, b_ref, o_ref, *scr):
        part = _dot_tn(_bf(a_ref[...]), _bf(b_ref[...]))
        if nt == 1:
            o_ref[...] = part.astype(o_ref.dtype)
        else:
            acc_ref = scr[0]
            ti = pl.program_id(2)

            @pl.when(ti == 0)
            def _():
                acc_ref[...] = part

            @pl.when(ti > 0)
            def _():
                acc_ref[...] += part

            @pl.when(ti == nt - 1)
            def _():
                o_ref[...] = acc_ref[...].astype(o_ref.dtype)

    in_specs = [pl.BlockSpec((None, tt, tkk), lambda ni, ki, ti: (ki // kbp, ti, ki % kbp)),
                pl.BlockSpec((None, tt, tn), lambda ni, ki, ti: (ni // nbb, ti, ni % nbb))]
    out_spec = pl.BlockSpec((None, tkk, tn), lambda ni, ki, ti: (ni // npj, ki, ni % npj))
    scratch = [pltpu.VMEM((tkk, tn), F32)] if nt > 1 else []
    return _call(body, name, (big_n // tn, k // tkk, nt), in_specs, out_spec,
                 _sds((n_shards, k, n), out_dtype), scratch)(a, b)


_ANY = pl.BlockSpec(memory_space=pl.ANY)
_MESH = pl.DeviceIdType.MESH


def all_gather(shards, name):
    n = len(shards)

    def body(*refs):
        ins, outs = refs[:n], refs[n:2 * n]
        send_sems, recv_sems, local_sems = refs[2 * n:]
        x, y, c = lax.axis_index("x"), lax.axis_index("y"), lax.axis_index("c")
        me, sibling = (x, y, c), (x, y, 1 - c)
        chips = [(1 - x, y), (x, 1 - y), (1 - x, 1 - y)]

        def copy(i, k, block, to, src=None):
            px, py, pc = block
            dst = outs[i].at[4 * px + 2 * py + pc]
            return pltpu.make_async_remote_copy(src_ref=dst if src is None else src, dst_ref=dst,
                                                send_sem=send_sems.at[i, k], recv_sem=recv_sems.at[i, k],
                                                device_id=to, device_id_type=_MESH)

        mine = [pltpu.make_async_copy(ins[i], outs[i].at[4 * x + 2 * y + c], local_sems.at[i]) for i in range(n)]
        for cp in mine:
            cp.start()
        first = []
        for i in range(n):
            first.append(copy(i, 0, me, sibling, src=ins[i]))
            first += [copy(i, 1 + q, me, (*chip, c), src=ins[i]) for q, chip in enumerate(chips)]
        for cp in first:
            cp.start()
        passed = []
        for q, chip in enumerate(chips):
            for i in range(n):
                copy(i, 1 + q, (*chip, c), me).wait_recv()
                cp = copy(i, 4 + q, (*chip, c), sibling)
                cp.start()
                passed.append(cp)
        for i in range(n):
            copy(i, 0, sibling, me).wait_recv()
            for q, chip in enumerate(chips):
                copy(i, 4 + q, (*chip, 1 - c), me).wait_recv()
        for cp in first + passed:
            cp.wait_send()
        for cp in mine:
            cp.wait()

    outs = pl.pallas_call(
        body, name=name, out_shape=[_sds((N_DEV,) + s.shape, s.dtype) for s in shards],
        in_specs=[_ANY] * n, out_specs=[_ANY] * n,
        scratch_shapes=[pltpu.SemaphoreType.DMA((n, 7)), pltpu.SemaphoreType.DMA((n, 7)),
                        pltpu.SemaphoreType.DMA((n,))],
    )(*shards)
    return list(outs)


def rs_core_exchange(parts, name, after=()):
    n, na = len(parts), len(after)

    def body(*refs):
        ins, outs = refs[:n], refs[n + na:2 * n + na]
        send_sems, recv_sems = refs[2 * n + na:]
        x, y, c = lax.axis_index("x"), lax.axis_index("y"), lax.axis_index("c")
        sibling = (x, y, 1 - c)
        started = []
        for i in range(n):
            for chip in range(4):
                cp = pltpu.make_async_remote_copy(src_ref=ins[i].at[2 * chip + (1 - c)], dst_ref=outs[i].at[chip],
                                                  send_sem=send_sems.at[i, chip], recv_sem=recv_sems.at[i, chip],
                                                  device_id=sibling, device_id_type=_MESH)
                cp.start()
                started.append(cp)
        for cp in started:
            cp.wait()

    outs = pl.pallas_call(
        body, name=name, out_shape=[_sds((4,) + p.shape[1:], p.dtype) for p in parts],
        in_specs=[_ANY] * (n + na), out_specs=[_ANY] * n,
        scratch_shapes=[pltpu.SemaphoreType.DMA((n, 4)), pltpu.SemaphoreType.DMA((n, 4))],
    )(*parts, *after)
    return list(outs)


def rs_chip_exchange(sums, name):
    n = len(sums)

    def body(*refs):
        ins, outs = refs[:n], refs[n:2 * n]
        send_sems, recv_sems = refs[2 * n:]
        x, y, c = lax.axis_index("x"), lax.axis_index("y"), lax.axis_index("c")
        peers = [(x, 1 - y), (1 - x, y), (1 - x, 1 - y)]
        started = []
        for i in range(n):
            for k, (px, py) in enumerate(peers):
                cp = pltpu.make_async_remote_copy(src_ref=ins[i].at[2 * px + py], dst_ref=outs[i].at[k],
                                                  send_sem=send_sems.at[i, k], recv_sem=recv_sems.at[i, k],
                                                  device_id=(px, py, c), device_id_type=_MESH)
                cp.start()
                started.append(cp)
        for cp in started:
            cp.wait()

    outs = pl.pallas_call(
        body, name=name, out_shape=[_sds((3,) + s.shape[1:], s.dtype) for s in sums],
        in_specs=[_ANY] * n, out_specs=[_ANY] * n,
        scratch_shapes=[pltpu.SemaphoreType.DMA((n, 3)), pltpu.SemaphoreType.DMA((n, 3))],
    )(*sums)
    return list(outs)


_HBM = pl.BlockSpec(memory_space=pltpu.HBM)
_SEM = pl.BlockSpec(memory_space=pltpu.SEMAPHORE)
_EFFECT = pltpu.SideEffectType.DATAFLOW_SIDE_EFFECTING


def _in_hbm(a):
    return pltpu.with_memory_space_constraint(a, pltpu.HBM)


def _plan_copies(plan, srcs, dsts, send_sems, recv_sems):
    x, y, c = lax.axis_index("x"), lax.axis_index("y"), lax.axis_index("c")
    return [pltpu.make_async_remote_copy(src_ref=s, dst_ref=d, send_sem=send_sems.at[q], recv_sem=recv_sems.at[q],
                                         device_id=to, device_id_type=_MESH)
            for q, (s, d, to) in enumerate(plan(x, y, c, srcs, dsts))]


def exchange_start(plan, n_copies, srcs, dsts, after, name):
    ns, nd, na = len(srcs), len(dsts), len(after)

    def body(*refs):
        ins = refs[:ns + nd + na]
        send_sems, recv_sems = refs[ns + nd + na], refs[ns + nd + na + 1]
        token = refs[-1]
        for cp in _plan_copies(plan, ins[:ns], ins[ns:ns + nd], send_sems, recv_sems):
            cp.start()
        token[...] = jnp.zeros_like(token)

    thru = [pltpu.HBM(a.shape, a.dtype) for a in list(srcs) + list(dsts)]
    out = pl.pallas_call(
        body, name=name,
        out_shape=[pltpu.SemaphoreType.DMA((n_copies,)), pltpu.SemaphoreType.DMA((n_copies,))] + thru
        + [_sds((SUBLANES, LANES), F32)],
        in_specs=[_HBM] * (ns + nd) + [_ANY] * na,
        out_specs=[_SEM, _SEM] + [_HBM] * (ns + nd) + [pl.BlockSpec(memory_space=pltpu.VMEM)],
        input_output_aliases={i: 2 + i for i in range(ns + nd)},
        compiler_params=pltpu.CompilerParams(has_side_effects=_EFFECT),
    )(*[_in_hbm(a) for a in list(srcs) + list(dsts)], *after)
    return out[0], out[1], list(out[2:2 + ns]), list(out[2 + ns:2 + ns + nd]), out[-1]


def exchange_wait(plan, started, after, name):
    send_sems, recv_sems, srcs, dsts, _ = started
    ns, nd, na = len(srcs), len(dsts), len(after)

    def body(*refs):
        ins = refs[:ns + nd]
        send_sems_ref, recv_sems_ref = refs[ns + nd], refs[ns + nd + 1]
        for cp in _plan_copies(plan, ins[:ns], ins[ns:ns + nd], send_sems_ref, recv_sems_ref):
            cp.wait_send()
            cp.wait_recv()

    thru = [pltpu.HBM(a.shape, a.dtype) for a in list(srcs) + list(dsts)]
    out = pl.pallas_call(
        body, name=name, out_shape=thru,
        in_specs=[_HBM] * (ns + nd) + [_SEM, _SEM] + [_ANY] * na,
        out_specs=[_HBM] * (ns + nd),
        input_output_aliases={i: i for i in range(ns + nd)},
        compiler_params=pltpu.CompilerParams(has_side_effects=_EFFECT),
    )(*srcs, *dsts, send_sems, recv_sems, *after)
    return list(out[:ns]), list(out[ns:])


def _dev_index(px, py, pc):
    return 4 * px + 2 * py + pc


def _ag_plan_first(x, y, c, srcs, dsts):
    me = _dev_index(x, y, c)
    peers = [(x, y, 1 - c), (1 - x, y, c), (x, 1 - y, c), (1 - x, 1 - y, c)]
    return [(s, d.at[me], to) for s, d in zip(srcs, dsts) for to in peers]


def _ag_plan_pass(x, y, c, srcs, dsts):
    chips = [(1 - x, y), (x, 1 - y), (1 - x, 1 - y)]
    return [(d.at[_dev_index(px, py, c)], d.at[_dev_index(px, py, c)], (x, y, 1 - c)) for d in dsts for px, py in chips]


def _rs_plan_cores(x, y, c, srcs, dsts):
    return [(s.at[2 * chip + (1 - c)], d.at[chip], (x, y, 1 - c)) for s, d in zip(srcs, dsts) for chip in range(4)]


def _rs_plan_chips(x, y, c, srcs, dsts):
    peers = [(x, 1 - y), (1 - x, y), (1 - x, 1 - y)]
    return [(s.at[2 * px + py], d.at[k], (px, py, c)) for s, d in zip(srcs, dsts) for k, (px, py) in enumerate(peers)]


def _row_tile(r, c, itemsize=4, budget=1 << 20):
    cap = max(SUBLANES, (budget // (c * itemsize)) // SUBLANES * SUBLANES)
    if r <= cap:
        return r
    best = 0
    for t in range(SUBLANES, cap + 1, SUBLANES):
        if r % t == 0:
            best = t
    return best if best else r


def rs_core_add(part, recv, core, name):
    _, r, c = part.shape
    tr = _row_tile(r, c, budget=1 << 22)

    def body(core_ref, p_ref, q_ref, o_ref):
        o_ref[...] = (p_ref[...].astype(F32) + q_ref[...].astype(F32)).astype(o_ref.dtype)

    in_specs = [pl.BlockSpec((None, tr, c), lambda ch, ri, core_ref: (2 * ch + core_ref[0], ri, 0)),
                pl.BlockSpec((None, tr, c), lambda ch, ri, core_ref: (ch, ri, 0))]
    out_spec = pl.BlockSpec((None, tr, c), lambda ch, ri, core_ref: (ch, ri, 0))
    return _call(body, name, (4, r // tr), in_specs, out_spec, _sds((4, r, c), part.dtype), prefetch=1)(core, part, recv)


def _adamw(w, g, m, v):
    m = ADAM_B1 * m + (1.0 - ADAM_B1) * g
    v = ADAM_B2 * v + (1.0 - ADAM_B2) * (g * g)
    m_hat = m / (1.0 - ADAM_B1 ** ADAM_STEP)
    v_hat = v / (1.0 - ADAM_B2 ** ADAM_STEP)
    delta = -ADAM_LR * (m_hat / (jnp.sqrt(v_hat) + ADAM_EPS) + ADAM_WD * w)
    return delta, m, v


def rs_finish(sums, recv, chip, name, adam=None):
    _, r, c = sums.shape
    tr = _row_tile(r, c, budget=1 << 20)

    def body(chip_ref, s_ref, q_ref, *refs):
        g = s_ref[...].astype(F32)
        for k in range(3):
            g = g + q_ref[k].astype(F32)
        if adam is None:
            refs[0][...] = g
        else:
            w_ref, m_ref, v_ref, g_ref, d_ref, nm_ref, nv_ref = refs
            g_ref[...] = g
            d_ref[...], nm_ref[...], nv_ref[...] = _adamw(w_ref[...], g, m_ref[...], v_ref[...])

    blk = pl.BlockSpec((tr, c), lambda ri, chip_ref: (ri, 0))
    in_specs = [pl.BlockSpec((None, tr, c), lambda ri, chip_ref: (chip_ref[0], ri, 0)),
                pl.BlockSpec((3, tr, c), lambda ri, chip_ref: (0, ri, 0))]
    args = [chip, sums, recv]
    if adam is None:
        out_specs, out_shape = blk, _sds((r, c), F32)
    else:
        in_specs += [blk] * 3
        args += list(adam)
        out_specs, out_shape = [blk] * 4, [_sds((r, c), F32)] * 4
    return _call(body, name, (r // tr,), in_specs, out_specs, out_shape, prefetch=1)(*args)


def adam_flat(w, g, m, v, name):
    r, c = w.shape
    tr = _row_tile(r, c, budget=1 << 19)

    def body(w_ref, g_ref, m_ref, v_ref, d_ref, nm_ref, nv_ref):
        d_ref[...], nm_ref[...], nv_ref[...] = _adamw(w_ref[...], g_ref[...], m_ref[...], v_ref[...])

    blk = pl.BlockSpec((tr, c), lambda ri: (ri, 0))
    return _call(body, name, (r // tr,), [blk] * 4, [blk] * 3, [_sds((r, c), F32)] * 3)(w, g, m, v)


def rms_fwd(x, g, name, add=None):
    t, d = x.shape
    tm = _tile(t, 256, SUBLANES)
    has_add = add is not None

    def body(*refs):
        if has_add:
            x_ref, a_ref, g_ref, h_ref, hn_ref = refs
            h = x_ref[...] + a_ref[...]
            h_ref[...] = h
        else:
            x_ref, g_ref, hn_ref = refs
            h = x_ref[...]
        r = lax.rsqrt(jnp.mean(h * h, axis=-1, keepdims=True) + NORM_EPS)
        hn_ref[...] = (h * r * g_ref[...]).astype(BF16)

    row = pl.BlockSpec((tm, d), lambda i: (i, 0))
    vec = pl.BlockSpec((1, d), lambda i: (0, 0))
    if has_add:
        return _call(body, name, (t // tm,), [row, row, vec], [row, row], [_sds((t, d), F32), _sds((t, d), BF16)])(
            x, add, g.reshape(1, d))
    return _call(body, name, (t // tm,), [row, vec], row, _sds((t, d), BF16))(x, g.reshape(1, d))


def _rms_bwd_math(x, g, dy):
    r = lax.rsqrt(jnp.mean(x * x, axis=-1, keepdims=True) + NORM_EPS)
    xh = x * r
    dyg = dy * g
    dx = r * (dyg - xh * jnp.mean(dyg * xh, axis=-1, keepdims=True))
    return dx, jnp.sum(dy * xh, axis=0, keepdims=True)


def rms_bwd(x, g, dhn, dres, name):
    t, d = x.shape
    tm = _tile(t, 256, SUBLANES)

    def body(x_ref, g_ref, dy_ref, dr_ref, dx_ref, dxb_ref, dg_ref):
        dx, dg = _rms_bwd_math(x_ref[...], g_ref[...], dy_ref[...])
        dx = dx + dr_ref[...]
        dx_ref[...] = dx
        dxb_ref[...] = dx.astype(BF16)

        @pl.when(pl.program_id(0) == 0)
        def _():
            dg_ref[...] = jnp.zeros_like(dg_ref)

        dg_ref[...] += dg

    row = pl.BlockSpec((tm, d), lambda i: (i, 0))
    vec = pl.BlockSpec((1, d), lambda i: (0, 0))
    return _call(body, name, (t // tm,), [row, vec, row, row], [row, row, vec],
                 [_sds((t, d), F32), _sds((t, d), BF16), _sds((1, d), F32)])(x, g.reshape(1, d), dhn, dres)


def final_loss(h, add, g, target, name):
    t, d = h.shape
    tm = _tile(t, 256, SUBLANES)

    def body(x_ref, a_ref, g_ref, t_ref, loss_ref, dx_ref, dxb_ref, dg_ref):
        x = x_ref[...] + a_ref[...]
        gain = g_ref[...]
        r = lax.rsqrt(jnp.mean(x * x, axis=-1, keepdims=True) + NORM_EPS)
        err = x * r * gain - t_ref[...]
        dx, dg = _rms_bwd_math(x, gain, err * (1.0 / d))
        dx_ref[...] = dx
        dxb_ref[...] = dx.astype(BF16)

        @pl.when(pl.program_id(0) == 0)
        def _():
            dg_ref[...] = jnp.zeros_like(dg_ref)
            loss_ref[...] = jnp.zeros_like(loss_ref)

        dg_ref[...] += dg
        part = jnp.sum(jnp.sum(err * err, axis=-1, keepdims=True), axis=0, keepdims=True) * (0.5 / d)
        loss_ref[...] += jnp.broadcast_to(part, loss_ref.shape)

    row = pl.BlockSpec((tm, d), lambda i: (i, 0))
    vec = pl.BlockSpec((1, d), lambda i: (0, 0))
    one = pl.BlockSpec((1, LANES), lambda i: (0, 0))
    return _call(body, name, (t // tm,), [row, row, vec, row], [one, row, row, vec],
                 [_sds((1, LANES), F32), _sds((t, d), F32), _sds((t, d), BF16), _sds((1, d), F32)])(
        h, add, g.reshape(1, d), target)


def ffn_gate_fwd(z2, dw_w2, dw_b2, name, tc=2 * LANES):
    _, b, s, f = z2.shape

    def body(z_ref, w_ref, b_ref, y_ref):
        g = _conv_causal(z_ref[0], lambda k: w_ref[0, k:k + 1, :], FFN_WIDTH) + b_ref[0]
        a = _conv_causal(z_ref[1], lambda k: w_ref[1, k:k + 1, :], FFN_WIDTH) + b_ref[1]
        y_ref[...] = (g * _sigmoid(g) * a).astype(BF16)

    in_specs = [pl.BlockSpec((2, None, s, tc), lambda bi, ci: (0, bi, 0, ci)),
                pl.BlockSpec((2, FFN_WIDTH, tc), lambda bi, ci: (0, 0, ci)),
                pl.BlockSpec((2, 1, tc), lambda bi, ci: (0, 0, ci))]
    out_spec = pl.BlockSpec((None, s, tc), lambda bi, ci: (bi, 0, ci))
    return _call(body, name, (b, f // tc), in_specs, out_spec, _sds((b, s, f), BF16))(z2, dw_w2, dw_b2)


def ffn_gate_bwd(z2, dy, dw_w2, dw_b2, name, tc=2 * LANES):
    _, b, s, f = z2.shape

    def body(z_ref, dy_ref, w_ref, b_ref, dz_ref, dw_ref, db_ref):
        @pl.when(pl.program_id(1) == 0)
        def _():
            dw_ref[...] = jnp.zeros_like(dw_ref)
            db_ref[...] = jnp.zeros_like(db_ref)

        zs = (z_ref[0], z_ref[1])
        g = _conv_causal(zs[0], lambda k: w_ref[0, k:k + 1, :], FFN_WIDTH) + b_ref[0]
        a = _conv_causal(zs[1], lambda k: w_ref[1, k:k + 1, :], FFN_WIDTH) + b_ref[1]
        sg = _sigmoid(g)
        dy = dy_ref[...]
        dcs = (dy * a * (sg * (1.0 + g * (1.0 - sg))), dy * (g * sg))
        for p in range(2):
            dz, dws, db = _conv_causal_bwd(zs[p], dcs[p], lambda k, p=p: w_ref[p, k:k + 1, :], FFN_WIDTH)
            dz_ref[p] = dz.astype(BF16)
            for k in range(FFN_WIDTH):
                dw_ref[p, k:k + 1, :] += dws[k]
            db_ref[p] += db

    in_specs = [pl.BlockSpec((2, None, s, tc), lambda ci, bi: (0, bi, 0, ci)),
                pl.BlockSpec((None, s, tc), lambda ci, bi: (bi, 0, ci)),
                pl.BlockSpec((2, FFN_WIDTH, tc), lambda ci, bi: (0, 0, ci)),
                pl.BlockSpec((2, 1, tc), lambda ci, bi: (0, 0, ci))]
    out_specs = [pl.BlockSpec((2, None, s, tc), lambda ci, bi: (0, bi, 0, ci)),
                 pl.BlockSpec((2, FFN_WIDTH, tc), lambda ci, bi: (0, 0, ci)),
                 pl.BlockSpec((2, 1, tc), lambda ci, bi: (0, 0, ci))]
    out_shape = [_sds((2, b, s, f), BF16), _sds((2, FFN_WIDTH, f), F32), _sds((2, 1, f), F32)]
    return _call(body, name, (f // tc, b), in_specs, out_specs, out_shape)(z2, dy, dw_w2, dw_b2)


def cc_conv_fwd(z2, dw_w, dw_b, name, tc=LANES):
    _, b, s, c = z2.shape

    def body(z_ref, w_ref, b_ref, o_ref):
        u = z_ref[0] * _sigmoid(z_ref[1])
        o_ref[...] = _conv_causal(u, lambda k: w_ref[k:k + 1, :], CC_WIDTH) + b_ref[...]

    in_specs = [pl.BlockSpec((2, None, s, tc), lambda bi, ci: (0, bi, 0, ci)),
                pl.BlockSpec((CC_WIDTH, tc), lambda bi, ci: (0, ci)),
                pl.BlockSpec((1, tc), lambda bi, ci: (0, ci))]
    out_spec = pl.BlockSpec((None, s, tc), lambda bi, ci: (bi, 0, ci))
    return _call(body, name, (b, c // tc), in_specs, out_spec, _sds((b, s, c), F32))(z2, dw_w, dw_b.reshape(1, c))


def cc_conv_bwd(z2, dcv, dw_w, name, tc=LANES):
    _, b, s, c = z2.shape

    def body(z_ref, dc_ref, w_ref, dz_ref, dbi_ref, dw_ref, db_ref):
        @pl.when(pl.program_id(1) == 0)
        def _():
            dbi_ref[...] = jnp.zeros_like(dbi_ref)
            dw_ref[...] = jnp.zeros_like(dw_ref)
            db_ref[...] = jnp.zeros_like(db_ref)

        a, gate = z_ref[0], z_ref[1]
        sg = _sigmoid(gate)
        u = a * sg
        du, dws, db = _conv_causal_bwd(u, dc_ref[...], lambda k: w_ref[k:k + 1, :], CC_WIDTH)
        for k in range(CC_WIDTH):
            dw_ref[k:k + 1, :] += dws[k]
        db_ref[...] += db
        da = du * sg
        dg = du * a * sg * (1.0 - sg)
        dz_ref[0] = da.astype(BF16)
        dz_ref[1] = dg.astype(BF16)
        dbi_ref[0] += jnp.sum(da, axis=0, keepdims=True)
        dbi_ref[1] += jnp.sum(dg, axis=0, keepdims=True)

    in_specs = [pl.BlockSpec((2, None, s, tc), lambda ci, bi: (0, bi, 0, ci)),
                pl.BlockSpec((None, s, tc), lambda ci, bi: (bi, 0, ci)),
                pl.BlockSpec((CC_WIDTH, tc), lambda ci, bi: (0, ci))]
    out_specs = [pl.BlockSpec((2, None, s, tc), lambda ci, bi: (0, bi, 0, ci)),
                 pl.BlockSpec((2, 1, tc), lambda ci, bi: (0, 0, ci)),
                 pl.BlockSpec((CC_WIDTH, tc), lambda ci, bi: (0, ci)),
                 pl.BlockSpec((1, tc), lambda ci, bi: (0, ci))]
    out_shape = [_sds((2, b, s, c), BF16), _sds((2, 1, c), F32), _sds((CC_WIDTH, c), F32), _sds((1, c), F32)]
    return _call(body, name, (c // tc, b), in_specs, out_specs, out_shape)(z2, dcv, dw_w)


def _ln_stats(v):
    mu = jnp.mean(v, axis=-1, keepdims=True)
    vc = v - mu
    r = lax.rsqrt(jnp.mean(vc * vc, axis=-1, keepdims=True) + NORM_EPS)
    return vc * r, r


def _ln_bwd(dln, xh, r, g):
    dxh = dln * g
    dx = r * (dxh - jnp.mean(dxh, axis=-1, keepdims=True) - xh * jnp.mean(dxh * xh, axis=-1, keepdims=True))
    return dx, jnp.sum(dln * xh, axis=0, keepdims=True), jnp.sum(dln, axis=0, keepdims=True)


def cc_ln_fwd(cv, ln_g, ln_b, name):
    t, c = cv.shape
    tm = _tile(t, 256, SUBLANES)

    def body(x_ref, g_ref, b_ref, o_ref):
        xh, _ = _ln_stats(x_ref[...])
        ln = xh * g_ref[...] + b_ref[...]
        o_ref[...] = (ln * _sigmoid(ln)).astype(BF16)

    row = pl.BlockSpec((tm, c), lambda i: (i, 0))
    vec = pl.BlockSpec((1, c), lambda i: (0, 0))
    return _call(body, name, (t // tm,), [row, vec, vec], row, _sds((t, c), BF16))(
        cv, ln_g.reshape(1, c), ln_b.reshape(1, c))


def cc_ln_bwd(cv, ln_g, ln_b, ds, name):
    t, c = cv.shape
    tm = _tile(t, 256, SUBLANES)

    def body(x_ref, g_ref, b_ref, ds_ref, dx_ref, dg_ref, db_ref):
        @pl.when(pl.program_id(0) == 0)
        def _():
            dg_ref[...] = jnp.zeros_like(dg_ref)
            db_ref[...] = jnp.zeros_like(db_ref)

        xh, r = _ln_stats(x_ref[...])
        ln = xh * g_ref[...] + b_ref[...]
        sg = _sigmoid(ln)
        dln = ds_ref[...] * (sg * (1.0 + ln * (1.0 - sg)))
        dx, dg, db = _ln_bwd(dln, xh, r, g_ref[...])
        dx_ref[...] = dx
        dg_ref[...] += dg
        db_ref[...] += db

    row = pl.BlockSpec((tm, c), lambda i: (i, 0))
    vec = pl.BlockSpec((1, c), lambda i: (0, 0))
    return _call(body, name, (t // tm,), [row, vec, vec, row], [row, vec, vec],
                 [_sds((t, c), F32), _sds((1, c), F32), _sds((1, c), F32)])(
        cv, ln_g.reshape(1, c), ln_b.reshape(1, c), ds)


def col_sum(v, name):
    t, n = v.shape
    tm = _tile(t, 256, SUBLANES)

    def body(v_ref, o_ref):
        @pl.when(pl.program_id(0) == 0)
        def _():
            o_ref[...] = jnp.zeros_like(o_ref)

        o_ref[...] += jnp.sum(v_ref[...], axis=0, keepdims=True)

    return _call(body, name, (t // tm,), [pl.BlockSpec((tm, n), lambda i: (i, 0))],
                 pl.BlockSpec((1, n), lambda i: (0, 0)), _sds((1, n), F32))(v)


def rope_tables_full(seq, dim):
    pos = jnp.arange(seq, dtype=F32)
    inv = ROPE_THETA ** (-(jnp.arange(0, dim, 2, dtype=F32) / dim))
    ang = pos[:, None] * inv[None, :]
    cos, sin = jnp.cos(ang), jnp.sin(ang)
    return jnp.concatenate([cos, cos], axis=-1), jnp.concatenate([-sin, sin], axis=-1)


def dsa_rope_fwd(qkv, cos, sin, seq, name):
    t, w = qkv.shape
    tm = _tile(seq, 256, SUBLANES)
    nsb = seq // tm
    hb = DSA_HEADS

    def body(x_ref, c_ref, s_ref, o_ref):
        cos_t, sin_t = c_ref[...], s_ref[...]
        for blk in range(w // LANES):
            v = x_ref[:, blk * LANES:(blk + 1) * LANES]
            if (blk // hb) % 3 < 2:
                v = _rope(v, cos_t, sin_t)
            o_ref[:, blk * LANES:(blk + 1) * LANES] = v

    row = pl.BlockSpec((tm, w), lambda i: (i, 0))
    tab = pl.BlockSpec((tm, LANES), lambda i: (i % nsb, 0))
    return _call(body, name, (t // tm,), [row, tab, tab], row, _sds((t, w), F32))(qkv, cos, sin)


def _dsa_rows(r, n, dil):
    start = r + n * DSA_BLOCK * dil
    return pl.ds(start, DSA_BLOCK) if dil == 1 else pl.ds(start, DSA_BLOCK, stride=dil)


def _dsa_load(ref, r, blocks, dil):
    parts = [ref[_dsa_rows(r, m, dil), :] for m in blocks]
    return parts[0] if len(parts) == 1 else jnp.concatenate(parts, axis=0)


def dsa_attn_all_fwd(qkv_r, batch, seq, name):
    t, w = qkv_r.shape
    h_n, n_g = DSA_HEADS, len(DSA_CONFIGS)
    scale = DSA_BLOCK ** -0.5
    chunk = _tile(seq, 256, SUBLANES)

    def body(q_ref, k_ref, v_ref, o_ref, ob_ref, lse_ref, og_ref, lg_ref):
        g = pl.program_id(2)
        for gi, (_, dil) in enumerate(DSA_CONFIGS):
            @pl.when(g == gi)
            def _(gi=gi, dil=dil):
                for r in range(dil):
                    for n in range(seq // dil // DSA_BLOCK):
                        blocks = (n - 1, n) if n else (n,)
                        rows = _dsa_rows(r, n, dil)
                        q = q_ref[rows, :].astype(BF16)
                        k = _dsa_load(k_ref, r, blocks, dil).astype(BF16)
                        v = _dsa_load(v_ref, r, blocks, dil).astype(BF16)
                        s = jnp.where(_dsa_mask(n), _dot_nt(q, k) * scale, NEG_INF)
                        m = jnp.max(s, axis=-1, keepdims=True)
                        p = jnp.exp(s - m)
                        l = jnp.sum(p, axis=-1, keepdims=True)
                        og_ref[gi, rows, :] = _dot(p.astype(BF16), v) / l
                        lg_ref[gi, rows, :] = jnp.broadcast_to(m + jnp.log(l), (DSA_BLOCK, LANES))

        @pl.when(g == n_g - 1)
        def _():
            for c0 in range(0, seq, chunk):
                sl = slice(c0, c0 + chunk)
                ls = [lg_ref[gi, sl, :] for gi in range(n_g)]
                m = functools.reduce(jnp.maximum, ls)
                es = [jnp.exp(v - m) for v in ls]
                tot = functools.reduce(lambda a, b: a + b, es)
                acc = jnp.zeros_like(m)
                for gi in range(n_g):
                    acc = acc + (es[gi] / tot) * og_ref[gi, sl, :]
                o_ref[sl, :] = acc
                ob_ref[sl, :] = acc.astype(BF16)
                lse_ref[sl, :] = m + jnp.log(tot)

    def spec(off):
        return pl.BlockSpec((None, seq, LANES), lambda b, h, g: (b, 0, (3 * g + off) * h_n + h))

    out_spec = pl.BlockSpec((None, seq, LANES), lambda b, h, g: (b, 0, h))
    view = qkv_r.reshape(batch, seq, w)
    shape = (batch, seq, h_n * LANES)
    o, o_b, lse = _call(body, name, (batch, h_n, n_g), [spec(0), spec(1), spec(2)], [out_spec] * 3,
                        [_sds(shape, F32), _sds(shape, BF16), _sds(shape, F32)],
                        scratch=[pltpu.VMEM((n_g, seq, LANES), F32)] * 2)(view, view, view)
    return o.reshape(t, -1), o_b.reshape(t, -1), lse.reshape(t, -1)


def dsa_attn_all_bwd(qkv_r, do, lse, delta, batch, seq, name):
    t, w = qkv_r.shape
    h_n, n_g = DSA_HEADS, len(DSA_CONFIGS)
    scale = DSA_BLOCK ** -0.5

    def body(q_ref, k_ref, v_ref, do_ref, lse_ref, dl_ref, d_ref):
        g = pl.program_id(2)
        d_ref[...] = jnp.zeros_like(d_ref)
        for gi, (_, dil) in enumerate(DSA_CONFIGS):
            @pl.when(g == gi)
            def _(dil=dil):
                for r in range(dil):
                    for n in range(seq // dil // DSA_BLOCK):
                        blocks = (n - 1, n) if n else (n,)
                        rows = _dsa_rows(r, n, dil)
                        q = q_ref[rows, :].astype(BF16)
                        k = _dsa_load(k_ref, r, blocks, dil).astype(BF16)
                        v = _dsa_load(v_ref, r, blocks, dil).astype(BF16)
                        dout = do_ref[rows, :].astype(BF16)
                        s = jnp.where(_dsa_mask(n), _dot_nt(q, k) * scale, NEG_INF)
                        p = jnp.exp(s - lse_ref[rows, :][:, 0:1])
                        ds = (p * (_dot_nt(dout, v) - dl_ref[rows, :][:, 0:1]) * scale).astype(BF16)
                        d_ref[0, rows, :] = _dot(ds, k)
                        dk = _dot_tn(ds, q)
                        dv = _dot_tn(p.astype(BF16), dout)
                        for i, blk in enumerate(blocks):
                            brows = _dsa_rows(r, blk, dil)
                            d_ref[1, brows, :] += dk[i * DSA_BLOCK:(i + 1) * DSA_BLOCK]
                            d_ref[2, brows, :] += dv[i * DSA_BLOCK:(i + 1) * DSA_BLOCK]

    def spec(off):
        return pl.BlockSpec((None, seq, LANES), lambda b, h, g: (b, 0, (3 * g + off) * h_n + h))

    head = pl.BlockSpec((None, seq, LANES), lambda b, h, g: (b, 0, h))
    out_spec = pl.BlockSpec((3, None, seq, LANES), lambda b, h, g: (g, b, 0, h))
    view = qkv_r.reshape(batch, seq, w)
    hv = [a.reshape(batch, seq, h_n * LANES) for a in (do, lse, delta)]
    out = _call(body, name, (batch, h_n, n_g), [spec(0), spec(1), spec(2), head, head, head], out_spec,
                _sds((3 * n_g, batch, seq, h_n * LANES), F32))(view, view, view, *hv)
    return out.reshape(3 * n_g, t, h_n * LANES)


def dsa_rope_bwd_all(grads, cos, sin, seq, name):
    n_parts, t, hw = grads.shape
    w = hw * n_parts
    tm = _tile(seq, 128, SUBLANES)
    nsb = seq // tm

    def body(g_ref, c_ref, s_ref, o_ref):
        cos_t, sin_t = c_ref[...], s_ref[...]
        for idx in range(n_parts):
            for h in range(hw // LANES):
                v = g_ref[idx, :, h * LANES:(h + 1) * LANES]
                if idx % 3 < 2:
                    v = _rope_t(v, cos_t, sin_t)
                col = idx * hw + h * LANES
                o_ref[:, col:col + LANES] = v.astype(BF16)

    tab = pl.BlockSpec((tm, LANES), lambda i: (i % nsb, 0))
    out = _call(body, name, (t // tm,), [pl.BlockSpec((n_parts, tm, hw), lambda i: (0, i, 0)), tab, tab],
                pl.BlockSpec((tm, w), lambda i: (i, 0)), _sds((t, w), BF16))(grads, cos, sin)
    return out[None]


def _dsa_mask(n):
    qi = lax.broadcasted_iota(jnp.int32, (DSA_BLOCK, (1 if n == 0 else 2) * DSA_BLOCK), 0)
    kj = lax.broadcasted_iota(jnp.int32, (DSA_BLOCK, (1 if n == 0 else 2) * DSA_BLOCK), 1)
    if n == 0:
        return qi >= kj
    dist = DSA_BLOCK + qi - kj
    return (dist >= 0) & (dist <= DSA_BLOCK)


def head_delta(do, o, name):
    t, hw = do.shape
    tm = _tile(t, 256, SUBLANES)

    def body(do_ref, o_ref, d_ref):
        for h in range(hw // LANES):
            sl = slice(h * LANES, (h + 1) * LANES)
            d = jnp.sum(do_ref[:, sl] * o_ref[:, sl], axis=-1, keepdims=True)
            d_ref[:, sl] = jnp.broadcast_to(d, (tm, LANES))

    row = pl.BlockSpec((tm, hw), lambda i: (i, 0))
    return _call(body, name, (t // tm,), [row, row], row, _sds((t, hw), F32))(do, o)


def mla_rope_tables(seq):
    pos = jnp.arange(seq, dtype=F32)
    inv = ROPE_THETA ** (-(jnp.arange(0, MLA_ROPE, 2, dtype=F32) / MLA_ROPE))
    ang = pos[:, None] * inv[None, :]
    cos, sin, zero = jnp.cos(ang), jnp.sin(ang), jnp.zeros_like(ang)
    return jnp.concatenate([cos, zero, cos, zero], axis=-1), jnp.concatenate([-sin, zero, sin, zero], axis=-1)


def _spread_rope_cols(w_pe):
    half = MLA_ROPE // 2
    zero = jnp.zeros(w_pe.shape[:-1] + (half,), w_pe.dtype)
    return jnp.concatenate([w_pe[..., :half], zero, w_pe[..., half:], zero], axis=-1)


def _gather_rope_cols(g_pe):
    half = MLA_ROPE // 2
    return jnp.concatenate([g_pe[..., :half], g_pe[..., 2 * half:3 * half]], axis=-1)


def mla_low_fwd(c, q_norm, kv_norm, cos, sin, seq, name):
    t, w = c.shape
    rk = MLA_RANK
    tm = _tile(seq, 256, SUBLANES)
    nsb = seq // tm

    def body(c_ref, qg_ref, kg_ref, cs_ref, sn_ref, qn_ref, kn_ref, kp_ref):
        for lo, g_ref, o_ref in ((0, qg_ref, qn_ref), (rk, kg_ref, kn_ref)):
            v = c_ref[:, lo:lo + rk]
            r = lax.rsqrt(jnp.mean(v * v, axis=-1, keepdims=True) + NORM_EPS)
            o_ref[...] = (v * r * g_ref[...]).astype(BF16)
        kp_ref[...] = _rope(c_ref[:, 2 * rk:], cs_ref[...], sn_ref[...]).astype(BF16)

    row = lambda n: pl.BlockSpec((tm, n), lambda i: (i, 0))
    vec = pl.BlockSpec((1, rk), lambda i: (0, 0))
    tab = pl.BlockSpec((tm, LANES), lambda i: (i % nsb, 0))
    return _call(body, name, (t // tm,), [row(w), vec, vec, tab, tab], [row(rk), row(rk), row(LANES)],
                 [_sds((t, rk), BF16), _sds((t, rk), BF16), _sds((t, LANES), BF16)])(
        c, q_norm.reshape(1, rk), kv_norm.reshape(1, rk), cos, sin)


def mla_low_bwd(c, q_norm, kv_norm, dqn, dkn, dkp, cos, sin, seq, name):
    t, w = c.shape
    rk = MLA_RANK
    tm = _tile(seq, 256, SUBLANES)
    nsb = seq // tm

    def body(c_ref, qg_ref, kg_ref, dq_ref, dk_ref, dp_ref, cs_ref, sn_ref, dc_ref, dqg_ref, dkg_ref):
        @pl.when(pl.program_id(0) == 0)
        def _():
            dqg_ref[...] = jnp.zeros_like(dqg_ref)
            dkg_ref[...] = jnp.zeros_like(dkg_ref)

        for lo, g_ref, d_ref, dg_ref in ((0, qg_ref, dq_ref, dqg_ref), (rk, kg_ref, dk_ref, dkg_ref)):
            dx, dg = _rms_bwd_math(c_ref[:, lo:lo + rk], g_ref[...], d_ref[...])
            dc_ref[:, lo:lo + rk] = dx.astype(BF16)
            dg_ref[...] += dg
        dc_ref[:, 2 * rk:] = _rope_t(dp_ref[...], cs_ref[...], sn_ref[...]).astype(BF16)

    row = lambda n: pl.BlockSpec((tm, n), lambda i: (i, 0))
    vec = pl.BlockSpec((1, rk), lambda i: (0, 0))
    tab = pl.BlockSpec((tm, LANES), lambda i: (i % nsb, 0))
    dc, dqg, dkg = _call(body, name, (t // tm,), [row(w), vec, vec, row(rk), row(rk), row(LANES), tab, tab],
                         [row(w), vec, vec], [_sds((t, w), BF16), _sds((1, rk), F32), _sds((1, rk), F32)])(
        c, q_norm.reshape(1, rk), kv_norm.reshape(1, rk), dqn, dkn, dkp, cos, sin)
    return dc[None], dqg, dkg


def mla_rope_cast(q, kv, cos, sin, seq, name, transpose=False):
    t, w = q.shape
    tm = _tile(seq, 256, SUBLANES)
    nsb = seq // tm
    fn = _rope_t if transpose else _rope

    def body(q_ref, kv_ref, cs_ref, sn_ref, qo_ref, kvo_ref):
        cos_t, sin_t = cs_ref[...], sn_ref[...]
        for blk in range(w // LANES):
            sl = slice(blk * LANES, (blk + 1) * LANES)
            v = q_ref[:, sl]
            if blk % 2 == 1:
                v = fn(v, cos_t, sin_t)
            qo_ref[:, sl] = v.astype(BF16)
        kvo_ref[...] = kv_ref[...].astype(BF16)

    row = pl.BlockSpec((tm, w), lambda i: (i, 0))
    tab = pl.BlockSpec((tm, LANES), lambda i: (i % nsb, 0))
    return _call(body, name, (t // tm,), [row, row, tab, tab], [row, row], [_sds((t, w), BF16)] * 2)(q, kv, cos, sin)


def _causal_mask(i, j, tq, tk):
    qpos = i * tq + lax.broadcasted_iota(jnp.int32, (tq, tk), 0)
    kpos = j * tk + lax.broadcasted_iota(jnp.int32, (tq, tk), 1)
    return kpos <= qpos


def mla_attn_fwd(q_b, kv_b, kp_b, batch, seq, name, tq=512):
    t, w = q_b.shape
    h_n = w // (2 * LANES)
    tq = _tile(seq, tq, SUBLANES)
    nq = seq // tq
    scale = (MLA_NOPE + MLA_ROPE) ** -0.5

    def body(q_ref, kv_ref, kp_ref, o_ref, lse_ref):
        i = pl.program_id(2)
        qn, qp = q_ref[:, :LANES], q_ref[:, LANES:]

        def step(j, carry, on_diagonal):
            m, l, acc = carry
            rows = pl.ds(pl.multiple_of(j * tq, tq), tq)
            s = (_dot_nt(qn, kv_ref[rows, :LANES]) + _dot_nt(qp, kp_ref[rows, :])) * scale
            if on_diagonal:
                s = jnp.where(_causal_mask(i, j, tq, tq), s, NEG_INF)
            mn = jnp.maximum(m, jnp.max(s, axis=-1, keepdims=True))
            p = jnp.exp(s - mn)
            a = jnp.exp(m - mn)
            return mn, a * l + jnp.sum(p, axis=-1, keepdims=True), a * acc + _dot(p.astype(BF16), kv_ref[rows, LANES:])

        init = (jnp.full((tq, 1), NEG_INF, F32), jnp.zeros((tq, 1), F32), jnp.zeros((tq, LANES), F32))
        before = lax.fori_loop(0, i, lambda j, carry: step(j, carry, False), init)
        m, l, acc = step(i, before, True)
        o_ref[...] = acc / l
        lse_ref[...] = jnp.broadcast_to(m + jnp.log(l), (tq, LANES))

    in_specs = [pl.BlockSpec((None, tq, 2 * LANES), lambda b, h, i: (b, i, h)),
                pl.BlockSpec((None, seq, 2 * LANES), lambda b, h, i: (b, 0, h)),
                pl.BlockSpec((None, seq, LANES), lambda b, h, i: (b, 0, 0))]
    out_spec = pl.BlockSpec((None, tq, LANES), lambda b, h, i: (b, i, h))
    o, lse = _call(body, name, (batch, h_n, nq), in_specs, [out_spec, out_spec],
                   [_sds((batch, seq, h_n * LANES), F32)] * 2)(
        q_b.reshape(batch, seq, w), kv_b.reshape(batch, seq, w), kp_b.reshape(batch, seq, LANES))
    return o.reshape(t, h_n * LANES), lse.reshape(t, h_n * LANES)


def mla_attn_bwd(q_b, kv_b, kp_b, do, lse, delta, batch, seq, name, tq=512):
    t, w = q_b.shape
    h_n = w // (2 * LANES)
    tq = _tile(seq, tq, SUBLANES)
    nq = seq // tq
    scale = (MLA_NOPE + MLA_ROPE) ** -0.5

    def body(q_ref, kv_ref, kp_ref, do_ref, lse_ref, dl_ref, dq_ref, dkv_ref, dkp_ref):
        dq_ref[...] = jnp.zeros_like(dq_ref)
        dkv_ref[...] = jnp.zeros_like(dkv_ref)

        @pl.when(pl.program_id(1) == 0)
        def _():
            dkp_ref[...] = jnp.zeros_like(dkp_ref)

        def outer(j, _):
            krows = pl.ds(pl.multiple_of(j * tq, tq), tq)
            kn, v, kp = kv_ref[krows, :LANES], kv_ref[krows, LANES:], kp_ref[krows, :]

            def pair(i, on_diagonal):
                qrows = pl.ds(pl.multiple_of(i * tq, tq), tq)
                qn, qp = q_ref[qrows, :LANES], q_ref[qrows, LANES:]
                dout = do_ref[qrows, :].astype(BF16)
                s = (_dot_nt(qn, kn) + _dot_nt(qp, kp)) * scale
                if on_diagonal:
                    s = jnp.where(_causal_mask(i, j, tq, tq), s, NEG_INF)
                p = jnp.exp(s - lse_ref[qrows, 0:1])
                ds = (p * (_dot_nt(dout, v) - dl_ref[qrows, 0:1]) * scale).astype(BF16)
                dq_ref[qrows, :LANES] += _dot(ds, kn)
                dq_ref[qrows, LANES:] += _dot(ds, kp)
                dkv_ref[krows, :LANES] += _dot_tn(ds, qn)
                dkv_ref[krows, LANES:] += _dot_tn(p.astype(BF16), dout)
                dkp_ref[krows, :] += _dot_tn(ds, qp)
                return 0

            pair(j, True)
            lax.fori_loop(j + 1, nq, lambda i, _: pair(i, False), 0)
            return 0

        lax.fori_loop(0, nq, outer, 0)

    wide = pl.BlockSpec((None, seq, 2 * LANES), lambda b, h: (b, 0, h))
    head = pl.BlockSpec((None, seq, LANES), lambda b, h: (b, 0, h))
    shared = pl.BlockSpec((None, seq, LANES), lambda b, h: (b, 0, 0))
    hv = [a.reshape(batch, seq, h_n * LANES) for a in (do, lse, delta)]
    dq, dkv, dkp = _call(body, name, (batch, h_n), [wide, wide, shared, head, head, head], [wide, wide, shared],
                         [_sds((batch, seq, w), F32), _sds((batch, seq, w), F32), _sds((batch, seq, LANES), F32)])(
        q_b.reshape(batch, seq, w), kv_b.reshape(batch, seq, w), kp_b.reshape(batch, seq, LANES), *hv)
    return dq.reshape(t, w), dkv.reshape(t, w), dkp.reshape(t, LANES)


def _sgu_common(z_ref, g_ref, b_ref, ws_ref, bs_ref):
    e = z_ref.shape[-1]
    ge = e // SG_GROUPS
    zv = z_ref[1]
    cv = _gelu_cdf(zv)
    xh, r = _ln_stats(zv * cv)
    vn = (xh * g_ref[...] + b_ref[...]).astype(BF16)
    tri = (lax.broadcasted_iota(jnp.int32, (SG_CHUNK, SG_CHUNK), 0)
           >= lax.broadcasted_iota(jnp.int32, (SG_CHUNK, SG_CHUNK), 1))
    ws = [jnp.where(tri, ws_ref[g], 0.0).astype(BF16) for g in range(SG_GROUPS)]
    v2 = [_dot(ws[g], vn[:, g * ge:(g + 1) * ge]) + bs_ref[:, g:g + 1] for g in range(SG_GROUPS)]
    return zv, cv, xh, r, vn, tri, ws, v2


def sgu_fwd(z2, ln_g, ln_b, w_s, b_s_t, name):
    _, t, e = z2.shape
    ge = e // SG_GROUPS

    def body(z_ref, g_ref, b_ref, ws_ref, bs_ref, y_ref):
        v2 = _sgu_common(z_ref, g_ref, b_ref, ws_ref, bs_ref)[-1]
        for g in range(SG_GROUPS):
            zu = z_ref[0, :, g * ge:(g + 1) * ge]
            y_ref[:, g * ge:(g + 1) * ge] = (zu * _gelu_cdf(zu) * v2[g]).astype(BF16)

    vec = pl.BlockSpec((1, e), lambda i: (0, 0))
    in_specs = [pl.BlockSpec((2, SG_CHUNK, e), lambda i: (0, i, 0)), vec, vec,
                pl.BlockSpec((SG_GROUPS, SG_CHUNK, SG_CHUNK), lambda i: (0, 0, 0)),
                pl.BlockSpec((SG_CHUNK, SG_GROUPS), lambda i: (0, 0))]
    return _call(body, name, (t // SG_CHUNK,), in_specs, pl.BlockSpec((SG_CHUNK, e), lambda i: (i, 0)),
                 _sds((t, e), BF16))(z2, ln_g.reshape(1, e), ln_b.reshape(1, e), w_s, b_s_t)


def sgu_bwd(z2, dy, ln_g, ln_b, w_s, b_s_t, name):
    _, t, e = z2.shape
    ge = e // SG_GROUPS

    def body(z_ref, dy_ref, g_ref, b_ref, ws_ref, bs_ref, dz_ref, dbi_ref, dg_ref, db_ref, dws_ref, dbs_ref):
        @pl.when(pl.program_id(0) == 0)
        def _():
            for ref in (dbi_ref, dg_ref, db_ref, dws_ref, dbs_ref):
                ref[...] = jnp.zeros_like(ref)

        zv, cv, xh, r, vn, tri, ws, v2 = _sgu_common(z_ref, g_ref, b_ref, ws_ref, bs_ref)
        dvn = []
        for g in range(SG_GROUPS):
            sl = slice(g * ge, (g + 1) * ge)
            dyg = dy_ref[:, sl]
            zu = z_ref[0, :, sl]
            cu = _gelu_cdf(zu)
            du = dyg * v2[g] * _gelu_grad(zu, cu)
            dz_ref[0, :, sl] = du.astype(BF16)
            dbi_ref[0, :, sl] += jnp.sum(du, axis=0, keepdims=True)
            dv2 = dyg * (zu * cu)
            dbs_ref[:, g:g + 1] += jnp.sum(dv2, axis=-1, keepdims=True)
            dv2b = dv2.astype(BF16)
            dws_ref[g] += jnp.where(tri, _dot_nt(dv2b, vn[:, sl]), 0.0)
            dvn.append(_dot_tn(ws[g], dv2b))
        dvn = jnp.concatenate(dvn, axis=-1)
        dv, dg, db = _ln_bwd(dvn, xh, r, g_ref[...])
        dzv = dv * _gelu_grad(zv, cv)
        dz_ref[1] = dzv.astype(BF16)
        dbi_ref[1] += jnp.sum(dzv, axis=0, keepdims=True)
        dg_ref[...] += dg
        db_ref[...] += db

    vec = pl.BlockSpec((1, e), lambda i: (0, 0))
    wsb = pl.BlockSpec((SG_GROUPS, SG_CHUNK, SG_CHUNK), lambda i: (0, 0, 0))
    bsb = pl.BlockSpec((SG_CHUNK, SG_GROUPS), lambda i: (0, 0))
    zblk = pl.BlockSpec((2, SG_CHUNK, e), lambda i: (0, i, 0))
    in_specs = [zblk, pl.BlockSpec((SG_CHUNK, e), lambda i: (i, 0)), vec, vec, wsb, bsb]
    out_specs = [zblk, pl.BlockSpec((2, 1, e), lambda i: (0, 0, 0)), vec, vec, wsb, bsb]
    out_shape = [_sds((2, t, e), BF16), _sds((2, 1, e), F32), _sds((1, e), F32), _sds((1, e), F32),
                 _sds((SG_GROUPS, SG_CHUNK, SG_CHUNK), F32), _sds((SG_CHUNK, SG_GROUPS), F32)]
    return _call(body, name, (t // SG_CHUNK,), in_specs, out_specs, out_shape)(
        z2, dy, ln_g.reshape(1, e), ln_b.reshape(1, e), w_s, b_s_t)


_COL = ("cc_w_in", "ffn_w_up", "dsa_w_qkv", "dsa_w_o", "mla_w_qb", "mla_w_kvb", "sg_w_in")
_ROW = ("cc_w_out", "ffn_w_down", "mla_w_in", "mla_w_o", "sg_w_out")
_SMALL_COL = ("cc_dw_w", "ffn_dw_w")
_RELAID = ("cc_w_in", "dsa_w_o", "mla_w_kvb")
_MIXERS = ("cc", "dsa", "mla", "sg")


def _base(name):
    return name.split("_", 1)[1] if name[0] == "l" and name[1].isdigit() else name


def _ffn_fwd(pre, p, full, h_prev, mix_out, bsz, seq):
    t, d = h_prev.shape
    h, hn = rms_fwd(h_prev, p[pre + "norm_ffn"], pre + "ffn_norm", add=mix_out)
    w_up, w_down = full[pre + "ffn_w_up"], full[pre + "ffn_w_down"]
    f = w_down.shape[1]
    z2r = mm_nn(hn[None], w_up, pre + "ffn_up", out_parts=2).reshape(2, bsz, seq, f)
    dw_w2 = jnp.transpose(full[pre + "ffn_dw_w"].reshape(FFN_WIDTH, 2, f), (1, 0, 2))
    dw_b2 = p[pre + "ffn_dw_b"].reshape(2, 1, f)
    y = ffn_gate_fwd(z2r, dw_w2, dw_b2, pre + "ffn_gate").reshape(1, t, f)
    out = mm_nn(y, w_down, pre + "ffn_down")[0]
    return out, dict(h=h, hn=hn, z2r=z2r, y=y, dw_w2=dw_w2, dw_b2=dw_b2)


def relayout_cols(g, name, to_shards):
    k = g.shape[1]
    n = g.shape[2] // N_DEV if to_shards else g.shape[2]
    tk = _row_tile(k, n, itemsize=g.dtype.itemsize, budget=1 << 21)

    def body(i_ref, o_ref):
        o_ref[...] = i_ref[...]

    shard = pl.BlockSpec((None, tk, n), lambda j, ki: (j, ki, 0))
    whole = pl.BlockSpec((None, tk, n), lambda j, ki: (0, ki, j))
    if to_shards:
        return _call(body, name, (N_DEV, k // tk), [whole], shard, _sds((N_DEV, k, n), g.dtype))(g)
    return _call(body, name, (N_DEV, k // tk), [shard], whole, _sds((1, k, N_DEV * n), g.dtype))(g)


def _col_grad(a, b, w, name):
    g = mm_tn(a, b, w.shape[0], name)
    return g if w.shape[0] == N_DEV else relayout_cols(g, name + "_shards", True)


def _ffn_bwd(pre, p, full, ctx, dh_out, big_g, small_g, bsz, seq, mid):
    w_up, w_down = full[pre + "ffn_w_up"], full[pre + "ffn_w_down"]
    f = w_down.shape[1]
    dh_out, dh_b = dh_out
    t, d = dh_out.shape
    dh3 = dh_b[None]
    dyv = mm_nt(dh3, w_down, pre + "ffn_down_dx")
    big_g[pre + "ffn_w_down"] = mm_tn(ctx["y"], dh3, 1, pre + "ffn_down_dw").reshape(N_DEV, f // N_DEV, d)
    mid(dyv)
    dz2, ddw2, ddb2 = ffn_gate_bwd(ctx["z2r"], dyv.reshape(bsz, seq, f), ctx["dw_w2"], ctx["dw_b2"],
                                   pre + "ffn_gate_bwd")
    dz2 = dz2.reshape(2, t, f)
    big_g[pre + "ffn_w_up"] = mm_tn(ctx["hn"][None], dz2, N_DEV, pre + "ffn_up_dw")
    dhn = mm_nt(dz2, w_up, pre + "ffn_up_dx")
    small_g[pre + "ffn_dw_w"] = jnp.transpose(ddw2, (1, 0, 2)).reshape(FFN_WIDTH, 2 * f)
    small_g[pre + "ffn_dw_b"] = ddb2.reshape(2 * f)
    dh, dh_b, dg = rms_bwd(ctx["h"], p[pre + "norm_ffn"], dhn, dh_out, pre + "ffn_norm_bwd")
    small_g[pre + "norm_ffn"] = dg.reshape(d)
    return dh, dh_b


def _cc_fwd(pre, p, full, hn, bsz, seq, aux):
    t, c = hn.shape
    z2r = mm_nn(hn[None], full[pre + "cc_w_in"], pre + "cc_in", bias=p[pre + "cc_b_in"], out_parts=2).reshape(
        2, bsz, seq, c)
    cv = cc_conv_fwd(z2r, full[pre + "cc_dw_w"], p[pre + "cc_dw_b"], pre + "cc_conv").reshape(t, c)
    s = cc_ln_fwd(cv, p[pre + "cc_ln_g"], p[pre + "cc_ln_b"], pre + "cc_ln")
    out = mm_nn(s[None], full[pre + "cc_w_out"], pre + "cc_out", bias=p[pre + "cc_b_out"])[0]
    return out, dict(z2r=z2r, cv=cv, s=s)


def _cc_bwd(pre, p, full, ctx, hn, dm, big_g, small_g, bsz, seq, aux, mid):
    dm, dm_b = dm
    t, c = dm.shape
    dm3 = dm_b[None]
    small_g[pre + "cc_b_out"] = col_sum(dm, pre + "cc_bout_g").reshape(c)
    ds = mm_nt(dm3, full[pre + "cc_w_out"], pre + "cc_out_dx")
    big_g[pre + "cc_w_out"] = mm_tn(ctx["s"][None], dm3, 1, pre + "cc_out_dw").reshape(N_DEV, c // N_DEV, c)
    mid(ds)
    dcv, dlg, dlb = cc_ln_bwd(ctx["cv"], p[pre + "cc_ln_g"], p[pre + "cc_ln_b"], ds, pre + "cc_ln_bwd")
    dz2, dbi, ddw, ddb = cc_conv_bwd(ctx["z2r"], dcv.reshape(bsz, seq, c), full[pre + "cc_dw_w"], pre + "cc_conv_bwd")
    dz2 = dz2.reshape(2, t, c)
    big_g[pre + "cc_w_in"] = _col_grad(hn[None], dz2, full[pre + "cc_w_in"], pre + "cc_in_dw")
    small_g[pre + "cc_ln_g"], small_g[pre + "cc_ln_b"] = dlg.reshape(c), dlb.reshape(c)
    small_g[pre + "cc_b_in"], small_g[pre + "cc_dw_w"], small_g[pre + "cc_dw_b"] = dbi.reshape(2 * c), ddw, ddb.reshape(c)
    return mm_nt(dz2, full[pre + "cc_w_in"], pre + "cc_in_dx")


def _dsa_fwd(pre, p, full, hn, bsz, seq, aux):
    cos, sin = aux["dsa_tables"]
    qkv = mm_nn(hn[None], full[pre + "dsa_w_qkv"], pre + "dsa_qkv")[0]
    qkv_r = dsa_rope_fwd(qkv, cos, sin, seq, pre + "dsa_rope")
    for window, dil in DSA_CONFIGS:
        assert window // dil == DSA_BLOCK and (seq // dil) % DSA_BLOCK == 0
    o, o_b, lse = dsa_attn_all_fwd(qkv_r, bsz, seq, pre + "dsa_attn")
    out = mm_nn(o_b[None], full[pre + "dsa_w_o"], pre + "dsa_o")[0]
    return out, dict(qkv_r=qkv_r, o=o, o_b=o_b, lse=lse)


def _dsa_bwd(pre, p, full, ctx, hn, dm, big_g, small_g, bsz, seq, aux, mid):
    cos, sin = aux["dsa_tables"]
    dm, dm_b = dm
    dm3 = dm_b[None]
    do = mm_nt(dm3, full[pre + "dsa_w_o"], pre + "dsa_o_dx")
    big_g[pre + "dsa_w_o"] = _col_grad(ctx["o_b"][None], dm3, full[pre + "dsa_w_o"], pre + "dsa_o_dw")
    mid(do)
    delta = head_delta(do, ctx["o"], pre + "dsa_delta")
    grads = dsa_attn_all_bwd(ctx["qkv_r"], do, ctx["lse"], delta, bsz, seq, pre + "dsa_attn_bwd")
    dqkv = dsa_rope_bwd_all(grads, cos, sin, seq, pre + "dsa_rope_bwd")
    big_g[pre + "dsa_w_qkv"] = mm_tn(hn[None], dqkv, N_DEV, pre + "dsa_qkv_dw")
    return mm_nt(dqkv, full[pre + "dsa_w_qkv"], pre + "dsa_qkv_dx")


def _mla_weights(pre, full):
    rk, hd = MLA_RANK, MLA_NOPE + MLA_ROPE
    w_in = full[pre + "mla_w_in"][0]
    w_in_p = jnp.concatenate([w_in[:, :2 * rk], _spread_rope_cols(w_in[:, 2 * rk:])], axis=1)[None]
    w_qb = jnp.transpose(full[pre + "mla_w_qb"], (1, 0, 2)).reshape(rk, MLA_HEADS, hd)
    w_qb_p = jnp.concatenate([w_qb[..., :MLA_NOPE], _spread_rope_cols(w_qb[..., MLA_NOPE:])], axis=-1)
    return w_in_p, w_qb_p.reshape(1, rk, MLA_HEADS * 2 * LANES)


def _mla_fwd(pre, p, full, hn, bsz, seq, aux):
    cos, sin = aux["mla_tables"]
    w_in_p, w_qb_p = _mla_weights(pre, full)
    c = mm_nn(hn[None], w_in_p, pre + "mla_in")[0]
    qn, kn, kp = mla_low_fwd(c, p[pre + "mla_q_norm"], p[pre + "mla_kv_norm"], cos, sin, seq, pre + "mla_low")
    q = mm_nn(qn[None], w_qb_p, pre + "mla_qb")[0]
    kv = mm_nn(kn[None], full[pre + "mla_w_kvb"], pre + "mla_kvb")[0]
    q_b, kv_b = mla_rope_cast(q, kv, cos, sin, seq, pre + "mla_rope")
    o, lse = mla_attn_fwd(q_b, kv_b, kp, bsz, seq, pre + "mla_attn")
    o_b = o.astype(BF16)
    out = mm_nn(o_b[None], full[pre + "mla_w_o"], pre + "mla_o")[0]
    return out, dict(c=c, qn=qn, kn=kn, kp=kp, q_b=q_b, kv_b=kv_b, o=o, o_b=o_b, lse=lse, w_in_p=w_in_p, w_qb_p=w_qb_p)


def _mla_bwd(pre, p, full, ctx, hn, dm, big_g, small_g, bsz, seq, aux, mid):
    cos, sin = aux["mla_tables"]
    dm, dm_b = dm
    t, d = dm.shape
    rk = MLA_RANK
    dm3 = dm_b[None]
    do = mm_nt(dm3, full[pre + "mla_w_o"], pre + "mla_o_dx")
    big_g[pre + "mla_w_o"] = mm_tn(ctx["o_b"][None], dm3, 1, pre + "mla_o_dw").reshape(N_DEV, -1, d)
    mid(do)
    delta = head_delta(do, ctx["o"], pre + "mla_delta")
    dq, dkv, dkp = mla_attn_bwd(ctx["q_b"], ctx["kv_b"], ctx["kp"], do, ctx["lse"], delta, bsz, seq, pre + "mla_attn_bwd")
    dq_b, dkv_b = mla_rope_cast(dq, dkv, cos, sin, seq, pre + "mla_rope_bwd", transpose=True)
    g_qb = mm_tn(ctx["qn"][None], dq_b[None], 1, pre + "mla_qb_dw")[0].reshape(rk, MLA_HEADS, 2 * LANES)
    g_qb = jnp.concatenate([g_qb[..., :MLA_NOPE], _gather_rope_cols(g_qb[..., MLA_NOPE:])], axis=-1)
    big_g[pre + "mla_w_qb"] = jnp.transpose(g_qb.reshape(rk, N_DEV, -1), (1, 0, 2))
    dqn = mm_nt(dq_b[None], ctx["w_qb_p"], pre + "mla_qb_dx")
    big_g[pre + "mla_w_kvb"] = _col_grad(ctx["kn"][None], dkv_b[None], full[pre + "mla_w_kvb"], pre + "mla_kvb_dw")
    dkn = mm_nt(dkv_b[None], full[pre + "mla_w_kvb"], pre + "mla_kvb_dx")
    dc, dqg, dkg = mla_low_bwd(ctx["c"], p[pre + "mla_q_norm"], p[pre + "mla_kv_norm"], dqn, dkn, dkp, cos, sin, seq,
                               pre + "mla_low_bwd")
    small_g[pre + "mla_q_norm"], small_g[pre + "mla_kv_norm"] = dqg.reshape(rk), dkg.reshape(rk)
    g_in = mm_tn(hn[None], dc, 1, pre + "mla_in_dw")[0]
    g_in = jnp.concatenate([g_in[:, :2 * rk], _gather_rope_cols(g_in[:, 2 * rk:])], axis=1)
    big_g[pre + "mla_w_in"] = g_in.reshape(N_DEV, d // N_DEV, -1)
    return mm_nt(dc, ctx["w_in_p"], pre + "mla_in_dx")


def _sg_fwd(pre, p, full, hn, bsz, seq, aux):
    z2 = mm_nn(hn[None], full[pre + "sg_w_in"], pre + "sg_in", bias=p[pre + "sg_b_in"], out_parts=2)
    b_s_t = jnp.transpose(p[pre + "sg_b_s"])
    y = sgu_fwd(z2, p[pre + "sg_ln_g"], p[pre + "sg_ln_b"], p[pre + "sg_w_s"], b_s_t, pre + "sg_mix")
    out = mm_nn(y[None], full[pre + "sg_w_out"], pre + "sg_out", bias=p[pre + "sg_b_out"])[0]
    return out, dict(z2=z2, y=y, b_s_t=b_s_t)


def _sg_bwd(pre, p, full, ctx, hn, dm, big_g, small_g, bsz, seq, aux, mid):
    dm, dm_b = dm
    t, d = dm.shape
    dm3 = dm_b[None]
    small_g[pre + "sg_b_out"] = col_sum(dm, pre + "sg_bout_g").reshape(d)
    dy = mm_nt(dm3, full[pre + "sg_w_out"], pre + "sg_out_dx")
    big_g[pre + "sg_w_out"] = mm_tn(ctx["y"][None], dm3, 1, pre + "sg_out_dw").reshape(N_DEV, -1, d)
    mid(dy)
    dz2, dbi, dlg, dlb, dws, dbs_t = sgu_bwd(ctx["z2"], dy, p[pre + "sg_ln_g"], p[pre + "sg_ln_b"], p[pre + "sg_w_s"],
                                             ctx["b_s_t"], pre + "sg_mix_bwd")
    big_g[pre + "sg_w_in"] = mm_tn(hn[None], dz2, N_DEV, pre + "sg_in_dw")
    small_g[pre + "sg_b_in"] = dbi.reshape(-1)
    small_g[pre + "sg_ln_g"], small_g[pre + "sg_ln_b"] = dlg.reshape(-1), dlb.reshape(-1)
    small_g[pre + "sg_w_s"], small_g[pre + "sg_b_s"] = dws, jnp.transpose(dbs_t)
    return mm_nt(dz2, full[pre + "sg_w_in"], pre + "sg_in_dx")


_MIX_FWD = dict(cc=_cc_fwd, dsa=_dsa_fwd, mla=_mla_fwd, sg=_sg_fwd)
_MIX_BWD = dict(cc=_cc_bwd, dsa=_dsa_bwd, mla=_mla_bwd, sg=_sg_bwd)


def _reduce_scatter(parts, core, chip, tag, adams=None, after=()):
    recv = rs_core_exchange(parts, tag + "_cores", after)
    sums = [rs_core_add(pt, rc, core, "%s_add%d" % (tag, i)) for i, (pt, rc) in enumerate(zip(parts, recv))]
    recv2 = rs_chip_exchange(sums, tag + "_chips")
    return [rs_finish(s, r2, chip, "%s_fin%d" % (tag, i), adam=None if adams is None else adams[i])
            for i, (s, r2) in enumerate(zip(sums, recv2))]


class _Schedule:
    def __init__(self):
        self.actions = {}

    def at(self, key, fn):
        self.actions.setdefault(key, []).append(fn)

    def run(self, key, x):
        after = list(x) if isinstance(x, (list, tuple)) else [x]
        for fn in self.actions.pop(key, []):
            fn(after)


def _aside(fn):
    saved = list(_PENDING)
    _PENDING.clear()
    out = fn()
    _PENDING[:0] = saved
    return out


def cast_bf16(w, after, name):
    r, c = w.shape
    tr = _row_tile(r, c, budget=1 << 21)
    na = len(after)

    def body(*refs):
        refs[1 + na][...] = refs[0][...].astype(BF16)

    blk = pl.BlockSpec((tr, c), lambda i: (i, 0))
    return pl.pallas_call(body, name=name, grid=(r // tr,), in_specs=[blk] + [_ANY] * na, out_specs=blk,
                          out_shape=_sds((r, c), BF16),
                          compiler_params=pltpu.CompilerParams(vmem_limit_bytes=VMEM_LIMIT))(w, *after)


def _gather_overlapped(sched, keys, shards_fn, n, dev, tag, deliver):
    state = {}

    def start(x):
        x = list(x) + list(_PENDING)
        shards = shards_fn(x)
        lands = [lax.dynamic_update_slice(lax.empty((N_DEV,) + s.shape, s.dtype), s[None], (dev,) + (0,) * s.ndim)
                 for s in shards]
        state["first"] = exchange_start(_ag_plan_first, 4 * n, shards, lands, x, tag +"_start")
        _PENDING.append(state["first"][4])

    def pass_on(x):
        _, lands = exchange_wait(_ag_plan_first, state["first"], x, tag +"_wait")
        state["pass"] = exchange_start(_ag_plan_pass, 3 * n, [], lands, [], tag + "_pass")
        _PENDING.append(state["pass"][4])

    def done(x):
        deliver(exchange_wait(_ag_plan_pass, state["pass"], x, tag +"_done")[1])

    for key, fn in zip(keys, (start, pass_on, done)):
        sched.at(key, fn)


def _reduce_overlapped(sched, keys, parts, adams, core, chip, tag, deliver):
    n = len(parts)
    state = {}

    def start(x):
        lands = [lax.empty((4,) + pt.shape[1:], pt.dtype) for pt in parts]
        state["cores"] = exchange_start(_rs_plan_cores, 4 * n, parts, lands, x, tag +"_start")
        _PENDING.append(state["cores"][4])

    def middle(x):
        mine, recv = exchange_wait(_rs_plan_cores, state["cores"], x, tag +"_wait")
        sums = _aside(lambda: [rs_core_add(pt, rc, core, "%s_add%d" % (tag, i))
                               for i, (pt, rc) in enumerate(zip(mine, recv))])
        lands = [lax.empty((3,) + s.shape[1:], s.dtype) for s in sums]
        state["chips"] = exchange_start(_rs_plan_chips, 3 * n, sums, lands, [], tag + "_send")
        _PENDING.append(state["chips"][4])

    def done(x):
        sums, recv = exchange_wait(_rs_plan_chips, state["chips"], x, tag +"_done")
        deliver(_aside(lambda: [rs_finish(s, r2, chip, "%s_fin%d" % (tag, i), adam=adams[i])
                                for i, (s, r2) in enumerate(zip(sums, recv))]))

    for key, fn in zip(keys, (start, middle, done)):
        sched.at(key, fn)


def _pack(arrays):
    flat = jnp.concatenate([a.reshape(-1) for a in arrays])
    unit = N_DEV * SUBLANES * LANES
    padded = -(-flat.shape[0] // unit) * unit
    return jnp.pad(flat, (0, padded - flat.shape[0]))


def _unpack(flat, like):
    out, pos = [], 0
    for a in like:
        out.append(flat[pos:pos + a.size].reshape(a.shape))
        pos += a.size
    return out


def _train_step(p):
    names = list(p)
    wnames = names[1:names.index("loss_target")]
    x = p["x"]
    bsz, seq, d = x.shape
    t = bsz * seq
    xi, yi, ci = lax.axis_index("x"), lax.axis_index("y"), lax.axis_index("c")
    core = jnp.reshape(ci, (1,)).astype(jnp.int32)
    chip = jnp.reshape(2 * xi + yi, (1,)).astype(jnp.int32)
    dev = 4 * xi + 2 * yi + ci
    n_layers = 1 + max(int(n[1]) for n in wnames if n[0] == "l" and n[1].isdigit())
    big = [n for n in wnames if _base(n) in _COL + _ROW]
    aux = dict(dsa_tables=rope_tables_full(seq, LANES), mla_tables=mla_rope_tables(seq))

    sched = _Schedule()
    full = {}

    def deliver_weights(grp):
        def deliver(lands):
            for n, g in zip(grp, lands):
                if _base(n) in _RELAID:
                    full[n] = _aside(lambda: relayout_cols(g, n + "_whole", False))
                elif _base(n) in _COL:
                    full[n] = g
                elif _base(n) in _ROW:
                    full[n] = g.reshape(1, N_DEV * g.shape[1], g.shape[2])
                else:
                    full[n] = jnp.transpose(g, (1, 0, 2)).reshape(g.shape[1], N_DEV * g.shape[2])
        return deliver

    groups = []
    for layer in range(n_layers):
        lw = [n for n in wnames if n.startswith("l%d_" % layer) and _base(n) in _COL + _ROW + _SMALL_COL]
        groups += [[n for n in lw if "_ffn_" not in n], [n for n in lw if "_ffn_" in n]]

    def shards_of(grp):
        def shards(after):
            if not after:
                return [p[n] if _base(n) in _SMALL_COL else p[n].astype(BF16) for n in grp]
            return [p[n] if _base(n) in _SMALL_COL else cast_bf16(p[n], after, n + "_bf16") for n in grp]
        return shards

    for s in range(len(groups)):
        if s == 0:
            keys = [("f", -1), ("first", 0), ("first", 0)]
        elif s == 1:
            keys = [("f", -1), ("f", 1), ("f", 1)]
        else:
            keys = [("f", s - 2 if s > 2 else s - 1), ("f", s - 1 if s > 2 else s), ("f", s)]
        _gather_overlapped(sched, keys, shards_of(groups[s]), len(groups[s]), dev, "ag_s%d" % s,
                           deliver_weights(groups[s]))
    sched.run(("f", -1), [])

    h = x.reshape(t, d)
    ffn_out = None
    ctxs = []
    for layer in range(n_layers):
        pre = "l%d_" % layer
        kind = _MIXERS[layer % len(_MIXERS)]
        if ffn_out is None:
            hn = rms_fwd(h, p[pre + "norm_mix"], pre + "mix_norm")
        else:
            h, hn = rms_fwd(h, p[pre + "norm_mix"], pre + "mix_norm", add=ffn_out)
        if layer == 0:
            sched.run(("first", 0), hn)
        sched.run(("f", 2 * layer), [hn] + (list(full.values()) if layer == 0 else []))
        mix_out, mctx = _MIX_FWD[kind](pre, p, full, hn, bsz, seq, aux)
        sched.run(("f", 2 * layer + 1), mix_out)
        ffn_out, fctx = _ffn_fwd(pre, p, full, h, mix_out, bsz, seq)
        ctxs.append((h, hn, mctx, fctx))
        h = fctx["h"]

    loss_row, dh, dh_b, dfin = final_loss(h, ffn_out, p["final_norm"], p["loss_target"].reshape(t, d), "final_loss")
    dh = (dh, dh_b)
    loss = lax.psum(loss_row[0, 0], ("x", "y", "c"))

    small_g = {"final_norm": dfin.reshape(d)}
    results = {}

    def deliver_grads(lnames):
        def deliver(outs):
            for n, o in zip(lnames, outs):
                results[n] = o
        return deliver

    def reduce_later(q, lnames, big_g, tag):
        keys = [("b", q + 2), ("b", q + 3), ("b", q + 5)]
        _reduce_overlapped(sched, keys, [big_g[n] for n in lnames], [(p[n], p["m_" + n], p["v_" + n]) for n in lnames],
                           core, chip, tag, deliver_grads(lnames))

    q = 0
    for layer in reversed(range(n_layers)):
        pre = "l%d_" % layer
        kind = _MIXERS[layer % len(_MIXERS)]
        h_in, hn, mctx, fctx = ctxs[layer]
        big_g = {}
        sched.run(("b", q), dh[0])
        dh = _ffn_bwd(pre, p, full, fctx, dh, big_g, small_g, bsz, seq, lambda v, q=q: sched.run(("b", q + 1), v))
        reduce_later(q, [n for n in big if n.startswith(pre) and "_ffn_" in n], big_g, "rs_l%d_ffn" % layer)
        q += 2
        sched.run(("b", q), dh[0])
        dhn = _MIX_BWD[kind](pre, p, full, mctx, hn, dh, big_g, small_g, bsz, seq, aux,
                             lambda v, q=q: sched.run(("b", q + 1), v))
        dh_f, dh_b, dg = rms_bwd(h_in, p[pre + "norm_mix"], dhn, dh[0], pre + "mix_norm_bwd")
        dh = (dh_f, dh_b)
        small_g[pre + "norm_mix"] = dg.reshape(d)
        reduce_later(q, [n for n in big if n.startswith(pre) and "_ffn_" not in n], big_g, "rs_l%d_mix" % layer)
        q += 2
    for tail in range(q, q + 4):
        sched.run(("b", tail), dh[0])
    assert not sched.actions, sched.actions.keys()

    small = [n for n in wnames if n not in big]
    packed = _pack([small_g[n] for n in small]).reshape(N_DEV, -1, LANES)
    settled = [results[n][1] for n in big if not n.startswith("l0_")]
    reduced = _reduce_scatter([packed], core, chip, "rs_small", after=settled)[0]
    gathered = all_gather([reduced], "ag_small")[0].reshape(-1)
    g_local = []
    for n, g in zip(small, _unpack(gathered, [small_g[n] for n in small])):
        if _base(n) in _SMALL_COL:
            width = p[n].shape[1]
            g = lax.dynamic_slice_in_dim(g, dev * width, width, axis=1)
        g_local.append(g)
    stacked = [_pack(v).reshape(-1, LANES) for v in
               ([p[n] for n in small], g_local, [p["m_" + n] for n in small], [p["v_" + n] for n in small])]
    upd = adam_flat(*stacked, "adam_small")
    upd = [_unpack(u.reshape(-1), g_local) for u in upd]
    for i, n in enumerate(small):
        results[n] = (g_local[i], upd[0][i], upd[1][i], upd[2][i])

    grad_x = dh[0].reshape(bsz, seq, d)
    cols = [[results[n][k] for n in wnames] for k in range(4)]
    return (loss, grad_x, *cols[0], *cols[1], *cols[2], *cols[3])


def kernel(x, l0_norm_mix, l0_cc_w_in, l0_cc_b_in, l0_cc_dw_w, l0_cc_dw_b, l0_cc_ln_g, l0_cc_ln_b, l0_cc_w_out, l0_cc_b_out, l0_norm_ffn, l0_ffn_w_up, l0_ffn_dw_w, l0_ffn_dw_b, l0_ffn_w_down, l1_norm_mix, l1_dsa_w_qkv, l1_dsa_w_o, l1_norm_ffn, l1_ffn_w_up, l1_ffn_dw_w, l1_ffn_dw_b, l1_ffn_w_down, l2_norm_mix, l2_mla_w_in, l2_mla_q_norm, l2_mla_w_qb, l2_mla_kv_norm, l2_mla_w_kvb, l2_mla_w_o, l2_norm_ffn, l2_ffn_w_up, l2_ffn_dw_w, l2_ffn_dw_b, l2_ffn_w_down, l3_norm_mix, l3_sg_w_in, l3_sg_b_in, l3_sg_ln_g, l3_sg_ln_b, l3_sg_w_s, l3_sg_b_s, l3_sg_w_out, l3_sg_b_out, l3_norm_ffn, l3_ffn_w_up, l3_ffn_dw_w, l3_ffn_dw_b, l3_ffn_w_down, final_norm, loss_target, m_l0_norm_mix, m_l0_cc_w_in, m_l0_cc_b_in, m_l0_cc_dw_w, m_l0_cc_dw_b, m_l0_cc_ln_g, m_l0_cc_ln_b, m_l0_cc_w_out, m_l0_cc_b_out, m_l0_norm_ffn, m_l0_ffn_w_up, m_l0_ffn_dw_w, m_l0_ffn_dw_b, m_l0_ffn_w_down, m_l1_norm_mix, m_l1_dsa_w_qkv, m_l1_dsa_w_o, m_l1_norm_ffn, m_l1_ffn_w_up, m_l1_ffn_dw_w, m_l1_ffn_dw_b, m_l1_ffn_w_down, m_l2_norm_mix, m_l2_mla_w_in, m_l2_mla_q_norm, m_l2_mla_w_qb, m_l2_mla_kv_norm, m_l2_mla_w_kvb, m_l2_mla_w_o, m_l2_norm_ffn, m_l2_ffn_w_up, m_l2_ffn_dw_w, m_l2_ffn_dw_b, m_l2_ffn_w_down, m_l3_norm_mix, m_l3_sg_w_in, m_l3_sg_b_in, m_l3_sg_ln_g, m_l3_sg_ln_b, m_l3_sg_w_s, m_l3_sg_b_s, m_l3_sg_w_out, m_l3_sg_b_out, m_l3_norm_ffn, m_l3_ffn_w_up, m_l3_ffn_dw_w, m_l3_ffn_dw_b, m_l3_ffn_w_down, m_final_norm, v_l0_norm_mix, v_l0_cc_w_in, v_l0_cc_b_in, v_l0_cc_dw_w, v_l0_cc_dw_b, v_l0_cc_ln_g, v_l0_cc_ln_b, v_l0_cc_w_out, v_l0_cc_b_out, v_l0_norm_ffn, v_l0_ffn_w_up, v_l0_ffn_dw_w, v_l0_ffn_dw_b, v_l0_ffn_w_down, v_l1_norm_mix, v_l1_dsa_w_qkv, v_l1_dsa_w_o, v_l1_norm_ffn, v_l1_ffn_w_up, v_l1_ffn_dw_w, v_l1_ffn_dw_b, v_l1_ffn_w_down, v_l2_norm_mix, v_l2_mla_w_in, v_l2_mla_q_norm, v_l2_mla_w_qb, v_l2_mla_kv_norm, v_l2_mla_w_kvb, v_l2_mla_w_o, v_l2_norm_ffn, v_l2_ffn_w_up, v_l2_ffn_dw_w, v_l2_ffn_dw_b, v_l2_ffn_w_down, v_l3_norm_mix, v_l3_sg_w_in, v_l3_sg_b_in, v_l3_sg_ln_g, v_l3_sg_ln_b, v_l3_sg_w_s, v_l3_sg_b_s, v_l3_sg_w_out, v_l3_sg_b_out, v_l3_norm_ffn, v_l3_ffn_w_up, v_l3_ffn_dw_w, v_l3_ffn_dw_b, v_l3_ffn_w_down, v_final_norm):
    return _train_step(dict(locals()))
```
